```python
import jax, jax.numpy as jnp
from jax import lax
import numpy as np

D_MODEL = 2048
BATCH = 8
SEQ = 4096
DEPTH = 1

N_HEADS_MLA = 8
Q_LORA_RANK = 512
KV_LORA_RANK = 512
QK_NOPE_DIM = 128
QK_ROPE_DIM = 64
QK_HEAD_DIM = QK_NOPE_DIM + QK_ROPE_DIM
V_HEAD_DIM = 128
MLA_WIDTH = N_HEADS_MLA * V_HEAD_DIM
N_HEADS_SB = 8
SB_HEAD_DIM = 128
SB_WIDTH = N_HEADS_SB * SB_HEAD_DIM
D_FF = -(-8 * D_MODEL // (3 * 256)) * 256
D_IN = Q_LORA_RANK + KV_LORA_RANK + QK_ROPE_DIM + 3 * SB_WIDTH + 2 * D_MODEL
Q_BLOCK = 128
ROPE_THETA = 10000.0
EPS = 1e-6

kernel_name = "hybrid_mla_stickbreaking_gated_block"


def _rms(x, g):
    xf = x.astype(jnp.float32)
    y = xf * lax.rsqrt(jnp.mean(xf * xf, axis=-1, keepdims=True) + EPS)
    return (y * g.astype(jnp.float32)).astype(x.dtype)


def _rope(x, pos):
    half = x.shape[-1] // 2
    freqs = ROPE_THETA ** (-jnp.arange(half, dtype=jnp.float32) / half)
    ang = pos.astype(jnp.float32)[..., None] * freqs
    cos = jnp.cos(ang)[:, :, None, :]
    sin = jnp.sin(ang)[:, :, None, :]
    xf = x.astype(jnp.float32)
    x1, x2 = xf[..., :half], xf[..., half:]
    return jnp.concatenate([x1 * cos - x2 * sin, x1 * sin + x2 * cos], axis=-1).astype(x.dtype)


def _mla_attention(q, k, v):
    S = q.shape[2]
    scale = QK_HEAD_DIM ** -0.5
    outs = []
    for i in range(S // Q_BLOCK):
        end = (i + 1) * Q_BLOCK
        qb = q[:, :, i * Q_BLOCK:end]
        s = jnp.einsum('bhqd,bhkd->bhqk', qb, k[:, :, :end]).astype(jnp.float32) * scale
        qi = i * Q_BLOCK + jnp.arange(Q_BLOCK)
        ki = jnp.arange(end)
        s = jnp.where(ki[None, :] <= qi[:, None], s, -jnp.inf)
        p = jax.nn.softmax(s, axis=-1).astype(v.dtype)
        outs.append(jnp.einsum('bhqk,bhkd->bhqd', p, v[:, :, :end]))
    return jnp.concatenate(outs, axis=2)


def _stick_breaking(q, k, v):
    S = q.shape[2]
    scale = SB_HEAD_DIM ** -0.5
    outs = []
    for i in range(S // Q_BLOCK):
        end = (i + 1) * Q_BLOCK
        qb = q[:, :, i * Q_BLOCK:end]
        z = jnp.einsum('bhqd,bhkd->bhqk', qb, k[:, :, :end]).astype(jnp.float32) * scale
        qi = i * Q_BLOCK + jnp.arange(Q_BLOCK)
        ki = jnp.arange(end)
        mask = ki[None, :] < qi[:, None]
        log_beta = jax.nn.log_sigmoid(z)
        log_one_minus = jnp.where(mask, jax.nn.log_sigmoid(-z), 0.0)
        tail = lax.cumsum(log_one_minus, axis=3, reverse=True) - log_one_minus
        a = jnp.where(mask, jnp.exp(log_beta + tail), 0.0).astype(v.dtype)
        outs.append(jnp.einsum('bhqk,bhkd->bhqd', a, v[:, :, :end]))
    return jnp.concatenate(outs, axis=2)


def _layer(x, c_act, pos, w_ada, b_ada, g_norm1, g_norm2, w_in, g_q_latent, g_kv_latent,
           w_uq, w_ukv, g_q_head, g_k_head, w_proj_mla, w_proj_sb, w_out, w_ffn_in, w_ffn_out):
    B, S, _ = x.shape
    ada = (c_act @ w_ada + b_ada)[:, None, :]
    sh1, sc1, gt1, sh2, sc2, gt2 = jnp.split(ada, 6, axis=-1)

    h = _rms(x, g_norm1) * (1 + sc1) + sh1
    proj = h @ w_in
    offs = np.cumsum([Q_LORA_RANK, KV_LORA_RANK, QK_ROPE_DIM, SB_WIDTH, SB_WIDTH, SB_WIDTH, D_MODEL])
    c_q, c_kv, k_pe, q_sb, k_sb, v_sb, gl_a, gl_b = jnp.split(proj, [int(o) for o in offs], axis=-1)

    q = (_rms(c_q, g_q_latent) @ w_uq).reshape(B, S, N_HEADS_MLA, QK_HEAD_DIM)
    kv = (_rms(c_kv, g_kv_latent) @ w_ukv).reshape(B, S, N_HEADS_MLA, QK_NOPE_DIM + V_HEAD_DIM)
    k_nope, v = kv[..., :QK_NOPE_DIM], kv[..., QK_NOPE_DIM:]
    k_pe_h = jnp.broadcast_to(k_pe[:, :, None, :], (B, S, N_HEADS_MLA, QK_ROPE_DIM))
    k = jnp.concatenate([k_nope, k_pe_h], axis=-1)
    q = _rms(q, g_q_head)
    k = _rms(k, g_k_head)
    q = jnp.concatenate([q[..., :QK_NOPE_DIM], _rope(q[..., QK_NOPE_DIM:], pos)], axis=-1)
    k = jnp.concatenate([k[..., :QK_NOPE_DIM], _rope(k[..., QK_NOPE_DIM:], pos)], axis=-1)
    y_a = _mla_attention(q.transpose(0, 2, 1, 3), k.transpose(0, 2, 1, 3), v.transpose(0, 2, 1, 3))
    y_a = y_a.transpose(0, 2, 1, 3).reshape(B, S, MLA_WIDTH)

    to_heads = lambda t: t.reshape(B, S, N_HEADS_SB, SB_HEAD_DIM).transpose(0, 2, 1, 3)
    y_b = _stick_breaking(to_heads(q_sb), to_heads(k_sb), to_heads(v_sb))
    y_b = y_b.transpose(0, 2, 1, 3).reshape(B, S, SB_WIDTH)

    merged = jax.nn.sigmoid(gl_a) * (y_a @ w_proj_mla) + jax.nn.sigmoid(gl_b) * (y_b @ w_proj_sb)
    x = x + gt1 * (merged @ w_out)

    h2 = _rms(x, g_norm2) * (1 + sc2) + sh2
    gate, up = jnp.split(h2 @ w_ffn_in, 2, axis=-1)
    x = x + gt2 * ((jax.nn.silu(gate) * up) @ w_ffn_out)
    return x


def _fwd_setup_inputs(seed: int = 0) -> dict:
    key = jax.random.key(seed)
    ks = jax.random.split(key, 24)
    f32 = jnp.float32

    def nrm(k, shape, fan_in):
        return jax.random.normal(k, shape, f32) * (fan_in ** -0.5)

    def gain(k, n):
        return 1.0 + 0.02 * jax.random.normal(k, (DEPTH, n), f32)

    x = jax.random.normal(ks[0], (BATCH, SEQ, D_MODEL), f32)
    c = jax.random.normal(ks[1], (BATCH, D_MODEL), f32)
    offset = jax.random.randint(ks[2], (BATCH, 1), 0, 1024, dtype=jnp.int32)
    positions = offset + jnp.arange(SEQ, dtype=jnp.int32)[None, :]
    return {
        "x": x,
        "c": c,
        "positions": positions,
        "w_ada": nrm(ks[3], (DEPTH, D_MODEL, 6 * D_MODEL), D_MODEL),
        "b_ada": 0.02 * jax.random.normal(ks[4], (DEPTH, 6 * D_MODEL), f32),
        "g_norm1": gain(ks[5], D_MODEL),
        "g_norm2": gain(ks[6], D_MODEL),
        "w_in": nrm(ks[7], (DEPTH, D_MODEL, D_IN), D_MODEL),
        "g_q_latent": gain(ks[8], Q_LORA_RANK),
        "g_kv_latent": gain(ks[9], KV_LORA_RANK),
        "w_uq": nrm(ks[10], (DEPTH, Q_LORA_RANK, N_HEADS_MLA * QK_HEAD_DIM), Q_LORA_RANK),
        "w_ukv": nrm(ks[11], (DEPTH, KV_LORA_RANK, N_HEADS_MLA * (QK_NOPE_DIM + V_HEAD_DIM)), KV_LORA_RANK),
        "g_q_head": gain(ks[12], QK_HEAD_DIM),
        "g_k_head": gain(ks[13], QK_HEAD_DIM),
        "w_proj_mla": nrm(ks[14], (DEPTH, MLA_WIDTH, D_MODEL), MLA_WIDTH),
        "w_proj_sb": nrm(ks[15], (DEPTH, SB_WIDTH, D_MODEL), SB_WIDTH),
        "w_out": nrm(ks[16], (DEPTH, D_MODEL, D_MODEL), D_MODEL),
        "w_ffn_in": nrm(ks[17], (DEPTH, D_MODEL, 2 * D_FF), D_MODEL),
        "w_ffn_out": nrm(ks[18], (DEPTH, D_FF, D_MODEL), D_FF),
    }


def _fwd_reference(x, c, positions, w_ada, b_ada, g_norm1, g_norm2, w_in, g_q_latent, g_kv_latent,
              w_uq, w_ukv, g_q_head, g_k_head, w_proj_mla, w_proj_sb, w_out, w_ffn_in, w_ffn_out):
    c_act = jax.nn.silu(c)
    for l in range(DEPTH):
        x = _layer(x, c_act, positions, w_ada[l], b_ada[l], g_norm1[l], g_norm2[l], w_in[l],
                   g_q_latent[l], g_kv_latent[l], w_uq[l], w_ukv[l], g_q_head[l], g_k_head[l],
                   w_proj_mla[l], w_proj_sb[l], w_out[l], w_ffn_in[l], w_ffn_out[l])
    return x


import jax as _jax
import jax.numpy as _jnp

TWIN_FORMAT = 'train_step'
FWD_PARAMS = ['x', 'c', 'positions', 'w_ada', 'b_ada', 'g_norm1', 'g_norm2', 'w_in', 'g_q_latent', 'g_kv_latent', 'w_uq', 'w_ukv', 'g_q_head', 'g_k_head', 'w_proj_mla', 'w_proj_sb', 'w_out', 'w_ffn_in', 'w_ffn_out']
TWIN_WEIGHTS = ['w_ada', 'b_ada', 'g_norm1', 'g_norm2', 'w_in', 'g_q_latent', 'g_kv_latent', 'w_uq', 'w_ukv', 'g_q_head', 'g_k_head', 'w_proj_mla', 'w_proj_sb', 'w_out', 'w_ffn_in', 'w_ffn_out']
TWIN_DIFF_INPUT = 'x'
TWIN_INPUTS = ['x', 'c', 'positions', 'w_ada', 'b_ada', 'g_norm1', 'g_norm2', 'w_in', 'g_q_latent', 'g_kv_latent', 'w_uq', 'w_ukv', 'g_q_head', 'g_k_head', 'w_proj_mla', 'w_proj_sb', 'w_out', 'w_ffn_in', 'w_ffn_out', 'loss_target', 'm_w_ada', 'm_b_ada', 'm_g_norm1', 'm_g_norm2', 'm_w_in', 'm_g_q_latent', 'm_g_kv_latent', 'm_w_uq', 'm_w_ukv', 'm_g_q_head', 'm_g_k_head', 'm_w_proj_mla', 'm_w_proj_sb', 'm_w_out', 'm_w_ffn_in', 'm_w_ffn_out', 'v_w_ada', 'v_b_ada', 'v_g_norm1', 'v_g_norm2', 'v_w_in', 'v_g_q_latent', 'v_g_kv_latent', 'v_w_uq', 'v_w_ukv', 'v_g_q_head', 'v_g_k_head', 'v_w_proj_mla', 'v_w_proj_sb', 'v_w_out', 'v_w_ffn_in', 'v_w_ffn_out']
TWIN_OUTPUTS = ['loss', 'grad_x', 'grad_w_ada', 'grad_b_ada', 'grad_g_norm1', 'grad_g_norm2', 'grad_w_in', 'grad_g_q_latent', 'grad_g_kv_latent', 'grad_w_uq', 'grad_w_ukv', 'grad_g_q_head', 'grad_g_k_head', 'grad_w_proj_mla', 'grad_w_proj_sb', 'grad_w_out', 'grad_w_ffn_in', 'grad_w_ffn_out', 'delta_w_ada', 'delta_b_ada', 'delta_g_norm1', 'delta_g_norm2', 'delta_w_in', 'delta_g_q_latent', 'delta_g_kv_latent', 'delta_w_uq', 'delta_w_ukv', 'delta_g_q_head', 'delta_g_k_head', 'delta_w_proj_mla', 'delta_w_proj_sb', 'delta_w_out', 'delta_w_ffn_in', 'delta_w_ffn_out', 'new_m_w_ada', 'new_m_b_ada', 'new_m_g_norm1', 'new_m_g_norm2', 'new_m_w_in', 'new_m_g_q_latent', 'new_m_g_kv_latent', 'new_m_w_uq', 'new_m_w_ukv', 'new_m_g_q_head', 'new_m_g_k_head', 'new_m_w_proj_mla', 'new_m_w_proj_sb', 'new_m_w_out', 'new_m_w_ffn_in', 'new_m_w_ffn_out', 'new_v_w_ada', 'new_v_b_ada', 'new_v_g_norm1', 'new_v_g_norm2', 'new_v_w_in', 'new_v_g_q_latent', 'new_v_g_kv_latent', 'new_v_w_uq', 'new_v_w_ukv', 'new_v_g_q_head', 'new_v_g_k_head', 'new_v_w_proj_mla', 'new_v_w_proj_sb', 'new_v_w_out', 'new_v_w_ffn_in', 'new_v_w_ffn_out']
TWIN_LEAF_KINDS = {'loss': 'loss', 'grad_x': 'grad_x', 'grad_w_ada': 'grad_w', 'grad_b_ada': 'grad_w', 'grad_g_norm1': 'grad_w', 'grad_g_norm2': 'grad_w', 'grad_w_in': 'grad_w', 'grad_g_q_latent': 'grad_w', 'grad_g_kv_latent': 'grad_w', 'grad_w_uq': 'grad_w', 'grad_w_ukv': 'grad_w', 'grad_g_q_head': 'grad_w', 'grad_g_k_head': 'grad_w', 'grad_w_proj_mla': 'grad_w', 'grad_w_proj_sb': 'grad_w', 'grad_w_out': 'grad_w', 'grad_w_ffn_in': 'grad_w', 'grad_w_ffn_out': 'grad_w', 'delta_w_ada': 'delta_w', 'delta_b_ada': 'delta_w', 'delta_g_norm1': 'delta_w', 'delta_g_norm2': 'delta_w', 'delta_w_in': 'delta_w', 'delta_g_q_latent': 'delta_w', 'delta_g_kv_latent': 'delta_w', 'delta_w_uq': 'delta_w', 'delta_w_ukv': 'delta_w', 'delta_g_q_head': 'delta_w', 'delta_g_k_head': 'delta_w', 'delta_w_proj_mla': 'delta_w', 'delta_w_proj_sb': 'delta_w', 'delta_w_out': 'delta_w', 'delta_w_ffn_in': 'delta_w', 'delta_w_ffn_out': 'delta_w', 'new_m_w_ada': 'new_m', 'new_m_b_ada': 'new_m', 'new_m_g_norm1': 'new_m', 'new_m_g_norm2': 'new_m', 'new_m_w_in': 'new_m', 'new_m_g_q_latent': 'new_m', 'new_m_g_kv_latent': 'new_m', 'new_m_w_uq': 'new_m', 'new_m_w_ukv': 'new_m', 'new_m_g_q_head': 'new_m', 'new_m_g_k_head': 'new_m', 'new_m_w_proj_mla': 'new_m', 'new_m_w_proj_sb': 'new_m', 'new_m_w_out': 'new_m', 'new_m_w_ffn_in': 'new_m', 'new_m_w_ffn_out': 'new_m', 'new_v_w_ada': 'new_v', 'new_v_b_ada': 'new_v', 'new_v_g_norm1': 'new_v', 'new_v_g_norm2': 'new_v', 'new_v_w_in': 'new_v', 'new_v_g_q_latent': 'new_v', 'new_v_g_kv_latent': 'new_v', 'new_v_w_uq': 'new_v', 'new_v_w_ukv': 'new_v', 'new_v_g_q_head': 'new_v', 'new_v_g_k_head': 'new_v', 'new_v_w_proj_mla': 'new_v', 'new_v_w_proj_sb': 'new_v', 'new_v_w_out': 'new_v', 'new_v_w_ffn_in': 'new_v', 'new_v_w_ffn_out': 'new_v'}


def _forward(args):
    return _fwd_reference(*[args[k] for k in FWD_PARAMS])


def _output_shape():
    def fwd():
        inp = _fwd_setup_inputs(0)
        return _fwd_reference(*[inp[k] for k in FWD_PARAMS])
    out = _jax.eval_shape(fwd)
    return out.shape, out.dtype

N_MICROBATCH = 1
ADAM_LR = 0.001
ADAM_B1 = 0.9
ADAM_B2 = 0.999
ADAM_EPS = 1e-08
ADAM_WD = 0.01
ADAM_STEP = 10
PER_EXAMPLE_BATCH_AXIS = {'x': 0, 'c': 0, 'positions': 0, 'loss_target': 0}
SHARED_INPUTS = []
_WEIGHT_DTYPES = {'w_ada': _jnp.float32, 'b_ada': _jnp.float32, 'g_norm1': _jnp.float32, 'g_norm2': _jnp.float32, 'w_in': _jnp.float32, 'g_q_latent': _jnp.float32, 'g_kv_latent': _jnp.float32, 'w_uq': _jnp.float32, 'w_ukv': _jnp.float32, 'g_q_head': _jnp.float32, 'g_k_head': _jnp.float32, 'w_proj_mla': _jnp.float32, 'w_proj_sb': _jnp.float32, 'w_out': _jnp.float32, 'w_ffn_in': _jnp.float32, 'w_ffn_out': _jnp.float32}
MOMENT_SCALE = {'w_ada': 1.826848e+00, 'b_ada': 4.812662e+00, 'g_norm1': 1.548094e+00, 'g_norm2': 1.291537e+01, 'w_in': 5.968249e-01, 'g_q_latent': 3.293415e-02, 'g_kv_latent': 2.341585e+00, 'w_uq': 1.803132e-02, 'w_ukv': 6.382865e-01, 'g_q_head': 9.299203e-02, 'g_k_head': 9.462631e-02, 'w_proj_mla': 6.090649e-01, 'w_proj_sb': 7.760147e-01, 'w_out': 9.236784e-01, 'w_ffn_in': 4.761113e-01, 'w_ffn_out': 3.941187e-01}


def _to_microbatches(a, axis):
    t = _jnp.moveaxis(a, axis, 0)
    t = t.reshape((N_MICROBATCH, t.shape[0] // N_MICROBATCH) + t.shape[1:])
    return _jnp.moveaxis(t, 1, axis + 1)


def setup_inputs(seed: int = 0) -> dict:
    inp = _fwd_setup_inputs(seed)
    key = _jax.random.fold_in(_jax.random.key(seed), 7919)
    shape, _ = _output_shape()
    out = dict(inp)
    out["loss_target"] = _jax.random.normal(_jax.random.fold_in(key, 0), shape, _jnp.float32)
    for i, name in enumerate(TWIN_WEIGHTS):
        w = inp[name].astype(_jnp.float32)
        if MOMENT_SCALE is None:
            s = _jnp.sqrt(_jnp.mean(_jnp.square(w)) + 1e-30)
        else:
            s = MOMENT_SCALE[name]
        km, kv = _jax.random.split(_jax.random.fold_in(key, i + 1))
        out[name] = w
        out["m_" + name] = s * _jax.random.normal(km, w.shape, _jnp.float32)
        out["v_" + name] = (s * s) * _jax.random.uniform(kv, w.shape, _jnp.float32, 0.5, 1.5)
    if N_MICROBATCH > 1:
        for name, axis in PER_EXAMPLE_BATCH_AXIS.items():
            out[name] = _to_microbatches(out[name], axis)
    return {'x': out['x'], 'c': out['c'], 'positions': out['positions'], 'w_ada': out['w_ada'], 'b_ada': out['b_ada'], 'g_norm1': out['g_norm1'], 'g_norm2': out['g_norm2'], 'w_in': out['w_in'], 'g_q_latent': out['g_q_latent'], 'g_kv_latent': out['g_kv_latent'], 'w_uq': out['w_uq'], 'w_ukv': out['w_ukv'], 'g_q_head': out['g_q_head'], 'g_k_head': out['g_k_head'], 'w_proj_mla': out['w_proj_mla'], 'w_proj_sb': out['w_proj_sb'], 'w_out': out['w_out'], 'w_ffn_in': out['w_ffn_in'], 'w_ffn_out': out['w_ffn_out'], 'loss_target': out['loss_target'], 'm_w_ada': out['m_w_ada'], 'm_b_ada': out['m_b_ada'], 'm_g_norm1': out['m_g_norm1'], 'm_g_norm2': out['m_g_norm2'], 'm_w_in': out['m_w_in'], 'm_g_q_latent': out['m_g_q_latent'], 'm_g_kv_latent': out['m_g_kv_latent'], 'm_w_uq': out['m_w_uq'], 'm_w_ukv': out['m_w_ukv'], 'm_g_q_head': out['m_g_q_head'], 'm_g_k_head': out['m_g_k_head'], 'm_w_proj_mla': out['m_w_proj_mla'], 'm_w_proj_sb': out['m_w_proj_sb'], 'm_w_out': out['m_w_out'], 'm_w_ffn_in': out['m_w_ffn_in'], 'm_w_ffn_out': out['m_w_ffn_out'], 'v_w_ada': out['v_w_ada'], 'v_b_ada': out['v_b_ada'], 'v_g_norm1': out['v_g_norm1'], 'v_g_norm2': out['v_g_norm2'], 'v_w_in': out['v_w_in'], 'v_g_q_latent': out['v_g_q_latent'], 'v_g_kv_latent': out['v_g_kv_latent'], 'v_w_uq': out['v_w_uq'], 'v_w_ukv': out['v_w_ukv'], 'v_g_q_head': out['v_g_q_head'], 'v_g_k_head': out['v_g_k_head'], 'v_w_proj_mla': out['v_w_proj_mla'], 'v_w_proj_sb': out['v_w_proj_sb'], 'v_w_out': out['v_w_out'], 'v_w_ffn_in': out['v_w_ffn_in'], 'v_w_ffn_out': out['v_w_ffn_out']}


def _loss(weights, diff, rest, loss_target):
    with _jax.named_scope("forward"):
        args = {**rest, TWIN_DIFF_INPUT: diff, **{k: w.astype(_WEIGHT_DTYPES[k]) for k, w in weights.items()}}
        y = _forward(args)
    with _jax.named_scope("loss_head"):
        err = _jnp.square(y.astype(_jnp.float32) - loss_target)
        return 0.5 * _jnp.sum(_jnp.mean(err, axis=-1)) if err.ndim else 0.5 * err


def _adamw(w, g, m, v):
    m = ADAM_B1 * m + (1.0 - ADAM_B1) * g
    v = ADAM_B2 * v + (1.0 - ADAM_B2) * _jnp.square(g)
    m_hat = m / (1.0 - ADAM_B1 ** ADAM_STEP)
    v_hat = v / (1.0 - ADAM_B2 ** ADAM_STEP)
    delta = -ADAM_LR * (m_hat / (_jnp.sqrt(v_hat) + ADAM_EPS) + ADAM_WD * w)
    return delta, m, v


def reference(x, c, positions, w_ada, b_ada, g_norm1, g_norm2, w_in, g_q_latent, g_kv_latent, w_uq, w_ukv, g_q_head, g_k_head, w_proj_mla, w_proj_sb, w_out, w_ffn_in, w_ffn_out, loss_target, m_w_ada, m_b_ada, m_g_norm1, m_g_norm2, m_w_in, m_g_q_latent, m_g_kv_latent, m_w_uq, m_w_ukv, m_g_q_head, m_g_k_head, m_w_proj_mla, m_w_proj_sb, m_w_out, m_w_ffn_in, m_w_ffn_out, v_w_ada, v_b_ada, v_g_norm1, v_g_norm2, v_w_in, v_g_q_latent, v_g_kv_latent, v_w_uq, v_w_ukv, v_g_q_head, v_g_k_head, v_w_proj_mla, v_w_proj_sb, v_w_out, v_w_ffn_in, v_w_ffn_out):
    given = dict(x=x, c=c, positions=positions, w_ada=w_ada, b_ada=b_ada, g_norm1=g_norm1, g_norm2=g_norm2, w_in=w_in, g_q_latent=g_q_latent, g_kv_latent=g_kv_latent, w_uq=w_uq, w_ukv=w_ukv, g_q_head=g_q_head, g_k_head=g_k_head, w_proj_mla=w_proj_mla, w_proj_sb=w_proj_sb, w_out=w_out, w_ffn_in=w_ffn_in, w_ffn_out=w_ffn_out, loss_target=loss_target, m_w_ada=m_w_ada, m_b_ada=m_b_ada, m_g_norm1=m_g_norm1, m_g_norm2=m_g_norm2, m_w_in=m_w_in, m_g_q_latent=m_g_q_latent, m_g_kv_latent=m_g_kv_latent, m_w_uq=m_w_uq, m_w_ukv=m_w_ukv, m_g_q_head=m_g_q_head, m_g_k_head=m_g_k_head, m_w_proj_mla=m_w_proj_mla, m_w_proj_sb=m_w_proj_sb, m_w_out=m_w_out, m_w_ffn_in=m_w_ffn_in, m_w_ffn_out=m_w_ffn_out, v_w_ada=v_w_ada, v_b_ada=v_b_ada, v_g_norm1=v_g_norm1, v_g_norm2=v_g_norm2, v_w_in=v_w_in, v_g_q_latent=v_g_q_latent, v_g_kv_latent=v_g_kv_latent, v_w_uq=v_w_uq, v_w_ukv=v_w_ukv, v_g_q_head=v_g_q_head, v_g_k_head=v_g_k_head, v_w_proj_mla=v_w_proj_mla, v_w_proj_sb=v_w_proj_sb, v_w_out=v_w_out, v_w_ffn_in=v_w_ffn_in, v_w_ffn_out=v_w_ffn_out)
    weights = {n: given[n] for n in TWIN_WEIGHTS}
    shared = {n: given[n] for n in SHARED_INPUTS}
    per_example = {n: given[n] for n in ['x', 'c', 'positions']}
    grad_fn = _jax.value_and_grad(_loss, argnums=(0, 1))

    def one_microbatch(ex, loss_target):
        ex = dict(ex)
        diff = ex.pop(TWIN_DIFF_INPUT)
        return grad_fn(weights, diff, {**shared, **ex}, loss_target)

    if N_MICROBATCH == 1:
        loss, (grad_w, grad_x) = one_microbatch(per_example, given["loss_target"])
    else:
        def body(carry, xs):
            loss_sum, grad_sum = carry
            l_k, (gw_k, gx_k) = one_microbatch(xs[0], xs[1])
            with _jax.named_scope("update"):
                return (loss_sum + l_k, _jax.tree.map(_jnp.add, grad_sum, gw_k)), gx_k

        init = (_jnp.zeros((), _jnp.float32), _jax.tree.map(_jnp.zeros_like, weights))
        (loss, grad_w), grad_x = _jax.lax.scan(body, init, (per_example, given["loss_target"]))
    with _jax.named_scope("update"):
        delta_w, new_m, new_v = {}, {}, {}
        for n in TWIN_WEIGHTS:
            delta_w[n], new_m[n], new_v[n] = _adamw(weights[n], grad_w[n], given["m_" + n], given["v_" + n])
    return (loss, grad_x, *[grad_w[n] for n in TWIN_WEIGHTS], *[delta_w[n] for n in TWIN_WEIGHTS],
            *[new_m[n] for n in TWIN_WEIGHTS], *[new_v[n] for n in TWIN_WEIGHTS])
```

```python
import functools
import math

import jax
import jax.numpy as jnp
from jax import lax
from jax.experimental import pallas as pl
from jax.experimental.pallas import tpu as pltpu

F32 = jnp.float32
BF16 = jnp.bfloat16
MESH = pl.DeviceIdType.MESH

EPS = 1e-6
ROPE_THETA = 10000.0
NOPE = 128
ROPE = 64
QK_DIM = NOPE + ROPE
HEAD_PAD = 256
HEAD = 128
N_DEV = 8
LANE = 128
VMEM_LIMIT = 48 * 1024 * 1024

ADAM_LR = 0.001
ADAM_B1 = 0.9
ADAM_B2 = 0.999
ADAM_EPS = 1e-08
ADAM_WD = 0.01
ADAM_STEP = 10


def _tile(n, target):
    if n <= target:
        return n
    t = (target // LANE) * LANE
    while t >= LANE:
        if n % t == 0:
            return t
        t -= LANE
    return n


def _params(sem):
    return pltpu.CompilerParams(dimension_semantics=sem, vmem_limit_bytes=VMEM_LIMIT)


def _rows(tm, w, col=0):
    return pl.BlockSpec((tm, w), lambda i: (i, col))


def _vec(w, col=0, rows=1):
    return pl.BlockSpec((rows, w), lambda i: (0, col))


def _mm(a, b, *, name, ta=False, tb=False, out_dtype=F32, a_fn=None, bias=None, tm=512, tn=1024, tk=1024):
    M = a.shape[1] if ta else a.shape[0]
    K = a.shape[0] if ta else a.shape[1]
    N = b.shape[0] if tb else b.shape[1]
    assert K == (b.shape[1] if tb else b.shape[0]), (a.shape, b.shape, ta, tb)
    tm, tn, tk = _tile(M, tm), _tile(N, tn), _tile(K, tk)
    nk = K // tk
    dn = (((0 if ta else 1,), (1 if tb else 0,)), ((), ()))

    def body(*refs):
        if bias is None:
            a_ref, b_ref, o_ref, acc_ref = refs
        else:
            a_ref, b_ref, bias_ref, o_ref, acc_ref = refs
        k = pl.program_id(2)

        @pl.when(k == 0)
        def _():
            acc_ref[...] = jnp.zeros_like(acc_ref)

        av = a_ref[...]
        if a_fn is not None:
            av = a_fn(av.astype(F32))
        acc_ref[...] += lax.dot_general(av.astype(BF16), b_ref[...].astype(BF16), dn, preferred_element_type=F32)

        @pl.when(k == nk - 1)
        def _():
            r = acc_ref[...]
            if bias is not None:
                r = r + bias_ref[...]
            o_ref[...] = r.astype(o_ref.dtype)

    a_spec = pl.BlockSpec((tk, tm), lambda i, j, k: (k, i)) if ta else pl.BlockSpec((tm, tk), lambda i, j, k: (i, k))
    b_spec = pl.BlockSpec((tn, tk), lambda i, j, k: (j, k)) if tb else pl.BlockSpec((tk, tn), lambda i, j, k: (k, j))
    in_specs = [a_spec, b_spec]
    args = [a, b]
    if bias is not None:
        in_specs.append(pl.BlockSpec((1, tn), lambda i, j, k: (0, j)))
        args.append(bias)
    return pl.pallas_call(
        body, name=name, grid=(M // tm, N // tn, nk), in_specs=in_specs,
        out_specs=pl.BlockSpec((tm, tn), lambda i, j, k: (i, j)),
        out_shape=jax.ShapeDtypeStruct((M, N), out_dtype),
        scratch_shapes=[pltpu.VMEM((tm, tn), F32)],
        compiler_params=_params(("parallel", "parallel", "arbitrary")),
    )(*args)


def _rms_rows(v):
    return lax.rsqrt(jnp.mean(v * v, axis=-1, keepdims=True) + EPS)


def _rmsmod(x, g, ada, sc_col, sh_col, *, name):
    S, D = x.shape
    tm = _tile(S, 256)

    def body(x_ref, g_ref, sc_ref, sh_ref, h_ref):
        xv = x_ref[...]
        h = (xv * _rms_rows(xv) * g_ref[...]) * (1.0 + sc_ref[...]) + sh_ref[...]
        h_ref[...] = h.astype(h_ref.dtype)

    return pl.pallas_call(
        body, name=name, grid=(S // tm,),
        in_specs=[_rows(tm, D), _vec(D), _vec(D, sc_col), _vec(D, sh_col)],
        out_specs=_rows(tm, D), out_shape=jax.ShapeDtypeStruct((S, D), BF16),
        compiler_params=_params(("parallel",)),
    )(x, g, ada, ada)


def _latent_norm(proj, g_q, g_kv, ql):
    S = proj.shape[0]
    tm = _tile(S, 512)

    def body(cq_ref, ckv_ref, gq_ref, gkv_ref, oq_ref, okv_ref):
        cq = cq_ref[...]
        oq_ref[...] = (cq * _rms_rows(cq) * gq_ref[...]).astype(BF16)
        ckv = ckv_ref[...]
        okv_ref[...] = (ckv * _rms_rows(ckv) * gkv_ref[...]).astype(BF16)

    return pl.pallas_call(
        body, name="latent_norm", grid=(S // tm,),
        in_specs=[_rows(tm, ql, 0), _rows(tm, ql, 1), _vec(ql), _vec(ql)],
        out_specs=[_rows(tm, ql), _rows(tm, ql)],
        out_shape=[jax.ShapeDtypeStruct((S, ql), BF16)] * 2,
        compiler_params=_params(("parallel",)),
    )(proj, proj, g_q, g_kv)


def _rope_fwd(y, c, s1, s2):
    return y * c + pltpu.roll(y, ROPE // 2, 1) * s1 + pltpu.roll(y, HEAD_PAD - ROPE // 2, 1) * s2


def _rope_bwd(d, c, s1, s2):
    return d * c + pltpu.roll(d * s1, HEAD_PAD - ROPE // 2, 1) + pltpu.roll(d * s2, ROPE // 2, 1)


def _head_rms(v):
    return lax.rsqrt(jnp.sum(v * v, axis=-1, keepdims=True) * (1.0 / QK_DIM) + EPS)


def _q_prep(q0, g_qh, tabs, nh):
    S = q0.shape[0]
    tm = _tile(S, 256)

    def body(q_ref, g_ref, c_ref, s1_ref, s2_ref, o_ref):
        c, s1, s2, g = c_ref[...], s1_ref[...], s2_ref[...], g_ref[...]
        for h in range(nh):
            sl = slice(h * HEAD_PAD, (h + 1) * HEAD_PAD)
            xs = q_ref[:, sl]
            o_ref[:, sl] = _rope_fwd(xs * _head_rms(xs) * g, c, s1, s2).astype(BF16)

    w = nh * HEAD_PAD
    return pl.pallas_call(
        body, name="mla_q_prep", grid=(S // tm,),
        in_specs=[_rows(tm, w), _vec(HEAD_PAD)] + [_rows(tm, HEAD_PAD)] * 3,
        out_specs=_rows(tm, w), out_shape=jax.ShapeDtypeStruct((S, w), BF16),
        compiler_params=_params(("parallel",)),
    )(q0, g_qh, *tabs)


def _k_prep(kv0, proj, kpe_col, g_kh, tabs, nh):
    S = kv0.shape[0]
    tm = _tile(S, 256)

    def body(kv_ref, kpe_ref, g_ref, c_ref, s1_ref, s2_ref, o_ref):
        c, s1, s2, g = c_ref[...], s1_ref[...], s2_ref[...], g_ref[...]
        kpe = kpe_ref[...]
        for h in range(nh):
            k0 = jnp.concatenate([kv_ref[:, h * HEAD:(h + 1) * HEAD], kpe], axis=1)
            o_ref[:, h * HEAD_PAD:(h + 1) * HEAD_PAD] = _rope_fwd(k0 * _head_rms(k0) * g, c, s1, s2).astype(BF16)

    return pl.pallas_call(
        body, name="mla_k_prep", grid=(S // tm,),
        in_specs=[_rows(tm, nh * HEAD, 0), _rows(tm, LANE, kpe_col), _vec(HEAD_PAD)] + [_rows(tm, HEAD_PAD)] * 3,
        out_specs=_rows(tm, nh * HEAD_PAD), out_shape=jax.ShapeDtypeStruct((S, nh * HEAD_PAD), BF16),
        compiler_params=_params(("parallel",)),
    )(kv0, proj, g_kh, *tabs)


def _gate_merge(pa, pb, proj, gla_col, glb_col):
    S, D = pa.shape
    tm = _tile(S, 256)

    def body(pa_ref, pb_ref, ga_ref, gb_ref, o_ref):
        o_ref[...] = (jax.nn.sigmoid(ga_ref[...]) * pa_ref[...] + jax.nn.sigmoid(gb_ref[...]) * pb_ref[...]).astype(BF16)

    return pl.pallas_call(
        body, name="gate_merge", grid=(S // tm,),
        in_specs=[_rows(tm, D), _rows(tm, D), _rows(tm, D, gla_col), _rows(tm, D, glb_col)],
        out_specs=_rows(tm, D), out_shape=jax.ShapeDtypeStruct((S, D), BF16),
        compiler_params=_params(("parallel",)),
    )(pa, pb, proj, proj)


def _resid_rmsmod(x, o, g, ada, gt_col, sc_col, sh_col):
    S, D = x.shape
    tm = _tile(S, 256)

    def body(x_ref, o_ref, g_ref, gt_ref, sc_ref, sh_ref, x2_ref, h_ref):
        x2 = x_ref[...] + gt_ref[...] * o_ref[...]
        x2_ref[...] = x2
        h_ref[...] = ((x2 * _rms_rows(x2) * g_ref[...]) * (1.0 + sc_ref[...]) + sh_ref[...]).astype(BF16)

    return pl.pallas_call(
        body, name="resid_rmsmod2", grid=(S // tm,),
        in_specs=[_rows(tm, D), _rows(tm, D), _vec(D), _vec(D, gt_col), _vec(D, sc_col), _vec(D, sh_col)],
        out_specs=[_rows(tm, D), _rows(tm, D)],
        out_shape=[jax.ShapeDtypeStruct((S, D), F32), jax.ShapeDtypeStruct((S, D), BF16)],
        compiler_params=_params(("parallel",)),
    )(x, o, g, ada, ada, ada)


def _swiglu(ff, dff_half):
    S = ff.shape[0]
    tm = _tile(S, 256)

    def body(g_ref, u_ref, o_ref):
        o_ref[...] = (jax.nn.silu(g_ref[...].astype(F32)) * u_ref[...].astype(F32)).astype(BF16)

    return pl.pallas_call(
        body, name="swiglu", grid=(S // tm,),
        in_specs=[_rows(tm, dff_half, 0), _rows(tm, dff_half, 1)],
        out_specs=_rows(tm, dff_half), out_shape=jax.ShapeDtypeStruct((S, dff_half), BF16),
        compiler_params=_params(("parallel",)),
    )(ff, ff)


def _loss_head(x2, f, tgt, ada, gt_col):
    S, D = x2.shape
    tm = _tile(S, 256)

    def body(x2_ref, f_ref, t_ref, gt_ref, dy_ref, df_ref, red_ref, loss_ref):
        @pl.when(pl.program_id(0) == 0)
        def _():
            red_ref[...] = jnp.zeros_like(red_ref)
            loss_ref[...] = jnp.zeros_like(loss_ref)

        fv = f_ref[...]
        gt = gt_ref[...]
        err = x2_ref[...] + gt * fv - t_ref[...]
        dy = err * (1.0 / D)
        dy_ref[...] = dy
        df_ref[...] = (dy * gt).astype(BF16)
        red_ref[0:1, :] += jnp.sum(dy * fv, axis=0, keepdims=True)
        loss_ref[...] += (0.5 / D) * jnp.sum(err * err)

    return pl.pallas_call(
        body, name="loss_head", grid=(S // tm,),
        in_specs=[_rows(tm, D), _rows(tm, D), _rows(tm, D), _vec(D, gt_col)],
        out_specs=[_rows(tm, D), _rows(tm, D), _vec(D, rows=8), _vec(LANE, rows=8)],
        out_shape=[jax.ShapeDtypeStruct((S, D), F32), jax.ShapeDtypeStruct((S, D), BF16),
                   jax.ShapeDtypeStruct((8, D), F32), jax.ShapeDtypeStruct((8, LANE), F32)],
        compiler_params=_params(("arbitrary",)),
    )(x2, f, tgt, ada)


def _swiglu_bwd(dact, ff, dff_half):
    S = ff.shape[0]
    tm = _tile(S, 128)

    def body(d_ref, g_ref, u_ref, o_ref):
        d = d_ref[...]
        g = g_ref[...].astype(F32)
        u = u_ref[...].astype(F32)
        sg = jax.nn.sigmoid(g)
        o_ref[:, :dff_half] = (d * u * sg * (1.0 + g * (1.0 - sg))).astype(BF16)
        o_ref[:, dff_half:] = (d * g * sg).astype(BF16)

    return pl.pallas_call(
        body, name="swiglu_bwd", grid=(S // tm,),
        in_specs=[_rows(tm, dff_half), _rows(tm, dff_half, 0), _rows(tm, dff_half, 1)],
        out_specs=_rows(tm, 2 * dff_half), out_shape=jax.ShapeDtypeStruct((S, 2 * dff_half), BF16),
        compiler_params=_params(("parallel",)),
    )(dact, ff, ff)


def _rmsmod2_bwd(dh2, x2, dy, o, g, ada, sc_col, gt_col):
    S, D = x2.shape
    tm = _tile(S, 256)

    def body(dh_ref, x2_ref, dy_ref, o_ref, g_ref, sc_ref, gt_ref, dx_ref, do_ref, red_ref):
        @pl.when(pl.program_id(0) == 0)
        def _():
            red_ref[...] = jnp.zeros_like(red_ref)

        dh = dh_ref[...]
        x2 = x2_ref[...]
        gv = g_ref[...]
        mod = 1.0 + sc_ref[...]
        r = _rms_rows(x2)
        xn = x2 * r
        t = dh * xn
        red_ref[0:1, :] += jnp.sum(dh, axis=0, keepdims=True)
        red_ref[1:2, :] += jnp.sum(t * gv, axis=0, keepdims=True)
        red_ref[2:3, :] += jnp.sum(t * mod, axis=0, keepdims=True)
        dxn = dh * gv * mod
        dx = dy_ref[...] + r * (dxn - xn * jnp.mean(dxn * xn, axis=-1, keepdims=True))
        dx_ref[...] = dx
        red_ref[3:4, :] += jnp.sum(dx * o_ref[...], axis=0, keepdims=True)
        do_ref[...] = (dx * gt_ref[...]).astype(BF16)

    return pl.pallas_call(
        body, name="rmsmod2_bwd", grid=(S // tm,),
        in_specs=[_rows(tm, D)] * 4 + [_vec(D), _vec(D, sc_col), _vec(D, gt_col)],
        out_specs=[_rows(tm, D), _rows(tm, D), _vec(D, rows=8)],
        out_shape=[jax.ShapeDtypeStruct((S, D), F32), jax.ShapeDtypeStruct((S, D), BF16),
                   jax.ShapeDtypeStruct((8, D), F32)],
        compiler_params=_params(("arbitrary",)),
    )(dh2, x2, dy, o, g, ada, ada)


def _rmsmod1_bwd(dh, x, dx2, g, ada, sc_col):
    S, D = x.shape
    tm = _tile(S, 256)

    def body(dh_ref, x_ref, dx2_ref, g_ref, sc_ref, gx_ref, red_ref):
        @pl.when(pl.program_id(0) == 0)
        def _():
            red_ref[...] = jnp.zeros_like(red_ref)

        dh = dh_ref[...]
        xv = x_ref[...]
        gv = g_ref[...]
        mod = 1.0 + sc_ref[...]
        r = _rms_rows(xv)
        xn = xv * r
        t = dh * xn
        red_ref[0:1, :] += jnp.sum(dh, axis=0, keepdims=True)
        red_ref[1:2, :] += jnp.sum(t * gv, axis=0, keepdims=True)
        red_ref[2:3, :] += jnp.sum(t * mod, axis=0, keepdims=True)
        dxn = dh * gv * mod
        gx_ref[...] = dx2_ref[...] + r * (dxn - xn * jnp.mean(dxn * xn, axis=-1, keepdims=True))

    return pl.pallas_call(
        body, name="rmsmod1_bwd", grid=(S // tm,),
        in_specs=[_rows(tm, D)] * 3 + [_vec(D), _vec(D, sc_col)],
        out_specs=[_rows(tm, D), _vec(D, rows=8)],
        out_shape=[jax.ShapeDtypeStruct((S, D), F32), jax.ShapeDtypeStruct((8, D), F32)],
        compiler_params=_params(("arbitrary",)),
    )(dh, x, dx2, g, ada)


def _gate_bwd(dm, pa, pb, proj, gla_col, glb_col):
    S, D = pa.shape
    tm = _tile(S, 256)

    def body(dm_ref, pa_ref, pb_ref, la_ref, lb_ref, dpa_ref, dpb_ref, dla_ref, dlb_ref):
        dm_ = dm_ref[...]
        ga = jax.nn.sigmoid(la_ref[...])
        gb = jax.nn.sigmoid(lb_ref[...])
        dpa_ref[...] = (dm_ * ga).astype(BF16)
        dpb_ref[...] = (dm_ * gb).astype(BF16)
        dla_ref[...] = (dm_ * pa_ref[...] * ga * (1.0 - ga)).astype(BF16)
        dlb_ref[...] = (dm_ * pb_ref[...] * gb * (1.0 - gb)).astype(BF16)

    return pl.pallas_call(
        body, name="gate_bwd", grid=(S // tm,),
        in_specs=[_rows(tm, D)] * 3 + [_rows(tm, D, gla_col), _rows(tm, D, glb_col)],
        out_specs=[_rows(tm, D)] * 4, out_shape=[jax.ShapeDtypeStruct((S, D), BF16)] * 4,
        compiler_params=_params(("parallel",)),
    )(dm, pa, pb, proj, proj)


def _q_prep_bwd(dq, q0, g_qh, tabs, nh):
    S = q0.shape[0]
    tm = _tile(S, 256)

    def body(dq_ref, q_ref, g_ref, c_ref, s1_ref, s2_ref, o_ref, red_ref):
        @pl.when(pl.program_id(0) == 0)
        def _():
            red_ref[...] = jnp.zeros_like(red_ref)

        c, s1, s2, g = c_ref[...], s1_ref[...], s2_ref[...], g_ref[...]
        dg = jnp.zeros((1, HEAD_PAD), F32)
        for h in range(nh):
            sl = slice(h * HEAD_PAD, (h + 1) * HEAD_PAD)
            d1 = _rope_bwd(dq_ref[:, sl], c, s1, s2)
            xs = q_ref[:, sl]
            r = _head_rms(xs)
            qn = xs * r
            dg = dg + jnp.sum(d1 * qn, axis=0, keepdims=True)
            dn = d1 * g
            o_ref[:, sl] = (r * (dn - qn * (jnp.sum(dn * qn, axis=-1, keepdims=True) * (1.0 / QK_DIM)))).astype(BF16)
        red_ref[0:1, :] += dg

    w = nh * HEAD_PAD
    return pl.pallas_call(
        body, name="mla_q_prep_bwd", grid=(S // tm,),
        in_specs=[_rows(tm, w), _rows(tm, w), _vec(HEAD_PAD)] + [_rows(tm, HEAD_PAD)] * 3,
        out_specs=[_rows(tm, w), _vec(HEAD_PAD, rows=8)],
        out_shape=[jax.ShapeDtypeStruct((S, w), BF16), jax.ShapeDtypeStruct((8, HEAD_PAD), F32)],
        compiler_params=_params(("arbitrary",)),
    )(dq, q0, g_qh, *tabs)


def _k_prep_bwd(dk, dv, kv0, proj, kpe_col, g_kh, tabs, nh):
    S = kv0.shape[0]
    tm = _tile(S, 256)
    wv = nh * HEAD

    def body(dk_ref, dv_ref, kv_ref, kpe_ref, g_ref, c_ref, s1_ref, s2_ref, o_ref, dpe_ref, red_ref):
        @pl.when(pl.program_id(0) == 0)
        def _():
            red_ref[...] = jnp.zeros_like(red_ref)

        c, s1, s2, g = c_ref[...], s1_ref[...], s2_ref[...], g_ref[...]
        kpe = kpe_ref[...]
        dg = jnp.zeros((1, HEAD_PAD), F32)
        dpe = jnp.zeros((tm, LANE), F32)
        for h in range(nh):
            d1 = _rope_bwd(dk_ref[:, h * HEAD_PAD:(h + 1) * HEAD_PAD], c, s1, s2)
            k0 = jnp.concatenate([kv_ref[:, h * HEAD:(h + 1) * HEAD], kpe], axis=1)
            r = _head_rms(k0)
            kn = k0 * r
            dg = dg + jnp.sum(d1 * kn, axis=0, keepdims=True)
            dn = d1 * g
            dk0 = r * (dn - kn * (jnp.sum(dn * kn, axis=-1, keepdims=True) * (1.0 / QK_DIM)))
            o_ref[:, h * HEAD:(h + 1) * HEAD] = dk0[:, :HEAD].astype(BF16)
            dpe = dpe + dk0[:, HEAD:]
        o_ref[:, wv:] = dv_ref[...].astype(BF16)
        dpe_ref[...] = dpe.astype(BF16)
        red_ref[0:1, :] += dg

    return pl.pallas_call(
        body, name="mla_k_prep_bwd", grid=(S // tm,),
        in_specs=[_rows(tm, nh * HEAD_PAD), _rows(tm, wv), _rows(tm, wv, 0), _rows(tm, LANE, kpe_col),
                  _vec(HEAD_PAD)] + [_rows(tm, HEAD_PAD)] * 3,
        out_specs=[_rows(tm, 2 * wv), _rows(tm, LANE), _vec(HEAD_PAD, rows=8)],
        out_shape=[jax.ShapeDtypeStruct((S, 2 * wv), BF16), jax.ShapeDtypeStruct((S, LANE), BF16),
                   jax.ShapeDtypeStruct((8, HEAD_PAD), F32)],
        compiler_params=_params(("arbitrary",)),
    )(dk, dv, kv0, proj, g_kh, *tabs)


def _latent_norm_bwd(dcqn, dckvn, proj, g_q, g_kv, ql):
    S = proj.shape[0]
    tm = _tile(S, 512)

    def body(dq_ref, dkv_ref, cq_ref, ckv_ref, gq_ref, gkv_ref, oq_ref, okv_ref, red_ref):
        @pl.when(pl.program_id(0) == 0)
        def _():
            red_ref[...] = jnp.zeros_like(red_ref)

        for row, (d_ref, c_ref, g_ref, o_ref) in enumerate(((dq_ref, cq_ref, gq_ref, oq_ref),
                                                            (dkv_ref, ckv_ref, gkv_ref, okv_ref))):
            d = d_ref[...]
            cv = c_ref[...]
            r = _rms_rows(cv)
            ch = cv * r
            red_ref[row:row + 1, :] += jnp.sum(d * ch, axis=0, keepdims=True)
            dn = d * g_ref[...]
            o_ref[...] = (r * (dn - ch * jnp.mean(dn * ch, axis=-1, keepdims=True))).astype(BF16)

    return pl.pallas_call(
        body, name="latent_norm_bwd", grid=(S // tm,),
        in_specs=[_rows(tm, ql), _rows(tm, ql), _rows(tm, ql, 0), _rows(tm, ql, 1), _vec(ql), _vec(ql)],
        out_specs=[_rows(tm, ql), _rows(tm, ql), _vec(ql, rows=8)],
        out_shape=[jax.ShapeDtypeStruct((S, ql), BF16)] * 2 + [jax.ShapeDtypeStruct((8, ql), F32)],
        compiler_params=_params(("arbitrary",)),
    )(dcqn, dckvn, proj, proj, g_q, g_kv)


NEG = -1e30
_NT = (((1,), (1,)), ((), ()))
_TN = (((0,), (0,)), ((), ()))


def _dot(a, b, dn=(((1,), (0,)), ((), ()))):
    return lax.dot_general(a, b, dn, preferred_element_type=F32)


def _key_rows(kb, tk):
    return pl.ds(pl.multiple_of(kb * tk, tk), tk)


def _causal_ids(i, kb, tq, tk):
    qi = i * tq + lax.broadcasted_iota(jnp.int32, (tq, tk), 0)
    ki = kb * tk + lax.broadcasted_iota(jnp.int32, (tq, tk), 1)
    return qi, ki


def _mla_fwd(q, k, kv0, nh):
    S = q.shape[0]
    tq = _tile(S, 256)
    tk = _tile(S, 256)
    scale = QK_DIM ** -0.5

    def body(q_ref, k_ref, v_ref, o_ref, lse_ref):
        i = pl.program_id(1)
        qv = q_ref[...]

        def step(kb, carry):
            m, l, acc = carry
            rows = _key_rows(kb, tk)
            s = _dot(qv, k_ref[rows, :], _NT) * scale
            qi, ki = _causal_ids(i, kb, tq, tk)
            s = jnp.where(ki <= qi, s, NEG)
            m_new = jnp.maximum(m, jnp.max(s, axis=-1, keepdims=True))
            alpha = jnp.exp(m - m_new)
            p = jnp.exp(s - m_new)
            l = alpha * l + jnp.sum(p, axis=-1, keepdims=True)
            acc = alpha * acc + _dot(p.astype(BF16), v_ref[rows, :].astype(BF16))
            return m_new, l, acc

        init = (jnp.full((tq, 1), NEG, F32), jnp.zeros((tq, 1), F32), jnp.zeros((tq, HEAD), F32))
        m, l, acc = lax.fori_loop(0, (i + 1) * (tq // tk), step, init)
        o_ref[...] = acc / l
        lse_ref[...] = m + jnp.log(l)

    return pl.pallas_call(
        body, name="mla_attn_fwd", grid=(nh, S // tq),
        in_specs=[pl.BlockSpec((tq, HEAD_PAD), lambda h, i: (i, h)),
                  pl.BlockSpec((S, HEAD_PAD), lambda h, i: (0, h)),
                  pl.BlockSpec((S, HEAD), lambda h, i: (0, nh + h))],
        out_specs=[pl.BlockSpec((tq, HEAD), lambda h, i: (i, h)),
                   pl.BlockSpec((None, tq, 1), lambda h, i: (h, i, 0))],
        out_shape=[jax.ShapeDtypeStruct((S, nh * HEAD), F32), jax.ShapeDtypeStruct((nh, S, 1), F32)],
        compiler_params=_params(("parallel", "arbitrary")),
    )(q, k, kv0)


def _mla_bwd(q, k, kv0, o, do, lse, nh):
    S = q.shape[0]
    tq = _tile(S, 256)
    tk = _tile(S, 256)
    scale = QK_DIM ** -0.5

    def body(q_ref, k_ref, v_ref, o_ref, do_ref, lse_ref, dq_ref, dk_ref, dv_ref):
        i = pl.program_id(1)

        @pl.when(i == 0)
        def _():
            dk_ref[...] = jnp.zeros_like(dk_ref)
            dv_ref[...] = jnp.zeros_like(dv_ref)

        qv = q_ref[...]
        dov = do_ref[...]
        delta = jnp.sum(dov * o_ref[...], axis=-1, keepdims=True)
        dob = dov.astype(BF16)
        lse = lse_ref[...]

        def step(kb, dq):
            rows = _key_rows(kb, tk)
            ks = k_ref[rows, :]
            vs = v_ref[rows, :].astype(BF16)
            s = _dot(qv, ks, _NT) * scale
            qi, ki = _causal_ids(i, kb, tq, tk)
            p = jnp.where(ki <= qi, jnp.exp(s - lse), 0.0)
            dp = _dot(dob, vs, _NT)
            ds = (p * (dp - delta) * scale).astype(BF16)
            dk_ref[rows, :] += _dot(ds, qv, _TN)
            dv_ref[rows, :] += _dot(p.astype(BF16), dob, _TN)
            return dq + _dot(ds, ks)

        dq_ref[...] = lax.fori_loop(0, (i + 1) * (tq // tk), step, jnp.zeros((tq, HEAD_PAD), F32))

    return pl.pallas_call(
        body, name="mla_attn_bwd", grid=(nh, S // tq),
        in_specs=[pl.BlockSpec((tq, HEAD_PAD), lambda h, i: (i, h)),
                  pl.BlockSpec((S, HEAD_PAD), lambda h, i: (0, h)),
                  pl.BlockSpec((S, HEAD), lambda h, i: (0, nh + h)),
                  pl.BlockSpec((tq, HEAD), lambda h, i: (i, h)),
                  pl.BlockSpec((tq, HEAD), lambda h, i: (i, h)),
                  pl.BlockSpec((None, tq, 1), lambda h, i: (h, i, 0))],
        out_specs=[pl.BlockSpec((tq, HEAD_PAD), lambda h, i: (i, h)),
                   pl.BlockSpec((S, HEAD_PAD), lambda h, i: (0, h)),
                   pl.BlockSpec((S, HEAD), lambda h, i: (0, h))],
        out_shape=[jax.ShapeDtypeStruct((S, nh * HEAD_PAD), F32), jax.ShapeDtypeStruct((S, nh * HEAD_PAD), F32),
                   jax.ShapeDtypeStruct((S, nh * HEAD), F32)],
        compiler_params=_params(("parallel", "arbitrary")),
    )(q, k, kv0, o, do, lse)


def _split_dot(v, tri):
    hi = v.astype(BF16)
    lo = (v - hi.astype(F32)).astype(BF16)
    return _dot(hi, tri) + _dot(lo, tri)


def _tri(tk, cmp):
    r = lax.broadcasted_iota(jnp.int32, (tk, tk), 0)
    c = lax.broadcasted_iota(jnp.int32, (tk, tk), 1)
    return jnp.where(cmp(r, c), 1.0, 0.0).astype(BF16)


def _sb_scores(qv, ks, i, kb, tq, tk, scale):
    z = _dot(qv, ks, _NT) * scale
    qi, ki = _causal_ids(i, kb, tq, tk)
    mask = ki < qi
    lb = jnp.minimum(z, 0.0) - jnp.log(1.0 + jnp.exp(-jnp.abs(z)))
    lom = jnp.where(mask, lb - z, 0.0)
    return mask, lb, lom


def _sb_fwd(proj, q_col, k_col, v_col, nh):
    S = proj.shape[0]
    tq = _tile(S, 256)
    tk = _tile(S, 128)
    scale = HEAD ** -0.5

    def body(q_ref, k_ref, v_ref, o_ref):
        i = pl.program_id(1)
        qv = q_ref[...].astype(BF16)
        upper = _tri(tk, lambda j, s: j > s)
        nkb = (i + 1) * (tq // tk)

        def step(j, carry):
            run, acc = carry
            kb = nkb - 1 - j
            rows = _key_rows(kb, tk)
            mask, lb, lom = _sb_scores(qv, k_ref[rows, :].astype(BF16), i, kb, tq, tk, scale)
            a = jnp.where(mask, jnp.exp(lb + _split_dot(lom, upper) + run), 0.0)
            acc = acc + _dot(a.astype(BF16), v_ref[rows, :].astype(BF16))
            return run + jnp.sum(lom, axis=-1, keepdims=True), acc

        _, acc = lax.fori_loop(0, nkb, step, (jnp.zeros((tq, 1), F32), jnp.zeros((tq, HEAD), F32)))
        o_ref[...] = acc

    return pl.pallas_call(
        body, name="sb_attn_fwd", grid=(nh, S // tq),
        in_specs=[pl.BlockSpec((tq, HEAD), lambda h, i: (i, q_col + h)),
                  pl.BlockSpec((S, HEAD), lambda h, i: (0, k_col + h)),
                  pl.BlockSpec((S, HEAD), lambda h, i: (0, v_col + h))],
        out_specs=pl.BlockSpec((tq, HEAD), lambda h, i: (i, h)),
        out_shape=jax.ShapeDtypeStruct((S, nh * HEAD), F32),
        compiler_params=_params(("parallel", "arbitrary")),
    )(proj, proj, proj)


def _sb_bwd(proj, q_col, k_col, v_col, dy, nh):
    S = proj.shape[0]
    tq = _tile(S, 256)
    tk = _tile(S, 128)
    scale = HEAD ** -0.5

    def body(q_ref, k_ref, v_ref, dy_ref, dq_ref, dk_ref, dv_ref, run_ref):
        i = pl.program_id(1)

        @pl.when(i == 0)
        def _():
            dk_ref[...] = jnp.zeros_like(dk_ref)
            dv_ref[...] = jnp.zeros_like(dv_ref)

        qv = q_ref[...].astype(BF16)
        dyb = dy_ref[...].astype(BF16)
        upper = _tri(tk, lambda j, s: j > s)
        before = _tri(tk, lambda s, j: s < j)
        nkb = (i + 1) * (tq // tk)

        def suffix(j, run):
            kb = nkb - 1 - j
            _, _, lom = _sb_scores(qv, k_ref[_key_rows(kb, tk), :].astype(BF16), i, kb, tq, tk, scale)
            run_ref[kb] = jnp.broadcast_to(run, (tq, LANE))
            return run + jnp.sum(lom, axis=-1, keepdims=True)

        lax.fori_loop(0, nkb, suffix, jnp.zeros((tq, 1), F32))

        def step(kb, carry):
            prefix, dq = carry
            rows = _key_rows(kb, tk)
            ks = k_ref[rows, :].astype(BF16)
            vs = v_ref[rows, :].astype(BF16)
            mask, lb, lom = _sb_scores(qv, ks, i, kb, tq, tk, scale)
            a = jnp.where(mask, jnp.exp(lb + _split_dot(lom, upper) + run_ref[kb][:, 0:1]), 0.0)
            dl = a * _dot(dyb, vs, _NT)
            left = _dot(dl.astype(BF16), before) + prefix
            beta = jnp.exp(lb)
            dz = (jnp.where(mask, dl * (1.0 - beta) - beta * left, 0.0) * scale).astype(BF16)
            dk_ref[rows, :] += _dot(dz, qv, _TN)
            dv_ref[rows, :] += _dot(a.astype(BF16), dyb, _TN)
            return prefix + jnp.sum(dl, axis=-1, keepdims=True), dq + _dot(dz, ks)

        init = (jnp.zeros((tq, 1), F32), jnp.zeros((tq, HEAD), F32))
        dq_ref[...] = lax.fori_loop(0, nkb, step, init)[1]

    full = pl.BlockSpec((S, HEAD), lambda h, i: (0, h))
    tile = pl.BlockSpec((tq, HEAD), lambda h, i: (i, h))
    return pl.pallas_call(
        body, name="sb_attn_bwd", grid=(nh, S // tq),
        in_specs=[pl.BlockSpec((tq, HEAD), lambda h, i: (i, q_col + h)),
                  pl.BlockSpec((S, HEAD), lambda h, i: (0, k_col + h)),
                  pl.BlockSpec((S, HEAD), lambda h, i: (0, v_col + h)), tile],
        out_specs=[tile, full, full],
        out_shape=[jax.ShapeDtypeStruct((S, nh * HEAD), F32)] * 3,
        scratch_shapes=[pltpu.VMEM((S // tk, tq, LANE), F32)],
        compiler_params=_params(("parallel", "arbitrary")),
    )(proj, proj, proj, dy)


def _place():
    return lax.axis_index("x"), lax.axis_index("y"), lax.axis_index("c")


def _other_chips(x, y):
    return [(1 - x, y), (x, 1 - y), (1 - x, 1 - y)]


def _dev_index(p):
    return 4 * p[0] + 2 * p[1] + p[2]


def _gather_blocks(blocks, *, name, in_vmem):
    n = len(blocks)
    per = 7

    def body(*refs):
        ins, outs = refs[:n], refs[n:2 * n]
        send_sems, recv_sems, local_sems = refs[2 * n:]
        x, y, c = _place()
        me, sibling = (x, y, c), (x, y, 1 - c)
        chips = _other_chips(x, y)

        def slot(a, p):
            return outs[a].at[_dev_index(p)]

        def copy(a, k, block, to, src=None):
            return pltpu.make_async_remote_copy(
                src_ref=slot(a, block) if src is None else src, dst_ref=slot(a, block),
                send_sem=send_sems.at[a * per + k], recv_sem=recv_sems.at[a * per + k],
                device_id=to, device_id_type=MESH)

        mine = [pltpu.make_async_copy(ins[a], slot(a, me), local_sems.at[a]) for a in range(n)]
        for cp in mine:
            cp.start()
        first = []
        for a in range(n):
            first.append(copy(a, 0, me, sibling, src=ins[a]))
            first += [copy(a, 1 + j, me, (*chip, c), src=ins[a]) for j, chip in enumerate(chips)]
        for cp in first:
            cp.start()
        passed = []
        for a in range(n):
            for j, chip in enumerate(chips):
                copy(a, 1 + j, (*chip, c), me).wait_recv()
                cp = copy(a, 4 + j, (*chip, c), sibling)
                cp.start()
                passed.append(cp)
        for a in range(n):
            copy(a, 0, sibling, me).wait_recv()
            for j, chip in enumerate(chips):
                copy(a, 4 + j, (*chip, 1 - c), me).wait_recv()
        for cp in first + passed:
            cp.wait_send()
        for cp in mine:
            cp.wait()

    space = pltpu.VMEM if in_vmem else pl.ANY
    spec = pl.BlockSpec(memory_space=space)
    outs = pl.pallas_call(
        body, name=name, in_specs=[spec] * n, out_specs=[spec] * n,
        out_shape=[jax.ShapeDtypeStruct((N_DEV,) + b.shape, b.dtype) for b in blocks],
        scratch_shapes=[pltpu.SemaphoreType.DMA((n * per,)), pltpu.SemaphoreType.DMA((n * per,)),
                        pltpu.SemaphoreType.DMA((n,))],
        compiler_params=pltpu.CompilerParams(vmem_limit_bytes=VMEM_LIMIT),
    )(*blocks)
    return list(outs)


def _sibling_swap(arrs, *, name):
    n = len(arrs)

    def body(*refs):
        ins, outs = refs[:n], refs[n:2 * n]
        send_sems, recv_sems = refs[2 * n:]
        x, y, c = _place()
        copies = [pltpu.make_async_remote_copy(
            src_ref=ins[a].at[1 - c], dst_ref=outs[a], send_sem=send_sems.at[a], recv_sem=recv_sems.at[a],
            device_id=(x, y, 1 - c), device_id_type=MESH) for a in range(n)]
        for cp in copies:
            cp.start()
        for cp in copies:
            cp.wait()

    spec = pl.BlockSpec(memory_space=pl.ANY)
    return list(pl.pallas_call(
        body, name=name, in_specs=[spec] * n, out_specs=[spec] * n,
        out_shape=[jax.ShapeDtypeStruct(a.shape[1:], a.dtype) for a in arrs],
        scratch_shapes=[pltpu.SemaphoreType.DMA((n,)), pltpu.SemaphoreType.DMA((n,))],
    )(*arrs))


def _chip_exchange(arrs, *, name):
    n = len(arrs)

    def body(*refs):
        ins, outs = refs[:n], refs[n:2 * n]
        send_sems, recv_sems = refs[2 * n:]
        x, y, c = _place()
        copies = []
        for a in range(n):
            for j, (px, py) in enumerate(_other_chips(x, y)):
                copies.append(pltpu.make_async_remote_copy(
                    src_ref=ins[a].at[2 * px + py], dst_ref=outs[a].at[j],
                    send_sem=send_sems.at[3 * a + j], recv_sem=recv_sems.at[3 * a + j],
                    device_id=(px, py, c), device_id_type=MESH))
        for cp in copies:
            cp.start()
        for cp in copies:
            cp.wait()

    spec = pl.BlockSpec(memory_space=pl.ANY)
    return list(pl.pallas_call(
        body, name=name, in_specs=[spec] * n, out_specs=[spec] * n,
        out_shape=[jax.ShapeDtypeStruct((3,) + a.shape[1:], a.dtype) for a in arrs],
        scratch_shapes=[pltpu.SemaphoreType.DMA((3 * n,)), pltpu.SemaphoreType.DMA((3 * n,))],
    )(*arrs))


def _sibling_join(halves, *, name):
    n = len(halves)

    def body(*refs):
        ins, outs = refs[:n], refs[n:2 * n]
        send_sems, recv_sems, local_sems = refs[2 * n:]
        x, y, c = _place()
        mine = [pltpu.make_async_copy(ins[a], outs[a].at[c], local_sems.at[a]) for a in range(n)]
        copies = [pltpu.make_async_remote_copy(
            src_ref=ins[a], dst_ref=outs[a].at[c], send_sem=send_sems.at[a], recv_sem=recv_sems.at[a],
            device_id=(x, y, 1 - c), device_id_type=MESH) for a in range(n)]
        for cp in mine + copies:
            cp.start()
        for a in range(n):
            pltpu.make_async_remote_copy(
                src_ref=ins[a], dst_ref=outs[a].at[1 - c], send_sem=send_sems.at[a], recv_sem=recv_sems.at[a],
                device_id=(x, y, 1 - c), device_id_type=MESH).wait()
        for cp in mine:
            cp.wait()

    spec = pl.BlockSpec(memory_space=pl.ANY)
    return list(pl.pallas_call(
        body, name=name, in_specs=[spec] * n, out_specs=[spec] * n,
        out_shape=[jax.ShapeDtypeStruct((2,) + a.shape, a.dtype) for a in halves],
        scratch_shapes=[pltpu.SemaphoreType.DMA((n,)), pltpu.SemaphoreType.DMA((n,)), pltpu.SemaphoreType.DMA((n,))],
    )(*halves))


def _flat2(a, lead):
    return a.reshape(a.shape[:lead] + (-1, a.shape[-1]))


def _pair_sum(g, recv, c_idx, *, name):
    _, nchip, r, w = g.shape
    tm = _tile(r, 256) if r % 8 == 0 else r

    def body(c_ref, g_ref, r_ref, o_ref):
        o_ref[...] = (g_ref[...].astype(F32) + r_ref[...].astype(F32)).astype(o_ref.dtype)

    return pl.pallas_call(
        body, name=name,
        grid_spec=pltpu.PrefetchScalarGridSpec(
            num_scalar_prefetch=1, grid=(nchip, r // tm),
            in_specs=[pl.BlockSpec((None, None, tm, w), lambda k, i, c_ref: (c_ref[0], k, i, 0)),
                      pl.BlockSpec((None, tm, w), lambda k, i, c_ref: (k, i, 0))],
            out_specs=pl.BlockSpec((None, tm, w), lambda k, i, c_ref: (k, i, 0))),
        out_shape=jax.ShapeDtypeStruct((nchip, r, w), BF16),
        compiler_params=_params(("parallel", "parallel")),
    )(c_idx, g, recv)


def _chip_sum(s1, recv, chip_idx, *, name):
    _, r, w = s1.shape
    tm = _tile(r, 256) if r % 8 == 0 else r

    def body(k_ref, s_ref, r_ref, o_ref):
        acc = s_ref[...].astype(F32)
        for j in range(3):
            acc = acc + r_ref[j].astype(F32)
        o_ref[...] = acc

    return pl.pallas_call(
        body, name=name,
        grid_spec=pltpu.PrefetchScalarGridSpec(
            num_scalar_prefetch=1, grid=(r // tm,),
            in_specs=[pl.BlockSpec((None, tm, w), lambda i, k_ref: (k_ref[0], i, 0)),
                      pl.BlockSpec((3, tm, w), lambda i, k_ref: (0, i, 0))],
            out_specs=pl.BlockSpec((tm, w), lambda i, k_ref: (i, 0))),
        out_shape=jax.ShapeDtypeStruct((r, w), F32),
        compiler_params=_params(("parallel",)),
    )(chip_idx, s1, recv)


def _adam_math(w, g, m, v):
    m = ADAM_B1 * m + (1.0 - ADAM_B1) * g
    v = ADAM_B2 * v + (1.0 - ADAM_B2) * (g * g)
    m_hat = m / (1.0 - ADAM_B1 ** ADAM_STEP)
    v_hat = v / (1.0 - ADAM_B2 ** ADAM_STEP)
    delta = -ADAM_LR * (m_hat / (jnp.sqrt(v_hat) + ADAM_EPS) + ADAM_WD * w)
    return delta, m, v


def _adamw(w, g, m, v, *, name):
    r, cw = w.shape
    tm = _tile(r, 256) if r % 8 == 0 else r

    def body(w_ref, g_ref, m_ref, v_ref, d_ref, nm_ref, nv_ref):
        d_ref[...], nm_ref[...], nv_ref[...] = _adam_math(w_ref[...], g_ref[...], m_ref[...], v_ref[...])

    return pl.pallas_call(
        body, name=name, grid=(r // tm,), in_specs=[_rows(tm, cw)] * 4, out_specs=[_rows(tm, cw)] * 3,
        out_shape=[jax.ShapeDtypeStruct((r, cw), F32)] * 3, compiler_params=_params(("parallel",)),
    )(w, g, m, v)


def _adamw_ada(cact_t, dada, w, m, v):
    r, cw = w.shape
    nb = cact_t.shape[1]
    tm = _tile(r, 256)
    tn = _tile(cw, 1024)

    def body(a_ref, d_ref, w_ref, m_ref, v_ref, g_ref, dl_ref, nm_ref, nv_ref):
        a = a_ref[...]
        d = d_ref[...]
        g = a[:, 0:1] * d[0:1, :]
        for b in range(1, nb):
            g = g + a[:, b:b + 1] * d[b:b + 1, :]
        g_ref[...] = g
        dl_ref[...], nm_ref[...], nv_ref[...] = _adam_math(w_ref[...], g, m_ref[...], v_ref[...])

    blk = pl.BlockSpec((tm, tn), lambda i, j: (i, j))
    return pl.pallas_call(
        body, name="adamw_ada", grid=(r // tm, cw // tn),
        in_specs=[pl.BlockSpec((tm, nb), lambda i, j: (i, 0)), pl.BlockSpec((nb, tn), lambda i, j: (0, j)), blk, blk, blk],
        out_specs=[blk] * 4, out_shape=[jax.ShapeDtypeStruct((r, cw), F32)] * 4,
        compiler_params=_params(("parallel", "parallel")),
    )(cact_t, dada, w, m, v)


def _adamw_vec(parts, w, m, v):
    n = w.shape[1]

    def body(p_ref, w_ref, m_ref, v_ref, g_ref, d_ref, nm_ref, nv_ref):
        p = p_ref[...]
        g = p[0:1, :]
        for b in range(1, N_DEV):
            g = g + p[b:b + 1, :]
        g_ref[...] = g
        d_ref[...], nm_ref[...], nv_ref[...] = _adam_math(w_ref[...], g, m_ref[...], v_ref[...])

    return pl.pallas_call(
        body, name="adamw_vec", out_shape=[jax.ShapeDtypeStruct((1, n), F32)] * 4,
        compiler_params=pltpu.CompilerParams(vmem_limit_bytes=VMEM_LIMIT),
    )(parts, w, m, v)


def _cols_from_chips(g8, rows):
    cs = g8.shape[-1]
    return g8.reshape(4, rows, cs).transpose(1, 0, 2).reshape(rows, 4 * cs)


def _cols_to_pieces(g):
    rows, c4 = g.shape
    return g.reshape(2, rows // 2, 4, c4 // 4).transpose(0, 2, 1, 3)


def _rows_to_pieces(g):
    r4, cols = g.shape
    return g.reshape(4, 2, r4 // 8, cols).transpose(1, 0, 2, 3)


def _pad_cols(a, w):
    return jnp.pad(a, ((0, 0), (0, w - a.shape[1])))


def kernel(x, c, positions, w_ada, b_ada, g_norm1, g_norm2, w_in, g_q_latent, g_kv_latent, w_uq, w_ukv, g_q_head, g_k_head, w_proj_mla, w_proj_sb, w_out, w_ffn_in, w_ffn_out, loss_target, m_w_ada, m_b_ada, m_g_norm1, m_g_norm2, m_w_in, m_g_q_latent, m_g_kv_latent, m_w_uq, m_w_ukv, m_g_q_head, m_g_k_head, m_w_proj_mla, m_w_proj_sb, m_w_out, m_w_ffn_in, m_w_ffn_out, v_w_ada, v_b_ada, v_g_norm1, v_g_norm2, v_w_in, v_g_q_latent, v_g_kv_latent, v_w_uq, v_w_ukv, v_g_q_head, v_g_k_head, v_w_proj_mla, v_w_proj_sb, v_w_out, v_w_ffn_in, v_w_ffn_out):
    xi, yi, ci = _place()
    chip = 2 * xi + yi
    dev = 2 * chip + ci
    c_idx = jnp.reshape(ci, (1,)).astype(jnp.int32)
    chip_idx = jnp.reshape(chip, (1,)).astype(jnp.int32)

    x = x[0]
    tgt = loss_target[0]
    S, D = x.shape
    ql = g_q_latent.shape[1]
    assert g_kv_latent.shape[1] == ql
    mlaw = w_proj_mla.shape[1]
    nh = mlaw // HEAD
    sbw = w_proj_sb.shape[1]
    assert sbw == mlaw
    dff = w_ffn_out.shape[1] * 4
    d_in = 2 * ql + ROPE + 3 * sbw + 2 * D
    d_in_p = d_in + ROPE
    q_col = (2 * ql) // HEAD
    k_col = q_col + nh
    v_col = k_col + nh
    gla_col = (2 * ql + 3 * sbw) // D
    glb_col = gla_col + 1
    kpe_col = (d_in - ROPE) // LANE
    assert (2 * ql + 3 * sbw) % D == 0 and (d_in - ROPE) % LANE == 0

    mats = {"w_in": w_in[0], "w_uq": w_uq[0], "w_ukv": w_ukv[0], "w_proj_mla": w_proj_mla[0],
            "w_proj_sb": w_proj_sb[0], "w_out": w_out[0], "w_ffn_in": w_ffn_in[0], "w_ffn_out": w_ffn_out[0]}
    names = list(mats)
    row_sharded = {"w_out", "w_ffn_out"}

    halves = []
    for nm in names:
        w = mats[nm]
        hr = w.shape[0] // 2
        halves.append(lax.dynamic_slice_in_dim(w, ci * hr, hr, axis=0).astype(BF16))
    gathered = dict(zip(names, _gather_blocks(halves, name="gather_weights", in_vmem=False)))

    def full_cols(nm):
        return _cols_from_chips(gathered[nm], mats[nm].shape[0])

    w_in_f = full_cols("w_in")
    kpe0 = 2 * ql
    w_in_p = jnp.concatenate([w_in_f[:, :kpe0], w_in_f[:, kpe0 + ROPE:], w_in_f[:, kpe0:kpe0 + ROPE],
                              jnp.zeros((D, ROPE), BF16)], axis=1)
    w_uq_p = jnp.pad(full_cols("w_uq").reshape(ql, nh, QK_DIM), ((0, 0), (0, 0), (0, HEAD_PAD - QK_DIM))
                     ).reshape(ql, nh * HEAD_PAD)
    w_ukv4 = full_cols("w_ukv").reshape(ql, nh, 2 * HEAD)
    w_ukv_p = jnp.concatenate([w_ukv4[:, :, :HEAD].reshape(ql, mlaw), w_ukv4[:, :, HEAD:].reshape(ql, mlaw)], axis=1)
    w_pm = full_cols("w_proj_mla")
    w_ps = full_cols("w_proj_sb")
    w_o = gathered["w_out"].reshape(D, D)
    w_fi = full_cols("w_ffn_in")
    w_fo = gathered["w_ffn_out"].reshape(dff, D)

    c_all = _gather_blocks([jnp.broadcast_to(c, (8, D))], name="gather_cond", in_vmem=True)[0][:, 0, :]
    n_ada = w_ada.shape[2]
    b_shard = lax.dynamic_slice_in_dim(b_ada, chip * n_ada, n_ada, axis=1)
    ada_shard = _mm(c_all, w_ada[0], name="ada_proj", a_fn=jax.nn.silu, bias=b_shard)
    ada_all = _gather_blocks([ada_shard], name="gather_ada", in_vmem=True)[0]
    ada_rows = lax.dynamic_index_in_dim(ada_all, dev, axis=1, keepdims=False)
    ada = ada_rows[0::2].reshape(1, 4 * n_ada)
    SH1, SC1, GT1, SH2, SC2, GT2 = range(6)

    half = ROPE // 2
    freqs = ROPE_THETA ** (-jnp.arange(half, dtype=F32) / half)
    ang = positions[0].astype(F32)[:, None] * freqs
    cos, sin = jnp.cos(ang), jnp.sin(ang)
    one = jnp.ones((S, NOPE), F32)
    zero = jnp.zeros((S, NOPE), F32)
    zh = jnp.zeros((S, half), F32)
    tabs = (jnp.concatenate([one, cos, cos, one[:, :HEAD_PAD - QK_DIM]], axis=1),
            jnp.concatenate([zero, zh, sin, zero[:, :HEAD_PAD - QK_DIM]], axis=1),
            jnp.concatenate([zero, -sin, zh, zero[:, :HEAD_PAD - QK_DIM]], axis=1))
    g_qh_p = _pad_cols(g_q_head, HEAD_PAD)
    g_kh_p = _pad_cols(g_k_head, HEAD_PAD)

    h1 = _rmsmod(x, g_norm1, ada, SC1, SH1, name="rmsmod1")
    proj = _mm(h1, w_in_p, name="mm_proj", tn=640)
    cqn, ckvn = _latent_norm(proj, g_q_latent, g_kv_latent, ql)
    q0 = _mm(cqn, w_uq_p, name="mm_q_up")
    kv0 = _mm(ckvn, w_ukv_p, name="mm_kv_up")
    q = _q_prep(q0, g_qh_p, tabs, nh)
    k = _k_prep(kv0, proj, kpe_col, g_kh_p, tabs, nh)
    y_a, lse = _mla_fwd(q, k, kv0, nh)
    y_b = _sb_fwd(proj, q_col, k_col, v_col, nh)
    pa = _mm(y_a, w_pm, name="mm_proj_mla")
    pb = _mm(y_b, w_ps, name="mm_proj_sb")
    merged = _gate_merge(pa, pb, proj, gla_col, glb_col)
    o = _mm(merged, w_o, name="mm_out")
    x2, h2 = _resid_rmsmod(x, o, g_norm2, ada, GT1, SC2, SH2)
    ff = _mm(h2, w_fi, name="mm_ffn_in", out_dtype=BF16)
    act = _swiglu(ff, dff)
    f = _mm(act, w_fo, name="mm_ffn_out")
    dy, df, red_l, loss_p = _loss_head(x2, f, tgt, ada, GT2)
    loss = lax.psum(loss_p[0, 0], ("x", "y", "c"))

    dact = _mm(df, w_fo, name="mm_d_act", tb=True)
    gw_fo = _mm(act, df, name="mm_gw_ffn_out", ta=True, out_dtype=BF16)
    dff_ = _swiglu_bwd(dact, ff, dff)
    dh2 = _mm(dff_, w_fi, name="mm_d_h2", tb=True)
    gw_fi = _mm(h2, dff_, name="mm_gw_ffn_in", ta=True, out_dtype=BF16)
    dx2, do, red_2 = _rmsmod2_bwd(dh2, x2, dy, o, g_norm2, ada, SC2, GT1)
    dmerged = _mm(do, w_o, name="mm_d_merged", tb=True)
    gw_o = _mm(merged, do, name="mm_gw_out", ta=True, out_dtype=BF16)
    dpa, dpb, dgla, dglb = _gate_bwd(dmerged, pa, pb, proj, gla_col, glb_col)
    dya = _mm(dpa, w_pm, name="mm_d_ya", tb=True)
    gw_pm = _mm(y_a, dpa, name="mm_gw_proj_mla", ta=True, out_dtype=BF16)
    dyb = _mm(dpb, w_ps, name="mm_d_yb", tb=True)
    gw_ps = _mm(y_b, dpb, name="mm_gw_proj_sb", ta=True, out_dtype=BF16)
    dq, dk, dv = _mla_bwd(q, k, kv0, y_a, dya, lse, nh)
    dq_sb, dk_sb, dv_sb = _sb_bwd(proj, q_col, k_col, v_col, dyb, nh)
    dq0, red_qh = _q_prep_bwd(dq, q0, g_qh_p, tabs, nh)
    dkv0, dkpe, red_kh = _k_prep_bwd(dk, dv, kv0, proj, kpe_col, g_kh_p, tabs, nh)
    dcqn = _mm(dq0, w_uq_p, name="mm_d_cqn", tb=True)
    gw_uq_p = _mm(cqn, dq0, name="mm_gw_uq", ta=True, out_dtype=BF16)
    dckvn = _mm(dkv0, w_ukv_p, name="mm_d_ckvn", tb=True)
    gw_ukv_p = _mm(ckvn, dkv0, name="mm_gw_ukv", ta=True, out_dtype=BF16)
    dcq, dckv, red_lat = _latent_norm_bwd(dcqn, dckvn, proj, g_q_latent, g_kv_latent, ql)
    dproj = jnp.concatenate([dcq, dckv, dq_sb.astype(BF16), dk_sb.astype(BF16), dv_sb.astype(BF16),
                             dgla, dglb, dkpe], axis=1)
    dh1 = _mm(dproj, w_in_p, name="mm_d_h1", tb=True, tk=640)
    gw_in_p = _mm(h1, dproj, name="mm_gw_in", ta=True, out_dtype=BF16, tn=640)
    grad_x, red_1 = _rmsmod1_bwd(dh1, x, dx2, g_norm1, ada, SC1)

    nsb = d_in_p - 2 * ROPE
    gw_in = jnp.concatenate([gw_in_p[:, :kpe0], gw_in_p[:, nsb:nsb + ROPE], gw_in_p[:, kpe0:nsb]], axis=1)
    gw_uq = gw_uq_p.reshape(ql, nh, HEAD_PAD)[:, :, :QK_DIM].reshape(ql, nh * QK_DIM)
    gw_ukv = jnp.concatenate([gw_ukv_p[:, :mlaw].reshape(ql, nh, HEAD), gw_ukv_p[:, mlaw:].reshape(ql, nh, HEAD)],
                             axis=2).reshape(ql, 2 * mlaw)
    full_grads = {"w_in": gw_in, "w_uq": gw_uq, "w_ukv": gw_ukv, "w_proj_mla": gw_pm, "w_proj_sb": gw_ps,
                  "w_out": gw_o, "w_ffn_in": gw_fi, "w_ffn_out": gw_fo}
    pieces = [(_rows_to_pieces if nm in row_sharded else _cols_to_pieces)(full_grads[nm]) for nm in names]

    from_sibling = _sibling_swap(pieces, name="rs_sibling_swap")
    pair = [_pair_sum(p, r, c_idx, name="rs_pair_sum_" + nm) for p, r, nm in zip(pieces, from_sibling, names)]
    from_chips = _chip_exchange(pair, name="rs_chip_exchange")
    reduced = [_chip_sum(s, r, chip_idx, name="rs_chip_sum_" + nm) for s, r, nm in zip(pair, from_chips, names)]
    joined = _sibling_join(reduced, name="rs_sibling_join")
    grads = {nm: j.reshape(mats[nm].shape) for nm, j in zip(names, joined)}

    vec_names = ["b_ada", "g_norm1", "g_norm2", "g_q_latent", "g_kv_latent", "g_q_head", "g_k_head"]
    vec_w = dict(b_ada=b_ada, g_norm1=g_norm1, g_norm2=g_norm2, g_q_latent=g_q_latent, g_kv_latent=g_kv_latent,
                 g_q_head=g_q_head, g_k_head=g_k_head)
    vec_m = dict(b_ada=m_b_ada, g_norm1=m_g_norm1, g_norm2=m_g_norm2, g_q_latent=m_g_q_latent,
                 g_kv_latent=m_g_kv_latent, g_q_head=m_g_q_head, g_k_head=m_g_k_head)
    vec_v = dict(b_ada=v_b_ada, g_norm1=v_g_norm1, g_norm2=v_g_norm2, g_q_latent=v_g_q_latent,
                 g_kv_latent=v_g_kv_latent, g_q_head=v_g_q_head, g_k_head=v_g_k_head)
    d_ada = jnp.concatenate([red_1[0:1], red_1[1:2], red_2[3:4], red_2[0:1], red_2[1:2], red_l[0:1]], axis=1)
    vec_parts = dict(b_ada=d_ada, g_norm1=red_1[2:3], g_norm2=red_2[2:3], g_q_latent=red_lat[0:1],
                     g_kv_latent=red_lat[1:2], g_q_head=red_qh[0:1], g_k_head=red_kh[0:1])
    widths = [-(-vec_w[nm].shape[1] // LANE) * LANE for nm in vec_names]
    offs = [sum(widths[:i]) for i in range(len(widths))]
    pack = lambda d: jnp.concatenate([_pad_cols(d[nm][:, :vec_w[nm].shape[1]], wd) for nm, wd in zip(vec_names, widths)], axis=1)
    nvec = sum(widths)
    parts_all = _gather_blocks([jnp.broadcast_to(pack(vec_parts), (8, nvec))], name="gather_vec_grads",
                               in_vmem=True)[0][:, 0, :]
    gvec, dvec, nmvec, nvvec = _adamw_vec(parts_all, pack(vec_w), pack(vec_m), pack(vec_v))
    unpack = lambda a: {nm: a[:, o_:o_ + vec_w[nm].shape[1]] for nm, o_ in zip(vec_names, offs)}
    gvec, dvec, nmvec, nvvec = unpack(gvec), unpack(dvec), unpack(nmvec), unpack(nvvec)

    dada_all = lax.dynamic_slice_in_dim(parts_all[:, :6 * D], chip * n_ada, n_ada, axis=1)
    cact_t = jax.nn.silu(c_all).T
    g_ada, d_ada_w, nm_ada, nv_ada = _adamw_ada(cact_t, dada_all, w_ada[0], m_w_ada[0], v_w_ada[0])

    ms = dict(w_in=m_w_in, w_uq=m_w_uq, w_ukv=m_w_ukv, w_proj_mla=m_w_proj_mla, w_proj_sb=m_w_proj_sb,
              w_out=m_w_out, w_ffn_in=m_w_ffn_in, w_ffn_out=m_w_ffn_out)
    vs = dict(w_in=v_w_in, w_uq=v_w_uq, w_ukv=v_w_ukv, w_proj_mla=v_w_proj_mla, w_proj_sb=v_w_proj_sb,
              w_out=v_w_out, w_ffn_in=v_w_ffn_in, w_ffn_out=v_w_ffn_out)
    G, DL, NM, NV = {}, {}, {}, {}
    for nm in names:
        G[nm] = grads[nm][None]
        d_, m_, v_ = _adamw(mats[nm], grads[nm], ms[nm][0], vs[nm][0], name="adamw_" + nm)
        DL[nm], NM[nm], NV[nm] = d_[None], m_[None], v_[None]
    G["w_ada"], DL["w_ada"], NM["w_ada"], NV["w_ada"] = g_ada[None], d_ada_w[None], nm_ada[None], nv_ada[None]
    for nm in vec_names:
        G[nm], DL[nm], NM[nm], NV[nm] = gvec[nm], dvec[nm], nmvec[nm], nvvec[nm]

    order = ["w_ada", "b_ada", "g_norm1", "g_norm2", "w_in", "g_q_latent", "g_kv_latent", "w_uq", "w_ukv",
             "g_q_head", "g_k_head", "w_proj_mla", "w_proj_sb", "w_out", "w_ffn_in", "w_ffn_out"]
    return (loss, grad_x[None], *[G[n] for n in order], *[DL[n] for n in order],
            *[NM[n] for n in order], *[NV[n] for n in order])
```

```python
import functools
import math

import jax
import jax.numpy as jnp
from jax import lax
from jax.experimental import pallas as pl
from jax.experimental.pallas import tpu as pltpu

F32 = jnp.float32
BF16 = jnp.bfloat16
MESH = pl.DeviceIdType.MESH

EPS = 1e-6
ROPE_THETA = 10000.0
NOPE = 128
ROPE = 64
QK_DIM = NOPE + ROPE
HEAD_PAD = 256
HEAD = 128
N_DEV = 8
LANE = 128
VMEM_LIMIT = 48 * 1024 * 1024

ADAM_LR = 0.001
ADAM_B1 = 0.9
ADAM_B2 = 0.999
ADAM_EPS = 1e-08
ADAM_WD = 0.01
ADAM_STEP = 10


def _tile(n, target):
    if n <= target:
        return n
    t = (target // LANE) * LANE
    while t >= LANE:
        if n % t == 0:
            return t
        t -= LANE
    return n


def _row_tile(rows, row_bytes, budget=24 * 1024 * 1024):
    cap = max(8, budget // (2 * row_bytes))
    best = None
    for t in range(8, min(rows, cap) + 1, 8):
        if rows % t == 0:
            best = t
    return best if best is not None else rows


def _params(sem):
    return pltpu.CompilerParams(dimension_semantics=sem, vmem_limit_bytes=VMEM_LIMIT)


def _rows(tm, w, col=0):
    return pl.BlockSpec((tm, w), lambda i: (i, col))


def _vec(w, col=0, rows=1):
    return pl.BlockSpec((rows, w), lambda i: (0, col))


MM_VMEM_BUDGET = 36 * 1024 * 1024


def _mm(a, b, *, name, ta=False, tb=False, out_dtype=F32, a_fn=None, bias=None, tm=1024, tn=1024):
    M = a.shape[1] if ta else a.shape[0]
    K = a.shape[0] if ta else a.shape[1]
    N = b.shape[0] if tb else b.shape[1]
    assert K == (b.shape[1] if tb else b.shape[0]), (a.shape, b.shape, ta, tb)
    tm, tn = _tile(M, tm), _tile(N, tn)
    sa, sb, so = a.dtype.itemsize, b.dtype.itemsize, jnp.dtype(out_dtype).itemsize

    def fits(tk):
        return 2 * tk * (tm * sa + tn * sb) + tm * tn * (2 * so + 4) <= MM_VMEM_BUDGET

    tk = K
    while not fits(tk):
        smaller = _tile(K, tk - LANE)
        if smaller >= tk:
            break
        tk = smaller
    nk = K // tk
    dn = (((0 if ta else 1,), (1 if tb else 0,)), ((), ()))
    b_outer = nk == 1 and a.size * sa * (N // tn) < b.size * sb * (M // tm)

    def body(*refs):
        a_ref, b_ref = refs[:2]
        bias_ref = refs[2] if bias is not None else None
        o_ref = refs[3 if bias is not None else 2]
        av = a_ref[...]
        if a_fn is not None:
            av = a_fn(av.astype(F32))
        part = lax.dot_general(av.astype(BF16), b_ref[...].astype(BF16), dn, preferred_element_type=F32)

        def finish(r):
            if bias is not None:
                r = r + bias_ref[...]
            o_ref[...] = r.astype(o_ref.dtype)

        if nk == 1:
            finish(part)
        else:
            acc_ref = refs[-1]
            k = pl.program_id(2)

            @pl.when(k == 0)
            def _():
                acc_ref[...] = part

            @pl.when(k > 0)
            def _():
                acc_ref[...] += part

            @pl.when(k == nk - 1)
            def _():
                finish(acc_ref[...])

    def ij(g0, g1):
        return (g1, g0) if b_outer else (g0, g1)

    def amap(g0, g1, k):
        i, _ = ij(g0, g1)
        return (k, i) if ta else (i, k)

    def bmap(g0, g1, k):
        _, j = ij(g0, g1)
        return (j, k) if tb else (k, j)

    in_specs = [pl.BlockSpec((tk, tm) if ta else (tm, tk), amap), pl.BlockSpec((tn, tk) if tb else (tk, tn), bmap)]
    args = [a, b]
    if bias is not None:
        in_specs.append(pl.BlockSpec((1, tn), lambda g0, g1, k: (0, ij(g0, g1)[1])))
        args.append(bias)
    grid = (N // tn, M // tm, nk) if b_outer else (M // tm, N // tn, nk)
    return pl.pallas_call(
        body, name=name, grid=grid, in_specs=in_specs,
        out_specs=pl.BlockSpec((tm, tn), lambda g0, g1, k: ij(g0, g1)),
        out_shape=jax.ShapeDtypeStruct((M, N), out_dtype),
        scratch_shapes=[pltpu.VMEM((tm, tn), F32)] if nk > 1 else [],
        compiler_params=_params(("parallel", "parallel", "arbitrary")),
    )(*args)


def _rms_rows(v):
    return lax.rsqrt(jnp.mean(v * v, axis=-1, keepdims=True) + EPS)


def _rmsmod(x, g, ada, sc_col, sh_col, *, name):
    S, D = x.shape
    tm = _tile(S, 256)

    def body(x_ref, g_ref, sc_ref, sh_ref, h_ref):
        xv = x_ref[...]
        h = (xv * _rms_rows(xv) * g_ref[...]) * (1.0 + sc_ref[...]) + sh_ref[...]
        h_ref[...] = h.astype(h_ref.dtype)

    return pl.pallas_call(
        body, name=name, grid=(S // tm,),
        in_specs=[_rows(tm, D), _vec(D), _vec(D, sc_col), _vec(D, sh_col)],
        out_specs=_rows(tm, D), out_shape=jax.ShapeDtypeStruct((S, D), BF16),
        compiler_params=_params(("parallel",)),
    )(x, g, ada, ada)


def _latent_norm(proj, g_q, g_kv, ql):
    S = proj.shape[0]
    tm = _tile(S, 512)

    def body(cq_ref, ckv_ref, gq_ref, gkv_ref, oq_ref, okv_ref):
        cq = cq_ref[...]
        oq_ref[...] = (cq * _rms_rows(cq) * gq_ref[...]).astype(BF16)
        ckv = ckv_ref[...]
        okv_ref[...] = (ckv * _rms_rows(ckv) * gkv_ref[...]).astype(BF16)

    return pl.pallas_call(
        body, name="latent_norm", grid=(S // tm,),
        in_specs=[_rows(tm, ql, 0), _rows(tm, ql, 1), _vec(ql), _vec(ql)],
        out_specs=[_rows(tm, ql), _rows(tm, ql)],
        out_shape=[jax.ShapeDtypeStruct((S, ql), BF16)] * 2,
        compiler_params=_params(("parallel",)),
    )(proj, proj, g_q, g_kv)


def _rope_fwd(y, c, s1, s2):
    return y * c + pltpu.roll(y, ROPE // 2, 1) * s1 + pltpu.roll(y, HEAD_PAD - ROPE // 2, 1) * s2


def _rope_bwd(d, c, s1, s2):
    return d * c + pltpu.roll(d * s1, HEAD_PAD - ROPE // 2, 1) + pltpu.roll(d * s2, ROPE // 2, 1)


def _head_rms(v):
    return lax.rsqrt(jnp.sum(v * v, axis=-1, keepdims=True) * (1.0 / QK_DIM) + EPS)


def _q_prep(q0, g_qh, tabs, nh):
    S = q0.shape[0]
    tm = _tile(S, 256)

    def body(q_ref, g_ref, c_ref, s1_ref, s2_ref, o_ref):
        c, s1, s2, g = c_ref[...], s1_ref[...], s2_ref[...], g_ref[...]
        for h in range(nh):
            sl = slice(h * HEAD_PAD, (h + 1) * HEAD_PAD)
            xs = q_ref[:, sl]
            o_ref[:, sl] = _rope_fwd(xs * _head_rms(xs) * g, c, s1, s2).astype(BF16)

    w = nh * HEAD_PAD
    return pl.pallas_call(
        body, name="mla_q_prep", grid=(S // tm,),
        in_specs=[_rows(tm, w), _vec(HEAD_PAD)] + [_rows(tm, HEAD_PAD)] * 3,
        out_specs=_rows(tm, w), out_shape=jax.ShapeDtypeStruct((S, w), BF16),
        compiler_params=_params(("parallel",)),
    )(q0, g_qh, *tabs)


def _k_prep(kv0, proj, kpe_col, g_kh, tabs, nh):
    S = kv0.shape[0]
    tm = _tile(S, 256)

    def body(kv_ref, kpe_ref, g_ref, c_ref, s1_ref, s2_ref, o_ref):
        c, s1, s2, g = c_ref[...], s1_ref[...], s2_ref[...], g_ref[...]
        kpe = kpe_ref[...]
        for h in range(nh):
            k0 = jnp.concatenate([kv_ref[:, h * HEAD:(h + 1) * HEAD], kpe], axis=1)
            o_ref[:, h * HEAD_PAD:(h + 1) * HEAD_PAD] = _rope_fwd(k0 * _head_rms(k0) * g, c, s1, s2).astype(BF16)

    return pl.pallas_call(
        body, name="mla_k_prep", grid=(S // tm,),
        in_specs=[_rows(tm, nh * HEAD, 0), _rows(tm, LANE, kpe_col), _vec(HEAD_PAD)] + [_rows(tm, HEAD_PAD)] * 3,
        out_specs=_rows(tm, nh * HEAD_PAD), out_shape=jax.ShapeDtypeStruct((S, nh * HEAD_PAD), BF16),
        compiler_params=_params(("parallel",)),
    )(kv0, proj, g_kh, *tabs)


def _gate_merge(pa, pb, proj, gla_col, glb_col):
    S, D = pa.shape
    tm = _tile(S, 256)

    def body(pa_ref, pb_ref, ga_ref, gb_ref, o_ref):
        o_ref[...] = (jax.nn.sigmoid(ga_ref[...]) * pa_ref[...] + jax.nn.sigmoid(gb_ref[...]) * pb_ref[...]).astype(BF16)

    return pl.pallas_call(
        body, name="gate_merge", grid=(S // tm,),
        in_specs=[_rows(tm, D), _rows(tm, D), _rows(tm, D, gla_col), _rows(tm, D, glb_col)],
        out_specs=_rows(tm, D), out_shape=jax.ShapeDtypeStruct((S, D), BF16),
        compiler_params=_params(("parallel",)),
    )(pa, pb, proj, proj)


def _resid_rmsmod(x, o, g, ada, gt_col, sc_col, sh_col):
    S, D = x.shape
    tm = _tile(S, 256)

    def body(x_ref, o_ref, g_ref, gt_ref, sc_ref, sh_ref, x2_ref, h_ref):
        x2 = x_ref[...] + gt_ref[...] * o_ref[...]
        x2_ref[...] = x2
        h_ref[...] = ((x2 * _rms_rows(x2) * g_ref[...]) * (1.0 + sc_ref[...]) + sh_ref[...]).astype(BF16)

    return pl.pallas_call(
        body, name="resid_rmsmod2", grid=(S // tm,),
        in_specs=[_rows(tm, D), _rows(tm, D), _vec(D), _vec(D, gt_col), _vec(D, sc_col), _vec(D, sh_col)],
        out_specs=[_rows(tm, D), _rows(tm, D)],
        out_shape=[jax.ShapeDtypeStruct((S, D), F32), jax.ShapeDtypeStruct((S, D), BF16)],
        compiler_params=_params(("parallel",)),
    )(x, o, g, ada, ada, ada)


def _swiglu(ff, dff_half):
    S = ff.shape[0]
    tm = _tile(S, 256)

    def body(g_ref, u_ref, o_ref):
        o_ref[...] = (jax.nn.silu(g_ref[...].astype(F32)) * u_ref[...].astype(F32)).astype(BF16)

    return pl.pallas_call(
        body, name="swiglu", grid=(S // tm,),
        in_specs=[_rows(tm, dff_half, 0), _rows(tm, dff_half, 1)],
        out_specs=_rows(tm, dff_half), out_shape=jax.ShapeDtypeStruct((S, dff_half), BF16),
        compiler_params=_params(("parallel",)),
    )(ff, ff)


def _loss_head(x2, f, tgt, ada, gt_col):
    S, D = x2.shape
    tm = _tile(S, 256)

    def body(x2_ref, f_ref, t_ref, gt_ref, dy_ref, df_ref, red_ref, loss_ref):
        @pl.when(pl.program_id(0) == 0)
        def _():
            red_ref[...] = jnp.zeros_like(red_ref)
            loss_ref[...] = jnp.zeros_like(loss_ref)

        fv = f_ref[...]
        gt = gt_ref[...]
        err = x2_ref[...] + gt * fv - t_ref[...]
        dy = err * (1.0 / D)
        dy_ref[...] = dy
        df_ref[...] = (dy * gt).astype(BF16)
        red_ref[0:1, :] += jnp.sum(dy * fv, axis=0, keepdims=True)
        loss_ref[...] += (0.5 / D) * jnp.sum(err * err)

    return pl.pallas_call(
        body, name="loss_head", grid=(S // tm,),
        in_specs=[_rows(tm, D), _rows(tm, D), _rows(tm, D), _vec(D, gt_col)],
        out_specs=[_rows(tm, D), _rows(tm, D), _vec(D, rows=8), _vec(LANE, rows=8)],
        out_shape=[jax.ShapeDtypeStruct((S, D), F32), jax.ShapeDtypeStruct((S, D), BF16),
                   jax.ShapeDtypeStruct((8, D), F32), jax.ShapeDtypeStruct((8, LANE), F32)],
        compiler_params=_params(("arbitrary",)),
    )(x2, f, tgt, ada)


def _swiglu_bwd(dact, ff, dff_half):
    S = ff.shape[0]
    tm = _tile(S, 128)

    def body(d_ref, g_ref, u_ref, o_ref):
        d = d_ref[...]
        g = g_ref[...].astype(F32)
        u = u_ref[...].astype(F32)
        sg = jax.nn.sigmoid(g)
        o_ref[:, :dff_half] = (d * u * sg * (1.0 + g * (1.0 - sg))).astype(BF16)
        o_ref[:, dff_half:] = (d * g * sg).astype(BF16)

    return pl.pallas_call(
        body, name="swiglu_bwd", grid=(S // tm,),
        in_specs=[_rows(tm, dff_half), _rows(tm, dff_half, 0), _rows(tm, dff_half, 1)],
        out_specs=_rows(tm, 2 * dff_half), out_shape=jax.ShapeDtypeStruct((S, 2 * dff_half), BF16),
        compiler_params=_params(("parallel",)),
    )(dact, ff, ff)


def _rmsmod2_bwd(dh2, x2, dy, o, g, ada, sc_col, gt_col):
    S, D = x2.shape
    tm = _tile(S, 256)

    def body(dh_ref, x2_ref, dy_ref, o_ref, g_ref, sc_ref, gt_ref, dx_ref, do_ref, red_ref):
        @pl.when(pl.program_id(0) == 0)
        def _():
            red_ref[...] = jnp.zeros_like(red_ref)

        dh = dh_ref[...]
        x2 = x2_ref[...]
        gv = g_ref[...]
        mod = 1.0 + sc_ref[...]
        r = _rms_rows(x2)
        xn = x2 * r
        t = dh * xn
        red_ref[0:1, :] += jnp.sum(dh, axis=0, keepdims=True)
        red_ref[1:2, :] += jnp.sum(t * gv, axis=0, keepdims=True)
        red_ref[2:3, :] += jnp.sum(t * mod, axis=0, keepdims=True)
        dxn = dh * gv * mod
        dx = dy_ref[...] + r * (dxn - xn * jnp.mean(dxn * xn, axis=-1, keepdims=True))
        dx_ref[...] = dx
        red_ref[3:4, :] += jnp.sum(dx * o_ref[...], axis=0, keepdims=True)
        do_ref[...] = (dx * gt_ref[...]).astype(BF16)

    return pl.pallas_call(
        body, name="rmsmod2_bwd", grid=(S // tm,),
        in_specs=[_rows(tm, D)] * 4 + [_vec(D), _vec(D, sc_col), _vec(D, gt_col)],
        out_specs=[_rows(tm, D), _rows(tm, D), _vec(D, rows=8)],
        out_shape=[jax.ShapeDtypeStruct((S, D), F32), jax.ShapeDtypeStruct((S, D), BF16),
                   jax.ShapeDtypeStruct((8, D), F32)],
        compiler_params=_params(("arbitrary",)),
    )(dh2, x2, dy, o, g, ada, ada)


def _rmsmod1_bwd(dh, x, dx2, g, ada, sc_col):
    S, D = x.shape
    tm = _tile(S, 256)

    def body(dh_ref, x_ref, dx2_ref, g_ref, sc_ref, gx_ref, red_ref):
        @pl.when(pl.program_id(0) == 0)
        def _():
            red_ref[...] = jnp.zeros_like(red_ref)

        dh = dh_ref[...]
        xv = x_ref[...]
        gv = g_ref[...]
        mod = 1.0 + sc_ref[...]
        r = _rms_rows(xv)
        xn = xv * r
        t = dh * xn
        red_ref[0:1, :] += jnp.sum(dh, axis=0, keepdims=True)
        red_ref[1:2, :] += jnp.sum(t * gv, axis=0, keepdims=True)
        red_ref[2:3, :] += jnp.sum(t * mod, axis=0, keepdims=True)
        dxn = dh * gv * mod
        gx_ref[...] = dx2_ref[...] + r * (dxn - xn * jnp.mean(dxn * xn, axis=-1, keepdims=True))

    return pl.pallas_call(
        body, name="rmsmod1_bwd", grid=(S // tm,),
        in_specs=[_rows(tm, D)] * 3 + [_vec(D), _vec(D, sc_col)],
        out_specs=[_rows(tm, D), _vec(D, rows=8)],
        out_shape=[jax.ShapeDtypeStruct((S, D), F32), jax.ShapeDtypeStruct((8, D), F32)],
        compiler_params=_params(("arbitrary",)),
    )(dh, x, dx2, g, ada)


def _gate_bwd(dm, pa, pb, proj, gla_col, glb_col):
    S, D = pa.shape
    tm = _tile(S, 256)

    def body(dm_ref, pa_ref, pb_ref, la_ref, lb_ref, dpa_ref, dpb_ref, dla_ref, dlb_ref):
        dm_ = dm_ref[...]
        ga = jax.nn.sigmoid(la_ref[...])
        gb = jax.nn.sigmoid(lb_ref[...])
        dpa_ref[...] = (dm_ * ga).astype(BF16)
        dpb_ref[...] = (dm_ * gb).astype(BF16)
        dla_ref[...] = (dm_ * pa_ref[...] * ga * (1.0 - ga)).astype(BF16)
        dlb_ref[...] = (dm_ * pb_ref[...] * gb * (1.0 - gb)).astype(BF16)

    return pl.pallas_call(
        body, name="gate_bwd", grid=(S // tm,),
        in_specs=[_rows(tm, D)] * 3 + [_rows(tm, D, gla_col), _rows(tm, D, glb_col)],
        out_specs=[_rows(tm, D)] * 4, out_shape=[jax.ShapeDtypeStruct((S, D), BF16)] * 4,
        compiler_params=_params(("parallel",)),
    )(dm, pa, pb, proj, proj)


def _q_prep_bwd(dq, q0, g_qh, tabs, nh):
    S = q0.shape[0]
    tm = _tile(S, 256)

    def body(dq_ref, q_ref, g_ref, c_ref, s1_ref, s2_ref, o_ref, red_ref):
        @pl.when(pl.program_id(0) == 0)
        def _():
            red_ref[...] = jnp.zeros_like(red_ref)

        c, s1, s2, g = c_ref[...], s1_ref[...], s2_ref[...], g_ref[...]
        dg = jnp.zeros((1, HEAD_PAD), F32)
        for h in range(nh):
            sl = slice(h * HEAD_PAD, (h + 1) * HEAD_PAD)
            d1 = _rope_bwd(dq_ref[:, sl], c, s1, s2)
            xs = q_ref[:, sl]
            r = _head_rms(xs)
            qn = xs * r
            dg = dg + jnp.sum(d1 * qn, axis=0, keepdims=True)
            dn = d1 * g
            o_ref[:, sl] = (r * (dn - qn * (jnp.sum(dn * qn, axis=-1, keepdims=True) * (1.0 / QK_DIM)))).astype(BF16)
        red_ref[0:1, :] += dg

    w = nh * HEAD_PAD
    return pl.pallas_call(
        body, name="mla_q_prep_bwd", grid=(S // tm,),
        in_specs=[_rows(tm, w), _rows(tm, w), _vec(HEAD_PAD)] + [_rows(tm, HEAD_PAD)] * 3,
        out_specs=[_rows(tm, w), _vec(HEAD_PAD, rows=8)],
        out_shape=[jax.ShapeDtypeStruct((S, w), BF16), jax.ShapeDtypeStruct((8, HEAD_PAD), F32)],
        compiler_params=_params(("arbitrary",)),
    )(dq, q0, g_qh, *tabs)


def _k_prep_bwd(dk, dv, kv0, proj, kpe_col, g_kh, tabs, nh):
    S = kv0.shape[0]
    tm = _tile(S, 256)
    wv = nh * HEAD

    def body(dk_ref, dv_ref, kv_ref, kpe_ref, g_ref, c_ref, s1_ref, s2_ref, o_ref, dpe_ref, red_ref):
        @pl.when(pl.program_id(0) == 0)
        def _():
            red_ref[...] = jnp.zeros_like(red_ref)

        c, s1, s2, g = c_ref[...], s1_ref[...], s2_ref[...], g_ref[...]
        kpe = kpe_ref[...]
        dg = jnp.zeros((1, HEAD_PAD), F32)
        dpe = jnp.zeros((tm, LANE), F32)
        for h in range(nh):
            d1 = _rope_bwd(dk_ref[:, h * HEAD_PAD:(h + 1) * HEAD_PAD], c, s1, s2)
            k0 = jnp.concatenate([kv_ref[:, h * HEAD:(h + 1) * HEAD], kpe], axis=1)
            r = _head_rms(k0)
            kn = k0 * r
            dg = dg + jnp.sum(d1 * kn, axis=0, keepdims=True)
            dn = d1 * g
            dk0 = r * (dn - kn * (jnp.sum(dn * kn, axis=-1, keepdims=True) * (1.0 / QK_DIM)))
            o_ref[:, h * HEAD:(h + 1) * HEAD] = dk0[:, :HEAD].astype(BF16)
            dpe = dpe + dk0[:, HEAD:]
        o_ref[:, wv:] = dv_ref[...].astype(BF16)
        dpe_ref[...] = dpe.astype(BF16)
        red_ref[0:1, :] += dg

    return pl.pallas_call(
        body, name="mla_k_prep_bwd", grid=(S // tm,),
        in_specs=[_rows(tm, nh * HEAD_PAD), _rows(tm, wv), _rows(tm, wv, 0), _rows(tm, LANE, kpe_col),
                  _vec(HEAD_PAD)] + [_rows(tm, HEAD_PAD)] * 3,
        out_specs=[_rows(tm, 2 * wv), _rows(tm, LANE), _vec(HEAD_PAD, rows=8)],
        out_shape=[jax.ShapeDtypeStruct((S, 2 * wv), BF16), jax.ShapeDtypeStruct((S, LANE), BF16),
                   jax.ShapeDtypeStruct((8, HEAD_PAD), F32)],
        compiler_params=_params(("arbitrary",)),
    )(dk, dv, kv0, proj, g_kh, *tabs)


def _latent_norm_bwd(dcqn, dckvn, proj, g_q, g_kv, ql):
    S = proj.shape[0]
    tm = _tile(S, 512)

    def body(dq_ref, dkv_ref, cq_ref, ckv_ref, gq_ref, gkv_ref, oq_ref, okv_ref, red_ref):
        @pl.when(pl.program_id(0) == 0)
        def _():
            red_ref[...] = jnp.zeros_like(red_ref)

        for row, (d_ref, c_ref, g_ref, o_ref) in enumerate(((dq_ref, cq_ref, gq_ref, oq_ref),
                                                            (dkv_ref, ckv_ref, gkv_ref, okv_ref))):
            d = d_ref[...]
            cv = c_ref[...]
            r = _rms_rows(cv)
            ch = cv * r
            red_ref[row:row + 1, :] += jnp.sum(d * ch, axis=0, keepdims=True)
            dn = d * g_ref[...]
            o_ref[...] = (r * (dn - ch * jnp.mean(dn * ch, axis=-1, keepdims=True))).astype(BF16)

    return pl.pallas_call(
        body, name="latent_norm_bwd", grid=(S // tm,),
        in_specs=[_rows(tm, ql), _rows(tm, ql), _rows(tm, ql, 0), _rows(tm, ql, 1), _vec(ql), _vec(ql)],
        out_specs=[_rows(tm, ql), _rows(tm, ql), _vec(ql, rows=8)],
        out_shape=[jax.ShapeDtypeStruct((S, ql), BF16)] * 2 + [jax.ShapeDtypeStruct((8, ql), F32)],
        compiler_params=_params(("arbitrary",)),
    )(dcqn, dckvn, proj, proj, g_q, g_kv)


NEG = -1e30
ATT_TILE = 512
SB_SUB = 128
_NT = (((1,), (1,)), ((), ()))
_TN = (((0,), (0,)), ((), ()))


def _dot(a, b, dn=(((1,), (0,)), ((), ()))):
    return lax.dot_general(a, b, dn, preferred_element_type=F32)


def _key_rows(kb, t):
    return pl.ds(pl.multiple_of(kb * t, t), t)


def _diag_mask(t, strict):
    r = lax.broadcasted_iota(jnp.int32, (t, t), 0)
    c = lax.broadcasted_iota(jnp.int32, (t, t), 1)
    return c < r if strict else c <= r


def _mla_fwd(q, k, kv0, nh):
    S = q.shape[0]
    t = _tile(S, ATT_TILE)
    scale = QK_DIM ** -0.5

    def body(q_ref, k_ref, v_ref, o_ref, lse_ref):
        i = pl.program_id(1)
        qv = q_ref[...]

        def block(kb, carry, masked):
            m, l, acc = carry
            rows = _key_rows(kb, t)
            s = _dot(qv, k_ref[rows, :], _NT) * scale
            if masked:
                s = jnp.where(_diag_mask(t, False), s, NEG)
            m_new = jnp.maximum(m, jnp.max(s, axis=-1, keepdims=True))
            alpha = jnp.exp(m - m_new)
            p = jnp.exp(s - m_new)
            l = alpha * l + jnp.sum(p, axis=-1, keepdims=True)
            acc = alpha * acc + _dot(p.astype(BF16), v_ref[rows, :].astype(BF16))
            return m_new, l, acc

        init = (jnp.full((t, 1), NEG, F32), jnp.zeros((t, 1), F32), jnp.zeros((t, HEAD), F32))
        carry = lax.fori_loop(0, i, lambda kb, c: block(kb, c, False), init)
        m, l, acc = block(i, carry, True)
        o_ref[...] = acc / l
        lse_ref[...] = m + jnp.log(l)

    return pl.pallas_call(
        body, name="mla_attn_fwd", grid=(nh, S // t),
        in_specs=[pl.BlockSpec((t, HEAD_PAD), lambda h, i: (i, h)),
                  pl.BlockSpec((S, HEAD_PAD), lambda h, i: (0, h)),
                  pl.BlockSpec((S, HEAD), lambda h, i: (0, nh + h))],
        out_specs=[pl.BlockSpec((t, HEAD), lambda h, i: (i, h)),
                   pl.BlockSpec((None, t, 1), lambda h, i: (h, i, 0))],
        out_shape=[jax.ShapeDtypeStruct((S, nh * HEAD), F32), jax.ShapeDtypeStruct((nh, S, 1), F32)],
        compiler_params=_params(("parallel", "arbitrary")),
    )(q, k, kv0)


def _mla_bwd(q, k, kv0, o, do, lse, nh):
    S = q.shape[0]
    t = _tile(S, ATT_TILE)
    scale = QK_DIM ** -0.5

    def body(q_ref, k_ref, v_ref, o_ref, do_ref, lse_ref, dq_ref, dk_ref, dv_ref):
        i = pl.program_id(1)

        @pl.when(i == 0)
        def _():
            dk_ref[...] = jnp.zeros_like(dk_ref)
            dv_ref[...] = jnp.zeros_like(dv_ref)

        qv = q_ref[...]
        dov = do_ref[...]
        delta = jnp.sum(dov * o_ref[...], axis=-1, keepdims=True)
        dob = dov.astype(BF16)
        lse = lse_ref[...]

        def block(kb, dq, masked):
            rows = _key_rows(kb, t)
            ks = k_ref[rows, :]
            vs = v_ref[rows, :].astype(BF16)
            p = jnp.exp(_dot(qv, ks, _NT) * scale - lse)
            if masked:
                p = jnp.where(_diag_mask(t, False), p, 0.0)
            ds = (p * (_dot(dob, vs, _NT) - delta) * scale).astype(BF16)
            dk_ref[rows, :] += _dot(ds, qv, _TN)
            dv_ref[rows, :] += _dot(p.astype(BF16), dob, _TN)
            return dq + _dot(ds, ks)

        dq = lax.fori_loop(0, i, lambda kb, c: block(kb, c, False), jnp.zeros((t, HEAD_PAD), F32))
        dq_ref[...] = block(i, dq, True)

    return pl.pallas_call(
        body, name="mla_attn_bwd", grid=(nh, S // t),
        in_specs=[pl.BlockSpec((t, HEAD_PAD), lambda h, i: (i, h)),
                  pl.BlockSpec((S, HEAD_PAD), lambda h, i: (0, h)),
                  pl.BlockSpec((S, HEAD), lambda h, i: (0, nh + h)),
                  pl.BlockSpec((t, HEAD), lambda h, i: (i, h)),
                  pl.BlockSpec((t, HEAD), lambda h, i: (i, h)),
                  pl.BlockSpec((None, t, 1), lambda h, i: (h, i, 0))],
        out_specs=[pl.BlockSpec((t, HEAD_PAD), lambda h, i: (i, h)),
                   pl.BlockSpec((S, HEAD_PAD), lambda h, i: (0, h)),
                   pl.BlockSpec((S, HEAD), lambda h, i: (0, h))],
        out_shape=[jax.ShapeDtypeStruct((S, nh * HEAD_PAD), F32), jax.ShapeDtypeStruct((S, nh * HEAD_PAD), F32),
                   jax.ShapeDtypeStruct((S, nh * HEAD), F32)],
        compiler_params=_params(("parallel", "arbitrary")),
    )(q, k, kv0, o, do, lse)


def _split_dot(v, tri):
    hi = v.astype(BF16)
    lo = (v - hi.astype(F32)).astype(BF16)
    return _dot(hi, tri) + _dot(lo, tri)


def _tri(n, cmp):
    r = lax.broadcasted_iota(jnp.int32, (n, n), 0)
    c = lax.broadcasted_iota(jnp.int32, (n, n), 1)
    return jnp.where(cmp(r, c), 1.0, 0.0).astype(BF16)


def _sb_block(qv, ks, run, upper, t, scale, masked):
    z = _dot(qv, ks, _NT) * scale
    lb = jnp.minimum(z, 0.0) - jnp.log(1.0 + jnp.exp(-jnp.abs(z)))
    lom = lb - z
    mask = _diag_mask(t, True) if masked else None
    if masked:
        lom = jnp.where(mask, lom, 0.0)
    tails = []
    for sblk in reversed(range(t // SB_SUB)):
        part = lom[:, sblk * SB_SUB:(sblk + 1) * SB_SUB]
        tails.append(_split_dot(part, upper) + run)
        run = run + jnp.sum(part, axis=-1, keepdims=True)
    a = jnp.exp(lb + jnp.concatenate(tails[::-1], axis=1))
    if masked:
        a = jnp.where(mask, a, 0.0)
    return a, lb, mask, run


def _sb_fwd(proj, q_col, k_col, v_col, nh):
    S = proj.shape[0]
    t = _tile(S, ATT_TILE)
    scale = HEAD ** -0.5

    def body(q_ref, k_ref, v_ref, o_ref):
        i = pl.program_id(1)
        qv = q_ref[...].astype(BF16)
        upper = _tri(SB_SUB, lambda j, s: j > s)

        def block(kb, carry, masked):
            run, acc = carry
            rows = _key_rows(kb, t)
            a, _, _, run = _sb_block(qv, k_ref[rows, :].astype(BF16), run, upper, t, scale, masked)
            return run, acc + _dot(a.astype(BF16), v_ref[rows, :].astype(BF16))

        carry = block(i, (jnp.zeros((t, 1), F32), jnp.zeros((t, HEAD), F32)), True)
        o_ref[...] = lax.fori_loop(0, i, lambda j, c: block(i - 1 - j, c, False), carry)[1]

    return pl.pallas_call(
        body, name="sb_attn_fwd", grid=(nh, S // t),
        in_specs=[pl.BlockSpec((t, HEAD), lambda h, i: (i, q_col + h)),
                  pl.BlockSpec((S, HEAD), lambda h, i: (0, k_col + h)),
                  pl.BlockSpec((S, HEAD), lambda h, i: (0, v_col + h))],
        out_specs=pl.BlockSpec((t, HEAD), lambda h, i: (i, h)),
        out_shape=jax.ShapeDtypeStruct((S, nh * HEAD), F32),
        compiler_params=_params(("parallel", "arbitrary")),
    )(proj, proj, proj)


def _sb_bwd(proj, q_col, k_col, v_col, dy, nh):
    S = proj.shape[0]
    t = _tile(S, ATT_TILE)
    scale = HEAD ** -0.5

    def body(q_ref, k_ref, v_ref, dy_ref, dq_ref, dk_ref, dv_ref, run_ref):
        i = pl.program_id(1)

        @pl.when(i == 0)
        def _():
            dk_ref[...] = jnp.zeros_like(dk_ref)
            dv_ref[...] = jnp.zeros_like(dv_ref)

        qv = q_ref[...].astype(BF16)
        dyb = dy_ref[...].astype(BF16)
        upper = _tri(SB_SUB, lambda j, s: j > s)
        before = _tri(SB_SUB, lambda s, j: s < j)

        def suffix(kb, run, masked):
            z = _dot(qv, k_ref[_key_rows(kb, t), :].astype(BF16), _NT) * scale
            lom = jnp.minimum(z, 0.0) - jnp.log(1.0 + jnp.exp(-jnp.abs(z))) - z
            if masked:
                lom = jnp.where(_diag_mask(t, True), lom, 0.0)
            run_ref[kb] = jnp.broadcast_to(run, (t, LANE))
            return run + jnp.sum(lom, axis=-1, keepdims=True)

        run0 = suffix(i, jnp.zeros((t, 1), F32), True)
        lax.fori_loop(0, i, lambda j, r: suffix(i - 1 - j, r, False), run0)

        def block(kb, carry, masked):
            prefix, dq = carry
            rows = _key_rows(kb, t)
            ks = k_ref[rows, :].astype(BF16)
            vs = v_ref[rows, :].astype(BF16)
            a, lb, mask, _ = _sb_block(qv, ks, run_ref[kb][:, 0:1], upper, t, scale, masked)
            dl = a * _dot(dyb, vs, _NT)
            lefts = []
            for sblk in range(t // SB_SUB):
                part = dl[:, sblk * SB_SUB:(sblk + 1) * SB_SUB]
                lefts.append(_dot(part.astype(BF16), before) + prefix)
                prefix = prefix + jnp.sum(part, axis=-1, keepdims=True)
            beta = jnp.exp(lb)
            dz = dl * (1.0 - beta) - beta * jnp.concatenate(lefts, axis=1)
            if masked:
                dz = jnp.where(mask, dz, 0.0)
            dz = (dz * scale).astype(BF16)
            dk_ref[rows, :] += _dot(dz, qv, _TN)
            dv_ref[rows, :] += _dot(a.astype(BF16), dyb, _TN)
            return prefix, dq + _dot(dz, ks)

        carry = lax.fori_loop(0, i, lambda kb, c: block(kb, c, False),
                              (jnp.zeros((t, 1), F32), jnp.zeros((t, HEAD), F32)))
        dq_ref[...] = block(i, carry, True)[1]

    full = pl.BlockSpec((S, HEAD), lambda h, i: (0, h))
    tile = pl.BlockSpec((t, HEAD), lambda h, i: (i, h))
    return pl.pallas_call(
        body, name="sb_attn_bwd", grid=(nh, S // t),
        in_specs=[pl.BlockSpec((t, HEAD), lambda h, i: (i, q_col + h)),
                  pl.BlockSpec((S, HEAD), lambda h, i: (0, k_col + h)),
                  pl.BlockSpec((S, HEAD), lambda h, i: (0, v_col + h)), tile],
        out_specs=[tile, full, full],
        out_shape=[jax.ShapeDtypeStruct((S, nh * HEAD), F32)] * 3,
        scratch_shapes=[pltpu.VMEM((S // t, t, LANE), F32)],
        compiler_params=_params(("parallel", "arbitrary")),
    )(proj, proj, proj, dy)


def _place():
    return lax.axis_index("x"), lax.axis_index("y"), lax.axis_index("c")


def _other_chips(x, y):
    return [(1 - x, y), (x, 1 - y), (1 - x, 1 - y)]


def _dev_index(p):
    return 4 * p[0] + 2 * p[1] + p[2]


def _gather_blocks(blocks, *, name, in_vmem):
    n = len(blocks)
    per = 7

    def body(*refs):
        ins, outs = refs[:n], refs[n:2 * n]
        send_sems, recv_sems, local_sems = refs[2 * n:]
        x, y, c = _place()
        me, sibling = (x, y, c), (x, y, 1 - c)
        chips = _other_chips(x, y)

        def slot(a, p):
            return outs[a].at[_dev_index(p)]

        def copy(a, k, block, to, src=None):
            return pltpu.make_async_remote_copy(
                src_ref=slot(a, block) if src is None else src, dst_ref=slot(a, block),
                send_sem=send_sems.at[a * per + k], recv_sem=recv_sems.at[a * per + k],
                device_id=to, device_id_type=MESH)

        mine = [pltpu.make_async_copy(ins[a], slot(a, me), local_sems.at[a]) for a in range(n)]
        for cp in mine:
            cp.start()
        first = []
        for a in range(n):
            first.append(copy(a, 0, me, sibling, src=ins[a]))
            first += [copy(a, 1 + j, me, (*chip, c), src=ins[a]) for j, chip in enumerate(chips)]
        for cp in first:
            cp.start()
        passed = []
        for a in range(n):
            for j, chip in enumerate(chips):
                copy(a, 1 + j, (*chip, c), me).wait_recv()
                cp = copy(a, 4 + j, (*chip, c), sibling)
                cp.start()
                passed.append(cp)
        for a in range(n):
            copy(a, 0, sibling, me).wait_recv()
            for j, chip in enumerate(chips):
                copy(a, 4 + j, (*chip, 1 - c), me).wait_recv()
        for cp in first + passed:
            cp.wait_send()
        for cp in mine:
            cp.wait()

    space = pltpu.VMEM if in_vmem else pl.ANY
    spec = pl.BlockSpec(memory_space=space)
    outs = pl.pallas_call(
        body, name=name, in_specs=[spec] * n, out_specs=[spec] * n,
        out_shape=[jax.ShapeDtypeStruct((N_DEV,) + b.shape, b.dtype) for b in blocks],
        scratch_shapes=[pltpu.SemaphoreType.DMA((n * per,)), pltpu.SemaphoreType.DMA((n * per,)),
                        pltpu.SemaphoreType.DMA((n,))],
        compiler_params=pltpu.CompilerParams(vmem_limit_bytes=VMEM_LIMIT),
    )(*blocks)
    return list(outs)


def _sibling_swap(arrs, *, name, whole=False):
    n = len(arrs)

    def body(*refs):
        ins, outs = refs[:n], refs[n:2 * n]
        send_sems, recv_sems = refs[2 * n:]
        x, y, c = _place()
        copies = [pltpu.make_async_remote_copy(
            src_ref=ins[a] if whole else ins[a].at[1 - c], dst_ref=outs[a],
            send_sem=send_sems.at[a], recv_sem=recv_sems.at[a],
            device_id=(x, y, 1 - c), device_id_type=MESH) for a in range(n)]
        for cp in copies:
            cp.start()
        for cp in copies:
            cp.wait()

    spec = pl.BlockSpec(memory_space=pl.ANY)
    return list(pl.pallas_call(
        body, name=name, in_specs=[spec] * n, out_specs=[spec] * n,
        out_shape=[jax.ShapeDtypeStruct(a.shape if whole else a.shape[1:], a.dtype) for a in arrs],
        scratch_shapes=[pltpu.SemaphoreType.DMA((n,)), pltpu.SemaphoreType.DMA((n,))],
    )(*arrs))


def _chip_exchange(arrs, *, name):
    n = len(arrs)

    def body(*refs):
        ins, outs = refs[:n], refs[n:2 * n]
        send_sems, recv_sems = refs[2 * n:]
        x, y, c = _place()
        copies = []
        for a in range(n):
            for j, (px, py) in enumerate(_other_chips(x, y)):
                copies.append(pltpu.make_async_remote_copy(
                    src_ref=ins[a].at[2 * px + py], dst_ref=outs[a].at[j],
                    send_sem=send_sems.at[3 * a + j], recv_sem=recv_sems.at[3 * a + j],
                    device_id=(px, py, c), device_id_type=MESH))
        for cp in copies:
            cp.start()
        for cp in copies:
            cp.wait()

    spec = pl.BlockSpec(memory_space=pl.ANY)
    return list(pl.pallas_call(
        body, name=name, in_specs=[spec] * n, out_specs=[spec] * n,
        out_shape=[jax.ShapeDtypeStruct((3,) + a.shape[1:], a.dtype) for a in arrs],
        scratch_shapes=[pltpu.SemaphoreType.DMA((3 * n,)), pltpu.SemaphoreType.DMA((3 * n,))],
    )(*arrs))


def _flat2(a, lead):
    return a.reshape(a.shape[:lead] + (-1, a.shape[-1]))


def _pair_sum(g, recv, c_idx, *, name):
    _, nchip, r, w = g.shape
    tm = _tile(r, 256) if r % 8 == 0 else r

    def body(c_ref, g_ref, r_ref, o_ref):
        o_ref[...] = (g_ref[...].astype(F32) + r_ref[...].astype(F32)).astype(o_ref.dtype)

    return pl.pallas_call(
        body, name=name,
        grid_spec=pltpu.PrefetchScalarGridSpec(
            num_scalar_prefetch=1, grid=(nchip, r // tm),
            in_specs=[pl.BlockSpec((None, None, tm, w), lambda k, i, c_ref: (c_ref[0], k, i, 0)),
                      pl.BlockSpec((None, tm, w), lambda k, i, c_ref: (k, i, 0))],
            out_specs=pl.BlockSpec((None, tm, w), lambda k, i, c_ref: (k, i, 0))),
        out_shape=jax.ShapeDtypeStruct((nchip, r, w), BF16),
        compiler_params=_params(("parallel", "parallel")),
    )(c_idx, g, recv)


def _chip_sum(s1, recv, chip_idx, *, name):
    _, r, w = s1.shape
    tm = _tile(r, 256) if r % 8 == 0 else r

    def body(k_ref, s_ref, r_ref, o_ref):
        acc = s_ref[...].astype(F32)
        for j in range(3):
            acc = acc + r_ref[j].astype(F32)
        o_ref[...] = acc

    return pl.pallas_call(
        body, name=name,
        grid_spec=pltpu.PrefetchScalarGridSpec(
            num_scalar_prefetch=1, grid=(r // tm,),
            in_specs=[pl.BlockSpec((None, tm, w), lambda i, k_ref: (k_ref[0], i, 0)),
                      pl.BlockSpec((3, tm, w), lambda i, k_ref: (0, i, 0))],
            out_specs=pl.BlockSpec((tm, w), lambda i, k_ref: (i, 0))),
        out_shape=jax.ShapeDtypeStruct((r, w), F32),
        compiler_params=_params(("parallel",)),
    )(chip_idx, s1, recv)


def _adam_math(w, g, m, v):
    m = ADAM_B1 * m + (1.0 - ADAM_B1) * g
    v = ADAM_B2 * v + (1.0 - ADAM_B2) * (g * g)
    m_hat = m / (1.0 - ADAM_B1 ** ADAM_STEP)
    v_hat = v / (1.0 - ADAM_B2 ** ADAM_STEP)
    delta = -ADAM_LR * (m_hat / (jnp.sqrt(v_hat) + ADAM_EPS) + ADAM_WD * w)
    return delta, m, v


def _adamw(w, mine, other, c_idx, m, v, *, name):
    r, cw = w.shape
    hr = r // 2
    tm = _row_tile(hr, 9 * cw * 4)

    def body(c_ref, w_ref, a_ref, b_ref, m_ref, v_ref, g_ref, d_ref, nm_ref, nv_ref):
        g = jnp.where(pl.program_id(0) == c_ref[0], a_ref[...], b_ref[...])
        g_ref[...] = g
        d_ref[...], nm_ref[...], nv_ref[...] = _adam_math(w_ref[...], g, m_ref[...], v_ref[...])

    full = pl.BlockSpec((None, tm, cw), lambda h, i, c_ref: (h, i, 0))
    half = pl.BlockSpec((tm, cw), lambda h, i, c_ref: (i, 0))
    outs = pl.pallas_call(
        body, name=name,
        grid_spec=pltpu.PrefetchScalarGridSpec(
            num_scalar_prefetch=1, grid=(2, hr // tm),
            in_specs=[full, half, half, full, full], out_specs=[full] * 4),
        out_shape=[jax.ShapeDtypeStruct((2, hr, cw), F32)] * 4,
        compiler_params=_params(("parallel", "parallel")),
    )(c_idx, w.reshape(2, hr, cw), mine, other, m.reshape(2, hr, cw), v.reshape(2, hr, cw))
    return [o.reshape(r, cw) for o in outs]


def _adamw_ada(cact_t, dada, w, m, v):
    r, cw = w.shape
    nb = cact_t.shape[1]
    tm = _tile(r, 256)
    tn = _tile(cw, 1024)

    def body(a_ref, d_ref, w_ref, m_ref, v_ref, g_ref, dl_ref, nm_ref, nv_ref):
        a = a_ref[...]
        d = d_ref[...]
        g = a[:, 0:1] * d[0:1, :]
        for b in range(1, nb):
            g = g + a[:, b:b + 1] * d[b:b + 1, :]
        g_ref[...] = g
        dl_ref[...], nm_ref[...], nv_ref[...] = _adam_math(w_ref[...], g, m_ref[...], v_ref[...])

    blk = pl.BlockSpec((tm, tn), lambda i, j: (i, j))
    return pl.pallas_call(
        body, name="adamw_ada", grid=(r // tm, cw // tn),
        in_specs=[pl.BlockSpec((tm, nb), lambda i, j: (i, 0)), pl.BlockSpec((nb, tn), lambda i, j: (0, j)), blk, blk, blk],
        out_specs=[blk] * 4, out_shape=[jax.ShapeDtypeStruct((r, cw), F32)] * 4,
        compiler_params=_params(("parallel", "parallel")),
    )(cact_t, dada, w, m, v)


def _adamw_vec(parts, w, m, v):
    n = w.shape[1]

    def body(p_ref, w_ref, m_ref, v_ref, g_ref, d_ref, nm_ref, nv_ref):
        p = p_ref[...]
        g = p[0:1, :]
        for b in range(1, N_DEV):
            g = g + p[b:b + 1, :]
        g_ref[...] = g
        d_ref[...], nm_ref[...], nv_ref[...] = _adam_math(w_ref[...], g, m_ref[...], v_ref[...])

    return pl.pallas_call(
        body, name="adamw_vec", out_shape=[jax.ShapeDtypeStruct((1, n), F32)] * 4,
        compiler_params=pltpu.CompilerParams(vmem_limit_bytes=VMEM_LIMIT),
    )(parts, w, m, v)


def _cols_from_chips(g8, rows):
    cs = g8.shape[-1]
    return g8.reshape(4, rows, cs).transpose(1, 0, 2).reshape(rows, 4 * cs)


def _cols_to_pieces(g):
    rows, c4 = g.shape
    return g.reshape(2, rows // 2, 4, c4 // 4).transpose(0, 2, 1, 3)


def _rows_to_pieces(g):
    r4, cols = g.shape
    return g.reshape(4, 2, r4 // 8, cols).transpose(1, 0, 2, 3)


def _pad_cols(a, w):
    return jnp.pad(a, ((0, 0), (0, w - a.shape[1])))


def kernel(x, c, positions, w_ada, b_ada, g_norm1, g_norm2, w_in, g_q_latent, g_kv_latent, w_uq, w_ukv, g_q_head, g_k_head, w_proj_mla, w_proj_sb, w_out, w_ffn_in, w_ffn_out, loss_target, m_w_ada, m_b_ada, m_g_norm1, m_g_norm2, m_w_in, m_g_q_latent, m_g_kv_latent, m_w_uq, m_w_ukv, m_g_q_head, m_g_k_head, m_w_proj_mla, m_w_proj_sb, m_w_out, m_w_ffn_in, m_w_ffn_out, v_w_ada, v_b_ada, v_g_norm1, v_g_norm2, v_w_in, v_g_q_latent, v_g_kv_latent, v_w_uq, v_w_ukv, v_g_q_head, v_g_k_head, v_w_proj_mla, v_w_proj_sb, v_w_out, v_w_ffn_in, v_w_ffn_out):
    xi, yi, ci = _place()
    chip = 2 * xi + yi
    dev = 2 * chip + ci
    c_idx = jnp.reshape(ci, (1,)).astype(jnp.int32)
    chip_idx = jnp.reshape(chip, (1,)).astype(jnp.int32)

    x = x[0]
    tgt = loss_target[0]
    S, D = x.shape
    ql = g_q_latent.shape[1]
    assert g_kv_latent.shape[1] == ql
    mlaw = w_proj_mla.shape[1]
    nh = mlaw // HEAD
    sbw = w_proj_sb.shape[1]
    assert sbw == mlaw
    dff = w_ffn_out.shape[1] * 4
    d_in = 2 * ql + ROPE + 3 * sbw + 2 * D
    d_in_p = d_in + ROPE
    q_col = (2 * ql) // HEAD
    k_col = q_col + nh
    v_col = k_col + nh
    gla_col = (2 * ql + 3 * sbw) // D
    glb_col = gla_col + 1
    kpe_col = (d_in - ROPE) // LANE
    assert (2 * ql + 3 * sbw) % D == 0 and (d_in - ROPE) % LANE == 0

    mats = {"w_in": w_in[0], "w_uq": w_uq[0], "w_ukv": w_ukv[0], "w_proj_mla": w_proj_mla[0],
            "w_proj_sb": w_proj_sb[0], "w_out": w_out[0], "w_ffn_in": w_ffn_in[0], "w_ffn_out": w_ffn_out[0]}
    names = list(mats)
    row_sharded = {"w_out", "w_ffn_out"}

    halves = []
    for nm in names:
        w = mats[nm]
        hr = w.shape[0] // 2
        halves.append(lax.dynamic_slice_in_dim(w, ci * hr, hr, axis=0).astype(BF16))
    gathered = dict(zip(names, _gather_blocks(halves, name="gather_weights", in_vmem=False)))

    def full_cols(nm):
        return _cols_from_chips(gathered[nm], mats[nm].shape[0])

    w_in_f = full_cols("w_in")
    kpe0 = 2 * ql
    w_in_p = jnp.concatenate([w_in_f[:, :kpe0], w_in_f[:, kpe0 + ROPE:], w_in_f[:, kpe0:kpe0 + ROPE],
                              jnp.zeros((D, ROPE), BF16)], axis=1)
    w_uq_p = jnp.pad(full_cols("w_uq").reshape(ql, nh, QK_DIM), ((0, 0), (0, 0), (0, HEAD_PAD - QK_DIM))
                     ).reshape(ql, nh * HEAD_PAD)
    w_ukv4 = full_cols("w_ukv").reshape(ql, nh, 2 * HEAD)
    w_ukv_p = jnp.concatenate([w_ukv4[:, :, :HEAD].reshape(ql, mlaw), w_ukv4[:, :, HEAD:].reshape(ql, mlaw)], axis=1)
    w_pm = full_cols("w_proj_mla")
    w_ps = full_cols("w_proj_sb")
    w_o = gathered["w_out"].reshape(D, D)
    w_fi = full_cols("w_ffn_in")
    w_fo = gathered["w_ffn_out"].reshape(dff, D)

    c_all = _gather_blocks([jnp.broadcast_to(c, (8, D))], name="gather_cond", in_vmem=True)[0][:, 0, :]
    n_ada = w_ada.shape[2]
    b_shard = lax.dynamic_slice_in_dim(b_ada, chip * n_ada, n_ada, axis=1)
    ada_shard = _mm(c_all, w_ada[0], name="ada_proj", a_fn=jax.nn.silu, bias=b_shard)
    ada_all = _gather_blocks([ada_shard], name="gather_ada", in_vmem=True)[0]
    ada_rows = lax.dynamic_index_in_dim(ada_all, dev, axis=1, keepdims=False)
    ada = ada_rows[0::2].reshape(1, 4 * n_ada)
    SH1, SC1, GT1, SH2, SC2, GT2 = range(6)

    half = ROPE // 2
    freqs = ROPE_THETA ** (-jnp.arange(half, dtype=F32) / half)
    ang = positions[0].astype(F32)[:, None] * freqs
    cos, sin = jnp.cos(ang), jnp.sin(ang)
    one = jnp.ones((S, NOPE), F32)
    zero = jnp.zeros((S, NOPE), F32)
    zh = jnp.zeros((S, half), F32)
    tabs = (jnp.concatenate([one, cos, cos, one[:, :HEAD_PAD - QK_DIM]], axis=1),
            jnp.concatenate([zero, zh, sin, zero[:, :HEAD_PAD - QK_DIM]], axis=1),
            jnp.concatenate([zero, -sin, zh, zero[:, :HEAD_PAD - QK_DIM]], axis=1))
    g_qh_p = _pad_cols(g_q_head, HEAD_PAD)
    g_kh_p = _pad_cols(g_k_head, HEAD_PAD)

    h1 = _rmsmod(x, g_norm1, ada, SC1, SH1, name="rmsmod1")
    proj = _mm(h1, w_in_p, name="mm_proj", tn=640)
    cqn, ckvn = _latent_norm(proj, g_q_latent, g_kv_latent, ql)
    q0 = _mm(cqn, w_uq_p, name="mm_q_up")
    kv0 = _mm(ckvn, w_ukv_p, name="mm_kv_up")
    q = _q_prep(q0, g_qh_p, tabs, nh)
    k = _k_prep(kv0, proj, kpe_col, g_kh_p, tabs, nh)
    y_a, lse = _mla_fwd(q, k, kv0, nh)
    y_b = _sb_fwd(proj, q_col, k_col, v_col, nh)
    pa = _mm(y_a, w_pm, name="mm_proj_mla")
    pb = _mm(y_b, w_ps, name="mm_proj_sb")
    merged = _gate_merge(pa, pb, proj, gla_col, glb_col)
    o = _mm(merged, w_o, name="mm_out")
    x2, h2 = _resid_rmsmod(x, o, g_norm2, ada, GT1, SC2, SH2)
    ff = _mm(h2, w_fi, name="mm_ffn_in", out_dtype=BF16)
    act = _swiglu(ff, dff)
    f = _mm(act, w_fo, name="mm_ffn_out")
    dy, df, red_l, loss_p = _loss_head(x2, f, tgt, ada, GT2)
    loss = lax.psum(loss_p[0, 0], ("x", "y", "c"))

    dact = _mm(df, w_fo, name="mm_d_act", tb=True)
    gw_fo = _mm(act, df, name="mm_gw_ffn_out", ta=True, out_dtype=BF16)
    dff_ = _swiglu_bwd(dact, ff, dff)
    dh2 = _mm(dff_, w_fi, name="mm_d_h2", tb=True)
    gw_fi = _mm(h2, dff_, name="mm_gw_ffn_in", ta=True, out_dtype=BF16)
    dx2, do, red_2 = _rmsmod2_bwd(dh2, x2, dy, o, g_norm2, ada, SC2, GT1)
    dmerged = _mm(do, w_o, name="mm_d_merged", tb=True)
    gw_o = _mm(merged, do, name="mm_gw_out", ta=True, out_dtype=BF16)
    dpa, dpb, dgla, dglb = _gate_bwd(dmerged, pa, pb, proj, gla_col, glb_col)
    dya = _mm(dpa, w_pm, name="mm_d_ya", tb=True)
    gw_pm = _mm(y_a, dpa, name="mm_gw_proj_mla", ta=True, out_dtype=BF16)
    dyb = _mm(dpb, w_ps, name="mm_d_yb", tb=True)
    gw_ps = _mm(y_b, dpb, name="mm_gw_proj_sb", ta=True, out_dtype=BF16)
    dq, dk, dv = _mla_bwd(q, k, kv0, y_a, dya, lse, nh)
    dq_sb, dk_sb, dv_sb = _sb_bwd(proj, q_col, k_col, v_col, dyb, nh)
    dq0, red_qh = _q_prep_bwd(dq, q0, g_qh_p, tabs, nh)
    dkv0, dkpe, red_kh = _k_prep_bwd(dk, dv, kv0, proj, kpe_col, g_kh_p, tabs, nh)
    dcqn = _mm(dq0, w_uq_p, name="mm_d_cqn", tb=True)
    gw_uq_p = _mm(cqn, dq0, name="mm_gw_uq", ta=True, out_dtype=BF16)
    dckvn = _mm(dkv0, w_ukv_p, name="mm_d_ckvn", tb=True)
    gw_ukv_p = _mm(ckvn, dkv0, name="mm_gw_ukv", ta=True, out_dtype=BF16)
    dcq, dckv, red_lat = _latent_norm_bwd(dcqn, dckvn, proj, g_q_latent, g_kv_latent, ql)
    dproj = jnp.concatenate([dcq, dckv, dq_sb.astype(BF16), dk_sb.astype(BF16), dv_sb.astype(BF16),
                             dgla, dglb, dkpe], axis=1)
    dh1 = _mm(dproj, w_in_p, name="mm_d_h1", tb=True)
    gw_in_p = _mm(h1, dproj, name="mm_gw_in", ta=True, out_dtype=BF16, tn=640)
    grad_x, red_1 = _rmsmod1_bwd(dh1, x, dx2, g_norm1, ada, SC1)

    nsb = d_in_p - 2 * ROPE
    gw_in = jnp.concatenate([gw_in_p[:, :kpe0], gw_in_p[:, nsb:nsb + ROPE], gw_in_p[:, kpe0:nsb]], axis=1)
    gw_uq = gw_uq_p.reshape(ql, nh, HEAD_PAD)[:, :, :QK_DIM].reshape(ql, nh * QK_DIM)
    gw_ukv = jnp.concatenate([gw_ukv_p[:, :mlaw].reshape(ql, nh, HEAD), gw_ukv_p[:, mlaw:].reshape(ql, nh, HEAD)],
                             axis=2).reshape(ql, 2 * mlaw)
    full_grads = {"w_in": gw_in, "w_uq": gw_uq, "w_ukv": gw_ukv, "w_proj_mla": gw_pm, "w_proj_sb": gw_ps,
                  "w_out": gw_o, "w_ffn_in": gw_fi, "w_ffn_out": gw_fo}
    pieces = [(_rows_to_pieces if nm in row_sharded else _cols_to_pieces)(full_grads[nm]) for nm in names]

    from_sibling = _sibling_swap(pieces, name="rs_sibling_swap")
    pair = [_pair_sum(p, r, c_idx, name="rs_pair_sum_" + nm) for p, r, nm in zip(pieces, from_sibling, names)]
    from_chips = _chip_exchange(pair, name="rs_chip_exchange")
    reduced = [_chip_sum(s, r, chip_idx, name="rs_chip_sum_" + nm) for s, r, nm in zip(pair, from_chips, names)]
    from_sibling2 = _sibling_swap(reduced, name="rs_sibling_send", whole=True)

    vec_names = ["b_ada", "g_norm1", "g_norm2", "g_q_latent", "g_kv_latent", "g_q_head", "g_k_head"]
    vec_w = dict(b_ada=b_ada, g_norm1=g_norm1, g_norm2=g_norm2, g_q_latent=g_q_latent, g_kv_latent=g_kv_latent,
                 g_q_head=g_q_head, g_k_head=g_k_head)
    vec_m = dict(b_ada=m_b_ada, g_norm1=m_g_norm1, g_norm2=m_g_norm2, g_q_latent=m_g_q_latent,
                 g_kv_latent=m_g_kv_latent, g_q_head=m_g_q_head, g_k_head=m_g_k_head)
    vec_v = dict(b_ada=v_b_ada, g_norm1=v_g_norm1, g_norm2=v_g_norm2, g_q_latent=v_g_q_latent,
                 g_kv_latent=v_g_kv_latent, g_q_head=v_g_q_head, g_k_head=v_g_k_head)
    d_ada = jnp.concatenate([red_1[0:1], red_1[1:2], red_2[3:4], red_2[0:1], red_2[1:2], red_l[0:1]], axis=1)
    vec_parts = dict(b_ada=d_ada, g_norm1=red_1[2:3], g_norm2=red_2[2:3], g_q_latent=red_lat[0:1],
                     g_kv_latent=red_lat[1:2], g_q_head=red_qh[0:1], g_k_head=red_kh[0:1])
    widths = [-(-vec_w[nm].shape[1] // LANE) * LANE for nm in vec_names]
    offs = [sum(widths[:i]) for i in range(len(widths))]
    pack = lambda d: jnp.concatenate([_pad_cols(d[nm][:, :vec_w[nm].shape[1]], wd) for nm, wd in zip(vec_names, widths)], axis=1)
    nvec = sum(widths)
    parts_all = _gather_blocks([jnp.broadcast_to(pack(vec_parts), (8, nvec))], name="gather_vec_grads",
                               in_vmem=True)[0][:, 0, :]
    gvec, dvec, nmvec, nvvec = _adamw_vec(parts_all, pack(vec_w), pack(vec_m), pack(vec_v))
    unpack = lambda a: {nm: a[:, o_:o_ + vec_w[nm].shape[1]] for nm, o_ in zip(vec_names, offs)}
    gvec, dvec, nmvec, nvvec = unpack(gvec), unpack(dvec), unpack(nmvec), unpack(nvvec)

    dada_all = lax.dynamic_slice_in_dim(parts_all[:, :6 * D], chip * n_ada, n_ada, axis=1)
    cact_t = jax.nn.silu(c_all).T
    g_ada, d_ada_w, nm_ada, nv_ada = _adamw_ada(cact_t, dada_all, w_ada[0], m_w_ada[0], v_w_ada[0])

    ms = dict(w_in=m_w_in, w_uq=m_w_uq, w_ukv=m_w_ukv, w_proj_mla=m_w_proj_mla, w_proj_sb=m_w_proj_sb,
              w_out=m_w_out, w_ffn_in=m_w_ffn_in, w_ffn_out=m_w_ffn_out)
    vs = dict(w_in=v_w_in, w_uq=v_w_uq, w_ukv=v_w_ukv, w_proj_mla=v_w_proj_mla, w_proj_sb=v_w_proj_sb,
              w_out=v_w_out, w_ffn_in=v_w_ffn_in, w_ffn_out=v_w_ffn_out)
    G, DL, NM, NV = {}, {}, {}, {}
    for nm, mine, other in zip(names, reduced, from_sibling2):
        g_, d_, m_, v_ = _adamw(mats[nm], mine, other, c_idx, ms[nm][0], vs[nm][0], name="adamw_" + nm)
        G[nm], DL[nm], NM[nm], NV[nm] = g_[None], d_[None], m_[None], v_[None]
    G["w_ada"], DL["w_ada"], NM["w_ada"], NV["w_ada"] = g_ada[None], d_ada_w[None], nm_ada[None], nv_ada[None]
    for nm in vec_names:
        G[nm], DL[nm], NM[nm], NV[nm] = gvec[nm], dvec[nm], nmvec[nm], nvvec[nm]

    order = ["w_ada", "b_ada", "g_norm1", "g_norm2", "w_in", "g_q_latent", "g_kv_latent", "w_uq", "w_ukv",
             "g_q_head", "g_k_head", "w_proj_mla", "w_proj_sb", "w_out", "w_ffn_in", "w_ffn_out"]
    return (loss, grad_x[None], *[G[n] for n in order], *[DL[n] for n in order],
            *[NM[n] for n in order], *[NV[n] for n in order])
```

```python
import functools
import math

import jax
import jax.numpy as jnp
from jax import lax
from jax.experimental import pallas as pl
from jax.experimental.pallas import tpu as pltpu

F32 = jnp.float32
BF16 = jnp.bfloat16
MESH = pl.DeviceIdType.MESH

EPS = 1e-6
ROPE_THETA = 10000.0
NOPE = 128
ROPE = 64
QK_DIM = NOPE + ROPE
HEAD_PAD = 256
HEAD = 128
N_DEV = 8
LANE = 128
VMEM_LIMIT = 48 * 1024 * 1024

ADAM_LR = 0.001
ADAM_B1 = 0.9
ADAM_B2 = 0.999
ADAM_EPS = 1e-08
ADAM_WD = 0.01
ADAM_STEP = 10


def _tile(n, target):
    if n <= target:
        return n
    t = (target // LANE) * LANE
    while t >= LANE:
        if n % t == 0:
            return t
        t -= LANE
    return n


def _row_tile(rows, row_bytes, budget=24 * 1024 * 1024):
    cap = max(8, budget // (2 * row_bytes))
    best = None
    for t in range(8, min(rows, cap) + 1, 8):
        if rows % t == 0:
            best = t
    return best if best is not None else rows


def _params(sem):
    return pltpu.CompilerParams(dimension_semantics=sem, vmem_limit_bytes=VMEM_LIMIT)


def _rows(tm, w, col=0):
    return pl.BlockSpec((tm, w), lambda i: (i, col))


def _vec(w, col=0, rows=1):
    return pl.BlockSpec((rows, w), lambda i: (0, col))


MM_VMEM_BUDGET = 36 * 1024 * 1024


def _mm(a, b, *, name, ta=False, tb=False, out_dtype=F32, a_fn=None, bias=None, tm=1024, tn=1024):
    M = a.shape[1] if ta else a.shape[0]
    K = a.shape[0] if ta else a.shape[1]
    N = b.shape[0] if tb else b.shape[1]
    assert K == (b.shape[1] if tb else b.shape[0]), (a.shape, b.shape, ta, tb)
    tm, tn = _tile(M, tm), _tile(N, tn)
    sa, sb, so = a.dtype.itemsize, b.dtype.itemsize, jnp.dtype(out_dtype).itemsize

    def fits(tk):
        return 2 * tk * (tm * sa + tn * sb) + tm * tn * (2 * so + 4) <= MM_VMEM_BUDGET

    tk = K
    while not fits(tk):
        smaller = _tile(K, tk - LANE)
        if smaller >= tk:
            break
        tk = smaller
    nk = K // tk
    dn = (((0 if ta else 1,), (1 if tb else 0,)), ((), ()))
    b_outer = nk == 1 and a.size * sa * (N // tn) < b.size * sb * (M // tm)

    def body(*refs):
        a_ref, b_ref = refs[:2]
        bias_ref = refs[2] if bias is not None else None
        o_ref = refs[3 if bias is not None else 2]
        av = a_ref[...]
        if a_fn is not None:
            av = a_fn(av.astype(F32))
        part = lax.dot_general(av.astype(BF16), b_ref[...].astype(BF16), dn, preferred_element_type=F32)

        def finish(r):
            if bias is not None:
                r = r + bias_ref[...]
            o_ref[...] = r.astype(o_ref.dtype)

        if nk == 1:
            finish(part)
        else:
            acc_ref = refs[-1]
            k = pl.program_id(2)

            @pl.when(k == 0)
            def _():
                acc_ref[...] = part

            @pl.when(k > 0)
            def _():
                acc_ref[...] += part

            @pl.when(k == nk - 1)
            def _():
                finish(acc_ref[...])

    def ij(g0, g1):
        return (g1, g0) if b_outer else (g0, g1)

    def amap(g0, g1, k):
        i, _ = ij(g0, g1)
        return (k, i) if ta else (i, k)

    def bmap(g0, g1, k):
        _, j = ij(g0, g1)
        return (j, k) if tb else (k, j)

    in_specs = [pl.BlockSpec((tk, tm) if ta else (tm, tk), amap), pl.BlockSpec((tn, tk) if tb else (tk, tn), bmap)]
    args = [a, b]
    if bias is not None:
        in_specs.append(pl.BlockSpec((1, tn), lambda g0, g1, k: (0, ij(g0, g1)[1])))
        args.append(bias)
    grid = (N // tn, M // tm, nk) if b_outer else (M // tm, N // tn, nk)
    return pl.pallas_call(
        body, name=name, grid=grid, in_specs=in_specs,
        out_specs=pl.BlockSpec((tm, tn), lambda g0, g1, k: ij(g0, g1)),
        out_shape=jax.ShapeDtypeStruct((M, N), out_dtype),
        scratch_shapes=[pltpu.VMEM((tm, tn), F32)] if nk > 1 else [],
        compiler_params=_params(("parallel", "parallel", "arbitrary")),
    )(*args)


def _rms_rows(v):
    return lax.rsqrt(jnp.mean(v * v, axis=-1, keepdims=True) + EPS)


def _rmsmod(x, g, ada, sc_col, sh_col, *, name):
    S, D = x.shape
    tm = _tile(S, 256)

    def body(x_ref, g_ref, sc_ref, sh_ref, h_ref):
        xv = x_ref[...]
        h = (xv * _rms_rows(xv) * g_ref[...]) * (1.0 + sc_ref[...]) + sh_ref[...]
        h_ref[...] = h.astype(h_ref.dtype)

    return pl.pallas_call(
        body, name=name, grid=(S // tm,),
        in_specs=[_rows(tm, D), _vec(D), _vec(D, sc_col), _vec(D, sh_col)],
        out_specs=_rows(tm, D), out_shape=jax.ShapeDtypeStruct((S, D), BF16),
        compiler_params=_params(("parallel",)),
    )(x, g, ada, ada)


def _latent_norm(proj, g_q, g_kv, ql):
    S = proj.shape[0]
    tm = _tile(S, 512)

    def body(cq_ref, ckv_ref, gq_ref, gkv_ref, oq_ref, okv_ref):
        cq = cq_ref[...]
        oq_ref[...] = (cq * _rms_rows(cq) * gq_ref[...]).astype(BF16)
        ckv = ckv_ref[...]
        okv_ref[...] = (ckv * _rms_rows(ckv) * gkv_ref[...]).astype(BF16)

    return pl.pallas_call(
        body, name="latent_norm", grid=(S // tm,),
        in_specs=[_rows(tm, ql, 0), _rows(tm, ql, 1), _vec(ql), _vec(ql)],
        out_specs=[_rows(tm, ql), _rows(tm, ql)],
        out_shape=[jax.ShapeDtypeStruct((S, ql), BF16)] * 2,
        compiler_params=_params(("parallel",)),
    )(proj, proj, g_q, g_kv)


def _rope_fwd(y, c, s1, s2):
    return y * c + pltpu.roll(y, ROPE // 2, 1) * s1 + pltpu.roll(y, HEAD_PAD - ROPE // 2, 1) * s2


def _rope_bwd(d, c, s1, s2):
    return d * c + pltpu.roll(d * s1, HEAD_PAD - ROPE // 2, 1) + pltpu.roll(d * s2, ROPE // 2, 1)


def _head_rms(v):
    return lax.rsqrt(jnp.sum(v * v, axis=-1, keepdims=True) * (1.0 / QK_DIM) + EPS)


def _q_prep(q0, g_qh, tabs, nh):
    S = q0.shape[0]
    tm = _tile(S, 256)

    def body(q_ref, g_ref, c_ref, s1_ref, s2_ref, o_ref):
        c, s1, s2, g = c_ref[...], s1_ref[...], s2_ref[...], g_ref[...]
        for h in range(nh):
            sl = slice(h * HEAD_PAD, (h + 1) * HEAD_PAD)
            xs = q_ref[:, sl]
            o_ref[:, sl] = _rope_fwd(xs * _head_rms(xs) * g, c, s1, s2).astype(BF16)

    w = nh * HEAD_PAD
    return pl.pallas_call(
        body, name="mla_q_prep", grid=(S // tm,),
        in_specs=[_rows(tm, w), _vec(HEAD_PAD)] + [_rows(tm, HEAD_PAD)] * 3,
        out_specs=_rows(tm, w), out_shape=jax.ShapeDtypeStruct((S, w), BF16),
        compiler_params=_params(("parallel",)),
    )(q0, g_qh, *tabs)


def _k_prep(kv0, proj, kpe_col, g_kh, tabs, nh):
    S = kv0.shape[0]
    tm = _tile(S, 256)

    def body(kv_ref, kpe_ref, g_ref, c_ref, s1_ref, s2_ref, o_ref):
        c, s1, s2, g = c_ref[...], s1_ref[...], s2_ref[...], g_ref[...]
        kpe = kpe_ref[...]
        for h in range(nh):
            k0 = jnp.concatenate([kv_ref[:, h * HEAD:(h + 1) * HEAD], kpe], axis=1)
            o_ref[:, h * HEAD_PAD:(h + 1) * HEAD_PAD] = _rope_fwd(k0 * _head_rms(k0) * g, c, s1, s2).astype(BF16)

    return pl.pallas_call(
        body, name="mla_k_prep", grid=(S // tm,),
        in_specs=[_rows(tm, nh * HEAD, 0), _rows(tm, LANE, kpe_col), _vec(HEAD_PAD)] + [_rows(tm, HEAD_PAD)] * 3,
        out_specs=_rows(tm, nh * HEAD_PAD), out_shape=jax.ShapeDtypeStruct((S, nh * HEAD_PAD), BF16),
        compiler_params=_params(("parallel",)),
    )(kv0, proj, g_kh, *tabs)


def _gate_merge(pa, pb, proj, gla_col, glb_col):
    S, D = pa.shape
    tm = _tile(S, 256)

    def body(pa_ref, pb_ref, ga_ref, gb_ref, o_ref):
        o_ref[...] = (jax.nn.sigmoid(ga_ref[...]) * pa_ref[...] + jax.nn.sigmoid(gb_ref[...]) * pb_ref[...]).astype(BF16)

    return pl.pallas_call(
        body, name="gate_merge", grid=(S // tm,),
        in_specs=[_rows(tm, D), _rows(tm, D), _rows(tm, D, gla_col), _rows(tm, D, glb_col)],
        out_specs=_rows(tm, D), out_shape=jax.ShapeDtypeStruct((S, D), BF16),
        compiler_params=_params(("parallel",)),
    )(pa, pb, proj, proj)


def _resid_rmsmod(x, o, g, ada, gt_col, sc_col, sh_col):
    S, D = x.shape
    tm = _tile(S, 256)

    def body(x_ref, o_ref, g_ref, gt_ref, sc_ref, sh_ref, x2_ref, h_ref):
        x2 = x_ref[...] + gt_ref[...] * o_ref[...]
        x2_ref[...] = x2
        h_ref[...] = ((x2 * _rms_rows(x2) * g_ref[...]) * (1.0 + sc_ref[...]) + sh_ref[...]).astype(BF16)

    return pl.pallas_call(
        body, name="resid_rmsmod2", grid=(S // tm,),
        in_specs=[_rows(tm, D), _rows(tm, D), _vec(D), _vec(D, gt_col), _vec(D, sc_col), _vec(D, sh_col)],
        out_specs=[_rows(tm, D), _rows(tm, D)],
        out_shape=[jax.ShapeDtypeStruct((S, D), F32), jax.ShapeDtypeStruct((S, D), BF16)],
        compiler_params=_params(("parallel",)),
    )(x, o, g, ada, ada, ada)


def _swiglu(ff, dff_half):
    S = ff.shape[0]
    tm = _tile(S, 256)

    def body(g_ref, u_ref, o_ref):
        o_ref[...] = (jax.nn.silu(g_ref[...].astype(F32)) * u_ref[...].astype(F32)).astype(BF16)

    return pl.pallas_call(
        body, name="swiglu", grid=(S // tm,),
        in_specs=[_rows(tm, dff_half, 0), _rows(tm, dff_half, 1)],
        out_specs=_rows(tm, dff_half), out_shape=jax.ShapeDtypeStruct((S, dff_half), BF16),
        compiler_params=_params(("parallel",)),
    )(ff, ff)


def _loss_head(x2, f, tgt, ada, gt_col):
    S, D = x2.shape
    tm = _tile(S, 256)

    def body(x2_ref, f_ref, t_ref, gt_ref, dy_ref, df_ref, red_ref, loss_ref):
        @pl.when(pl.program_id(0) == 0)
        def _():
            red_ref[...] = jnp.zeros_like(red_ref)
            loss_ref[...] = jnp.zeros_like(loss_ref)

        fv = f_ref[...]
        gt = gt_ref[...]
        err = x2_ref[...] + gt * fv - t_ref[...]
        dy = err * (1.0 / D)
        dy_ref[...] = dy
        df_ref[...] = (dy * gt).astype(BF16)
        red_ref[0:1, :] += jnp.sum(dy * fv, axis=0, keepdims=True)
        loss_ref[...] += (0.5 / D) * jnp.sum(err * err)

    return pl.pallas_call(
        body, name="loss_head", grid=(S // tm,),
        in_specs=[_rows(tm, D), _rows(tm, D), _rows(tm, D), _vec(D, gt_col)],
        out_specs=[_rows(tm, D), _rows(tm, D), _vec(D, rows=8), _vec(LANE, rows=8)],
        out_shape=[jax.ShapeDtypeStruct((S, D), F32), jax.ShapeDtypeStruct((S, D), BF16),
                   jax.ShapeDtypeStruct((8, D), F32), jax.ShapeDtypeStruct((8, LANE), F32)],
        compiler_params=_params(("arbitrary",)),
    )(x2, f, tgt, ada)


def _swiglu_bwd(dact, ff, dff_half):
    S = ff.shape[0]
    tm = _tile(S, 128)

    def body(d_ref, g_ref, u_ref, o_ref):
        d = d_ref[...]
        g = g_ref[...].astype(F32)
        u = u_ref[...].astype(F32)
        sg = jax.nn.sigmoid(g)
        o_ref[:, :dff_half] = (d * u * sg * (1.0 + g * (1.0 - sg))).astype(BF16)
        o_ref[:, dff_half:] = (d * g * sg).astype(BF16)

    return pl.pallas_call(
        body, name="swiglu_bwd", grid=(S // tm,),
        in_specs=[_rows(tm, dff_half), _rows(tm, dff_half, 0), _rows(tm, dff_half, 1)],
        out_specs=_rows(tm, 2 * dff_half), out_shape=jax.ShapeDtypeStruct((S, 2 * dff_half), BF16),
        compiler_params=_params(("parallel",)),
    )(dact, ff, ff)


def _rmsmod2_bwd(dh2, x2, dy, o, g, ada, sc_col, gt_col):
    S, D = x2.shape
    tm = _tile(S, 256)

    def body(dh_ref, x2_ref, dy_ref, o_ref, g_ref, sc_ref, gt_ref, dx_ref, do_ref, red_ref):
        @pl.when(pl.program_id(0) == 0)
        def _():
            red_ref[...] = jnp.zeros_like(red_ref)

        dh = dh_ref[...]
        x2 = x2_ref[...]
        gv = g_ref[...]
        mod = 1.0 + sc_ref[...]
        r = _rms_rows(x2)
        xn = x2 * r
        t = dh * xn
        red_ref[0:1, :] += jnp.sum(dh, axis=0, keepdims=True)
        red_ref[1:2, :] += jnp.sum(t * gv, axis=0, keepdims=True)
        red_ref[2:3, :] += jnp.sum(t * mod, axis=0, keepdims=True)
        dxn = dh * gv * mod
        dx = dy_ref[...] + r * (dxn - xn * jnp.mean(dxn * xn, axis=-1, keepdims=True))
        dx_ref[...] = dx
        red_ref[3:4, :] += jnp.sum(dx * o_ref[...], axis=0, keepdims=True)
        do_ref[...] = (dx * gt_ref[...]).astype(BF16)

    return pl.pallas_call(
        body, name="rmsmod2_bwd", grid=(S // tm,),
        in_specs=[_rows(tm, D)] * 4 + [_vec(D), _vec(D, sc_col), _vec(D, gt_col)],
        out_specs=[_rows(tm, D), _rows(tm, D), _vec(D, rows=8)],
        out_shape=[jax.ShapeDtypeStruct((S, D), F32), jax.ShapeDtypeStruct((S, D), BF16),
                   jax.ShapeDtypeStruct((8, D), F32)],
        compiler_params=_params(("arbitrary",)),
    )(dh2, x2, dy, o, g, ada, ada)


def _rmsmod1_bwd(dh, x, dx2, g, ada, sc_col):
    S, D = x.shape
    tm = _tile(S, 256)

    def body(dh_ref, x_ref, dx2_ref, g_ref, sc_ref, gx_ref, red_ref):
        @pl.when(pl.program_id(0) == 0)
        def _():
            red_ref[...] = jnp.zeros_like(red_ref)

        dh = dh_ref[...]
        xv = x_ref[...]
        gv = g_ref[...]
        mod = 1.0 + sc_ref[...]
        r = _rms_rows(xv)
        xn = xv * r
        t = dh * xn
        red_ref[0:1, :] += jnp.sum(dh, axis=0, keepdims=True)
        red_ref[1:2, :] += jnp.sum(t * gv, axis=0, keepdims=True)
        red_ref[2:3, :] += jnp.sum(t * mod, axis=0, keepdims=True)
        dxn = dh * gv * mod
        gx_ref[...] = dx2_ref[...] + r * (dxn - xn * jnp.mean(dxn * xn, axis=-1, keepdims=True))

    return pl.pallas_call(
        body, name="rmsmod1_bwd", grid=(S // tm,),
        in_specs=[_rows(tm, D)] * 3 + [_vec(D), _vec(D, sc_col)],
        out_specs=[_rows(tm, D), _vec(D, rows=8)],
        out_shape=[jax.ShapeDtypeStruct((S, D), F32), jax.ShapeDtypeStruct((8, D), F32)],
        compiler_params=_params(("arbitrary",)),
    )(dh, x, dx2, g, ada)


def _gate_bwd(dm, pa, pb, proj, gla_col, glb_col):
    S, D = pa.shape
    tm = _tile(S, 256)

    def body(dm_ref, pa_ref, pb_ref, la_ref, lb_ref, dpa_ref, dpb_ref, dla_ref, dlb_ref):
        dm_ = dm_ref[...]
        ga = jax.nn.sigmoid(la_ref[...])
        gb = jax.nn.sigmoid(lb_ref[...])
        dpa_ref[...] = (dm_ * ga).astype(BF16)
        dpb_ref[...] = (dm_ * gb).astype(BF16)
        dla_ref[...] = (dm_ * pa_ref[...] * ga * (1.0 - ga)).astype(BF16)
        dlb_ref[...] = (dm_ * pb_ref[...] * gb * (1.0 - gb)).astype(BF16)

    return pl.pallas_call(
        body, name="gate_bwd", grid=(S // tm,),
        in_specs=[_rows(tm, D)] * 3 + [_rows(tm, D, gla_col), _rows(tm, D, glb_col)],
        out_specs=[_rows(tm, D)] * 4, out_shape=[jax.ShapeDtypeStruct((S, D), BF16)] * 4,
        compiler_params=_params(("parallel",)),
    )(dm, pa, pb, proj, proj)


def _q_prep_bwd(dq, q0, g_qh, tabs, nh):
    S = q0.shape[0]
    tm = _tile(S, 256)

    def body(dq_ref, q_ref, g_ref, c_ref, s1_ref, s2_ref, o_ref, red_ref):
        @pl.when(pl.program_id(0) == 0)
        def _():
            red_ref[...] = jnp.zeros_like(red_ref)

        c, s1, s2, g = c_ref[...], s1_ref[...], s2_ref[...], g_ref[...]
        dg = jnp.zeros((1, HEAD_PAD), F32)
        for h in range(nh):
            sl = slice(h * HEAD_PAD, (h + 1) * HEAD_PAD)
            d1 = _rope_bwd(dq_ref[:, sl], c, s1, s2)
            xs = q_ref[:, sl]
            r = _head_rms(xs)
            qn = xs * r
            dg = dg + jnp.sum(d1 * qn, axis=0, keepdims=True)
            dn = d1 * g
            o_ref[:, sl] = (r * (dn - qn * (jnp.sum(dn * qn, axis=-1, keepdims=True) * (1.0 / QK_DIM)))).astype(BF16)
        red_ref[0:1, :] += dg

    w = nh * HEAD_PAD
    return pl.pallas_call(
        body, name="mla_q_prep_bwd", grid=(S // tm,),
        in_specs=[_rows(tm, w), _rows(tm, w), _vec(HEAD_PAD)] + [_rows(tm, HEAD_PAD)] * 3,
        out_specs=[_rows(tm, w), _vec(HEAD_PAD, rows=8)],
        out_shape=[jax.ShapeDtypeStruct((S, w), BF16), jax.ShapeDtypeStruct((8, HEAD_PAD), F32)],
        compiler_params=_params(("arbitrary",)),
    )(dq, q0, g_qh, *tabs)


def _k_prep_bwd(dk, dv, kv0, proj, kpe_col, g_kh, tabs, nh):
    S = kv0.shape[0]
    tm = _tile(S, 256)
    wv = nh * HEAD

    def body(dk_ref, dv_ref, kv_ref, kpe_ref, g_ref, c_ref, s1_ref, s2_ref, o_ref, dpe_ref, red_ref):
        @pl.when(pl.program_id(0) == 0)
        def _():
            red_ref[...] = jnp.zeros_like(red_ref)

        c, s1, s2, g = c_ref[...], s1_ref[...], s2_ref[...], g_ref[...]
        kpe = kpe_ref[...]
        dg = jnp.zeros((1, HEAD_PAD), F32)
        dpe = jnp.zeros((tm, LANE), F32)
        for h in range(nh):
            d1 = _rope_bwd(dk_ref[:, h * HEAD_PAD:(h + 1) * HEAD_PAD], c, s1, s2)
            k0 = jnp.concatenate([kv_ref[:, h * HEAD:(h + 1) * HEAD], kpe], axis=1)
            r = _head_rms(k0)
            kn = k0 * r
            dg = dg + jnp.sum(d1 * kn, axis=0, keepdims=True)
            dn = d1 * g
            dk0 = r * (dn - kn * (jnp.sum(dn * kn, axis=-1, keepdims=True) * (1.0 / QK_DIM)))
            o_ref[:, h * HEAD:(h + 1) * HEAD] = dk0[:, :HEAD].astype(BF16)
            dpe = dpe + dk0[:, HEAD:]
        o_ref[:, wv:] = dv_ref[...].astype(BF16)
        dpe_ref[...] = dpe.astype(BF16)
        red_ref[0:1, :] += dg

    return pl.pallas_call(
        body, name="mla_k_prep_bwd", grid=(S // tm,),
        in_specs=[_rows(tm, nh * HEAD_PAD), _rows(tm, wv), _rows(tm, wv, 0), _rows(tm, LANE, kpe_col),
                  _vec(HEAD_PAD)] + [_rows(tm, HEAD_PAD)] * 3,
        out_specs=[_rows(tm, 2 * wv), _rows(tm, LANE), _vec(HEAD_PAD, rows=8)],
        out_shape=[jax.ShapeDtypeStruct((S, 2 * wv), BF16), jax.ShapeDtypeStruct((S, LANE), BF16),
                   jax.ShapeDtypeStruct((8, HEAD_PAD), F32)],
        compiler_params=_params(("arbitrary",)),
    )(dk, dv, kv0, proj, g_kh, *tabs)


def _latent_norm_bwd(dcqn, dckvn, proj, g_q, g_kv, ql):
    S = proj.shape[0]
    tm = _tile(S, 512)

    def body(dq_ref, dkv_ref, cq_ref, ckv_ref, gq_ref, gkv_ref, oq_ref, okv_ref, red_ref):
        @pl.when(pl.program_id(0) == 0)
        def _():
            red_ref[...] = jnp.zeros_like(red_ref)

        for row, (d_ref, c_ref, g_ref, o_ref) in enumerate(((dq_ref, cq_ref, gq_ref, oq_ref),
                                                            (dkv_ref, ckv_ref, gkv_ref, okv_ref))):
            d = d_ref[...]
            cv = c_ref[...]
            r = _rms_rows(cv)
            ch = cv * r
            red_ref[row:row + 1, :] += jnp.sum(d * ch, axis=0, keepdims=True)
            dn = d * g_ref[...]
            o_ref[...] = (r * (dn - ch * jnp.mean(dn * ch, axis=-1, keepdims=True))).astype(BF16)

    return pl.pallas_call(
        body, name="latent_norm_bwd", grid=(S // tm,),
        in_specs=[_rows(tm, ql), _rows(tm, ql), _rows(tm, ql, 0), _rows(tm, ql, 1), _vec(ql), _vec(ql)],
        out_specs=[_rows(tm, ql), _rows(tm, ql), _vec(ql, rows=8)],
        out_shape=[jax.ShapeDtypeStruct((S, ql), BF16)] * 2 + [jax.ShapeDtypeStruct((8, ql), F32)],
        compiler_params=_params(("arbitrary",)),
    )(dcqn, dckvn, proj, proj, g_q, g_kv)


NEG = -1e30
ATT_TILE = 512
SB_SUB = 128
_NT = (((1,), (1,)), ((), ()))
_TN = (((0,), (0,)), ((), ()))


def _dot(a, b, dn=(((1,), (0,)), ((), ()))):
    return lax.dot_general(a, b, dn, preferred_element_type=F32)


def _key_rows(kb, t):
    return pl.ds(pl.multiple_of(kb * t, t), t)


def _diag_mask(t, strict):
    r = lax.broadcasted_iota(jnp.int32, (t, t), 0)
    c = lax.broadcasted_iota(jnp.int32, (t, t), 1)
    return c < r if strict else c <= r


def _mla_fwd(q, k, kv0, nh):
    S = q.shape[0]
    t = _tile(S, ATT_TILE)
    scale = QK_DIM ** -0.5

    def body(q_ref, k_ref, v_ref, o_ref, lse_ref):
        i = pl.program_id(1)
        qv = q_ref[...]

        def block(kb, carry, masked):
            m, l, acc = carry
            rows = _key_rows(kb, t)
            s = _dot(qv, k_ref[rows, :], _NT) * scale
            if masked:
                s = jnp.where(_diag_mask(t, False), s, NEG)
            m_new = jnp.maximum(m, jnp.max(s, axis=-1, keepdims=True))
            alpha = jnp.exp(m - m_new)
            p = jnp.exp(s - m_new)
            l = alpha * l + jnp.sum(p, axis=-1, keepdims=True)
            acc = alpha * acc + _dot(p.astype(BF16), v_ref[rows, :].astype(BF16))
            return m_new, l, acc

        init = (jnp.full((t, 1), NEG, F32), jnp.zeros((t, 1), F32), jnp.zeros((t, HEAD), F32))
        carry = lax.fori_loop(0, i, lambda kb, c: block(kb, c, False), init)
        m, l, acc = block(i, carry, True)
        o_ref[...] = acc / l
        lse_ref[...] = m + jnp.log(l)

    return pl.pallas_call(
        body, name="mla_attn_fwd", grid=(nh, S // t),
        in_specs=[pl.BlockSpec((t, HEAD_PAD), lambda h, i: (i, h)),
                  pl.BlockSpec((S, HEAD_PAD), lambda h, i: (0, h)),
                  pl.BlockSpec((S, HEAD), lambda h, i: (0, nh + h))],
        out_specs=[pl.BlockSpec((t, HEAD), lambda h, i: (i, h)),
                   pl.BlockSpec((None, t, 1), lambda h, i: (h, i, 0))],
        out_shape=[jax.ShapeDtypeStruct((S, nh * HEAD), F32), jax.ShapeDtypeStruct((nh, S, 1), F32)],
        compiler_params=_params(("parallel", "arbitrary")),
    )(q, k, kv0)


def _mla_bwd(q, k, kv0, o, do, lse, nh):
    S = q.shape[0]
    t = _tile(S, ATT_TILE)
    scale = QK_DIM ** -0.5

    def body(q_ref, k_ref, v_ref, o_ref, do_ref, lse_ref, dq_ref, dk_ref, dv_ref):
        i = pl.program_id(1)

        @pl.when(i == 0)
        def _():
            dk_ref[...] = jnp.zeros_like(dk_ref)
            dv_ref[...] = jnp.zeros_like(dv_ref)

        qv = q_ref[...]
        dov = do_ref[...]
        delta = jnp.sum(dov * o_ref[...], axis=-1, keepdims=True)
        dob = dov.astype(BF16)
        lse = lse_ref[...]

        def block(kb, dq, masked):
            rows = _key_rows(kb, t)
            ks = k_ref[rows, :]
            vs = v_ref[rows, :].astype(BF16)
            p = jnp.exp(_dot(qv, ks, _NT) * scale - lse)
            if masked:
                p = jnp.where(_diag_mask(t, False), p, 0.0)
            ds = (p * (_dot(dob, vs, _NT) - delta) * scale).astype(BF16)
            dk_ref[rows, :] += _dot(ds, qv, _TN)
            dv_ref[rows, :] += _dot(p.astype(BF16), dob, _TN)
            return dq + _dot(ds, ks)

        dq = lax.fori_loop(0, i, lambda kb, c: block(kb, c, False), jnp.zeros((t, HEAD_PAD), F32))
        dq_ref[...] = block(i, dq, True)

    return pl.pallas_call(
        body, name="mla_attn_bwd", grid=(nh, S // t),
        in_specs=[pl.BlockSpec((t, HEAD_PAD), lambda h, i: (i, h)),
                  pl.BlockSpec((S, HEAD_PAD), lambda h, i: (0, h)),
                  pl.BlockSpec((S, HEAD), lambda h, i: (0, nh + h)),
                  pl.BlockSpec((t, HEAD), lambda h, i: (i, h)),
                  pl.BlockSpec((t, HEAD), lambda h, i: (i, h)),
                  pl.BlockSpec((None, t, 1), lambda h, i: (h, i, 0))],
        out_specs=[pl.BlockSpec((t, HEAD_PAD), lambda h, i: (i, h)),
                   pl.BlockSpec((S, HEAD_PAD), lambda h, i: (0, h)),
                   pl.BlockSpec((S, HEAD), lambda h, i: (0, h))],
        out_shape=[jax.ShapeDtypeStruct((S, nh * HEAD_PAD), F32), jax.ShapeDtypeStruct((S, nh * HEAD_PAD), F32),
                   jax.ShapeDtypeStruct((S, nh * HEAD), F32)],
        compiler_params=_params(("parallel", "arbitrary")),
    )(q, k, kv0, o, do, lse)


def _split_dot(v, tri):
    hi = v.astype(BF16)
    lo = (v - hi.astype(F32)).astype(BF16)
    return _dot(hi, tri) + _dot(lo, tri)


def _tri(n, cmp):
    r = lax.broadcasted_iota(jnp.int32, (n, n), 0)
    c = lax.broadcasted_iota(jnp.int32, (n, n), 1)
    return jnp.where(cmp(r, c), 1.0, 0.0).astype(BF16)


def _sb_block(qv, ks, run, upper, t, scale, masked):
    z = _dot(qv, ks, _NT) * scale
    lb = jnp.minimum(z, 0.0) - jnp.log(1.0 + jnp.exp(-jnp.abs(z)))
    lom = lb - z
    mask = _diag_mask(t, True) if masked else None
    if masked:
        lom = jnp.where(mask, lom, 0.0)
    tails = []
    for sblk in reversed(range(t // SB_SUB)):
        part = lom[:, sblk * SB_SUB:(sblk + 1) * SB_SUB]
        tails.append(_split_dot(part, upper) + run)
        run = run + jnp.sum(part, axis=-1, keepdims=True)
    a = jnp.exp(lb + jnp.concatenate(tails[::-1], axis=1))
    if masked:
        a = jnp.where(mask, a, 0.0)
    return a, lb, mask, run


def _sb_fwd(proj, q_col, k_col, v_col, nh):
    S = proj.shape[0]
    t = _tile(S, ATT_TILE)
    scale = HEAD ** -0.5

    def body(q_ref, k_ref, v_ref, o_ref):
        i = pl.program_id(1)
        qv = q_ref[...].astype(BF16)
        upper = _tri(SB_SUB, lambda j, s: j > s)

        def block(kb, carry, masked):
            run, acc = carry
            rows = _key_rows(kb, t)
            a, _, _, run = _sb_block(qv, k_ref[rows, :].astype(BF16), run, upper, t, scale, masked)
            return run, acc + _dot(a.astype(BF16), v_ref[rows, :].astype(BF16))

        carry = block(i, (jnp.zeros((t, 1), F32), jnp.zeros((t, HEAD), F32)), True)
        o_ref[...] = lax.fori_loop(0, i, lambda j, c: block(i - 1 - j, c, False), carry)[1]

    return pl.pallas_call(
        body, name="sb_attn_fwd", grid=(nh, S // t),
        in_specs=[pl.BlockSpec((t, HEAD), lambda h, i: (i, q_col + h)),
                  pl.BlockSpec((S, HEAD), lambda h, i: (0, k_col + h)),
                  pl.BlockSpec((S, HEAD), lambda h, i: (0, v_col + h))],
        out_specs=pl.BlockSpec((t, HEAD), lambda h, i: (i, h)),
        out_shape=jax.ShapeDtypeStruct((S, nh * HEAD), F32),
        compiler_params=_params(("parallel", "arbitrary")),
    )(proj, proj, proj)


def _sb_bwd(proj, q_col, k_col, v_col, dy, nh):
    S = proj.shape[0]
    t = _tile(S, ATT_TILE)
    scale = HEAD ** -0.5

    def body(q_ref, k_ref, v_ref, dy_ref, dq_ref, dk_ref, dv_ref, run_ref):
        i = pl.program_id(1)

        @pl.when(i == 0)
        def _():
            dk_ref[...] = jnp.zeros_like(dk_ref)
            dv_ref[...] = jnp.zeros_like(dv_ref)

        qv = q_ref[...].astype(BF16)
        dyb = dy_ref[...].astype(BF16)
        upper = _tri(SB_SUB, lambda j, s: j > s)
        before = _tri(SB_SUB, lambda s, j: s < j)

        def suffix(kb, run, masked):
            z = _dot(qv, k_ref[_key_rows(kb, t), :].astype(BF16), _NT) * scale
            lom = jnp.minimum(z, 0.0) - jnp.log(1.0 + jnp.exp(-jnp.abs(z))) - z
            if masked:
                lom = jnp.where(_diag_mask(t, True), lom, 0.0)
            run_ref[kb] = jnp.broadcast_to(run, (t, LANE))
            return run + jnp.sum(lom, axis=-1, keepdims=True)

        run0 = suffix(i, jnp.zeros((t, 1), F32), True)
        lax.fori_loop(0, i, lambda j, r: suffix(i - 1 - j, r, False), run0)

        def block(kb, carry, masked):
            prefix, dq = carry
            rows = _key_rows(kb, t)
            ks = k_ref[rows, :].astype(BF16)
            vs = v_ref[rows, :].astype(BF16)
            a, lb, mask, _ = _sb_block(qv, ks, run_ref[kb][:, 0:1], upper, t, scale, masked)
            dl = a * _dot(dyb, vs, _NT)
            lefts = []
            for sblk in range(t // SB_SUB):
                part = dl[:, sblk * SB_SUB:(sblk + 1) * SB_SUB]
                lefts.append(_dot(part.astype(BF16), before) + prefix)
                prefix = prefix + jnp.sum(part, axis=-1, keepdims=True)
            beta = jnp.exp(lb)
            dz = dl * (1.0 - beta) - beta * jnp.concatenate(lefts, axis=1)
            if masked:
                dz = jnp.where(mask, dz, 0.0)
            dz = (dz * scale).astype(BF16)
            dk_ref[rows, :] += _dot(dz, qv, _TN)
            dv_ref[rows, :] += _dot(a.astype(BF16), dyb, _TN)
            return prefix, dq + _dot(dz, ks)

        carry = lax.fori_loop(0, i, lambda kb, c: block(kb, c, False),
                              (jnp.zeros((t, 1), F32), jnp.zeros((t, HEAD), F32)))
        dq_ref[...] = block(i, carry, True)[1]

    full = pl.BlockSpec((S, HEAD), lambda h, i: (0, h))
    tile = pl.BlockSpec((t, HEAD), lambda h, i: (i, h))
    return pl.pallas_call(
        body, name="sb_attn_bwd", grid=(nh, S // t),
        in_specs=[pl.BlockSpec((t, HEAD), lambda h, i: (i, q_col + h)),
                  pl.BlockSpec((S, HEAD), lambda h, i: (0, k_col + h)),
                  pl.BlockSpec((S, HEAD), lambda h, i: (0, v_col + h)), tile],
        out_specs=[tile, full, full],
        out_shape=[jax.ShapeDtypeStruct((S, nh * HEAD), F32)] * 3,
        scratch_shapes=[pltpu.VMEM((S // t, t, LANE), F32)],
        compiler_params=_params(("parallel", "arbitrary")),
    )(proj, proj, proj, dy)


def _place():
    return lax.axis_index("x"), lax.axis_index("y"), lax.axis_index("c")


def _other_chips(x, y):
    return [(1 - x, y), (x, 1 - y), (1 - x, 1 - y)]


def _dev_index(p):
    return 4 * p[0] + 2 * p[1] + p[2]


def _gather_blocks(blocks, *, name, in_vmem):
    n = len(blocks)
    per = 7

    def body(*refs):
        ins, outs = refs[:n], refs[n:2 * n]
        send_sems, recv_sems, local_sems = refs[2 * n:]
        x, y, c = _place()
        me, sibling = (x, y, c), (x, y, 1 - c)
        chips = _other_chips(x, y)

        def slot(a, p):
            return outs[a].at[_dev_index(p)]

        def copy(a, k, block, to, src=None):
            return pltpu.make_async_remote_copy(
                src_ref=slot(a, block) if src is None else src, dst_ref=slot(a, block),
                send_sem=send_sems.at[a * per + k], recv_sem=recv_sems.at[a * per + k],
                device_id=to, device_id_type=MESH)

        mine = [pltpu.make_async_copy(ins[a], slot(a, me), local_sems.at[a]) for a in range(n)]
        for cp in mine:
            cp.start()
        first = []
        for a in range(n):
            first.append(copy(a, 0, me, sibling, src=ins[a]))
            first += [copy(a, 1 + j, me, (*chip, c), src=ins[a]) for j, chip in enumerate(chips)]
        for cp in first:
            cp.start()
        passed = []
        for a in range(n):
            for j, chip in enumerate(chips):
                copy(a, 1 + j, (*chip, c), me).wait_recv()
                cp = copy(a, 4 + j, (*chip, c), sibling)
                cp.start()
                passed.append(cp)
        for a in range(n):
            copy(a, 0, sibling, me).wait_recv()
            for j, chip in enumerate(chips):
                copy(a, 4 + j, (*chip, 1 - c), me).wait_recv()
        for cp in first + passed:
            cp.wait_send()
        for cp in mine:
            cp.wait()

    space = pltpu.VMEM if in_vmem else pl.ANY
    spec = pl.BlockSpec(memory_space=space)
    outs = pl.pallas_call(
        body, name=name, in_specs=[spec] * n, out_specs=[spec] * n,
        out_shape=[jax.ShapeDtypeStruct((N_DEV,) + b.shape, b.dtype) for b in blocks],
        scratch_shapes=[pltpu.SemaphoreType.DMA((n * per,)), pltpu.SemaphoreType.DMA((n * per,)),
                        pltpu.SemaphoreType.DMA((n,))],
        compiler_params=pltpu.CompilerParams(vmem_limit_bytes=VMEM_LIMIT),
    )(*blocks)
    return list(outs)


def _sibling_swap(arrs, *, name, whole=False):
    n = len(arrs)

    def body(*refs):
        ins, outs = refs[:n], refs[n:2 * n]
        send_sems, recv_sems = refs[2 * n:]
        x, y, c = _place()
        copies = [pltpu.make_async_remote_copy(
            src_ref=ins[a] if whole else ins[a].at[1 - c], dst_ref=outs[a],
            send_sem=send_sems.at[a], recv_sem=recv_sems.at[a],
            device_id=(x, y, 1 - c), device_id_type=MESH) for a in range(n)]
        for cp in copies:
            cp.start()
        for cp in copies:
            cp.wait()

    spec = pl.BlockSpec(memory_space=pl.ANY)
    return list(pl.pallas_call(
        body, name=name, in_specs=[spec] * n, out_specs=[spec] * n,
        out_shape=[jax.ShapeDtypeStruct(a.shape if whole else a.shape[1:], a.dtype) for a in arrs],
        scratch_shapes=[pltpu.SemaphoreType.DMA((n,)), pltpu.SemaphoreType.DMA((n,))],
    )(*arrs))


def _chip_exchange(arrs, *, name):
    n = len(arrs)

    def body(*refs):
        ins, outs = refs[:n], refs[n:2 * n]
        send_sems, recv_sems = refs[2 * n:]
        x, y, c = _place()
        copies = []
        for a in range(n):
            for j, (px, py) in enumerate(_other_chips(x, y)):
                copies.append(pltpu.make_async_remote_copy(
                    src_ref=ins[a].at[2 * px + py], dst_ref=outs[a].at[j],
                    send_sem=send_sems.at[3 * a + j], recv_sem=recv_sems.at[3 * a + j],
                    device_id=(px, py, c), device_id_type=MESH))
        for cp in copies:
            cp.start()
        for cp in copies:
            cp.wait()

    spec = pl.BlockSpec(memory_space=pl.ANY)
    return list(pl.pallas_call(
        body, name=name, in_specs=[spec] * n, out_specs=[spec] * n,
        out_shape=[jax.ShapeDtypeStruct((3,) + a.shape[1:], a.dtype) for a in arrs],
        scratch_shapes=[pltpu.SemaphoreType.DMA((3 * n,)), pltpu.SemaphoreType.DMA((3 * n,))],
    )(*arrs))


_HBM = pl.BlockSpec(memory_space=pltpu.HBM)
_SEM = pl.BlockSpec(memory_space=pltpu.SEMAPHORE)
_EFFECT = pltpu.SideEffectType.DATAFLOW_SIDE_EFFECTING


def _in_hbm(a):
    return pltpu.with_memory_space_constraint(a, pltpu.HBM)


def _split_copies(srcs, lands, send_sems, recv_sems, plan):
    x, y, c = _place()
    copies = []
    for a, (src, land) in enumerate(zip(srcs, lands)):
        steps = plan(x, y, c)
        for k, (pick, slot, to) in enumerate(steps):
            copies.append(pltpu.make_async_remote_copy(
                src_ref=pick(src), dst_ref=slot(land), send_sem=send_sems.at[a * len(steps) + k],
                recv_sem=recv_sems.at[a * len(steps) + k], device_id=to, device_id_type=MESH))
    return copies


def _split_start(srcs, land_shapes, plan, per, *, name):
    n = len(srcs)

    def body(*refs):
        send_sems, recv_sems = refs[2 * n], refs[2 * n + 1]
        for cp in _split_copies(refs[:n], refs[n:2 * n], send_sems, recv_sems, plan):
            cp.start()
        token = refs[-1]
        token[...] = jnp.zeros_like(token)

    lands = [_in_hbm(lax.empty(s.shape, s.dtype)) for s in land_shapes]
    outs = pl.pallas_call(
        body, name=name,
        out_shape=(pltpu.SemaphoreType.DMA((n * per,)), pltpu.SemaphoreType.DMA((n * per,)),
                   *[pltpu.HBM(s.shape, s.dtype) for s in srcs], *[pltpu.HBM(s.shape, s.dtype) for s in land_shapes],
                   jax.ShapeDtypeStruct((8, LANE), F32)),
        in_specs=[_HBM] * (2 * n),
        out_specs=(_SEM, _SEM, *[_HBM] * (2 * n), pl.BlockSpec(memory_space=pltpu.VMEM)),
        input_output_aliases={i: 2 + i for i in range(2 * n)},
        compiler_params=pltpu.CompilerParams(has_side_effects=_EFFECT),
    )(*[_in_hbm(s) for s in srcs], *lands)
    return outs[0], outs[1], list(outs[2:2 + n]), list(outs[2 + n:2 + 2 * n]), outs[-1]


def _split_wait(send_sems, recv_sems, srcs, lands, after, plan, *, name):
    n = len(srcs)

    def body(*refs):
        for cp in _split_copies(refs[:n], refs[n:2 * n], refs[2 * n], refs[2 * n + 1], plan):
            cp.wait_send()
            cp.wait_recv()

    outs = pl.pallas_call(
        body, name=name,
        out_shape=(*[pltpu.HBM(s.shape, s.dtype) for s in srcs], *[pltpu.HBM(s.shape, s.dtype) for s in lands]),
        in_specs=[_HBM] * (2 * n) + [_SEM, _SEM, pl.BlockSpec(memory_space=pl.ANY)],
        out_specs=tuple([_HBM] * (2 * n)),
        input_output_aliases={i: i for i in range(2 * n)},
        compiler_params=pltpu.CompilerParams(has_side_effects=_EFFECT),
    )(*srcs, *lands, send_sems, recv_sems, after)
    return list(outs[:n]), list(outs[n:])


def _gather_plan(x, y, c):
    slot = lambda land: land.at[_dev_index((x, y, c))]
    whole = lambda src: src
    return [(whole, slot, (x, y, 1 - c))] + [(whole, slot, (px, py, c)) for px, py in _other_chips(x, y)]


def _exchange_plan(x, y, c):
    return [(lambda src, k=2 * px + py: src.at[k], lambda land, j=j: land.at[j], (px, py, c))
            for j, (px, py) in enumerate(_other_chips(x, y))]


def _gather_forward(lands, halves, *, name):
    n = len(lands)

    def body(*refs):
        lands_in, own, outs = refs[:n], refs[n:2 * n], refs[2 * n:3 * n]
        send_sems, recv_sems, local_sems = refs[3 * n:]
        x, y, c = _place()
        mine = [pltpu.make_async_copy(own[a], outs[a].at[_dev_index((x, y, c))], local_sems.at[a]) for a in range(n)]
        copies = []
        for a in range(n):
            for j, (px, py) in enumerate(_other_chips(x, y)):
                copies.append((pltpu.make_async_remote_copy(
                    src_ref=lands_in[a].at[_dev_index((px, py, c))], dst_ref=outs[a].at[_dev_index((px, py, c))],
                    send_sem=send_sems.at[3 * a + j], recv_sem=recv_sems.at[3 * a + j],
                    device_id=(x, y, 1 - c), device_id_type=MESH), a, j, (px, py)))
        for cp in mine:
            cp.start()
        for cp, _, _, _ in copies:
            cp.start()
        for cp, a, j, (px, py) in copies:
            cp.wait_send()
            pltpu.make_async_remote_copy(
                src_ref=lands_in[a].at[_dev_index((px, py, 1 - c))], dst_ref=outs[a].at[_dev_index((px, py, 1 - c))],
                send_sem=send_sems.at[3 * a + j], recv_sem=recv_sems.at[3 * a + j],
                device_id=(x, y, 1 - c), device_id_type=MESH).wait_recv()
        for cp in mine:
            cp.wait()

    spec = pl.BlockSpec(memory_space=pl.ANY)
    return list(pl.pallas_call(
        body, name=name, in_specs=[spec] * (2 * n), out_specs=[spec] * n,
        out_shape=[jax.ShapeDtypeStruct(a.shape, a.dtype) for a in lands],
        input_output_aliases={a: a for a in range(n)},
        scratch_shapes=[pltpu.SemaphoreType.DMA((3 * n,)), pltpu.SemaphoreType.DMA((3 * n,)),
                        pltpu.SemaphoreType.DMA((n,))],
    )(*lands, *halves))


def _flat2(a, lead):
    return a.reshape(a.shape[:lead] + (-1, a.shape[-1]))


def _pair_sum(g, recv, c_idx, *, name):
    _, nchip, r, w = g.shape
    tm = _tile(r, 256) if r % 8 == 0 else r

    def body(c_ref, g_ref, r_ref, o_ref):
        o_ref[...] = (g_ref[...].astype(F32) + r_ref[...].astype(F32)).astype(o_ref.dtype)

    return pl.pallas_call(
        body, name=name,
        grid_spec=pltpu.PrefetchScalarGridSpec(
            num_scalar_prefetch=1, grid=(nchip, r // tm),
            in_specs=[pl.BlockSpec((None, None, tm, w), lambda k, i, c_ref: (c_ref[0], k, i, 0)),
                      pl.BlockSpec((None, tm, w), lambda k, i, c_ref: (k, i, 0))],
            out_specs=pl.BlockSpec((None, tm, w), lambda k, i, c_ref: (k, i, 0))),
        out_shape=jax.ShapeDtypeStruct((nchip, r, w), BF16),
        compiler_params=_params(("parallel", "parallel")),
    )(c_idx, g, recv)


def _chip_sum(s1, recv, chip_idx, *, name):
    _, r, w = s1.shape
    tm = _tile(r, 256) if r % 8 == 0 else r

    def body(k_ref, s_ref, r_ref, o_ref):
        acc = s_ref[...].astype(F32)
        for j in range(3):
            acc = acc + r_ref[j].astype(F32)
        o_ref[...] = acc

    return pl.pallas_call(
        body, name=name,
        grid_spec=pltpu.PrefetchScalarGridSpec(
            num_scalar_prefetch=1, grid=(r // tm,),
            in_specs=[pl.BlockSpec((None, tm, w), lambda i, k_ref: (k_ref[0], i, 0)),
                      pl.BlockSpec((3, tm, w), lambda i, k_ref: (0, i, 0))],
            out_specs=pl.BlockSpec((tm, w), lambda i, k_ref: (i, 0))),
        out_shape=jax.ShapeDtypeStruct((r, w), F32),
        compiler_params=_params(("parallel",)),
    )(chip_idx, s1, recv)


def _adam_math(w, g, m, v):
    m = ADAM_B1 * m + (1.0 - ADAM_B1) * g
    v = ADAM_B2 * v + (1.0 - ADAM_B2) * (g * g)
    m_hat = m / (1.0 - ADAM_B1 ** ADAM_STEP)
    v_hat = v / (1.0 - ADAM_B2 ** ADAM_STEP)
    delta = -ADAM_LR * (m_hat / (jnp.sqrt(v_hat) + ADAM_EPS) + ADAM_WD * w)
    return delta, m, v


def _adamw(w, mine, other, c_idx, m, v, *, name):
    r, cw = w.shape
    hr = r // 2
    tm = _row_tile(hr, 9 * cw * 4)

    def body(c_ref, w_ref, a_ref, b_ref, m_ref, v_ref, g_ref, d_ref, nm_ref, nv_ref):
        g = jnp.where(pl.program_id(0) == c_ref[0], a_ref[...], b_ref[...])
        g_ref[...] = g
        d_ref[...], nm_ref[...], nv_ref[...] = _adam_math(w_ref[...], g, m_ref[...], v_ref[...])

    full = pl.BlockSpec((None, tm, cw), lambda h, i, c_ref: (h, i, 0))
    half = pl.BlockSpec((tm, cw), lambda h, i, c_ref: (i, 0))
    outs = pl.pallas_call(
        body, name=name,
        grid_spec=pltpu.PrefetchScalarGridSpec(
            num_scalar_prefetch=1, grid=(2, hr // tm),
            in_specs=[full, half, half, full, full], out_specs=[full] * 4),
        out_shape=[jax.ShapeDtypeStruct((2, hr, cw), F32)] * 4,
        compiler_params=_params(("parallel", "parallel")),
    )(c_idx, w.reshape(2, hr, cw), mine, other, m.reshape(2, hr, cw), v.reshape(2, hr, cw))
    return [o.reshape(r, cw) for o in outs]


def _adamw_ada(cact_t, dada, w, m, v):
    r, cw = w.shape
    nb = cact_t.shape[1]
    tm = _tile(r, 256)
    tn = _tile(cw, 1024)

    def body(a_ref, d_ref, w_ref, m_ref, v_ref, g_ref, dl_ref, nm_ref, nv_ref):
        a = a_ref[...]
        d = d_ref[...]
        g = a[:, 0:1] * d[0:1, :]
        for b in range(1, nb):
            g = g + a[:, b:b + 1] * d[b:b + 1, :]
        g_ref[...] = g
        dl_ref[...], nm_ref[...], nv_ref[...] = _adam_math(w_ref[...], g, m_ref[...], v_ref[...])

    blk = pl.BlockSpec((tm, tn), lambda i, j: (i, j))
    return pl.pallas_call(
        body, name="adamw_ada", grid=(r // tm, cw // tn),
        in_specs=[pl.BlockSpec((tm, nb), lambda i, j: (i, 0)), pl.BlockSpec((nb, tn), lambda i, j: (0, j)), blk, blk, blk],
        out_specs=[blk] * 4, out_shape=[jax.ShapeDtypeStruct((r, cw), F32)] * 4,
        compiler_params=_params(("parallel", "parallel")),
    )(cact_t, dada, w, m, v)


def _adamw_vec(parts, w, m, v):
    n = w.shape[1]

    def body(p_ref, w_ref, m_ref, v_ref, g_ref, d_ref, nm_ref, nv_ref):
        p = p_ref[...]
        g = p[0:1, :]
        for b in range(1, N_DEV):
            g = g + p[b:b + 1, :]
        g_ref[...] = g
        d_ref[...], nm_ref[...], nv_ref[...] = _adam_math(w_ref[...], g, m_ref[...], v_ref[...])

    return pl.pallas_call(
        body, name="adamw_vec", out_shape=[jax.ShapeDtypeStruct((1, n), F32)] * 4,
        compiler_params=pltpu.CompilerParams(vmem_limit_bytes=VMEM_LIMIT),
    )(parts, w, m, v)


def _cols_from_chips(g8, rows):
    cs = g8.shape[-1]
    return g8.reshape(4, rows, cs).transpose(1, 0, 2).reshape(rows, 4 * cs)


def _cols_to_pieces(g):
    rows, c4 = g.shape
    return g.reshape(2, rows // 2, 4, c4 // 4).transpose(0, 2, 1, 3)


def _rows_to_pieces(g):
    r4, cols = g.shape
    return g.reshape(4, 2, r4 // 8, cols).transpose(1, 0, 2, 3)


def _pad_cols(a, w):
    return jnp.pad(a, ((0, 0), (0, w - a.shape[1])))


def kernel(x, c, positions, w_ada, b_ada, g_norm1, g_norm2, w_in, g_q_latent, g_kv_latent, w_uq, w_ukv, g_q_head, g_k_head, w_proj_mla, w_proj_sb, w_out, w_ffn_in, w_ffn_out, loss_target, m_w_ada, m_b_ada, m_g_norm1, m_g_norm2, m_w_in, m_g_q_latent, m_g_kv_latent, m_w_uq, m_w_ukv, m_g_q_head, m_g_k_head, m_w_proj_mla, m_w_proj_sb, m_w_out, m_w_ffn_in, m_w_ffn_out, v_w_ada, v_b_ada, v_g_norm1, v_g_norm2, v_w_in, v_g_q_latent, v_g_kv_latent, v_w_uq, v_w_ukv, v_g_q_head, v_g_k_head, v_w_proj_mla, v_w_proj_sb, v_w_out, v_w_ffn_in, v_w_ffn_out):
    xi, yi, ci = _place()
    chip = 2 * xi + yi
    dev = 2 * chip + ci
    c_idx = jnp.reshape(ci, (1,)).astype(jnp.int32)
    chip_idx = jnp.reshape(chip, (1,)).astype(jnp.int32)

    x = x[0]
    tgt = loss_target[0]
    S, D = x.shape
    ql = g_q_latent.shape[1]
    assert g_kv_latent.shape[1] == ql
    mlaw = w_proj_mla.shape[1]
    nh = mlaw // HEAD
    sbw = w_proj_sb.shape[1]
    assert sbw == mlaw
    dff = w_ffn_out.shape[1] * 4
    d_in = 2 * ql + ROPE + 3 * sbw + 2 * D
    d_in_p = d_in + ROPE
    q_col = (2 * ql) // HEAD
    k_col = q_col + nh
    v_col = k_col + nh
    gla_col = (2 * ql + 3 * sbw) // D
    glb_col = gla_col + 1
    kpe_col = (d_in - ROPE) // LANE
    assert (2 * ql + 3 * sbw) % D == 0 and (d_in - ROPE) % LANE == 0

    mats = {"w_in": w_in[0], "w_uq": w_uq[0], "w_ukv": w_ukv[0], "w_proj_mla": w_proj_mla[0],
            "w_proj_sb": w_proj_sb[0], "w_out": w_out[0], "w_ffn_in": w_ffn_in[0], "w_ffn_out": w_ffn_out[0]}
    names = list(mats)
    row_sharded = {"w_out", "w_ffn_out"}

    halves = []
    for nm in names:
        w = mats[nm]
        hr = w.shape[0] // 2
        halves.append(lax.dynamic_slice_in_dim(w, ci * hr, hr, axis=0).astype(BF16))
    half_of = dict(zip(names, halves))
    early = ["w_in", "w_uq", "w_ukv"]
    late = ["w_proj_mla", "w_proj_sb", "w_out", "w_ffn_in", "w_ffn_out"]
    gathered = dict(zip(early, _gather_blocks([half_of[nm] for nm in early], name="gather_weights", in_vmem=False)))
    late_halves = [half_of[nm] for nm in late]
    late_halves[0] = late_halves[0] + (gathered["w_uq"][0, 0:1, 0:1] * 0).astype(BF16)
    late_send, late_recv, late_srcs, late_lands, late_token = _split_start(
        late_halves, [jax.ShapeDtypeStruct((N_DEV,) + h.shape, h.dtype) for h in late_halves], _gather_plan, 4,
        name="gather_late_start")

    def full_cols(nm):
        return _cols_from_chips(gathered[nm], mats[nm].shape[0])

    w_in_f = full_cols("w_in")
    kpe0 = 2 * ql
    w_in_p = jnp.concatenate([w_in_f[:, :kpe0], w_in_f[:, kpe0 + ROPE:], w_in_f[:, kpe0:kpe0 + ROPE],
                              jnp.zeros((D, ROPE), BF16)], axis=1)
    w_uq_p = jnp.pad(full_cols("w_uq").reshape(ql, nh, QK_DIM), ((0, 0), (0, 0), (0, HEAD_PAD - QK_DIM))
                     ).reshape(ql, nh * HEAD_PAD)
    w_ukv4 = full_cols("w_ukv").reshape(ql, nh, 2 * HEAD)
    w_ukv_p = jnp.concatenate([w_ukv4[:, :, :HEAD].reshape(ql, mlaw), w_ukv4[:, :, HEAD:].reshape(ql, mlaw)], axis=1)

    c_all = _gather_blocks([jnp.broadcast_to(c, (8, D))], name="gather_cond", in_vmem=True)[0][:, 0, :]
    n_ada = w_ada.shape[2]
    b_shard = lax.dynamic_slice_in_dim(b_ada, chip * n_ada, n_ada, axis=1)
    ada_shard = _mm(c_all, w_ada[0], name="ada_proj", a_fn=jax.nn.silu, bias=b_shard)
    ada_all = _gather_blocks([ada_shard], name="gather_ada", in_vmem=True)[0]
    ada_rows = lax.dynamic_index_in_dim(ada_all, dev, axis=1, keepdims=False)
    ada = ada_rows[0::2].reshape(1, 4 * n_ada) + late_token[0:1, 0:1]
    SH1, SC1, GT1, SH2, SC2, GT2 = range(6)

    half = ROPE // 2
    freqs = ROPE_THETA ** (-jnp.arange(half, dtype=F32) / half)
    ang = positions[0].astype(F32)[:, None] * freqs
    cos, sin = jnp.cos(ang), jnp.sin(ang)
    one = jnp.ones((S, NOPE), F32)
    zero = jnp.zeros((S, NOPE), F32)
    zh = jnp.zeros((S, half), F32)
    tabs = (jnp.concatenate([one, cos, cos, one[:, :HEAD_PAD - QK_DIM]], axis=1),
            jnp.concatenate([zero, zh, sin, zero[:, :HEAD_PAD - QK_DIM]], axis=1),
            jnp.concatenate([zero, -sin, zh, zero[:, :HEAD_PAD - QK_DIM]], axis=1))
    g_qh_p = _pad_cols(g_q_head, HEAD_PAD)
    g_kh_p = _pad_cols(g_k_head, HEAD_PAD)

    h1 = _rmsmod(x, g_norm1, ada, SC1, SH1, name="rmsmod1")
    proj = _mm(h1, w_in_p, name="mm_proj", tn=640)
    cqn, ckvn = _latent_norm(proj, g_q_latent, g_kv_latent, ql)
    q0 = _mm(cqn, w_uq_p, name="mm_q_up")
    kv0 = _mm(ckvn, w_ukv_p, name="mm_kv_up")
    q = _q_prep(q0, g_qh_p, tabs, nh)
    k = _k_prep(kv0, proj, kpe_col, g_kh_p, tabs, nh)
    y_a, lse = _mla_fwd(q, k, kv0, nh)
    y_b = _sb_fwd(proj, q_col, k_col, v_col, nh)
    late_srcs, late_lands = _split_wait(late_send, late_recv, late_srcs, late_lands, y_b, _gather_plan,
                                        name="gather_late_wait")
    gathered.update(zip(late, _gather_forward(late_lands, late_srcs, name="gather_late_forward")))
    w_pm = full_cols("w_proj_mla")
    w_ps = full_cols("w_proj_sb")
    w_o = gathered["w_out"].reshape(D, D)
    w_fi = full_cols("w_ffn_in")
    w_fo = gathered["w_ffn_out"].reshape(dff, D)
    pa = _mm(y_a, w_pm, name="mm_proj_mla")
    pb = _mm(y_b, w_ps, name="mm_proj_sb")
    merged = _gate_merge(pa, pb, proj, gla_col, glb_col)
    o = _mm(merged, w_o, name="mm_out")
    x2, h2 = _resid_rmsmod(x, o, g_norm2, ada, GT1, SC2, SH2)
    ff = _mm(h2, w_fi, name="mm_ffn_in", out_dtype=BF16)
    act = _swiglu(ff, dff)
    f = _mm(act, w_fo, name="mm_ffn_out")
    dy, df, red_l, loss_p = _loss_head(x2, f, tgt, ada, GT2)
    loss = lax.psum(loss_p[0, 0], ("x", "y", "c"))

    dact = _mm(df, w_fo, name="mm_d_act", tb=True)
    gw_fo = _mm(act, df, name="mm_gw_ffn_out", ta=True, out_dtype=BF16)
    dff_ = _swiglu_bwd(dact, ff, dff)
    dh2 = _mm(dff_, w_fi, name="mm_d_h2", tb=True)
    gw_fi = _mm(h2, dff_, name="mm_gw_ffn_in", ta=True, out_dtype=BF16)

    def pair_sums(nms, grads, tag):
        pcs = [(_rows_to_pieces if nm in row_sharded else _cols_to_pieces)(g) for nm, g in zip(nms, grads)]
        got = _sibling_swap(pcs, name="rs_sibling_swap_" + tag)
        return [_pair_sum(p, r, c_idx, name="rs_pair_sum_" + nm) for p, r, nm in zip(pcs, got, nms)]

    ffn = ["w_ffn_in", "w_ffn_out"]
    ffn_pair = pair_sums(ffn, [gw_fi, gw_fo], "ffn")
    ffn_send, ffn_recv, ffn_pair, ffn_lands, ffn_token = _split_start(
        ffn_pair, [jax.ShapeDtypeStruct((3,) + p.shape[1:], p.dtype) for p in ffn_pair], _exchange_plan, 3,
        name="rs_exchange_ffn_start")
    ada = ada + ffn_token[0:1, 0:1]
    dx2, do, red_2 = _rmsmod2_bwd(dh2, x2, dy, o, g_norm2, ada, SC2, GT1)
    dmerged = _mm(do, w_o, name="mm_d_merged", tb=True)
    gw_o = _mm(merged, do, name="mm_gw_out", ta=True, out_dtype=BF16)
    dpa, dpb, dgla, dglb = _gate_bwd(dmerged, pa, pb, proj, gla_col, glb_col)
    dya = _mm(dpa, w_pm, name="mm_d_ya", tb=True)
    gw_pm = _mm(y_a, dpa, name="mm_gw_proj_mla", ta=True, out_dtype=BF16)
    dyb = _mm(dpb, w_ps, name="mm_d_yb", tb=True)
    gw_ps = _mm(y_b, dpb, name="mm_gw_proj_sb", ta=True, out_dtype=BF16)
    dq, dk, dv = _mla_bwd(q, k, kv0, y_a, dya, lse, nh)
    dq_sb, dk_sb, dv_sb = _sb_bwd(proj, q_col, k_col, v_col, dyb, nh)
    dq0, red_qh = _q_prep_bwd(dq, q0, g_qh_p, tabs, nh)
    dkv0, dkpe, red_kh = _k_prep_bwd(dk, dv, kv0, proj, kpe_col, g_kh_p, tabs, nh)
    dcqn = _mm(dq0, w_uq_p, name="mm_d_cqn", tb=True)
    gw_uq_p = _mm(cqn, dq0, name="mm_gw_uq", ta=True, out_dtype=BF16)
    dckvn = _mm(dkv0, w_ukv_p, name="mm_d_ckvn", tb=True)
    gw_ukv_p = _mm(ckvn, dkv0, name="mm_gw_ukv", ta=True, out_dtype=BF16)
    dcq, dckv, red_lat = _latent_norm_bwd(dcqn, dckvn, proj, g_q_latent, g_kv_latent, ql)
    dproj = jnp.concatenate([dcq, dckv, dq_sb.astype(BF16), dk_sb.astype(BF16), dv_sb.astype(BF16),
                             dgla, dglb, dkpe], axis=1)
    dh1 = _mm(dproj, w_in_p, name="mm_d_h1", tb=True)
    gw_in_p = _mm(h1, dproj, name="mm_gw_in", ta=True, out_dtype=BF16, tn=640)
    grad_x, red_1 = _rmsmod1_bwd(dh1, x, dx2, g_norm1, ada, SC1)

    nsb = d_in_p - 2 * ROPE
    gw_in = jnp.concatenate([gw_in_p[:, :kpe0], gw_in_p[:, nsb:nsb + ROPE], gw_in_p[:, kpe0:nsb]], axis=1)
    gw_uq = gw_uq_p.reshape(ql, nh, HEAD_PAD)[:, :, :QK_DIM].reshape(ql, nh * QK_DIM)
    gw_ukv = jnp.concatenate([gw_ukv_p[:, :mlaw].reshape(ql, nh, HEAD), gw_ukv_p[:, mlaw:].reshape(ql, nh, HEAD)],
                             axis=2).reshape(ql, 2 * mlaw)
    rest = ["w_in", "w_uq", "w_ukv", "w_proj_mla", "w_proj_sb", "w_out"]
    assert rest + ffn == names

    rest_pair = pair_sums(rest, [gw_in, gw_uq, gw_ukv, gw_pm, gw_ps, gw_o], "rest")
    rest_chips = _chip_exchange(rest_pair, name="rs_chip_exchange")
    ffn_pair, ffn_chips = _split_wait(ffn_send, ffn_recv, ffn_pair, ffn_lands, grad_x, _exchange_plan,
                                      name="rs_exchange_ffn_wait")
    reduced = [_chip_sum(s, r, chip_idx, name="rs_chip_sum_" + nm)
               for s, r, nm in zip(rest_pair + ffn_pair, rest_chips + ffn_chips, names)]
    from_sibling2 = _sibling_swap(reduced, name="rs_sibling_send", whole=True)

    vec_names = ["b_ada", "g_norm1", "g_norm2", "g_q_latent", "g_kv_latent", "g_q_head", "g_k_head"]
    vec_w = dict(b_ada=b_ada, g_norm1=g_norm1, g_norm2=g_norm2, g_q_latent=g_q_latent, g_kv_latent=g_kv_latent,
                 g_q_head=g_q_head, g_k_head=g_k_head)
    vec_m = dict(b_ada=m_b_ada, g_norm1=m_g_norm1, g_norm2=m_g_norm2, g_q_latent=m_g_q_latent,
                 g_kv_latent=m_g_kv_latent, g_q_head=m_g_q_head, g_k_head=m_g_k_head)
    vec_v = dict(b_ada=v_b_ada, g_norm1=v_g_norm1, g_norm2=v_g_norm2, g_q_latent=v_g_q_latent,
                 g_kv_latent=v_g_kv_latent, g_q_head=v_g_q_head, g_k_head=v_g_k_head)
    d_ada = jnp.concatenate([red_1[0:1], red_1[1:2], red_2[3:4], red_2[0:1], red_2[1:2], red_l[0:1]], axis=1)
    vec_parts = dict(b_ada=d_ada, g_norm1=red_1[2:3], g_norm2=red_2[2:3], g_q_latent=red_lat[0:1],
                     g_kv_latent=red_lat[1:2], g_q_head=red_qh[0:1], g_k_head=red_kh[0:1])
    widths = [-(-vec_w[nm].shape[1] // LANE) * LANE for nm in vec_names]
    offs = [sum(widths[:i]) for i in range(len(widths))]
    pack = lambda d: jnp.concatenate([_pad_cols(d[nm][:, :vec_w[nm].shape[1]], wd) for nm, wd in zip(vec_names, widths)], axis=1)
    nvec = sum(widths)
    parts_all = _gather_blocks([jnp.broadcast_to(pack(vec_parts), (8, nvec))], name="gather_vec_grads",
                               in_vmem=True)[0][:, 0, :]
    gvec, dvec, nmvec, nvvec = _adamw_vec(parts_all, pack(vec_w), pack(vec_m), pack(vec_v))
    unpack = lambda a: {nm: a[:, o_:o_ + vec_w[nm].shape[1]] for nm, o_ in zip(vec_names, offs)}
    gvec, dvec, nmvec, nvvec = unpack(gvec), unpack(dvec), unpack(nmvec), unpack(nvvec)

    dada_all = lax.dynamic_slice_in_dim(parts_all[:, :6 * D], chip * n_ada, n_ada, axis=1)
    cact_t = jax.nn.silu(c_all).T
    g_ada, d_ada_w, nm_ada, nv_ada = _adamw_ada(cact_t, dada_all, w_ada[0], m_w_ada[0], v_w_ada[0])

    ms = dict(w_in=m_w_in, w_uq=m_w_uq, w_ukv=m_w_ukv, w_proj_mla=m_w_proj_mla, w_proj_sb=m_w_proj_sb,
              w_out=m_w_out, w_ffn_in=m_w_ffn_in, w_ffn_out=m_w_ffn_out)
    vs = dict(w_in=v_w_in, w_uq=v_w_uq, w_ukv=v_w_ukv, w_proj_mla=v_w_proj_mla, w_proj_sb=v_w_proj_sb,
              w_out=v_w_out, w_ffn_in=v_w_ffn_in, w_ffn_out=v_w_ffn_out)
    G, DL, NM, NV = {}, {}, {}, {}
    for nm, mine, other in zip(names, reduced, from_sibling2):
        g_, d_, m_, v_ = _adamw(mats[nm], mine, other, c_idx, ms[nm][0], vs[nm][0], name="adamw_" + nm)
        G[nm], DL[nm], NM[nm], NV[nm] = g_[None], d_[None], m_[None], v_[None]
    G["w_ada"], DL["w_ada"], NM["w_ada"], NV["w_ada"] = g_ada[None], d_ada_w[None], nm_ada[None], nv_ada[None]
    for nm in vec_names:
        G[nm], DL[nm], NM[nm], NV[nm] = gvec[nm], dvec[nm], nmvec[nm], nvvec[nm]

    order = ["w_ada", "b_ada", "g_norm1", "g_norm2", "w_in", "g_q_latent", "g_kv_latent", "w_uq", "w_ukv",
             "g_q_head", "g_k_head", "w_proj_mla", "w_proj_sb", "w_out", "w_ffn_in", "w_ffn_out"]
    return (loss, grad_x[None], *[G[n] for n in order], *[DL[n] for n in order],
            *[NM[n] for n in order], *[NV[n] for n in order])
```

```python
import functools
import math

import jax
import jax.numpy as jnp
from jax import lax
from jax.experimental import pallas as pl
from jax.experimental.pallas import tpu as pltpu

F32 = jnp.float32
BF16 = jnp.bfloat16
MESH = pl.DeviceIdType.MESH

EPS = 1e-6
ROPE_THETA = 10000.0
NOPE = 128
ROPE = 64
QK_DIM = NOPE + ROPE
HEAD_PAD = 256
HEAD = 128
N_DEV = 8
LANE = 128
VMEM_LIMIT = 48 * 1024 * 1024

ADAM_LR = 0.001
ADAM_B1 = 0.9
ADAM_B2 = 0.999
ADAM_EPS = 1e-08
ADAM_WD = 0.01
ADAM_STEP = 10


def _tile(n, target):
    if n <= target:
        return n
    t = (target // LANE) * LANE
    while t >= LANE:
        if n % t == 0:
            return t
        t -= LANE
    return n


def _row_tile(rows, row_bytes, budget=24 * 1024 * 1024):
    cap = max(8, budget // (2 * row_bytes))
    best = None
    for t in range(8, min(rows, cap) + 1, 8):
        if rows % t == 0:
            best = t
    return best if best is not None else rows


def _params(sem):
    return pltpu.CompilerParams(dimension_semantics=sem, vmem_limit_bytes=VMEM_LIMIT)


def _rows(tm, w, col=0):
    return pl.BlockSpec((tm, w), lambda i: (i, col))


def _vec(w, col=0, rows=1):
    return pl.BlockSpec((rows, w), lambda i: (0, col))


MM_VMEM_BUDGET = 36 * 1024 * 1024


def _mm(a, b, *, name, ta=False, tb=False, out_dtype=F32, a_fn=None, bias=None, tm=1024, tn=1024):
    M = a.shape[1] if ta else a.shape[0]
    K = a.shape[0] if ta else a.shape[1]
    N = b.shape[0] if tb else b.shape[1]
    assert K == (b.shape[1] if tb else b.shape[0]), (a.shape, b.shape, ta, tb)
    tm, tn = _tile(M, tm), _tile(N, tn)
    sa, sb, so = a.dtype.itemsize, b.dtype.itemsize, jnp.dtype(out_dtype).itemsize

    def fits(tk):
        return 2 * tk * (tm * sa + tn * sb) + tm * tn * (2 * so + 4) <= MM_VMEM_BUDGET

    tk = K
    while not fits(tk):
        smaller = _tile(K, tk - LANE)
        if smaller >= tk:
            break
        tk = smaller
    nk = K // tk
    dn = (((0 if ta else 1,), (1 if tb else 0,)), ((), ()))
    b_outer = nk == 1 and a.size * sa * (N // tn) < b.size * sb * (M // tm)

    def body(*refs):
        a_ref, b_ref = refs[:2]
        bias_ref = refs[2] if bias is not None else None
        o_ref = refs[3 if bias is not None else 2]
        av = a_ref[...]
        if a_fn is not None:
            av = a_fn(av.astype(F32))
        part = lax.dot_general(av.astype(BF16), b_ref[...].astype(BF16), dn, preferred_element_type=F32)

        def finish(r):
            if bias is not None:
                r = r + bias_ref[...]
            o_ref[...] = r.astype(o_ref.dtype)

        if nk == 1:
            finish(part)
        else:
            acc_ref = refs[-1]
            k = pl.program_id(2)

            @pl.when(k == 0)
            def _():
                acc_ref[...] = part

            @pl.when(k > 0)
            def _():
                acc_ref[...] += part

            @pl.when(k == nk - 1)
            def _():
                finish(acc_ref[...])

    def ij(g0, g1):
        return (g1, g0) if b_outer else (g0, g1)

    def amap(g0, g1, k):
        i, _ = ij(g0, g1)
        return (k, i) if ta else (i, k)

    def bmap(g0, g1, k):
        _, j = ij(g0, g1)
        return (j, k) if tb else (k, j)

    in_specs = [pl.BlockSpec((tk, tm) if ta else (tm, tk), amap), pl.BlockSpec((tn, tk) if tb else (tk, tn), bmap)]
    args = [a, b]
    if bias is not None:
        in_specs.append(pl.BlockSpec((1, tn), lambda g0, g1, k: (0, ij(g0, g1)[1])))
        args.append(bias)
    grid = (N // tn, M // tm, nk) if b_outer else (M // tm, N // tn, nk)
    return pl.pallas_call(
        body, name=name, grid=grid, in_specs=in_specs,
        out_specs=pl.BlockSpec((tm, tn), lambda g0, g1, k: ij(g0, g1)),
        out_shape=jax.ShapeDtypeStruct((M, N), out_dtype),
        scratch_shapes=[pltpu.VMEM((tm, tn), F32)] if nk > 1 else [],
        compiler_params=_params(("parallel", "parallel", "arbitrary")),
    )(*args)


def _rms_rows(v):
    return lax.rsqrt(jnp.mean(v * v, axis=-1, keepdims=True) + EPS)


def _rmsmod(x, g, ada, sc_col, sh_col, *, name):
    S, D = x.shape
    tm = _tile(S, 256)

    def body(x_ref, g_ref, sc_ref, sh_ref, h_ref):
        xv = x_ref[...]
        h = (xv * _rms_rows(xv) * g_ref[...]) * (1.0 + sc_ref[...]) + sh_ref[...]
        h_ref[...] = h.astype(h_ref.dtype)

    return pl.pallas_call(
        body, name=name, grid=(S // tm,),
        in_specs=[_rows(tm, D), _vec(D), _vec(D, sc_col), _vec(D, sh_col)],
        out_specs=_rows(tm, D), out_shape=jax.ShapeDtypeStruct((S, D), BF16),
        compiler_params=_params(("parallel",)),
    )(x, g, ada, ada)


def _latent_norm(proj, g_q, g_kv, ql):
    S = proj.shape[0]
    tm = _tile(S, 512)

    def body(cq_ref, ckv_ref, gq_ref, gkv_ref, oq_ref, okv_ref):
        cq = cq_ref[...]
        oq_ref[...] = (cq * _rms_rows(cq) * gq_ref[...]).astype(BF16)
        ckv = ckv_ref[...]
        okv_ref[...] = (ckv * _rms_rows(ckv) * gkv_ref[...]).astype(BF16)

    return pl.pallas_call(
        body, name="latent_norm", grid=(S // tm,),
        in_specs=[_rows(tm, ql, 0), _rows(tm, ql, 1), _vec(ql), _vec(ql)],
        out_specs=[_rows(tm, ql), _rows(tm, ql)],
        out_shape=[jax.ShapeDtypeStruct((S, ql), BF16)] * 2,
        compiler_params=_params(("parallel",)),
    )(proj, proj, g_q, g_kv)


def _rope_fwd(y, c, s1, s2):
    return y * c + pltpu.roll(y, ROPE // 2, 1) * s1 + pltpu.roll(y, HEAD_PAD - ROPE // 2, 1) * s2


def _rope_bwd(d, c, s1, s2):
    return d * c + pltpu.roll(d * s1, HEAD_PAD - ROPE // 2, 1) + pltpu.roll(d * s2, ROPE // 2, 1)


def _head_rms(v):
    return lax.rsqrt(jnp.sum(v * v, axis=-1, keepdims=True) * (1.0 / QK_DIM) + EPS)


def _q_prep(q0, g_qh, tabs, nh):
    S = q0.shape[0]
    tm = _tile(S, 256)

    def body(q_ref, g_ref, c_ref, s1_ref, s2_ref, o_ref):
        c, s1, s2, g = c_ref[...], s1_ref[...], s2_ref[...], g_ref[...]
        for h in range(nh):
            sl = slice(h * HEAD_PAD, (h + 1) * HEAD_PAD)
            xs = q_ref[:, sl]
            o_ref[:, sl] = _rope_fwd(xs * _head_rms(xs) * g, c, s1, s2).astype(BF16)

    w = nh * HEAD_PAD
    return pl.pallas_call(
        body, name="mla_q_prep", grid=(S // tm,),
        in_specs=[_rows(tm, w), _vec(HEAD_PAD)] + [_rows(tm, HEAD_PAD)] * 3,
        out_specs=_rows(tm, w), out_shape=jax.ShapeDtypeStruct((S, w), BF16),
        compiler_params=_params(("parallel",)),
    )(q0, g_qh, *tabs)


def _k_prep(kv0, proj, kpe_col, g_kh, tabs, nh):
    S = kv0.shape[0]
    tm = _tile(S, 256)

    def body(kv_ref, kpe_ref, g_ref, c_ref, s1_ref, s2_ref, o_ref):
        c, s1, s2, g = c_ref[...], s1_ref[...], s2_ref[...], g_ref[...]
        kpe = kpe_ref[...]
        for h in range(nh):
            k0 = jnp.concatenate([kv_ref[:, h * HEAD:(h + 1) * HEAD], kpe], axis=1)
            o_ref[:, h * HEAD_PAD:(h + 1) * HEAD_PAD] = _rope_fwd(k0 * _head_rms(k0) * g, c, s1, s2).astype(BF16)

    return pl.pallas_call(
        body, name="mla_k_prep", grid=(S // tm,),
        in_specs=[_rows(tm, nh * HEAD, 0), _rows(tm, LANE, kpe_col), _vec(HEAD_PAD)] + [_rows(tm, HEAD_PAD)] * 3,
        out_specs=_rows(tm, nh * HEAD_PAD), out_shape=jax.ShapeDtypeStruct((S, nh * HEAD_PAD), BF16),
        compiler_params=_params(("parallel",)),
    )(kv0, proj, g_kh, *tabs)


def _gate_merge(pa, pb, proj, gla_col, glb_col):
    S, D = pa.shape
    tm = _tile(S, 256)

    def body(pa_ref, pb_ref, ga_ref, gb_ref, o_ref):
        o_ref[...] = (jax.nn.sigmoid(ga_ref[...]) * pa_ref[...] + jax.nn.sigmoid(gb_ref[...]) * pb_ref[...]).astype(BF16)

    return pl.pallas_call(
        body, name="gate_merge", grid=(S // tm,),
        in_specs=[_rows(tm, D), _rows(tm, D), _rows(tm, D, gla_col), _rows(tm, D, glb_col)],
        out_specs=_rows(tm, D), out_shape=jax.ShapeDtypeStruct((S, D), BF16),
        compiler_params=_params(("parallel",)),
    )(pa, pb, proj, proj)


def _resid_rmsmod(x, o, g, ada, gt_col, sc_col, sh_col):
    S, D = x.shape
    tm = _tile(S, 256)

    def body(x_ref, o_ref, g_ref, gt_ref, sc_ref, sh_ref, x2_ref, h_ref):
        x2 = x_ref[...] + gt_ref[...] * o_ref[...]
        x2_ref[...] = x2
        h_ref[...] = ((x2 * _rms_rows(x2) * g_ref[...]) * (1.0 + sc_ref[...]) + sh_ref[...]).astype(BF16)

    return pl.pallas_call(
        body, name="resid_rmsmod2", grid=(S // tm,),
        in_specs=[_rows(tm, D), _rows(tm, D), _vec(D), _vec(D, gt_col), _vec(D, sc_col), _vec(D, sh_col)],
        out_specs=[_rows(tm, D), _rows(tm, D)],
        out_shape=[jax.ShapeDtypeStruct((S, D), F32), jax.ShapeDtypeStruct((S, D), BF16)],
        compiler_params=_params(("parallel",)),
    )(x, o, g, ada, ada, ada)


def _swiglu(ff, dff_half):
    S = ff.shape[0]
    tm = _tile(S, 256)

    def body(g_ref, u_ref, o_ref):
        o_ref[...] = (jax.nn.silu(g_ref[...].astype(F32)) * u_ref[...].astype(F32)).astype(BF16)

    return pl.pallas_call(
        body, name="swiglu", grid=(S // tm,),
        in_specs=[_rows(tm, dff_half, 0), _rows(tm, dff_half, 1)],
        out_specs=_rows(tm, dff_half), out_shape=jax.ShapeDtypeStruct((S, dff_half), BF16),
        compiler_params=_params(("parallel",)),
    )(ff, ff)


def _loss_head(x2, f, tgt, ada, gt_col):
    S, D = x2.shape
    tm = _tile(S, 256)

    def body(x2_ref, f_ref, t_ref, gt_ref, dy_ref, df_ref, red_ref, loss_ref):
        @pl.when(pl.program_id(0) == 0)
        def _():
            red_ref[...] = jnp.zeros_like(red_ref)
            loss_ref[...] = jnp.zeros_like(loss_ref)

        fv = f_ref[...]
        gt = gt_ref[...]
        err = x2_ref[...] + gt * fv - t_ref[...]
        dy = err * (1.0 / D)
        dy_ref[...] = dy
        df_ref[...] = (dy * gt).astype(BF16)
        red_ref[0:1, :] += jnp.sum(dy * fv, axis=0, keepdims=True)
        loss_ref[...] += (0.5 / D) * jnp.sum(err * err)

    return pl.pallas_call(
        body, name="loss_head", grid=(S // tm,),
        in_specs=[_rows(tm, D), _rows(tm, D), _rows(tm, D), _vec(D, gt_col)],
        out_specs=[_rows(tm, D), _rows(tm, D), _vec(D, rows=8), _vec(LANE, rows=8)],
        out_shape=[jax.ShapeDtypeStruct((S, D), F32), jax.ShapeDtypeStruct((S, D), BF16),
                   jax.ShapeDtypeStruct((8, D), F32), jax.ShapeDtypeStruct((8, LANE), F32)],
        compiler_params=_params(("arbitrary",)),
    )(x2, f, tgt, ada)


def _swiglu_bwd(dact, ff, dff_half):
    S = ff.shape[0]
    tm = _tile(S, 128)

    def body(d_ref, g_ref, u_ref, o_ref):
        d = d_ref[...]
        g = g_ref[...].astype(F32)
        u = u_ref[...].astype(F32)
        sg = jax.nn.sigmoid(g)
        o_ref[:, :dff_half] = (d * u * sg * (1.0 + g * (1.0 - sg))).astype(BF16)
        o_ref[:, dff_half:] = (d * g * sg).astype(BF16)

    return pl.pallas_call(
        body, name="swiglu_bwd", grid=(S // tm,),
        in_specs=[_rows(tm, dff_half), _rows(tm, dff_half, 0), _rows(tm, dff_half, 1)],
        out_specs=_rows(tm, 2 * dff_half), out_shape=jax.ShapeDtypeStruct((S, 2 * dff_half), BF16),
        compiler_params=_params(("parallel",)),
    )(dact, ff, ff)


def _rmsmod2_bwd(dh2, x2, dy, o, g, ada, sc_col, gt_col):
    S, D = x2.shape
    tm = _tile(S, 256)

    def body(dh_ref, x2_ref, dy_ref, o_ref, g_ref, sc_ref, gt_ref, dx_ref, do_ref, red_ref):
        @pl.when(pl.program_id(0) == 0)
        def _():
            red_ref[...] = jnp.zeros_like(red_ref)

        dh = dh_ref[...]
        x2 = x2_ref[...]
        gv = g_ref[...]
        mod = 1.0 + sc_ref[...]
        r = _rms_rows(x2)
        xn = x2 * r
        t = dh * xn
        red_ref[0:1, :] += jnp.sum(dh, axis=0, keepdims=True)
        red_ref[1:2, :] += jnp.sum(t * gv, axis=0, keepdims=True)
        red_ref[2:3, :] += jnp.sum(t * mod, axis=0, keepdims=True)
        dxn = dh * gv * mod
        dx = dy_ref[...] + r * (dxn - xn * jnp.mean(dxn * xn, axis=-1, keepdims=True))
        dx_ref[...] = dx
        red_ref[3:4, :] += jnp.sum(dx * o_ref[...], axis=0, keepdims=True)
        do_ref[...] = (dx * gt_ref[...]).astype(BF16)

    return pl.pallas_call(
        body, name="rmsmod2_bwd", grid=(S // tm,),
        in_specs=[_rows(tm, D)] * 4 + [_vec(D), _vec(D, sc_col), _vec(D, gt_col)],
        out_specs=[_rows(tm, D), _rows(tm, D), _vec(D, rows=8)],
        out_shape=[jax.ShapeDtypeStruct((S, D), F32), jax.ShapeDtypeStruct((S, D), BF16),
                   jax.ShapeDtypeStruct((8, D), F32)],
        compiler_params=_params(("arbitrary",)),
    )(dh2, x2, dy, o, g, ada, ada)


def _rmsmod1_bwd(dh, x, dx2, g, ada, sc_col):
    S, D = x.shape
    tm = _tile(S, 256)

    def body(dh_ref, x_ref, dx2_ref, g_ref, sc_ref, gx_ref, red_ref):
        @pl.when(pl.program_id(0) == 0)
        def _():
            red_ref[...] = jnp.zeros_like(red_ref)

        dh = dh_ref[...]
        xv = x_ref[...]
        gv = g_ref[...]
        mod = 1.0 + sc_ref[...]
        r = _rms_rows(xv)
        xn = xv * r
        t = dh * xn
        red_ref[0:1, :] += jnp.sum(dh, axis=0, keepdims=True)
        red_ref[1:2, :] += jnp.sum(t * gv, axis=0, keepdims=True)
        red_ref[2:3, :] += jnp.sum(t * mod, axis=0, keepdims=True)
        dxn = dh * gv * mod
        gx_ref[...] = dx2_ref[...] + r * (dxn - xn * jnp.mean(dxn * xn, axis=-1, keepdims=True))

    return pl.pallas_call(
        body, name="rmsmod1_bwd", grid=(S // tm,),
        in_specs=[_rows(tm, D)] * 3 + [_vec(D), _vec(D, sc_col)],
        out_specs=[_rows(tm, D), _vec(D, rows=8)],
        out_shape=[jax.ShapeDtypeStruct((S, D), F32), jax.ShapeDtypeStruct((8, D), F32)],
        compiler_params=_params(("arbitrary",)),
    )(dh, x, dx2, g, ada)


def _gate_bwd(dm, pa, pb, proj, gla_col, glb_col):
    S, D = pa.shape
    tm = _tile(S, 256)

    def body(dm_ref, pa_ref, pb_ref, la_ref, lb_ref, dpa_ref, dpb_ref, dla_ref, dlb_ref):
        dm_ = dm_ref[...]
        ga = jax.nn.sigmoid(la_ref[...])
        gb = jax.nn.sigmoid(lb_ref[...])
        dpa_ref[...] = (dm_ * ga).astype(BF16)
        dpb_ref[...] = (dm_ * gb).astype(BF16)
        dla_ref[...] = (dm_ * pa_ref[...] * ga * (1.0 - ga)).astype(BF16)
        dlb_ref[...] = (dm_ * pb_ref[...] * gb * (1.0 - gb)).astype(BF16)

    return pl.pallas_call(
        body, name="gate_bwd", grid=(S // tm,),
        in_specs=[_rows(tm, D)] * 3 + [_rows(tm, D, gla_col), _rows(tm, D, glb_col)],
        out_specs=[_rows(tm, D)] * 4, out_shape=[jax.ShapeDtypeStruct((S, D), BF16)] * 4,
        compiler_params=_params(("parallel",)),
    )(dm, pa, pb, proj, proj)


def _q_prep_bwd(dq, q0, g_qh, tabs, nh):
    S = q0.shape[0]
    tm = _tile(S, 256)

    def body(dq_ref, q_ref, g_ref, c_ref, s1_ref, s2_ref, o_ref, red_ref):
        @pl.when(pl.program_id(0) == 0)
        def _():
            red_ref[...] = jnp.zeros_like(red_ref)

        c, s1, s2, g = c_ref[...], s1_ref[...], s2_ref[...], g_ref[...]
        dg = jnp.zeros((1, HEAD_PAD), F32)
        for h in range(nh):
            sl = slice(h * HEAD_PAD, (h + 1) * HEAD_PAD)
            d1 = _rope_bwd(dq_ref[:, sl], c, s1, s2)
            xs = q_ref[:, sl]
            r = _head_rms(xs)
            qn = xs * r
            dg = dg + jnp.sum(d1 * qn, axis=0, keepdims=True)
            dn = d1 * g
            o_ref[:, sl] = (r * (dn - qn * (jnp.sum(dn * qn, axis=-1, keepdims=True) * (1.0 / QK_DIM)))).astype(BF16)
        red_ref[0:1, :] += dg

    w = nh * HEAD_PAD
    return pl.pallas_call(
        body, name="mla_q_prep_bwd", grid=(S // tm,),
        in_specs=[_rows(tm, w), _rows(tm, w), _vec(HEAD_PAD)] + [_rows(tm, HEAD_PAD)] * 3,
        out_specs=[_rows(tm, w), _vec(HEAD_PAD, rows=8)],
        out_shape=[jax.ShapeDtypeStruct((S, w), BF16), jax.ShapeDtypeStruct((8, HEAD_PAD), F32)],
        compiler_params=_params(("arbitrary",)),
    )(dq, q0, g_qh, *tabs)


def _k_prep_bwd(dk, dv, kv0, proj, kpe_col, g_kh, tabs, nh):
    S = kv0.shape[0]
    tm = _tile(S, 256)
    wv = nh * HEAD

    def body(dk_ref, dv_ref, kv_ref, kpe_ref, g_ref, c_ref, s1_ref, s2_ref, o_ref, dpe_ref, red_ref):
        @pl.when(pl.program_id(0) == 0)
        def _():
            red_ref[...] = jnp.zeros_like(red_ref)

        c, s1, s2, g = c_ref[...], s1_ref[...], s2_ref[...], g_ref[...]
        kpe = kpe_ref[...]
        dg = jnp.zeros((1, HEAD_PAD), F32)
        dpe = jnp.zeros((tm, LANE), F32)
        for h in range(nh):
            d1 = _rope_bwd(dk_ref[:, h * HEAD_PAD:(h + 1) * HEAD_PAD], c, s1, s2)
            k0 = jnp.concatenate([kv_ref[:, h * HEAD:(h + 1) * HEAD], kpe], axis=1)
            r = _head_rms(k0)
            kn = k0 * r
            dg = dg + jnp.sum(d1 * kn, axis=0, keepdims=True)
            dn = d1 * g
            dk0 = r * (dn - kn * (jnp.sum(dn * kn, axis=-1, keepdims=True) * (1.0 / QK_DIM)))
            o_ref[:, h * HEAD:(h + 1) * HEAD] = dk0[:, :HEAD].astype(BF16)
            dpe = dpe + dk0[:, HEAD:]
        o_ref[:, wv:] = dv_ref[...].astype(BF16)
        dpe_ref[...] = dpe.astype(BF16)
        red_ref[0:1, :] += dg

    return pl.pallas_call(
        body, name="mla_k_prep_bwd", grid=(S // tm,),
        in_specs=[_rows(tm, nh * HEAD_PAD), _rows(tm, wv), _rows(tm, wv, 0), _rows(tm, LANE, kpe_col),
                  _vec(HEAD_PAD)] + [_rows(tm, HEAD_PAD)] * 3,
        out_specs=[_rows(tm, 2 * wv), _rows(tm, LANE), _vec(HEAD_PAD, rows=8)],
        out_shape=[jax.ShapeDtypeStruct((S, 2 * wv), BF16), jax.ShapeDtypeStruct((S, LANE), BF16),
                   jax.ShapeDtypeStruct((8, HEAD_PAD), F32)],
        compiler_params=_params(("arbitrary",)),
    )(dk, dv, kv0, proj, g_kh, *tabs)


def _latent_norm_bwd(dcqn, dckvn, proj, g_q, g_kv, ql):
    S = proj.shape[0]
    tm = _tile(S, 512)

    def body(dq_ref, dkv_ref, cq_ref, ckv_ref, gq_ref, gkv_ref, oq_ref, okv_ref, red_ref):
        @pl.when(pl.program_id(0) == 0)
        def _():
            red_ref[...] = jnp.zeros_like(red_ref)

        for row, (d_ref, c_ref, g_ref, o_ref) in enumerate(((dq_ref, cq_ref, gq_ref, oq_ref),
                                                            (dkv_ref, ckv_ref, gkv_ref, okv_ref))):
            d = d_ref[...]
            cv = c_ref[...]
            r = _rms_rows(cv)
            ch = cv * r
            red_ref[row:row + 1, :] += jnp.sum(d * ch, axis=0, keepdims=True)
            dn = d * g_ref[...]
            o_ref[...] = (r * (dn - ch * jnp.mean(dn * ch, axis=-1, keepdims=True))).astype(BF16)

    return pl.pallas_call(
        body, name="latent_norm_bwd", grid=(S // tm,),
        in_specs=[_rows(tm, ql), _rows(tm, ql), _rows(tm, ql, 0), _rows(tm, ql, 1), _vec(ql), _vec(ql)],
        out_specs=[_rows(tm, ql), _rows(tm, ql), _vec(ql, rows=8)],
        out_shape=[jax.ShapeDtypeStruct((S, ql), BF16)] * 2 + [jax.ShapeDtypeStruct((8, ql), F32)],
        compiler_params=_params(("arbitrary",)),
    )(dcqn, dckvn, proj, proj, g_q, g_kv)


NEG = -1e30
ATT_TILE = 512
SB_SUB = 128
_NT = (((1,), (1,)), ((), ()))
_TN = (((0,), (0,)), ((), ()))


def _dot(a, b, dn=(((1,), (0,)), ((), ()))):
    return lax.dot_general(a, b, dn, preferred_element_type=F32)


def _key_rows(kb, t):
    return pl.ds(pl.multiple_of(kb * t, t), t)


def _diag_mask(t, strict):
    r = lax.broadcasted_iota(jnp.int32, (t, t), 0)
    c = lax.broadcasted_iota(jnp.int32, (t, t), 1)
    return c < r if strict else c <= r


def _mla_fwd(q, k, kv0, nh):
    S = q.shape[0]
    t = _tile(S, ATT_TILE)
    scale = QK_DIM ** -0.5

    def body(q_ref, k_ref, v_ref, o_ref, lse_ref):
        i = pl.program_id(1)
        qv = q_ref[...]

        def block(kb, carry, masked):
            m, l, acc = carry
            rows = _key_rows(kb, t)
            s = _dot(qv, k_ref[rows, :], _NT) * scale
            if masked:
                s = jnp.where(_diag_mask(t, False), s, NEG)
            m_new = jnp.maximum(m, jnp.max(s, axis=-1, keepdims=True))
            alpha = jnp.exp(m - m_new)
            p = jnp.exp(s - m_new)
            l = alpha * l + jnp.sum(p, axis=-1, keepdims=True)
            acc = alpha * acc + _dot(p.astype(BF16), v_ref[rows, :].astype(BF16))
            return m_new, l, acc

        init = (jnp.full((t, 1), NEG, F32), jnp.zeros((t, 1), F32), jnp.zeros((t, HEAD), F32))
        carry = lax.fori_loop(0, i, lambda kb, c: block(kb, c, False), init)
        m, l, acc = block(i, carry, True)
        o_ref[...] = acc / l
        lse_ref[...] = m + jnp.log(l)

    return pl.pallas_call(
        body, name="mla_attn_fwd", grid=(nh, S // t),
        in_specs=[pl.BlockSpec((t, HEAD_PAD), lambda h, i: (i, h)),
                  pl.BlockSpec((S, HEAD_PAD), lambda h, i: (0, h)),
                  pl.BlockSpec((S, HEAD), lambda h, i: (0, nh + h))],
        out_specs=[pl.BlockSpec((t, HEAD), lambda h, i: (i, h)),
                   pl.BlockSpec((None, t, 1), lambda h, i: (h, i, 0))],
        out_shape=[jax.ShapeDtypeStruct((S, nh * HEAD), F32), jax.ShapeDtypeStruct((nh, S, 1), F32)],
        compiler_params=_params(("parallel", "arbitrary")),
    )(q, k, kv0)


def _mla_bwd(q, k, kv0, o, do, lse, nh):
    S = q.shape[0]
    t = _tile(S, ATT_TILE)
    scale = QK_DIM ** -0.5

    def body(q_ref, k_ref, v_ref, o_ref, do_ref, lse_ref, dq_ref, dk_ref, dv_ref):
        i = pl.program_id(1)

        @pl.when(i == 0)
        def _():
            dk_ref[...] = jnp.zeros_like(dk_ref)
            dv_ref[...] = jnp.zeros_like(dv_ref)

        qv = q_ref[...]
        dov = do_ref[...]
        delta = jnp.sum(dov * o_ref[...], axis=-1, keepdims=True)
        dob = dov.astype(BF16)
        lse = lse_ref[...]

        def block(kb, dq, masked):
            rows = _key_rows(kb, t)
            ks = k_ref[rows, :]
            vs = v_ref[rows, :].astype(BF16)
            p = jnp.exp(_dot(qv, ks, _NT) * scale - lse)
            if masked:
                p = jnp.where(_diag_mask(t, False), p, 0.0)
            ds = (p * (_dot(dob, vs, _NT) - delta) * scale).astype(BF16)
            dk_ref[rows, :] += _dot(ds, qv, _TN)
            dv_ref[rows, :] += _dot(p.astype(BF16), dob, _TN)
            return dq + _dot(ds, ks)

        dq = lax.fori_loop(0, i, lambda kb, c: block(kb, c, False), jnp.zeros((t, HEAD_PAD), F32))
        dq_ref[...] = block(i, dq, True)

    return pl.pallas_call(
        body, name="mla_attn_bwd", grid=(nh, S // t),
        in_specs=[pl.BlockSpec((t, HEAD_PAD), lambda h, i: (i, h)),
                  pl.BlockSpec((S, HEAD_PAD), lambda h, i: (0, h)),
                  pl.BlockSpec((S, HEAD), lambda h, i: (0, nh + h)),
                  pl.BlockSpec((t, HEAD), lambda h, i: (i, h)),
                  pl.BlockSpec((t, HEAD), lambda h, i: (i, h)),
                  pl.BlockSpec((None, t, 1), lambda h, i: (h, i, 0))],
        out_specs=[pl.BlockSpec((t, HEAD_PAD), lambda h, i: (i, h)),
                   pl.BlockSpec((S, HEAD_PAD), lambda h, i: (0, h)),
                   pl.BlockSpec((S, HEAD), lambda h, i: (0, h))],
        out_shape=[jax.ShapeDtypeStruct((S, nh * HEAD_PAD), F32), jax.ShapeDtypeStruct((S, nh * HEAD_PAD), F32),
                   jax.ShapeDtypeStruct((S, nh * HEAD), F32)],
        compiler_params=_params(("parallel", "arbitrary")),
    )(q, k, kv0, o, do, lse)


def _split_dot(v, tri):
    hi = v.astype(BF16)
    lo = (v - hi.astype(F32)).astype(BF16)
    return _dot(hi, tri) + _dot(lo, tri)


def _tri(n, cmp):
    r = lax.broadcasted_iota(jnp.int32, (n, n), 0)
    c = lax.broadcasted_iota(jnp.int32, (n, n), 1)
    return jnp.where(cmp(r, c), 1.0, 0.0).astype(BF16)


def _sb_block(qv, ks, run, upper, t, scale, masked):
    z = _dot(qv, ks, _NT) * scale
    lb = jnp.minimum(z, 0.0) - jnp.log(1.0 + jnp.exp(-jnp.abs(z)))
    lom = lb - z
    mask = _diag_mask(t, True) if masked else None
    if masked:
        lom = jnp.where(mask, lom, 0.0)
    tails = []
    for sblk in reversed(range(t // SB_SUB)):
        part = lom[:, sblk * SB_SUB:(sblk + 1) * SB_SUB]
        tails.append(_split_dot(part, upper) + run)
        run = run + jnp.sum(part, axis=-1, keepdims=True)
    a = jnp.exp(lb + jnp.concatenate(tails[::-1], axis=1))
    if masked:
        a = jnp.where(mask, a, 0.0)
    return a, lb, mask, run


def _sb_fwd(proj, q_col, k_col, v_col, nh):
    S = proj.shape[0]
    t = _tile(S, ATT_TILE)
    scale = HEAD ** -0.5

    def body(q_ref, k_ref, v_ref, o_ref):
        i = pl.program_id(1)
        qv = q_ref[...].astype(BF16)
        upper = _tri(SB_SUB, lambda j, s: j > s)

        def block(kb, carry, masked):
            run, acc = carry
            rows = _key_rows(kb, t)
            a, _, _, run = _sb_block(qv, k_ref[rows, :].astype(BF16), run, upper, t, scale, masked)
            return run, acc + _dot(a.astype(BF16), v_ref[rows, :].astype(BF16))

        carry = block(i, (jnp.zeros((t, 1), F32), jnp.zeros((t, HEAD), F32)), True)
        o_ref[...] = lax.fori_loop(0, i, lambda j, c: block(i - 1 - j, c, False), carry)[1]

    return pl.pallas_call(
        body, name="sb_attn_fwd", grid=(nh, S // t),
        in_specs=[pl.BlockSpec((t, HEAD), lambda h, i: (i, q_col + h)),
                  pl.BlockSpec((S, HEAD), lambda h, i: (0, k_col + h)),
                  pl.BlockSpec((S, HEAD), lambda h, i: (0, v_col + h))],
        out_specs=pl.BlockSpec((t, HEAD), lambda h, i: (i, h)),
        out_shape=jax.ShapeDtypeStruct((S, nh * HEAD), F32),
        compiler_params=_params(("parallel", "arbitrary")),
    )(proj, proj, proj)


def _sb_bwd(proj, q_col, k_col, v_col, dy, nh):
    S = proj.shape[0]
    t = _tile(S, ATT_TILE)
    scale = HEAD ** -0.5

    def body(q_ref, k_ref, v_ref, dy_ref, dq_ref, dk_ref, dv_ref, run_ref):
        i = pl.program_id(1)

        @pl.when(i == 0)
        def _():
            dk_ref[...] = jnp.zeros_like(dk_ref)
            dv_ref[...] = jnp.zeros_like(dv_ref)

        qv = q_ref[...].astype(BF16)
        dyb = dy_ref[...].astype(BF16)
        upper = _tri(SB_SUB, lambda j, s: j > s)
        before = _tri(SB_SUB, lambda s, j: s < j)

        def suffix(kb, run, masked):
            z = _dot(qv, k_ref[_key_rows(kb, t), :].astype(BF16), _NT) * scale
            lom = jnp.minimum(z, 0.0) - jnp.log(1.0 + jnp.exp(-jnp.abs(z))) - z
            if masked:
                lom = jnp.where(_diag_mask(t, True), lom, 0.0)
            run_ref[kb] = jnp.broadcast_to(run, (t, LANE))
            return run + jnp.sum(lom, axis=-1, keepdims=True)

        run0 = suffix(i, jnp.zeros((t, 1), F32), True)
        lax.fori_loop(0, i, lambda j, r: suffix(i - 1 - j, r, False), run0)

        def block(kb, carry, masked):
            prefix, dq = carry
            rows = _key_rows(kb, t)
            ks = k_ref[rows, :].astype(BF16)
            vs = v_ref[rows, :].astype(BF16)
            a, lb, mask, _ = _sb_block(qv, ks, run_ref[kb][:, 0:1], upper, t, scale, masked)
            dl = a * _dot(dyb, vs, _NT)
            lefts = []
            for sblk in range(t // SB_SUB):
                part = dl[:, sblk * SB_SUB:(sblk + 1) * SB_SUB]
                lefts.append(_dot(part.astype(BF16), before) + prefix)
                prefix = prefix + jnp.sum(part, axis=-1, keepdims=True)
            beta = jnp.exp(lb)
            dz = dl * (1.0 - beta) - beta * jnp.concatenate(lefts, axis=1)
            if masked:
                dz = jnp.where(mask, dz, 0.0)
            dz = (dz * scale).astype(BF16)
            dk_ref[rows, :] += _dot(dz, qv, _TN)
            dv_ref[rows, :] += _dot(a.astype(BF16), dyb, _TN)
            return prefix, dq + _dot(dz, ks)

        carry = lax.fori_loop(0, i, lambda kb, c: block(kb, c, False),
                              (jnp.zeros((t, 1), F32), jnp.zeros((t, HEAD), F32)))
        dq_ref[...] = block(i, carry, True)[1]

    full = pl.BlockSpec((S, HEAD), lambda h, i: (0, h))
    tile = pl.BlockSpec((t, HEAD), lambda h, i: (i, h))
    return pl.pallas_call(
        body, name="sb_attn_bwd", grid=(nh, S // t),
        in_specs=[pl.BlockSpec((t, HEAD), lambda h, i: (i, q_col + h)),
                  pl.BlockSpec((S, HEAD), lambda h, i: (0, k_col + h)),
                  pl.BlockSpec((S, HEAD), lambda h, i: (0, v_col + h)), tile],
        out_specs=[tile, full, full],
        out_shape=[jax.ShapeDtypeStruct((S, nh * HEAD), F32)] * 3,
        scratch_shapes=[pltpu.VMEM((S // t, t, LANE), F32)],
        compiler_params=_params(("parallel", "arbitrary")),
    )(proj, proj, proj, dy)


def _place():
    return lax.axis_index("x"), lax.axis_index("y"), lax.axis_index("c")


def _other_chips(x, y):
    return [(1 - x, y), (x, 1 - y), (1 - x, 1 - y)]


def _dev_index(p):
    return 4 * p[0] + 2 * p[1] + p[2]


def _gather_blocks(blocks, *, name, in_vmem):
    n = len(blocks)
    per = 7

    def body(*refs):
        ins, outs = refs[:n], refs[n:2 * n]
        send_sems, recv_sems, local_sems = refs[2 * n:]
        x, y, c = _place()
        me, sibling = (x, y, c), (x, y, 1 - c)
        chips = _other_chips(x, y)

        def slot(a, p):
            return outs[a].at[_dev_index(p)]

        def copy(a, k, block, to, src=None):
            return pltpu.make_async_remote_copy(
                src_ref=slot(a, block) if src is None else src, dst_ref=slot(a, block),
                send_sem=send_sems.at[a * per + k], recv_sem=recv_sems.at[a * per + k],
                device_id=to, device_id_type=MESH)

        mine = [pltpu.make_async_copy(ins[a], slot(a, me), local_sems.at[a]) for a in range(n)] if in_vmem else []
        for cp in mine:
            cp.start()
        first = []
        for a in range(n):
            first.append(copy(a, 0, me, sibling, src=ins[a]))
            first += [copy(a, 1 + j, me, (*chip, c), src=ins[a]) for j, chip in enumerate(chips)]
        for cp in first:
            cp.start()
        passed = []
        for a in range(n):
            for j, chip in enumerate(chips):
                copy(a, 1 + j, (*chip, c), me).wait_recv()
                cp = copy(a, 4 + j, (*chip, c), sibling)
                cp.start()
                passed.append(cp)
        for a in range(n):
            copy(a, 0, sibling, me).wait_recv()
            for j, chip in enumerate(chips):
                copy(a, 4 + j, (*chip, 1 - c), me).wait_recv()
        for cp in first + passed:
            cp.wait_send()
        for cp in mine:
            cp.wait()

    space = pltpu.VMEM if in_vmem else pl.ANY
    spec = pl.BlockSpec(memory_space=space)
    outs = pl.pallas_call(
        body, name=name, in_specs=[spec] * n, out_specs=[spec] * n,
        out_shape=[jax.ShapeDtypeStruct((N_DEV,) + b.shape, b.dtype) for b in blocks],
        scratch_shapes=[pltpu.SemaphoreType.DMA((n * per,)), pltpu.SemaphoreType.DMA((n * per,)),
                        pltpu.SemaphoreType.DMA((n,))],
        compiler_params=pltpu.CompilerParams(vmem_limit_bytes=VMEM_LIMIT),
    )(*blocks)
    return list(outs)


def _sibling_swap(arrs, *, name, whole=False):
    n = len(arrs)

    def body(*refs):
        ins, outs = refs[:n], refs[n:2 * n]
        send_sems, recv_sems = refs[2 * n:]
        x, y, c = _place()
        copies = [pltpu.make_async_remote_copy(
            src_ref=ins[a] if whole else ins[a].at[1 - c], dst_ref=outs[a],
            send_sem=send_sems.at[a], recv_sem=recv_sems.at[a],
            device_id=(x, y, 1 - c), device_id_type=MESH) for a in range(n)]
        for cp in copies:
            cp.start()
        for cp in copies:
            cp.wait()

    spec = pl.BlockSpec(memory_space=pl.ANY)
    return list(pl.pallas_call(
        body, name=name, in_specs=[spec] * n, out_specs=[spec] * n,
        out_shape=[jax.ShapeDtypeStruct(a.shape if whole else a.shape[1:], a.dtype) for a in arrs],
        scratch_shapes=[pltpu.SemaphoreType.DMA((n,)), pltpu.SemaphoreType.DMA((n,))],
    )(*arrs))


def _chip_exchange(arrs, *, name):
    n = len(arrs)

    def body(*refs):
        ins, outs = refs[:n], refs[n:2 * n]
        send_sems, recv_sems = refs[2 * n:]
        x, y, c = _place()
        copies = []
        for a in range(n):
            for j, (px, py) in enumerate(_other_chips(x, y)):
                copies.append(pltpu.make_async_remote_copy(
                    src_ref=ins[a].at[2 * px + py], dst_ref=outs[a].at[j],
                    send_sem=send_sems.at[3 * a + j], recv_sem=recv_sems.at[3 * a + j],
                    device_id=(px, py, c), device_id_type=MESH))
        for cp in copies:
            cp.start()
        for cp in copies:
            cp.wait()

    spec = pl.BlockSpec(memory_space=pl.ANY)
    return list(pl.pallas_call(
        body, name=name, in_specs=[spec] * n, out_specs=[spec] * n,
        out_shape=[jax.ShapeDtypeStruct((3,) + a.shape[1:], a.dtype) for a in arrs],
        scratch_shapes=[pltpu.SemaphoreType.DMA((3 * n,)), pltpu.SemaphoreType.DMA((3 * n,))],
    )(*arrs))


_HBM = pl.BlockSpec(memory_space=pltpu.HBM)
_SEM = pl.BlockSpec(memory_space=pltpu.SEMAPHORE)
_EFFECT = pltpu.SideEffectType.DATAFLOW_SIDE_EFFECTING


def _in_hbm(a):
    return pltpu.with_memory_space_constraint(a, pltpu.HBM)


def _split_copies(srcs, lands, send_sems, recv_sems, plan):
    x, y, c = _place()
    copies = []
    for a, (src, land) in enumerate(zip(srcs, lands)):
        steps = plan(x, y, c)
        for k, (pick, slot, to) in enumerate(steps):
            copies.append(pltpu.make_async_remote_copy(
                src_ref=pick(src), dst_ref=slot(land), send_sem=send_sems.at[a * len(steps) + k],
                recv_sem=recv_sems.at[a * len(steps) + k], device_id=to, device_id_type=MESH))
    return copies


def _split_start(srcs, land_shapes, plan, per, *, name):
    n = len(srcs)

    def body(*refs):
        send_sems, recv_sems = refs[2 * n], refs[2 * n + 1]
        for cp in _split_copies(refs[:n], refs[n:2 * n], send_sems, recv_sems, plan):
            cp.start()
        token = refs[-1]
        token[...] = jnp.zeros_like(token)

    lands = [_in_hbm(lax.empty(s.shape, s.dtype)) for s in land_shapes]
    outs = pl.pallas_call(
        body, name=name,
        out_shape=(pltpu.SemaphoreType.DMA((n * per,)), pltpu.SemaphoreType.DMA((n * per,)),
                   *[pltpu.HBM(s.shape, s.dtype) for s in srcs], *[pltpu.HBM(s.shape, s.dtype) for s in land_shapes],
                   jax.ShapeDtypeStruct((8, LANE), F32)),
        in_specs=[_HBM] * (2 * n),
        out_specs=(_SEM, _SEM, *[_HBM] * (2 * n), pl.BlockSpec(memory_space=pltpu.VMEM)),
        input_output_aliases={i: 2 + i for i in range(2 * n)},
        compiler_params=pltpu.CompilerParams(has_side_effects=_EFFECT),
    )(*[_in_hbm(s) for s in srcs], *lands)
    return outs[0], outs[1], list(outs[2:2 + n]), list(outs[2 + n:2 + 2 * n]), outs[-1]


def _split_wait(send_sems, recv_sems, srcs, lands, after, plan, *, name):
    n = len(srcs)

    def body(*refs):
        for cp in _split_copies(refs[:n], refs[n:2 * n], refs[2 * n], refs[2 * n + 1], plan):
            cp.wait_send()
            cp.wait_recv()

    outs = pl.pallas_call(
        body, name=name,
        out_shape=(*[pltpu.HBM(s.shape, s.dtype) for s in srcs], *[pltpu.HBM(s.shape, s.dtype) for s in lands]),
        in_specs=[_HBM] * (2 * n) + [_SEM, _SEM, pl.BlockSpec(memory_space=pl.ANY)],
        out_specs=tuple([_HBM] * (2 * n)),
        input_output_aliases={i: i for i in range(2 * n)},
        compiler_params=pltpu.CompilerParams(has_side_effects=_EFFECT),
    )(*srcs, *lands, send_sems, recv_sems, after)
    return list(outs[:n]), list(outs[n:])


def _gather_plan(x, y, c):
    slot = lambda land: land.at[_dev_index((x, y, c))]
    whole = lambda src: src
    return [(whole, slot, (x, y, 1 - c))] + [(whole, slot, (px, py, c)) for px, py in _other_chips(x, y)]


def _exchange_plan(x, y, c):
    return [(lambda src, k=2 * px + py: src.at[k], lambda land, j=j: land.at[j], (px, py, c))
            for j, (px, py) in enumerate(_other_chips(x, y))]


def _gather_forward(lands, *, name):
    n = len(lands)

    def body(*refs):
        lands_in, outs = refs[:n], refs[n:2 * n]
        send_sems, recv_sems = refs[2 * n:]
        x, y, c = _place()
        copies = []
        for a in range(n):
            for j, (px, py) in enumerate(_other_chips(x, y)):
                copies.append((pltpu.make_async_remote_copy(
                    src_ref=lands_in[a].at[_dev_index((px, py, c))], dst_ref=outs[a].at[_dev_index((px, py, c))],
                    send_sem=send_sems.at[3 * a + j], recv_sem=recv_sems.at[3 * a + j],
                    device_id=(x, y, 1 - c), device_id_type=MESH), a, j, (px, py)))
        for cp, _, _, _ in copies:
            cp.start()
        for cp, a, j, (px, py) in copies:
            cp.wait_send()
            pltpu.make_async_remote_copy(
                src_ref=lands_in[a].at[_dev_index((px, py, 1 - c))], dst_ref=outs[a].at[_dev_index((px, py, 1 - c))],
                send_sem=send_sems.at[3 * a + j], recv_sem=recv_sems.at[3 * a + j],
                device_id=(x, y, 1 - c), device_id_type=MESH).wait_recv()

    spec = pl.BlockSpec(memory_space=pl.ANY)
    return list(pl.pallas_call(
        body, name=name, in_specs=[spec] * n, out_specs=[spec] * n,
        out_shape=[jax.ShapeDtypeStruct(a.shape, a.dtype) for a in lands],
        input_output_aliases={a: a for a in range(n)},
        scratch_shapes=[pltpu.SemaphoreType.DMA((3 * n,)), pltpu.SemaphoreType.DMA((3 * n,))],
    )(*lands))


def _flat2(a, lead):
    return a.reshape(a.shape[:lead] + (-1, a.shape[-1]))


def _pair_sum(g, recv, c_idx, *, name):
    _, nchip, r, w = g.shape
    tm = _tile(r, 256) if r % 8 == 0 else r

    def body(c_ref, g_ref, r_ref, o_ref):
        o_ref[...] = (g_ref[...].astype(F32) + r_ref[...].astype(F32)).astype(o_ref.dtype)

    return pl.pallas_call(
        body, name=name,
        grid_spec=pltpu.PrefetchScalarGridSpec(
            num_scalar_prefetch=1, grid=(nchip, r // tm),
            in_specs=[pl.BlockSpec((None, None, tm, w), lambda k, i, c_ref: (c_ref[0], k, i, 0)),
                      pl.BlockSpec((None, tm, w), lambda k, i, c_ref: (k, i, 0))],
            out_specs=pl.BlockSpec((None, tm, w), lambda k, i, c_ref: (k, i, 0))),
        out_shape=jax.ShapeDtypeStruct((nchip, r, w), BF16),
        compiler_params=_params(("parallel", "parallel")),
    )(c_idx, g, recv)


def _chip_sum(s1, recv, chip_idx, *, name):
    _, r, w = s1.shape
    tm = _tile(r, 256) if r % 8 == 0 else r

    def body(k_ref, s_ref, r_ref, o_ref):
        acc = s_ref[...].astype(F32)
        for j in range(3):
            acc = acc + r_ref[j].astype(F32)
        o_ref[...] = acc

    return pl.pallas_call(
        body, name=name,
        grid_spec=pltpu.PrefetchScalarGridSpec(
            num_scalar_prefetch=1, grid=(r // tm,),
            in_specs=[pl.BlockSpec((None, tm, w), lambda i, k_ref: (k_ref[0], i, 0)),
                      pl.BlockSpec((3, tm, w), lambda i, k_ref: (0, i, 0))],
            out_specs=pl.BlockSpec((tm, w), lambda i, k_ref: (i, 0))),
        out_shape=jax.ShapeDtypeStruct((r, w), F32),
        compiler_params=_params(("parallel",)),
    )(chip_idx, s1, recv)


def _adam_math(w, g, m, v):
    m = ADAM_B1 * m + (1.0 - ADAM_B1) * g
    v = ADAM_B2 * v + (1.0 - ADAM_B2) * (g * g)
    m_hat = m / (1.0 - ADAM_B1 ** ADAM_STEP)
    v_hat = v / (1.0 - ADAM_B2 ** ADAM_STEP)
    delta = -ADAM_LR * (m_hat / (jnp.sqrt(v_hat) + ADAM_EPS) + ADAM_WD * w)
    return delta, m, v


def _adamw(w, mine, other, c_idx, m, v, *, name):
    r, cw = w.shape
    hr = r // 2
    tm = _row_tile(hr, 9 * cw * 4)

    def body(c_ref, w_ref, a_ref, b_ref, m_ref, v_ref, g_ref, d_ref, nm_ref, nv_ref):
        g = jnp.where(pl.program_id(0) == c_ref[0], a_ref[...], b_ref[...])
        g_ref[...] = g
        d_ref[...], nm_ref[...], nv_ref[...] = _adam_math(w_ref[...], g, m_ref[...], v_ref[...])

    full = pl.BlockSpec((None, tm, cw), lambda h, i, c_ref: (h, i, 0))
    half = pl.BlockSpec((tm, cw), lambda h, i, c_ref: (i, 0))
    outs = pl.pallas_call(
        body, name=name,
        grid_spec=pltpu.PrefetchScalarGridSpec(
            num_scalar_prefetch=1, grid=(2, hr // tm),
            in_specs=[full, half, half, full, full], out_specs=[full] * 4),
        out_shape=[jax.ShapeDtypeStruct((2, hr, cw), F32)] * 4,
        compiler_params=_params(("parallel", "parallel")),
    )(c_idx, w.reshape(2, hr, cw), mine, other, m.reshape(2, hr, cw), v.reshape(2, hr, cw))
    return [o.reshape(r, cw) for o in outs]


def _adamw_ada(cact_t, dada, w, m, v):
    r, cw = w.shape
    nb = cact_t.shape[1]
    tm = _tile(r, 256)
    tn = _tile(cw, 1024)

    def body(a_ref, d_ref, w_ref, m_ref, v_ref, g_ref, dl_ref, nm_ref, nv_ref):
        a = a_ref[...]
        d = d_ref[...]
        g = a[:, 0:1] * d[0:1, :]
        for b in range(1, nb):
            g = g + a[:, b:b + 1] * d[b:b + 1, :]
        g_ref[...] = g
        dl_ref[...], nm_ref[...], nv_ref[...] = _adam_math(w_ref[...], g, m_ref[...], v_ref[...])

    blk = pl.BlockSpec((tm, tn), lambda i, j: (i, j))
    return pl.pallas_call(
        body, name="adamw_ada", grid=(r // tm, cw // tn),
        in_specs=[pl.BlockSpec((tm, nb), lambda i, j: (i, 0)), pl.BlockSpec((nb, tn), lambda i, j: (0, j)), blk, blk, blk],
        out_specs=[blk] * 4, out_shape=[jax.ShapeDtypeStruct((r, cw), F32)] * 4,
        compiler_params=_params(("parallel", "parallel")),
    )(cact_t, dada, w, m, v)


def _adamw_vec(parts, w, m, v):
    n = w.shape[1]

    def body(p_ref, w_ref, m_ref, v_ref, g_ref, d_ref, nm_ref, nv_ref):
        p = p_ref[...]
        g = p[0:1, :]
        for b in range(1, N_DEV):
            g = g + p[b:b + 1, :]
        g_ref[...] = g
        d_ref[...], nm_ref[...], nv_ref[...] = _adam_math(w_ref[...], g, m_ref[...], v_ref[...])

    return pl.pallas_call(
        body, name="adamw_vec", out_shape=[jax.ShapeDtypeStruct((1, n), F32)] * 4,
        compiler_params=pltpu.CompilerParams(vmem_limit_bytes=VMEM_LIMIT),
    )(parts, w, m, v)


def _cols_from_chips(g8, rows):
    cs = g8.shape[-1]
    return g8.reshape(4, rows, cs).transpose(1, 0, 2).reshape(rows, 4 * cs)


def _cols_to_pieces(g):
    rows, c4 = g.shape
    return g.reshape(2, rows // 2, 4, c4 // 4).transpose(0, 2, 1, 3)


def _rows_to_pieces(g):
    r4, cols = g.shape
    return g.reshape(4, 2, r4 // 8, cols).transpose(1, 0, 2, 3)


def _pad_cols(a, w):
    return jnp.pad(a, ((0, 0), (0, w - a.shape[1])))


def kernel(x, c, positions, w_ada, b_ada, g_norm1, g_norm2, w_in, g_q_latent, g_kv_latent, w_uq, w_ukv, g_q_head, g_k_head, w_proj_mla, w_proj_sb, w_out, w_ffn_in, w_ffn_out, loss_target, m_w_ada, m_b_ada, m_g_norm1, m_g_norm2, m_w_in, m_g_q_latent, m_g_kv_latent, m_w_uq, m_w_ukv, m_g_q_head, m_g_k_head, m_w_proj_mla, m_w_proj_sb, m_w_out, m_w_ffn_in, m_w_ffn_out, v_w_ada, v_b_ada, v_g_norm1, v_g_norm2, v_w_in, v_g_q_latent, v_g_kv_latent, v_w_uq, v_w_ukv, v_g_q_head, v_g_k_head, v_w_proj_mla, v_w_proj_sb, v_w_out, v_w_ffn_in, v_w_ffn_out):
    xi, yi, ci = _place()
    chip = 2 * xi + yi
    dev = 2 * chip + ci
    c_idx = jnp.reshape(ci, (1,)).astype(jnp.int32)
    chip_idx = jnp.reshape(chip, (1,)).astype(jnp.int32)

    x = x[0]
    tgt = loss_target[0]
    S, D = x.shape
    ql = g_q_latent.shape[1]
    assert g_kv_latent.shape[1] == ql
    mlaw = w_proj_mla.shape[1]
    nh = mlaw // HEAD
    sbw = w_proj_sb.shape[1]
    assert sbw == mlaw
    dff = w_ffn_out.shape[1] * 4
    d_in = 2 * ql + ROPE + 3 * sbw + 2 * D
    d_in_p = d_in + ROPE
    q_col = (2 * ql) // HEAD
    k_col = q_col + nh
    v_col = k_col + nh
    gla_col = (2 * ql + 3 * sbw) // D
    glb_col = gla_col + 1
    kpe_col = (d_in - ROPE) // LANE
    assert (2 * ql + 3 * sbw) % D == 0 and (d_in - ROPE) % LANE == 0

    mats = {"w_in": w_in[0], "w_uq": w_uq[0], "w_ukv": w_ukv[0], "w_proj_mla": w_proj_mla[0],
            "w_proj_sb": w_proj_sb[0], "w_out": w_out[0], "w_ffn_in": w_ffn_in[0], "w_ffn_out": w_ffn_out[0]}
    names = list(mats)
    row_sharded = {"w_out", "w_ffn_out"}

    c_all = _gather_blocks([jnp.broadcast_to(c, (8, D))], name="gather_cond", in_vmem=True)[0][:, 0, :]
    n_ada = w_ada.shape[2]
    b_shard = lax.dynamic_slice_in_dim(b_ada, chip * n_ada, n_ada, axis=1)
    ada_shard = _mm(c_all, w_ada[0], name="ada_proj", a_fn=jax.nn.silu, bias=b_shard)
    ada_all = _gather_blocks([ada_shard], name="gather_ada", in_vmem=True)[0]
    ada_rows = lax.dynamic_index_in_dim(ada_all, dev, axis=1, keepdims=False)
    ada = ada_rows[0::2].reshape(1, 4 * n_ada)
    SH1, SC1, GT1, SH2, SC2, GT2 = range(6)

    def after(dep, a):
        return a + (dep.reshape(-1)[0:1].reshape((1,) * a.ndim) * 0).astype(a.dtype)

    def fill_own(g8, own):
        return lax.dynamic_update_index_in_dim(g8, own, dev, 0)

    halves = []
    for nm in names:
        w = mats[nm]
        hr = w.shape[0] // 2
        halves.append(lax.dynamic_slice_in_dim(w, ci * hr, hr, axis=0).astype(BF16))
    half_of = dict(zip(names, halves))
    early = ["w_in", "w_uq", "w_ukv"]
    late = ["w_proj_mla", "w_proj_sb", "w_out", "w_ffn_in", "w_ffn_out"]
    early_halves = [half_of[nm] for nm in early]
    early_halves[0] = after(ada, early_halves[0])
    early_got = _gather_blocks(early_halves, name="gather_weights", in_vmem=False)
    late_halves = [half_of[nm] for nm in late]
    late_halves[0] = after(early_got[1], late_halves[0])
    late_send, late_recv, late_srcs, late_lands, late_token = _split_start(
        late_halves, [jax.ShapeDtypeStruct((N_DEV,) + h.shape, h.dtype) for h in late_halves], _gather_plan, 4,
        name="gather_late_start")
    ada = ada + late_token[0:1, 0:1]
    gathered = {nm: fill_own(g8, own) for nm, g8, own in zip(early, early_got, early_halves)}

    def full_cols(nm):
        return _cols_from_chips(gathered[nm], mats[nm].shape[0])

    w_in_f = full_cols("w_in")
    kpe0 = 2 * ql
    w_in_p = jnp.concatenate([w_in_f[:, :kpe0], w_in_f[:, kpe0 + ROPE:], w_in_f[:, kpe0:kpe0 + ROPE],
                              jnp.zeros((D, ROPE), BF16)], axis=1)
    w_uq_p = jnp.pad(full_cols("w_uq").reshape(ql, nh, QK_DIM), ((0, 0), (0, 0), (0, HEAD_PAD - QK_DIM))
                     ).reshape(ql, nh * HEAD_PAD)
    w_ukv4 = full_cols("w_ukv").reshape(ql, nh, 2 * HEAD)
    w_ukv_p = jnp.concatenate([w_ukv4[:, :, :HEAD].reshape(ql, mlaw), w_ukv4[:, :, HEAD:].reshape(ql, mlaw)], axis=1)

    half = ROPE // 2
    freqs = ROPE_THETA ** (-jnp.arange(half, dtype=F32) / half)
    ang = positions[0].astype(F32)[:, None] * freqs
    cos, sin = jnp.cos(ang), jnp.sin(ang)
    one = jnp.ones((S, NOPE), F32)
    zero = jnp.zeros((S, NOPE), F32)
    zh = jnp.zeros((S, half), F32)
    tabs = (jnp.concatenate([one, cos, cos, one[:, :HEAD_PAD - QK_DIM]], axis=1),
            jnp.concatenate([zero, zh, sin, zero[:, :HEAD_PAD - QK_DIM]], axis=1),
            jnp.concatenate([zero, -sin, zh, zero[:, :HEAD_PAD - QK_DIM]], axis=1))
    g_qh_p = _pad_cols(g_q_head, HEAD_PAD)
    g_kh_p = _pad_cols(g_k_head, HEAD_PAD)

    h1 = _rmsmod(x, g_norm1, ada, SC1, SH1, name="rmsmod1")
    proj = _mm(h1, w_in_p, name="mm_proj", tn=640)
    cqn, ckvn = _latent_norm(proj, g_q_latent, g_kv_latent, ql)
    q0 = _mm(cqn, w_uq_p, name="mm_q_up")
    kv0 = _mm(ckvn, w_ukv_p, name="mm_kv_up")
    q = _q_prep(q0, g_qh_p, tabs, nh)
    k = _k_prep(kv0, proj, kpe_col, g_kh_p, tabs, nh)
    y_a, lse = _mla_fwd(q, k, kv0, nh)
    y_b = _sb_fwd(proj, q_col, k_col, v_col, nh)
    late_srcs, late_lands = _split_wait(late_send, late_recv, late_srcs, late_lands, y_b, _gather_plan,
                                        name="gather_late_wait")
    late_got = _gather_forward(late_lands, name="gather_late_forward")
    gathered.update({nm: fill_own(g8, own) for nm, g8, own in zip(late, late_got, late_srcs)})
    w_pm = full_cols("w_proj_mla")
    w_ps = full_cols("w_proj_sb")
    w_o = gathered["w_out"].reshape(D, D)
    w_fi = full_cols("w_ffn_in")
    w_fo = gathered["w_ffn_out"].reshape(dff, D)
    pa = _mm(y_a, w_pm, name="mm_proj_mla")
    pb = _mm(y_b, w_ps, name="mm_proj_sb")
    merged = _gate_merge(pa, pb, proj, gla_col, glb_col)
    o = _mm(merged, w_o, name="mm_out")
    x2, h2 = _resid_rmsmod(x, o, g_norm2, ada, GT1, SC2, SH2)
    ff = _mm(h2, w_fi, name="mm_ffn_in", out_dtype=BF16)
    act = _swiglu(ff, dff)
    f = _mm(act, w_fo, name="mm_ffn_out")
    dy, df, red_l, loss_p = _loss_head(x2, f, tgt, ada, GT2)
    loss = lax.psum(loss_p[0, 0], ("x", "y", "c"))

    dact = _mm(df, w_fo, name="mm_d_act", tb=True)
    gw_fo = _mm(act, df, name="mm_gw_ffn_out", ta=True, out_dtype=BF16)
    dff_ = _swiglu_bwd(dact, ff, dff)
    dh2 = _mm(dff_, w_fi, name="mm_d_h2", tb=True)
    gw_fi = _mm(h2, dff_, name="mm_gw_ffn_in", ta=True, out_dtype=BF16)

    def pair_sums(nms, grads, tag):
        pcs = [(_rows_to_pieces if nm in row_sharded else _cols_to_pieces)(g) for nm, g in zip(nms, grads)]
        got = _sibling_swap(pcs, name="rs_sibling_swap_" + tag)
        return [_pair_sum(p, r, c_idx, name="rs_pair_sum_" + nm) for p, r, nm in zip(pcs, got, nms)]

    ffn = ["w_ffn_in", "w_ffn_out"]
    ffn_pair = pair_sums(ffn, [gw_fi, gw_fo], "ffn")
    ffn_send, ffn_recv, ffn_pair, ffn_lands, ffn_token = _split_start(
        ffn_pair, [jax.ShapeDtypeStruct((3,) + p.shape[1:], p.dtype) for p in ffn_pair], _exchange_plan, 3,
        name="rs_exchange_ffn_start")
    ada = ada + ffn_token[0:1, 0:1]
    dx2, do, red_2 = _rmsmod2_bwd(dh2, x2, dy, o, g_norm2, ada, SC2, GT1)
    dmerged = _mm(do, w_o, name="mm_d_merged", tb=True)
    gw_o = _mm(merged, do, name="mm_gw_out", ta=True, out_dtype=BF16)
    dpa, dpb, dgla, dglb = _gate_bwd(dmerged, pa, pb, proj, gla_col, glb_col)
    dya = _mm(dpa, w_pm, name="mm_d_ya", tb=True)
    gw_pm = _mm(y_a, dpa, name="mm_gw_proj_mla", ta=True, out_dtype=BF16)
    dyb = _mm(dpb, w_ps, name="mm_d_yb", tb=True)
    gw_ps = _mm(y_b, dpb, name="mm_gw_proj_sb", ta=True, out_dtype=BF16)
    mid = ["w_proj_mla", "w_proj_sb", "w_out"]
    mid_pair = pair_sums(mid, [gw_pm, gw_ps, gw_o], "mid")
    mid_send, mid_recv, mid_pair, mid_lands, mid_token = _split_start(
        mid_pair, [jax.ShapeDtypeStruct((3,) + p.shape[1:], p.dtype) for p in mid_pair], _exchange_plan, 3,
        name="rs_exchange_mid_start")
    lse = lse + mid_token[0:1, 0:1]
    dq, dk, dv = _mla_bwd(q, k, kv0, y_a, dya, lse, nh)
    dq_sb, dk_sb, dv_sb = _sb_bwd(proj, q_col, k_col, v_col, dyb, nh)
    dq0, red_qh = _q_prep_bwd(dq, q0, g_qh_p, tabs, nh)
    dkv0, dkpe, red_kh = _k_prep_bwd(dk, dv, kv0, proj, kpe_col, g_kh_p, tabs, nh)
    dcqn = _mm(dq0, w_uq_p, name="mm_d_cqn", tb=True)
    gw_uq_p = _mm(cqn, dq0, name="mm_gw_uq", ta=True, out_dtype=BF16)
    dckvn = _mm(dkv0, w_ukv_p, name="mm_d_ckvn", tb=True)
    gw_ukv_p = _mm(ckvn, dkv0, name="mm_gw_ukv", ta=True, out_dtype=BF16)
    dcq, dckv, red_lat = _latent_norm_bwd(dcqn, dckvn, proj, g_q_latent, g_kv_latent, ql)
    dproj = jnp.concatenate([dcq, dckv, dq_sb.astype(BF16), dk_sb.astype(BF16), dv_sb.astype(BF16),
                             dgla, dglb, dkpe], axis=1)
    dh1 = _mm(dproj, w_in_p, name="mm_d_h1", tb=True)
    gw_in_p = _mm(h1, dproj, name="mm_gw_in", ta=True, out_dtype=BF16, tn=640)
    grad_x, red_1 = _rmsmod1_bwd(dh1, x, dx2, g_norm1, ada, SC1)

    nsb = d_in_p - 2 * ROPE
    gw_in = jnp.concatenate([gw_in_p[:, :kpe0], gw_in_p[:, nsb:nsb + ROPE], gw_in_p[:, kpe0:nsb]], axis=1)
    gw_uq = gw_uq_p.reshape(ql, nh, HEAD_PAD)[:, :, :QK_DIM].reshape(ql, nh * QK_DIM)
    gw_ukv = jnp.concatenate([gw_ukv_p[:, :mlaw].reshape(ql, nh, HEAD), gw_ukv_p[:, mlaw:].reshape(ql, nh, HEAD)],
                             axis=2).reshape(ql, 2 * mlaw)
    last = ["w_in", "w_uq", "w_ukv"]
    assert last + mid + ffn == names

    last_pair = pair_sums(last, [gw_in, gw_uq, gw_ukv], "last")
    last_chips = _chip_exchange(last_pair, name="rs_chip_exchange")
    mid_pair, mid_chips = _split_wait(mid_send, mid_recv, mid_pair, mid_lands, grad_x, _exchange_plan,
                                      name="rs_exchange_mid_wait")
    ffn_pair, ffn_chips = _split_wait(ffn_send, ffn_recv, ffn_pair, ffn_lands, grad_x, _exchange_plan,
                                      name="rs_exchange_ffn_wait")
    reduced = [_chip_sum(s, r, chip_idx, name="rs_chip_sum_" + nm)
               for s, r, nm in zip(last_pair + mid_pair + ffn_pair, last_chips + mid_chips + ffn_chips, names)]
    from_sibling2 = _sibling_swap(reduced, name="rs_sibling_send", whole=True)

    vec_names = ["b_ada", "g_norm1", "g_norm2", "g_q_latent", "g_kv_latent", "g_q_head", "g_k_head"]
    vec_w = dict(b_ada=b_ada, g_norm1=g_norm1, g_norm2=g_norm2, g_q_latent=g_q_latent, g_kv_latent=g_kv_latent,
                 g_q_head=g_q_head, g_k_head=g_k_head)
    vec_m = dict(b_ada=m_b_ada, g_norm1=m_g_norm1, g_norm2=m_g_norm2, g_q_latent=m_g_q_latent,
                 g_kv_latent=m_g_kv_latent, g_q_head=m_g_q_head, g_k_head=m_g_k_head)
    vec_v = dict(b_ada=v_b_ada, g_norm1=v_g_norm1, g_norm2=v_g_norm2, g_q_latent=v_g_q_latent,
                 g_kv_latent=v_g_kv_latent, g_q_head=v_g_q_head, g_k_head=v_g_k_head)
    d_ada = jnp.concatenate([red_1[0:1], red_1[1:2], red_2[3:4], red_2[0:1], red_2[1:2], red_l[0:1]], axis=1)
    vec_parts = dict(b_ada=d_ada, g_norm1=red_1[2:3], g_norm2=red_2[2:3], g_q_latent=red_lat[0:1],
                     g_kv_latent=red_lat[1:2], g_q_head=red_qh[0:1], g_k_head=red_kh[0:1])
    widths = [-(-vec_w[nm].shape[1] // LANE) * LANE for nm in vec_names]
    offs = [sum(widths[:i]) for i in range(len(widths))]
    pack = lambda d: jnp.concatenate([_pad_cols(d[nm][:, :vec_w[nm].shape[1]], wd) for nm, wd in zip(vec_names, widths)], axis=1)
    nvec = sum(widths)
    parts_all = _gather_blocks([jnp.broadcast_to(pack(vec_parts), (8, nvec))], name="gather_vec_grads",
                               in_vmem=True)[0][:, 0, :]
    gvec, dvec, nmvec, nvvec = _adamw_vec(parts_all, pack(vec_w), pack(vec_m), pack(vec_v))
    unpack = lambda a: {nm: a[:, o_:o_ + vec_w[nm].shape[1]] for nm, o_ in zip(vec_names, offs)}
    gvec, dvec, nmvec, nvvec = unpack(gvec), unpack(dvec), unpack(nmvec), unpack(nvvec)

    dada_all = lax.dynamic_slice_in_dim(parts_all[:, :6 * D], chip * n_ada, n_ada, axis=1)
    cact_t = jax.nn.silu(c_all).T
    g_ada, d_ada_w, nm_ada, nv_ada = _adamw_ada(cact_t, dada_all, w_ada[0], m_w_ada[0], v_w_ada[0])

    ms = dict(w_in=m_w_in, w_uq=m_w_uq, w_ukv=m_w_ukv, w_proj_mla=m_w_proj_mla, w_proj_sb=m_w_proj_sb,
              w_out=m_w_out, w_ffn_in=m_w_ffn_in, w_ffn_out=m_w_ffn_out)
    vs = dict(w_in=v_w_in, w_uq=v_w_uq, w_ukv=v_w_ukv, w_proj_mla=v_w_proj_mla, w_proj_sb=v_w_proj_sb,
              w_out=v_w_out, w_ffn_in=v_w_ffn_in, w_ffn_out=v_w_ffn_out)
    G, DL, NM, NV = {}, {}, {}, {}
    for nm, mine, other in zip(names, reduced, from_sibling2):
        g_, d_, m_, v_ = _adamw(mats[nm], mine, other, c_idx, ms[nm][0], vs[nm][0], name="adamw_" + nm)
        G[nm], DL[nm], NM[nm], NV[nm] = g_[None], d_[None], m_[None], v_[None]
    G["w_ada"], DL["w_ada"], NM["w_ada"], NV["w_ada"] = g_ada[None], d_ada_w[None], nm_ada[None], nv_ada[None]
    for nm in vec_names:
        G[nm], DL[nm], NM[nm], NV[nm] = gvec[nm], dvec[nm], nmvec[nm], nvvec[nm]

    order = ["w_ada", "b_ada", "g_norm1", "g_norm2", "w_in", "g_q_latent", "g_kv_latent", "w_uq", "w_ukv",
             "g_q_head", "g_k_head", "w_proj_mla", "w_proj_sb", "w_out", "w_ffn_in", "w_ffn_out"]
    return (loss, grad_x[None], *[G[n] for n in order], *[DL[n] for n in order],
            *[NM[n] for n in order], *[NV[n] for n in order])
```

```python
import functools
import math

import jax
import jax.numpy as jnp
from jax import lax
from jax.experimental import pallas as pl
from jax.experimental.pallas import tpu as pltpu

F32 = jnp.float32
BF16 = jnp.bfloat16
MESH = pl.DeviceIdType.MESH

EPS = 1e-6
ROPE_THETA = 10000.0
NOPE = 128
ROPE = 64
QK_DIM = NOPE + ROPE
HEAD_PAD = 256
HEAD = 128
N_DEV = 8
LANE = 128
VMEM_LIMIT = 48 * 1024 * 1024

ADAM_LR = 0.001
ADAM_B1 = 0.9
ADAM_B2 = 0.999
ADAM_EPS = 1e-08
ADAM_WD = 0.01
ADAM_STEP = 10


def _tile(n, target):
    if n <= target:
        return n
    t = (target // LANE) * LANE
    while t >= LANE:
        if n % t == 0:
            return t
        t -= LANE
    return n


def _row_tile(rows, row_bytes, budget=24 * 1024 * 1024):
    cap = max(8, budget // (2 * row_bytes))
    best = None
    for t in range(8, min(rows, cap) + 1, 8):
        if rows % t == 0:
            best = t
    return best if best is not None else rows


def _params(sem):
    return pltpu.CompilerParams(dimension_semantics=sem, vmem_limit_bytes=VMEM_LIMIT)


def _rows(tm, w, col=0):
    return pl.BlockSpec((tm, w), lambda i: (i, col))


def _vec(w, col=0, rows=1):
    return pl.BlockSpec((rows, w), lambda i: (0, col))


MM_VMEM_BUDGET = 36 * 1024 * 1024


def _mm(a, b, *, name, ta=False, tb=False, out_dtype=F32, a_fn=None, bias=None, tm=1024, tn=1024):
    M = a.shape[1] if ta else a.shape[0]
    K = a.shape[0] if ta else a.shape[1]
    N = b.shape[0] if tb else b.shape[1]
    assert K == (b.shape[1] if tb else b.shape[0]), (a.shape, b.shape, ta, tb)
    tm, tn = _tile(M, tm), _tile(N, tn)
    sa, sb, so = a.dtype.itemsize, b.dtype.itemsize, jnp.dtype(out_dtype).itemsize

    def fits(tk):
        return 2 * tk * (tm * sa + tn * sb) + tm * tn * (2 * so + 4) <= MM_VMEM_BUDGET

    tk = K
    while not fits(tk):
        smaller = _tile(K, tk - LANE)
        if smaller >= tk:
            break
        tk = smaller
    nk = K // tk
    dn = (((0 if ta else 1,), (1 if tb else 0,)), ((), ()))
    b_outer = nk == 1 and a.size * sa * (N // tn) < b.size * sb * (M // tm)

    def body(*refs):
        a_ref, b_ref = refs[:2]
        bias_ref = refs[2] if bias is not None else None
        o_ref = refs[3 if bias is not None else 2]
        av = a_ref[...]
        if a_fn is not None:
            av = a_fn(av.astype(F32))
        part = lax.dot_general(av.astype(BF16), b_ref[...].astype(BF16), dn, preferred_element_type=F32)

        def finish(r):
            if bias is not None:
                r = r + bias_ref[...]
            o_ref[...] = r.astype(o_ref.dtype)

        if nk == 1:
            finish(part)
        else:
            acc_ref = refs[-1]
            k = pl.program_id(2)

            @pl.when(k == 0)
            def _():
                acc_ref[...] = part

            @pl.when(k > 0)
            def _():
                acc_ref[...] += part

            @pl.when(k == nk - 1)
            def _():
                finish(acc_ref[...])

    def ij(g0, g1):
        return (g1, g0) if b_outer else (g0, g1)

    def amap(g0, g1, k):
        i, _ = ij(g0, g1)
        return (k, i) if ta else (i, k)

    def bmap(g0, g1, k):
        _, j = ij(g0, g1)
        return (j, k) if tb else (k, j)

    in_specs = [pl.BlockSpec((tk, tm) if ta else (tm, tk), amap), pl.BlockSpec((tn, tk) if tb else (tk, tn), bmap)]
    args = [a, b]
    if bias is not None:
        in_specs.append(pl.BlockSpec((1, tn), lambda g0, g1, k: (0, ij(g0, g1)[1])))
        args.append(bias)
    grid = (N // tn, M // tm, nk) if b_outer else (M // tm, N // tn, nk)
    return pl.pallas_call(
        body, name=name, grid=grid, in_specs=in_specs,
        out_specs=pl.BlockSpec((tm, tn), lambda g0, g1, k: ij(g0, g1)),
        out_shape=jax.ShapeDtypeStruct((M, N), out_dtype),
        scratch_shapes=[pltpu.VMEM((tm, tn), F32)] if nk > 1 else [],
        compiler_params=_params(("parallel", "parallel", "arbitrary")),
    )(*args)


def _rms_rows(v):
    return lax.rsqrt(jnp.mean(v * v, axis=-1, keepdims=True) + EPS)


def _rmsmod(x, g, ada, sc_col, sh_col, *, name):
    S, D = x.shape
    tm = _tile(S, 256)

    def body(x_ref, g_ref, sc_ref, sh_ref, h_ref):
        xv = x_ref[...]
        h = (xv * _rms_rows(xv) * g_ref[...]) * (1.0 + sc_ref[...]) + sh_ref[...]
        h_ref[...] = h.astype(h_ref.dtype)

    return pl.pallas_call(
        body, name=name, grid=(S // tm,),
        in_specs=[_rows(tm, D), _vec(D), _vec(D, sc_col), _vec(D, sh_col)],
        out_specs=_rows(tm, D), out_shape=jax.ShapeDtypeStruct((S, D), BF16),
        compiler_params=_params(("parallel",)),
    )(x, g, ada, ada)


def _latent_norm(proj, g_q, g_kv, ql):
    S = proj.shape[0]
    tm = _tile(S, 512)

    def body(cq_ref, ckv_ref, gq_ref, gkv_ref, oq_ref, okv_ref):
        cq = cq_ref[...]
        oq_ref[...] = (cq * _rms_rows(cq) * gq_ref[...]).astype(BF16)
        ckv = ckv_ref[...]
        okv_ref[...] = (ckv * _rms_rows(ckv) * gkv_ref[...]).astype(BF16)

    return pl.pallas_call(
        body, name="latent_norm", grid=(S // tm,),
        in_specs=[_rows(tm, ql, 0), _rows(tm, ql, 1), _vec(ql), _vec(ql)],
        out_specs=[_rows(tm, ql), _rows(tm, ql)],
        out_shape=[jax.ShapeDtypeStruct((S, ql), BF16)] * 2,
        compiler_params=_params(("parallel",)),
    )(proj, proj, g_q, g_kv)


def _rope_fwd(y, c, s1, s2):
    return y * c + pltpu.roll(y, ROPE // 2, 1) * s1 + pltpu.roll(y, HEAD_PAD - ROPE // 2, 1) * s2


def _rope_bwd(d, c, s1, s2):
    return d * c + pltpu.roll(d * s1, HEAD_PAD - ROPE // 2, 1) + pltpu.roll(d * s2, ROPE // 2, 1)


def _head_rms(v):
    return lax.rsqrt(jnp.sum(v * v, axis=-1, keepdims=True) * (1.0 / QK_DIM) + EPS)


def _q_prep(q0, g_qh, tabs, nh):
    S = q0.shape[0]
    tm = _tile(S, 256)

    def body(q_ref, g_ref, c_ref, s1_ref, s2_ref, o_ref):
        c, s1, s2, g = c_ref[...], s1_ref[...], s2_ref[...], g_ref[...]
        for h in range(nh):
            sl = slice(h * HEAD_PAD, (h + 1) * HEAD_PAD)
            xs = q_ref[:, sl]
            o_ref[:, sl] = _rope_fwd(xs * _head_rms(xs) * g, c, s1, s2).astype(BF16)

    w = nh * HEAD_PAD
    return pl.pallas_call(
        body, name="mla_q_prep", grid=(S // tm,),
        in_specs=[_rows(tm, w), _vec(HEAD_PAD)] + [_rows(tm, HEAD_PAD)] * 3,
        out_specs=_rows(tm, w), out_shape=jax.ShapeDtypeStruct((S, w), BF16),
        compiler_params=_params(("parallel",)),
    )(q0, g_qh, *tabs)


def _k_prep(kv0, proj, kpe_col, g_kh, tabs, nh):
    S = kv0.shape[0]
    tm = _tile(S, 256)

    def body(kv_ref, kpe_ref, g_ref, c_ref, s1_ref, s2_ref, o_ref):
        c, s1, s2, g = c_ref[...], s1_ref[...], s2_ref[...], g_ref[...]
        kpe = kpe_ref[...]
        for h in range(nh):
            k0 = jnp.concatenate([kv_ref[:, h * HEAD:(h + 1) * HEAD], kpe], axis=1)
            o_ref[:, h * HEAD_PAD:(h + 1) * HEAD_PAD] = _rope_fwd(k0 * _head_rms(k0) * g, c, s1, s2).astype(BF16)

    return pl.pallas_call(
        body, name="mla_k_prep", grid=(S // tm,),
        in_specs=[_rows(tm, nh * HEAD, 0), _rows(tm, LANE, kpe_col), _vec(HEAD_PAD)] + [_rows(tm, HEAD_PAD)] * 3,
        out_specs=_rows(tm, nh * HEAD_PAD), out_shape=jax.ShapeDtypeStruct((S, nh * HEAD_PAD), BF16),
        compiler_params=_params(("parallel",)),
    )(kv0, proj, g_kh, *tabs)


def _gate_merge(pa, pb, proj, gla_col, glb_col):
    S, D = pa.shape
    tm = _tile(S, 256)

    def body(pa_ref, pb_ref, ga_ref, gb_ref, o_ref):
        o_ref[...] = (jax.nn.sigmoid(ga_ref[...]) * pa_ref[...] + jax.nn.sigmoid(gb_ref[...]) * pb_ref[...]).astype(BF16)

    return pl.pallas_call(
        body, name="gate_merge", grid=(S // tm,),
        in_specs=[_rows(tm, D), _rows(tm, D), _rows(tm, D, gla_col), _rows(tm, D, glb_col)],
        out_specs=_rows(tm, D), out_shape=jax.ShapeDtypeStruct((S, D), BF16),
        compiler_params=_params(("parallel",)),
    )(pa, pb, proj, proj)


def _resid_rmsmod(x, o, g, ada, gt_col, sc_col, sh_col):
    S, D = x.shape
    tm = _tile(S, 256)

    def body(x_ref, o_ref, g_ref, gt_ref, sc_ref, sh_ref, x2_ref, h_ref):
        x2 = x_ref[...] + gt_ref[...] * o_ref[...]
        x2_ref[...] = x2
        h_ref[...] = ((x2 * _rms_rows(x2) * g_ref[...]) * (1.0 + sc_ref[...]) + sh_ref[...]).astype(BF16)

    return pl.pallas_call(
        body, name="resid_rmsmod2", grid=(S // tm,),
        in_specs=[_rows(tm, D), _rows(tm, D), _vec(D), _vec(D, gt_col), _vec(D, sc_col), _vec(D, sh_col)],
        out_specs=[_rows(tm, D), _rows(tm, D)],
        out_shape=[jax.ShapeDtypeStruct((S, D), F32), jax.ShapeDtypeStruct((S, D), BF16)],
        compiler_params=_params(("parallel",)),
    )(x, o, g, ada, ada, ada)


def _swiglu(ff, dff_half):
    S = ff.shape[0]
    tm = _tile(S, 256)

    def body(g_ref, u_ref, o_ref):
        o_ref[...] = (jax.nn.silu(g_ref[...].astype(F32)) * u_ref[...].astype(F32)).astype(BF16)

    return pl.pallas_call(
        body, name="swiglu", grid=(S // tm,),
        in_specs=[_rows(tm, dff_half, 0), _rows(tm, dff_half, 1)],
        out_specs=_rows(tm, dff_half), out_shape=jax.ShapeDtypeStruct((S, dff_half), BF16),
        compiler_params=_params(("parallel",)),
    )(ff, ff)


def _loss_head(x2, f, tgt, ada, gt_col):
    S, D = x2.shape
    tm = _tile(S, 256)

    def body(x2_ref, f_ref, t_ref, gt_ref, dy_ref, df_ref, red_ref, loss_ref):
        @pl.when(pl.program_id(0) == 0)
        def _():
            red_ref[...] = jnp.zeros_like(red_ref)
            loss_ref[...] = jnp.zeros_like(loss_ref)

        fv = f_ref[...]
        gt = gt_ref[...]
        err = x2_ref[...] + gt * fv - t_ref[...]
        dy = err * (1.0 / D)
        dy_ref[...] = dy
        df_ref[...] = (dy * gt).astype(BF16)
        red_ref[0:1, :] += jnp.sum(dy * fv, axis=0, keepdims=True)
        loss_ref[...] += (0.5 / D) * jnp.sum(err * err)

    return pl.pallas_call(
        body, name="loss_head", grid=(S // tm,),
        in_specs=[_rows(tm, D), _rows(tm, D), _rows(tm, D), _vec(D, gt_col)],
        out_specs=[_rows(tm, D), _rows(tm, D), _vec(D, rows=8), _vec(LANE, rows=8)],
        out_shape=[jax.ShapeDtypeStruct((S, D), F32), jax.ShapeDtypeStruct((S, D), BF16),
                   jax.ShapeDtypeStruct((8, D), F32), jax.ShapeDtypeStruct((8, LANE), F32)],
        compiler_params=_params(("arbitrary",)),
    )(x2, f, tgt, ada)


def _swiglu_bwd(dact, ff, dff_half):
    S = ff.shape[0]
    tm = _tile(S, 128)

    def body(d_ref, g_ref, u_ref, o_ref):
        d = d_ref[...]
        g = g_ref[...].astype(F32)
        u = u_ref[...].astype(F32)
        sg = jax.nn.sigmoid(g)
        o_ref[:, :dff_half] = (d * u * sg * (1.0 + g * (1.0 - sg))).astype(BF16)
        o_ref[:, dff_half:] = (d * g * sg).astype(BF16)

    return pl.pallas_call(
        body, name="swiglu_bwd", grid=(S // tm,),
        in_specs=[_rows(tm, dff_half), _rows(tm, dff_half, 0), _rows(tm, dff_half, 1)],
        out_specs=_rows(tm, 2 * dff_half), out_shape=jax.ShapeDtypeStruct((S, 2 * dff_half), BF16),
        compiler_params=_params(("parallel",)),
    )(dact, ff, ff)


def _rmsmod2_bwd(dh2, x2, dy, o, g, ada, sc_col, gt_col):
    S, D = x2.shape
    tm = _tile(S, 256)

    def body(dh_ref, x2_ref, dy_ref, o_ref, g_ref, sc_ref, gt_ref, dx_ref, do_ref, red_ref):
        @pl.when(pl.program_id(0) == 0)
        def _():
            red_ref[...] = jnp.zeros_like(red_ref)

        dh = dh_ref[...]
        x2 = x2_ref[...]
        gv = g_ref[...]
        mod = 1.0 + sc_ref[...]
        r = _rms_rows(x2)
        xn = x2 * r
        t = dh * xn
        red_ref[0:1, :] += jnp.sum(dh, axis=0, keepdims=True)
        red_ref[1:2, :] += jnp.sum(t * gv, axis=0, keepdims=True)
        red_ref[2:3, :] += jnp.sum(t * mod, axis=0, keepdims=True)
        dxn = dh * gv * mod
        dx = dy_ref[...] + r * (dxn - xn * jnp.mean(dxn * xn, axis=-1, keepdims=True))
        dx_ref[...] = dx
        red_ref[3:4, :] += jnp.sum(dx * o_ref[...], axis=0, keepdims=True)
        do_ref[...] = (dx * gt_ref[...]).astype(BF16)

    return pl.pallas_call(
        body, name="rmsmod2_bwd", grid=(S // tm,),
        in_specs=[_rows(tm, D)] * 4 + [_vec(D), _vec(D, sc_col), _vec(D, gt_col)],
        out_specs=[_rows(tm, D), _rows(tm, D), _vec(D, rows=8)],
        out_shape=[jax.ShapeDtypeStruct((S, D), F32), jax.ShapeDtypeStruct((S, D), BF16),
                   jax.ShapeDtypeStruct((8, D), F32)],
        compiler_params=_params(("arbitrary",)),
    )(dh2, x2, dy, o, g, ada, ada)


def _rmsmod1_bwd(dh, x, dx2, g, ada, sc_col):
    S, D = x.shape
    tm = _tile(S, 256)

    def body(dh_ref, x_ref, dx2_ref, g_ref, sc_ref, gx_ref, red_ref):
        @pl.when(pl.program_id(0) == 0)
        def _():
            red_ref[...] = jnp.zeros_like(red_ref)

        dh = dh_ref[...]
        xv = x_ref[...]
        gv = g_ref[...]
        mod = 1.0 + sc_ref[...]
        r = _rms_rows(xv)
        xn = xv * r
        t = dh * xn
        red_ref[0:1, :] += jnp.sum(dh, axis=0, keepdims=True)
        red_ref[1:2, :] += jnp.sum(t * gv, axis=0, keepdims=True)
        red_ref[2:3, :] += jnp.sum(t * mod, axis=0, keepdims=True)
        dxn = dh * gv * mod
        gx_ref[...] = dx2_ref[...] + r * (dxn - xn * jnp.mean(dxn * xn, axis=-1, keepdims=True))

    return pl.pallas_call(
        body, name="rmsmod1_bwd", grid=(S // tm,),
        in_specs=[_rows(tm, D)] * 3 + [_vec(D), _vec(D, sc_col)],
        out_specs=[_rows(tm, D), _vec(D, rows=8)],
        out_shape=[jax.ShapeDtypeStruct((S, D), F32), jax.ShapeDtypeStruct((8, D), F32)],
        compiler_params=_params(("arbitrary",)),
    )(dh, x, dx2, g, ada)


def _gate_bwd(dm, pa, pb, proj, gla_col, glb_col):
    S, D = pa.shape
    tm = _tile(S, 256)

    def body(dm_ref, pa_ref, pb_ref, la_ref, lb_ref, dpa_ref, dpb_ref, dla_ref, dlb_ref):
        dm_ = dm_ref[...]
        ga = jax.nn.sigmoid(la_ref[...])
        gb = jax.nn.sigmoid(lb_ref[...])
        dpa_ref[...] = (dm_ * ga).astype(BF16)
        dpb_ref[...] = (dm_ * gb).astype(BF16)
        dla_ref[...] = (dm_ * pa_ref[...] * ga * (1.0 - ga)).astype(BF16)
        dlb_ref[...] = (dm_ * pb_ref[...] * gb * (1.0 - gb)).astype(BF16)

    return pl.pallas_call(
        body, name="gate_bwd", grid=(S // tm,),
        in_specs=[_rows(tm, D)] * 3 + [_rows(tm, D, gla_col), _rows(tm, D, glb_col)],
        out_specs=[_rows(tm, D)] * 4, out_shape=[jax.ShapeDtypeStruct((S, D), BF16)] * 4,
        compiler_params=_params(("parallel",)),
    )(dm, pa, pb, proj, proj)


def _q_prep_bwd(dq, q0, g_qh, tabs, nh):
    S = q0.shape[0]
    tm = _tile(S, 256)

    def body(dq_ref, q_ref, g_ref, c_ref, s1_ref, s2_ref, o_ref, red_ref):
        @pl.when(pl.program_id(0) == 0)
        def _():
            red_ref[...] = jnp.zeros_like(red_ref)

        c, s1, s2, g = c_ref[...], s1_ref[...], s2_ref[...], g_ref[...]
        dg = jnp.zeros((1, HEAD_PAD), F32)
        for h in range(nh):
            sl = slice(h * HEAD_PAD, (h + 1) * HEAD_PAD)
            d1 = _rope_bwd(dq_ref[:, sl], c, s1, s2)
            xs = q_ref[:, sl]
            r = _head_rms(xs)
            qn = xs * r
            dg = dg + jnp.sum(d1 * qn, axis=0, keepdims=True)
            dn = d1 * g
            o_ref[:, sl] = (r * (dn - qn * (jnp.sum(dn * qn, axis=-1, keepdims=True) * (1.0 / QK_DIM)))).astype(BF16)
        red_ref[0:1, :] += dg

    w = nh * HEAD_PAD
    return pl.pallas_call(
        body, name="mla_q_prep_bwd", grid=(S // tm,),
        in_specs=[_rows(tm, w), _rows(tm, w), _vec(HEAD_PAD)] + [_rows(tm, HEAD_PAD)] * 3,
        out_specs=[_rows(tm, w), _vec(HEAD_PAD, rows=8)],
        out_shape=[jax.ShapeDtypeStruct((S, w), BF16), jax.ShapeDtypeStruct((8, HEAD_PAD), F32)],
        compiler_params=_params(("arbitrary",)),
    )(dq, q0, g_qh, *tabs)


def _k_prep_bwd(dk, dv, kv0, proj, kpe_col, g_kh, tabs, nh):
    S = kv0.shape[0]
    tm = _tile(S, 256)
    wv = nh * HEAD

    def body(dk_ref, dv_ref, kv_ref, kpe_ref, g_ref, c_ref, s1_ref, s2_ref, o_ref, dpe_ref, red_ref):
        @pl.when(pl.program_id(0) == 0)
        def _():
            red_ref[...] = jnp.zeros_like(red_ref)

        c, s1, s2, g = c_ref[...], s1_ref[...], s2_ref[...], g_ref[...]
        kpe = kpe_ref[...]
        dg = jnp.zeros((1, HEAD_PAD), F32)
        dpe = jnp.zeros((tm, LANE), F32)
        for h in range(nh):
            d1 = _rope_bwd(dk_ref[:, h * HEAD_PAD:(h + 1) * HEAD_PAD], c, s1, s2)
            k0 = jnp.concatenate([kv_ref[:, h * HEAD:(h + 1) * HEAD], kpe], axis=1)
            r = _head_rms(k0)
            kn = k0 * r
            dg = dg + jnp.sum(d1 * kn, axis=0, keepdims=True)
            dn = d1 * g
            dk0 = r * (dn - kn * (jnp.sum(dn * kn, axis=-1, keepdims=True) * (1.0 / QK_DIM)))
            o_ref[:, h * HEAD:(h + 1) * HEAD] = dk0[:, :HEAD].astype(BF16)
            dpe = dpe + dk0[:, HEAD:]
        o_ref[:, wv:] = dv_ref[...].astype(BF16)
        dpe_ref[...] = dpe.astype(BF16)
        red_ref[0:1, :] += dg

    return pl.pallas_call(
        body, name="mla_k_prep_bwd", grid=(S // tm,),
        in_specs=[_rows(tm, nh * HEAD_PAD), _rows(tm, wv), _rows(tm, wv, 0), _rows(tm, LANE, kpe_col),
                  _vec(HEAD_PAD)] + [_rows(tm, HEAD_PAD)] * 3,
        out_specs=[_rows(tm, 2 * wv), _rows(tm, LANE), _vec(HEAD_PAD, rows=8)],
        out_shape=[jax.ShapeDtypeStruct((S, 2 * wv), BF16), jax.ShapeDtypeStruct((S, LANE), BF16),
                   jax.ShapeDtypeStruct((8, HEAD_PAD), F32)],
        compiler_params=_params(("arbitrary",)),
    )(dk, dv, kv0, proj, g_kh, *tabs)


def _latent_norm_bwd(dcqn, dckvn, proj, g_q, g_kv, ql):
    S = proj.shape[0]
    tm = _tile(S, 512)

    def body(dq_ref, dkv_ref, cq_ref, ckv_ref, gq_ref, gkv_ref, oq_ref, okv_ref, red_ref):
        @pl.when(pl.program_id(0) == 0)
        def _():
            red_ref[...] = jnp.zeros_like(red_ref)

        for row, (d_ref, c_ref, g_ref, o_ref) in enumerate(((dq_ref, cq_ref, gq_ref, oq_ref),
                                                            (dkv_ref, ckv_ref, gkv_ref, okv_ref))):
            d = d_ref[...]
            cv = c_ref[...]
            r = _rms_rows(cv)
            ch = cv * r
            red_ref[row:row + 1, :] += jnp.sum(d * ch, axis=0, keepdims=True)
            dn = d * g_ref[...]
            o_ref[...] = (r * (dn - ch * jnp.mean(dn * ch, axis=-1, keepdims=True))).astype(BF16)

    return pl.pallas_call(
        body, name="latent_norm_bwd", grid=(S // tm,),
        in_specs=[_rows(tm, ql), _rows(tm, ql), _rows(tm, ql, 0), _rows(tm, ql, 1), _vec(ql), _vec(ql)],
        out_specs=[_rows(tm, ql), _rows(tm, ql), _vec(ql, rows=8)],
        out_shape=[jax.ShapeDtypeStruct((S, ql), BF16)] * 2 + [jax.ShapeDtypeStruct((8, ql), F32)],
        compiler_params=_params(("arbitrary",)),
    )(dcqn, dckvn, proj, proj, g_q, g_kv)


NEG = -1e30
ATT_TILE = 512
SB_SUB = 128
_NT = (((1,), (1,)), ((), ()))
_TN = (((0,), (0,)), ((), ()))


def _dot(a, b, dn=(((1,), (0,)), ((), ()))):
    return lax.dot_general(a, b, dn, preferred_element_type=F32)


def _key_rows(kb, t):
    return pl.ds(pl.multiple_of(kb * t, t), t)


def _diag_mask(t, strict):
    r = lax.broadcasted_iota(jnp.int32, (t, t), 0)
    c = lax.broadcasted_iota(jnp.int32, (t, t), 1)
    return c < r if strict else c <= r


def _mla_fwd(q, k, kv0, nh):
    S = q.shape[0]
    t = _tile(S, ATT_TILE)
    scale = QK_DIM ** -0.5

    def body(q_ref, k_ref, v_ref, o_ref, lse_ref):
        i = pl.program_id(1)
        qv = q_ref[...]

        def block(kb, carry, masked):
            m, l, acc = carry
            rows = _key_rows(kb, t)
            s = _dot(qv, k_ref[rows, :], _NT) * scale
            if masked:
                s = jnp.where(_diag_mask(t, False), s, NEG)
            m_new = jnp.maximum(m, jnp.max(s, axis=-1, keepdims=True))
            alpha = jnp.exp(m - m_new)
            p = jnp.exp(s - m_new)
            l = alpha * l + jnp.sum(p, axis=-1, keepdims=True)
            acc = alpha * acc + _dot(p.astype(BF16), v_ref[rows, :].astype(BF16))
            return m_new, l, acc

        init = (jnp.full((t, 1), NEG, F32), jnp.zeros((t, 1), F32), jnp.zeros((t, HEAD), F32))
        carry = lax.fori_loop(0, i, lambda kb, c: block(kb, c, False), init)
        m, l, acc = block(i, carry, True)
        o_ref[...] = acc / l
        lse_ref[...] = m + jnp.log(l)

    return pl.pallas_call(
        body, name="mla_attn_fwd", grid=(nh, S // t),
        in_specs=[pl.BlockSpec((t, HEAD_PAD), lambda h, i: (i, h)),
                  pl.BlockSpec((S, HEAD_PAD), lambda h, i: (0, h)),
                  pl.BlockSpec((S, HEAD), lambda h, i: (0, nh + h))],
        out_specs=[pl.BlockSpec((t, HEAD), lambda h, i: (i, h)),
                   pl.BlockSpec((None, t, 1), lambda h, i: (h, i, 0))],
        out_shape=[jax.ShapeDtypeStruct((S, nh * HEAD), F32), jax.ShapeDtypeStruct((nh, S, 1), F32)],
        compiler_params=_params(("parallel", "arbitrary")),
    )(q, k, kv0)


def _mla_bwd(q, k, kv0, o, do, lse, nh):
    S = q.shape[0]
    t = _tile(S, ATT_TILE)
    scale = QK_DIM ** -0.5

    def body(q_ref, k_ref, v_ref, o_ref, do_ref, lse_ref, dq_ref, dk_ref, dv_ref):
        i = pl.program_id(1)

        @pl.when(i == 0)
        def _():
            dk_ref[...] = jnp.zeros_like(dk_ref)
            dv_ref[...] = jnp.zeros_like(dv_ref)

        qv = q_ref[...]
        dov = do_ref[...]
        delta = jnp.sum(dov * o_ref[...], axis=-1, keepdims=True)
        dob = dov.astype(BF16)
        lse = lse_ref[...]

        def block(kb, dq, masked):
            rows = _key_rows(kb, t)
            ks = k_ref[rows, :]
            vs = v_ref[rows, :].astype(BF16)
            p = jnp.exp(_dot(qv, ks, _NT) * scale - lse)
            if masked:
                p = jnp.where(_diag_mask(t, False), p, 0.0)
            ds = (p * (_dot(dob, vs, _NT) - delta) * scale).astype(BF16)
            dk_ref[rows, :] += _dot(ds, qv, _TN)
            dv_ref[rows, :] += _dot(p.astype(BF16), dob, _TN)
            return dq + _dot(ds, ks)

        dq = lax.fori_loop(0, i, lambda kb, c: block(kb, c, False), jnp.zeros((t, HEAD_PAD), F32))
        dq_ref[...] = block(i, dq, True)

    return pl.pallas_call(
        body, name="mla_attn_bwd", grid=(nh, S // t),
        in_specs=[pl.BlockSpec((t, HEAD_PAD), lambda h, i: (i, h)),
                  pl.BlockSpec((S, HEAD_PAD), lambda h, i: (0, h)),
                  pl.BlockSpec((S, HEAD), lambda h, i: (0, nh + h)),
                  pl.BlockSpec((t, HEAD), lambda h, i: (i, h)),
                  pl.BlockSpec((t, HEAD), lambda h, i: (i, h)),
                  pl.BlockSpec((None, t, 1), lambda h, i: (h, i, 0))],
        out_specs=[pl.BlockSpec((t, HEAD_PAD), lambda h, i: (i, h)),
                   pl.BlockSpec((S, HEAD_PAD), lambda h, i: (0, h)),
                   pl.BlockSpec((S, HEAD), lambda h, i: (0, h))],
        out_shape=[jax.ShapeDtypeStruct((S, nh * HEAD_PAD), F32), jax.ShapeDtypeStruct((S, nh * HEAD_PAD), F32),
                   jax.ShapeDtypeStruct((S, nh * HEAD), F32)],
        compiler_params=_params(("parallel", "arbitrary")),
    )(q, k, kv0, o, do, lse)


def _split_dot(v, tri):
    hi = v.astype(BF16)
    lo = (v - hi.astype(F32)).astype(BF16)
    return _dot(hi, tri) + _dot(lo, tri)


def _tri(n, cmp):
    r = lax.broadcasted_iota(jnp.int32, (n, n), 0)
    c = lax.broadcasted_iota(jnp.int32, (n, n), 1)
    return jnp.where(cmp(r, c), 1.0, 0.0).astype(BF16)


def _sb_block(qv, ks, run, upper, t, scale, masked):
    z = _dot(qv, ks, _NT) * scale
    lb = jnp.minimum(z, 0.0) - jnp.log(1.0 + jnp.exp(-jnp.abs(z)))
    lom = lb - z
    mask = _diag_mask(t, True) if masked else None
    if masked:
        lom = jnp.where(mask, lom, 0.0)
    tails = []
    for sblk in reversed(range(t // SB_SUB)):
        part = lom[:, sblk * SB_SUB:(sblk + 1) * SB_SUB]
        tails.append(_split_dot(part, upper) + run)
        run = run + jnp.sum(part, axis=-1, keepdims=True)
    a = jnp.exp(lb + jnp.concatenate(tails[::-1], axis=1))
    if masked:
        a = jnp.where(mask, a, 0.0)
    return a, lb, mask, run


def _sb_fwd(proj, q_col, k_col, v_col, nh):
    S = proj.shape[0]
    t = _tile(S, ATT_TILE)
    scale = HEAD ** -0.5

    def body(q_ref, k_ref, v_ref, o_ref):
        i = pl.program_id(1)
        qv = q_ref[...].astype(BF16)
        upper = _tri(SB_SUB, lambda j, s: j > s)

        def block(kb, carry, masked):
            run, acc = carry
            rows = _key_rows(kb, t)
            a, _, _, run = _sb_block(qv, k_ref[rows, :].astype(BF16), run, upper, t, scale, masked)
            return run, acc + _dot(a.astype(BF16), v_ref[rows, :].astype(BF16))

        carry = block(i, (jnp.zeros((t, 1), F32), jnp.zeros((t, HEAD), F32)), True)
        o_ref[...] = lax.fori_loop(0, i, lambda j, c: block(i - 1 - j, c, False), carry)[1]

    return pl.pallas_call(
        body, name="sb_attn_fwd", grid=(nh, S // t),
        in_specs=[pl.BlockSpec((t, HEAD), lambda h, i: (i, q_col + h)),
                  pl.BlockSpec((S, HEAD), lambda h, i: (0, k_col + h)),
                  pl.BlockSpec((S, HEAD), lambda h, i: (0, v_col + h))],
        out_specs=pl.BlockSpec((t, HEAD), lambda h, i: (i, h)),
        out_shape=jax.ShapeDtypeStruct((S, nh * HEAD), F32),
        compiler_params=_params(("parallel", "arbitrary")),
    )(proj, proj, proj)


def _sb_bwd(proj, q_col, k_col, v_col, dy, nh):
    S = proj.shape[0]
    t = _tile(S, ATT_TILE)
    scale = HEAD ** -0.5

    def body(q_ref, k_ref, v_ref, dy_ref, dq_ref, dk_ref, dv_ref, run_ref):
        i = pl.program_id(1)

        @pl.when(i == 0)
        def _():
            dk_ref[...] = jnp.zeros_like(dk_ref)
            dv_ref[...] = jnp.zeros_like(dv_ref)

        qv = q_ref[...].astype(BF16)
        dyb = dy_ref[...].astype(BF16)
        upper = _tri(SB_SUB, lambda j, s: j > s)
        before = _tri(SB_SUB, lambda s, j: s < j)

        def suffix(kb, run, masked):
            z = _dot(qv, k_ref[_key_rows(kb, t), :].astype(BF16), _NT) * scale
            lom = jnp.minimum(z, 0.0) - jnp.log(1.0 + jnp.exp(-jnp.abs(z))) - z
            if masked:
                lom = jnp.where(_diag_mask(t, True), lom, 0.0)
            run_ref[kb] = jnp.broadcast_to(run, (t, LANE))
            return run + jnp.sum(lom, axis=-1, keepdims=True)

        run0 = suffix(i, jnp.zeros((t, 1), F32), True)
        lax.fori_loop(0, i, lambda j, r: suffix(i - 1 - j, r, False), run0)

        def block(kb, carry, masked):
            prefix, dq = carry
            rows = _key_rows(kb, t)
            ks = k_ref[rows, :].astype(BF16)
            vs = v_ref[rows, :].astype(BF16)
            a, lb, mask, _ = _sb_block(qv, ks, run_ref[kb][:, 0:1], upper, t, scale, masked)
            dl = a * _dot(dyb, vs, _NT)
            lefts = []
            for sblk in range(t // SB_SUB):
                part = dl[:, sblk * SB_SUB:(sblk + 1) * SB_SUB]
                lefts.append(_dot(part.astype(BF16), before) + prefix)
                prefix = prefix + jnp.sum(part, axis=-1, keepdims=True)
            beta = jnp.exp(lb)
            dz = dl * (1.0 - beta) - beta * jnp.concatenate(lefts, axis=1)
            if masked:
                dz = jnp.where(mask, dz, 0.0)
            dz = (dz * scale).astype(BF16)
            dk_ref[rows, :] += _dot(dz, qv, _TN)
            dv_ref[rows, :] += _dot(a.astype(BF16), dyb, _TN)
            return prefix, dq + _dot(dz, ks)

        carry = lax.fori_loop(0, i, lambda kb, c: block(kb, c, False),
                              (jnp.zeros((t, 1), F32), jnp.zeros((t, HEAD), F32)))
        dq_ref[...] = block(i, carry, True)[1]

    full = pl.BlockSpec((S, HEAD), lambda h, i: (0, h))
    tile = pl.BlockSpec((t, HEAD), lambda h, i: (i, h))
    return pl.pallas_call(
        body, name="sb_attn_bwd", grid=(nh, S // t),
        in_specs=[pl.BlockSpec((t, HEAD), lambda h, i: (i, q_col + h)),
                  pl.BlockSpec((S, HEAD), lambda h, i: (0, k_col + h)),
                  pl.BlockSpec((S, HEAD), lambda h, i: (0, v_col + h)), tile],
        out_specs=[tile, full, full],
        out_shape=[jax.ShapeDtypeStruct((S, nh * HEAD), F32)] * 3,
        scratch_shapes=[pltpu.VMEM((S // t, t, LANE), F32)],
        compiler_params=_params(("parallel", "arbitrary")),
    )(proj, proj, proj, dy)


def _place():
    return lax.axis_index("x"), lax.axis_index("y"), lax.axis_index("c")


def _other_chips(x, y):
    return [(1 - x, y), (x, 1 - y), (1 - x, 1 - y)]


def _dev_index(p):
    return 4 * p[0] + 2 * p[1] + p[2]


def _gather_blocks(blocks, *, name, in_vmem):
    n = len(blocks)
    per = 7

    def body(*refs):
        ins, outs = refs[:n], refs[n:2 * n]
        send_sems, recv_sems, local_sems = refs[2 * n:]
        x, y, c = _place()
        me, sibling = (x, y, c), (x, y, 1 - c)
        chips = _other_chips(x, y)

        def slot(a, p):
            return outs[a].at[_dev_index(p)]

        def copy(a, k, block, to, src=None):
            return pltpu.make_async_remote_copy(
                src_ref=slot(a, block) if src is None else src, dst_ref=slot(a, block),
                send_sem=send_sems.at[a * per + k], recv_sem=recv_sems.at[a * per + k],
                device_id=to, device_id_type=MESH)

        mine = [pltpu.make_async_copy(ins[a], slot(a, me), local_sems.at[a]) for a in range(n)] if in_vmem else []
        for cp in mine:
            cp.start()
        first = []
        for a in range(n):
            first.append(copy(a, 0, me, sibling, src=ins[a]))
            first += [copy(a, 1 + j, me, (*chip, c), src=ins[a]) for j, chip in enumerate(chips)]
        for cp in first:
            cp.start()
        passed = []
        for a in range(n):
            for j, chip in enumerate(chips):
                copy(a, 1 + j, (*chip, c), me).wait_recv()
                cp = copy(a, 4 + j, (*chip, c), sibling)
                cp.start()
                passed.append(cp)
        for a in range(n):
            copy(a, 0, sibling, me).wait_recv()
            for j, chip in enumerate(chips):
                copy(a, 4 + j, (*chip, 1 - c), me).wait_recv()
        for cp in first + passed:
            cp.wait_send()
        for cp in mine:
            cp.wait()

    space = pltpu.VMEM if in_vmem else pl.ANY
    spec = pl.BlockSpec(memory_space=space)
    outs = pl.pallas_call(
        body, name=name, in_specs=[spec] * n, out_specs=[spec] * n,
        out_shape=[jax.ShapeDtypeStruct((N_DEV,) + b.shape, b.dtype) for b in blocks],
        scratch_shapes=[pltpu.SemaphoreType.DMA((n * per,)), pltpu.SemaphoreType.DMA((n * per,)),
                        pltpu.SemaphoreType.DMA((n,))],
        compiler_params=pltpu.CompilerParams(vmem_limit_bytes=VMEM_LIMIT),
    )(*blocks)
    return list(outs)


def _sibling_swap(arrs, *, name, whole=False):
    n = len(arrs)

    def body(*refs):
        ins, outs = refs[:n], refs[n:2 * n]
        send_sems, recv_sems = refs[2 * n:]
        x, y, c = _place()
        copies = [pltpu.make_async_remote_copy(
            src_ref=ins[a] if whole else ins[a].at[1 - c], dst_ref=outs[a],
            send_sem=send_sems.at[a], recv_sem=recv_sems.at[a],
            device_id=(x, y, 1 - c), device_id_type=MESH) for a in range(n)]
        for cp in copies:
            cp.start()
        for cp in copies:
            cp.wait()

    spec = pl.BlockSpec(memory_space=pl.ANY)
    return list(pl.pallas_call(
        body, name=name, in_specs=[spec] * n, out_specs=[spec] * n,
        out_shape=[jax.ShapeDtypeStruct(a.shape if whole else a.shape[1:], a.dtype) for a in arrs],
        scratch_shapes=[pltpu.SemaphoreType.DMA((n,)), pltpu.SemaphoreType.DMA((n,))],
    )(*arrs))


_HBM = pl.BlockSpec(memory_space=pltpu.HBM)
_SEM = pl.BlockSpec(memory_space=pltpu.SEMAPHORE)
_EFFECT = pltpu.SideEffectType.DATAFLOW_SIDE_EFFECTING


def _in_hbm(a):
    return pltpu.with_memory_space_constraint(a, pltpu.HBM)


def _split_copies(srcs, lands, send_sems, recv_sems, plan):
    x, y, c = _place()
    copies = []
    for a, (src, land) in enumerate(zip(srcs, lands)):
        steps = plan(x, y, c)
        for k, (pick, slot, to) in enumerate(steps):
            copies.append(pltpu.make_async_remote_copy(
                src_ref=pick(src), dst_ref=slot(land), send_sem=send_sems.at[a * len(steps) + k],
                recv_sem=recv_sems.at[a * len(steps) + k], device_id=to, device_id_type=MESH))
    return copies


def _split_start(srcs, land_shapes, plan, per, *, name):
    n = len(srcs)

    def body(*refs):
        send_sems, recv_sems = refs[2 * n], refs[2 * n + 1]
        for cp in _split_copies(refs[:n], refs[n:2 * n], send_sems, recv_sems, plan):
            cp.start()
        token = refs[-1]
        token[...] = jnp.zeros_like(token)

    lands = [_in_hbm(lax.empty(s.shape, s.dtype)) for s in land_shapes]
    outs = pl.pallas_call(
        body, name=name,
        out_shape=(pltpu.SemaphoreType.DMA((n * per,)), pltpu.SemaphoreType.DMA((n * per,)),
                   *[pltpu.HBM(s.shape, s.dtype) for s in srcs], *[pltpu.HBM(s.shape, s.dtype) for s in land_shapes],
                   jax.ShapeDtypeStruct((8, LANE), F32)),
        in_specs=[_HBM] * (2 * n),
        out_specs=(_SEM, _SEM, *[_HBM] * (2 * n), pl.BlockSpec(memory_space=pltpu.VMEM)),
        input_output_aliases={i: 2 + i for i in range(2 * n)},
        compiler_params=pltpu.CompilerParams(has_side_effects=_EFFECT),
    )(*[_in_hbm(s) for s in srcs], *lands)
    return outs[0], outs[1], list(outs[2:2 + n]), list(outs[2 + n:2 + 2 * n]), outs[-1]


def _split_wait(send_sems, recv_sems, srcs, lands, after, plan, *, name):
    n = len(srcs)

    def body(*refs):
        for cp in _split_copies(refs[:n], refs[n:2 * n], refs[2 * n], refs[2 * n + 1], plan):
            cp.wait_send()
            cp.wait_recv()

    outs = pl.pallas_call(
        body, name=name,
        out_shape=(*[pltpu.HBM(s.shape, s.dtype) for s in srcs], *[pltpu.HBM(s.shape, s.dtype) for s in lands]),
        in_specs=[_HBM] * (2 * n) + [_SEM, _SEM, pl.BlockSpec(memory_space=pl.ANY)],
        out_specs=tuple([_HBM] * (2 * n)),
        input_output_aliases={i: i for i in range(2 * n)},
        compiler_params=pltpu.CompilerParams(has_side_effects=_EFFECT),
    )(*srcs, *lands, send_sems, recv_sems, after)
    return list(outs[:n]), list(outs[n:])


def _gather_plan(x, y, c):
    slot = lambda land: land.at[_dev_index((x, y, c))]
    whole = lambda src: src
    return [(whole, slot, (x, y, 1 - c))] + [(whole, slot, (px, py, c)) for px, py in _other_chips(x, y)]


def _exchange_plan(x, y, c):
    return [(lambda src, k=2 * px + py: src.at[k], lambda land, j=j: land.at[j], (px, py, c))
            for j, (px, py) in enumerate(_other_chips(x, y))]


def _gather_forward(lands, *, name):
    n = len(lands)

    def body(*refs):
        lands_in, outs = refs[:n], refs[n:2 * n]
        send_sems, recv_sems = refs[2 * n:]
        x, y, c = _place()
        copies = []
        for a in range(n):
            for j, (px, py) in enumerate(_other_chips(x, y)):
                copies.append((pltpu.make_async_remote_copy(
                    src_ref=lands_in[a].at[_dev_index((px, py, c))], dst_ref=outs[a].at[_dev_index((px, py, c))],
                    send_sem=send_sems.at[3 * a + j], recv_sem=recv_sems.at[3 * a + j],
                    device_id=(x, y, 1 - c), device_id_type=MESH), a, j, (px, py)))
        for cp, _, _, _ in copies:
            cp.start()
        for cp, a, j, (px, py) in copies:
            cp.wait_send()
            pltpu.make_async_remote_copy(
                src_ref=lands_in[a].at[_dev_index((px, py, 1 - c))], dst_ref=outs[a].at[_dev_index((px, py, 1 - c))],
                send_sem=send_sems.at[3 * a + j], recv_sem=recv_sems.at[3 * a + j],
                device_id=(x, y, 1 - c), device_id_type=MESH).wait_recv()

    spec = pl.BlockSpec(memory_space=pl.ANY)
    return list(pl.pallas_call(
        body, name=name, in_specs=[spec] * n, out_specs=[spec] * n,
        out_shape=[jax.ShapeDtypeStruct(a.shape, a.dtype) for a in lands],
        input_output_aliases={a: a for a in range(n)},
        scratch_shapes=[pltpu.SemaphoreType.DMA((3 * n,)), pltpu.SemaphoreType.DMA((3 * n,))],
    )(*lands))


def _flat2(a, lead):
    return a.reshape(a.shape[:lead] + (-1, a.shape[-1]))


def _pair_sum(g, recv, c_idx, *, name):
    _, nchip, r, w = g.shape
    tm = _tile(r, 256) if r % 8 == 0 else r

    def body(c_ref, g_ref, r_ref, o_ref):
        o_ref[...] = (g_ref[...].astype(F32) + r_ref[...].astype(F32)).astype(o_ref.dtype)

    return pl.pallas_call(
        body, name=name,
        grid_spec=pltpu.PrefetchScalarGridSpec(
            num_scalar_prefetch=1, grid=(nchip, r // tm),
            in_specs=[pl.BlockSpec((None, None, tm, w), lambda k, i, c_ref: (c_ref[0], k, i, 0)),
                      pl.BlockSpec((None, tm, w), lambda k, i, c_ref: (k, i, 0))],
            out_specs=pl.BlockSpec((None, tm, w), lambda k, i, c_ref: (k, i, 0))),
        out_shape=jax.ShapeDtypeStruct((nchip, r, w), BF16),
        compiler_params=_params(("parallel", "parallel")),
    )(c_idx, g, recv)


def _chip_sum(s1, recv, chip_idx, *, name):
    _, r, w = s1.shape
    tm = _tile(r, 256) if r % 8 == 0 else r

    def body(k_ref, s_ref, r_ref, o_ref):
        acc = s_ref[...].astype(F32)
        for j in range(3):
            acc = acc + r_ref[j].astype(F32)
        o_ref[...] = acc

    return pl.pallas_call(
        body, name=name,
        grid_spec=pltpu.PrefetchScalarGridSpec(
            num_scalar_prefetch=1, grid=(r // tm,),
            in_specs=[pl.BlockSpec((None, tm, w), lambda i, k_ref: (k_ref[0], i, 0)),
                      pl.BlockSpec((3, tm, w), lambda i, k_ref: (0, i, 0))],
            out_specs=pl.BlockSpec((tm, w), lambda i, k_ref: (i, 0))),
        out_shape=jax.ShapeDtypeStruct((r, w), F32),
        compiler_params=_params(("parallel",)),
    )(chip_idx, s1, recv)


def _adam_math(w, g, m, v):
    m = ADAM_B1 * m + (1.0 - ADAM_B1) * g
    v = ADAM_B2 * v + (1.0 - ADAM_B2) * (g * g)
    m_hat = m / (1.0 - ADAM_B1 ** ADAM_STEP)
    v_hat = v / (1.0 - ADAM_B2 ** ADAM_STEP)
    delta = -ADAM_LR * (m_hat / (jnp.sqrt(v_hat) + ADAM_EPS) + ADAM_WD * w)
    return delta, m, v


def _adamw(w, mine, other, c_idx, m, v, *, name):
    r, cw = w.shape
    hr = r // 2
    tm = _row_tile(hr, 9 * cw * 4)

    def body(c_ref, w_ref, a_ref, b_ref, m_ref, v_ref, g_ref, d_ref, nm_ref, nv_ref):
        g = jnp.where(pl.program_id(0) == c_ref[0], a_ref[...], b_ref[...])
        g_ref[...] = g
        d_ref[...], nm_ref[...], nv_ref[...] = _adam_math(w_ref[...], g, m_ref[...], v_ref[...])

    full = pl.BlockSpec((None, tm, cw), lambda h, i, c_ref: (h, i, 0))
    half = pl.BlockSpec((tm, cw), lambda h, i, c_ref: (i, 0))
    outs = pl.pallas_call(
        body, name=name,
        grid_spec=pltpu.PrefetchScalarGridSpec(
            num_scalar_prefetch=1, grid=(2, hr // tm),
            in_specs=[full, half, half, full, full], out_specs=[full] * 4),
        out_shape=[jax.ShapeDtypeStruct((2, hr, cw), F32)] * 4,
        compiler_params=_params(("parallel", "parallel")),
    )(c_idx, w.reshape(2, hr, cw), mine, other, m.reshape(2, hr, cw), v.reshape(2, hr, cw))
    return [o.reshape(r, cw) for o in outs]


def _adamw_ada(cact_t, dada, w, m, v):
    r, cw = w.shape
    nb = cact_t.shape[1]
    tm = _tile(r, 256)
    tn = _tile(cw, 1024)

    def body(a_ref, d_ref, w_ref, m_ref, v_ref, g_ref, dl_ref, nm_ref, nv_ref):
        a = a_ref[...]
        d = d_ref[...]
        g = a[:, 0:1] * d[0:1, :]
        for b in range(1, nb):
            g = g + a[:, b:b + 1] * d[b:b + 1, :]
        g_ref[...] = g
        dl_ref[...], nm_ref[...], nv_ref[...] = _adam_math(w_ref[...], g, m_ref[...], v_ref[...])

    blk = pl.BlockSpec((tm, tn), lambda i, j: (i, j))
    return pl.pallas_call(
        body, name="adamw_ada", grid=(r // tm, cw // tn),
        in_specs=[pl.BlockSpec((tm, nb), lambda i, j: (i, 0)), pl.BlockSpec((nb, tn), lambda i, j: (0, j)), blk, blk, blk],
        out_specs=[blk] * 4, out_shape=[jax.ShapeDtypeStruct((r, cw), F32)] * 4,
        compiler_params=_params(("parallel", "parallel")),
    )(cact_t, dada, w, m, v)


def _adamw_vec(parts, w, m, v):
    n = w.shape[1]

    def body(p_ref, w_ref, m_ref, v_ref, g_ref, d_ref, nm_ref, nv_ref):
        p = p_ref[...]
        g = p[0:1, :]
        for b in range(1, N_DEV):
            g = g + p[b:b + 1, :]
        g_ref[...] = g
        d_ref[...], nm_ref[...], nv_ref[...] = _adam_math(w_ref[...], g, m_ref[...], v_ref[...])

    return pl.pallas_call(
        body, name="adamw_vec", out_shape=[jax.ShapeDtypeStruct((1, n), F32)] * 4,
        compiler_params=pltpu.CompilerParams(vmem_limit_bytes=VMEM_LIMIT),
    )(parts, w, m, v)


def _cols_from_chips(g8, rows):
    cs = g8.shape[-1]
    return g8.reshape(4, rows, cs).transpose(1, 0, 2).reshape(rows, 4 * cs)


def _cols_to_pieces(g):
    rows, c4 = g.shape
    return g.reshape(2, rows // 2, 4, c4 // 4).transpose(0, 2, 1, 3)


def _rows_to_pieces(g):
    r4, cols = g.shape
    return g.reshape(4, 2, r4 // 8, cols).transpose(1, 0, 2, 3)


def _pad_cols(a, w):
    return jnp.pad(a, ((0, 0), (0, w - a.shape[1])))


def kernel(x, c, positions, w_ada, b_ada, g_norm1, g_norm2, w_in, g_q_latent, g_kv_latent, w_uq, w_ukv, g_q_head, g_k_head, w_proj_mla, w_proj_sb, w_out, w_ffn_in, w_ffn_out, loss_target, m_w_ada, m_b_ada, m_g_norm1, m_g_norm2, m_w_in, m_g_q_latent, m_g_kv_latent, m_w_uq, m_w_ukv, m_g_q_head, m_g_k_head, m_w_proj_mla, m_w_proj_sb, m_w_out, m_w_ffn_in, m_w_ffn_out, v_w_ada, v_b_ada, v_g_norm1, v_g_norm2, v_w_in, v_g_q_latent, v_g_kv_latent, v_w_uq, v_w_ukv, v_g_q_head, v_g_k_head, v_w_proj_mla, v_w_proj_sb, v_w_out, v_w_ffn_in, v_w_ffn_out):
    xi, yi, ci = _place()
    chip = 2 * xi + yi
    dev = 2 * chip + ci
    c_idx = jnp.reshape(ci, (1,)).astype(jnp.int32)
    chip_idx = jnp.reshape(chip, (1,)).astype(jnp.int32)

    x = x[0]
    tgt = loss_target[0]
    S, D = x.shape
    ql = g_q_latent.shape[1]
    assert g_kv_latent.shape[1] == ql
    mlaw = w_proj_mla.shape[1]
    nh = mlaw // HEAD
    sbw = w_proj_sb.shape[1]
    assert sbw == mlaw
    dff = w_ffn_out.shape[1] * 4
    d_in = 2 * ql + ROPE + 3 * sbw + 2 * D
    d_in_p = d_in + ROPE
    q_col = (2 * ql) // HEAD
    k_col = q_col + nh
    v_col = k_col + nh
    gla_col = (2 * ql + 3 * sbw) // D
    glb_col = gla_col + 1
    kpe_col = (d_in - ROPE) // LANE
    assert (2 * ql + 3 * sbw) % D == 0 and (d_in - ROPE) % LANE == 0

    mats = {"w_in": w_in[0], "w_uq": w_uq[0], "w_ukv": w_ukv[0], "w_proj_mla": w_proj_mla[0],
            "w_proj_sb": w_proj_sb[0], "w_out": w_out[0], "w_ffn_in": w_ffn_in[0], "w_ffn_out": w_ffn_out[0]}
    names = list(mats)
    row_sharded = {"w_out", "w_ffn_out"}

    c_all = _gather_blocks([jnp.broadcast_to(c, (8, D))], name="gather_cond", in_vmem=True)[0][:, 0, :]
    n_ada = w_ada.shape[2]
    b_shard = lax.dynamic_slice_in_dim(b_ada, chip * n_ada, n_ada, axis=1)
    ada_shard = _mm(c_all, w_ada[0], name="ada_proj", a_fn=jax.nn.silu, bias=b_shard)
    ada_all = _gather_blocks([ada_shard], name="gather_ada", in_vmem=True)[0]
    ada_rows = lax.dynamic_index_in_dim(ada_all, dev, axis=1, keepdims=False)
    ada = ada_rows[0::2].reshape(1, 4 * n_ada)
    SH1, SC1, GT1, SH2, SC2, GT2 = range(6)

    def after(dep, a):
        return a + (dep.reshape(-1)[0:1].reshape((1,) * a.ndim) * 0).astype(a.dtype)

    def fill_own(g8, own):
        return lax.dynamic_update_index_in_dim(g8, own, dev, 0)

    halves = []
    for nm in names:
        w = mats[nm]
        hr = w.shape[0] // 2
        halves.append(lax.dynamic_slice_in_dim(w, ci * hr, hr, axis=0).astype(BF16))
    half_of = dict(zip(names, halves))
    early = ["w_in", "w_uq", "w_ukv"]
    late = ["w_proj_mla", "w_proj_sb", "w_out", "w_ffn_in", "w_ffn_out"]
    early_halves = [half_of[nm] for nm in early]
    early_halves[0] = after(ada, early_halves[0])
    early_got = _gather_blocks(early_halves, name="gather_weights", in_vmem=False)
    late_halves = [half_of[nm] for nm in late]
    late_halves[0] = after(early_got[1], late_halves[0])
    late_send, late_recv, late_srcs, late_lands, late_token = _split_start(
        late_halves, [jax.ShapeDtypeStruct((N_DEV,) + h.shape, h.dtype) for h in late_halves], _gather_plan, 4,
        name="gather_late_start")
    ada = ada + late_token[0:1, 0:1]
    gathered = {nm: fill_own(g8, own) for nm, g8, own in zip(early, early_got, early_halves)}

    def full_cols(nm):
        return _cols_from_chips(gathered[nm], mats[nm].shape[0])

    w_in_f = full_cols("w_in")
    kpe0 = 2 * ql
    w_in_p = jnp.concatenate([w_in_f[:, :kpe0], w_in_f[:, kpe0 + ROPE:], w_in_f[:, kpe0:kpe0 + ROPE],
                              jnp.zeros((D, ROPE), BF16)], axis=1)
    w_uq_p = jnp.pad(full_cols("w_uq").reshape(ql, nh, QK_DIM), ((0, 0), (0, 0), (0, HEAD_PAD - QK_DIM))
                     ).reshape(ql, nh * HEAD_PAD)
    w_ukv4 = full_cols("w_ukv").reshape(ql, nh, 2 * HEAD)
    w_ukv_p = jnp.concatenate([w_ukv4[:, :, :HEAD].reshape(ql, mlaw), w_ukv4[:, :, HEAD:].reshape(ql, mlaw)], axis=1)

    half = ROPE // 2
    freqs = ROPE_THETA ** (-jnp.arange(half, dtype=F32) / half)
    ang = positions[0].astype(F32)[:, None] * freqs
    cos, sin = jnp.cos(ang), jnp.sin(ang)
    one = jnp.ones((S, NOPE), F32)
    zero = jnp.zeros((S, NOPE), F32)
    zh = jnp.zeros((S, half), F32)
    tabs = (jnp.concatenate([one, cos, cos, one[:, :HEAD_PAD - QK_DIM]], axis=1),
            jnp.concatenate([zero, zh, sin, zero[:, :HEAD_PAD - QK_DIM]], axis=1),
            jnp.concatenate([zero, -sin, zh, zero[:, :HEAD_PAD - QK_DIM]], axis=1))
    g_qh_p = _pad_cols(g_q_head, HEAD_PAD)
    g_kh_p = _pad_cols(g_k_head, HEAD_PAD)

    h1 = _rmsmod(x, g_norm1, ada, SC1, SH1, name="rmsmod1")
    proj = _mm(h1, w_in_p, name="mm_proj", tn=640)
    cqn, ckvn = _latent_norm(proj, g_q_latent, g_kv_latent, ql)
    q0 = _mm(cqn, w_uq_p, name="mm_q_up")
    kv0 = _mm(ckvn, w_ukv_p, name="mm_kv_up")
    q = _q_prep(q0, g_qh_p, tabs, nh)
    k = _k_prep(kv0, proj, kpe_col, g_kh_p, tabs, nh)
    y_a, lse = _mla_fwd(q, k, kv0, nh)
    y_b = _sb_fwd(proj, q_col, k_col, v_col, nh)
    late_srcs, late_lands = _split_wait(late_send, late_recv, late_srcs, late_lands, y_b, _gather_plan,
                                        name="gather_late_wait")
    late_got = _gather_forward(late_lands, name="gather_late_forward")
    gathered.update({nm: fill_own(g8, own) for nm, g8, own in zip(late, late_got, late_srcs)})
    w_pm = full_cols("w_proj_mla")
    w_ps = full_cols("w_proj_sb")
    w_o = gathered["w_out"].reshape(D, D)
    w_fi = full_cols("w_ffn_in")
    w_fo = gathered["w_ffn_out"].reshape(dff, D)
    pa = _mm(y_a, w_pm, name="mm_proj_mla")
    pb = _mm(y_b, w_ps, name="mm_proj_sb")
    merged = _gate_merge(pa, pb, proj, gla_col, glb_col)
    o = _mm(merged, w_o, name="mm_out")
    x2, h2 = _resid_rmsmod(x, o, g_norm2, ada, GT1, SC2, SH2)
    ff = _mm(h2, w_fi, name="mm_ffn_in", out_dtype=BF16)
    act = _swiglu(ff, dff)
    f = _mm(act, w_fo, name="mm_ffn_out")
    dy, df, red_l, loss_p = _loss_head(x2, f, tgt, ada, GT2)

    dact = _mm(df, w_fo, name="mm_d_act", tb=True)
    gw_fo = _mm(act, df, name="mm_gw_ffn_out", ta=True, out_dtype=BF16)
    dff_ = _swiglu_bwd(dact, ff, dff)
    dh2 = _mm(dff_, w_fi, name="mm_d_h2", tb=True)
    gw_fi = _mm(h2, dff_, name="mm_gw_ffn_in", ta=True, out_dtype=BF16)

    def pair_sums(nms, grads, tag):
        pcs = [(_rows_to_pieces if nm in row_sharded else _cols_to_pieces)(g) for nm, g in zip(nms, grads)]
        got = _sibling_swap(pcs, name="rs_sibling_swap_" + tag)
        return [_pair_sum(p, r, c_idx, name="rs_pair_sum_" + nm) for p, r, nm in zip(pcs, got, nms)]

    ffn = ["w_ffn_in", "w_ffn_out"]
    ffn_pair = pair_sums(ffn, [gw_fi, gw_fo], "ffn")
    ffn_send, ffn_recv, ffn_pair, ffn_lands, ffn_token = _split_start(
        ffn_pair, [jax.ShapeDtypeStruct((3,) + p.shape[1:], p.dtype) for p in ffn_pair], _exchange_plan, 3,
        name="rs_exchange_ffn_start")
    ada = ada + ffn_token[0:1, 0:1]
    dx2, do, red_2 = _rmsmod2_bwd(dh2, x2, dy, o, g_norm2, ada, SC2, GT1)
    dmerged = _mm(do, w_o, name="mm_d_merged", tb=True)
    gw_o = _mm(merged, do, name="mm_gw_out", ta=True, out_dtype=BF16)
    dpa, dpb, dgla, dglb = _gate_bwd(dmerged, pa, pb, proj, gla_col, glb_col)
    dya = _mm(dpa, w_pm, name="mm_d_ya", tb=True)
    gw_pm = _mm(y_a, dpa, name="mm_gw_proj_mla", ta=True, out_dtype=BF16)
    dyb = _mm(dpb, w_ps, name="mm_d_yb", tb=True)
    gw_ps = _mm(y_b, dpb, name="mm_gw_proj_sb", ta=True, out_dtype=BF16)
    mid = ["w_proj_mla", "w_proj_sb", "w_out"]
    mid_pair = pair_sums(mid, [gw_pm, gw_ps, gw_o], "mid")
    mid_send, mid_recv, mid_pair, mid_lands, mid_token = _split_start(
        mid_pair, [jax.ShapeDtypeStruct((3,) + p.shape[1:], p.dtype) for p in mid_pair], _exchange_plan, 3,
        name="rs_exchange_mid_start")
    lse = lse + mid_token[0:1, 0:1]
    dq, dk, dv = _mla_bwd(q, k, kv0, y_a, dya, lse, nh)
    dq_sb, dk_sb, dv_sb = _sb_bwd(proj, q_col, k_col, v_col, dyb, nh)
    dq0, red_qh = _q_prep_bwd(dq, q0, g_qh_p, tabs, nh)
    dkv0, dkpe, red_kh = _k_prep_bwd(dk, dv, kv0, proj, kpe_col, g_kh_p, tabs, nh)
    dcqn = _mm(dq0, w_uq_p, name="mm_d_cqn", tb=True)
    gw_uq_p = _mm(cqn, dq0, name="mm_gw_uq", ta=True, out_dtype=BF16)
    dckvn = _mm(dkv0, w_ukv_p, name="mm_d_ckvn", tb=True)
    gw_ukv_p = _mm(ckvn, dkv0, name="mm_gw_ukv", ta=True, out_dtype=BF16)
    dcq, dckv, red_lat = _latent_norm_bwd(dcqn, dckvn, proj, g_q_latent, g_kv_latent, ql)
    dproj = jnp.concatenate([dcq, dckv, dq_sb.astype(BF16), dk_sb.astype(BF16), dv_sb.astype(BF16),
                             dgla, dglb, dkpe], axis=1)
    gw_in_p = _mm(h1, dproj, name="mm_gw_in", ta=True, out_dtype=BF16, tn=640)

    nsb = d_in_p - 2 * ROPE
    gw_in = jnp.concatenate([gw_in_p[:, :kpe0], gw_in_p[:, nsb:nsb + ROPE], gw_in_p[:, kpe0:nsb]], axis=1)
    gw_uq = gw_uq_p.reshape(ql, nh, HEAD_PAD)[:, :, :QK_DIM].reshape(ql, nh * QK_DIM)
    gw_ukv = jnp.concatenate([gw_ukv_p[:, :mlaw].reshape(ql, nh, HEAD), gw_ukv_p[:, mlaw:].reshape(ql, nh, HEAD)],
                             axis=2).reshape(ql, 2 * mlaw)
    last = ["w_in", "w_uq", "w_ukv"]
    assert last + mid + ffn == names

    last_pair = pair_sums(last, [gw_in, gw_uq, gw_ukv], "last")
    last_send, last_recv, last_pair, last_lands, last_token = _split_start(
        last_pair, [jax.ShapeDtypeStruct((3,) + p.shape[1:], p.dtype) for p in last_pair], _exchange_plan, 3,
        name="rs_exchange_last_start")
    ada = ada + last_token[0:1, 0:1]
    dh1 = _mm(dproj, w_in_p, name="mm_d_h1", tb=True)
    grad_x, red_1 = _rmsmod1_bwd(dh1, x, dx2, g_norm1, ada, SC1)
    last_pair, last_chips = _split_wait(last_send, last_recv, last_pair, last_lands, grad_x, _exchange_plan,
                                        name="rs_exchange_last_wait")
    mid_pair, mid_chips = _split_wait(mid_send, mid_recv, mid_pair, mid_lands, grad_x, _exchange_plan,
                                      name="rs_exchange_mid_wait")
    ffn_pair, ffn_chips = _split_wait(ffn_send, ffn_recv, ffn_pair, ffn_lands, grad_x, _exchange_plan,
                                      name="rs_exchange_ffn_wait")
    reduced = [_chip_sum(s, r, chip_idx, name="rs_chip_sum_" + nm)
               for s, r, nm in zip(last_pair + mid_pair + ffn_pair, last_chips + mid_chips + ffn_chips, names)]
    from_sibling2 = _sibling_swap(reduced, name="rs_sibling_send", whole=True)

    vec_names = ["b_ada", "g_norm1", "g_norm2", "g_q_latent", "g_kv_latent", "g_q_head", "g_k_head"]
    vec_w = dict(b_ada=b_ada, g_norm1=g_norm1, g_norm2=g_norm2, g_q_latent=g_q_latent, g_kv_latent=g_kv_latent,
                 g_q_head=g_q_head, g_k_head=g_k_head)
    vec_m = dict(b_ada=m_b_ada, g_norm1=m_g_norm1, g_norm2=m_g_norm2, g_q_latent=m_g_q_latent,
                 g_kv_latent=m_g_kv_latent, g_q_head=m_g_q_head, g_k_head=m_g_k_head)
    vec_v = dict(b_ada=v_b_ada, g_norm1=v_g_norm1, g_norm2=v_g_norm2, g_q_latent=v_g_q_latent,
                 g_kv_latent=v_g_kv_latent, g_q_head=v_g_q_head, g_k_head=v_g_k_head)
    d_ada = jnp.concatenate([red_1[0:1], red_1[1:2], red_2[3:4], red_2[0:1], red_2[1:2], red_l[0:1]], axis=1)
    vec_parts = dict(b_ada=d_ada, g_norm1=red_1[2:3], g_norm2=red_2[2:3], g_q_latent=red_lat[0:1],
                     g_kv_latent=red_lat[1:2], g_q_head=red_qh[0:1], g_k_head=red_kh[0:1])
    widths = [-(-vec_w[nm].shape[1] // LANE) * LANE for nm in vec_names]
    offs = [sum(widths[:i]) for i in range(len(widths))]
    pack = lambda d: jnp.concatenate([_pad_cols(d[nm][:, :vec_w[nm].shape[1]], wd) for nm, wd in zip(vec_names, widths)], axis=1)
    nvec = sum(widths) + LANE
    no_loss = jnp.zeros((1, LANE), F32)
    parts = jnp.concatenate([pack(vec_parts), loss_p[0:1, :]], axis=1)
    parts_all = _gather_blocks([jnp.broadcast_to(parts, (8, nvec))], name="gather_vec_grads",
                               in_vmem=True)[0][:, 0, :]
    gvec, dvec, nmvec, nvvec = _adamw_vec(parts_all, *[jnp.concatenate([pack(d), no_loss], axis=1)
                                                       for d in (vec_w, vec_m, vec_v)])
    loss = gvec[0, nvec - LANE]
    unpack = lambda a: {nm: a[:, o_:o_ + vec_w[nm].shape[1]] for nm, o_ in zip(vec_names, offs)}
    gvec, dvec, nmvec, nvvec = unpack(gvec), unpack(dvec), unpack(nmvec), unpack(nvvec)

    dada_all = lax.dynamic_slice_in_dim(parts_all[:, :6 * D], chip * n_ada, n_ada, axis=1)
    cact_t = jax.nn.silu(c_all).T
    g_ada, d_ada_w, nm_ada, nv_ada = _adamw_ada(cact_t, dada_all, w_ada[0], m_w_ada[0], v_w_ada[0])

    ms = dict(w_in=m_w_in, w_uq=m_w_uq, w_ukv=m_w_ukv, w_proj_mla=m_w_proj_mla, w_proj_sb=m_w_proj_sb,
              w_out=m_w_out, w_ffn_in=m_w_ffn_in, w_ffn_out=m_w_ffn_out)
    vs = dict(w_in=v_w_in, w_uq=v_w_uq, w_ukv=v_w_ukv, w_proj_mla=v_w_proj_mla, w_proj_sb=v_w_proj_sb,
              w_out=v_w_out, w_ffn_in=v_w_ffn_in, w_ffn_out=v_w_ffn_out)
    G, DL, NM, NV = {}, {}, {}, {}
    for nm, mine, other in zip(names, reduced, from_sibling2):
        g_, d_, m_, v_ = _adamw(mats[nm], mine, other, c_idx, ms[nm][0], vs[nm][0], name="adamw_" + nm)
        G[nm], DL[nm], NM[nm], NV[nm] = g_[None], d_[None], m_[None], v_[None]
    G["w_ada"], DL["w_ada"], NM["w_ada"], NV["w_ada"] = g_ada[None], d_ada_w[None], nm_ada[None], nv_ada[None]
    for nm in vec_names:
        G[nm], DL[nm], NM[nm], NV[nm] = gvec[nm], dvec[nm], nmvec[nm], nvvec[nm]

    order = ["w_ada", "b_ada", "g_norm1", "g_norm2", "w_in", "g_q_latent", "g_kv_latent", "w_uq", "w_ukv",
             "g_q_head", "g_k_head", "w_proj_mla", "w_proj_sb", "w_out", "w_ffn_in", "w_ffn_out"]
    return (loss, grad_x[None], *[G[n] for n in order], *[DL[n] for n in order],
            *[NM[n] for n in order], *[NV[n] for n in order])
```

```python
import functools
import math

import jax
import jax.numpy as jnp
from jax import lax
from jax.experimental import pallas as pl
from jax.experimental.pallas import tpu as pltpu

F32 = jnp.float32
BF16 = jnp.bfloat16
MESH = pl.DeviceIdType.MESH

EPS = 1e-6
ROPE_THETA = 10000.0
NOPE = 128
ROPE = 64
QK_DIM = NOPE + ROPE
HEAD_PAD = 256
HEAD = 128
N_DEV = 8
LANE = 128
VMEM_LIMIT = 48 * 1024 * 1024

ADAM_LR = 0.001
ADAM_B1 = 0.9
ADAM_B2 = 0.999
ADAM_EPS = 1e-08
ADAM_WD = 0.01
ADAM_STEP = 10


def _tile(n, target):
    if n <= target:
        return n
    t = (target // LANE) * LANE
    while t >= LANE:
        if n % t == 0:
            return t
        t -= LANE
    return n


def _row_tile(rows, row_bytes, budget=24 * 1024 * 1024):
    cap = max(8, budget // (2 * row_bytes))
    best = None
    for t in range(8, min(rows, cap) + 1, 8):
        if rows % t == 0:
            best = t
    return best if best is not None else rows


def _params(sem):
    return pltpu.CompilerParams(dimension_semantics=sem, vmem_limit_bytes=VMEM_LIMIT)


def _rows(tm, w, col=0):
    return pl.BlockSpec((tm, w), lambda i: (i, col))


def _vec(w, col=0, rows=1):
    return pl.BlockSpec((rows, w), lambda i: (0, col))


MM_VMEM_BUDGET = 36 * 1024 * 1024


def _mm(a, b, *, name, ta=False, tb=False, out_dtype=F32, a_fn=None, bias=None, tm=1024, tn=1024, pieces=None):
    M = a.shape[1] if ta else a.shape[0]
    K = a.shape[0] if ta else a.shape[1]
    N = b.shape[0] if tb else b.shape[1]
    assert K == (b.shape[1] if tb else b.shape[0]), (a.shape, b.shape, ta, tb)
    if pieces == "cols":
        tm, tn = _tile(M // 2, tm), _tile(N // 4, tn)
        assert (M // 2) % tm == 0 and (N // 4) % tn == 0
    elif pieces == "rows":
        tm, tn = M // 4, _tile(N, tn)
    else:
        tm, tn = _tile(M, tm), _tile(N, tn)
    sa, sb, so = a.dtype.itemsize, b.dtype.itemsize, jnp.dtype(out_dtype).itemsize

    def fits(tk):
        return 2 * tk * (tm * sa + tn * sb) + tm * tn * (2 * so + 4) <= MM_VMEM_BUDGET

    tk = K
    while not fits(tk):
        smaller = _tile(K, tk - LANE)
        if smaller >= tk:
            break
        tk = smaller
    nk = K // tk
    dn = (((0 if ta else 1,), (1 if tb else 0,)), ((), ()))
    b_outer = nk == 1 and a.size * sa * (N // tn) < b.size * sb * (M // tm)

    def body(*refs):
        a_ref, b_ref = refs[:2]
        bias_ref = refs[2] if bias is not None else None
        o_ref = refs[3 if bias is not None else 2]
        av = a_ref[...]
        if a_fn is not None:
            av = a_fn(av.astype(F32))
        part = lax.dot_general(av.astype(BF16), b_ref[...].astype(BF16), dn, preferred_element_type=F32)

        def finish(r):
            if bias is not None:
                r = r + bias_ref[...]
            if pieces == "rows":
                o_ref[0] = r[:tm // 2].astype(o_ref.dtype)
                o_ref[1] = r[tm // 2:].astype(o_ref.dtype)
            else:
                o_ref[...] = r.astype(o_ref.dtype)

        if nk == 1:
            finish(part)
        else:
            acc_ref = refs[-1]
            k = pl.program_id(2)

            @pl.when(k == 0)
            def _():
                acc_ref[...] = part

            @pl.when(k > 0)
            def _():
                acc_ref[...] += part

            @pl.when(k == nk - 1)
            def _():
                finish(acc_ref[...])

    def ij(g0, g1):
        return (g1, g0) if b_outer else (g0, g1)

    def amap(g0, g1, k):
        i, _ = ij(g0, g1)
        return (k, i) if ta else (i, k)

    def bmap(g0, g1, k):
        _, j = ij(g0, g1)
        return (j, k) if tb else (k, j)

    in_specs = [pl.BlockSpec((tk, tm) if ta else (tm, tk), amap), pl.BlockSpec((tn, tk) if tb else (tk, tn), bmap)]
    args = [a, b]
    if bias is not None:
        in_specs.append(pl.BlockSpec((1, tn), lambda g0, g1, k: (0, ij(g0, g1)[1])))
        args.append(bias)
    grid = (N // tn, M // tm, nk) if b_outer else (M // tm, N // tn, nk)
    if pieces == "cols":
        ni, nj = M // 2 // tm, N // 4 // tn

        def omap(g0, g1, k):
            i, j = ij(g0, g1)
            return (i // ni, j // nj, i % ni, j % nj)

        out_spec = pl.BlockSpec((None, None, tm, tn), omap)
        out_shape = jax.ShapeDtypeStruct((2, 4, M // 2, N // 4), out_dtype)
    elif pieces == "rows":
        out_spec = pl.BlockSpec((2, None, tm // 2, tn), lambda g0, g1, k: (0, ij(g0, g1)[0], 0, ij(g0, g1)[1]))
        out_shape = jax.ShapeDtypeStruct((2, 4, tm // 2, N), out_dtype)
    else:
        out_spec = pl.BlockSpec((tm, tn), lambda g0, g1, k: ij(g0, g1))
        out_shape = jax.ShapeDtypeStruct((M, N), out_dtype)
    return pl.pallas_call(
        body, name=name, grid=grid, in_specs=in_specs, out_specs=out_spec, out_shape=out_shape,
        scratch_shapes=[pltpu.VMEM((tm, tn), F32)] if nk > 1 else [],
        compiler_params=_params(("parallel", "parallel", "arbitrary")),
    )(*args)


def _rms_rows(v):
    return lax.rsqrt(jnp.mean(v * v, axis=-1, keepdims=True) + EPS)


def _rmsmod(x, g, ada, sc_col, sh_col, *, name):
    S, D = x.shape
    tm = _tile(S, 256)

    def body(x_ref, g_ref, sc_ref, sh_ref, h_ref):
        xv = x_ref[...]
        h = (xv * _rms_rows(xv) * g_ref[...]) * (1.0 + sc_ref[...]) + sh_ref[...]
        h_ref[...] = h.astype(h_ref.dtype)

    return pl.pallas_call(
        body, name=name, grid=(S // tm,),
        in_specs=[_rows(tm, D), _vec(D), _vec(D, sc_col), _vec(D, sh_col)],
        out_specs=_rows(tm, D), out_shape=jax.ShapeDtypeStruct((S, D), BF16),
        compiler_params=_params(("parallel",)),
    )(x, g, ada, ada)


def _latent_norm(proj, g_q, g_kv, ql):
    S = proj.shape[0]
    tm = _tile(S, 512)

    def body(cq_ref, ckv_ref, gq_ref, gkv_ref, oq_ref, okv_ref):
        cq = cq_ref[...]
        oq_ref[...] = (cq * _rms_rows(cq) * gq_ref[...]).astype(BF16)
        ckv = ckv_ref[...]
        okv_ref[...] = (ckv * _rms_rows(ckv) * gkv_ref[...]).astype(BF16)

    return pl.pallas_call(
        body, name="latent_norm", grid=(S // tm,),
        in_specs=[_rows(tm, ql, 0), _rows(tm, ql, 1), _vec(ql), _vec(ql)],
        out_specs=[_rows(tm, ql), _rows(tm, ql)],
        out_shape=[jax.ShapeDtypeStruct((S, ql), BF16)] * 2,
        compiler_params=_params(("parallel",)),
    )(proj, proj, g_q, g_kv)


def _rope_fwd(y, c, s1, s2):
    return y * c + pltpu.roll(y, ROPE // 2, 1) * s1 + pltpu.roll(y, HEAD_PAD - ROPE // 2, 1) * s2


def _rope_bwd(d, c, s1, s2):
    return d * c + pltpu.roll(d * s1, HEAD_PAD - ROPE // 2, 1) + pltpu.roll(d * s2, ROPE // 2, 1)


def _head_rms(v):
    return lax.rsqrt(jnp.sum(v * v, axis=-1, keepdims=True) * (1.0 / QK_DIM) + EPS)


def _q_prep(q0, g_qh, tabs, nh):
    S = q0.shape[0]
    tm = _tile(S, 256)

    def body(q_ref, g_ref, c_ref, s1_ref, s2_ref, o_ref):
        c, s1, s2, g = c_ref[...], s1_ref[...], s2_ref[...], g_ref[...]
        for h in range(nh):
            sl = slice(h * HEAD_PAD, (h + 1) * HEAD_PAD)
            xs = q_ref[:, sl]
            o_ref[:, sl] = (_rope_fwd(xs * _head_rms(xs) * g, c, s1, s2) * (QK_DIM ** -0.5)).astype(BF16)

    w = nh * HEAD_PAD
    return pl.pallas_call(
        body, name="mla_q_prep", grid=(S // tm,),
        in_specs=[_rows(tm, w), _vec(HEAD_PAD)] + [_rows(tm, HEAD_PAD)] * 3,
        out_specs=_rows(tm, w), out_shape=jax.ShapeDtypeStruct((S, w), BF16),
        compiler_params=_params(("parallel",)),
    )(q0, g_qh, *tabs)


def _k_prep(kv0, proj, kpe_col, g_kh, tabs, nh):
    S = kv0.shape[0]
    tm = _tile(S, 256)

    def body(kv_ref, kpe_ref, g_ref, c_ref, s1_ref, s2_ref, o_ref):
        c, s1, s2, g = c_ref[...], s1_ref[...], s2_ref[...], g_ref[...]
        kpe = kpe_ref[...]
        for h in range(nh):
            k0 = jnp.concatenate([kv_ref[:, h * HEAD:(h + 1) * HEAD], kpe], axis=1)
            o_ref[:, h * HEAD_PAD:(h + 1) * HEAD_PAD] = _rope_fwd(k0 * _head_rms(k0) * g, c, s1, s2).astype(BF16)

    return pl.pallas_call(
        body, name="mla_k_prep", grid=(S // tm,),
        in_specs=[_rows(tm, nh * HEAD, 0), _rows(tm, LANE, kpe_col), _vec(HEAD_PAD)] + [_rows(tm, HEAD_PAD)] * 3,
        out_specs=_rows(tm, nh * HEAD_PAD), out_shape=jax.ShapeDtypeStruct((S, nh * HEAD_PAD), BF16),
        compiler_params=_params(("parallel",)),
    )(kv0, proj, g_kh, *tabs)


def _gate_merge(pa, pb, proj, gla_col, glb_col):
    S, D = pa.shape
    tm = _tile(S, 256)

    def body(pa_ref, pb_ref, ga_ref, gb_ref, o_ref):
        o_ref[...] = (jax.nn.sigmoid(ga_ref[...]) * pa_ref[...] + jax.nn.sigmoid(gb_ref[...]) * pb_ref[...]).astype(BF16)

    return pl.pallas_call(
        body, name="gate_merge", grid=(S // tm,),
        in_specs=[_rows(tm, D), _rows(tm, D), _rows(tm, D, gla_col), _rows(tm, D, glb_col)],
        out_specs=_rows(tm, D), out_shape=jax.ShapeDtypeStruct((S, D), BF16),
        compiler_params=_params(("parallel",)),
    )(pa, pb, proj, proj)


def _resid_rmsmod(x, o, g, ada, gt_col, sc_col, sh_col):
    S, D = x.shape
    tm = _tile(S, 256)

    def body(x_ref, o_ref, g_ref, gt_ref, sc_ref, sh_ref, x2_ref, h_ref):
        x2 = x_ref[...] + gt_ref[...] * o_ref[...]
        x2_ref[...] = x2
        h_ref[...] = ((x2 * _rms_rows(x2) * g_ref[...]) * (1.0 + sc_ref[...]) + sh_ref[...]).astype(BF16)

    return pl.pallas_call(
        body, name="resid_rmsmod2", grid=(S // tm,),
        in_specs=[_rows(tm, D), _rows(tm, D), _vec(D), _vec(D, gt_col), _vec(D, sc_col), _vec(D, sh_col)],
        out_specs=[_rows(tm, D), _rows(tm, D)],
        out_shape=[jax.ShapeDtypeStruct((S, D), F32), jax.ShapeDtypeStruct((S, D), BF16)],
        compiler_params=_params(("parallel",)),
    )(x, o, g, ada, ada, ada)


def _swiglu(ff, dff_half):
    S = ff.shape[0]
    tm = _tile(S, 256)

    def body(g_ref, u_ref, o_ref):
        o_ref[...] = (jax.nn.silu(g_ref[...].astype(F32)) * u_ref[...].astype(F32)).astype(BF16)

    return pl.pallas_call(
        body, name="swiglu", grid=(S // tm,),
        in_specs=[_rows(tm, dff_half, 0), _rows(tm, dff_half, 1)],
        out_specs=_rows(tm, dff_half), out_shape=jax.ShapeDtypeStruct((S, dff_half), BF16),
        compiler_params=_params(("parallel",)),
    )(ff, ff)


def _loss_head(x2, f, tgt, ada, gt_col):
    S, D = x2.shape
    tm = _tile(S, 256)

    def body(x2_ref, f_ref, t_ref, gt_ref, dy_ref, df_ref, red_ref, loss_ref):
        @pl.when(pl.program_id(0) == 0)
        def _():
            red_ref[...] = jnp.zeros_like(red_ref)
            loss_ref[...] = jnp.zeros_like(loss_ref)

        fv = f_ref[...]
        gt = gt_ref[...]
        err = x2_ref[...] + gt * fv - t_ref[...]
        dy = err * (1.0 / D)
        dy_ref[...] = dy
        df_ref[...] = (dy * gt).astype(BF16)
        red_ref[0:1, :] += jnp.sum(dy * fv, axis=0, keepdims=True)
        loss_ref[...] += (0.5 / D) * jnp.sum(err * err)

    return pl.pallas_call(
        body, name="loss_head", grid=(S // tm,),
        in_specs=[_rows(tm, D), _rows(tm, D), _rows(tm, D), _vec(D, gt_col)],
        out_specs=[_rows(tm, D), _rows(tm, D), _vec(D, rows=8), _vec(LANE, rows=8)],
        out_shape=[jax.ShapeDtypeStruct((S, D), F32), jax.ShapeDtypeStruct((S, D), BF16),
                   jax.ShapeDtypeStruct((8, D), F32), jax.ShapeDtypeStruct((8, LANE), F32)],
        compiler_params=_params(("arbitrary",)),
    )(x2, f, tgt, ada)


def _swiglu_bwd(dact, ff, dff_half):
    S = ff.shape[0]
    tm = _tile(S, 128)

    def body(d_ref, g_ref, u_ref, o_ref):
        d = d_ref[...]
        g = g_ref[...].astype(F32)
        u = u_ref[...].astype(F32)
        sg = jax.nn.sigmoid(g)
        o_ref[:, :dff_half] = (d * u * sg * (1.0 + g * (1.0 - sg))).astype(BF16)
        o_ref[:, dff_half:] = (d * g * sg).astype(BF16)

    return pl.pallas_call(
        body, name="swiglu_bwd", grid=(S // tm,),
        in_specs=[_rows(tm, dff_half), _rows(tm, dff_half, 0), _rows(tm, dff_half, 1)],
        out_specs=_rows(tm, 2 * dff_half), out_shape=jax.ShapeDtypeStruct((S, 2 * dff_half), BF16),
        compiler_params=_params(("parallel",)),
    )(dact, ff, ff)


def _rmsmod2_bwd(dh2, x2, dy, o, g, ada, sc_col, gt_col):
    S, D = x2.shape
    tm = _tile(S, 256)

    def body(dh_ref, x2_ref, dy_ref, o_ref, g_ref, sc_ref, gt_ref, dx_ref, do_ref, red_ref):
        @pl.when(pl.program_id(0) == 0)
        def _():
            red_ref[...] = jnp.zeros_like(red_ref)

        dh = dh_ref[...]
        x2 = x2_ref[...]
        gv = g_ref[...]
        mod = 1.0 + sc_ref[...]
        r = _rms_rows(x2)
        xn = x2 * r
        t = dh * xn
        red_ref[0:1, :] += jnp.sum(dh, axis=0, keepdims=True)
        red_ref[1:2, :] += jnp.sum(t * gv, axis=0, keepdims=True)
        red_ref[2:3, :] += jnp.sum(t * mod, axis=0, keepdims=True)
        dxn = dh * gv * mod
        dx = dy_ref[...] + r * (dxn - xn * jnp.mean(dxn * xn, axis=-1, keepdims=True))
        dx_ref[...] = dx
        red_ref[3:4, :] += jnp.sum(dx * o_ref[...], axis=0, keepdims=True)
        do_ref[...] = (dx * gt_ref[...]).astype(BF16)

    return pl.pallas_call(
        body, name="rmsmod2_bwd", grid=(S // tm,),
        in_specs=[_rows(tm, D)] * 4 + [_vec(D), _vec(D, sc_col), _vec(D, gt_col)],
        out_specs=[_rows(tm, D), _rows(tm, D), _vec(D, rows=8)],
        out_shape=[jax.ShapeDtypeStruct((S, D), F32), jax.ShapeDtypeStruct((S, D), BF16),
                   jax.ShapeDtypeStruct((8, D), F32)],
        compiler_params=_params(("arbitrary",)),
    )(dh2, x2, dy, o, g, ada, ada)


def _rmsmod1_bwd(dh, x, dx2, g, ada, sc_col):
    S, D = x.shape
    tm = _tile(S, 256)

    def body(dh_ref, x_ref, dx2_ref, g_ref, sc_ref, gx_ref, red_ref):
        @pl.when(pl.program_id(0) == 0)
        def _():
            red_ref[...] = jnp.zeros_like(red_ref)

        dh = dh_ref[...]
        xv = x_ref[...]
        gv = g_ref[...]
        mod = 1.0 + sc_ref[...]
        r = _rms_rows(xv)
        xn = xv * r
        t = dh * xn
        red_ref[0:1, :] += jnp.sum(dh, axis=0, keepdims=True)
        red_ref[1:2, :] += jnp.sum(t * gv, axis=0, keepdims=True)
        red_ref[2:3, :] += jnp.sum(t * mod, axis=0, keepdims=True)
        dxn = dh * gv * mod
        gx_ref[...] = dx2_ref[...] + r * (dxn - xn * jnp.mean(dxn * xn, axis=-1, keepdims=True))

    return pl.pallas_call(
        body, name="rmsmod1_bwd", grid=(S // tm,),
        in_specs=[_rows(tm, D)] * 3 + [_vec(D), _vec(D, sc_col)],
        out_specs=[_rows(tm, D), _vec(D, rows=8)],
        out_shape=[jax.ShapeDtypeStruct((S, D), F32), jax.ShapeDtypeStruct((8, D), F32)],
        compiler_params=_params(("arbitrary",)),
    )(dh, x, dx2, g, ada)


def _gate_bwd(dm, pa, pb, proj, gla_col, glb_col):
    S, D = pa.shape
    tm = _tile(S, 256)

    def body(dm_ref, pa_ref, pb_ref, la_ref, lb_ref, dpa_ref, dpb_ref, dla_ref, dlb_ref):
        dm_ = dm_ref[...]
        ga = jax.nn.sigmoid(la_ref[...])
        gb = jax.nn.sigmoid(lb_ref[...])
        dpa_ref[...] = (dm_ * ga).astype(BF16)
        dpb_ref[...] = (dm_ * gb).astype(BF16)
        dla_ref[...] = (dm_ * pa_ref[...] * ga * (1.0 - ga)).astype(BF16)
        dlb_ref[...] = (dm_ * pb_ref[...] * gb * (1.0 - gb)).astype(BF16)

    return pl.pallas_call(
        body, name="gate_bwd", grid=(S // tm,),
        in_specs=[_rows(tm, D)] * 3 + [_rows(tm, D, gla_col), _rows(tm, D, glb_col)],
        out_specs=[_rows(tm, D)] * 4, out_shape=[jax.ShapeDtypeStruct((S, D), BF16)] * 4,
        compiler_params=_params(("parallel",)),
    )(dm, pa, pb, proj, proj)


def _q_prep_bwd(dq, q0, g_qh, tabs, nh):
    S = q0.shape[0]
    tm = _tile(S, 256)

    def body(dq_ref, q_ref, g_ref, c_ref, s1_ref, s2_ref, o_ref, red_ref):
        @pl.when(pl.program_id(0) == 0)
        def _():
            red_ref[...] = jnp.zeros_like(red_ref)

        c, s1, s2, g = c_ref[...], s1_ref[...], s2_ref[...], g_ref[...]
        dg = jnp.zeros((1, HEAD_PAD), F32)
        for h in range(nh):
            sl = slice(h * HEAD_PAD, (h + 1) * HEAD_PAD)
            d1 = _rope_bwd(dq_ref[:, sl], c, s1, s2)
            xs = q_ref[:, sl]
            r = _head_rms(xs)
            qn = xs * r
            dg = dg + jnp.sum(d1 * qn, axis=0, keepdims=True)
            dn = d1 * g
            o_ref[:, sl] = (r * (dn - qn * (jnp.sum(dn * qn, axis=-1, keepdims=True) * (1.0 / QK_DIM)))).astype(BF16)
        red_ref[0:1, :] += dg

    w = nh * HEAD_PAD
    return pl.pallas_call(
        body, name="mla_q_prep_bwd", grid=(S // tm,),
        in_specs=[_rows(tm, w), _rows(tm, w), _vec(HEAD_PAD)] + [_rows(tm, HEAD_PAD)] * 3,
        out_specs=[_rows(tm, w), _vec(HEAD_PAD, rows=8)],
        out_shape=[jax.ShapeDtypeStruct((S, w), BF16), jax.ShapeDtypeStruct((8, HEAD_PAD), F32)],
        compiler_params=_params(("arbitrary",)),
    )(dq, q0, g_qh, *tabs)


def _k_prep_bwd(dk, dv, kv0, proj, kpe_col, g_kh, tabs, nh):
    S = kv0.shape[0]
    tm = _tile(S, 256)
    wv = nh * HEAD

    def body(dk_ref, dv_ref, kv_ref, kpe_ref, g_ref, c_ref, s1_ref, s2_ref, o_ref, dpe_ref, red_ref):
        @pl.when(pl.program_id(0) == 0)
        def _():
            red_ref[...] = jnp.zeros_like(red_ref)

        c, s1, s2, g = c_ref[...], s1_ref[...], s2_ref[...], g_ref[...]
        kpe = kpe_ref[...]
        dg = jnp.zeros((1, HEAD_PAD), F32)
        dpe = jnp.zeros((tm, LANE), F32)
        for h in range(nh):
            d1 = _rope_bwd(dk_ref[:, h * HEAD_PAD:(h + 1) * HEAD_PAD], c, s1, s2)
            k0 = jnp.concatenate([kv_ref[:, h * HEAD:(h + 1) * HEAD], kpe], axis=1)
            r = _head_rms(k0)
            kn = k0 * r
            dg = dg + jnp.sum(d1 * kn, axis=0, keepdims=True)
            dn = d1 * g
            dk0 = r * (dn - kn * (jnp.sum(dn * kn, axis=-1, keepdims=True) * (1.0 / QK_DIM)))
            o_ref[:, h * HEAD:(h + 1) * HEAD] = dk0[:, :HEAD].astype(BF16)
            dpe = dpe + dk0[:, HEAD:]
        o_ref[:, wv:] = dv_ref[...].astype(BF16)
        dpe_ref[...] = dpe.astype(BF16)
        red_ref[0:1, :] += dg

    return pl.pallas_call(
        body, name="mla_k_prep_bwd", grid=(S // tm,),
        in_specs=[_rows(tm, nh * HEAD_PAD), _rows(tm, wv), _rows(tm, wv, 0), _rows(tm, LANE, kpe_col),
                  _vec(HEAD_PAD)] + [_rows(tm, HEAD_PAD)] * 3,
        out_specs=[_rows(tm, 2 * wv), _rows(tm, LANE), _vec(HEAD_PAD, rows=8)],
        out_shape=[jax.ShapeDtypeStruct((S, 2 * wv), BF16), jax.ShapeDtypeStruct((S, LANE), BF16),
                   jax.ShapeDtypeStruct((8, HEAD_PAD), F32)],
        compiler_params=_params(("arbitrary",)),
    )(dk, dv, kv0, proj, g_kh, *tabs)


def _latent_norm_bwd(dcqn, dckvn, proj, g_q, g_kv, ql):
    S = proj.shape[0]
    tm = _tile(S, 512)

    def body(dq_ref, dkv_ref, cq_ref, ckv_ref, gq_ref, gkv_ref, oq_ref, okv_ref, red_ref):
        @pl.when(pl.program_id(0) == 0)
        def _():
            red_ref[...] = jnp.zeros_like(red_ref)

        for row, (d_ref, c_ref, g_ref, o_ref) in enumerate(((dq_ref, cq_ref, gq_ref, oq_ref),
                                                            (dkv_ref, ckv_ref, gkv_ref, okv_ref))):
            d = d_ref[...]
            cv = c_ref[...]
            r = _rms_rows(cv)
            ch = cv * r
            red_ref[row:row + 1, :] += jnp.sum(d * ch, axis=0, keepdims=True)
            dn = d * g_ref[...]
            o_ref[...] = (r * (dn - ch * jnp.mean(dn * ch, axis=-1, keepdims=True))).astype(BF16)

    return pl.pallas_call(
        body, name="latent_norm_bwd", grid=(S // tm,),
        in_specs=[_rows(tm, ql), _rows(tm, ql), _rows(tm, ql, 0), _rows(tm, ql, 1), _vec(ql), _vec(ql)],
        out_specs=[_rows(tm, ql), _rows(tm, ql), _vec(ql, rows=8)],
        out_shape=[jax.ShapeDtypeStruct((S, ql), BF16)] * 2 + [jax.ShapeDtypeStruct((8, ql), F32)],
        compiler_params=_params(("arbitrary",)),
    )(dcqn, dckvn, proj, proj, g_q, g_kv)


NEG = -1e30
ATT_TILE = 512
SB_SUB = 128
_NT = (((1,), (1,)), ((), ()))
_TN = (((0,), (0,)), ((), ()))


def _dot(a, b, dn=(((1,), (0,)), ((), ()))):
    return lax.dot_general(a, b, dn, preferred_element_type=F32)


def _key_rows(kb, t):
    return pl.ds(pl.multiple_of(kb * t, t), t)


def _diag_mask(t, strict):
    r = lax.broadcasted_iota(jnp.int32, (t, t), 0)
    c = lax.broadcasted_iota(jnp.int32, (t, t), 1)
    return c < r if strict else c <= r


def _mla_fwd(q, k, kv0, nh):
    S = q.shape[0]
    t = _tile(S, ATT_TILE)

    def body(q_ref, k_ref, v_ref, o_ref, lse_ref):
        i = pl.program_id(1)
        qv = q_ref[...]

        def block(kb, carry, masked):
            m, l, acc = carry
            rows = _key_rows(kb, t)
            s = _dot(qv, k_ref[rows, :], _NT)
            if masked:
                s = jnp.where(_diag_mask(t, False), s, NEG)
            m_new = jnp.maximum(m, jnp.max(s, axis=-1, keepdims=True))
            alpha = jnp.exp(m - m_new)
            p = jnp.exp(s - m_new)
            l = alpha * l + jnp.sum(p, axis=-1, keepdims=True)
            acc = alpha * acc + _dot(p.astype(BF16), v_ref[rows, :].astype(BF16))
            return m_new, l, acc

        init = (jnp.full((t, 1), NEG, F32), jnp.zeros((t, 1), F32), jnp.zeros((t, HEAD), F32))
        carry = lax.fori_loop(0, i, lambda kb, c: block(kb, c, False), init)
        m, l, acc = block(i, carry, True)
        o_ref[...] = acc / l
        lse_ref[...] = m + jnp.log(l)

    return pl.pallas_call(
        body, name="mla_attn_fwd", grid=(nh, S // t),
        in_specs=[pl.BlockSpec((t, HEAD_PAD), lambda h, i: (i, h)),
                  pl.BlockSpec((S, HEAD_PAD), lambda h, i: (0, h)),
                  pl.BlockSpec((S, HEAD), lambda h, i: (0, nh + h))],
        out_specs=[pl.BlockSpec((t, HEAD), lambda h, i: (i, h)),
                   pl.BlockSpec((None, t, 1), lambda h, i: (h, i, 0))],
        out_shape=[jax.ShapeDtypeStruct((S, nh * HEAD), F32), jax.ShapeDtypeStruct((nh, S, 1), F32)],
        compiler_params=_params(("parallel", "arbitrary")),
    )(q, k, kv0)


def _mla_bwd(q, k, kv0, o, do, lse, nh):
    S = q.shape[0]
    t = _tile(S, ATT_TILE)
    scale = QK_DIM ** -0.5

    def body(q_ref, k_ref, v_ref, o_ref, do_ref, lse_ref, dq_ref, dk_ref, dv_ref):
        i = pl.program_id(1)

        @pl.when(i == 0)
        def _():
            dk_ref[...] = jnp.zeros_like(dk_ref)
            dv_ref[...] = jnp.zeros_like(dv_ref)

        qv = q_ref[...]
        dov = do_ref[...]
        delta = jnp.sum(dov * o_ref[...], axis=-1, keepdims=True)
        dob = dov.astype(BF16)
        lse = lse_ref[...]

        def block(kb, dq, masked):
            rows = _key_rows(kb, t)
            ks = k_ref[rows, :]
            vs = v_ref[rows, :].astype(BF16)
            p = jnp.exp(_dot(qv, ks, _NT) - lse)
            if masked:
                p = jnp.where(_diag_mask(t, False), p, 0.0)
            ds = (p * (_dot(dob, vs, _NT) - delta)).astype(BF16)
            dk_ref[rows, :] += _dot(ds, qv, _TN)
            dv_ref[rows, :] += _dot(p.astype(BF16), dob, _TN)
            return dq + _dot(ds, ks)

        dq = lax.fori_loop(0, i, lambda kb, c: block(kb, c, False), jnp.zeros((t, HEAD_PAD), F32))
        dq_ref[...] = block(i, dq, True) * scale

    return pl.pallas_call(
        body, name="mla_attn_bwd", grid=(nh, S // t),
        in_specs=[pl.BlockSpec((t, HEAD_PAD), lambda h, i: (i, h)),
                  pl.BlockSpec((S, HEAD_PAD), lambda h, i: (0, h)),
                  pl.BlockSpec((S, HEAD), lambda h, i: (0, nh + h)),
                  pl.BlockSpec((t, HEAD), lambda h, i: (i, h)),
                  pl.BlockSpec((t, HEAD), lambda h, i: (i, h)),
                  pl.BlockSpec((None, t, 1), lambda h, i: (h, i, 0))],
        out_specs=[pl.BlockSpec((t, HEAD_PAD), lambda h, i: (i, h)),
                   pl.BlockSpec((S, HEAD_PAD), lambda h, i: (0, h)),
                   pl.BlockSpec((S, HEAD), lambda h, i: (0, h))],
        out_shape=[jax.ShapeDtypeStruct((S, nh * HEAD_PAD), F32), jax.ShapeDtypeStruct((S, nh * HEAD_PAD), F32),
                   jax.ShapeDtypeStruct((S, nh * HEAD), F32)],
        compiler_params=_params(("parallel", "arbitrary")),
    )(q, k, kv0, o, do, lse)


def _tri(n, cmp):
    r = lax.broadcasted_iota(jnp.int32, (n, n), 0)
    c = lax.broadcasted_iota(jnp.int32, (n, n), 1)
    return jnp.where(cmp(r, c), 1.0, 0.0).astype(BF16)


def _sb_block(qv, ks, run, upper, t, masked):
    z = _dot(qv, ks, _NT)
    lb = jnp.minimum(z, 0.0) - jnp.log(1.0 + jnp.exp(-jnp.abs(z)))
    lom = lb - z
    mask = _diag_mask(t, True) if masked else None
    if masked:
        lom = jnp.where(mask, lom, 0.0)
    tails = []
    for sblk in reversed(range(t // SB_SUB)):
        part = lom[:, sblk * SB_SUB:(sblk + 1) * SB_SUB]
        tails.append(_dot(part.astype(BF16), upper) + run)
        run = run + jnp.sum(part, axis=-1, keepdims=True)
    a = jnp.exp(lb + jnp.concatenate(tails[::-1], axis=1))
    if masked:
        a = jnp.where(mask, a, 0.0)
    return a, lb, mask, run


def _sb_fwd(proj, q_col, k_col, v_col, nh):
    S = proj.shape[0]
    t = _tile(S, ATT_TILE)
    scale = HEAD ** -0.5

    def body(q_ref, k_ref, v_ref, o_ref):
        i = pl.program_id(1)
        qv = (q_ref[...] * scale).astype(BF16)
        upper = _tri(SB_SUB, lambda j, s: j > s)

        def block(kb, carry, masked):
            run, acc = carry
            rows = _key_rows(kb, t)
            a, _, _, run = _sb_block(qv, k_ref[rows, :].astype(BF16), run, upper, t, masked)
            return run, acc + _dot(a.astype(BF16), v_ref[rows, :].astype(BF16))

        carry = block(i, (jnp.zeros((t, 1), F32), jnp.zeros((t, HEAD), F32)), True)
        o_ref[...] = lax.fori_loop(0, i, lambda j, c: block(i - 1 - j, c, False), carry)[1]

    return pl.pallas_call(
        body, name="sb_attn_fwd", grid=(nh, S // t),
        in_specs=[pl.BlockSpec((t, HEAD), lambda h, i: (i, q_col + h)),
                  pl.BlockSpec((S, HEAD), lambda h, i: (0, k_col + h)),
                  pl.BlockSpec((S, HEAD), lambda h, i: (0, v_col + h))],
        out_specs=pl.BlockSpec((t, HEAD), lambda h, i: (i, h)),
        out_shape=jax.ShapeDtypeStruct((S, nh * HEAD), F32),
        compiler_params=_params(("parallel", "arbitrary")),
    )(proj, proj, proj)


def _sb_bwd(proj, q_col, k_col, v_col, dy, nh):
    S = proj.shape[0]
    t = _tile(S, ATT_TILE)
    scale = HEAD ** -0.5

    def body(q_ref, k_ref, v_ref, dy_ref, dq_ref, dk_ref, dv_ref, run_ref):
        i = pl.program_id(1)

        @pl.when(i == 0)
        def _():
            dk_ref[...] = jnp.zeros_like(dk_ref)
            dv_ref[...] = jnp.zeros_like(dv_ref)

        qv = (q_ref[...] * scale).astype(BF16)
        dyb = dy_ref[...].astype(BF16)
        upper = _tri(SB_SUB, lambda j, s: j > s)
        before = _tri(SB_SUB, lambda s, j: s < j)

        def suffix(kb, run, masked):
            z = _dot(qv, k_ref[_key_rows(kb, t), :].astype(BF16), _NT)
            lom = jnp.minimum(z, 0.0) - jnp.log(1.0 + jnp.exp(-jnp.abs(z))) - z
            if masked:
                lom = jnp.where(_diag_mask(t, True), lom, 0.0)
            run_ref[kb] = jnp.broadcast_to(run, (t, LANE))
            return run + jnp.sum(lom, axis=-1, keepdims=True)

        run0 = suffix(i, jnp.zeros((t, 1), F32), True)
        lax.fori_loop(0, i, lambda j, r: suffix(i - 1 - j, r, False), run0)

        def block(kb, carry, masked):
            prefix, dq = carry
            rows = _key_rows(kb, t)
            ks = k_ref[rows, :].astype(BF16)
            vs = v_ref[rows, :].astype(BF16)
            a, lb, mask, _ = _sb_block(qv, ks, run_ref[kb][:, 0:1], upper, t, masked)
            dl = a * _dot(dyb, vs, _NT)
            lefts = []
            for sblk in range(t // SB_SUB):
                part = dl[:, sblk * SB_SUB:(sblk + 1) * SB_SUB]
                lefts.append(_dot(part.astype(BF16), before) + prefix)
                prefix = prefix + jnp.sum(part, axis=-1, keepdims=True)
            beta = jnp.exp(lb)
            dz = dl * (1.0 - beta) - beta * jnp.concatenate(lefts, axis=1)
            if masked:
                dz = jnp.where(mask, dz, 0.0)
            dz = dz.astype(BF16)
            dk_ref[rows, :] += _dot(dz, qv, _TN)
            dv_ref[rows, :] += _dot(a.astype(BF16), dyb, _TN)
            return prefix, dq + _dot(dz, ks)

        carry = lax.fori_loop(0, i, lambda kb, c: block(kb, c, False),
                              (jnp.zeros((t, 1), F32), jnp.zeros((t, HEAD), F32)))
        dq_ref[...] = block(i, carry, True)[1] * scale

    full = pl.BlockSpec((S, HEAD), lambda h, i: (0, h))
    tile = pl.BlockSpec((t, HEAD), lambda h, i: (i, h))
    return pl.pallas_call(
        body, name="sb_attn_bwd", grid=(nh, S // t),
        in_specs=[pl.BlockSpec((t, HEAD), lambda h, i: (i, q_col + h)),
                  pl.BlockSpec((S, HEAD), lambda h, i: (0, k_col + h)),
                  pl.BlockSpec((S, HEAD), lambda h, i: (0, v_col + h)), tile],
        out_specs=[tile, full, full],
        out_shape=[jax.ShapeDtypeStruct((S, nh * HEAD), F32)] * 3,
        scratch_shapes=[pltpu.VMEM((S // t, t, LANE), F32)],
        compiler_params=_params(("parallel", "arbitrary")),
    )(proj, proj, proj, dy)


def _place():
    return lax.axis_index("x"), lax.axis_index("y"), lax.axis_index("c")


def _other_chips(x, y):
    return [(1 - x, y), (x, 1 - y), (1 - x, 1 - y)]


def _dev_index(p):
    return 4 * p[0] + 2 * p[1] + p[2]


def _gather_blocks(blocks, *, name, in_vmem):
    n = len(blocks)
    per = 7

    def body(*refs):
        ins, outs = refs[:n], refs[n:2 * n]
        send_sems, recv_sems, local_sems = refs[2 * n:]
        x, y, c = _place()
        me, sibling = (x, y, c), (x, y, 1 - c)
        chips = _other_chips(x, y)

        def slot(a, p):
            return outs[a].at[_dev_index(p)]

        def copy(a, k, block, to, src=None):
            return pltpu.make_async_remote_copy(
                src_ref=slot(a, block) if src is None else src, dst_ref=slot(a, block),
                send_sem=send_sems.at[a * per + k], recv_sem=recv_sems.at[a * per + k],
                device_id=to, device_id_type=MESH)

        mine = [pltpu.make_async_copy(ins[a], slot(a, me), local_sems.at[a]) for a in range(n)] if in_vmem else []
        for cp in mine:
            cp.start()
        first = []
        for a in range(n):
            first.append(copy(a, 0, me, sibling, src=ins[a]))
            first += [copy(a, 1 + j, me, (*chip, c), src=ins[a]) for j, chip in enumerate(chips)]
        for cp in first:
            cp.start()
        passed = []
        for a in range(n):
            for j, chip in enumerate(chips):
                copy(a, 1 + j, (*chip, c), me).wait_recv()
                cp = copy(a, 4 + j, (*chip, c), sibling)
                cp.start()
                passed.append(cp)
        for a in range(n):
            copy(a, 0, sibling, me).wait_recv()
            for j, chip in enumerate(chips):
                copy(a, 4 + j, (*chip, 1 - c), me).wait_recv()
        for cp in first + passed:
            cp.wait_send()
        for cp in mine:
            cp.wait()

    space = pltpu.VMEM if in_vmem else pl.ANY
    spec = pl.BlockSpec(memory_space=space)
    outs = pl.pallas_call(
        body, name=name, in_specs=[spec] * n, out_specs=[spec] * n,
        out_shape=[jax.ShapeDtypeStruct((N_DEV,) + b.shape, b.dtype) for b in blocks],
        scratch_shapes=[pltpu.SemaphoreType.DMA((n * per,)), pltpu.SemaphoreType.DMA((n * per,)),
                        pltpu.SemaphoreType.DMA((n,))],
        compiler_params=pltpu.CompilerParams(vmem_limit_bytes=VMEM_LIMIT),
    )(*blocks)
    return list(outs)


def _sibling_swap(arrs, *, name, whole=False):
    n = len(arrs)

    def body(*refs):
        ins, outs = refs[:n], refs[n:2 * n]
        send_sems, recv_sems = refs[2 * n:]
        x, y, c = _place()
        copies = [pltpu.make_async_remote_copy(
            src_ref=ins[a] if whole else ins[a].at[1 - c], dst_ref=outs[a],
            send_sem=send_sems.at[a], recv_sem=recv_sems.at[a],
            device_id=(x, y, 1 - c), device_id_type=MESH) for a in range(n)]
        for cp in copies:
            cp.start()
        for cp in copies:
            cp.wait()

    spec = pl.BlockSpec(memory_space=pl.ANY)
    return list(pl.pallas_call(
        body, name=name, in_specs=[spec] * n, out_specs=[spec] * n,
        out_shape=[jax.ShapeDtypeStruct(a.shape if whole else a.shape[1:], a.dtype) for a in arrs],
        scratch_shapes=[pltpu.SemaphoreType.DMA((n,)), pltpu.SemaphoreType.DMA((n,))],
    )(*arrs))


_HBM = pl.BlockSpec(memory_space=pltpu.HBM)
_SEM = pl.BlockSpec(memory_space=pltpu.SEMAPHORE)
_EFFECT = pltpu.SideEffectType.DATAFLOW_SIDE_EFFECTING


def _in_hbm(a):
    return pltpu.with_memory_space_constraint(a, pltpu.HBM)


def _split_copies(srcs, lands, send_sems, recv_sems, plan):
    x, y, c = _place()
    copies = []
    for a, (src, land) in enumerate(zip(srcs, lands)):
        steps = plan(x, y, c)
        for k, (pick, slot, to) in enumerate(steps):
            copies.append(pltpu.make_async_remote_copy(
                src_ref=pick(src), dst_ref=slot(land), send_sem=send_sems.at[a * len(steps) + k],
                recv_sem=recv_sems.at[a * len(steps) + k], device_id=to, device_id_type=MESH))
    return copies


def _split_start(srcs, land_shapes, plan, per, *, name):
    n = len(srcs)

    def body(*refs):
        send_sems, recv_sems = refs[2 * n], refs[2 * n + 1]
        for cp in _split_copies(refs[:n], refs[n:2 * n], send_sems, recv_sems, plan):
            cp.start()
        token = refs[-1]
        token[...] = jnp.zeros_like(token)

    lands = [_in_hbm(lax.empty(s.shape, s.dtype)) for s in land_shapes]
    outs = pl.pallas_call(
        body, name=name,
        out_shape=(pltpu.SemaphoreType.DMA((n * per,)), pltpu.SemaphoreType.DMA((n * per,)),
                   *[pltpu.HBM(s.shape, s.dtype) for s in srcs], *[pltpu.HBM(s.shape, s.dtype) for s in land_shapes],
                   jax.ShapeDtypeStruct((8, LANE), F32)),
        in_specs=[_HBM] * (2 * n),
        out_specs=(_SEM, _SEM, *[_HBM] * (2 * n), pl.BlockSpec(memory_space=pltpu.VMEM)),
        input_output_aliases={i: 2 + i for i in range(2 * n)},
        compiler_params=pltpu.CompilerParams(has_side_effects=_EFFECT),
    )(*[_in_hbm(s) for s in srcs], *lands)
    return outs[0], outs[1], list(outs[2:2 + n]), list(outs[2 + n:2 + 2 * n]), outs[-1]


def _split_wait(send_sems, recv_sems, srcs, lands, after, plan, *, name):
    n = len(srcs)

    def body(*refs):
        for cp in _split_copies(refs[:n], refs[n:2 * n], refs[2 * n], refs[2 * n + 1], plan):
            cp.wait_send()
            cp.wait_recv()

    outs = pl.pallas_call(
        body, name=name,
        out_shape=(*[pltpu.HBM(s.shape, s.dtype) for s in srcs], *[pltpu.HBM(s.shape, s.dtype) for s in lands]),
        in_specs=[_HBM] * (2 * n) + [_SEM, _SEM, pl.BlockSpec(memory_space=pl.ANY)],
        out_specs=tuple([_HBM] * (2 * n)),
        input_output_aliases={i: i for i in range(2 * n)},
        compiler_params=pltpu.CompilerParams(has_side_effects=_EFFECT),
    )(*srcs, *lands, send_sems, recv_sems, after)
    return list(outs[:n]), list(outs[n:])


def _gather_plan(x, y, c):
    slot = lambda land: land.at[_dev_index((x, y, c))]
    whole = lambda src: src
    return [(whole, slot, (x, y, 1 - c))] + [(whole, slot, (px, py, c)) for px, py in _other_chips(x, y)]


def _exchange_plan(x, y, c):
    return [(lambda src, k=2 * px + py: src.at[k], lambda land, j=j: land.at[j], (px, py, c))
            for j, (px, py) in enumerate(_other_chips(x, y))]


def _gather_forward(lands, *, name):
    n = len(lands)

    def body(*refs):
        lands_in, outs = refs[:n], refs[n:2 * n]
        send_sems, recv_sems = refs[2 * n:]
        x, y, c = _place()
        copies = []
        for a in range(n):
            for j, (px, py) in enumerate(_other_chips(x, y)):
                copies.append((pltpu.make_async_remote_copy(
                    src_ref=lands_in[a].at[_dev_index((px, py, c))], dst_ref=outs[a].at[_dev_index((px, py, c))],
                    send_sem=send_sems.at[3 * a + j], recv_sem=recv_sems.at[3 * a + j],
                    device_id=(x, y, 1 - c), device_id_type=MESH), a, j, (px, py)))
        for cp, _, _, _ in copies:
            cp.start()
        for cp, a, j, (px, py) in copies:
            cp.wait_send()
            pltpu.make_async_remote_copy(
                src_ref=lands_in[a].at[_dev_index((px, py, 1 - c))], dst_ref=outs[a].at[_dev_index((px, py, 1 - c))],
                send_sem=send_sems.at[3 * a + j], recv_sem=recv_sems.at[3 * a + j],
                device_id=(x, y, 1 - c), device_id_type=MESH).wait_recv()

    spec = pl.BlockSpec(memory_space=pl.ANY)
    return list(pl.pallas_call(
        body, name=name, in_specs=[spec] * n, out_specs=[spec] * n,
        out_shape=[jax.ShapeDtypeStruct(a.shape, a.dtype) for a in lands],
        input_output_aliases={a: a for a in range(n)},
        scratch_shapes=[pltpu.SemaphoreType.DMA((3 * n,)), pltpu.SemaphoreType.DMA((3 * n,))],
    )(*lands))


def _flat2(a, lead):
    return a.reshape(a.shape[:lead] + (-1, a.shape[-1]))


def _pair_sum(g, recv, c_idx, *, name):
    _, nchip, r, w = g.shape
    tm = _tile(r, 256) if r % 8 == 0 else r

    def body(c_ref, g_ref, r_ref, o_ref):
        o_ref[...] = (g_ref[...].astype(F32) + r_ref[...].astype(F32)).astype(o_ref.dtype)

    return pl.pallas_call(
        body, name=name,
        grid_spec=pltpu.PrefetchScalarGridSpec(
            num_scalar_prefetch=1, grid=(nchip, r // tm),
            in_specs=[pl.BlockSpec((None, None, tm, w), lambda k, i, c_ref: (c_ref[0], k, i, 0)),
                      pl.BlockSpec((None, tm, w), lambda k, i, c_ref: (k, i, 0))],
            out_specs=pl.BlockSpec((None, tm, w), lambda k, i, c_ref: (k, i, 0))),
        out_shape=jax.ShapeDtypeStruct((nchip, r, w), BF16),
        compiler_params=_params(("parallel", "parallel")),
    )(c_idx, g, recv)


def _chip_sum(s1, recv, chip_idx, *, name):
    _, r, w = s1.shape
    tm = _tile(r, 256) if r % 8 == 0 else r

    def body(k_ref, s_ref, r_ref, o_ref):
        acc = s_ref[...].astype(F32)
        for j in range(3):
            acc = acc + r_ref[j].astype(F32)
        o_ref[...] = acc

    return pl.pallas_call(
        body, name=name,
        grid_spec=pltpu.PrefetchScalarGridSpec(
            num_scalar_prefetch=1, grid=(r // tm,),
            in_specs=[pl.BlockSpec((None, tm, w), lambda i, k_ref: (k_ref[0], i, 0)),
                      pl.BlockSpec((3, tm, w), lambda i, k_ref: (0, i, 0))],
            out_specs=pl.BlockSpec((tm, w), lambda i, k_ref: (i, 0))),
        out_shape=jax.ShapeDtypeStruct((r, w), F32),
        compiler_params=_params(("parallel",)),
    )(chip_idx, s1, recv)


def _adam_math(w, g, m, v):
    m = ADAM_B1 * m + (1.0 - ADAM_B1) * g
    v = ADAM_B2 * v + (1.0 - ADAM_B2) * (g * g)
    m_hat = m / (1.0 - ADAM_B1 ** ADAM_STEP)
    v_hat = v / (1.0 - ADAM_B2 ** ADAM_STEP)
    delta = -ADAM_LR * (m_hat / (jnp.sqrt(v_hat) + ADAM_EPS) + ADAM_WD * w)
    return delta, m, v


def _adamw(w, mine, other, c_idx, m, v, *, name):
    r, cw = w.shape
    hr = r // 2
    tm = _row_tile(hr, 9 * cw * 4)

    def body(c_ref, w_ref, a_ref, b_ref, m_ref, v_ref, g_ref, d_ref, nm_ref, nv_ref):
        g = jnp.where(pl.program_id(0) == c_ref[0], a_ref[...], b_ref[...])
        g_ref[...] = g
        d_ref[...], nm_ref[...], nv_ref[...] = _adam_math(w_ref[...], g, m_ref[...], v_ref[...])

    full = pl.BlockSpec((None, tm, cw), lambda h, i, c_ref: (h, i, 0))
    half = pl.BlockSpec((tm, cw), lambda h, i, c_ref: (i, 0))
    outs = pl.pallas_call(
        body, name=name,
        grid_spec=pltpu.PrefetchScalarGridSpec(
            num_scalar_prefetch=1, grid=(2, hr // tm),
            in_specs=[full, half, half, full, full], out_specs=[full] * 4),
        out_shape=[jax.ShapeDtypeStruct((2, hr, cw), F32)] * 4,
        compiler_params=_params(("parallel", "parallel")),
    )(c_idx, w.reshape(2, hr, cw), mine, other, m.reshape(2, hr, cw), v.reshape(2, hr, cw))
    return [o.reshape(r, cw) for o in outs]


def _adamw_ada(cact_t, dada, w, m, v):
    r, cw = w.shape
    nb = cact_t.shape[1]
    tm = _tile(r, 256)
    tn = _tile(cw, 1024)

    def body(a_ref, d_ref, w_ref, m_ref, v_ref, g_ref, dl_ref, nm_ref, nv_ref):
        a = a_ref[...]
        d = d_ref[...]
        g = a[:, 0:1] * d[0:1, :]
        for b in range(1, nb):
            g = g + a[:, b:b + 1] * d[b:b + 1, :]
        g_ref[...] = g
        dl_ref[...], nm_ref[...], nv_ref[...] = _adam_math(w_ref[...], g, m_ref[...], v_ref[...])

    blk = pl.BlockSpec((tm, tn), lambda i, j: (i, j))
    return pl.pallas_call(
        body, name="adamw_ada", grid=(r // tm, cw // tn),
        in_specs=[pl.BlockSpec((tm, nb), lambda i, j: (i, 0)), pl.BlockSpec((nb, tn), lambda i, j: (0, j)), blk, blk, blk],
        out_specs=[blk] * 4, out_shape=[jax.ShapeDtypeStruct((r, cw), F32)] * 4,
        compiler_params=_params(("parallel", "parallel")),
    )(cact_t, dada, w, m, v)


def _adamw_vec(parts, w, m, v):
    n = w.shape[1]

    def body(p_ref, w_ref, m_ref, v_ref, g_ref, d_ref, nm_ref, nv_ref):
        p = p_ref[...]
        g = p[0:1, :]
        for b in range(1, N_DEV):
            g = g + p[b:b + 1, :]
        g_ref[...] = g
        d_ref[...], nm_ref[...], nv_ref[...] = _adam_math(w_ref[...], g, m_ref[...], v_ref[...])

    return pl.pallas_call(
        body, name="adamw_vec", out_shape=[jax.ShapeDtypeStruct((1, n), F32)] * 4,
        compiler_params=pltpu.CompilerParams(vmem_limit_bytes=VMEM_LIMIT),
    )(parts, w, m, v)


def _w_in_segments(kpe0, d_in, cs):
    segs = []
    for k in range(4):
        lo, hi = k * cs, (k + 1) * cs
        for a, b, shift in ((0, kpe0, 0), (kpe0, kpe0 + ROPE, d_in - ROPE - kpe0), (kpe0 + ROPE, d_in, -ROPE)):
            a, b = max(lo, a), min(hi, b)
            if a < b:
                segs.append((k, a - lo, a + shift, b - a))
    return segs


def _w_in_layout(g8, kpe0):
    _, hr, cs = g8.shape
    rows, d_in = 2 * hr, 4 * cs
    segs = _w_in_segments(kpe0, d_in, cs)
    tm = _tile(rows, 256)

    def body(g_ref, o_ref):
        for k, src, dst, w in segs:
            o_ref[:, dst:dst + w] = g_ref[k, :, src:src + w]
        o_ref[:, d_in:] = jnp.zeros((tm, ROPE), o_ref.dtype)

    return pl.pallas_call(
        body, name="w_in_layout", grid=(rows // tm,),
        in_specs=[pl.BlockSpec((4, tm, cs), lambda i: (0, i, 0))], out_specs=_rows(tm, d_in + ROPE),
        out_shape=jax.ShapeDtypeStruct((rows, d_in + ROPE), g8.dtype), compiler_params=_params(("parallel",)),
    )(g8.reshape(4, rows, cs))


def _w_in_grad_pieces(g, kpe0):
    rows, d_in_p = g.shape
    d_in = d_in_p - ROPE
    cs = d_in // 4
    segs = _w_in_segments(kpe0, d_in, cs)
    hr = rows // 2
    tm = _tile(hr, 256)
    per_half = hr // tm

    def body(g_ref, o_ref):
        for k, src, dst, w in segs:
            o_ref[k, :, src:src + w] = g_ref[:, dst:dst + w]

    return pl.pallas_call(
        body, name="w_in_grad_pieces", grid=(rows // tm,),
        in_specs=[_rows(tm, d_in_p)],
        out_specs=pl.BlockSpec((None, 4, tm, cs), lambda i: (i // per_half, 0, i % per_half, 0)),
        out_shape=jax.ShapeDtypeStruct((2, 4, hr, cs), g.dtype), compiler_params=_params(("parallel",)),
    )(g)


def _cols_from_chips(g8, rows):
    cs = g8.shape[-1]
    return g8.reshape(4, rows, cs).transpose(1, 0, 2).reshape(rows, 4 * cs)


def _cols_to_pieces(g):
    rows, c4 = g.shape
    return g.reshape(2, rows // 2, 4, c4 // 4).transpose(0, 2, 1, 3)


def _rows_to_pieces(g):
    r4, cols = g.shape
    return g.reshape(4, 2, r4 // 8, cols).transpose(1, 0, 2, 3)


def _pad_cols(a, w):
    return jnp.pad(a, ((0, 0), (0, w - a.shape[1])))


def kernel(x, c, positions, w_ada, b_ada, g_norm1, g_norm2, w_in, g_q_latent, g_kv_latent, w_uq, w_ukv, g_q_head, g_k_head, w_proj_mla, w_proj_sb, w_out, w_ffn_in, w_ffn_out, loss_target, m_w_ada, m_b_ada, m_g_norm1, m_g_norm2, m_w_in, m_g_q_latent, m_g_kv_latent, m_w_uq, m_w_ukv, m_g_q_head, m_g_k_head, m_w_proj_mla, m_w_proj_sb, m_w_out, m_w_ffn_in, m_w_ffn_out, v_w_ada, v_b_ada, v_g_norm1, v_g_norm2, v_w_in, v_g_q_latent, v_g_kv_latent, v_w_uq, v_w_ukv, v_g_q_head, v_g_k_head, v_w_proj_mla, v_w_proj_sb, v_w_out, v_w_ffn_in, v_w_ffn_out):
    xi, yi, ci = _place()
    chip = 2 * xi + yi
    dev = 2 * chip + ci
    c_idx = jnp.reshape(ci, (1,)).astype(jnp.int32)
    chip_idx = jnp.reshape(chip, (1,)).astype(jnp.int32)

    x = x[0]
    tgt = loss_target[0]
    S, D = x.shape
    ql = g_q_latent.shape[1]
    assert g_kv_latent.shape[1] == ql
    mlaw = w_proj_mla.shape[1]
    nh = mlaw // HEAD
    sbw = w_proj_sb.shape[1]
    assert sbw == mlaw
    dff = w_ffn_out.shape[1] * 4
    d_in = 2 * ql + ROPE + 3 * sbw + 2 * D
    d_in_p = d_in + ROPE
    q_col = (2 * ql) // HEAD
    k_col = q_col + nh
    v_col = k_col + nh
    gla_col = (2 * ql + 3 * sbw) // D
    glb_col = gla_col + 1
    kpe_col = (d_in - ROPE) // LANE
    assert (2 * ql + 3 * sbw) % D == 0 and (d_in - ROPE) % LANE == 0

    mats = {"w_in": w_in[0], "w_uq": w_uq[0], "w_ukv": w_ukv[0], "w_proj_mla": w_proj_mla[0],
            "w_proj_sb": w_proj_sb[0], "w_out": w_out[0], "w_ffn_in": w_ffn_in[0], "w_ffn_out": w_ffn_out[0]}
    names = list(mats)
    row_sharded = {"w_out", "w_ffn_out"}

    c_all = _gather_blocks([jnp.broadcast_to(c, (8, D))], name="gather_cond", in_vmem=True)[0][:, 0, :]
    n_ada = w_ada.shape[2]
    b_shard = lax.dynamic_slice_in_dim(b_ada, chip * n_ada, n_ada, axis=1)
    ada_shard = _mm(c_all, w_ada[0], name="ada_proj", a_fn=jax.nn.silu, bias=b_shard)
    ada_all = _gather_blocks([ada_shard], name="gather_ada", in_vmem=True)[0]
    ada_rows = lax.dynamic_index_in_dim(ada_all, dev, axis=1, keepdims=False)
    ada = ada_rows[0::2].reshape(1, 4 * n_ada)
    SH1, SC1, GT1, SH2, SC2, GT2 = range(6)

    def after(dep, a):
        return a + (dep.reshape(-1)[0:1].reshape((1,) * a.ndim) * 0).astype(a.dtype)

    def fill_own(g8, own):
        return lax.dynamic_update_index_in_dim(g8, own, dev, 0)

    halves = []
    for nm in names:
        w = mats[nm]
        hr = w.shape[0] // 2
        halves.append(lax.dynamic_slice_in_dim(w, ci * hr, hr, axis=0).astype(BF16))
    half_of = dict(zip(names, halves))
    early = ["w_in", "w_uq", "w_ukv"]
    late = ["w_proj_mla", "w_proj_sb", "w_out", "w_ffn_in", "w_ffn_out"]
    early_halves = [half_of[nm] for nm in early]
    early_halves[0] = after(ada, early_halves[0])
    early_got = _gather_blocks(early_halves, name="gather_weights", in_vmem=False)
    gathered = {nm: fill_own(g8, own) for nm, g8, own in zip(early, early_got, early_halves)}
    late_halves = [half_of[nm] for nm in late]
    late_halves[0] = after(gathered[early[1]], late_halves[0])
    late_send, late_recv, late_srcs, late_lands, late_token = _split_start(
        late_halves, [jax.ShapeDtypeStruct((N_DEV,) + h.shape, h.dtype) for h in late_halves], _gather_plan, 4,
        name="gather_late_start")
    ada = ada + late_token[0:1, 0:1]

    def full_cols(nm):
        return _cols_from_chips(gathered[nm], mats[nm].shape[0])

    kpe0 = 2 * ql
    w_in_p = _w_in_layout(gathered["w_in"], kpe0)
    w_uq_p = jnp.pad(full_cols("w_uq").reshape(ql, nh, QK_DIM), ((0, 0), (0, 0), (0, HEAD_PAD - QK_DIM))
                     ).reshape(ql, nh * HEAD_PAD)
    w_ukv4 = full_cols("w_ukv").reshape(ql, nh, 2 * HEAD)
    w_ukv_p = jnp.concatenate([w_ukv4[:, :, :HEAD].reshape(ql, mlaw), w_ukv4[:, :, HEAD:].reshape(ql, mlaw)], axis=1)

    half = ROPE // 2
    freqs = ROPE_THETA ** (-jnp.arange(half, dtype=F32) / half)
    ang = positions[0].astype(F32)[:, None] * freqs
    cos, sin = jnp.cos(ang), jnp.sin(ang)
    one = jnp.ones((S, NOPE), F32)
    zero = jnp.zeros((S, NOPE), F32)
    zh = jnp.zeros((S, half), F32)
    tabs = (jnp.concatenate([one, cos, cos, one[:, :HEAD_PAD - QK_DIM]], axis=1),
            jnp.concatenate([zero, zh, sin, zero[:, :HEAD_PAD - QK_DIM]], axis=1),
            jnp.concatenate([zero, -sin, zh, zero[:, :HEAD_PAD - QK_DIM]], axis=1))
    g_qh_p = _pad_cols(g_q_head, HEAD_PAD)
    g_kh_p = _pad_cols(g_k_head, HEAD_PAD)

    h1 = _rmsmod(x, g_norm1, ada, SC1, SH1, name="rmsmod1")
    proj = _mm(h1, w_in_p, name="mm_proj", tn=640)
    cqn, ckvn = _latent_norm(proj, g_q_latent, g_kv_latent, ql)
    q0 = _mm(cqn, w_uq_p, name="mm_q_up")
    kv0 = _mm(ckvn, w_ukv_p, name="mm_kv_up")
    q = _q_prep(q0, g_qh_p, tabs, nh)
    k = _k_prep(kv0, proj, kpe_col, g_kh_p, tabs, nh)
    y_a, lse = _mla_fwd(q, k, kv0, nh)
    y_b = _sb_fwd(proj, q_col, k_col, v_col, nh)
    late_srcs, late_lands = _split_wait(late_send, late_recv, late_srcs, late_lands, y_b, _gather_plan,
                                        name="gather_late_wait")
    late_got = _gather_forward(late_lands, name="gather_late_forward")
    gathered.update({nm: fill_own(g8, own) for nm, g8, own in zip(late, late_got, late_srcs)})
    w_pm = full_cols("w_proj_mla")
    w_ps = full_cols("w_proj_sb")
    w_o = gathered["w_out"].reshape(D, D)
    w_fi = full_cols("w_ffn_in")
    w_fo = gathered["w_ffn_out"].reshape(dff, D)
    pa = _mm(y_a, w_pm, name="mm_proj_mla")
    pb = _mm(y_b, w_ps, name="mm_proj_sb")
    merged = _gate_merge(pa, pb, proj, gla_col, glb_col)
    o = _mm(merged, w_o, name="mm_out")
    x2, h2 = _resid_rmsmod(x, o, g_norm2, ada, GT1, SC2, SH2)
    ff = _mm(h2, w_fi, name="mm_ffn_in", out_dtype=BF16)
    act = _swiglu(ff, dff)
    f = _mm(act, w_fo, name="mm_ffn_out")
    dy, df, red_l, loss_p = _loss_head(x2, f, tgt, ada, GT2)

    dact = _mm(df, w_fo, name="mm_d_act", tb=True)
    def pc(kind):
        if kind == "cols":
            return kind
        return kind if (D // 4) % LANE == 0 and (dff // 4) % LANE == 0 else None

    gw_fo = _mm(act, df, name="mm_gw_ffn_out", ta=True, out_dtype=BF16, pieces=pc("rows"))
    dff_ = _swiglu_bwd(dact, ff, dff)
    dh2 = _mm(dff_, w_fi, name="mm_d_h2", tb=True)
    gw_fi = _mm(h2, dff_, name="mm_gw_ffn_in", ta=True, out_dtype=BF16, pieces=pc("cols"))

    def pair_sums(nms, grads, tag):
        pcs = [g if g.ndim == 4 else (_rows_to_pieces if nm in row_sharded else _cols_to_pieces)(g)
               for nm, g in zip(nms, grads)]
        got = _sibling_swap(pcs, name="rs_sibling_swap_" + tag)
        return [_pair_sum(p, r, c_idx, name="rs_pair_sum_" + nm) for p, r, nm in zip(pcs, got, nms)]

    ffn = ["w_ffn_in", "w_ffn_out"]
    ffn_pair = pair_sums(ffn, [gw_fi, gw_fo], "ffn")
    ffn_send, ffn_recv, ffn_pair, ffn_lands, ffn_token = _split_start(
        ffn_pair, [jax.ShapeDtypeStruct((3,) + p.shape[1:], p.dtype) for p in ffn_pair], _exchange_plan, 3,
        name="rs_exchange_ffn_start")
    ada = ada + ffn_token[0:1, 0:1]
    dx2, do, red_2 = _rmsmod2_bwd(dh2, x2, dy, o, g_norm2, ada, SC2, GT1)
    dmerged = _mm(do, w_o, name="mm_d_merged", tb=True)
    gw_o = _mm(merged, do, name="mm_gw_out", ta=True, out_dtype=BF16, pieces=pc("rows"))
    dpa, dpb, dgla, dglb = _gate_bwd(dmerged, pa, pb, proj, gla_col, glb_col)
    dya = _mm(dpa, w_pm, name="mm_d_ya", tb=True)
    gw_pm = _mm(y_a, dpa, name="mm_gw_proj_mla", ta=True, out_dtype=BF16, pieces=pc("cols"))
    dyb = _mm(dpb, w_ps, name="mm_d_yb", tb=True)
    gw_ps = _mm(y_b, dpb, name="mm_gw_proj_sb", ta=True, out_dtype=BF16, pieces=pc("cols"))
    mid = ["w_proj_mla", "w_proj_sb", "w_out"]
    mid_pair = pair_sums(mid, [gw_pm, gw_ps, gw_o], "mid")
    mid_send, mid_recv, mid_pair, mid_lands, mid_token = _split_start(
        mid_pair, [jax.ShapeDtypeStruct((3,) + p.shape[1:], p.dtype) for p in mid_pair], _exchange_plan, 3,
        name="rs_exchange_mid_start")
    lse = lse + mid_token[0:1, 0:1]
    dq, dk, dv = _mla_bwd(q, k, kv0, y_a, dya, lse, nh)
    dq_sb, dk_sb, dv_sb = _sb_bwd(proj, q_col, k_col, v_col, dyb, nh)
    dq0, red_qh = _q_prep_bwd(dq, q0, g_qh_p, tabs, nh)
    dkv0, dkpe, red_kh = _k_prep_bwd(dk, dv, kv0, proj, kpe_col, g_kh_p, tabs, nh)
    dcqn = _mm(dq0, w_uq_p, name="mm_d_cqn", tb=True)
    gw_uq_p = _mm(cqn, dq0, name="mm_gw_uq", ta=True, out_dtype=BF16)
    dckvn = _mm(dkv0, w_ukv_p, name="mm_d_ckvn", tb=True)
    gw_ukv_p = _mm(ckvn, dkv0, name="mm_gw_ukv", ta=True, out_dtype=BF16)
    dcq, dckv, red_lat = _latent_norm_bwd(dcqn, dckvn, proj, g_q_latent, g_kv_latent, ql)
    dproj = jnp.concatenate([dcq, dckv, dq_sb.astype(BF16), dk_sb.astype(BF16), dv_sb.astype(BF16),
                             dgla, dglb, dkpe], axis=1)
    gw_in_p = _mm(h1, dproj, name="mm_gw_in", ta=True, out_dtype=BF16, tn=640)

    gw_in = _w_in_grad_pieces(gw_in_p, kpe0)
    gw_uq = gw_uq_p.reshape(ql, nh, HEAD_PAD)[:, :, :QK_DIM].reshape(ql, nh * QK_DIM)
    gw_ukv = jnp.concatenate([gw_ukv_p[:, :mlaw].reshape(ql, nh, HEAD), gw_ukv_p[:, mlaw:].reshape(ql, nh, HEAD)],
                             axis=2).reshape(ql, 2 * mlaw)
    last = ["w_in", "w_uq", "w_ukv"]
    assert last + mid + ffn == names

    last_pair = pair_sums(last, [gw_in, gw_uq, gw_ukv], "last")
    last_send, last_recv, last_pair, last_lands, last_token = _split_start(
        last_pair, [jax.ShapeDtypeStruct((3,) + p.shape[1:], p.dtype) for p in last_pair], _exchange_plan, 3,
        name="rs_exchange_last_start")
    ada = ada + last_token[0:1, 0:1]
    dh1 = _mm(dproj, w_in_p, name="mm_d_h1", tb=True, bias=jnp.zeros((1, D), F32) + last_token[0:1, 0:1])
    grad_x, red_1 = _rmsmod1_bwd(dh1, x, dx2, g_norm1, ada, SC1)
    last_pair, last_chips = _split_wait(last_send, last_recv, last_pair, last_lands, grad_x, _exchange_plan,
                                        name="rs_exchange_last_wait")
    mid_pair, mid_chips = _split_wait(mid_send, mid_recv, mid_pair, mid_lands, grad_x, _exchange_plan,
                                      name="rs_exchange_mid_wait")
    ffn_pair, ffn_chips = _split_wait(ffn_send, ffn_recv, ffn_pair, ffn_lands, grad_x, _exchange_plan,
                                      name="rs_exchange_ffn_wait")
    reduced = [_chip_sum(s, r, chip_idx, name="rs_chip_sum_" + nm)
               for s, r, nm in zip(last_pair + mid_pair + ffn_pair, last_chips + mid_chips + ffn_chips, names)]
    from_sibling2 = _sibling_swap(reduced, name="rs_sibling_send", whole=True)

    vec_names = ["b_ada", "g_norm1", "g_norm2", "g_q_latent", "g_kv_latent", "g_q_head", "g_k_head"]
    vec_w = dict(b_ada=b_ada, g_norm1=g_norm1, g_norm2=g_norm2, g_q_latent=g_q_latent, g_kv_latent=g_kv_latent,
                 g_q_head=g_q_head, g_k_head=g_k_head)
    vec_m = dict(b_ada=m_b_ada, g_norm1=m_g_norm1, g_norm2=m_g_norm2, g_q_latent=m_g_q_latent,
                 g_kv_latent=m_g_kv_latent, g_q_head=m_g_q_head, g_k_head=m_g_k_head)
    vec_v = dict(b_ada=v_b_ada, g_norm1=v_g_norm1, g_norm2=v_g_norm2, g_q_latent=v_g_q_latent,
                 g_kv_latent=v_g_kv_latent, g_q_head=v_g_q_head, g_k_head=v_g_k_head)
    d_ada = jnp.concatenate([red_1[0:1], red_1[1:2], red_2[3:4], red_2[0:1], red_2[1:2], red_l[0:1]], axis=1)
    vec_parts = dict(b_ada=d_ada, g_norm1=red_1[2:3], g_norm2=red_2[2:3], g_q_latent=red_lat[0:1],
                     g_kv_latent=red_lat[1:2], g_q_head=red_qh[0:1], g_k_head=red_kh[0:1])
    widths = [-(-vec_w[nm].shape[1] // LANE) * LANE for nm in vec_names]
    offs = [sum(widths[:i]) for i in range(len(widths))]
    pack = lambda d: jnp.concatenate([_pad_cols(d[nm][:, :vec_w[nm].shape[1]], wd) for nm, wd in zip(vec_names, widths)], axis=1)
    nvec = sum(widths) + LANE
    no_loss = jnp.zeros((1, LANE), F32)
    parts = jnp.concatenate([pack(vec_parts), loss_p[0:1, :]], axis=1)
    parts_all = _gather_blocks([jnp.broadcast_to(parts, (8, nvec))], name="gather_vec_grads",
                               in_vmem=True)[0][:, 0, :]
    gvec, dvec, nmvec, nvvec = _adamw_vec(parts_all, *[jnp.concatenate([pack(d), no_loss], axis=1)
                                                       for d in (vec_w, vec_m, vec_v)])
    loss = gvec[0, nvec - LANE]
    unpack = lambda a: {nm: a[:, o_:o_ + vec_w[nm].shape[1]] for nm, o_ in zip(vec_names, offs)}
    gvec, dvec, nmvec, nvvec = unpack(gvec), unpack(dvec), unpack(nmvec), unpack(nvvec)

    dada_all = lax.dynamic_slice_in_dim(parts_all[:, :6 * D], chip * n_ada, n_ada, axis=1)
    cact_t = jax.nn.silu(c_all).T
    g_ada, d_ada_w, nm_ada, nv_ada = _adamw_ada(cact_t, dada_all, w_ada[0], m_w_ada[0], v_w_ada[0])

    ms = dict(w_in=m_w_in, w_uq=m_w_uq, w_ukv=m_w_ukv, w_proj_mla=m_w_proj_mla, w_proj_sb=m_w_proj_sb,
              w_out=m_w_out, w_ffn_in=m_w_ffn_in, w_ffn_out=m_w_ffn_out)
    vs = dict(w_in=v_w_in, w_uq=v_w_uq, w_ukv=v_w_ukv, w_proj_mla=v_w_proj_mla, w_proj_sb=v_w_proj_sb,
              w_out=v_w_out, w_ffn_in=v_w_ffn_in, w_ffn_out=v_w_ffn_out)
    G, DL, NM, NV = {}, {}, {}, {}
    for nm, mine, other in zip(names, reduced, from_sibling2):
        g_, d_, m_, v_ = _adamw(mats[nm], mine, other, c_idx, ms[nm][0], vs[nm][0], name="adamw_" + nm)
        G[nm], DL[nm], NM[nm], NV[nm] = g_[None], d_[None], m_[None], v_[None]
    G["w_ada"], DL["w_ada"], NM["w_ada"], NV["w_ada"] = g_ada[None], d_ada_w[None], nm_ada[None], nv_ada[None]
    for nm in vec_names:
        G[nm], DL[nm], NM[nm], NV[nm] = gvec[nm], dvec[nm], nmvec[nm], nvvec[nm]

    order = ["w_ada", "b_ada", "g_norm1", "g_norm2", "w_in", "g_q_latent", "g_kv_latent", "w_uq", "w_ukv",
             "g_q_head", "g_k_head", "w_proj_mla", "w_proj_sb", "w_out", "w_ffn_in", "w_ffn_out"]
    return (loss, grad_x[None], *[G[n] for n in order], *[DL[n] for n in order],
            *[NM[n] for n in order], *[NV[n] for n in order])
```

```python
import functools
import math

import jax
import jax.numpy as jnp
from jax import lax
from jax.experimental import pallas as pl
from jax.experimental.pallas import tpu as pltpu

F32 = jnp.float32
BF16 = jnp.bfloat16
MESH = pl.DeviceIdType.MESH

EPS = 1e-6
ROPE_THETA = 10000.0
NOPE = 128
ROPE = 64
QK_DIM = NOPE + ROPE
HEAD_PAD = 256
HEAD = 128
N_DEV = 8
LANE = 128
VMEM_LIMIT = 48 * 1024 * 1024

ADAM_LR = 0.001
ADAM_B1 = 0.9
ADAM_B2 = 0.999
ADAM_EPS = 1e-08
ADAM_WD = 0.01
ADAM_STEP = 10


def _tile(n, target):
    if n <= target:
        return n
    t = (target // LANE) * LANE
    while t >= LANE:
        if n % t == 0:
            return t
        t -= LANE
    return n


def _row_tile(rows, row_bytes, budget=24 * 1024 * 1024):
    cap = max(8, budget // (2 * row_bytes))
    best = None
    for t in range(8, min(rows, cap) + 1, 8):
        if rows % t == 0:
            best = t
    return best if best is not None else rows


def _params(sem):
    return pltpu.CompilerParams(dimension_semantics=sem, vmem_limit_bytes=VMEM_LIMIT)


def _rows(tm, w, col=0):
    return pl.BlockSpec((tm, w), lambda i: (i, col))


def _vec(w, col=0, rows=1):
    return pl.BlockSpec((rows, w), lambda i: (0, col))


MM_VMEM_BUDGET = 36 * 1024 * 1024


def _mm(a, b, *, name, ta=False, tb=False, out_dtype=F32, a_fn=None, bias=None, tm=1024, tn=1024, pieces=None):
    M = a.shape[1] if ta else a.shape[0]
    K = a.shape[0] if ta else a.shape[1]
    N = b.shape[0] if tb else b.shape[1]
    assert K == (b.shape[1] if tb else b.shape[0]), (a.shape, b.shape, ta, tb)
    if pieces == "cols":
        tm, tn = _tile(M // 2, tm), _tile(N // 4, tn)
        assert (M // 2) % tm == 0 and (N // 4) % tn == 0
    elif pieces == "rows":
        tm, tn = M // 4, _tile(N, tn)
    else:
        tm, tn = _tile(M, tm), _tile(N, tn)
    sa, sb, so = a.dtype.itemsize, b.dtype.itemsize, jnp.dtype(out_dtype).itemsize

    def fits(tk):
        return 2 * tk * (tm * sa + tn * sb) + tm * tn * (2 * so + 4) <= MM_VMEM_BUDGET

    tk = K
    while not fits(tk):
        smaller = _tile(K, tk - LANE)
        if smaller >= tk:
            break
        tk = smaller
    nk = K // tk
    dn = (((0 if ta else 1,), (1 if tb else 0,)), ((), ()))
    b_outer = nk == 1 and a.size * sa * (N // tn) < b.size * sb * (M // tm)

    def body(*refs):
        a_ref, b_ref = refs[:2]
        bias_ref = refs[2] if bias is not None else None
        o_ref = refs[3 if bias is not None else 2]
        av = a_ref[...]
        if a_fn is not None:
            av = a_fn(av.astype(F32))
        part = lax.dot_general(av.astype(BF16), b_ref[...].astype(BF16), dn, preferred_element_type=F32)

        def finish(r):
            if bias is not None:
                r = r + bias_ref[...]
            if pieces == "rows":
                o_ref[0] = r[:tm // 2].astype(o_ref.dtype)
                o_ref[1] = r[tm // 2:].astype(o_ref.dtype)
            else:
                o_ref[...] = r.astype(o_ref.dtype)

        if nk == 1:
            finish(part)
        else:
            acc_ref = refs[-1]
            k = pl.program_id(2)

            @pl.when(k == 0)
            def _():
                acc_ref[...] = part

            @pl.when(k > 0)
            def _():
                acc_ref[...] += part

            @pl.when(k == nk - 1)
            def _():
                finish(acc_ref[...])

    def ij(g0, g1):
        return (g1, g0) if b_outer else (g0, g1)

    def amap(g0, g1, k):
        i, _ = ij(g0, g1)
        return (k, i) if ta else (i, k)

    def bmap(g0, g1, k):
        _, j = ij(g0, g1)
        return (j, k) if tb else (k, j)

    in_specs = [pl.BlockSpec((tk, tm) if ta else (tm, tk), amap), pl.BlockSpec((tn, tk) if tb else (tk, tn), bmap)]
    args = [a, b]
    if bias is not None:
        in_specs.append(pl.BlockSpec((1, tn), lambda g0, g1, k: (0, ij(g0, g1)[1])))
        args.append(bias)
    grid = (N // tn, M // tm, nk) if b_outer else (M // tm, N // tn, nk)
    if pieces == "cols":
        ni, nj = M // 2 // tm, N // 4 // tn

        def omap(g0, g1, k):
            i, j = ij(g0, g1)
            return (i // ni, j // nj, i % ni, j % nj)

        out_spec = pl.BlockSpec((None, None, tm, tn), omap)
        out_shape = jax.ShapeDtypeStruct((2, 4, M // 2, N // 4), out_dtype)
    elif pieces == "rows":
        out_spec = pl.BlockSpec((2, None, tm // 2, tn), lambda g0, g1, k: (0, ij(g0, g1)[0], 0, ij(g0, g1)[1]))
        out_shape = jax.ShapeDtypeStruct((2, 4, tm // 2, N), out_dtype)
    else:
        out_spec = pl.BlockSpec((tm, tn), lambda g0, g1, k: ij(g0, g1))
        out_shape = jax.ShapeDtypeStruct((M, N), out_dtype)
    return pl.pallas_call(
        body, name=name, grid=grid, in_specs=in_specs, out_specs=out_spec, out_shape=out_shape,
        scratch_shapes=[pltpu.VMEM((tm, tn), F32)] if nk > 1 else [],
        compiler_params=_params(("parallel", "parallel", "arbitrary")),
    )(*args)


def _rms_rows(v):
    return lax.rsqrt(jnp.mean(v * v, axis=-1, keepdims=True) + EPS)


def _rmsmod(x, g, ada, sc_col, sh_col, *, name):
    S, D = x.shape
    tm = _tile(S, 256)

    def body(x_ref, g_ref, sc_ref, sh_ref, h_ref):
        xv = x_ref[...]
        h = (xv * _rms_rows(xv) * g_ref[...]) * (1.0 + sc_ref[...]) + sh_ref[...]
        h_ref[...] = h.astype(h_ref.dtype)

    return pl.pallas_call(
        body, name=name, grid=(S // tm,),
        in_specs=[_rows(tm, D), _vec(D), _vec(D, sc_col), _vec(D, sh_col)],
        out_specs=_rows(tm, D), out_shape=jax.ShapeDtypeStruct((S, D), BF16),
        compiler_params=_params(("parallel",)),
    )(x, g, ada, ada)


def _latent_norm(proj, g_q, g_kv, ql):
    S = proj.shape[0]
    tm = _tile(S, 512)

    def body(cq_ref, ckv_ref, gq_ref, gkv_ref, oq_ref, okv_ref):
        cq = cq_ref[...]
        oq_ref[...] = (cq * _rms_rows(cq) * gq_ref[...]).astype(BF16)
        ckv = ckv_ref[...]
        okv_ref[...] = (ckv * _rms_rows(ckv) * gkv_ref[...]).astype(BF16)

    return pl.pallas_call(
        body, name="latent_norm", grid=(S // tm,),
        in_specs=[_rows(tm, ql, 0), _rows(tm, ql, 1), _vec(ql), _vec(ql)],
        out_specs=[_rows(tm, ql), _rows(tm, ql)],
        out_shape=[jax.ShapeDtypeStruct((S, ql), BF16)] * 2,
        compiler_params=_params(("parallel",)),
    )(proj, proj, g_q, g_kv)


def _rope_fwd(y, c, s1, s2):
    return y * c + pltpu.roll(y, ROPE // 2, 1) * s1 + pltpu.roll(y, HEAD_PAD - ROPE // 2, 1) * s2


def _rope_bwd(d, c, s1, s2):
    return d * c + pltpu.roll(d * s1, HEAD_PAD - ROPE // 2, 1) + pltpu.roll(d * s2, ROPE // 2, 1)


def _head_rms(v):
    return lax.rsqrt(jnp.sum(v * v, axis=-1, keepdims=True) * (1.0 / QK_DIM) + EPS)


def _q_prep(q0, g_qh, tabs, nh):
    S = q0.shape[0]
    tm = _tile(S, 256)

    def body(q_ref, g_ref, c_ref, s1_ref, s2_ref, o_ref):
        c, s1, s2, g = c_ref[...], s1_ref[...], s2_ref[...], g_ref[...]
        for h in range(nh):
            sl = slice(h * HEAD_PAD, (h + 1) * HEAD_PAD)
            xs = q_ref[:, sl]
            o_ref[:, sl] = (_rope_fwd(xs * _head_rms(xs) * g, c, s1, s2) * (QK_DIM ** -0.5)).astype(BF16)

    w = nh * HEAD_PAD
    return pl.pallas_call(
        body, name="mla_q_prep", grid=(S // tm,),
        in_specs=[_rows(tm, w), _vec(HEAD_PAD)] + [_rows(tm, HEAD_PAD)] * 3,
        out_specs=_rows(tm, w), out_shape=jax.ShapeDtypeStruct((S, w), BF16),
        compiler_params=_params(("parallel",)),
    )(q0, g_qh, *tabs)


def _k_prep(kv0, proj, kpe_col, g_kh, tabs, nh):
    S = kv0.shape[0]
    tm = _tile(S, 256)

    def body(kv_ref, kpe_ref, g_ref, c_ref, s1_ref, s2_ref, o_ref):
        c, s1, s2, g = c_ref[...], s1_ref[...], s2_ref[...], g_ref[...]
        kpe = kpe_ref[...]
        for h in range(nh):
            k0 = jnp.concatenate([kv_ref[:, h * HEAD:(h + 1) * HEAD], kpe], axis=1)
            o_ref[:, h * HEAD_PAD:(h + 1) * HEAD_PAD] = _rope_fwd(k0 * _head_rms(k0) * g, c, s1, s2).astype(BF16)

    return pl.pallas_call(
        body, name="mla_k_prep", grid=(S // tm,),
        in_specs=[_rows(tm, nh * HEAD, 0), _rows(tm, LANE, kpe_col), _vec(HEAD_PAD)] + [_rows(tm, HEAD_PAD)] * 3,
        out_specs=_rows(tm, nh * HEAD_PAD), out_shape=jax.ShapeDtypeStruct((S, nh * HEAD_PAD), BF16),
        compiler_params=_params(("parallel",)),
    )(kv0, proj, g_kh, *tabs)


def _gate_merge(pa, pb, proj, gla_col, glb_col):
    S, D = pa.shape
    tm = _tile(S, 256)

    def body(pa_ref, pb_ref, ga_ref, gb_ref, o_ref):
        o_ref[...] = (jax.nn.sigmoid(ga_ref[...]) * pa_ref[...] + jax.nn.sigmoid(gb_ref[...]) * pb_ref[...]).astype(BF16)

    return pl.pallas_call(
        body, name="gate_merge", grid=(S // tm,),
        in_specs=[_rows(tm, D), _rows(tm, D), _rows(tm, D, gla_col), _rows(tm, D, glb_col)],
        out_specs=_rows(tm, D), out_shape=jax.ShapeDtypeStruct((S, D), BF16),
        compiler_params=_params(("parallel",)),
    )(pa, pb, proj, proj)


def _resid_rmsmod(x, o, g, ada, gt_col, sc_col, sh_col):
    S, D = x.shape
    tm = _tile(S, 256)

    def body(x_ref, o_ref, g_ref, gt_ref, sc_ref, sh_ref, x2_ref, h_ref):
        x2 = x_ref[...] + gt_ref[...] * o_ref[...]
        x2_ref[...] = x2
        h_ref[...] = ((x2 * _rms_rows(x2) * g_ref[...]) * (1.0 + sc_ref[...]) + sh_ref[...]).astype(BF16)

    return pl.pallas_call(
        body, name="resid_rmsmod2", grid=(S // tm,),
        in_specs=[_rows(tm, D), _rows(tm, D), _vec(D), _vec(D, gt_col), _vec(D, sc_col), _vec(D, sh_col)],
        out_specs=[_rows(tm, D), _rows(tm, D)],
        out_shape=[jax.ShapeDtypeStruct((S, D), F32), jax.ShapeDtypeStruct((S, D), BF16)],
        compiler_params=_params(("parallel",)),
    )(x, o, g, ada, ada, ada)


def _swiglu(ff, dff_half):
    S = ff.shape[0]
    tm = _tile(S, 256)

    def body(g_ref, u_ref, o_ref):
        o_ref[...] = (jax.nn.silu(g_ref[...].astype(F32)) * u_ref[...].astype(F32)).astype(BF16)

    return pl.pallas_call(
        body, name="swiglu", grid=(S // tm,),
        in_specs=[_rows(tm, dff_half, 0), _rows(tm, dff_half, 1)],
        out_specs=_rows(tm, dff_half), out_shape=jax.ShapeDtypeStruct((S, dff_half), BF16),
        compiler_params=_params(("parallel",)),
    )(ff, ff)


def _loss_head(x2, f, tgt, ada, gt_col):
    S, D = x2.shape
    tm = _tile(S, 256)

    def body(x2_ref, f_ref, t_ref, gt_ref, dy_ref, df_ref, red_ref, loss_ref):
        @pl.when(pl.program_id(0) == 0)
        def _():
            red_ref[...] = jnp.zeros_like(red_ref)
            loss_ref[...] = jnp.zeros_like(loss_ref)

        fv = f_ref[...]
        gt = gt_ref[...]
        err = x2_ref[...] + gt * fv - t_ref[...]
        dy = err * (1.0 / D)
        dy_ref[...] = dy
        df_ref[...] = (dy * gt).astype(BF16)
        red_ref[0:1, :] += jnp.sum(dy * fv, axis=0, keepdims=True)
        loss_ref[...] += (0.5 / D) * jnp.sum(err * err)

    return pl.pallas_call(
        body, name="loss_head", grid=(S // tm,),
        in_specs=[_rows(tm, D), _rows(tm, D), _rows(tm, D), _vec(D, gt_col)],
        out_specs=[_rows(tm, D), _rows(tm, D), _vec(D, rows=8), _vec(LANE, rows=8)],
        out_shape=[jax.ShapeDtypeStruct((S, D), F32), jax.ShapeDtypeStruct((S, D), BF16),
                   jax.ShapeDtypeStruct((8, D), F32), jax.ShapeDtypeStruct((8, LANE), F32)],
        compiler_params=_params(("arbitrary",)),
    )(x2, f, tgt, ada)


def _swiglu_bwd(dact, ff, dff_half):
    S = ff.shape[0]
    tm = _tile(S, 128)

    def body(d_ref, g_ref, u_ref, o_ref):
        d = d_ref[...]
        g = g_ref[...].astype(F32)
        u = u_ref[...].astype(F32)
        sg = jax.nn.sigmoid(g)
        o_ref[:, :dff_half] = (d * u * sg * (1.0 + g * (1.0 - sg))).astype(BF16)
        o_ref[:, dff_half:] = (d * g * sg).astype(BF16)

    return pl.pallas_call(
        body, name="swiglu_bwd", grid=(S // tm,),
        in_specs=[_rows(tm, dff_half), _rows(tm, dff_half, 0), _rows(tm, dff_half, 1)],
        out_specs=_rows(tm, 2 * dff_half), out_shape=jax.ShapeDtypeStruct((S, 2 * dff_half), BF16),
        compiler_params=_params(("parallel",)),
    )(dact, ff, ff)


def _rmsmod2_bwd(dh2, x2, dy, o, g, ada, sc_col, gt_col):
    S, D = x2.shape
    tm = _tile(S, 256)

    def body(dh_ref, x2_ref, dy_ref, o_ref, g_ref, sc_ref, gt_ref, dx_ref, do_ref, red_ref):
        @pl.when(pl.program_id(0) == 0)
        def _():
            red_ref[...] = jnp.zeros_like(red_ref)

        dh = dh_ref[...]
        x2 = x2_ref[...]
        gv = g_ref[...]
        mod = 1.0 + sc_ref[...]
        r = _rms_rows(x2)
        xn = x2 * r
        t = dh * xn
        red_ref[0:1, :] += jnp.sum(dh, axis=0, keepdims=True)
        red_ref[1:2, :] += jnp.sum(t * gv, axis=0, keepdims=True)
        red_ref[2:3, :] += jnp.sum(t * mod, axis=0, keepdims=True)
        dxn = dh * gv * mod
        dx = dy_ref[...] + r * (dxn - xn * jnp.mean(dxn * xn, axis=-1, keepdims=True))
        dx_ref[...] = dx
        red_ref[3:4, :] += jnp.sum(dx * o_ref[...], axis=0, keepdims=True)
        do_ref[...] = (dx * gt_ref[...]).astype(BF16)

    return pl.pallas_call(
        body, name="rmsmod2_bwd", grid=(S // tm,),
        in_specs=[_rows(tm, D)] * 4 + [_vec(D), _vec(D, sc_col), _vec(D, gt_col)],
        out_specs=[_rows(tm, D), _rows(tm, D), _vec(D, rows=8)],
        out_shape=[jax.ShapeDtypeStruct((S, D), F32), jax.ShapeDtypeStruct((S, D), BF16),
                   jax.ShapeDtypeStruct((8, D), F32)],
        compiler_params=_params(("arbitrary",)),
    )(dh2, x2, dy, o, g, ada, ada)


def _rmsmod1_bwd(dh, x, dx2, g, ada, sc_col):
    S, D = x.shape
    tm = _tile(S, 256)

    def body(dh_ref, x_ref, dx2_ref, g_ref, sc_ref, gx_ref, red_ref):
        @pl.when(pl.program_id(0) == 0)
        def _():
            red_ref[...] = jnp.zeros_like(red_ref)

        dh = dh_ref[...]
        xv = x_ref[...]
        gv = g_ref[...]
        mod = 1.0 + sc_ref[...]
        r = _rms_rows(xv)
        xn = xv * r
        t = dh * xn
        red_ref[0:1, :] += jnp.sum(dh, axis=0, keepdims=True)
        red_ref[1:2, :] += jnp.sum(t * gv, axis=0, keepdims=True)
        red_ref[2:3, :] += jnp.sum(t * mod, axis=0, keepdims=True)
        dxn = dh * gv * mod
        gx_ref[...] = dx2_ref[...] + r * (dxn - xn * jnp.mean(dxn * xn, axis=-1, keepdims=True))

    return pl.pallas_call(
        body, name="rmsmod1_bwd", grid=(S // tm,),
        in_specs=[_rows(tm, D)] * 3 + [_vec(D), _vec(D, sc_col)],
        out_specs=[_rows(tm, D), _vec(D, rows=8)],
        out_shape=[jax.ShapeDtypeStruct((S, D), F32), jax.ShapeDtypeStruct((8, D), F32)],
        compiler_params=_params(("arbitrary",)),
    )(dh, x, dx2, g, ada)


def _gate_bwd(dm, pa, pb, proj, gla_col, glb_col):
    S, D = pa.shape
    tm = _tile(S, 256)

    def body(dm_ref, pa_ref, pb_ref, la_ref, lb_ref, dpa_ref, dpb_ref, dla_ref, dlb_ref):
        dm_ = dm_ref[...]
        ga = jax.nn.sigmoid(la_ref[...])
        gb = jax.nn.sigmoid(lb_ref[...])
        dpa_ref[...] = (dm_ * ga).astype(BF16)
        dpb_ref[...] = (dm_ * gb).astype(BF16)
        dla_ref[...] = (dm_ * pa_ref[...] * ga * (1.0 - ga)).astype(BF16)
        dlb_ref[...] = (dm_ * pb_ref[...] * gb * (1.0 - gb)).astype(BF16)

    return pl.pallas_call(
        body, name="gate_bwd", grid=(S // tm,),
        in_specs=[_rows(tm, D)] * 3 + [_rows(tm, D, gla_col), _rows(tm, D, glb_col)],
        out_specs=[_rows(tm, D)] * 4, out_shape=[jax.ShapeDtypeStruct((S, D), BF16)] * 4,
        compiler_params=_params(("parallel",)),
    )(dm, pa, pb, proj, proj)


def _q_prep_bwd(dq, q0, g_qh, tabs, nh):
    S = q0.shape[0]
    tm = _tile(S, 256)

    def body(dq_ref, q_ref, g_ref, c_ref, s1_ref, s2_ref, o_ref, red_ref):
        @pl.when(pl.program_id(0) == 0)
        def _():
            red_ref[...] = jnp.zeros_like(red_ref)

        c, s1, s2, g = c_ref[...], s1_ref[...], s2_ref[...], g_ref[...]
        dg = jnp.zeros((1, HEAD_PAD), F32)
        for h in range(nh):
            sl = slice(h * HEAD_PAD, (h + 1) * HEAD_PAD)
            d1 = _rope_bwd(dq_ref[:, sl], c, s1, s2)
            xs = q_ref[:, sl]
            r = _head_rms(xs)
            qn = xs * r
            dg = dg + jnp.sum(d1 * qn, axis=0, keepdims=True)
            dn = d1 * g
            o_ref[:, sl] = (r * (dn - qn * (jnp.sum(dn * qn, axis=-1, keepdims=True) * (1.0 / QK_DIM)))).astype(BF16)
        red_ref[0:1, :] += dg

    w = nh * HEAD_PAD
    return pl.pallas_call(
        body, name="mla_q_prep_bwd", grid=(S // tm,),
        in_specs=[_rows(tm, w), _rows(tm, w), _vec(HEAD_PAD)] + [_rows(tm, HEAD_PAD)] * 3,
        out_specs=[_rows(tm, w), _vec(HEAD_PAD, rows=8)],
        out_shape=[jax.ShapeDtypeStruct((S, w), BF16), jax.ShapeDtypeStruct((8, HEAD_PAD), F32)],
        compiler_params=_params(("arbitrary",)),
    )(dq, q0, g_qh, *tabs)


def _k_prep_bwd(dk, dv, kv0, proj, kpe_col, g_kh, tabs, nh):
    S = kv0.shape[0]
    tm = _tile(S, 256)
    wv = nh * HEAD

    def body(dk_ref, dv_ref, kv_ref, kpe_ref, g_ref, c_ref, s1_ref, s2_ref, o_ref, dpe_ref, red_ref):
        @pl.when(pl.program_id(0) == 0)
        def _():
            red_ref[...] = jnp.zeros_like(red_ref)

        c, s1, s2, g = c_ref[...], s1_ref[...], s2_ref[...], g_ref[...]
        kpe = kpe_ref[...]
        dg = jnp.zeros((1, HEAD_PAD), F32)
        dpe = jnp.zeros((tm, LANE), F32)
        for h in range(nh):
            d1 = _rope_bwd(dk_ref[:, h * HEAD_PAD:(h + 1) * HEAD_PAD], c, s1, s2)
            k0 = jnp.concatenate([kv_ref[:, h * HEAD:(h + 1) * HEAD], kpe], axis=1)
            r = _head_rms(k0)
            kn = k0 * r
            dg = dg + jnp.sum(d1 * kn, axis=0, keepdims=True)
            dn = d1 * g
            dk0 = r * (dn - kn * (jnp.sum(dn * kn, axis=-1, keepdims=True) * (1.0 / QK_DIM)))
            o_ref[:, h * HEAD:(h + 1) * HEAD] = dk0[:, :HEAD].astype(BF16)
            dpe = dpe + dk0[:, HEAD:]
        o_ref[:, wv:] = dv_ref[...].astype(BF16)
        dpe_ref[...] = dpe.astype(BF16)
        red_ref[0:1, :] += dg

    return pl.pallas_call(
        body, name="mla_k_prep_bwd", grid=(S // tm,),
        in_specs=[_rows(tm, nh * HEAD_PAD), _rows(tm, wv), _rows(tm, wv, 0), _rows(tm, LANE, kpe_col),
                  _vec(HEAD_PAD)] + [_rows(tm, HEAD_PAD)] * 3,
        out_specs=[_rows(tm, 2 * wv), _rows(tm, LANE), _vec(HEAD_PAD, rows=8)],
        out_shape=[jax.ShapeDtypeStruct((S, 2 * wv), BF16), jax.ShapeDtypeStruct((S, LANE), BF16),
                   jax.ShapeDtypeStruct((8, HEAD_PAD), F32)],
        compiler_params=_params(("arbitrary",)),
    )(dk, dv, kv0, proj, g_kh, *tabs)


def _latent_norm_bwd(dcqn, dckvn, proj, g_q, g_kv, ql):
    S = proj.shape[0]
    tm = _tile(S, 512)

    def body(dq_ref, dkv_ref, cq_ref, ckv_ref, gq_ref, gkv_ref, oq_ref, okv_ref, red_ref):
        @pl.when(pl.program_id(0) == 0)
        def _():
            red_ref[...] = jnp.zeros_like(red_ref)

        for row, (d_ref, c_ref, g_ref, o_ref) in enumerate(((dq_ref, cq_ref, gq_ref, oq_ref),
                                                            (dkv_ref, ckv_ref, gkv_ref, okv_ref))):
            d = d_ref[...]
            cv = c_ref[...]
            r = _rms_rows(cv)
            ch = cv * r
            red_ref[row:row + 1, :] += jnp.sum(d * ch, axis=0, keepdims=True)
            dn = d * g_ref[...]
            o_ref[...] = (r * (dn - ch * jnp.mean(dn * ch, axis=-1, keepdims=True))).astype(BF16)

    return pl.pallas_call(
        body, name="latent_norm_bwd", grid=(S // tm,),
        in_specs=[_rows(tm, ql), _rows(tm, ql), _rows(tm, ql, 0), _rows(tm, ql, 1), _vec(ql), _vec(ql)],
        out_specs=[_rows(tm, ql), _rows(tm, ql), _vec(ql, rows=8)],
        out_shape=[jax.ShapeDtypeStruct((S, ql), BF16)] * 2 + [jax.ShapeDtypeStruct((8, ql), F32)],
        compiler_params=_params(("arbitrary",)),
    )(dcqn, dckvn, proj, proj, g_q, g_kv)


NEG = -1e30
ATT_TILE = 512
SB_SUB = 128
_NT = (((1,), (1,)), ((), ()))
_TN = (((0,), (0,)), ((), ()))


def _dot(a, b, dn=(((1,), (0,)), ((), ()))):
    return lax.dot_general(a, b, dn, preferred_element_type=F32)


def _key_rows(kb, t):
    return pl.ds(pl.multiple_of(kb * t, t), t)


def _diag_mask(t, strict):
    r = lax.broadcasted_iota(jnp.int32, (t, t), 0)
    c = lax.broadcasted_iota(jnp.int32, (t, t), 1)
    return c < r if strict else c <= r


def _mla_fwd(q, k, kv0, nh):
    S = q.shape[0]
    t = _tile(S, ATT_TILE)

    def body(q_ref, k_ref, v_ref, o_ref, lse_ref):
        i = pl.program_id(1)
        qv = q_ref[...]

        def block(kb, carry, masked):
            m, l, acc = carry
            rows = _key_rows(kb, t)
            s = _dot(qv, k_ref[rows, :], _NT)
            if masked:
                s = jnp.where(_diag_mask(t, False), s, NEG)
            m_new = jnp.maximum(m, jnp.max(s, axis=-1, keepdims=True))
            alpha = jnp.exp(m - m_new)
            p = jnp.exp(s - m_new)
            l = alpha * l + jnp.sum(p, axis=-1, keepdims=True)
            acc = alpha * acc + _dot(p.astype(BF16), v_ref[rows, :].astype(BF16))
            return m_new, l, acc

        init = (jnp.full((t, 1), NEG, F32), jnp.zeros((t, 1), F32), jnp.zeros((t, HEAD), F32))
        carry = lax.fori_loop(0, i, lambda kb, c: block(kb, c, False), init)
        m, l, acc = block(i, carry, True)
        o_ref[...] = acc / l
        lse_ref[...] = m + jnp.log(l)

    return pl.pallas_call(
        body, name="mla_attn_fwd", grid=(nh, S // t),
        in_specs=[pl.BlockSpec((t, HEAD_PAD), lambda h, i: (i, h)),
                  pl.BlockSpec((S, HEAD_PAD), lambda h, i: (0, h)),
                  pl.BlockSpec((S, HEAD), lambda h, i: (0, nh + h))],
        out_specs=[pl.BlockSpec((t, HEAD), lambda h, i: (i, h)),
                   pl.BlockSpec((None, t, 1), lambda h, i: (h, i, 0))],
        out_shape=[jax.ShapeDtypeStruct((S, nh * HEAD), F32), jax.ShapeDtypeStruct((nh, S, 1), F32)],
        compiler_params=_params(("parallel", "arbitrary")),
    )(q, k, kv0)


def _mla_bwd(q, k, kv0, o, do, lse, nh):
    S = q.shape[0]
    t = _tile(S, ATT_TILE)
    scale = QK_DIM ** -0.5

    def body(q_ref, k_ref, v_ref, o_ref, do_ref, lse_ref, dq_ref, dk_ref, dv_ref):
        i = pl.program_id(1)

        @pl.when(i == 0)
        def _():
            dk_ref[...] = jnp.zeros_like(dk_ref)
            dv_ref[...] = jnp.zeros_like(dv_ref)

        qv = q_ref[...]
        dov = do_ref[...]
        delta = jnp.sum(dov * o_ref[...], axis=-1, keepdims=True)
        dob = dov.astype(BF16)
        lse = lse_ref[...]

        def block(kb, dq, masked):
            rows = _key_rows(kb, t)
            ks = k_ref[rows, :]
            vs = v_ref[rows, :].astype(BF16)
            p = jnp.exp(_dot(qv, ks, _NT) - lse)
            if masked:
                p = jnp.where(_diag_mask(t, False), p, 0.0)
            ds = (p * (_dot(dob, vs, _NT) - delta)).astype(BF16)
            dk_ref[rows, :] += _dot(ds, qv, _TN)
            dv_ref[rows, :] += _dot(p.astype(BF16), dob, _TN)
            return dq + _dot(ds, ks)

        dq = lax.fori_loop(0, i, lambda kb, c: block(kb, c, False), jnp.zeros((t, HEAD_PAD), F32))
        dq_ref[...] = block(i, dq, True) * scale

    return pl.pallas_call(
        body, name="mla_attn_bwd", grid=(nh, S // t),
        in_specs=[pl.BlockSpec((t, HEAD_PAD), lambda h, i: (i, h)),
                  pl.BlockSpec((S, HEAD_PAD), lambda h, i: (0, h)),
                  pl.BlockSpec((S, HEAD), lambda h, i: (0, nh + h)),
                  pl.BlockSpec((t, HEAD), lambda h, i: (i, h)),
                  pl.BlockSpec((t, HEAD), lambda h, i: (i, h)),
                  pl.BlockSpec((None, t, 1), lambda h, i: (h, i, 0))],
        out_specs=[pl.BlockSpec((t, HEAD_PAD), lambda h, i: (i, h)),
                   pl.BlockSpec((S, HEAD_PAD), lambda h, i: (0, h)),
                   pl.BlockSpec((S, HEAD), lambda h, i: (0, h))],
        out_shape=[jax.ShapeDtypeStruct((S, nh * HEAD_PAD), F32), jax.ShapeDtypeStruct((S, nh * HEAD_PAD), F32),
                   jax.ShapeDtypeStruct((S, nh * HEAD), F32)],
        compiler_params=_params(("parallel", "arbitrary")),
    )(q, k, kv0, o, do, lse)


def _tri(n, cmp):
    r = lax.broadcasted_iota(jnp.int32, (n, n), 0)
    c = lax.broadcasted_iota(jnp.int32, (n, n), 1)
    return jnp.where(cmp(r, c), 1.0, 0.0).astype(BF16)


def _sb_block(qv, ks, run, upper, t, masked):
    z = _dot(qv, ks, _NT)
    lb = jnp.minimum(z, 0.0) - jnp.log(1.0 + jnp.exp(-jnp.abs(z)))
    lom = lb - z
    mask = _diag_mask(t, True) if masked else None
    if masked:
        lom = jnp.where(mask, lom, 0.0)
    tails = []
    for sblk in reversed(range(t // SB_SUB)):
        part = lom[:, sblk * SB_SUB:(sblk + 1) * SB_SUB]
        tails.append(_dot(part.astype(BF16), upper) + run)
        run = run + jnp.sum(part, axis=-1, keepdims=True)
    a = jnp.exp(lb + jnp.concatenate(tails[::-1], axis=1))
    if masked:
        a = jnp.where(mask, a, 0.0)
    return a, lb, mask, run


def _sb_fwd(proj, q_col, k_col, v_col, nh):
    S = proj.shape[0]
    t = _tile(S, ATT_TILE)
    assert S // t <= LANE
    scale = HEAD ** -0.5

    def body(q_ref, k_ref, v_ref, o_ref, runs_ref):
        i = pl.program_id(1)
        qv = (q_ref[...] * scale).astype(BF16)
        upper = _tri(SB_SUB, lambda j, s: j > s)
        lane = lax.broadcasted_iota(jnp.int32, (t, LANE), 1)

        def block(kb, carry, masked):
            run, acc, runs = carry
            runs = jnp.where(lane == kb, run, runs)
            rows = _key_rows(kb, t)
            a, _, _, run = _sb_block(qv, k_ref[rows, :].astype(BF16), run, upper, t, masked)
            return run, acc + _dot(a.astype(BF16), v_ref[rows, :].astype(BF16)), runs

        carry = block(i, (jnp.zeros((t, 1), F32), jnp.zeros((t, HEAD), F32), jnp.zeros((t, LANE), F32)), True)
        _, o_ref[...], runs_ref[...] = lax.fori_loop(0, i, lambda j, c: block(i - 1 - j, c, False), carry)

    return pl.pallas_call(
        body, name="sb_attn_fwd", grid=(nh, S // t),
        in_specs=[pl.BlockSpec((t, HEAD), lambda h, i: (i, q_col + h)),
                  pl.BlockSpec((S, HEAD), lambda h, i: (0, k_col + h)),
                  pl.BlockSpec((S, HEAD), lambda h, i: (0, v_col + h))],
        out_specs=[pl.BlockSpec((t, HEAD), lambda h, i: (i, h)), pl.BlockSpec((None, t, LANE), lambda h, i: (h, i, 0))],
        out_shape=[jax.ShapeDtypeStruct((S, nh * HEAD), F32), jax.ShapeDtypeStruct((nh, S, LANE), F32)],
        compiler_params=_params(("parallel", "arbitrary")),
    )(proj, proj, proj)


def _sb_bwd(proj, q_col, k_col, v_col, dy, runs, nh):
    S = proj.shape[0]
    t = _tile(S, ATT_TILE)
    scale = HEAD ** -0.5

    def body(q_ref, k_ref, v_ref, dy_ref, runs_ref, dq_ref, dk_ref, dv_ref):
        i = pl.program_id(1)

        @pl.when(i == 0)
        def _():
            dk_ref[...] = jnp.zeros_like(dk_ref)
            dv_ref[...] = jnp.zeros_like(dv_ref)

        qv = (q_ref[...] * scale).astype(BF16)
        dyb = dy_ref[...].astype(BF16)
        runs_v = runs_ref[...]
        lane = lax.broadcasted_iota(jnp.int32, (t, LANE), 1)
        upper = _tri(SB_SUB, lambda j, s: j > s)
        before = _tri(SB_SUB, lambda s, j: s < j)

        def block(kb, carry, masked):
            prefix, dq = carry
            rows = _key_rows(kb, t)
            ks = k_ref[rows, :].astype(BF16)
            vs = v_ref[rows, :].astype(BF16)
            run = jnp.sum(jnp.where(lane == kb, runs_v, 0.0), axis=-1, keepdims=True)
            a, lb, mask, _ = _sb_block(qv, ks, run, upper, t, masked)
            dl = a * _dot(dyb, vs, _NT)
            lefts = []
            for sblk in range(t // SB_SUB):
                part = dl[:, sblk * SB_SUB:(sblk + 1) * SB_SUB]
                lefts.append(_dot(part.astype(BF16), before) + prefix)
                prefix = prefix + jnp.sum(part, axis=-1, keepdims=True)
            beta = jnp.exp(lb)
            dz = dl * (1.0 - beta) - beta * jnp.concatenate(lefts, axis=1)
            if masked:
                dz = jnp.where(mask, dz, 0.0)
            dz = dz.astype(BF16)
            dk_ref[rows, :] += _dot(dz, qv, _TN)
            dv_ref[rows, :] += _dot(a.astype(BF16), dyb, _TN)
            return prefix, dq + _dot(dz, ks)

        carry = lax.fori_loop(0, i, lambda kb, c: block(kb, c, False),
                              (jnp.zeros((t, 1), F32), jnp.zeros((t, HEAD), F32)))
        dq_ref[...] = block(i, carry, True)[1] * scale

    full = pl.BlockSpec((S, HEAD), lambda h, i: (0, h))
    tile = pl.BlockSpec((t, HEAD), lambda h, i: (i, h))
    return pl.pallas_call(
        body, name="sb_attn_bwd", grid=(nh, S // t),
        in_specs=[pl.BlockSpec((t, HEAD), lambda h, i: (i, q_col + h)),
                  pl.BlockSpec((S, HEAD), lambda h, i: (0, k_col + h)),
                  pl.BlockSpec((S, HEAD), lambda h, i: (0, v_col + h)), tile,
                  pl.BlockSpec((None, t, LANE), lambda h, i: (h, i, 0))],
        out_specs=[tile, full, full],
        out_shape=[jax.ShapeDtypeStruct((S, nh * HEAD), F32)] * 3,
        compiler_params=_params(("parallel", "arbitrary")),
    )(proj, proj, proj, dy, runs)


def _place():
    return lax.axis_index("x"), lax.axis_index("y"), lax.axis_index("c")


def _other_chips(x, y):
    return [(1 - x, y), (x, 1 - y), (1 - x, 1 - y)]


def _dev_index(p):
    return 4 * p[0] + 2 * p[1] + p[2]


def _gather_blocks(blocks, *, name, in_vmem):
    n = len(blocks)
    per = 7

    def body(*refs):
        ins, outs = refs[:n], refs[n:2 * n]
        send_sems, recv_sems, local_sems = refs[2 * n:]
        x, y, c = _place()
        me, sibling = (x, y, c), (x, y, 1 - c)
        chips = _other_chips(x, y)

        def slot(a, p):
            return outs[a].at[_dev_index(p)]

        def copy(a, k, block, to, src=None):
            return pltpu.make_async_remote_copy(
                src_ref=slot(a, block) if src is None else src, dst_ref=slot(a, block),
                send_sem=send_sems.at[a * per + k], recv_sem=recv_sems.at[a * per + k],
                device_id=to, device_id_type=MESH)

        mine = [pltpu.make_async_copy(ins[a], slot(a, me), local_sems.at[a]) for a in range(n)] if in_vmem else []
        for cp in mine:
            cp.start()
        first = []
        for a in range(n):
            first.append(copy(a, 0, me, sibling, src=ins[a]))
            first += [copy(a, 1 + j, me, (*chip, c), src=ins[a]) for j, chip in enumerate(chips)]
        for cp in first:
            cp.start()
        passed = []
        for a in range(n):
            for j, chip in enumerate(chips):
                copy(a, 1 + j, (*chip, c), me).wait_recv()
                cp = copy(a, 4 + j, (*chip, c), sibling)
                cp.start()
                passed.append(cp)
        for a in range(n):
            copy(a, 0, sibling, me).wait_recv()
            for j, chip in enumerate(chips):
                copy(a, 4 + j, (*chip, 1 - c), me).wait_recv()
        for cp in first + passed:
            cp.wait_send()
        for cp in mine:
            cp.wait()

    space = pltpu.VMEM if in_vmem else pl.ANY
    spec = pl.BlockSpec(memory_space=space)
    outs = pl.pallas_call(
        body, name=name, in_specs=[spec] * n, out_specs=[spec] * n,
        out_shape=[jax.ShapeDtypeStruct((N_DEV,) + b.shape, b.dtype) for b in blocks],
        scratch_shapes=[pltpu.SemaphoreType.DMA((n * per,)), pltpu.SemaphoreType.DMA((n * per,)),
                        pltpu.SemaphoreType.DMA((n,))],
        compiler_params=pltpu.CompilerParams(vmem_limit_bytes=VMEM_LIMIT),
    )(*blocks)
    return list(outs)


def _sibling_swap(arrs, *, name, whole=False):
    n = len(arrs)

    def body(*refs):
        ins, outs = refs[:n], refs[n:2 * n]
        send_sems, recv_sems = refs[2 * n:]
        x, y, c = _place()
        copies = [pltpu.make_async_remote_copy(
            src_ref=ins[a] if whole else ins[a].at[1 - c], dst_ref=outs[a],
            send_sem=send_sems.at[a], recv_sem=recv_sems.at[a],
            device_id=(x, y, 1 - c), device_id_type=MESH) for a in range(n)]
        for cp in copies:
            cp.start()
        for cp in copies:
            cp.wait()

    spec = pl.BlockSpec(memory_space=pl.ANY)
    return list(pl.pallas_call(
        body, name=name, in_specs=[spec] * n, out_specs=[spec] * n,
        out_shape=[jax.ShapeDtypeStruct(a.shape if whole else a.shape[1:], a.dtype) for a in arrs],
        scratch_shapes=[pltpu.SemaphoreType.DMA((n,)), pltpu.SemaphoreType.DMA((n,))],
    )(*arrs))


_HBM = pl.BlockSpec(memory_space=pltpu.HBM)
_SEM = pl.BlockSpec(memory_space=pltpu.SEMAPHORE)
_EFFECT = pltpu.SideEffectType.DATAFLOW_SIDE_EFFECTING


def _in_hbm(a):
    return pltpu.with_memory_space_constraint(a, pltpu.HBM)


def _split_copies(srcs, lands, send_sems, recv_sems, plan):
    x, y, c = _place()
    copies = []
    for a, (src, land) in enumerate(zip(srcs, lands)):
        steps = plan(x, y, c)
        for k, (pick, slot, to) in enumerate(steps):
            copies.append(pltpu.make_async_remote_copy(
                src_ref=pick(src), dst_ref=slot(land), send_sem=send_sems.at[a * len(steps) + k],
                recv_sem=recv_sems.at[a * len(steps) + k], device_id=to, device_id_type=MESH))
    return copies


def _split_start(srcs, land_shapes, plan, per, *, name):
    n = len(srcs)

    def body(*refs):
        send_sems, recv_sems = refs[2 * n], refs[2 * n + 1]
        for cp in _split_copies(refs[:n], refs[n:2 * n], send_sems, recv_sems, plan):
            cp.start()
        token = refs[-1]
        token[...] = jnp.zeros_like(token)

    lands = [_in_hbm(lax.empty(s.shape, s.dtype)) for s in land_shapes]
    outs = pl.pallas_call(
        body, name=name,
        out_shape=(pltpu.SemaphoreType.DMA((n * per,)), pltpu.SemaphoreType.DMA((n * per,)),
                   *[pltpu.HBM(s.shape, s.dtype) for s in srcs], *[pltpu.HBM(s.shape, s.dtype) for s in land_shapes],
                   jax.ShapeDtypeStruct((8, LANE), F32)),
        in_specs=[_HBM] * (2 * n),
        out_specs=(_SEM, _SEM, *[_HBM] * (2 * n), pl.BlockSpec(memory_space=pltpu.VMEM)),
        input_output_aliases={i: 2 + i for i in range(2 * n)},
        compiler_params=pltpu.CompilerParams(has_side_effects=_EFFECT),
    )(*[_in_hbm(s) for s in srcs], *lands)
    return outs[0], outs[1], list(outs[2:2 + n]), list(outs[2 + n:2 + 2 * n]), outs[-1]


def _split_wait(send_sems, recv_sems, srcs, lands, after, plan, *, name):
    n = len(srcs)

    def body(*refs):
        for cp in _split_copies(refs[:n], refs[n:2 * n], refs[2 * n], refs[2 * n + 1], plan):
            cp.wait_send()
            cp.wait_recv()

    outs = pl.pallas_call(
        body, name=name,
        out_shape=(*[pltpu.HBM(s.shape, s.dtype) for s in srcs], *[pltpu.HBM(s.shape, s.dtype) for s in lands]),
        in_specs=[_HBM] * (2 * n) + [_SEM, _SEM, pl.BlockSpec(memory_space=pl.ANY)],
        out_specs=tuple([_HBM] * (2 * n)),
        input_output_aliases={i: i for i in range(2 * n)},
        compiler_params=pltpu.CompilerParams(has_side_effects=_EFFECT),
    )(*srcs, *lands, send_sems, recv_sems, after)
    return list(outs[:n]), list(outs[n:])


def _gather_plan(x, y, c):
    slot = lambda land: land.at[_dev_index((x, y, c))]
    whole = lambda src: src
    return [(whole, slot, (x, y, 1 - c))] + [(whole, slot, (px, py, c)) for px, py in _other_chips(x, y)]


def _exchange_plan(x, y, c):
    return [(lambda src, k=2 * px + py: src.at[k], lambda land, j=j: land.at[j], (px, py, c))
            for j, (px, py) in enumerate(_other_chips(x, y))]


def _gather_forward(lands, *, name):
    n = len(lands)

    def body(*refs):
        lands_in, outs = refs[:n], refs[n:2 * n]
        send_sems, recv_sems = refs[2 * n:]
        x, y, c = _place()
        copies = []
        for a in range(n):
            for j, (px, py) in enumerate(_other_chips(x, y)):
                copies.append((pltpu.make_async_remote_copy(
                    src_ref=lands_in[a].at[_dev_index((px, py, c))], dst_ref=outs[a].at[_dev_index((px, py, c))],
                    send_sem=send_sems.at[3 * a + j], recv_sem=recv_sems.at[3 * a + j],
                    device_id=(x, y, 1 - c), device_id_type=MESH), a, j, (px, py)))
        for cp, _, _, _ in copies:
            cp.start()
        for cp, a, j, (px, py) in copies:
            cp.wait_send()
            pltpu.make_async_remote_copy(
                src_ref=lands_in[a].at[_dev_index((px, py, 1 - c))], dst_ref=outs[a].at[_dev_index((px, py, 1 - c))],
                send_sem=send_sems.at[3 * a + j], recv_sem=recv_sems.at[3 * a + j],
                device_id=(x, y, 1 - c), device_id_type=MESH).wait_recv()

    spec = pl.BlockSpec(memory_space=pl.ANY)
    return list(pl.pallas_call(
        body, name=name, in_specs=[spec] * n, out_specs=[spec] * n,
        out_shape=[jax.ShapeDtypeStruct(a.shape, a.dtype) for a in lands],
        input_output_aliases={a: a for a in range(n)},
        scratch_shapes=[pltpu.SemaphoreType.DMA((3 * n,)), pltpu.SemaphoreType.DMA((3 * n,))],
    )(*lands))


def _flat2(a, lead):
    return a.reshape(a.shape[:lead] + (-1, a.shape[-1]))


def _pair_sum(g, recv, c_idx, *, name):
    _, nchip, r, w = g.shape
    tm = _tile(r, 256) if r % 8 == 0 else r

    def body(c_ref, g_ref, r_ref, o_ref):
        o_ref[...] = (g_ref[...].astype(F32) + r_ref[...].astype(F32)).astype(o_ref.dtype)

    return pl.pallas_call(
        body, name=name,
        grid_spec=pltpu.PrefetchScalarGridSpec(
            num_scalar_prefetch=1, grid=(nchip, r // tm),
            in_specs=[pl.BlockSpec((None, None, tm, w), lambda k, i, c_ref: (c_ref[0], k, i, 0)),
                      pl.BlockSpec((None, tm, w), lambda k, i, c_ref: (k, i, 0))],
            out_specs=pl.BlockSpec((None, tm, w), lambda k, i, c_ref: (k, i, 0))),
        out_shape=jax.ShapeDtypeStruct((nchip, r, w), BF16),
        compiler_params=_params(("parallel", "parallel")),
    )(c_idx, g, recv)


def _chip_sum(s1, recv, chip_idx, *, name):
    _, r, w = s1.shape
    tm = _tile(r, 256) if r % 8 == 0 else r

    def body(k_ref, s_ref, r_ref, o_ref):
        acc = s_ref[...].astype(F32)
        for j in range(3):
            acc = acc + r_ref[j].astype(F32)
        o_ref[...] = acc

    return pl.pallas_call(
        body, name=name,
        grid_spec=pltpu.PrefetchScalarGridSpec(
            num_scalar_prefetch=1, grid=(r // tm,),
            in_specs=[pl.BlockSpec((None, tm, w), lambda i, k_ref: (k_ref[0], i, 0)),
                      pl.BlockSpec((3, tm, w), lambda i, k_ref: (0, i, 0))],
            out_specs=pl.BlockSpec((tm, w), lambda i, k_ref: (i, 0))),
        out_shape=jax.ShapeDtypeStruct((r, w), F32),
        compiler_params=_params(("parallel",)),
    )(chip_idx, s1, recv)


def _adam_math(w, g, m, v):
    m = ADAM_B1 * m + (1.0 - ADAM_B1) * g
    v = ADAM_B2 * v + (1.0 - ADAM_B2) * (g * g)
    m_hat = m / (1.0 - ADAM_B1 ** ADAM_STEP)
    v_hat = v / (1.0 - ADAM_B2 ** ADAM_STEP)
    delta = -ADAM_LR * (m_hat / (jnp.sqrt(v_hat) + ADAM_EPS) + ADAM_WD * w)
    return delta, m, v


def _adamw(w, mine, other, c_idx, m, v, *, name):
    r, cw = w.shape
    hr = r // 2
    tm = _row_tile(hr, 9 * cw * 4)

    def body(c_ref, w_ref, a_ref, b_ref, m_ref, v_ref, g_ref, d_ref, nm_ref, nv_ref):
        g = jnp.where(pl.program_id(0) == c_ref[0], a_ref[...], b_ref[...])
        g_ref[...] = g
        d_ref[...], nm_ref[...], nv_ref[...] = _adam_math(w_ref[...], g, m_ref[...], v_ref[...])

    full = pl.BlockSpec((None, tm, cw), lambda h, i, c_ref: (h, i, 0))
    half = pl.BlockSpec((tm, cw), lambda h, i, c_ref: (i, 0))
    outs = pl.pallas_call(
        body, name=name,
        grid_spec=pltpu.PrefetchScalarGridSpec(
            num_scalar_prefetch=1, grid=(2, hr // tm),
            in_specs=[full, half, half, full, full], out_specs=[full] * 4),
        out_shape=[jax.ShapeDtypeStruct((2, hr, cw), F32)] * 4,
        compiler_params=_params(("parallel", "parallel")),
    )(c_idx, w.reshape(2, hr, cw), mine, other, m.reshape(2, hr, cw), v.reshape(2, hr, cw))
    return [o.reshape(r, cw) for o in outs]


def _adamw_ada(cact_t, dada, w, m, v):
    r, cw = w.shape
    nb = cact_t.shape[1]
    tm = _tile(r, 256)
    tn = _tile(cw, 1024)

    def body(a_ref, d_ref, w_ref, m_ref, v_ref, g_ref, dl_ref, nm_ref, nv_ref):
        a = a_ref[...]
        d = d_ref[...]
        g = a[:, 0:1] * d[0:1, :]
        for b in range(1, nb):
            g = g + a[:, b:b + 1] * d[b:b + 1, :]
        g_ref[...] = g
        dl_ref[...], nm_ref[...], nv_ref[...] = _adam_math(w_ref[...], g, m_ref[...], v_ref[...])

    blk = pl.BlockSpec((tm, tn), lambda i, j: (i, j))
    return pl.pallas_call(
        body, name="adamw_ada", grid=(r // tm, cw // tn),
        in_specs=[pl.BlockSpec((tm, nb), lambda i, j: (i, 0)), pl.BlockSpec((nb, tn), lambda i, j: (0, j)), blk, blk, blk],
        out_specs=[blk] * 4, out_shape=[jax.ShapeDtypeStruct((r, cw), F32)] * 4,
        compiler_params=_params(("parallel", "parallel")),
    )(cact_t, dada, w, m, v)


def _adamw_vec(parts, w, m, v):
    n = w.shape[1]

    def body(p_ref, w_ref, m_ref, v_ref, g_ref, d_ref, nm_ref, nv_ref):
        p = p_ref[...]
        g = p[0:1, :]
        for b in range(1, N_DEV):
            g = g + p[b:b + 1, :]
        g_ref[...] = g
        d_ref[...], nm_ref[...], nv_ref[...] = _adam_math(w_ref[...], g, m_ref[...], v_ref[...])

    return pl.pallas_call(
        body, name="adamw_vec", out_shape=[jax.ShapeDtypeStruct((1, n), F32)] * 4,
        compiler_params=pltpu.CompilerParams(vmem_limit_bytes=VMEM_LIMIT),
    )(parts, w, m, v)


def _w_in_segments(kpe0, d_in, cs):
    segs = []
    for k in range(4):
        lo, hi = k * cs, (k + 1) * cs
        for a, b, shift in ((0, kpe0, 0), (kpe0, kpe0 + ROPE, d_in - ROPE - kpe0), (kpe0 + ROPE, d_in, -ROPE)):
            a, b = max(lo, a), min(hi, b)
            if a < b:
                segs.append((k, a - lo, a + shift, b - a))
    return segs


def _w_in_layout(g8, kpe0):
    _, hr, cs = g8.shape
    rows, d_in = 2 * hr, 4 * cs
    segs = _w_in_segments(kpe0, d_in, cs)
    tm = _tile(rows, 256)

    def body(g_ref, o_ref):
        for k, src, dst, w in segs:
            o_ref[:, dst:dst + w] = g_ref[k, :, src:src + w]
        o_ref[:, d_in:] = jnp.zeros((tm, ROPE), o_ref.dtype)

    return pl.pallas_call(
        body, name="w_in_layout", grid=(rows // tm,),
        in_specs=[pl.BlockSpec((4, tm, cs), lambda i: (0, i, 0))], out_specs=_rows(tm, d_in + ROPE),
        out_shape=jax.ShapeDtypeStruct((rows, d_in + ROPE), g8.dtype), compiler_params=_params(("parallel",)),
    )(g8.reshape(4, rows, cs))


def _w_in_grad_pieces(g, kpe0):
    rows, d_in_p = g.shape
    d_in = d_in_p - ROPE
    cs = d_in // 4
    segs = _w_in_segments(kpe0, d_in, cs)
    hr = rows // 2
    tm = _tile(hr, 256)
    per_half = hr // tm

    def body(g_ref, o_ref):
        for k, src, dst, w in segs:
            o_ref[k, :, src:src + w] = g_ref[:, dst:dst + w]

    return pl.pallas_call(
        body, name="w_in_grad_pieces", grid=(rows // tm,),
        in_specs=[_rows(tm, d_in_p)],
        out_specs=pl.BlockSpec((None, 4, tm, cs), lambda i: (i // per_half, 0, i % per_half, 0)),
        out_shape=jax.ShapeDtypeStruct((2, 4, hr, cs), g.dtype), compiler_params=_params(("parallel",)),
    )(g)


def _cols_from_chips(g8, rows):
    cs = g8.shape[-1]
    return g8.reshape(4, rows, cs).transpose(1, 0, 2).reshape(rows, 4 * cs)


def _cols_to_pieces(g):
    rows, c4 = g.shape
    return g.reshape(2, rows // 2, 4, c4 // 4).transpose(0, 2, 1, 3)


def _rows_to_pieces(g):
    r4, cols = g.shape
    return g.reshape(4, 2, r4 // 8, cols).transpose(1, 0, 2, 3)


def _pad_cols(a, w):
    return jnp.pad(a, ((0, 0), (0, w - a.shape[1])))


def kernel(x, c, positions, w_ada, b_ada, g_norm1, g_norm2, w_in, g_q_latent, g_kv_latent, w_uq, w_ukv, g_q_head, g_k_head, w_proj_mla, w_proj_sb, w_out, w_ffn_in, w_ffn_out, loss_target, m_w_ada, m_b_ada, m_g_norm1, m_g_norm2, m_w_in, m_g_q_latent, m_g_kv_latent, m_w_uq, m_w_ukv, m_g_q_head, m_g_k_head, m_w_proj_mla, m_w_proj_sb, m_w_out, m_w_ffn_in, m_w_ffn_out, v_w_ada, v_b_ada, v_g_norm1, v_g_norm2, v_w_in, v_g_q_latent, v_g_kv_latent, v_w_uq, v_w_ukv, v_g_q_head, v_g_k_head, v_w_proj_mla, v_w_proj_sb, v_w_out, v_w_ffn_in, v_w_ffn_out):
    xi, yi, ci = _place()
    chip = 2 * xi + yi
    dev = 2 * chip + ci
    c_idx = jnp.reshape(ci, (1,)).astype(jnp.int32)
    chip_idx = jnp.reshape(chip, (1,)).astype(jnp.int32)

    x = x[0]
    tgt = loss_target[0]
    S, D = x.shape
    ql = g_q_latent.shape[1]
    assert g_kv_latent.shape[1] == ql
    mlaw = w_proj_mla.shape[1]
    nh = mlaw // HEAD
    sbw = w_proj_sb.shape[1]
    assert sbw == mlaw
    dff = w_ffn_out.shape[1] * 4
    d_in = 2 * ql + ROPE + 3 * sbw + 2 * D
    d_in_p = d_in + ROPE
    q_col = (2 * ql) // HEAD
    k_col = q_col + nh
    v_col = k_col + nh
    gla_col = (2 * ql + 3 * sbw) // D
    glb_col = gla_col + 1
    kpe_col = (d_in - ROPE) // LANE
    assert (2 * ql + 3 * sbw) % D == 0 and (d_in - ROPE) % LANE == 0

    mats = {"w_in": w_in[0], "w_uq": w_uq[0], "w_ukv": w_ukv[0], "w_proj_mla": w_proj_mla[0],
            "w_proj_sb": w_proj_sb[0], "w_out": w_out[0], "w_ffn_in": w_ffn_in[0], "w_ffn_out": w_ffn_out[0]}
    names = list(mats)
    row_sharded = {"w_out", "w_ffn_out"}

    c_all = _gather_blocks([jnp.broadcast_to(c, (8, D))], name="gather_cond", in_vmem=True)[0][:, 0, :]
    n_ada = w_ada.shape[2]
    b_shard = lax.dynamic_slice_in_dim(b_ada, chip * n_ada, n_ada, axis=1)
    ada_shard = _mm(c_all, w_ada[0], name="ada_proj", a_fn=jax.nn.silu, bias=b_shard)
    ada_all = _gather_blocks([ada_shard], name="gather_ada", in_vmem=True)[0]
    ada_rows = lax.dynamic_index_in_dim(ada_all, dev, axis=1, keepdims=False)
    ada = ada_rows[0::2].reshape(1, 4 * n_ada)
    SH1, SC1, GT1, SH2, SC2, GT2 = range(6)

    def after(dep, a):
        return a + (dep.reshape(-1)[0:1].reshape((1,) * a.ndim) * 0).astype(a.dtype)

    def fill_own(g8, own):
        return lax.dynamic_update_index_in_dim(g8, own, dev, 0)

    halves = []
    for nm in names:
        w = mats[nm]
        hr = w.shape[0] // 2
        halves.append(lax.dynamic_slice_in_dim(w, ci * hr, hr, axis=0).astype(BF16))
    half_of = dict(zip(names, halves))
    early = ["w_in", "w_uq", "w_ukv"]
    late = ["w_proj_mla", "w_proj_sb", "w_out", "w_ffn_in", "w_ffn_out"]
    early_halves = [half_of[nm] for nm in early]
    early_halves[0] = after(ada, early_halves[0])
    early_got = _gather_blocks(early_halves, name="gather_weights", in_vmem=False)
    gathered = {nm: fill_own(g8, own) for nm, g8, own in zip(early, early_got, early_halves)}
    late_halves = [half_of[nm] for nm in late]
    late_halves[0] = after(gathered[early[1]], late_halves[0])
    late_send, late_recv, late_srcs, late_lands, late_token = _split_start(
        late_halves, [jax.ShapeDtypeStruct((N_DEV,) + h.shape, h.dtype) for h in late_halves], _gather_plan, 4,
        name="gather_late_start")
    ada = ada + late_token[0:1, 0:1]

    def full_cols(nm):
        return _cols_from_chips(gathered[nm], mats[nm].shape[0])

    kpe0 = 2 * ql
    w_in_p = _w_in_layout(gathered["w_in"], kpe0)
    w_uq_p = jnp.pad(full_cols("w_uq").reshape(ql, nh, QK_DIM), ((0, 0), (0, 0), (0, HEAD_PAD - QK_DIM))
                     ).reshape(ql, nh * HEAD_PAD)
    w_ukv4 = full_cols("w_ukv").reshape(ql, nh, 2 * HEAD)
    w_ukv_p = jnp.concatenate([w_ukv4[:, :, :HEAD].reshape(ql, mlaw), w_ukv4[:, :, HEAD:].reshape(ql, mlaw)], axis=1)

    half = ROPE // 2
    freqs = ROPE_THETA ** (-jnp.arange(half, dtype=F32) / half)
    ang = positions[0].astype(F32)[:, None] * freqs
    cos, sin = jnp.cos(ang), jnp.sin(ang)
    one = jnp.ones((S, NOPE), F32)
    zero = jnp.zeros((S, NOPE), F32)
    zh = jnp.zeros((S, half), F32)
    tabs = (jnp.concatenate([one, cos, cos, one[:, :HEAD_PAD - QK_DIM]], axis=1),
            jnp.concatenate([zero, zh, sin, zero[:, :HEAD_PAD - QK_DIM]], axis=1),
            jnp.concatenate([zero, -sin, zh, zero[:, :HEAD_PAD - QK_DIM]], axis=1))
    g_qh_p = _pad_cols(g_q_head, HEAD_PAD)
    g_kh_p = _pad_cols(g_k_head, HEAD_PAD)

    h1 = _rmsmod(x, g_norm1, ada, SC1, SH1, name="rmsmod1")
    proj = _mm(h1, w_in_p, name="mm_proj", tn=640)
    cqn, ckvn = _latent_norm(proj, g_q_latent, g_kv_latent, ql)
    q0 = _mm(cqn, w_uq_p, name="mm_q_up")
    kv0 = _mm(ckvn, w_ukv_p, name="mm_kv_up")
    q = _q_prep(q0, g_qh_p, tabs, nh)
    k = _k_prep(kv0, proj, kpe_col, g_kh_p, tabs, nh)
    y_a, lse = _mla_fwd(q, k, kv0, nh)
    y_b, sb_runs = _sb_fwd(proj, q_col, k_col, v_col, nh)
    late_srcs, late_lands = _split_wait(late_send, late_recv, late_srcs, late_lands, y_b, _gather_plan,
                                        name="gather_late_wait")
    late_got = _gather_forward(late_lands, name="gather_late_forward")
    gathered.update({nm: fill_own(g8, own) for nm, g8, own in zip(late, late_got, late_srcs)})
    w_pm = full_cols("w_proj_mla")
    w_ps = full_cols("w_proj_sb")
    w_o = gathered["w_out"].reshape(D, D)
    w_fi = full_cols("w_ffn_in")
    w_fo = gathered["w_ffn_out"].reshape(dff, D)
    pa = _mm(y_a, w_pm, name="mm_proj_mla")
    pb = _mm(y_b, w_ps, name="mm_proj_sb")
    merged = _gate_merge(pa, pb, proj, gla_col, glb_col)
    o = _mm(merged, w_o, name="mm_out")
    x2, h2 = _resid_rmsmod(x, o, g_norm2, ada, GT1, SC2, SH2)
    ff = _mm(h2, w_fi, name="mm_ffn_in", out_dtype=BF16)
    act = _swiglu(ff, dff)
    f = _mm(act, w_fo, name="mm_ffn_out")
    dy, df, red_l, loss_p = _loss_head(x2, f, tgt, ada, GT2)

    dact = _mm(df, w_fo, name="mm_d_act", tb=True)
    def pc(kind):
        if kind == "cols":
            return kind
        return kind if (D // 4) % LANE == 0 and (dff // 4) % LANE == 0 else None

    gw_fo = _mm(act, df, name="mm_gw_ffn_out", ta=True, out_dtype=BF16, pieces=pc("rows"))
    dff_ = _swiglu_bwd(dact, ff, dff)
    dh2 = _mm(dff_, w_fi, name="mm_d_h2", tb=True)
    gw_fi = _mm(h2, dff_, name="mm_gw_ffn_in", ta=True, out_dtype=BF16, pieces=pc("cols"))

    def pair_sums(nms, grads, tag):
        pcs = [g if g.ndim == 4 else (_rows_to_pieces if nm in row_sharded else _cols_to_pieces)(g)
               for nm, g in zip(nms, grads)]
        got = _sibling_swap(pcs, name="rs_sibling_swap_" + tag)
        return [_pair_sum(p, r, c_idx, name="rs_pair_sum_" + nm) for p, r, nm in zip(pcs, got, nms)]

    ffn = ["w_ffn_in", "w_ffn_out"]
    ffn_pair = pair_sums(ffn, [gw_fi, gw_fo], "ffn")
    ffn_send, ffn_recv, ffn_pair, ffn_lands, ffn_token = _split_start(
        ffn_pair, [jax.ShapeDtypeStruct((3,) + p.shape[1:], p.dtype) for p in ffn_pair], _exchange_plan, 3,
        name="rs_exchange_ffn_start")
    ada = ada + ffn_token[0:1, 0:1]
    dx2, do, red_2 = _rmsmod2_bwd(dh2, x2, dy, o, g_norm2, ada, SC2, GT1)
    dmerged = _mm(do, w_o, name="mm_d_merged", tb=True)
    gw_o = _mm(merged, do, name="mm_gw_out", ta=True, out_dtype=BF16, pieces=pc("rows"))
    dpa, dpb, dgla, dglb = _gate_bwd(dmerged, pa, pb, proj, gla_col, glb_col)
    dya = _mm(dpa, w_pm, name="mm_d_ya", tb=True)
    gw_pm = _mm(y_a, dpa, name="mm_gw_proj_mla", ta=True, out_dtype=BF16, pieces=pc("cols"))
    dyb = _mm(dpb, w_ps, name="mm_d_yb", tb=True)
    gw_ps = _mm(y_b, dpb, name="mm_gw_proj_sb", ta=True, out_dtype=BF16, pieces=pc("cols"))
    mid = ["w_proj_mla", "w_proj_sb", "w_out"]
    mid_pair = pair_sums(mid, [gw_pm, gw_ps, gw_o], "mid")
    mid_send, mid_recv, mid_pair, mid_lands, mid_token = _split_start(
        mid_pair, [jax.ShapeDtypeStruct((3,) + p.shape[1:], p.dtype) for p in mid_pair], _exchange_plan, 3,
        name="rs_exchange_mid_start")
    lse = lse + mid_token[0:1, 0:1]
    dq, dk, dv = _mla_bwd(q, k, kv0, y_a, dya, lse, nh)
    dq_sb, dk_sb, dv_sb = _sb_bwd(proj, q_col, k_col, v_col, dyb, sb_runs, nh)
    dq0, red_qh = _q_prep_bwd(dq, q0, g_qh_p, tabs, nh)
    dkv0, dkpe, red_kh = _k_prep_bwd(dk, dv, kv0, proj, kpe_col, g_kh_p, tabs, nh)
    dcqn = _mm(dq0, w_uq_p, name="mm_d_cqn", tb=True)
    gw_uq_p = _mm(cqn, dq0, name="mm_gw_uq", ta=True, out_dtype=BF16)
    dckvn = _mm(dkv0, w_ukv_p, name="mm_d_ckvn", tb=True)
    gw_ukv_p = _mm(ckvn, dkv0, name="mm_gw_ukv", ta=True, out_dtype=BF16)
    dcq, dckv, red_lat = _latent_norm_bwd(dcqn, dckvn, proj, g_q_latent, g_kv_latent, ql)
    dproj = jnp.concatenate([dcq, dckv, dq_sb.astype(BF16), dk_sb.astype(BF16), dv_sb.astype(BF16),
                             dgla, dglb, dkpe], axis=1)
    gw_in_p = _mm(h1, dproj, name="mm_gw_in", ta=True, out_dtype=BF16, tn=640)

    gw_in = _w_in_grad_pieces(gw_in_p, kpe0)
    gw_uq = gw_uq_p.reshape(ql, nh, HEAD_PAD)[:, :, :QK_DIM].reshape(ql, nh * QK_DIM)
    gw_ukv = jnp.concatenate([gw_ukv_p[:, :mlaw].reshape(ql, nh, HEAD), gw_ukv_p[:, mlaw:].reshape(ql, nh, HEAD)],
                             axis=2).reshape(ql, 2 * mlaw)
    last = ["w_in", "w_uq", "w_ukv"]
    assert last + mid + ffn == names

    last_pair = pair_sums(last, [gw_in, gw_uq, gw_ukv], "last")
    last_send, last_recv, last_pair, last_lands, last_token = _split_start(
        last_pair, [jax.ShapeDtypeStruct((3,) + p.shape[1:], p.dtype) for p in last_pair], _exchange_plan, 3,
        name="rs_exchange_last_start")
    ada = ada + last_token[0:1, 0:1]
    dh1 = _mm(dproj, w_in_p, name="mm_d_h1", tb=True, bias=jnp.zeros((1, D), F32) + last_token[0:1, 0:1])
    grad_x, red_1 = _rmsmod1_bwd(dh1, x, dx2, g_norm1, ada, SC1)
    last_pair, last_chips = _split_wait(last_send, last_recv, last_pair, last_lands, grad_x, _exchange_plan,
                                        name="rs_exchange_last_wait")
    mid_pair, mid_chips = _split_wait(mid_send, mid_recv, mid_pair, mid_lands, grad_x, _exchange_plan,
                                      name="rs_exchange_mid_wait")
    ffn_pair, ffn_chips = _split_wait(ffn_send, ffn_recv, ffn_pair, ffn_lands, grad_x, _exchange_plan,
                                      name="rs_exchange_ffn_wait")
    reduced = [_chip_sum(s, r, chip_idx, name="rs_chip_sum_" + nm)
               for s, r, nm in zip(last_pair + mid_pair + ffn_pair, last_chips + mid_chips + ffn_chips, names)]
    from_sibling2 = _sibling_swap(reduced, name="rs_sibling_send", whole=True)

    vec_names = ["b_ada", "g_norm1", "g_norm2", "g_q_latent", "g_kv_latent", "g_q_head", "g_k_head"]
    vec_w = dict(b_ada=b_ada, g_norm1=g_norm1, g_norm2=g_norm2, g_q_latent=g_q_latent, g_kv_latent=g_kv_latent,
                 g_q_head=g_q_head, g_k_head=g_k_head)
    vec_m = dict(b_ada=m_b_ada, g_norm1=m_g_norm1, g_norm2=m_g_norm2, g_q_latent=m_g_q_latent,
                 g_kv_latent=m_g_kv_latent, g_q_head=m_g_q_head, g_k_head=m_g_k_head)
    vec_v = dict(b_ada=v_b_ada, g_norm1=v_g_norm1, g_norm2=v_g_norm2, g_q_latent=v_g_q_latent,
                 g_kv_latent=v_g_kv_latent, g_q_head=v_g_q_head, g_k_head=v_g_k_head)
    d_ada = jnp.concatenate([red_1[0:1], red_1[1:2], red_2[3:4], red_2[0:1], red_2[1:2], red_l[0:1]], axis=1)
    vec_parts = dict(b_ada=d_ada, g_norm1=red_1[2:3], g_norm2=red_2[2:3], g_q_latent=red_lat[0:1],
                     g_kv_latent=red_lat[1:2], g_q_head=red_qh[0:1], g_k_head=red_kh[0:1])
    widths = [-(-vec_w[nm].shape[1] // LANE) * LANE for nm in vec_names]
    offs = [sum(widths[:i]) for i in range(len(widths))]
    pack = lambda d: jnp.concatenate([_pad_cols(d[nm][:, :vec_w[nm].shape[1]], wd) for nm, wd in zip(vec_names, widths)], axis=1)
    nvec = sum(widths) + LANE
    no_loss = jnp.zeros((1, LANE), F32)
    parts = jnp.concatenate([pack(vec_parts), loss_p[0:1, :]], axis=1)
    parts_all = _gather_blocks([jnp.broadcast_to(parts, (8, nvec))], name="gather_vec_grads",
                               in_vmem=True)[0][:, 0, :]
    gvec, dvec, nmvec, nvvec = _adamw_vec(parts_all, *[jnp.concatenate([pack(d), no_loss], axis=1)
                                                       for d in (vec_w, vec_m, vec_v)])
    loss = gvec[0, nvec - LANE]
    unpack = lambda a: {nm: a[:, o_:o_ + vec_w[nm].shape[1]] for nm, o_ in zip(vec_names, offs)}
    gvec, dvec, nmvec, nvvec = unpack(gvec), unpack(dvec), unpack(nmvec), unpack(nvvec)

    dada_all = lax.dynamic_slice_in_dim(parts_all[:, :6 * D], chip * n_ada, n_ada, axis=1)
    cact_t = jax.nn.silu(c_all).T
    g_ada, d_ada_w, nm_ada, nv_ada = _adamw_ada(cact_t, dada_all, w_ada[0], m_w_ada[0], v_w_ada[0])

    ms = dict(w_in=m_w_in, w_uq=m_w_uq, w_ukv=m_w_ukv, w_proj_mla=m_w_proj_mla, w_proj_sb=m_w_proj_sb,
              w_out=m_w_out, w_ffn_in=m_w_ffn_in, w_ffn_out=m_w_ffn_out)
    vs = dict(w_in=v_w_in, w_uq=v_w_uq, w_ukv=v_w_ukv, w_proj_mla=v_w_proj_mla, w_proj_sb=v_w_proj_sb,
              w_out=v_w_out, w_ffn_in=v_w_ffn_in, w_ffn_out=v_w_ffn_out)
    G, DL, NM, NV = {}, {}, {}, {}
    for nm, mine, other in zip(names, reduced, from_sibling2):
        g_, d_, m_, v_ = _adamw(mats[nm], mine, other, c_idx, ms[nm][0], vs[nm][0], name="adamw_" + nm)
        G[nm], DL[nm], NM[nm], NV[nm] = g_[None], d_[None], m_[None], v_[None]
    G["w_ada"], DL["w_ada"], NM["w_ada"], NV["w_ada"] = g_ada[None], d_ada_w[None], nm_ada[None], nv_ada[None]
    for nm in vec_names:
        G[nm], DL[nm], NM[nm], NV[nm] = gvec[nm], dvec[nm], nmvec[nm], nvvec[nm]

    order = ["w_ada", "b_ada", "g_norm1", "g_norm2", "w_in", "g_q_latent", "g_kv_latent", "w_uq", "w_ukv",
             "g_q_head", "g_k_head", "w_proj_mla", "w_proj_sb", "w_out", "w_ffn_in", "w_ffn_out"]
    return (loss, grad_x[None], *[G[n] for n in order], *[DL[n] for n in order],
            *[NM[n] for n in order], *[NV[n] for n in order])
```

```python
import functools
import math

import jax
import jax.numpy as jnp
from jax import lax
from jax.experimental import pallas as pl
from jax.experimental.pallas import tpu as pltpu

F32 = jnp.float32
BF16 = jnp.bfloat16
MESH = pl.DeviceIdType.MESH

EPS = 1e-6
ROPE_THETA = 10000.0
NOPE = 128
ROPE = 64
QK_DIM = NOPE + ROPE
HEAD_PAD = 256
HEAD = 128
N_DEV = 8
LANE = 128
VMEM_LIMIT = 48 * 1024 * 1024

ADAM_LR = 0.001
ADAM_B1 = 0.9
ADAM_B2 = 0.999
ADAM_EPS = 1e-08
ADAM_WD = 0.01
ADAM_STEP = 10


def _tile(n, target):
    if n <= target:
        return n
    t = (target // LANE) * LANE
    while t >= LANE:
        if n % t == 0:
            return t
        t -= LANE
    return n


def _row_tile(rows, row_bytes, budget=24 * 1024 * 1024):
    cap = max(8, budget // (2 * row_bytes))
    best = None
    for t in range(8, min(rows, cap) + 1, 8):
        if rows % t == 0:
            best = t
    return best if best is not None else rows


def _params(sem):
    return pltpu.CompilerParams(dimension_semantics=sem, vmem_limit_bytes=VMEM_LIMIT)


def _rows(tm, w, col=0):
    return pl.BlockSpec((tm, w), lambda i: (i, col))


def _vec(w, col=0, rows=1):
    return pl.BlockSpec((rows, w), lambda i: (0, col))


MM_VMEM_BUDGET = 36 * 1024 * 1024


def _mm(a, b, *, name, ta=False, tb=False, out_dtype=F32, a_fn=None, bias=None, tm=1024, tn=1024, pieces=None):
    M = a.shape[1] if ta else a.shape[0]
    K = a.shape[0] if ta else a.shape[1]
    N = b.shape[0] if tb else b.shape[1]
    assert K == (b.shape[1] if tb else b.shape[0]), (a.shape, b.shape, ta, tb)
    if pieces == "cols":
        tm, tn = _tile(M // 2, tm), _tile(N // 4, tn)
        assert (M // 2) % tm == 0 and (N // 4) % tn == 0
    elif pieces == "rows":
        tm, tn = M // 4, _tile(N, tn)
    else:
        tm, tn = _tile(M, tm), _tile(N, tn)
    sa, sb, so = a.dtype.itemsize, b.dtype.itemsize, jnp.dtype(out_dtype).itemsize

    def fits(tk):
        return 2 * tk * (tm * sa + tn * sb) + tm * tn * (2 * so + 4) <= MM_VMEM_BUDGET

    tk = K
    while not fits(tk):
        smaller = _tile(K, tk - LANE)
        if smaller >= tk:
            break
        tk = smaller
    nk = K // tk
    dn = (((0 if ta else 1,), (1 if tb else 0,)), ((), ()))
    b_outer = nk == 1 and a.size * sa * (N // tn) < b.size * sb * (M // tm)

    def body(*refs):
        a_ref, b_ref = refs[:2]
        bias_ref = refs[2] if bias is not None else None
        o_ref = refs[3 if bias is not None else 2]
        av = a_ref[...]
        if a_fn is not None:
            av = a_fn(av.astype(F32))
        part = lax.dot_general(av.astype(BF16), b_ref[...].astype(BF16), dn, preferred_element_type=F32)

        def finish(r):
            if bias is not None:
                r = r + bias_ref[...]
            if pieces == "rows":
                o_ref[0] = r[:tm // 2].astype(o_ref.dtype)
                o_ref[1] = r[tm // 2:].astype(o_ref.dtype)
            else:
                o_ref[...] = r.astype(o_ref.dtype)

        if nk == 1:
            finish(part)
        else:
            acc_ref = refs[-1]
            k = pl.program_id(2)

            @pl.when(k == 0)
            def _():
                acc_ref[...] = part

            @pl.when(k > 0)
            def _():
                acc_ref[...] += part

            @pl.when(k == nk - 1)
            def _():
                finish(acc_ref[...])

    def ij(g0, g1):
        return (g1, g0) if b_outer else (g0, g1)

    def amap(g0, g1, k):
        i, _ = ij(g0, g1)
        return (k, i) if ta else (i, k)

    def bmap(g0, g1, k):
        _, j = ij(g0, g1)
        return (j, k) if tb else (k, j)

    in_specs = [pl.BlockSpec((tk, tm) if ta else (tm, tk), amap), pl.BlockSpec((tn, tk) if tb else (tk, tn), bmap)]
    args = [a, b]
    if bias is not None:
        in_specs.append(pl.BlockSpec((1, tn), lambda g0, g1, k: (0, ij(g0, g1)[1])))
        args.append(bias)
    grid = (N // tn, M // tm, nk) if b_outer else (M // tm, N // tn, nk)
    if pieces == "cols":
        ni, nj = M // 2 // tm, N // 4 // tn

        def omap(g0, g1, k):
            i, j = ij(g0, g1)
            return (i // ni, j // nj, i % ni, j % nj)

        out_spec = pl.BlockSpec((None, None, tm, tn), omap)
        out_shape = jax.ShapeDtypeStruct((2, 4, M // 2, N // 4), out_dtype)
    elif pieces == "rows":
        out_spec = pl.BlockSpec((2, None, tm // 2, tn), lambda g0, g1, k: (0, ij(g0, g1)[0], 0, ij(g0, g1)[1]))
        out_shape = jax.ShapeDtypeStruct((2, 4, tm // 2, N), out_dtype)
    else:
        out_spec = pl.BlockSpec((tm, tn), lambda g0, g1, k: ij(g0, g1))
        out_shape = jax.ShapeDtypeStruct((M, N), out_dtype)
    return pl.pallas_call(
        body, name=name, grid=grid, in_specs=in_specs, out_specs=out_spec, out_shape=out_shape,
        scratch_shapes=[pltpu.VMEM((tm, tn), F32)] if nk > 1 else [],
        compiler_params=_params(("parallel", "parallel", "arbitrary")),
    )(*args)


def _rms_rows(v):
    return lax.rsqrt(jnp.mean(v * v, axis=-1, keepdims=True) + EPS)


def _rmsmod(x, g, ada, sc_col, sh_col, *, name):
    S, D = x.shape
    tm = _tile(S, 256)

    def body(x_ref, g_ref, sc_ref, sh_ref, h_ref):
        xv = x_ref[...]
        h = (xv * _rms_rows(xv) * g_ref[...]) * (1.0 + sc_ref[...]) + sh_ref[...]
        h_ref[...] = h.astype(h_ref.dtype)

    return pl.pallas_call(
        body, name=name, grid=(S // tm,),
        in_specs=[_rows(tm, D), _vec(D), _vec(D, sc_col), _vec(D, sh_col)],
        out_specs=_rows(tm, D), out_shape=jax.ShapeDtypeStruct((S, D), BF16),
        compiler_params=_params(("parallel",)),
    )(x, g, ada, ada)


def _latent_norm(proj, g_q, g_kv, ql):
    S = proj.shape[0]
    tm = _tile(S, 512)

    def body(cq_ref, ckv_ref, gq_ref, gkv_ref, oq_ref, okv_ref):
        cq = cq_ref[...]
        oq_ref[...] = (cq * _rms_rows(cq) * gq_ref[...]).astype(BF16)
        ckv = ckv_ref[...]
        okv_ref[...] = (ckv * _rms_rows(ckv) * gkv_ref[...]).astype(BF16)

    return pl.pallas_call(
        body, name="latent_norm", grid=(S // tm,),
        in_specs=[_rows(tm, ql, 0), _rows(tm, ql, 1), _vec(ql), _vec(ql)],
        out_specs=[_rows(tm, ql), _rows(tm, ql)],
        out_shape=[jax.ShapeDtypeStruct((S, ql), BF16)] * 2,
        compiler_params=_params(("parallel",)),
    )(proj, proj, g_q, g_kv)


def _rope_fwd(y, c, s1, s2):
    return y * c + pltpu.roll(y, ROPE // 2, 1) * s1 + pltpu.roll(y, HEAD_PAD - ROPE // 2, 1) * s2


def _rope_bwd(d, c, s1, s2):
    return d * c + pltpu.roll(d * s1, HEAD_PAD - ROPE // 2, 1) + pltpu.roll(d * s2, ROPE // 2, 1)


def _head_rms(v):
    return lax.rsqrt(jnp.sum(v * v, axis=-1, keepdims=True) * (1.0 / QK_DIM) + EPS)


def _q_prep(q0, g_qh, tabs, nh):
    S = q0.shape[0]
    tm = _tile(S, 256)

    def body(q_ref, g_ref, c_ref, s1_ref, s2_ref, o_ref):
        c, s1, s2, g = c_ref[...], s1_ref[...], s2_ref[...], g_ref[...]
        for h in range(nh):
            sl = slice(h * HEAD_PAD, (h + 1) * HEAD_PAD)
            xs = q_ref[:, sl]
            o_ref[:, sl] = (_rope_fwd(xs * _head_rms(xs) * g, c, s1, s2) * (QK_DIM ** -0.5)).astype(BF16)

    w = nh * HEAD_PAD
    return pl.pallas_call(
        body, name="mla_q_prep", grid=(S // tm,),
        in_specs=[_rows(tm, w), _vec(HEAD_PAD)] + [_rows(tm, HEAD_PAD)] * 3,
        out_specs=_rows(tm, w), out_shape=jax.ShapeDtypeStruct((S, w), BF16),
        compiler_params=_params(("parallel",)),
    )(q0, g_qh, *tabs)


def _k_prep(kv0, proj, kpe_col, g_kh, tabs, nh):
    S = kv0.shape[0]
    tm = _tile(S, 256)

    def body(kv_ref, kpe_ref, g_ref, c_ref, s1_ref, s2_ref, o_ref):
        c, s1, s2, g = c_ref[...], s1_ref[...], s2_ref[...], g_ref[...]
        kpe = kpe_ref[...]
        for h in range(nh):
            k0 = jnp.concatenate([kv_ref[:, h * HEAD:(h + 1) * HEAD], kpe], axis=1)
            o_ref[:, h * HEAD_PAD:(h + 1) * HEAD_PAD] = _rope_fwd(k0 * _head_rms(k0) * g, c, s1, s2).astype(BF16)

    return pl.pallas_call(
        body, name="mla_k_prep", grid=(S // tm,),
        in_specs=[_rows(tm, nh * HEAD, 0), _rows(tm, LANE, kpe_col), _vec(HEAD_PAD)] + [_rows(tm, HEAD_PAD)] * 3,
        out_specs=_rows(tm, nh * HEAD_PAD), out_shape=jax.ShapeDtypeStruct((S, nh * HEAD_PAD), BF16),
        compiler_params=_params(("parallel",)),
    )(kv0, proj, g_kh, *tabs)


def _gate_merge(pa, pb, proj, gla_col, glb_col):
    S, D = pa.shape
    tm = _tile(S, 256)

    def body(pa_ref, pb_ref, ga_ref, gb_ref, o_ref):
        o_ref[...] = (jax.nn.sigmoid(ga_ref[...]) * pa_ref[...] + jax.nn.sigmoid(gb_ref[...]) * pb_ref[...]).astype(BF16)

    return pl.pallas_call(
        body, name="gate_merge", grid=(S // tm,),
        in_specs=[_rows(tm, D), _rows(tm, D), _rows(tm, D, gla_col), _rows(tm, D, glb_col)],
        out_specs=_rows(tm, D), out_shape=jax.ShapeDtypeStruct((S, D), BF16),
        compiler_params=_params(("parallel",)),
    )(pa, pb, proj, proj)


def _resid_rmsmod(x, o, g, ada, gt_col, sc_col, sh_col):
    S, D = x.shape
    tm = _tile(S, 256)

    def body(x_ref, o_ref, g_ref, gt_ref, sc_ref, sh_ref, x2_ref, h_ref):
        x2 = x_ref[...] + gt_ref[...] * o_ref[...]
        x2_ref[...] = x2
        h_ref[...] = ((x2 * _rms_rows(x2) * g_ref[...]) * (1.0 + sc_ref[...]) + sh_ref[...]).astype(BF16)

    return pl.pallas_call(
        body, name="resid_rmsmod2", grid=(S // tm,),
        in_specs=[_rows(tm, D), _rows(tm, D), _vec(D), _vec(D, gt_col), _vec(D, sc_col), _vec(D, sh_col)],
        out_specs=[_rows(tm, D), _rows(tm, D)],
        out_shape=[jax.ShapeDtypeStruct((S, D), F32), jax.ShapeDtypeStruct((S, D), BF16)],
        compiler_params=_params(("parallel",)),
    )(x, o, g, ada, ada, ada)


def _swiglu(ff, dff_half):
    S = ff.shape[0]
    tm = _tile(S, 256)

    def body(g_ref, u_ref, o_ref):
        o_ref[...] = (jax.nn.silu(g_ref[...].astype(F32)) * u_ref[...].astype(F32)).astype(BF16)

    return pl.pallas_call(
        body, name="swiglu", grid=(S // tm,),
        in_specs=[_rows(tm, dff_half, 0), _rows(tm, dff_half, 1)],
        out_specs=_rows(tm, dff_half), out_shape=jax.ShapeDtypeStruct((S, dff_half), BF16),
        compiler_params=_params(("parallel",)),
    )(ff, ff)


def _loss_head(x2, f, tgt, ada, gt_col):
    S, D = x2.shape
    tm = _tile(S, 256)

    def body(x2_ref, f_ref, t_ref, gt_ref, dy_ref, df_ref, red_ref, loss_ref):
        @pl.when(pl.program_id(0) == 0)
        def _():
            red_ref[...] = jnp.zeros_like(red_ref)
            loss_ref[...] = jnp.zeros_like(loss_ref)

        fv = f_ref[...]
        gt = gt_ref[...]
        err = x2_ref[...] + gt * fv - t_ref[...]
        dy = err * (1.0 / D)
        dy_ref[...] = dy
        df_ref[...] = (dy * gt).astype(BF16)
        red_ref[0:1, :] += jnp.sum(dy * fv, axis=0, keepdims=True)
        loss_ref[...] += (0.5 / D) * jnp.sum(err * err)

    return pl.pallas_call(
        body, name="loss_head", grid=(S // tm,),
        in_specs=[_rows(tm, D), _rows(tm, D), _rows(tm, D), _vec(D, gt_col)],
        out_specs=[_rows(tm, D), _rows(tm, D), _vec(D, rows=8), _vec(LANE, rows=8)],
        out_shape=[jax.ShapeDtypeStruct((S, D), F32), jax.ShapeDtypeStruct((S, D), BF16),
                   jax.ShapeDtypeStruct((8, D), F32), jax.ShapeDtypeStruct((8, LANE), F32)],
        compiler_params=_params(("arbitrary",)),
    )(x2, f, tgt, ada)


def _swiglu_bwd(dact, ff, dff_half):
    S = ff.shape[0]
    tm = _tile(S, 128)

    def body(d_ref, g_ref, u_ref, o_ref):
        d = d_ref[...]
        g = g_ref[...].astype(F32)
        u = u_ref[...].astype(F32)
        sg = jax.nn.sigmoid(g)
        o_ref[:, :dff_half] = (d * u * sg * (1.0 + g * (1.0 - sg))).astype(BF16)
        o_ref[:, dff_half:] = (d * g * sg).astype(BF16)

    return pl.pallas_call(
        body, name="swiglu_bwd", grid=(S // tm,),
        in_specs=[_rows(tm, dff_half), _rows(tm, dff_half, 0), _rows(tm, dff_half, 1)],
        out_specs=_rows(tm, 2 * dff_half), out_shape=jax.ShapeDtypeStruct((S, 2 * dff_half), BF16),
        compiler_params=_params(("parallel",)),
    )(dact, ff, ff)


def _rmsmod2_bwd(dh2, x2, dy, o, g, ada, sc_col, gt_col):
    S, D = x2.shape
    tm = _tile(S, 256)

    def body(dh_ref, x2_ref, dy_ref, o_ref, g_ref, sc_ref, gt_ref, dx_ref, do_ref, red_ref):
        @pl.when(pl.program_id(0) == 0)
        def _():
            red_ref[...] = jnp.zeros_like(red_ref)

        dh = dh_ref[...]
        x2 = x2_ref[...]
        gv = g_ref[...]
        mod = 1.0 + sc_ref[...]
        r = _rms_rows(x2)
        xn = x2 * r
        t = dh * xn
        red_ref[0:1, :] += jnp.sum(dh, axis=0, keepdims=True)
        red_ref[1:2, :] += jnp.sum(t * gv, axis=0, keepdims=True)
        red_ref[2:3, :] += jnp.sum(t * mod, axis=0, keepdims=True)
        dxn = dh * gv * mod
        dx = dy_ref[...] + r * (dxn - xn * jnp.mean(dxn * xn, axis=-1, keepdims=True))
        dx_ref[...] = dx
        red_ref[3:4, :] += jnp.sum(dx * o_ref[...], axis=0, keepdims=True)
        do_ref[...] = (dx * gt_ref[...]).astype(BF16)

    return pl.pallas_call(
        body, name="rmsmod2_bwd", grid=(S // tm,),
        in_specs=[_rows(tm, D)] * 4 + [_vec(D), _vec(D, sc_col), _vec(D, gt_col)],
        out_specs=[_rows(tm, D), _rows(tm, D), _vec(D, rows=8)],
        out_shape=[jax.ShapeDtypeStruct((S, D), F32), jax.ShapeDtypeStruct((S, D), BF16),
                   jax.ShapeDtypeStruct((8, D), F32)],
        compiler_params=_params(("arbitrary",)),
    )(dh2, x2, dy, o, g, ada, ada)


def _rmsmod1_bwd(dh, x, dx2, g, ada, sc_col):
    S, D = x.shape
    tm = _tile(S, 256)

    def body(dh_ref, x_ref, dx2_ref, g_ref, sc_ref, gx_ref, red_ref):
        @pl.when(pl.program_id(0) == 0)
        def _():
            red_ref[...] = jnp.zeros_like(red_ref)

        dh = dh_ref[...]
        xv = x_ref[...]
        gv = g_ref[...]
        mod = 1.0 + sc_ref[...]
        r = _rms_rows(xv)
        xn = xv * r
        t = dh * xn
        red_ref[0:1, :] += jnp.sum(dh, axis=0, keepdims=True)
        red_ref[1:2, :] += jnp.sum(t * gv, axis=0, keepdims=True)
        red_ref[2:3, :] += jnp.sum(t * mod, axis=0, keepdims=True)
        dxn = dh * gv * mod
        gx_ref[...] = dx2_ref[...] + r * (dxn - xn * jnp.mean(dxn * xn, axis=-1, keepdims=True))

    return pl.pallas_call(
        body, name="rmsmod1_bwd", grid=(S // tm,),
        in_specs=[_rows(tm, D)] * 3 + [_vec(D), _vec(D, sc_col)],
        out_specs=[_rows(tm, D), _vec(D, rows=8)],
        out_shape=[jax.ShapeDtypeStruct((S, D), F32), jax.ShapeDtypeStruct((8, D), F32)],
        compiler_params=_params(("arbitrary",)),
    )(dh, x, dx2, g, ada)


def _gate_bwd(dm, pa, pb, proj, gla_col, glb_col):
    S, D = pa.shape
    tm = _tile(S, 256)

    def body(dm_ref, pa_ref, pb_ref, la_ref, lb_ref, dpa_ref, dpb_ref, dla_ref, dlb_ref):
        dm_ = dm_ref[...]
        ga = jax.nn.sigmoid(la_ref[...])
        gb = jax.nn.sigmoid(lb_ref[...])
        dpa_ref[...] = (dm_ * ga).astype(BF16)
        dpb_ref[...] = (dm_ * gb).astype(BF16)
        dla_ref[...] = (dm_ * pa_ref[...] * ga * (1.0 - ga)).astype(BF16)
        dlb_ref[...] = (dm_ * pb_ref[...] * gb * (1.0 - gb)).astype(BF16)

    return pl.pallas_call(
        body, name="gate_bwd", grid=(S // tm,),
        in_specs=[_rows(tm, D)] * 3 + [_rows(tm, D, gla_col), _rows(tm, D, glb_col)],
        out_specs=[_rows(tm, D)] * 4, out_shape=[jax.ShapeDtypeStruct((S, D), BF16)] * 4,
        compiler_params=_params(("parallel",)),
    )(dm, pa, pb, proj, proj)


def _q_prep_bwd(dq, q0, g_qh, tabs, nh):
    S = q0.shape[0]
    tm = _tile(S, 256)

    def body(dq_ref, q_ref, g_ref, c_ref, s1_ref, s2_ref, o_ref, red_ref):
        @pl.when(pl.program_id(0) == 0)
        def _():
            red_ref[...] = jnp.zeros_like(red_ref)

        c, s1, s2, g = c_ref[...], s1_ref[...], s2_ref[...], g_ref[...]
        dg = jnp.zeros((1, HEAD_PAD), F32)
        for h in range(nh):
            sl = slice(h * HEAD_PAD, (h + 1) * HEAD_PAD)
            d1 = _rope_bwd(dq_ref[:, sl], c, s1, s2)
            xs = q_ref[:, sl]
            r = _head_rms(xs)
            qn = xs * r
            dg = dg + jnp.sum(d1 * qn, axis=0, keepdims=True)
            dn = d1 * g
            o_ref[:, sl] = (r * (dn - qn * (jnp.sum(dn * qn, axis=-1, keepdims=True) * (1.0 / QK_DIM)))).astype(BF16)
        red_ref[0:1, :] += dg

    w = nh * HEAD_PAD
    return pl.pallas_call(
        body, name="mla_q_prep_bwd", grid=(S // tm,),
        in_specs=[_rows(tm, w), _rows(tm, w), _vec(HEAD_PAD)] + [_rows(tm, HEAD_PAD)] * 3,
        out_specs=[_rows(tm, w), _vec(HEAD_PAD, rows=8)],
        out_shape=[jax.ShapeDtypeStruct((S, w), BF16), jax.ShapeDtypeStruct((8, HEAD_PAD), F32)],
        compiler_params=_params(("arbitrary",)),
    )(dq, q0, g_qh, *tabs)


def _k_prep_bwd(dk, dv, kv0, proj, kpe_col, g_kh, tabs, nh):
    S = kv0.shape[0]
    tm = _tile(S, 256)
    wv = nh * HEAD

    def body(dk_ref, dv_ref, kv_ref, kpe_ref, g_ref, c_ref, s1_ref, s2_ref, o_ref, dpe_ref, red_ref):
        @pl.when(pl.program_id(0) == 0)
        def _():
            red_ref[...] = jnp.zeros_like(red_ref)

        c, s1, s2, g = c_ref[...], s1_ref[...], s2_ref[...], g_ref[...]
        kpe = kpe_ref[...]
        dg = jnp.zeros((1, HEAD_PAD), F32)
        dpe = jnp.zeros((tm, LANE), F32)
        for h in range(nh):
            d1 = _rope_bwd(dk_ref[:, h * HEAD_PAD:(h + 1) * HEAD_PAD], c, s1, s2)
            k0 = jnp.concatenate([kv_ref[:, h * HEAD:(h + 1) * HEAD], kpe], axis=1)
            r = _head_rms(k0)
            kn = k0 * r
            dg = dg + jnp.sum(d1 * kn, axis=0, keepdims=True)
            dn = d1 * g
            dk0 = r * (dn - kn * (jnp.sum(dn * kn, axis=-1, keepdims=True) * (1.0 / QK_DIM)))
            o_ref[:, h * HEAD:(h + 1) * HEAD] = dk0[:, :HEAD].astype(BF16)
            dpe = dpe + dk0[:, HEAD:]
        o_ref[:, wv:] = dv_ref[...].astype(BF16)
        dpe_ref[...] = dpe.astype(BF16)
        red_ref[0:1, :] += dg

    return pl.pallas_call(
        body, name="mla_k_prep_bwd", grid=(S // tm,),
        in_specs=[_rows(tm, nh * HEAD_PAD), _rows(tm, wv), _rows(tm, wv, 0), _rows(tm, LANE, kpe_col),
                  _vec(HEAD_PAD)] + [_rows(tm, HEAD_PAD)] * 3,
        out_specs=[_rows(tm, 2 * wv), _rows(tm, LANE), _vec(HEAD_PAD, rows=8)],
        out_shape=[jax.ShapeDtypeStruct((S, 2 * wv), BF16), jax.ShapeDtypeStruct((S, LANE), BF16),
                   jax.ShapeDtypeStruct((8, HEAD_PAD), F32)],
        compiler_params=_params(("arbitrary",)),
    )(dk, dv, kv0, proj, g_kh, *tabs)


def _latent_norm_bwd(dcqn, dckvn, proj, g_q, g_kv, ql):
    S = proj.shape[0]
    tm = _tile(S, 512)

    def body(dq_ref, dkv_ref, cq_ref, ckv_ref, gq_ref, gkv_ref, oq_ref, okv_ref, red_ref):
        @pl.when(pl.program_id(0) == 0)
        def _():
            red_ref[...] = jnp.zeros_like(red_ref)

        for row, (d_ref, c_ref, g_ref, o_ref) in enumerate(((dq_ref, cq_ref, gq_ref, oq_ref),
                                                            (dkv_ref, ckv_ref, gkv_ref, okv_ref))):
            d = d_ref[...]
            cv = c_ref[...]
            r = _rms_rows(cv)
            ch = cv * r
            red_ref[row:row + 1, :] += jnp.sum(d * ch, axis=0, keepdims=True)
            dn = d * g_ref[...]
            o_ref[...] = (r * (dn - ch * jnp.mean(dn * ch, axis=-1, keepdims=True))).astype(BF16)

    return pl.pallas_call(
        body, name="latent_norm_bwd", grid=(S // tm,),
        in_specs=[_rows(tm, ql), _rows(tm, ql), _rows(tm, ql, 0), _rows(tm, ql, 1), _vec(ql), _vec(ql)],
        out_specs=[_rows(tm, ql), _rows(tm, ql), _vec(ql, rows=8)],
        out_shape=[jax.ShapeDtypeStruct((S, ql), BF16)] * 2 + [jax.ShapeDtypeStruct((8, ql), F32)],
        compiler_params=_params(("arbitrary",)),
    )(dcqn, dckvn, proj, proj, g_q, g_kv)


NEG = -1e30
ATT_TILE = 512
SB_SUB = 128
_NT = (((1,), (1,)), ((), ()))
_TN = (((0,), (0,)), ((), ()))


def _dot(a, b, dn=(((1,), (0,)), ((), ()))):
    return lax.dot_general(a, b, dn, preferred_element_type=F32)


def _key_rows(kb, t):
    return pl.ds(pl.multiple_of(kb * t, t), t)


def _diag_mask(t, strict):
    r = lax.broadcasted_iota(jnp.int32, (t, t), 0)
    c = lax.broadcasted_iota(jnp.int32, (t, t), 1)
    return c < r if strict else c <= r


def _mla_fwd(q, k, kv0, nh):
    S = q.shape[0]
    t = _tile(S, ATT_TILE)

    def body(q_ref, k_ref, v_ref, o_ref, lse_ref):
        i = pl.program_id(1)
        qv = q_ref[...]

        def block(kb, carry, masked):
            m, l, acc = carry
            rows = _key_rows(kb, t)
            s = _dot(qv, k_ref[rows, :], _NT)
            if masked:
                s = jnp.where(_diag_mask(t, False), s, NEG)
            m_new = jnp.maximum(m, jnp.max(s, axis=-1, keepdims=True))
            alpha = jnp.exp(m - m_new)
            p = jnp.exp(s - m_new)
            l = alpha * l + jnp.sum(p, axis=-1, keepdims=True)
            acc = alpha * acc + _dot(p.astype(BF16), v_ref[rows, :].astype(BF16))
            return m_new, l, acc

        init = (jnp.full((t, 1), NEG, F32), jnp.zeros((t, 1), F32), jnp.zeros((t, HEAD), F32))
        carry = lax.fori_loop(0, i, lambda kb, c: block(kb, c, False), init)
        m, l, acc = block(i, carry, True)
        o_ref[...] = acc / l
        lse_ref[...] = m + jnp.log(l)

    return pl.pallas_call(
        body, name="mla_attn_fwd", grid=(nh, S // t),
        in_specs=[pl.BlockSpec((t, HEAD_PAD), lambda h, i: (i, h)),
                  pl.BlockSpec((S, HEAD_PAD), lambda h, i: (0, h)),
                  pl.BlockSpec((S, HEAD), lambda h, i: (0, nh + h))],
        out_specs=[pl.BlockSpec((t, HEAD), lambda h, i: (i, h)),
                   pl.BlockSpec((None, t, 1), lambda h, i: (h, i, 0))],
        out_shape=[jax.ShapeDtypeStruct((S, nh * HEAD), F32), jax.ShapeDtypeStruct((nh, S, 1), F32)],
        compiler_params=_params(("parallel", "arbitrary")),
    )(q, k, kv0)


def _mla_bwd(q, k, kv0, o, do, lse, nh):
    S = q.shape[0]
    t = _tile(S, ATT_TILE)
    scale = QK_DIM ** -0.5

    def body(q_ref, k_ref, v_ref, o_ref, do_ref, lse_ref, dq_ref, dk_ref, dv_ref):
        i = pl.program_id(1)

        @pl.when(i == 0)
        def _():
            dk_ref[...] = jnp.zeros_like(dk_ref)
            dv_ref[...] = jnp.zeros_like(dv_ref)

        qv = q_ref[...]
        dov = do_ref[...]
        delta = jnp.sum(dov * o_ref[...], axis=-1, keepdims=True)
        dob = dov.astype(BF16)
        lse = lse_ref[...]

        def block(kb, dq, masked):
            rows = _key_rows(kb, t)
            ks = k_ref[rows, :]
            vs = v_ref[rows, :].astype(BF16)
            p = jnp.exp(_dot(qv, ks, _NT) - lse)
            if masked:
                p = jnp.where(_diag_mask(t, False), p, 0.0)
            ds = (p * (_dot(dob, vs, _NT) - delta)).astype(BF16)
            dk_ref[rows, :] += _dot(ds, qv, _TN)
            dv_ref[rows, :] += _dot(p.astype(BF16), dob, _TN)
            return dq + _dot(ds, ks)

        dq = lax.fori_loop(0, i, lambda kb, c: block(kb, c, False), jnp.zeros((t, HEAD_PAD), F32))
        dq_ref[...] = block(i, dq, True) * scale

    return pl.pallas_call(
        body, name="mla_attn_bwd", grid=(nh, S // t),
        in_specs=[pl.BlockSpec((t, HEAD_PAD), lambda h, i: (i, h)),
                  pl.BlockSpec((S, HEAD_PAD), lambda h, i: (0, h)),
                  pl.BlockSpec((S, HEAD), lambda h, i: (0, nh + h)),
                  pl.BlockSpec((t, HEAD), lambda h, i: (i, h)),
                  pl.BlockSpec((t, HEAD), lambda h, i: (i, h)),
                  pl.BlockSpec((None, t, 1), lambda h, i: (h, i, 0))],
        out_specs=[pl.BlockSpec((t, HEAD_PAD), lambda h, i: (i, h)),
                   pl.BlockSpec((S, HEAD_PAD), lambda h, i: (0, h)),
                   pl.BlockSpec((S, HEAD), lambda h, i: (0, h))],
        out_shape=[jax.ShapeDtypeStruct((S, nh * HEAD_PAD), F32), jax.ShapeDtypeStruct((S, nh * HEAD_PAD), F32),
                   jax.ShapeDtypeStruct((S, nh * HEAD), F32)],
        compiler_params=_params(("parallel", "arbitrary")),
    )(q, k, kv0, o, do, lse)


def _tri(n, cmp):
    r = lax.broadcasted_iota(jnp.int32, (n, n), 0)
    c = lax.broadcasted_iota(jnp.int32, (n, n), 1)
    return jnp.where(cmp(r, c), 1.0, 0.0).astype(BF16)


def _sb_block(qv, ks, run, upper, t, masked):
    z = _dot(qv, ks, _NT)
    lb = jnp.minimum(z, 0.0) - jnp.log(1.0 + jnp.exp(-jnp.abs(z)))
    lom = lb - z
    mask = _diag_mask(t, True) if masked else None
    if masked:
        lom = jnp.where(mask, lom, 0.0)
    tails = []
    for sblk in reversed(range(t // SB_SUB)):
        part = lom[:, sblk * SB_SUB:(sblk + 1) * SB_SUB]
        tails.append(_dot(part.astype(BF16), upper) + run)
        run = run + jnp.sum(part, axis=-1, keepdims=True)
    a = jnp.exp(lb + jnp.concatenate(tails[::-1], axis=1))
    if masked:
        a = jnp.where(mask, a, 0.0)
    return a, lb, mask, run


def _sb_fwd(proj, q_col, k_col, v_col, nh):
    S = proj.shape[0]
    t = _tile(S, ATT_TILE)
    assert S // t <= LANE
    scale = HEAD ** -0.5

    def body(q_ref, k_ref, v_ref, o_ref, runs_ref):
        i = pl.program_id(1)
        qv = (q_ref[...] * scale).astype(BF16)
        upper = _tri(SB_SUB, lambda j, s: j > s)
        lane = lax.broadcasted_iota(jnp.int32, (t, LANE), 1)

        def block(kb, carry, masked):
            run, acc, runs = carry
            runs = jnp.where(lane == kb, run, runs)
            rows = _key_rows(kb, t)
            a, _, _, run = _sb_block(qv, k_ref[rows, :].astype(BF16), run, upper, t, masked)
            return run, acc + _dot(a.astype(BF16), v_ref[rows, :].astype(BF16)), runs

        carry = block(i, (jnp.zeros((t, 1), F32), jnp.zeros((t, HEAD), F32), jnp.zeros((t, LANE), F32)), True)
        _, o_ref[...], runs_ref[...] = lax.fori_loop(0, i, lambda j, c: block(i - 1 - j, c, False), carry)

    return pl.pallas_call(
        body, name="sb_attn_fwd", grid=(nh, S // t),
        in_specs=[pl.BlockSpec((t, HEAD), lambda h, i: (i, q_col + h)),
                  pl.BlockSpec((S, HEAD), lambda h, i: (0, k_col + h)),
                  pl.BlockSpec((S, HEAD), lambda h, i: (0, v_col + h))],
        out_specs=[pl.BlockSpec((t, HEAD), lambda h, i: (i, h)), pl.BlockSpec((None, t, LANE), lambda h, i: (h, i, 0))],
        out_shape=[jax.ShapeDtypeStruct((S, nh * HEAD), F32), jax.ShapeDtypeStruct((nh, S, LANE), F32)],
        compiler_params=_params(("parallel", "arbitrary")),
    )(proj, proj, proj)


def _sb_bwd(proj, q_col, k_col, v_col, dy, runs, nh):
    S = proj.shape[0]
    t = _tile(S, ATT_TILE)
    scale = HEAD ** -0.5

    def body(q_ref, k_ref, v_ref, dy_ref, runs_ref, dq_ref, dk_ref, dv_ref):
        i = pl.program_id(1)

        @pl.when(i == 0)
        def _():
            dk_ref[...] = jnp.zeros_like(dk_ref)
            dv_ref[...] = jnp.zeros_like(dv_ref)

        qv = (q_ref[...] * scale).astype(BF16)
        dyb = dy_ref[...].astype(BF16)
        runs_v = runs_ref[...]
        lane = lax.broadcasted_iota(jnp.int32, (t, LANE), 1)
        upper = _tri(SB_SUB, lambda j, s: j > s)
        before = _tri(SB_SUB, lambda s, j: s < j)

        def block(kb, carry, masked):
            prefix, dq = carry
            rows = _key_rows(kb, t)
            ks = k_ref[rows, :].astype(BF16)
            vs = v_ref[rows, :].astype(BF16)
            run = jnp.sum(jnp.where(lane == kb, runs_v, 0.0), axis=-1, keepdims=True)
            a, lb, mask, _ = _sb_block(qv, ks, run, upper, t, masked)
            dl = a * _dot(dyb, vs, _NT)
            lefts = []
            for sblk in range(t // SB_SUB):
                part = dl[:, sblk * SB_SUB:(sblk + 1) * SB_SUB]
                lefts.append(_dot(part.astype(BF16), before) + prefix)
                prefix = prefix + jnp.sum(part, axis=-1, keepdims=True)
            beta = jnp.exp(lb)
            dz = dl * (1.0 - beta) - beta * jnp.concatenate(lefts, axis=1)
            if masked:
                dz = jnp.where(mask, dz, 0.0)
            dz = dz.astype(BF16)
            dk_ref[rows, :] += _dot(dz, qv, _TN)
            dv_ref[rows, :] += _dot(a.astype(BF16), dyb, _TN)
            return prefix, dq + _dot(dz, ks)

        carry = lax.fori_loop(0, i, lambda kb, c: block(kb, c, False),
                              (jnp.zeros((t, 1), F32), jnp.zeros((t, HEAD), F32)))
        dq_ref[...] = block(i, carry, True)[1] * scale

    full = pl.BlockSpec((S, HEAD), lambda h, i: (0, h))
    tile = pl.BlockSpec((t, HEAD), lambda h, i: (i, h))
    return pl.pallas_call(
        body, name="sb_attn_bwd", grid=(nh, S // t),
        in_specs=[pl.BlockSpec((t, HEAD), lambda h, i: (i, q_col + h)),
                  pl.BlockSpec((S, HEAD), lambda h, i: (0, k_col + h)),
                  pl.BlockSpec((S, HEAD), lambda h, i: (0, v_col + h)), tile,
                  pl.BlockSpec((None, t, LANE), lambda h, i: (h, i, 0))],
        out_specs=[tile, full, full],
        out_shape=[jax.ShapeDtypeStruct((S, nh * HEAD), F32)] * 3,
        compiler_params=_params(("parallel", "arbitrary")),
    )(proj, proj, proj, dy, runs)


def _place():
    return lax.axis_index("x"), lax.axis_index("y"), lax.axis_index("c")


def _other_chips(x, y):
    return [(1 - x, y), (x, 1 - y), (1 - x, 1 - y)]


def _dev_index(p):
    return 4 * p[0] + 2 * p[1] + p[2]


def _gather_blocks(blocks, *, name, in_vmem):
    n = len(blocks)
    per = 7

    def body(*refs):
        ins, outs = refs[:n], refs[n:2 * n]
        send_sems, recv_sems, local_sems = refs[2 * n:]
        x, y, c = _place()
        me, sibling = (x, y, c), (x, y, 1 - c)
        chips = _other_chips(x, y)

        def slot(a, p):
            return outs[a].at[_dev_index(p)]

        def copy(a, k, block, to, src=None):
            return pltpu.make_async_remote_copy(
                src_ref=slot(a, block) if src is None else src, dst_ref=slot(a, block),
                send_sem=send_sems.at[a * per + k], recv_sem=recv_sems.at[a * per + k],
                device_id=to, device_id_type=MESH)

        mine = [pltpu.make_async_copy(ins[a], slot(a, me), local_sems.at[a]) for a in range(n)] if in_vmem else []
        for cp in mine:
            cp.start()
        first = []
        for a in range(n):
            first.append(copy(a, 0, me, sibling, src=ins[a]))
            first += [copy(a, 1 + j, me, (*chip, c), src=ins[a]) for j, chip in enumerate(chips)]
        for cp in first:
            cp.start()
        passed = []
        for a in range(n):
            for j, chip in enumerate(chips):
                copy(a, 1 + j, (*chip, c), me).wait_recv()
                cp = copy(a, 4 + j, (*chip, c), sibling)
                cp.start()
                passed.append(cp)
        for a in range(n):
            copy(a, 0, sibling, me).wait_recv()
            for j, chip in enumerate(chips):
                copy(a, 4 + j, (*chip, 1 - c), me).wait_recv()
        for cp in first + passed:
            cp.wait_send()
        for cp in mine:
            cp.wait()

    space = pltpu.VMEM if in_vmem else pl.ANY
    spec = pl.BlockSpec(memory_space=space)
    outs = pl.pallas_call(
        body, name=name, in_specs=[spec] * n, out_specs=[spec] * n,
        out_shape=[jax.ShapeDtypeStruct((N_DEV,) + b.shape, b.dtype) for b in blocks],
        scratch_shapes=[pltpu.SemaphoreType.DMA((n * per,)), pltpu.SemaphoreType.DMA((n * per,)),
                        pltpu.SemaphoreType.DMA((n,))],
        compiler_params=pltpu.CompilerParams(vmem_limit_bytes=VMEM_LIMIT),
    )(*blocks)
    return list(outs)


def _sibling_swap(arrs, *, name, whole=False):
    n = len(arrs)

    def body(*refs):
        ins, outs = refs[:n], refs[n:2 * n]
        send_sems, recv_sems = refs[2 * n:]
        x, y, c = _place()
        copies = [pltpu.make_async_remote_copy(
            src_ref=ins[a] if whole else ins[a].at[1 - c], dst_ref=outs[a],
            send_sem=send_sems.at[a], recv_sem=recv_sems.at[a],
            device_id=(x, y, 1 - c), device_id_type=MESH) for a in range(n)]
        for cp in copies:
            cp.start()
        for cp in copies:
            cp.wait()

    spec = pl.BlockSpec(memory_space=pl.ANY)
    return list(pl.pallas_call(
        body, name=name, in_specs=[spec] * n, out_specs=[spec] * n,
        out_shape=[jax.ShapeDtypeStruct(a.shape if whole else a.shape[1:], a.dtype) for a in arrs],
        scratch_shapes=[pltpu.SemaphoreType.DMA((n,)), pltpu.SemaphoreType.DMA((n,))],
    )(*arrs))


_HBM = pl.BlockSpec(memory_space=pltpu.HBM)
_SEM = pl.BlockSpec(memory_space=pltpu.SEMAPHORE)
_EFFECT = pltpu.SideEffectType.DATAFLOW_SIDE_EFFECTING


def _in_hbm(a):
    return pltpu.with_memory_space_constraint(a, pltpu.HBM)


def _split_copies(srcs, lands, send_sems, recv_sems, plan):
    x, y, c = _place()
    copies = []
    for a, (src, land) in enumerate(zip(srcs, lands)):
        steps = plan(x, y, c)
        for k, (pick, slot, to) in enumerate(steps):
            copies.append(pltpu.make_async_remote_copy(
                src_ref=pick(src), dst_ref=slot(land), send_sem=send_sems.at[a * len(steps) + k],
                recv_sem=recv_sems.at[a * len(steps) + k], device_id=to, device_id_type=MESH))
    return copies


def _split_start(srcs, land_shapes, plan, per, *, name):
    n = len(srcs)

    def body(*refs):
        send_sems, recv_sems = refs[2 * n], refs[2 * n + 1]
        for cp in _split_copies(refs[:n], refs[n:2 * n], send_sems, recv_sems, plan):
            cp.start()
        token = refs[-1]
        token[...] = jnp.zeros_like(token)

    lands = [_in_hbm(lax.empty(s.shape, s.dtype)) for s in land_shapes]
    outs = pl.pallas_call(
        body, name=name,
        out_shape=(pltpu.SemaphoreType.DMA((n * per,)), pltpu.SemaphoreType.DMA((n * per,)),
                   *[pltpu.HBM(s.shape, s.dtype) for s in srcs], *[pltpu.HBM(s.shape, s.dtype) for s in land_shapes],
                   jax.ShapeDtypeStruct((8, LANE), F32)),
        in_specs=[_HBM] * (2 * n),
        out_specs=(_SEM, _SEM, *[_HBM] * (2 * n), pl.BlockSpec(memory_space=pltpu.VMEM)),
        input_output_aliases={i: 2 + i for i in range(2 * n)},
        compiler_params=pltpu.CompilerParams(has_side_effects=_EFFECT),
    )(*[_in_hbm(s) for s in srcs], *lands)
    return outs[0], outs[1], list(outs[2:2 + n]), list(outs[2 + n:2 + 2 * n]), outs[-1]


def _split_wait(send_sems, recv_sems, srcs, lands, after, plan, *, name):
    n = len(srcs)

    def body(*refs):
        for cp in _split_copies(refs[:n], refs[n:2 * n], refs[2 * n], refs[2 * n + 1], plan):
            cp.wait_send()
            cp.wait_recv()

    outs = pl.pallas_call(
        body, name=name,
        out_shape=(*[pltpu.HBM(s.shape, s.dtype) for s in srcs], *[pltpu.HBM(s.shape, s.dtype) for s in lands]),
        in_specs=[_HBM] * (2 * n) + [_SEM, _SEM, pl.BlockSpec(memory_space=pl.ANY)],
        out_specs=tuple([_HBM] * (2 * n)),
        input_output_aliases={i: i for i in range(2 * n)},
        compiler_params=pltpu.CompilerParams(has_side_effects=_EFFECT),
    )(*srcs, *lands, send_sems, recv_sems, after)
    return list(outs[:n]), list(outs[n:])


def _gather_plan(x, y, c):
    slot = lambda land: land.at[_dev_index((x, y, c))]
    whole = lambda src: src
    return [(whole, slot, (x, y, 1 - c))] + [(whole, slot, (px, py, c)) for px, py in _other_chips(x, y)]


def _exchange_plan(x, y, c):
    return [(lambda src, k=2 * px + py: src.at[k], lambda land, j=j: land.at[j], (px, py, c))
            for j, (px, py) in enumerate(_other_chips(x, y))]


def _gather_forward(lands, *, name):
    n = len(lands)

    def body(*refs):
        lands_in, outs = refs[:n], refs[n:2 * n]
        send_sems, recv_sems = refs[2 * n:]
        x, y, c = _place()
        copies = []
        for a in range(n):
            for j, (px, py) in enumerate(_other_chips(x, y)):
                copies.append((pltpu.make_async_remote_copy(
                    src_ref=lands_in[a].at[_dev_index((px, py, c))], dst_ref=outs[a].at[_dev_index((px, py, c))],
                    send_sem=send_sems.at[3 * a + j], recv_sem=recv_sems.at[3 * a + j],
                    device_id=(x, y, 1 - c), device_id_type=MESH), a, j, (px, py)))
        for cp, _, _, _ in copies:
            cp.start()
        for cp, a, j, (px, py) in copies:
            cp.wait_send()
            pltpu.make_async_remote_copy(
                src_ref=lands_in[a].at[_dev_index((px, py, 1 - c))], dst_ref=outs[a].at[_dev_index((px, py, 1 - c))],
                send_sem=send_sems.at[3 * a + j], recv_sem=recv_sems.at[3 * a + j],
                device_id=(x, y, 1 - c), device_id_type=MESH).wait_recv()

    spec = pl.BlockSpec(memory_space=pl.ANY)
    return list(pl.pallas_call(
        body, name=name, in_specs=[spec] * n, out_specs=[spec] * n,
        out_shape=[jax.ShapeDtypeStruct(a.shape, a.dtype) for a in lands],
        input_output_aliases={a: a for a in range(n)},
        scratch_shapes=[pltpu.SemaphoreType.DMA((3 * n,)), pltpu.SemaphoreType.DMA((3 * n,))],
    )(*lands))


def _flat2(a, lead):
    return a.reshape(a.shape[:lead] + (-1, a.shape[-1]))


def _pair_sum(g, recv, c_idx, *, name):
    _, nchip, r, w = g.shape
    tm = _tile(r, 256) if r % 8 == 0 else r

    def body(c_ref, g_ref, r_ref, o_ref):
        o_ref[...] = (g_ref[...].astype(F32) + r_ref[...].astype(F32)).astype(o_ref.dtype)

    return pl.pallas_call(
        body, name=name,
        grid_spec=pltpu.PrefetchScalarGridSpec(
            num_scalar_prefetch=1, grid=(nchip, r // tm),
            in_specs=[pl.BlockSpec((None, None, tm, w), lambda k, i, c_ref: (c_ref[0], k, i, 0)),
                      pl.BlockSpec((None, tm, w), lambda k, i, c_ref: (k, i, 0))],
            out_specs=pl.BlockSpec((None, tm, w), lambda k, i, c_ref: (k, i, 0))),
        out_shape=jax.ShapeDtypeStruct((nchip, r, w), BF16),
        compiler_params=_params(("parallel", "parallel")),
    )(c_idx, g, recv)


def _chip_sum(s1, recv, chip_idx, *, name):
    _, r, w = s1.shape
    tm = _tile(r, 256) if r % 8 == 0 else r

    def body(k_ref, s_ref, r_ref, o_ref):
        acc = s_ref[...].astype(F32)
        for j in range(3):
            acc = acc + r_ref[j].astype(F32)
        o_ref[...] = acc

    return pl.pallas_call(
        body, name=name,
        grid_spec=pltpu.PrefetchScalarGridSpec(
            num_scalar_prefetch=1, grid=(r // tm,),
            in_specs=[pl.BlockSpec((None, tm, w), lambda i, k_ref: (k_ref[0], i, 0)),
                      pl.BlockSpec((3, tm, w), lambda i, k_ref: (0, i, 0))],
            out_specs=pl.BlockSpec((tm, w), lambda i, k_ref: (i, 0))),
        out_shape=jax.ShapeDtypeStruct((r, w), F32),
        compiler_params=_params(("parallel",)),
    )(chip_idx, s1, recv)


def _adam_math(w, g, m, v):
    m = ADAM_B1 * m + (1.0 - ADAM_B1) * g
    v = ADAM_B2 * v + (1.0 - ADAM_B2) * (g * g)
    m_hat = m / (1.0 - ADAM_B1 ** ADAM_STEP)
    v_hat = v / (1.0 - ADAM_B2 ** ADAM_STEP)
    delta = -ADAM_LR * (m_hat / (jnp.sqrt(v_hat) + ADAM_EPS) + ADAM_WD * w)
    return delta, m, v


def _adamw(w, mine, other, c_idx, m, v, *, name):
    r, cw = w.shape
    hr = r // 2
    tm = _row_tile(hr, 9 * cw * 4)

    def body(c_ref, w_ref, a_ref, b_ref, m_ref, v_ref, g_ref, d_ref, nm_ref, nv_ref):
        g = jnp.where(pl.program_id(0) == c_ref[0], a_ref[...], b_ref[...])
        g_ref[...] = g
        d_ref[...], nm_ref[...], nv_ref[...] = _adam_math(w_ref[...], g, m_ref[...], v_ref[...])

    full = pl.BlockSpec((None, tm, cw), lambda h, i, c_ref: (h, i, 0))
    half = pl.BlockSpec((tm, cw), lambda h, i, c_ref: (i, 0))
    outs = pl.pallas_call(
        body, name=name,
        grid_spec=pltpu.PrefetchScalarGridSpec(
            num_scalar_prefetch=1, grid=(2, hr // tm),
            in_specs=[full, half, half, full, full], out_specs=[full] * 4),
        out_shape=[jax.ShapeDtypeStruct((2, hr, cw), F32)] * 4,
        compiler_params=_params(("parallel", "parallel")),
    )(c_idx, w.reshape(2, hr, cw), mine, other, m.reshape(2, hr, cw), v.reshape(2, hr, cw))
    return [o.reshape(r, cw) for o in outs]


def _adamw_ada(cact_t, dada, w, m, v):
    r, cw = w.shape
    nb = cact_t.shape[1]
    tm = _tile(r, 256)
    tn = _tile(cw, 1024)

    def body(a_ref, d_ref, w_ref, m_ref, v_ref, g_ref, dl_ref, nm_ref, nv_ref):
        a = a_ref[...]
        d = d_ref[...]
        g = a[:, 0:1] * d[0:1, :]
        for b in range(1, nb):
            g = g + a[:, b:b + 1] * d[b:b + 1, :]
        g_ref[...] = g
        dl_ref[...], nm_ref[...], nv_ref[...] = _adam_math(w_ref[...], g, m_ref[...], v_ref[...])

    blk = pl.BlockSpec((tm, tn), lambda i, j: (i, j))
    return pl.pallas_call(
        body, name="adamw_ada", grid=(r // tm, cw // tn),
        in_specs=[pl.BlockSpec((tm, nb), lambda i, j: (i, 0)), pl.BlockSpec((nb, tn), lambda i, j: (0, j)), blk, blk, blk],
        out_specs=[blk] * 4, out_shape=[jax.ShapeDtypeStruct((r, cw), F32)] * 4,
        compiler_params=_params(("parallel", "parallel")),
    )(cact_t, dada, w, m, v)


def _adamw_vec(parts, w, m, v):
    n = w.shape[1]

    def body(p_ref, w_ref, m_ref, v_ref, g_ref, d_ref, nm_ref, nv_ref):
        p = p_ref[...]
        g = p[0:1, :]
        for b in range(1, N_DEV):
            g = g + p[b:b + 1, :]
        g_ref[...] = g
        d_ref[...], nm_ref[...], nv_ref[...] = _adam_math(w_ref[...], g, m_ref[...], v_ref[...])

    return pl.pallas_call(
        body, name="adamw_vec", out_shape=[jax.ShapeDtypeStruct((1, n), F32)] * 4,
        compiler_params=pltpu.CompilerParams(vmem_limit_bytes=VMEM_LIMIT),
    )(parts, w, m, v)


def _w_in_segments(kpe0, d_in, cs):
    segs = []
    for k in range(4):
        lo, hi = k * cs, (k + 1) * cs
        for a, b, shift in ((0, kpe0, 0), (kpe0, kpe0 + ROPE, d_in - ROPE - kpe0), (kpe0 + ROPE, d_in, -ROPE)):
            a, b = max(lo, a), min(hi, b)
            if a < b:
                segs.append((k, a - lo, a + shift, b - a))
    return segs


def _w_in_layout(g8, kpe0):
    _, hr, cs = g8.shape
    rows, d_in = 2 * hr, 4 * cs
    segs = _w_in_segments(kpe0, d_in, cs)
    tm = _tile(rows, 256)

    def body(g_ref, o_ref):
        for k, src, dst, w in segs:
            o_ref[:, dst:dst + w] = g_ref[k, :, src:src + w]
        o_ref[:, d_in:] = jnp.zeros((tm, ROPE), o_ref.dtype)

    return pl.pallas_call(
        body, name="w_in_layout", grid=(rows // tm,),
        in_specs=[pl.BlockSpec((4, tm, cs), lambda i: (0, i, 0))], out_specs=_rows(tm, d_in + ROPE),
        out_shape=jax.ShapeDtypeStruct((rows, d_in + ROPE), g8.dtype), compiler_params=_params(("parallel",)),
    )(g8.reshape(4, rows, cs))


def _w_in_grad_pieces(g, kpe0):
    rows, d_in_p = g.shape
    d_in = d_in_p - ROPE
    cs = d_in // 4
    segs = _w_in_segments(kpe0, d_in, cs)
    hr = rows // 2
    tm = _tile(hr, 256)
    per_half = hr // tm

    def body(g_ref, o_ref):
        for k, src, dst, w in segs:
            o_ref[k, :, src:src + w] = g_ref[:, dst:dst + w]

    return pl.pallas_call(
        body, name="w_in_grad_pieces", grid=(rows // tm,),
        in_specs=[_rows(tm, d_in_p)],
        out_specs=pl.BlockSpec((None, 4, tm, cs), lambda i: (i // per_half, 0, i % per_half, 0)),
        out_shape=jax.ShapeDtypeStruct((2, 4, hr, cs), g.dtype), compiler_params=_params(("parallel",)),
    )(g)


def _cols_from_chips(g8, rows):
    cs = g8.shape[-1]
    return g8.reshape(4, rows, cs).transpose(1, 0, 2).reshape(rows, 4 * cs)


def _cols_to_pieces(g):
    rows, c4 = g.shape
    return g.reshape(2, rows // 2, 4, c4 // 4).transpose(0, 2, 1, 3)


def _rows_to_pieces(g):
    r4, cols = g.shape
    return g.reshape(4, 2, r4 // 8, cols).transpose(1, 0, 2, 3)


def _pad_cols(a, w):
    return jnp.pad(a, ((0, 0), (0, w - a.shape[1])))


def kernel(x, c, positions, w_ada, b_ada, g_norm1, g_norm2, w_in, g_q_latent, g_kv_latent, w_uq, w_ukv, g_q_head, g_k_head, w_proj_mla, w_proj_sb, w_out, w_ffn_in, w_ffn_out, loss_target, m_w_ada, m_b_ada, m_g_norm1, m_g_norm2, m_w_in, m_g_q_latent, m_g_kv_latent, m_w_uq, m_w_ukv, m_g_q_head, m_g_k_head, m_w_proj_mla, m_w_proj_sb, m_w_out, m_w_ffn_in, m_w_ffn_out, v_w_ada, v_b_ada, v_g_norm1, v_g_norm2, v_w_in, v_g_q_latent, v_g_kv_latent, v_w_uq, v_w_ukv, v_g_q_head, v_g_k_head, v_w_proj_mla, v_w_proj_sb, v_w_out, v_w_ffn_in, v_w_ffn_out):
    xi, yi, ci = _place()
    chip = 2 * xi + yi
    dev = 2 * chip + ci
    c_idx = jnp.reshape(ci, (1,)).astype(jnp.int32)
    chip_idx = jnp.reshape(chip, (1,)).astype(jnp.int32)

    x = x[0]
    tgt = loss_target[0]
    S, D = x.shape
    ql = g_q_latent.shape[1]
    assert g_kv_latent.shape[1] == ql
    mlaw = w_proj_mla.shape[1]
    nh = mlaw // HEAD
    sbw = w_proj_sb.shape[1]
    assert sbw == mlaw
    dff = w_ffn_out.shape[1] * 4
    d_in = 2 * ql + ROPE + 3 * sbw + 2 * D
    d_in_p = d_in + ROPE
    q_col = (2 * ql) // HEAD
    k_col = q_col + nh
    v_col = k_col + nh
    gla_col = (2 * ql + 3 * sbw) // D
    glb_col = gla_col + 1
    kpe_col = (d_in - ROPE) // LANE
    assert (2 * ql + 3 * sbw) % D == 0 and (d_in - ROPE) % LANE == 0

    mats = {"w_in": w_in[0], "w_uq": w_uq[0], "w_ukv": w_ukv[0], "w_proj_mla": w_proj_mla[0],
            "w_proj_sb": w_proj_sb[0], "w_out": w_out[0], "w_ffn_in": w_ffn_in[0], "w_ffn_out": w_ffn_out[0]}
    names = list(mats)
    row_sharded = {"w_out", "w_ffn_out"}

    c_all = _gather_blocks([jnp.broadcast_to(c, (8, D))], name="gather_cond", in_vmem=True)[0][:, 0, :]
    n_ada = w_ada.shape[2]
    b_shard = lax.dynamic_slice_in_dim(b_ada, chip * n_ada, n_ada, axis=1)
    ada_shard = _mm(c_all, w_ada[0], name="ada_proj", a_fn=jax.nn.silu, bias=b_shard)
    ada_all = _gather_blocks([ada_shard], name="gather_ada", in_vmem=True)[0]
    ada_rows = lax.dynamic_index_in_dim(ada_all, dev, axis=1, keepdims=False)
    ada = ada_rows[0::2].reshape(1, 4 * n_ada)
    SH1, SC1, GT1, SH2, SC2, GT2 = range(6)

    def after(dep, a):
        return a + (dep.reshape(-1)[0:1].reshape((1,) * a.ndim) * 0).astype(a.dtype)

    def fill_own(g8, own):
        return lax.dynamic_update_index_in_dim(g8, own, dev, 0)

    halves = []
    for nm in names:
        w = mats[nm]
        hr = w.shape[0] // 2
        halves.append(lax.dynamic_slice_in_dim(w, ci * hr, hr, axis=0).astype(BF16))
    half_of = dict(zip(names, halves))
    early = ["w_in", "w_uq", "w_ukv"]
    late = ["w_proj_mla", "w_proj_sb", "w_out", "w_ffn_in", "w_ffn_out"]
    early_halves = [half_of[nm] for nm in early]
    early_halves[0] = after(ada, early_halves[0])
    early_got = _gather_blocks(early_halves, name="gather_weights", in_vmem=False)
    gathered = {nm: fill_own(g8, own) for nm, g8, own in zip(early, early_got, early_halves)}
    late_halves = [half_of[nm] for nm in late]
    late_halves[0] = after(gathered[early[1]], late_halves[0])
    late_send, late_recv, late_srcs, late_lands, late_token = _split_start(
        late_halves, [jax.ShapeDtypeStruct((N_DEV,) + h.shape, h.dtype) for h in late_halves], _gather_plan, 4,
        name="gather_late_start")
    ada = ada + late_token[0:1, 0:1]

    def full_cols(nm):
        return _cols_from_chips(gathered[nm], mats[nm].shape[0])

    kpe0 = 2 * ql
    w_in_p = _w_in_layout(gathered["w_in"], kpe0)
    w_uq_p = jnp.pad(full_cols("w_uq").reshape(ql, nh, QK_DIM), ((0, 0), (0, 0), (0, HEAD_PAD - QK_DIM))
                     ).reshape(ql, nh * HEAD_PAD)
    w_ukv4 = full_cols("w_ukv").reshape(ql, nh, 2 * HEAD)
    w_ukv_p = jnp.concatenate([w_ukv4[:, :, :HEAD].reshape(ql, mlaw), w_ukv4[:, :, HEAD:].reshape(ql, mlaw)], axis=1)

    half = ROPE // 2
    freqs = ROPE_THETA ** (-jnp.arange(half, dtype=F32) / half)
    ang = positions[0].astype(F32)[:, None] * freqs
    cos, sin = jnp.cos(ang), jnp.sin(ang)
    one = jnp.ones((S, NOPE), F32)
    zero = jnp.zeros((S, NOPE), F32)
    zh = jnp.zeros((S, half), F32)
    tabs = (jnp.concatenate([one, cos, cos, one[:, :HEAD_PAD - QK_DIM]], axis=1),
            jnp.concatenate([zero, zh, sin, zero[:, :HEAD_PAD - QK_DIM]], axis=1),
            jnp.concatenate([zero, -sin, zh, zero[:, :HEAD_PAD - QK_DIM]], axis=1))
    g_qh_p = _pad_cols(g_q_head, HEAD_PAD)
    g_kh_p = _pad_cols(g_k_head, HEAD_PAD)

    h1 = _rmsmod(x, g_norm1, ada, SC1, SH1, name="rmsmod1")
    proj = _mm(h1, w_in_p, name="mm_proj", tn=640)
    cqn, ckvn = _latent_norm(proj, g_q_latent, g_kv_latent, ql)
    q0 = _mm(cqn, w_uq_p, name="mm_q_up")
    kv0 = _mm(ckvn, w_ukv_p, name="mm_kv_up")
    q = _q_prep(q0, g_qh_p, tabs, nh)
    k = _k_prep(kv0, proj, kpe_col, g_kh_p, tabs, nh)
    y_a, lse = _mla_fwd(q, k, kv0, nh)
    y_b, sb_runs = _sb_fwd(proj, q_col, k_col, v_col, nh)
    late_srcs, late_lands = _split_wait(late_send, late_recv, late_srcs, late_lands, y_b, _gather_plan,
                                        name="gather_late_wait")
    late_got = _gather_forward(late_lands, name="gather_late_forward")
    gathered.update({nm: fill_own(g8, own) for nm, g8, own in zip(late, late_got, late_srcs)})
    w_pm = full_cols("w_proj_mla")
    w_ps = full_cols("w_proj_sb")
    w_o = gathered["w_out"].reshape(D, D)
    w_fi = full_cols("w_ffn_in")
    w_fo = gathered["w_ffn_out"].reshape(dff, D)
    pa = _mm(y_a, w_pm, name="mm_proj_mla", out_dtype=BF16)
    pb = _mm(y_b, w_ps, name="mm_proj_sb", out_dtype=BF16)
    merged = _gate_merge(pa, pb, proj, gla_col, glb_col)
    o = _mm(merged, w_o, name="mm_out")
    x2, h2 = _resid_rmsmod(x, o, g_norm2, ada, GT1, SC2, SH2)
    ff = _mm(h2, w_fi, name="mm_ffn_in", out_dtype=BF16)
    act = _swiglu(ff, dff)
    f = _mm(act, w_fo, name="mm_ffn_out")
    dy, df, red_l, loss_p = _loss_head(x2, f, tgt, ada, GT2)

    dact = _mm(df, w_fo, name="mm_d_act", tb=True, out_dtype=BF16)
    def pc(kind):
        if kind == "cols":
            return kind
        return kind if (D // 4) % LANE == 0 and (dff // 4) % LANE == 0 else None

    gw_fo = _mm(act, df, name="mm_gw_ffn_out", ta=True, out_dtype=BF16, pieces=pc("rows"))
    dff_ = _swiglu_bwd(dact, ff, dff)
    dh2 = _mm(dff_, w_fi, name="mm_d_h2", tb=True)
    gw_fi = _mm(h2, dff_, name="mm_gw_ffn_in", ta=True, out_dtype=BF16, pieces=pc("cols"))

    def pair_sums(nms, grads, tag):
        pcs = [g if g.ndim == 4 else (_rows_to_pieces if nm in row_sharded else _cols_to_pieces)(g)
               for nm, g in zip(nms, grads)]
        got = _sibling_swap(pcs, name="rs_sibling_swap_" + tag)
        return [_pair_sum(p, r, c_idx, name="rs_pair_sum_" + nm) for p, r, nm in zip(pcs, got, nms)]

    ffn = ["w_ffn_in", "w_ffn_out"]
    ffn_pair = pair_sums(ffn, [gw_fi, gw_fo], "ffn")
    ffn_send, ffn_recv, ffn_pair, ffn_lands, ffn_token = _split_start(
        ffn_pair, [jax.ShapeDtypeStruct((3,) + p.shape[1:], p.dtype) for p in ffn_pair], _exchange_plan, 3,
        name="rs_exchange_ffn_start")
    ada = ada + ffn_token[0:1, 0:1]
    dx2, do, red_2 = _rmsmod2_bwd(dh2, x2, dy, o, g_norm2, ada, SC2, GT1)
    dmerged = _mm(do, w_o, name="mm_d_merged", tb=True, out_dtype=BF16)
    gw_o = _mm(merged, do, name="mm_gw_out", ta=True, out_dtype=BF16, pieces=pc("rows"))
    dpa, dpb, dgla, dglb = _gate_bwd(dmerged, pa, pb, proj, gla_col, glb_col)
    dya = _mm(dpa, w_pm, name="mm_d_ya", tb=True, out_dtype=BF16)
    gw_pm = _mm(y_a, dpa, name="mm_gw_proj_mla", ta=True, out_dtype=BF16, pieces=pc("cols"))
    dyb = _mm(dpb, w_ps, name="mm_d_yb", tb=True, out_dtype=BF16)
    gw_ps = _mm(y_b, dpb, name="mm_gw_proj_sb", ta=True, out_dtype=BF16, pieces=pc("cols"))
    mid = ["w_proj_mla", "w_proj_sb", "w_out"]
    mid_pair = pair_sums(mid, [gw_pm, gw_ps, gw_o], "mid")
    mid_send, mid_recv, mid_pair, mid_lands, mid_token = _split_start(
        mid_pair, [jax.ShapeDtypeStruct((3,) + p.shape[1:], p.dtype) for p in mid_pair], _exchange_plan, 3,
        name="rs_exchange_mid_start")
    lse = lse + mid_token[0:1, 0:1]
    dq, dk, dv = _mla_bwd(q, k, kv0, y_a, dya, lse, nh)
    dq_sb, dk_sb, dv_sb = _sb_bwd(proj, q_col, k_col, v_col, dyb, sb_runs, nh)
    dq0, red_qh = _q_prep_bwd(dq, q0, g_qh_p, tabs, nh)
    dkv0, dkpe, red_kh = _k_prep_bwd(dk, dv, kv0, proj, kpe_col, g_kh_p, tabs, nh)
    dcqn = _mm(dq0, w_uq_p, name="mm_d_cqn", tb=True, out_dtype=BF16)
    gw_uq_p = _mm(cqn, dq0, name="mm_gw_uq", ta=True, out_dtype=BF16)
    dckvn = _mm(dkv0, w_ukv_p, name="mm_d_ckvn", tb=True, out_dtype=BF16)
    gw_ukv_p = _mm(ckvn, dkv0, name="mm_gw_ukv", ta=True, out_dtype=BF16)
    dcq, dckv, red_lat = _latent_norm_bwd(dcqn, dckvn, proj, g_q_latent, g_kv_latent, ql)
    dproj = jnp.concatenate([dcq, dckv, dq_sb.astype(BF16), dk_sb.astype(BF16), dv_sb.astype(BF16),
                             dgla, dglb, dkpe], axis=1)
    gw_in_p = _mm(h1, dproj, name="mm_gw_in", ta=True, out_dtype=BF16, tn=640)

    gw_in = _w_in_grad_pieces(gw_in_p, kpe0)
    gw_uq = gw_uq_p.reshape(ql, nh, HEAD_PAD)[:, :, :QK_DIM].reshape(ql, nh * QK_DIM)
    gw_ukv = jnp.concatenate([gw_ukv_p[:, :mlaw].reshape(ql, nh, HEAD), gw_ukv_p[:, mlaw:].reshape(ql, nh, HEAD)],
                             axis=2).reshape(ql, 2 * mlaw)
    last = ["w_in", "w_uq", "w_ukv"]
    assert last + mid + ffn == names

    last_pair = pair_sums(last, [gw_in, gw_uq, gw_ukv], "last")
    last_send, last_recv, last_pair, last_lands, last_token = _split_start(
        last_pair, [jax.ShapeDtypeStruct((3,) + p.shape[1:], p.dtype) for p in last_pair], _exchange_plan, 3,
        name="rs_exchange_last_start")
    ada = ada + last_token[0:1, 0:1]
    dh1 = _mm(dproj, w_in_p, name="mm_d_h1", tb=True, bias=jnp.zeros((1, D), F32) + last_token[0:1, 0:1])
    grad_x, red_1 = _rmsmod1_bwd(dh1, x, dx2, g_norm1, ada, SC1)
    last_pair, last_chips = _split_wait(last_send, last_recv, last_pair, last_lands, grad_x, _exchange_plan,
                                        name="rs_exchange_last_wait")
    mid_pair, mid_chips = _split_wait(mid_send, mid_recv, mid_pair, mid_lands, grad_x, _exchange_plan,
                                      name="rs_exchange_mid_wait")
    ffn_pair, ffn_chips = _split_wait(ffn_send, ffn_recv, ffn_pair, ffn_lands, grad_x, _exchange_plan,
                                      name="rs_exchange_ffn_wait")
    reduced = [_chip_sum(s, r, chip_idx, name="rs_chip_sum_" + nm)
               for s, r, nm in zip(last_pair + mid_pair + ffn_pair, last_chips + mid_chips + ffn_chips, names)]
    from_sibling2 = _sibling_swap(reduced, name="rs_sibling_send", whole=True)

    vec_names = ["b_ada", "g_norm1", "g_norm2", "g_q_latent", "g_kv_latent", "g_q_head", "g_k_head"]
    vec_w = dict(b_ada=b_ada, g_norm1=g_norm1, g_norm2=g_norm2, g_q_latent=g_q_latent, g_kv_latent=g_kv_latent,
                 g_q_head=g_q_head, g_k_head=g_k_head)
    vec_m = dict(b_ada=m_b_ada, g_norm1=m_g_norm1, g_norm2=m_g_norm2, g_q_latent=m_g_q_latent,
                 g_kv_latent=m_g_kv_latent, g_q_head=m_g_q_head, g_k_head=m_g_k_head)
    vec_v = dict(b_ada=v_b_ada, g_norm1=v_g_norm1, g_norm2=v_g_norm2, g_q_latent=v_g_q_latent,
                 g_kv_latent=v_g_kv_latent, g_q_head=v_g_q_head, g_k_head=v_g_k_head)
    d_ada = jnp.concatenate([red_1[0:1], red_1[1:2], red_2[3:4], red_2[0:1], red_2[1:2], red_l[0:1]], axis=1)
    vec_parts = dict(b_ada=d_ada, g_norm1=red_1[2:3], g_norm2=red_2[2:3], g_q_latent=red_lat[0:1],
                     g_kv_latent=red_lat[1:2], g_q_head=red_qh[0:1], g_k_head=red_kh[0:1])
    widths = [-(-vec_w[nm].shape[1] // LANE) * LANE for nm in vec_names]
    offs = [sum(widths[:i]) for i in range(len(widths))]
    pack = lambda d: jnp.concatenate([_pad_cols(d[nm][:, :vec_w[nm].shape[1]], wd) for nm, wd in zip(vec_names, widths)], axis=1)
    nvec = sum(widths) + LANE
    no_loss = jnp.zeros((1, LANE), F32)
    parts = jnp.concatenate([pack(vec_parts), loss_p[0:1, :]], axis=1)
    parts_all = _gather_blocks([jnp.broadcast_to(parts, (8, nvec))], name="gather_vec_grads",
                               in_vmem=True)[0][:, 0, :]
    gvec, dvec, nmvec, nvvec = _adamw_vec(parts_all, *[jnp.concatenate([pack(d), no_loss], axis=1)
                                                       for d in (vec_w, vec_m, vec_v)])
    loss = gvec[0, nvec - LANE]
    unpack = lambda a: {nm: a[:, o_:o_ + vec_w[nm].shape[1]] for nm, o_ in zip(vec_names, offs)}
    gvec, dvec, nmvec, nvvec = unpack(gvec), unpack(dvec), unpack(nmvec), unpack(nvvec)

    dada_all = lax.dynamic_slice_in_dim(parts_all[:, :6 * D], chip * n_ada, n_ada, axis=1)
    cact_t = jax.nn.silu(c_all).T
    g_ada, d_ada_w, nm_ada, nv_ada = _adamw_ada(cact_t, dada_all, w_ada[0], m_w_ada[0], v_w_ada[0])

    ms = dict(w_in=m_w_in, w_uq=m_w_uq, w_ukv=m_w_ukv, w_proj_mla=m_w_proj_mla, w_proj_sb=m_w_proj_sb,
              w_out=m_w_out, w_ffn_in=m_w_ffn_in, w_ffn_out=m_w_ffn_out)
    vs = dict(w_in=v_w_in, w_uq=v_w_uq, w_ukv=v_w_ukv, w_proj_mla=v_w_proj_mla, w_proj_sb=v_w_proj_sb,
              w_out=v_w_out, w_ffn_in=v_w_ffn_in, w_ffn_out=v_w_ffn_out)
    G, DL, NM, NV = {}, {}, {}, {}
    for nm, mine, other in zip(names, reduced, from_sibling2):
        g_, d_, m_, v_ = _adamw(mats[nm], mine, other, c_idx, ms[nm][0], vs[nm][0], name="adamw_" + nm)
        G[nm], DL[nm], NM[nm], NV[nm] = g_[None], d_[None], m_[None], v_[None]
    G["w_ada"], DL["w_ada"], NM["w_ada"], NV["w_ada"] = g_ada[None], d_ada_w[None], nm_ada[None], nv_ada[None]
    for nm in vec_names:
        G[nm], DL[nm], NM[nm], NV[nm] = gvec[nm], dvec[nm], nmvec[nm], nvvec[nm]

    order = ["w_ada", "b_ada", "g_norm1", "g_norm2", "w_in", "g_q_latent", "g_kv_latent", "w_uq", "w_ukv",
             "g_q_head", "g_k_head", "w_proj_mla", "w_proj_sb", "w_out", "w_ffn_in", "w_ffn_out"]
    return (loss, grad_x[None], *[G[n] for n in order], *[DL[n] for n in order],
            *[NM[n] for n in order], *[NV[n] for n in order])
```

```python
import functools
import math

import jax
import jax.numpy as jnp
from jax import lax
from jax.experimental import pallas as pl
from jax.experimental.pallas import tpu as pltpu

F32 = jnp.float32
BF16 = jnp.bfloat16
MESH = pl.DeviceIdType.MESH

EPS = 1e-6
ROPE_THETA = 10000.0
NOPE = 128
ROPE = 64
QK_DIM = NOPE + ROPE
HEAD_PAD = 256
HEAD = 128
N_DEV = 8
LANE = 128
VMEM_LIMIT = 48 * 1024 * 1024

ADAM_LR = 0.001
ADAM_B1 = 0.9
ADAM_B2 = 0.999
ADAM_EPS = 1e-08
ADAM_WD = 0.01
ADAM_STEP = 10


def _tile(n, target):
    if n <= target:
        return n
    t = (target // LANE) * LANE
    while t >= LANE:
        if n % t == 0:
            return t
        t -= LANE
    return n


def _row_tile(rows, row_bytes, budget=24 * 1024 * 1024):
    cap = max(8, budget // (2 * row_bytes))
    best = None
    for t in range(8, min(rows, cap) + 1, 8):
        if rows % t == 0:
            best = t
    return best if best is not None else rows


def _params(sem):
    return pltpu.CompilerParams(dimension_semantics=sem, vmem_limit_bytes=VMEM_LIMIT)


def _rows(tm, w, col=0):
    return pl.BlockSpec((tm, w), lambda i: (i, col))


def _vec(w, col=0, rows=1):
    return pl.BlockSpec((rows, w), lambda i: (0, col))


MM_VMEM_BUDGET = 36 * 1024 * 1024


def _mm(a, b, *, name, ta=False, tb=False, out_dtype=F32, a_fn=None, bias=None, tm=1024, tn=1024, pieces=None,
        col_perm=None, fused=None):
    M = a.shape[1] if ta else a.shape[0]
    K = a.shape[0] if ta else a.shape[1]
    N = b.shape[0] if tb else b.shape[1]
    assert K == (b.shape[1] if tb else b.shape[0]), (a.shape, b.shape, ta, tb)
    if pieces == "cols":
        tm, tn = _tile(M // 2, tm), _tile(N // 4, tn)
        assert (M // 2) % tm == 0 and (N // 4) % tn == 0
    elif pieces == "rows":
        tm, tn = M // 4, _tile(N, tn)
    else:
        tm, tn = _tile(M, tm), _tile(N, tn)
    sa, sb, so = a.dtype.itemsize, b.dtype.itemsize, jnp.dtype(out_dtype).itemsize

    def fits(tk):
        return 2 * tk * (tm * sa + tn * sb) + tm * tn * (2 * so + 4) <= MM_VMEM_BUDGET

    tk = K
    while not fits(tk):
        smaller = _tile(K, tk - LANE)
        if smaller >= tk:
            break
        tk = smaller
    nk = K // tk
    dn = (((0 if ta else 1,), (1 if tb else 0,)), ((), ()))
    b_outer = nk == 1 and a.size * sa * (N // tn) < b.size * sb * (M // tm)

    n_extra = len(fused[1]) if fused else 0
    n_out = len(fused[2]) if fused else 1

    def body(*refs):
        a_ref, b_ref = refs[:2]
        bias_ref = refs[2] if bias is not None else None
        first = 3 if bias is not None else 2
        extra_refs = refs[first:first + n_extra]
        out_refs = refs[first + n_extra:first + n_extra + n_out]
        o_ref = out_refs[0]
        av = a_ref[...]
        if a_fn is not None:
            av = a_fn(av.astype(F32))
        part = lax.dot_general(av.astype(BF16), b_ref[...].astype(BF16), dn, preferred_element_type=F32)

        def finish(r):
            if bias is not None:
                r = r + bias_ref[...]
            if fused:
                for ref, tile in zip(out_refs, fused[0](r, *[e[...] for e in extra_refs])):
                    ref[...] = tile.astype(ref.dtype)
            elif pieces == "rows":
                o_ref[0] = r[:tm // 2].astype(o_ref.dtype)
                o_ref[1] = r[tm // 2:].astype(o_ref.dtype)
            else:
                o_ref[...] = r.astype(o_ref.dtype)

        if nk == 1:
            finish(part)
        else:
            acc_ref = refs[-1]
            k = pl.program_id(2)

            @pl.when(k == 0)
            def _():
                acc_ref[...] = part

            @pl.when(k > 0)
            def _():
                acc_ref[...] += part

            @pl.when(k == nk - 1)
            def _():
                finish(acc_ref[...])

    def ij(g0, g1):
        return (g1, g0) if b_outer else (g0, g1)

    def amap(g0, g1, k):
        i, _ = ij(g0, g1)
        return (k, i) if ta else (i, k)

    def bmap(g0, g1, k):
        _, j = ij(g0, g1)
        return (j, k) if tb else (k, j)

    in_specs = [pl.BlockSpec((tk, tm) if ta else (tm, tk), amap), pl.BlockSpec((tn, tk) if tb else (tk, tn), bmap)]
    args = [a, b]
    if bias is not None:
        in_specs.append(pl.BlockSpec((1, tn), lambda g0, g1, k: (0, ij(g0, g1)[1])))
        args.append(bias)
    grid = (N // tn, M // tm, nk) if b_outer else (M // tm, N // tn, nk)
    if pieces == "cols":
        ni, nj = M // 2 // tm, N // 4 // tn

        def omap(g0, g1, k):
            i, j = ij(g0, g1)
            j = col_perm(j) if col_perm else j
            return (i // ni, j // nj, i % ni, j % nj)

        out_spec = pl.BlockSpec((None, None, tm, tn), omap)
        out_shape = jax.ShapeDtypeStruct((2, 4, M // 2, N // 4), out_dtype)
    elif pieces == "rows":
        out_spec = pl.BlockSpec((2, None, tm // 2, tn), lambda g0, g1, k: (0, ij(g0, g1)[0], 0, ij(g0, g1)[1]))
        out_shape = jax.ShapeDtypeStruct((2, 4, tm // 2, N), out_dtype)
    else:
        out_spec = pl.BlockSpec((tm, tn), lambda g0, g1, k: ij(g0, g1))
        out_shape = jax.ShapeDtypeStruct((M, N), out_dtype)
    if fused:
        for arr, width in fused[1]:
            in_specs.append(pl.BlockSpec((tm, width), lambda g0, g1, k: ij(g0, g1)))
            args.append(arr)
        out_spec = [pl.BlockSpec((tm, width), lambda g0, g1, k: ij(g0, g1)) for _, width, _ in fused[2]]
        out_shape = [jax.ShapeDtypeStruct((M, cols), dt) for cols, _, dt in fused[2]]
    return pl.pallas_call(
        body, name=name, grid=grid, in_specs=in_specs, out_specs=out_spec, out_shape=out_shape,
        scratch_shapes=[pltpu.VMEM((tm, tn), F32)] if nk > 1 else [],
        compiler_params=_params(("parallel", "parallel", "arbitrary")),
    )(*args)


def _rms_rows(v):
    return lax.rsqrt(jnp.mean(v * v, axis=-1, keepdims=True) + EPS)


def _rmsmod(x, g, ada, sc_col, sh_col, *, name):
    S, D = x.shape
    tm = _tile(S, 256)

    def body(x_ref, g_ref, sc_ref, sh_ref, h_ref):
        xv = x_ref[...]
        h = (xv * _rms_rows(xv) * g_ref[...]) * (1.0 + sc_ref[...]) + sh_ref[...]
        h_ref[...] = h.astype(h_ref.dtype)

    return pl.pallas_call(
        body, name=name, grid=(S // tm,),
        in_specs=[_rows(tm, D), _vec(D), _vec(D, sc_col), _vec(D, sh_col)],
        out_specs=_rows(tm, D), out_shape=jax.ShapeDtypeStruct((S, D), BF16),
        compiler_params=_params(("parallel",)),
    )(x, g, ada, ada)


def _latent_norm(proj, g_q, g_kv, ql):
    S = proj.shape[0]
    tm = _tile(S, 512)

    def body(cq_ref, ckv_ref, gq_ref, gkv_ref, oq_ref, okv_ref):
        cq = cq_ref[...]
        oq_ref[...] = (cq * _rms_rows(cq) * gq_ref[...]).astype(BF16)
        ckv = ckv_ref[...]
        okv_ref[...] = (ckv * _rms_rows(ckv) * gkv_ref[...]).astype(BF16)

    return pl.pallas_call(
        body, name="latent_norm", grid=(S // tm,),
        in_specs=[_rows(tm, ql, 0), _rows(tm, ql, 1), _vec(ql), _vec(ql)],
        out_specs=[_rows(tm, ql), _rows(tm, ql)],
        out_shape=[jax.ShapeDtypeStruct((S, ql), BF16)] * 2,
        compiler_params=_params(("parallel",)),
    )(proj, proj, g_q, g_kv)


def _rope_fwd(y, c, s1, s2):
    return y * c + pltpu.roll(y, ROPE // 2, 1) * s1 + pltpu.roll(y, HEAD_PAD - ROPE // 2, 1) * s2


def _rope_bwd(d, c, s1, s2):
    return d * c + pltpu.roll(d * s1, HEAD_PAD - ROPE // 2, 1) + pltpu.roll(d * s2, ROPE // 2, 1)


def _head_rms(v):
    return lax.rsqrt(jnp.sum(v * v, axis=-1, keepdims=True) * (1.0 / QK_DIM) + EPS)


def _q_prep(q0, g_qh, tabs, nh):
    S = q0.shape[0]
    tm = _tile(S, 256)

    def body(q_ref, g_ref, c_ref, s1_ref, s2_ref, o_ref):
        c, s1, s2, g = c_ref[...], s1_ref[...], s2_ref[...], g_ref[...]
        for h in range(nh):
            sl = slice(h * HEAD_PAD, (h + 1) * HEAD_PAD)
            xs = q_ref[:, sl]
            o_ref[:, sl] = (_rope_fwd(xs * _head_rms(xs) * g, c, s1, s2) * (QK_DIM ** -0.5)).astype(BF16)

    w = nh * HEAD_PAD
    return pl.pallas_call(
        body, name="mla_q_prep", grid=(S // tm,),
        in_specs=[_rows(tm, w), _vec(HEAD_PAD)] + [_rows(tm, HEAD_PAD)] * 3,
        out_specs=_rows(tm, w), out_shape=jax.ShapeDtypeStruct((S, w), BF16),
        compiler_params=_params(("parallel",)),
    )(q0, g_qh, *tabs)


def _k_prep(kv0, proj, kpe_col, g_kh, tabs, nh):
    S = kv0.shape[0]
    tm = _tile(S, 256)

    def body(kv_ref, kpe_ref, g_ref, c_ref, s1_ref, s2_ref, o_ref):
        c, s1, s2, g = c_ref[...], s1_ref[...], s2_ref[...], g_ref[...]
        kpe = kpe_ref[...]
        for h in range(nh):
            k0 = jnp.concatenate([kv_ref[:, h * HEAD:(h + 1) * HEAD], kpe], axis=1)
            o_ref[:, h * HEAD_PAD:(h + 1) * HEAD_PAD] = _rope_fwd(k0 * _head_rms(k0) * g, c, s1, s2).astype(BF16)

    return pl.pallas_call(
        body, name="mla_k_prep", grid=(S // tm,),
        in_specs=[_rows(tm, nh * HEAD, 0), _rows(tm, LANE, kpe_col), _vec(HEAD_PAD)] + [_rows(tm, HEAD_PAD)] * 3,
        out_specs=_rows(tm, nh * HEAD_PAD), out_shape=jax.ShapeDtypeStruct((S, nh * HEAD_PAD), BF16),
        compiler_params=_params(("parallel",)),
    )(kv0, proj, g_kh, *tabs)


def _gate_merge(pa, pb, proj, gla_col, glb_col):
    S, D = pa.shape
    tm = _tile(S, 256)

    def body(pa_ref, pb_ref, ga_ref, gb_ref, o_ref):
        o_ref[...] = (jax.nn.sigmoid(ga_ref[...]) * pa_ref[...] + jax.nn.sigmoid(gb_ref[...]) * pb_ref[...]).astype(BF16)

    return pl.pallas_call(
        body, name="gate_merge", grid=(S // tm,),
        in_specs=[_rows(tm, D), _rows(tm, D), _rows(tm, D, gla_col), _rows(tm, D, glb_col)],
        out_specs=_rows(tm, D), out_shape=jax.ShapeDtypeStruct((S, D), BF16),
        compiler_params=_params(("parallel",)),
    )(pa, pb, proj, proj)


def _resid_rmsmod(x, o, g, ada, gt_col, sc_col, sh_col):
    S, D = x.shape
    tm = _tile(S, 256)

    def body(x_ref, o_ref, g_ref, gt_ref, sc_ref, sh_ref, x2_ref, h_ref):
        x2 = x_ref[...] + gt_ref[...] * o_ref[...]
        x2_ref[...] = x2
        h_ref[...] = ((x2 * _rms_rows(x2) * g_ref[...]) * (1.0 + sc_ref[...]) + sh_ref[...]).astype(BF16)

    return pl.pallas_call(
        body, name="resid_rmsmod2", grid=(S // tm,),
        in_specs=[_rows(tm, D), _rows(tm, D), _vec(D), _vec(D, gt_col), _vec(D, sc_col), _vec(D, sh_col)],
        out_specs=[_rows(tm, D), _rows(tm, D)],
        out_shape=[jax.ShapeDtypeStruct((S, D), F32), jax.ShapeDtypeStruct((S, D), BF16)],
        compiler_params=_params(("parallel",)),
    )(x, o, g, ada, ada, ada)


def _swiglu_tile(ib):
    def fn(r):
        pairs = r.shape[1] // (2 * ib)
        act = [jax.nn.silu(r[:, 2 * p * ib:(2 * p + 1) * ib]) * r[:, (2 * p + 1) * ib:(2 * p + 2) * ib] for p in range(pairs)]
        return r, jnp.concatenate(act, axis=1) if pairs > 1 else act[0]
    return fn


def _swiglu_bwd_tile(ib):
    def fn(d, ff):
        ff = ff.astype(F32)
        out = []
        for p in range(d.shape[1] // ib):
            dp = d[:, p * ib:(p + 1) * ib]
            g = ff[:, 2 * p * ib:(2 * p + 1) * ib]
            u = ff[:, (2 * p + 1) * ib:(2 * p + 2) * ib]
            sg = jax.nn.sigmoid(g)
            out += [dp * u * sg * (1.0 + g * (1.0 - sg)), dp * g * sg]
        return (jnp.concatenate(out, axis=1),)
    return fn


def _loss_head(x2, f, tgt, ada, gt_col):
    S, D = x2.shape
    tm = _tile(S, 256)

    def body(x2_ref, f_ref, t_ref, gt_ref, dy_ref, df_ref, red_ref, loss_ref):
        @pl.when(pl.program_id(0) == 0)
        def _():
            red_ref[...] = jnp.zeros_like(red_ref)
            loss_ref[...] = jnp.zeros_like(loss_ref)

        fv = f_ref[...]
        gt = gt_ref[...]
        err = x2_ref[...] + gt * fv - t_ref[...]
        dy = err * (1.0 / D)
        dy_ref[...] = dy
        df_ref[...] = (dy * gt).astype(BF16)
        red_ref[0:1, :] += jnp.sum(dy * fv, axis=0, keepdims=True)
        loss_ref[...] += (0.5 / D) * jnp.sum(err * err)

    return pl.pallas_call(
        body, name="loss_head", grid=(S // tm,),
        in_specs=[_rows(tm, D), _rows(tm, D), _rows(tm, D), _vec(D, gt_col)],
        out_specs=[_rows(tm, D), _rows(tm, D), _vec(D, rows=8), _vec(LANE, rows=8)],
        out_shape=[jax.ShapeDtypeStruct((S, D), F32), jax.ShapeDtypeStruct((S, D), BF16),
                   jax.ShapeDtypeStruct((8, D), F32), jax.ShapeDtypeStruct((8, LANE), F32)],
        compiler_params=_params(("arbitrary",)),
    )(x2, f, tgt, ada)


def _rmsmod2_bwd(dh2, x2, dy, o, g, ada, sc_col, gt_col):
    S, D = x2.shape
    tm = _tile(S, 256)

    def body(dh_ref, x2_ref, dy_ref, o_ref, g_ref, sc_ref, gt_ref, dx_ref, do_ref, red_ref):
        @pl.when(pl.program_id(0) == 0)
        def _():
            red_ref[...] = jnp.zeros_like(red_ref)

        dh = dh_ref[...]
        x2 = x2_ref[...]
        gv = g_ref[...]
        mod = 1.0 + sc_ref[...]
        r = _rms_rows(x2)
        xn = x2 * r
        t = dh * xn
        red_ref[0:1, :] += jnp.sum(dh, axis=0, keepdims=True)
        red_ref[1:2, :] += jnp.sum(t * gv, axis=0, keepdims=True)
        red_ref[2:3, :] += jnp.sum(t * mod, axis=0, keepdims=True)
        dxn = dh * gv * mod
        dx = dy_ref[...] + r * (dxn - xn * jnp.mean(dxn * xn, axis=-1, keepdims=True))
        dx_ref[...] = dx
        red_ref[3:4, :] += jnp.sum(dx * o_ref[...], axis=0, keepdims=True)
        do_ref[...] = (dx * gt_ref[...]).astype(BF16)

    return pl.pallas_call(
        body, name="rmsmod2_bwd", grid=(S // tm,),
        in_specs=[_rows(tm, D)] * 4 + [_vec(D), _vec(D, sc_col), _vec(D, gt_col)],
        out_specs=[_rows(tm, D), _rows(tm, D), _vec(D, rows=8)],
        out_shape=[jax.ShapeDtypeStruct((S, D), F32), jax.ShapeDtypeStruct((S, D), BF16),
                   jax.ShapeDtypeStruct((8, D), F32)],
        compiler_params=_params(("arbitrary",)),
    )(dh2, x2, dy, o, g, ada, ada)


def _rmsmod1_bwd(dh, x, dx2, g, ada, sc_col):
    S, D = x.shape
    tm = _tile(S, 256)

    def body(dh_ref, x_ref, dx2_ref, g_ref, sc_ref, gx_ref, red_ref):
        @pl.when(pl.program_id(0) == 0)
        def _():
            red_ref[...] = jnp.zeros_like(red_ref)

        dh = dh_ref[...]
        xv = x_ref[...]
        gv = g_ref[...]
        mod = 1.0 + sc_ref[...]
        r = _rms_rows(xv)
        xn = xv * r
        t = dh * xn
        red_ref[0:1, :] += jnp.sum(dh, axis=0, keepdims=True)
        red_ref[1:2, :] += jnp.sum(t * gv, axis=0, keepdims=True)
        red_ref[2:3, :] += jnp.sum(t * mod, axis=0, keepdims=True)
        dxn = dh * gv * mod
        gx_ref[...] = dx2_ref[...] + r * (dxn - xn * jnp.mean(dxn * xn, axis=-1, keepdims=True))

    return pl.pallas_call(
        body, name="rmsmod1_bwd", grid=(S // tm,),
        in_specs=[_rows(tm, D)] * 3 + [_vec(D), _vec(D, sc_col)],
        out_specs=[_rows(tm, D), _vec(D, rows=8)],
        out_shape=[jax.ShapeDtypeStruct((S, D), F32), jax.ShapeDtypeStruct((8, D), F32)],
        compiler_params=_params(("arbitrary",)),
    )(dh, x, dx2, g, ada)


def _gate_bwd(dm, pa, pb, proj, gla_col, glb_col):
    S, D = pa.shape
    tm = _tile(S, 256)

    def body(dm_ref, pa_ref, pb_ref, la_ref, lb_ref, dpa_ref, dpb_ref, dla_ref, dlb_ref):
        dm_ = dm_ref[...]
        ga = jax.nn.sigmoid(la_ref[...])
        gb = jax.nn.sigmoid(lb_ref[...])
        dpa_ref[...] = (dm_ * ga).astype(BF16)
        dpb_ref[...] = (dm_ * gb).astype(BF16)
        dla_ref[...] = (dm_ * pa_ref[...] * ga * (1.0 - ga)).astype(BF16)
        dlb_ref[...] = (dm_ * pb_ref[...] * gb * (1.0 - gb)).astype(BF16)

    return pl.pallas_call(
        body, name="gate_bwd", grid=(S // tm,),
        in_specs=[_rows(tm, D)] * 3 + [_rows(tm, D, gla_col), _rows(tm, D, glb_col)],
        out_specs=[_rows(tm, D)] * 4, out_shape=[jax.ShapeDtypeStruct((S, D), BF16)] * 4,
        compiler_params=_params(("parallel",)),
    )(dm, pa, pb, proj, proj)


def _q_prep_bwd(dq, q0, g_qh, tabs, nh):
    S = q0.shape[0]
    tm = _tile(S, 256)

    def body(dq_ref, q_ref, g_ref, c_ref, s1_ref, s2_ref, o_ref, red_ref):
        @pl.when(pl.program_id(0) == 0)
        def _():
            red_ref[...] = jnp.zeros_like(red_ref)

        c, s1, s2, g = c_ref[...], s1_ref[...], s2_ref[...], g_ref[...]
        dg = jnp.zeros((1, HEAD_PAD), F32)
        for h in range(nh):
            sl = slice(h * HEAD_PAD, (h + 1) * HEAD_PAD)
            d1 = _rope_bwd(dq_ref[:, sl], c, s1, s2)
            xs = q_ref[:, sl]
            r = _head_rms(xs)
            qn = xs * r
            dg = dg + jnp.sum(d1 * qn, axis=0, keepdims=True)
            dn = d1 * g
            o_ref[:, sl] = (r * (dn - qn * (jnp.sum(dn * qn, axis=-1, keepdims=True) * (1.0 / QK_DIM)))).astype(BF16)
        red_ref[0:1, :] += dg

    w = nh * HEAD_PAD
    return pl.pallas_call(
        body, name="mla_q_prep_bwd", grid=(S // tm,),
        in_specs=[_rows(tm, w), _rows(tm, w), _vec(HEAD_PAD)] + [_rows(tm, HEAD_PAD)] * 3,
        out_specs=[_rows(tm, w), _vec(HEAD_PAD, rows=8)],
        out_shape=[jax.ShapeDtypeStruct((S, w), BF16), jax.ShapeDtypeStruct((8, HEAD_PAD), F32)],
        compiler_params=_params(("arbitrary",)),
    )(dq, q0, g_qh, *tabs)


def _k_prep_bwd(dk, dv, kv0, proj, kpe_col, g_kh, tabs, nh):
    S = kv0.shape[0]
    tm = _tile(S, 256)
    wv = nh * HEAD

    def body(dk_ref, dv_ref, kv_ref, kpe_ref, g_ref, c_ref, s1_ref, s2_ref, o_ref, dpe_ref, red_ref):
        @pl.when(pl.program_id(0) == 0)
        def _():
            red_ref[...] = jnp.zeros_like(red_ref)

        c, s1, s2, g = c_ref[...], s1_ref[...], s2_ref[...], g_ref[...]
        kpe = kpe_ref[...]
        dg = jnp.zeros((1, HEAD_PAD), F32)
        dpe = jnp.zeros((tm, LANE), F32)
        for h in range(nh):
            d1 = _rope_bwd(dk_ref[:, h * HEAD_PAD:(h + 1) * HEAD_PAD], c, s1, s2)
            k0 = jnp.concatenate([kv_ref[:, h * HEAD:(h + 1) * HEAD], kpe], axis=1)
            r = _head_rms(k0)
            kn = k0 * r
            dg = dg + jnp.sum(d1 * kn, axis=0, keepdims=True)
            dn = d1 * g
            dk0 = r * (dn - kn * (jnp.sum(dn * kn, axis=-1, keepdims=True) * (1.0 / QK_DIM)))
            o_ref[:, h * HEAD:(h + 1) * HEAD] = dk0[:, :HEAD].astype(BF16)
            dpe = dpe + dk0[:, HEAD:]
        o_ref[:, wv:] = dv_ref[...].astype(BF16)
        dpe_ref[...] = dpe.astype(BF16)
        red_ref[0:1, :] += dg

    return pl.pallas_call(
        body, name="mla_k_prep_bwd", grid=(S // tm,),
        in_specs=[_rows(tm, nh * HEAD_PAD), _rows(tm, wv), _rows(tm, wv, 0), _rows(tm, LANE, kpe_col),
                  _vec(HEAD_PAD)] + [_rows(tm, HEAD_PAD)] * 3,
        out_specs=[_rows(tm, 2 * wv), _rows(tm, LANE), _vec(HEAD_PAD, rows=8)],
        out_shape=[jax.ShapeDtypeStruct((S, 2 * wv), BF16), jax.ShapeDtypeStruct((S, LANE), BF16),
                   jax.ShapeDtypeStruct((8, HEAD_PAD), F32)],
        compiler_params=_params(("arbitrary",)),
    )(dk, dv, kv0, proj, g_kh, *tabs)


def _latent_norm_bwd(dcqn, dckvn, proj, g_q, g_kv, ql):
    S = proj.shape[0]
    tm = _tile(S, 512)

    def body(dq_ref, dkv_ref, cq_ref, ckv_ref, gq_ref, gkv_ref, oq_ref, okv_ref, red_ref):
        @pl.when(pl.program_id(0) == 0)
        def _():
            red_ref[...] = jnp.zeros_like(red_ref)

        for row, (d_ref, c_ref, g_ref, o_ref) in enumerate(((dq_ref, cq_ref, gq_ref, oq_ref),
                                                            (dkv_ref, ckv_ref, gkv_ref, okv_ref))):
            d = d_ref[...]
            cv = c_ref[...]
            r = _rms_rows(cv)
            ch = cv * r
            red_ref[row:row + 1, :] += jnp.sum(d * ch, axis=0, keepdims=True)
            dn = d * g_ref[...]
            o_ref[...] = (r * (dn - ch * jnp.mean(dn * ch, axis=-1, keepdims=True))).astype(BF16)

    return pl.pallas_call(
        body, name="latent_norm_bwd", grid=(S // tm,),
        in_specs=[_rows(tm, ql), _rows(tm, ql), _rows(tm, ql, 0), _rows(tm, ql, 1), _vec(ql), _vec(ql)],
        out_specs=[_rows(tm, ql), _rows(tm, ql), _vec(ql, rows=8)],
        out_shape=[jax.ShapeDtypeStruct((S, ql), BF16)] * 2 + [jax.ShapeDtypeStruct((8, ql), F32)],
        compiler_params=_params(("arbitrary",)),
    )(dcqn, dckvn, proj, proj, g_q, g_kv)


NEG = -1e30
ATT_TILE = 512
SB_TILE = 512
SB_SUB = 128
_NT = (((1,), (1,)), ((), ()))
_TN = (((0,), (0,)), ((), ()))


def _dot(a, b, dn=(((1,), (0,)), ((), ()))):
    return lax.dot_general(a, b, dn, preferred_element_type=F32)


def _key_rows(kb, t):
    return pl.ds(pl.multiple_of(kb * t, t), t)


def _diag_mask(t, strict):
    r = lax.broadcasted_iota(jnp.int32, (t, t), 0)
    c = lax.broadcasted_iota(jnp.int32, (t, t), 1)
    return c < r if strict else c <= r


def _mla_fwd(q, k, kv0, nh):
    S = q.shape[0]
    t = _tile(S, ATT_TILE)

    def body(q_ref, k_ref, v_ref, o_ref, lse_ref):
        i = pl.program_id(1)
        qv = q_ref[...]

        def block(kb, carry, masked):
            m, l, acc = carry
            rows = _key_rows(kb, t)
            s = _dot(qv, k_ref[rows, :], _NT)
            if masked:
                s = jnp.where(_diag_mask(t, False), s, NEG)
            m_new = jnp.maximum(m, jnp.max(s, axis=-1, keepdims=True))
            alpha = jnp.exp(m - m_new)
            p = jnp.exp(s - m_new)
            l = alpha * l + jnp.sum(p, axis=-1, keepdims=True)
            acc = alpha * acc + _dot(p.astype(BF16), v_ref[rows, :].astype(BF16))
            return m_new, l, acc

        init = (jnp.full((t, 1), NEG, F32), jnp.zeros((t, 1), F32), jnp.zeros((t, HEAD), F32))
        carry = lax.fori_loop(0, i, lambda kb, c: block(kb, c, False), init)
        m, l, acc = block(i, carry, True)
        o_ref[...] = acc / l
        lse_ref[...] = m + jnp.log(l)

    return pl.pallas_call(
        body, name="mla_attn_fwd", grid=(nh, S // t),
        in_specs=[pl.BlockSpec((t, HEAD_PAD), lambda h, i: (i, h)),
                  pl.BlockSpec((S, HEAD_PAD), lambda h, i: (0, h)),
                  pl.BlockSpec((S, HEAD), lambda h, i: (0, nh + h))],
        out_specs=[pl.BlockSpec((t, HEAD), lambda h, i: (i, h)),
                   pl.BlockSpec((None, t, 1), lambda h, i: (h, i, 0))],
        out_shape=[jax.ShapeDtypeStruct((S, nh * HEAD), F32), jax.ShapeDtypeStruct((nh, S, 1), F32)],
        compiler_params=_params(("parallel", "arbitrary")),
    )(q, k, kv0)


def _mla_bwd(q, k, kv0, o, do, lse, nh):
    S = q.shape[0]
    t = _tile(S, ATT_TILE)
    scale = QK_DIM ** -0.5

    def body(q_ref, k_ref, v_ref, o_ref, do_ref, lse_ref, dq_ref, dk_ref, dv_ref):
        i = pl.program_id(1)

        @pl.when(i == 0)
        def _():
            dk_ref[...] = jnp.zeros_like(dk_ref)
            dv_ref[...] = jnp.zeros_like(dv_ref)

        qv = q_ref[...]
        dov = do_ref[...]
        delta = jnp.sum(dov * o_ref[...], axis=-1, keepdims=True)
        dob = dov.astype(BF16)
        lse = lse_ref[...]

        def block(kb, dq, masked):
            rows = _key_rows(kb, t)
            ks = k_ref[rows, :]
            vs = v_ref[rows, :].astype(BF16)
            p = jnp.exp(_dot(qv, ks, _NT) - lse)
            if masked:
                p = jnp.where(_diag_mask(t, False), p, 0.0)
            ds = (p * (_dot(dob, vs, _NT) - delta)).astype(BF16)
            dk_ref[rows, :] += _dot(ds, qv, _TN)
            dv_ref[rows, :] += _dot(p.astype(BF16), dob, _TN)
            return dq + _dot(ds, ks)

        dq = lax.fori_loop(0, i, lambda kb, c: block(kb, c, False), jnp.zeros((t, HEAD_PAD), F32))
        dq_ref[...] = block(i, dq, True) * scale

    return pl.pallas_call(
        body, name="mla_attn_bwd", grid=(nh, S // t),
        in_specs=[pl.BlockSpec((t, HEAD_PAD), lambda h, i: (i, h)),
                  pl.BlockSpec((S, HEAD_PAD), lambda h, i: (0, h)),
                  pl.BlockSpec((S, HEAD), lambda h, i: (0, nh + h)),
                  pl.BlockSpec((t, HEAD), lambda h, i: (i, h)),
                  pl.BlockSpec((t, HEAD), lambda h, i: (i, h)),
                  pl.BlockSpec((None, t, 1), lambda h, i: (h, i, 0))],
        out_specs=[pl.BlockSpec((t, HEAD_PAD), lambda h, i: (i, h)),
                   pl.BlockSpec((S, HEAD_PAD), lambda h, i: (0, h)),
                   pl.BlockSpec((S, HEAD), lambda h, i: (0, h))],
        out_shape=[jax.ShapeDtypeStruct((S, nh * HEAD_PAD), F32), jax.ShapeDtypeStruct((S, nh * HEAD_PAD), F32),
                   jax.ShapeDtypeStruct((S, nh * HEAD), F32)],
        compiler_params=_params(("parallel", "arbitrary")),
    )(q, k, kv0, o, do, lse)


def _tri(n, cmp):
    r = lax.broadcasted_iota(jnp.int32, (n, n), 0)
    c = lax.broadcasted_iota(jnp.int32, (n, n), 1)
    return jnp.where(cmp(r, c), 1.0, 0.0).astype(BF16)


def _sb_block(qv, ks, run, upper, t, masked):
    z = _dot(qv, ks, _NT)
    lb = jnp.minimum(z, 0.0) - jnp.log(1.0 + jnp.exp(-jnp.abs(z)))
    lom = lb - z
    mask = _diag_mask(t, True) if masked else None
    if masked:
        lom = jnp.where(mask, lom, 0.0)
    tails = []
    for sblk in reversed(range(t // SB_SUB)):
        part = lom[:, sblk * SB_SUB:(sblk + 1) * SB_SUB]
        tails.append(_dot(part.astype(BF16), upper) + run)
        run = run + jnp.sum(part, axis=-1, keepdims=True)
    a = jnp.exp(lb + jnp.concatenate(tails[::-1], axis=1))
    if masked:
        a = jnp.where(mask, a, 0.0)
    return a, lb, mask, run


def _sb_fwd(proj, q_col, k_col, v_col, nh):
    S = proj.shape[0]
    t = _tile(S, SB_TILE)
    assert S // t <= LANE
    scale = HEAD ** -0.5

    def body(q_ref, k_ref, v_ref, o_ref, runs_ref):
        i = pl.program_id(1)
        qv = (q_ref[...] * scale).astype(BF16)
        upper = _tri(SB_SUB, lambda j, s: j > s)
        lane = lax.broadcasted_iota(jnp.int32, (t, LANE), 1)

        def block(kb, carry, masked):
            run, acc, runs = carry
            runs = jnp.where(lane == kb, run, runs)
            rows = _key_rows(kb, t)
            a, _, _, run = _sb_block(qv, k_ref[rows, :].astype(BF16), run, upper, t, masked)
            return run, acc + _dot(a.astype(BF16), v_ref[rows, :].astype(BF16)), runs

        carry = block(i, (jnp.zeros((t, 1), F32), jnp.zeros((t, HEAD), F32), jnp.zeros((t, LANE), F32)), True)
        _, o_ref[...], runs_ref[...] = lax.fori_loop(0, i, lambda j, c: block(i - 1 - j, c, False), carry)

    return pl.pallas_call(
        body, name="sb_attn_fwd", grid=(nh, S // t),
        in_specs=[pl.BlockSpec((t, HEAD), lambda h, i: (i, q_col + h)),
                  pl.BlockSpec((S, HEAD), lambda h, i: (0, k_col + h)),
                  pl.BlockSpec((S, HEAD), lambda h, i: (0, v_col + h))],
        out_specs=[pl.BlockSpec((t, HEAD), lambda h, i: (i, h)), pl.BlockSpec((None, t, LANE), lambda h, i: (h, i, 0))],
        out_shape=[jax.ShapeDtypeStruct((S, nh * HEAD), F32), jax.ShapeDtypeStruct((nh, S, LANE), F32)],
        compiler_params=_params(("parallel", "arbitrary")),
    )(proj, proj, proj)


def _sb_bwd(proj, q_col, k_col, v_col, dy, runs, nh):
    S = proj.shape[0]
    t = _tile(S, SB_TILE)
    scale = HEAD ** -0.5

    def body(q_ref, k_ref, v_ref, dy_ref, runs_ref, dq_ref, dk_ref, dv_ref):
        i = pl.program_id(1)

        @pl.when(i == 0)
        def _():
            dk_ref[...] = jnp.zeros_like(dk_ref)
            dv_ref[...] = jnp.zeros_like(dv_ref)

        qv = (q_ref[...] * scale).astype(BF16)
        dyb = dy_ref[...].astype(BF16)
        runs_v = runs_ref[...]
        lane = lax.broadcasted_iota(jnp.int32, (t, LANE), 1)
        upper = _tri(SB_SUB, lambda j, s: j > s)
        before = _tri(SB_SUB, lambda s, j: s < j)

        def block(kb, carry, masked):
            prefix, dq = carry
            rows = _key_rows(kb, t)
            ks = k_ref[rows, :].astype(BF16)
            vs = v_ref[rows, :].astype(BF16)
            run = jnp.sum(jnp.where(lane == kb, runs_v, 0.0), axis=-1, keepdims=True)
            a, lb, mask, _ = _sb_block(qv, ks, run, upper, t, masked)
            dl = a * _dot(dyb, vs, _NT)
            lefts = []
            for sblk in range(t // SB_SUB):
                part = dl[:, sblk * SB_SUB:(sblk + 1) * SB_SUB]
                lefts.append(_dot(part.astype(BF16), before) + prefix)
                prefix = prefix + jnp.sum(part, axis=-1, keepdims=True)
            beta = jnp.exp(lb)
            dz = dl * (1.0 - beta) - beta * jnp.concatenate(lefts, axis=1)
            if masked:
                dz = jnp.where(mask, dz, 0.0)
            dz = dz.astype(BF16)
            dk_ref[rows, :] += _dot(dz, qv, _TN)
            dv_ref[rows, :] += _dot(a.astype(BF16), dyb, _TN)
            return prefix, dq + _dot(dz, ks)

        carry = lax.fori_loop(0, i, lambda kb, c: block(kb, c, False),
                              (jnp.zeros((t, 1), F32), jnp.zeros((t, HEAD), F32)))
        dq_ref[...] = block(i, carry, True)[1] * scale

    full = pl.BlockSpec((S, HEAD), lambda h, i: (0, h))
    tile = pl.BlockSpec((t, HEAD), lambda h, i: (i, h))
    return pl.pallas_call(
        body, name="sb_attn_bwd", grid=(nh, S // t),
        in_specs=[pl.BlockSpec((t, HEAD), lambda h, i: (i, q_col + h)),
                  pl.BlockSpec((S, HEAD), lambda h, i: (0, k_col + h)),
                  pl.BlockSpec((S, HEAD), lambda h, i: (0, v_col + h)), tile,
                  pl.BlockSpec((None, t, LANE), lambda h, i: (h, i, 0))],
        out_specs=[tile, full, full],
        out_shape=[jax.ShapeDtypeStruct((S, nh * HEAD), F32)] * 3,
        compiler_params=_params(("parallel", "arbitrary")),
    )(proj, proj, proj, dy, runs)


def _place():
    return lax.axis_index("x"), lax.axis_index("y"), lax.axis_index("c")


def _other_chips(x, y):
    return [(1 - x, y), (x, 1 - y), (1 - x, 1 - y)]


def _dev_index(p):
    return 4 * p[0] + 2 * p[1] + p[2]


def _gather_blocks(blocks, *, name, in_vmem):
    n = len(blocks)
    per = 7

    def body(*refs):
        ins, outs = refs[:n], refs[n:2 * n]
        send_sems, recv_sems, local_sems = refs[2 * n:]
        x, y, c = _place()
        me, sibling = (x, y, c), (x, y, 1 - c)
        chips = _other_chips(x, y)

        def slot(a, p):
            return outs[a].at[_dev_index(p)]

        def copy(a, k, block, to, src=None):
            return pltpu.make_async_remote_copy(
                src_ref=slot(a, block) if src is None else src, dst_ref=slot(a, block),
                send_sem=send_sems.at[a * per + k], recv_sem=recv_sems.at[a * per + k],
                device_id=to, device_id_type=MESH)

        mine = [pltpu.make_async_copy(ins[a], slot(a, me), local_sems.at[a]) for a in range(n)] if in_vmem else []
        for cp in mine:
            cp.start()
        first = []
        for a in range(n):
            first.append(copy(a, 0, me, sibling, src=ins[a]))
            first += [copy(a, 1 + j, me, (*chip, c), src=ins[a]) for j, chip in enumerate(chips)]
        for cp in first:
            cp.start()
        passed = []
        for a in range(n):
            for j, chip in enumerate(chips):
                copy(a, 1 + j, (*chip, c), me).wait_recv()
                cp = copy(a, 4 + j, (*chip, c), sibling)
                cp.start()
                passed.append(cp)
        for a in range(n):
            copy(a, 0, sibling, me).wait_recv()
            for j, chip in enumerate(chips):
                copy(a, 4 + j, (*chip, 1 - c), me).wait_recv()
        for cp in first + passed:
            cp.wait_send()
        for cp in mine:
            cp.wait()

    space = pltpu.VMEM if in_vmem else pl.ANY
    spec = pl.BlockSpec(memory_space=space)
    outs = pl.pallas_call(
        body, name=name, in_specs=[spec] * n, out_specs=[spec] * n,
        out_shape=[jax.ShapeDtypeStruct((N_DEV,) + b.shape, b.dtype) for b in blocks],
        scratch_shapes=[pltpu.SemaphoreType.DMA((n * per,)), pltpu.SemaphoreType.DMA((n * per,)),
                        pltpu.SemaphoreType.DMA((n,))],
        compiler_params=pltpu.CompilerParams(vmem_limit_bytes=VMEM_LIMIT),
    )(*blocks)
    return list(outs)


def _sibling_swap(arrs, *, name, whole=False):
    n = len(arrs)

    def body(*refs):
        ins, outs = refs[:n], refs[n:2 * n]
        send_sems, recv_sems = refs[2 * n:]
        x, y, c = _place()
        copies = [pltpu.make_async_remote_copy(
            src_ref=ins[a] if whole else ins[a].at[1 - c], dst_ref=outs[a],
            send_sem=send_sems.at[a], recv_sem=recv_sems.at[a],
            device_id=(x, y, 1 - c), device_id_type=MESH) for a in range(n)]
        for cp in copies:
            cp.start()
        for cp in copies:
            cp.wait()

    spec = pl.BlockSpec(memory_space=pl.ANY)
    return list(pl.pallas_call(
        body, name=name, in_specs=[spec] * n, out_specs=[spec] * n,
        out_shape=[jax.ShapeDtypeStruct(a.shape if whole else a.shape[1:], a.dtype) for a in arrs],
        scratch_shapes=[pltpu.SemaphoreType.DMA((n,)), pltpu.SemaphoreType.DMA((n,))],
    )(*arrs))


_HBM = pl.BlockSpec(memory_space=pltpu.HBM)
_SEM = pl.BlockSpec(memory_space=pltpu.SEMAPHORE)
_EFFECT = pltpu.SideEffectType.DATAFLOW_SIDE_EFFECTING


def _in_hbm(a):
    return pltpu.with_memory_space_constraint(a, pltpu.HBM)


def _split_copies(srcs, lands, send_sems, recv_sems, plan):
    x, y, c = _place()
    copies = []
    for a, (src, land) in enumerate(zip(srcs, lands)):
        steps = plan(x, y, c)
        for k, (pick, slot, to) in enumerate(steps):
            copies.append(pltpu.make_async_remote_copy(
                src_ref=pick(src), dst_ref=slot(land), send_sem=send_sems.at[a * len(steps) + k],
                recv_sem=recv_sems.at[a * len(steps) + k], device_id=to, device_id_type=MESH))
    return copies


def _split_start(srcs, land_shapes, plan, per, *, name):
    n = len(srcs)

    def body(*refs):
        send_sems, recv_sems = refs[2 * n], refs[2 * n + 1]
        for cp in _split_copies(refs[:n], refs[n:2 * n], send_sems, recv_sems, plan):
            cp.start()
        token = refs[-1]
        token[...] = jnp.zeros_like(token)

    lands = [_in_hbm(lax.empty(s.shape, s.dtype)) for s in land_shapes]
    outs = pl.pallas_call(
        body, name=name,
        out_shape=(pltpu.SemaphoreType.DMA((n * per,)), pltpu.SemaphoreType.DMA((n * per,)),
                   *[pltpu.HBM(s.shape, s.dtype) for s in srcs], *[pltpu.HBM(s.shape, s.dtype) for s in land_shapes],
                   jax.ShapeDtypeStruct((8, LANE), F32)),
        in_specs=[_HBM] * (2 * n),
        out_specs=(_SEM, _SEM, *[_HBM] * (2 * n), pl.BlockSpec(memory_space=pltpu.VMEM)),
        input_output_aliases={i: 2 + i for i in range(2 * n)},
        compiler_params=pltpu.CompilerParams(has_side_effects=_EFFECT),
    )(*[_in_hbm(s) for s in srcs], *lands)
    return outs[0], outs[1], list(outs[2:2 + n]), list(outs[2 + n:2 + 2 * n]), outs[-1]


def _split_wait(send_sems, recv_sems, srcs, lands, after, plan, *, name):
    n = len(srcs)

    def body(*refs):
        for cp in _split_copies(refs[:n], refs[n:2 * n], refs[2 * n], refs[2 * n + 1], plan):
            cp.wait_send()
            cp.wait_recv()

    outs = pl.pallas_call(
        body, name=name,
        out_shape=(*[pltpu.HBM(s.shape, s.dtype) for s in srcs], *[pltpu.HBM(s.shape, s.dtype) for s in lands]),
        in_specs=[_HBM] * (2 * n) + [_SEM, _SEM, pl.BlockSpec(memory_space=pl.ANY)],
        out_specs=tuple([_HBM] * (2 * n)),
        input_output_aliases={i: i for i in range(2 * n)},
        compiler_params=pltpu.CompilerParams(has_side_effects=_EFFECT),
    )(*srcs, *lands, send_sems, recv_sems, after)
    return list(outs[:n]), list(outs[n:])


def _gather_plan(x, y, c):
    slot = lambda land: land.at[_dev_index((x, y, c))]
    whole = lambda src: src
    return [(whole, slot, (x, y, 1 - c))] + [(whole, slot, (px, py, c)) for px, py in _other_chips(x, y)]


def _exchange_plan(x, y, c):
    return [(lambda src, k=2 * px + py: src.at[k], lambda land, j=j: land.at[j], (px, py, c))
            for j, (px, py) in enumerate(_other_chips(x, y))]


def _gather_forward(lands, *, name):
    n = len(lands)

    def body(*refs):
        lands_in, outs = refs[:n], refs[n:2 * n]
        send_sems, recv_sems = refs[2 * n:]
        x, y, c = _place()
        copies = []
        for a in range(n):
            for j, (px, py) in enumerate(_other_chips(x, y)):
                copies.append((pltpu.make_async_remote_copy(
                    src_ref=lands_in[a].at[_dev_index((px, py, c))], dst_ref=outs[a].at[_dev_index((px, py, c))],
                    send_sem=send_sems.at[3 * a + j], recv_sem=recv_sems.at[3 * a + j],
                    device_id=(x, y, 1 - c), device_id_type=MESH), a, j, (px, py)))
        for cp, _, _, _ in copies:
            cp.start()
        for cp, a, j, (px, py) in copies:
            cp.wait_send()
            pltpu.make_async_remote_copy(
                src_ref=lands_in[a].at[_dev_index((px, py, 1 - c))], dst_ref=outs[a].at[_dev_index((px, py, 1 - c))],
                send_sem=send_sems.at[3 * a + j], recv_sem=recv_sems.at[3 * a + j],
                device_id=(x, y, 1 - c), device_id_type=MESH).wait_recv()

    spec = pl.BlockSpec(memory_space=pl.ANY)
    return list(pl.pallas_call(
        body, name=name, in_specs=[spec] * n, out_specs=[spec] * n,
        out_shape=[jax.ShapeDtypeStruct(a.shape, a.dtype) for a in lands],
        input_output_aliases={a: a for a in range(n)},
        scratch_shapes=[pltpu.SemaphoreType.DMA((3 * n,)), pltpu.SemaphoreType.DMA((3 * n,))],
    )(*lands))


def _flat2(a, lead):
    return a.reshape(a.shape[:lead] + (-1, a.shape[-1]))


def _pair_sum(g, recv, c_idx, *, name):
    _, nchip, r, w = g.shape
    tm = _tile(r, 256) if r % 8 == 0 else r

    def body(c_ref, g_ref, r_ref, o_ref):
        o_ref[...] = (g_ref[...].astype(F32) + r_ref[...].astype(F32)).astype(o_ref.dtype)

    return pl.pallas_call(
        body, name=name,
        grid_spec=pltpu.PrefetchScalarGridSpec(
            num_scalar_prefetch=1, grid=(nchip, r // tm),
            in_specs=[pl.BlockSpec((None, None, tm, w), lambda k, i, c_ref: (c_ref[0], k, i, 0)),
                      pl.BlockSpec((None, tm, w), lambda k, i, c_ref: (k, i, 0))],
            out_specs=pl.BlockSpec((None, tm, w), lambda k, i, c_ref: (k, i, 0))),
        out_shape=jax.ShapeDtypeStruct((nchip, r, w), BF16),
        compiler_params=_params(("parallel", "parallel")),
    )(c_idx, g, recv)


def _chip_sum(s1, recv, chip_idx, *, name):
    _, r, w = s1.shape
    tm = _tile(r, 256) if r % 8 == 0 else r

    def body(k_ref, s_ref, r_ref, o_ref):
        acc = s_ref[...].astype(F32)
        for j in range(3):
            acc = acc + r_ref[j].astype(F32)
        o_ref[...] = acc

    return pl.pallas_call(
        body, name=name,
        grid_spec=pltpu.PrefetchScalarGridSpec(
            num_scalar_prefetch=1, grid=(r // tm,),
            in_specs=[pl.BlockSpec((None, tm, w), lambda i, k_ref: (k_ref[0], i, 0)),
                      pl.BlockSpec((3, tm, w), lambda i, k_ref: (0, i, 0))],
            out_specs=pl.BlockSpec((tm, w), lambda i, k_ref: (i, 0))),
        out_shape=jax.ShapeDtypeStruct((r, w), F32),
        compiler_params=_params(("parallel",)),
    )(chip_idx, s1, recv)


def _adam_math(w, g, m, v):
    m = ADAM_B1 * m + (1.0 - ADAM_B1) * g
    v = ADAM_B2 * v + (1.0 - ADAM_B2) * (g * g)
    m_hat = m / (1.0 - ADAM_B1 ** ADAM_STEP)
    v_hat = v / (1.0 - ADAM_B2 ** ADAM_STEP)
    delta = -ADAM_LR * (m_hat / (jnp.sqrt(v_hat) + ADAM_EPS) + ADAM_WD * w)
    return delta, m, v


def _adamw(w, mine, other, c_idx, m, v, *, name):
    r, cw = w.shape
    hr = r // 2
    tm = _row_tile(hr, 9 * cw * 4)

    def body(c_ref, w_ref, a_ref, b_ref, m_ref, v_ref, g_ref, d_ref, nm_ref, nv_ref):
        g = jnp.where(pl.program_id(0) == c_ref[0], a_ref[...], b_ref[...])
        g_ref[...] = g
        d_ref[...], nm_ref[...], nv_ref[...] = _adam_math(w_ref[...], g, m_ref[...], v_ref[...])

    full = pl.BlockSpec((None, tm, cw), lambda h, i, c_ref: (h, i, 0))
    half = pl.BlockSpec((tm, cw), lambda h, i, c_ref: (i, 0))
    outs = pl.pallas_call(
        body, name=name,
        grid_spec=pltpu.PrefetchScalarGridSpec(
            num_scalar_prefetch=1, grid=(2, hr // tm),
            in_specs=[full, half, half, full, full], out_specs=[full] * 4),
        out_shape=[jax.ShapeDtypeStruct((2, hr, cw), F32)] * 4,
        compiler_params=_params(("parallel", "parallel")),
    )(c_idx, w.reshape(2, hr, cw), mine, other, m.reshape(2, hr, cw), v.reshape(2, hr, cw))
    return [o.reshape(r, cw) for o in outs]


def _adamw_ada(cact_t, dada, w, m, v):
    r, cw = w.shape
    nb = cact_t.shape[1]
    tm = _tile(r, 256)
    tn = _tile(cw, 1024)

    def body(a_ref, d_ref, w_ref, m_ref, v_ref, g_ref, dl_ref, nm_ref, nv_ref):
        a = a_ref[...]
        d = d_ref[...]
        g = a[:, 0:1] * d[0:1, :]
        for b in range(1, nb):
            g = g + a[:, b:b + 1] * d[b:b + 1, :]
        g_ref[...] = g
        dl_ref[...], nm_ref[...], nv_ref[...] = _adam_math(w_ref[...], g, m_ref[...], v_ref[...])

    blk = pl.BlockSpec((tm, tn), lambda i, j: (i, j))
    return pl.pallas_call(
        body, name="adamw_ada", grid=(r // tm, cw // tn),
        in_specs=[pl.BlockSpec((tm, nb), lambda i, j: (i, 0)), pl.BlockSpec((nb, tn), lambda i, j: (0, j)), blk, blk, blk],
        out_specs=[blk] * 4, out_shape=[jax.ShapeDtypeStruct((r, cw), F32)] * 4,
        compiler_params=_params(("parallel", "parallel")),
    )(cact_t, dada, w, m, v)


def _adamw_vec(parts, w, m, v):
    n = w.shape[1]

    def body(p_ref, w_ref, m_ref, v_ref, g_ref, d_ref, nm_ref, nv_ref):
        p = p_ref[...]
        g = p[0:1, :]
        for b in range(1, N_DEV):
            g = g + p[b:b + 1, :]
        g_ref[...] = g
        d_ref[...], nm_ref[...], nv_ref[...] = _adam_math(w_ref[...], g, m_ref[...], v_ref[...])

    return pl.pallas_call(
        body, name="adamw_vec", out_shape=[jax.ShapeDtypeStruct((1, n), F32)] * 4,
        compiler_params=pltpu.CompilerParams(vmem_limit_bytes=VMEM_LIMIT),
    )(parts, w, m, v)


def _w_in_segments(kpe0, d_in, cs):
    segs = []
    for k in range(4):
        lo, hi = k * cs, (k + 1) * cs
        for a, b, shift in ((0, kpe0, 0), (kpe0, kpe0 + ROPE, d_in - ROPE - kpe0), (kpe0 + ROPE, d_in, -ROPE)):
            a, b = max(lo, a), min(hi, b)
            if a < b:
                segs.append((k, a - lo, a + shift, b - a))
    return segs


def _w_in_layout(g8, kpe0):
    _, hr, cs = g8.shape
    rows, d_in = 2 * hr, 4 * cs
    segs = _w_in_segments(kpe0, d_in, cs)
    tm = _tile(rows, 256)

    def body(g_ref, o_ref):
        for k, src, dst, w in segs:
            o_ref[:, dst:dst + w] = g_ref[k, :, src:src + w]
        o_ref[:, d_in:] = jnp.zeros((tm, ROPE), o_ref.dtype)

    return pl.pallas_call(
        body, name="w_in_layout", grid=(rows // tm,),
        in_specs=[pl.BlockSpec((4, tm, cs), lambda i: (0, i, 0))], out_specs=_rows(tm, d_in + ROPE),
        out_shape=jax.ShapeDtypeStruct((rows, d_in + ROPE), g8.dtype), compiler_params=_params(("parallel",)),
    )(g8.reshape(4, rows, cs))


def _w_in_grad_pieces(g, kpe0):
    rows, d_in_p = g.shape
    d_in = d_in_p - ROPE
    cs = d_in // 4
    segs = _w_in_segments(kpe0, d_in, cs)
    hr = rows // 2
    tm = _tile(hr, 256)
    per_half = hr // tm

    def body(g_ref, o_ref):
        for k, src, dst, w in segs:
            o_ref[k, :, src:src + w] = g_ref[:, dst:dst + w]

    return pl.pallas_call(
        body, name="w_in_grad_pieces", grid=(rows // tm,),
        in_specs=[_rows(tm, d_in_p)],
        out_specs=pl.BlockSpec((None, 4, tm, cs), lambda i: (i // per_half, 0, i % per_half, 0)),
        out_shape=jax.ShapeDtypeStruct((2, 4, hr, cs), g.dtype), compiler_params=_params(("parallel",)),
    )(g)


def _cols_from_chips(g8, rows):
    cs = g8.shape[-1]
    return g8.reshape(4, rows, cs).transpose(1, 0, 2).reshape(rows, 4 * cs)


def _cols_to_pieces(g):
    rows, c4 = g.shape
    return g.reshape(2, rows // 2, 4, c4 // 4).transpose(0, 2, 1, 3)


def _rows_to_pieces(g):
    r4, cols = g.shape
    return g.reshape(4, 2, r4 // 8, cols).transpose(1, 0, 2, 3)


def _pad_cols(a, w):
    return jnp.pad(a, ((0, 0), (0, w - a.shape[1])))


def kernel(x, c, positions, w_ada, b_ada, g_norm1, g_norm2, w_in, g_q_latent, g_kv_latent, w_uq, w_ukv, g_q_head, g_k_head, w_proj_mla, w_proj_sb, w_out, w_ffn_in, w_ffn_out, loss_target, m_w_ada, m_b_ada, m_g_norm1, m_g_norm2, m_w_in, m_g_q_latent, m_g_kv_latent, m_w_uq, m_w_ukv, m_g_q_head, m_g_k_head, m_w_proj_mla, m_w_proj_sb, m_w_out, m_w_ffn_in, m_w_ffn_out, v_w_ada, v_b_ada, v_g_norm1, v_g_norm2, v_w_in, v_g_q_latent, v_g_kv_latent, v_w_uq, v_w_ukv, v_g_q_head, v_g_k_head, v_w_proj_mla, v_w_proj_sb, v_w_out, v_w_ffn_in, v_w_ffn_out):
    xi, yi, ci = _place()
    chip = 2 * xi + yi
    dev = 2 * chip + ci
    c_idx = jnp.reshape(ci, (1,)).astype(jnp.int32)
    chip_idx = jnp.reshape(chip, (1,)).astype(jnp.int32)

    x = x[0]
    tgt = loss_target[0]
    S, D = x.shape
    ql = g_q_latent.shape[1]
    assert g_kv_latent.shape[1] == ql
    mlaw = w_proj_mla.shape[1]
    nh = mlaw // HEAD
    sbw = w_proj_sb.shape[1]
    assert sbw == mlaw
    dff = w_ffn_out.shape[1] * 4
    d_in = 2 * ql + ROPE + 3 * sbw + 2 * D
    d_in_p = d_in + ROPE
    q_col = (2 * ql) // HEAD
    k_col = q_col + nh
    v_col = k_col + nh
    gla_col = (2 * ql + 3 * sbw) // D
    glb_col = gla_col + 1
    kpe_col = (d_in - ROPE) // LANE
    assert (2 * ql + 3 * sbw) % D == 0 and (d_in - ROPE) % LANE == 0

    mats = {"w_in": w_in[0], "w_uq": w_uq[0], "w_ukv": w_ukv[0], "w_proj_mla": w_proj_mla[0],
            "w_proj_sb": w_proj_sb[0], "w_out": w_out[0], "w_ffn_in": w_ffn_in[0], "w_ffn_out": w_ffn_out[0]}
    names = list(mats)
    row_sharded = {"w_out", "w_ffn_out"}

    c_all = _gather_blocks([jnp.broadcast_to(c, (8, D))], name="gather_cond", in_vmem=True)[0][:, 0, :]
    n_ada = w_ada.shape[2]
    b_shard = lax.dynamic_slice_in_dim(b_ada, chip * n_ada, n_ada, axis=1)
    ada_shard = _mm(c_all, w_ada[0], name="ada_proj", a_fn=jax.nn.silu, bias=b_shard)
    ada_all = _gather_blocks([ada_shard], name="gather_ada", in_vmem=True)[0]
    ada_rows = lax.dynamic_index_in_dim(ada_all, dev, axis=1, keepdims=False)
    ada = ada_rows[0::2].reshape(1, 4 * n_ada)
    SH1, SC1, GT1, SH2, SC2, GT2 = range(6)

    def after(dep, a):
        return a + (dep.reshape(-1)[0:1].reshape((1,) * a.ndim) * 0).astype(a.dtype)

    def fill_own(g8, own):
        return lax.dynamic_update_index_in_dim(g8, own, dev, 0)

    halves = []
    for nm in names:
        w = mats[nm]
        hr = w.shape[0] // 2
        halves.append(lax.dynamic_slice_in_dim(w, ci * hr, hr, axis=0).astype(BF16))
    half_of = dict(zip(names, halves))
    early = ["w_in", "w_uq", "w_ukv"]
    late = ["w_proj_mla", "w_proj_sb", "w_out", "w_ffn_in", "w_ffn_out"]
    early_halves = [half_of[nm] for nm in early]
    early_halves[0] = after(ada, early_halves[0])
    early_got = _gather_blocks(early_halves, name="gather_weights", in_vmem=False)
    gathered = {nm: fill_own(g8, own) for nm, g8, own in zip(early, early_got, early_halves)}
    late_halves = [half_of[nm] for nm in late]
    late_halves[0] = after(gathered[early[1]], late_halves[0])
    late_send, late_recv, late_srcs, late_lands, late_token = _split_start(
        late_halves, [jax.ShapeDtypeStruct((N_DEV,) + h.shape, h.dtype) for h in late_halves], _gather_plan, 4,
        name="gather_late_start")
    ada = ada + late_token[0:1, 0:1]

    def full_cols(nm):
        return _cols_from_chips(gathered[nm], mats[nm].shape[0])

    kpe0 = 2 * ql
    w_in_p = _w_in_layout(gathered["w_in"], kpe0)
    w_uq_p = jnp.pad(full_cols("w_uq").reshape(ql, nh, QK_DIM), ((0, 0), (0, 0), (0, HEAD_PAD - QK_DIM))
                     ).reshape(ql, nh * HEAD_PAD)
    w_ukv4 = full_cols("w_ukv").reshape(ql, nh, 2 * HEAD)
    w_ukv_p = jnp.concatenate([w_ukv4[:, :, :HEAD].reshape(ql, mlaw), w_ukv4[:, :, HEAD:].reshape(ql, mlaw)], axis=1)

    half = ROPE // 2
    freqs = ROPE_THETA ** (-jnp.arange(half, dtype=F32) / half)
    ang = positions[0].astype(F32)[:, None] * freqs
    cos, sin = jnp.cos(ang), jnp.sin(ang)
    one = jnp.ones((S, NOPE), F32)
    zero = jnp.zeros((S, NOPE), F32)
    zh = jnp.zeros((S, half), F32)
    tabs = (jnp.concatenate([one, cos, cos, one[:, :HEAD_PAD - QK_DIM]], axis=1),
            jnp.concatenate([zero, zh, sin, zero[:, :HEAD_PAD - QK_DIM]], axis=1),
            jnp.concatenate([zero, -sin, zh, zero[:, :HEAD_PAD - QK_DIM]], axis=1))
    g_qh_p = _pad_cols(g_q_head, HEAD_PAD)
    g_kh_p = _pad_cols(g_k_head, HEAD_PAD)

    h1 = _rmsmod(x, g_norm1, ada, SC1, SH1, name="rmsmod1")
    proj = _mm(h1, w_in_p, name="mm_proj", tn=640)
    cqn, ckvn = _latent_norm(proj, g_q_latent, g_kv_latent, ql)
    q0 = _mm(cqn, w_uq_p, name="mm_q_up")
    kv0 = _mm(ckvn, w_ukv_p, name="mm_kv_up")
    q = _q_prep(q0, g_qh_p, tabs, nh)
    k = _k_prep(kv0, proj, kpe_col, g_kh_p, tabs, nh)
    y_a, lse = _mla_fwd(q, k, kv0, nh)
    y_b, sb_runs = _sb_fwd(proj, q_col, k_col, v_col, nh)
    late_srcs, late_lands = _split_wait(late_send, late_recv, late_srcs, late_lands, y_b, _gather_plan,
                                        name="gather_late_wait")
    late_got = _gather_forward(late_lands, name="gather_late_forward")
    gathered.update({nm: fill_own(g8, own) for nm, g8, own in zip(late, late_got, late_srcs)})
    w_pm = full_cols("w_proj_mla")
    w_ps = full_cols("w_proj_sb")
    w_o = gathered["w_out"].reshape(D, D)
    ib = 256 if (dff // 2) % 256 == 0 else LANE
    nb = dff // ib
    w_fi = full_cols("w_ffn_in").reshape(D, 2, nb, ib).transpose(0, 2, 1, 3).reshape(D, 2 * dff)
    w_fo = gathered["w_ffn_out"].reshape(dff, D)
    pa = _mm(y_a, w_pm, name="mm_proj_mla", out_dtype=BF16)
    pb = _mm(y_b, w_ps, name="mm_proj_sb", out_dtype=BF16)
    merged = _gate_merge(pa, pb, proj, gla_col, glb_col)
    o = _mm(merged, w_o, name="mm_out")
    x2, h2 = _resid_rmsmod(x, o, g_norm2, ada, GT1, SC2, SH2)
    ff, act = _mm(h2, w_fi, name="mm_ffn_in", tn=4 * ib,
                  fused=(_swiglu_tile(ib), [], [(2 * dff, 4 * ib, BF16), (dff, 2 * ib, BF16)]))
    f = _mm(act, w_fo, name="mm_ffn_out")
    dy, df, red_l, loss_p = _loss_head(x2, f, tgt, ada, GT2)

    dff_, = _mm(df, w_fo, name="mm_d_act", tb=True, tn=2 * ib,
                fused=(_swiglu_bwd_tile(ib), [(ff, 4 * ib)], [(2 * dff, 4 * ib, BF16)]))
    def pc(kind):
        if kind == "cols":
            return kind
        return kind if (D // 4) % LANE == 0 and (dff // 4) % LANE == 0 else None

    gw_fo = _mm(act, df, name="mm_gw_ffn_out", ta=True, out_dtype=BF16, pieces=pc("rows"))
    dh2 = _mm(dff_, w_fi, name="mm_d_h2", tb=True)
    gw_fi = _mm(h2, dff_, name="mm_gw_ffn_in", ta=True, out_dtype=BF16, pieces="cols", tn=ib,
                col_perm=lambda jj: jj // 2 + nb * (jj % 2))

    def pair_sums(nms, grads, tag):
        pcs = [g if g.ndim == 4 else (_rows_to_pieces if nm in row_sharded else _cols_to_pieces)(g)
               for nm, g in zip(nms, grads)]
        got = _sibling_swap(pcs, name="rs_sibling_swap_" + tag)
        return [_pair_sum(p, r, c_idx, name="rs_pair_sum_" + nm) for p, r, nm in zip(pcs, got, nms)]

    ffn = ["w_ffn_in", "w_ffn_out"]
    ffn_pair = pair_sums(ffn, [gw_fi, gw_fo], "ffn")
    ffn_send, ffn_recv, ffn_pair, ffn_lands, ffn_token = _split_start(
        ffn_pair, [jax.ShapeDtypeStruct((3,) + p.shape[1:], p.dtype) for p in ffn_pair], _exchange_plan, 3,
        name="rs_exchange_ffn_start")
    ada = ada + ffn_token[0:1, 0:1]
    dx2, do, red_2 = _rmsmod2_bwd(dh2, x2, dy, o, g_norm2, ada, SC2, GT1)
    dmerged = _mm(do, w_o, name="mm_d_merged", tb=True, out_dtype=BF16)
    gw_o = _mm(merged, do, name="mm_gw_out", ta=True, out_dtype=BF16, pieces=pc("rows"))
    dpa, dpb, dgla, dglb = _gate_bwd(dmerged, pa, pb, proj, gla_col, glb_col)
    dya = _mm(dpa, w_pm, name="mm_d_ya", tb=True, out_dtype=BF16)
    gw_pm = _mm(y_a, dpa, name="mm_gw_proj_mla", ta=True, out_dtype=BF16, pieces=pc("cols"))
    dyb = _mm(dpb, w_ps, name="mm_d_yb", tb=True, out_dtype=BF16)
    gw_ps = _mm(y_b, dpb, name="mm_gw_proj_sb", ta=True, out_dtype=BF16, pieces=pc("cols"))
    mid = ["w_proj_mla", "w_proj_sb", "w_out"]
    mid_pair = pair_sums(mid, [gw_pm, gw_ps, gw_o], "mid")
    mid_send, mid_recv, mid_pair, mid_lands, mid_token = _split_start(
        mid_pair, [jax.ShapeDtypeStruct((3,) + p.shape[1:], p.dtype) for p in mid_pair], _exchange_plan, 3,
        name="rs_exchange_mid_start")
    lse = lse + mid_token[0:1, 0:1]
    dq, dk, dv = _mla_bwd(q, k, kv0, y_a, dya, lse, nh)
    dq_sb, dk_sb, dv_sb = _sb_bwd(proj, q_col, k_col, v_col, dyb, sb_runs, nh)
    dq0, red_qh = _q_prep_bwd(dq, q0, g_qh_p, tabs, nh)
    dkv0, dkpe, red_kh = _k_prep_bwd(dk, dv, kv0, proj, kpe_col, g_kh_p, tabs, nh)
    dcqn = _mm(dq0, w_uq_p, name="mm_d_cqn", tb=True, out_dtype=BF16)
    gw_uq_p = _mm(cqn, dq0, name="mm_gw_uq", ta=True, out_dtype=BF16)
    dckvn = _mm(dkv0, w_ukv_p, name="mm_d_ckvn", tb=True, out_dtype=BF16)
    gw_ukv_p = _mm(ckvn, dkv0, name="mm_gw_ukv", ta=True, out_dtype=BF16)
    dcq, dckv, red_lat = _latent_norm_bwd(dcqn, dckvn, proj, g_q_latent, g_kv_latent, ql)
    dproj = jnp.concatenate([dcq, dckv, dq_sb.astype(BF16), dk_sb.astype(BF16), dv_sb.astype(BF16),
                             dgla, dglb, dkpe], axis=1)
    gw_in_p = _mm(h1, dproj, name="mm_gw_in", ta=True, out_dtype=BF16, tn=640)

    gw_in = _w_in_grad_pieces(gw_in_p, kpe0)
    gw_uq = gw_uq_p.reshape(ql, nh, HEAD_PAD)[:, :, :QK_DIM].reshape(ql, nh * QK_DIM)
    gw_ukv = jnp.concatenate([gw_ukv_p[:, :mlaw].reshape(ql, nh, HEAD), gw_ukv_p[:, mlaw:].reshape(ql, nh, HEAD)],
                             axis=2).reshape(ql, 2 * mlaw)
    last = ["w_in", "w_uq", "w_ukv"]
    assert last + mid + ffn == names

    last_pair = pair_sums(last, [gw_in, gw_uq, gw_ukv], "last")
    last_send, last_recv, last_pair, last_lands, last_token = _split_start(
        last_pair, [jax.ShapeDtypeStruct((3,) + p.shape[1:], p.dtype) for p in last_pair], _exchange_plan, 3,
        name="rs_exchange_last_start")
    ada = ada + last_token[0:1, 0:1]
    dh1 = _mm(dproj, w_in_p, name="mm_d_h1", tb=True, bias=jnp.zeros((1, D), F32) + last_token[0:1, 0:1])
    grad_x, red_1 = _rmsmod1_bwd(dh1, x, dx2, g_norm1, ada, SC1)
    last_pair, last_chips = _split_wait(last_send, last_recv, last_pair, last_lands, grad_x, _exchange_plan,
                                        name="rs_exchange_last_wait")
    mid_pair, mid_chips = _split_wait(mid_send, mid_recv, mid_pair, mid_lands, grad_x, _exchange_plan,
                                      name="rs_exchange_mid_wait")
    ffn_pair, ffn_chips = _split_wait(ffn_send, ffn_recv, ffn_pair, ffn_lands, grad_x, _exchange_plan,
                                      name="rs_exchange_ffn_wait")
    reduced = [_chip_sum(s, r, chip_idx, name="rs_chip_sum_" + nm)
               for s, r, nm in zip(last_pair + mid_pair + ffn_pair, last_chips + mid_chips + ffn_chips, names)]
    from_sibling2 = _sibling_swap(reduced, name="rs_sibling_send", whole=True)

    vec_names = ["b_ada", "g_norm1", "g_norm2", "g_q_latent", "g_kv_latent", "g_q_head", "g_k_head"]
    vec_w = dict(b_ada=b_ada, g_norm1=g_norm1, g_norm2=g_norm2, g_q_latent=g_q_latent, g_kv_latent=g_kv_latent,
                 g_q_head=g_q_head, g_k_head=g_k_head)
    vec_m = dict(b_ada=m_b_ada, g_norm1=m_g_norm1, g_norm2=m_g_norm2, g_q_latent=m_g_q_latent,
                 g_kv_latent=m_g_kv_latent, g_q_head=m_g_q_head, g_k_head=m_g_k_head)
    vec_v = dict(b_ada=v_b_ada, g_norm1=v_g_norm1, g_norm2=v_g_norm2, g_q_latent=v_g_q_latent,
                 g_kv_latent=v_g_kv_latent, g_q_head=v_g_q_head, g_k_head=v_g_k_head)
    d_ada = jnp.concatenate([red_1[0:1], red_1[1:2], red_2[3:4], red_2[0:1], red_2[1:2], red_l[0:1]], axis=1)
    vec_parts = dict(b_ada=d_ada, g_norm1=red_1[2:3], g_norm2=red_2[2:3], g_q_latent=red_lat[0:1],
                     g_kv_latent=red_lat[1:2], g_q_head=red_qh[0:1], g_k_head=red_kh[0:1])
    widths = [-(-vec_w[nm].shape[1] // LANE) * LANE for nm in vec_names]
    offs = [sum(widths[:i]) for i in range(len(widths))]
    pack = lambda d: jnp.concatenate([_pad_cols(d[nm][:, :vec_w[nm].shape[1]], wd) for nm, wd in zip(vec_names, widths)], axis=1)
    nvec = sum(widths) + LANE
    no_loss = jnp.zeros((1, LANE), F32)
    parts = jnp.concatenate([pack(vec_parts), loss_p[0:1, :]], axis=1)
    parts_all = _gather_blocks([jnp.broadcast_to(parts, (8, nvec))], name="gather_vec_grads",
                               in_vmem=True)[0][:, 0, :]
    gvec, dvec, nmvec, nvvec = _adamw_vec(parts_all, *[jnp.concatenate([pack(d), no_loss], axis=1)
                                                       for d in (vec_w, vec_m, vec_v)])
    loss = gvec[0, nvec - LANE]
    unpack = lambda a: {nm: a[:, o_:o_ + vec_w[nm].shape[1]] for nm, o_ in zip(vec_names, offs)}
    gvec, dvec, nmvec, nvvec = unpack(gvec), unpack(dvec), unpack(nmvec), unpack(nvvec)

    dada_all = lax.dynamic_slice_in_dim(parts_all[:, :6 * D], chip * n_ada, n_ada, axis=1)
    cact_t = jax.nn.silu(c_all).T
    g_ada, d_ada_w, nm_ada, nv_ada = _adamw_ada(cact_t, dada_all, w_ada[0], m_w_ada[0], v_w_ada[0])

    ms = dict(w_in=m_w_in, w_uq=m_w_uq, w_ukv=m_w_ukv, w_proj_mla=m_w_proj_mla, w_proj_sb=m_w_proj_sb,
              w_out=m_w_out, w_ffn_in=m_w_ffn_in, w_ffn_out=m_w_ffn_out)
    vs = dict(w_in=v_w_in, w_uq=v_w_uq, w_ukv=v_w_ukv, w_proj_mla=v_w_proj_mla, w_proj_sb=v_w_proj_sb,
              w_out=v_w_out, w_ffn_in=v_w_ffn_in, w_ffn_out=v_w_ffn_out)
    G, DL, NM, NV = {}, {}, {}, {}
    for nm, mine, other in zip(names, reduced, from_sibling2):
        g_, d_, m_, v_ = _adamw(mats[nm], mine, other, c_idx, ms[nm][0], vs[nm][0], name="adamw_" + nm)
        G[nm], DL[nm], NM[nm], NV[nm] = g_[None], d_[None], m_[None], v_[None]
    G["w_ada"], DL["w_ada"], NM["w_ada"], NV["w_ada"] = g_ada[None], d_ada_w[None], nm_ada[None], nv_ada[None]
    for nm in vec_names:
        G[nm], DL[nm], NM[nm], NV[nm] = gvec[nm], dvec[nm], nmvec[nm], nvvec[nm]

    order = ["w_ada", "b_ada", "g_norm1", "g_norm2", "w_in", "g_q_latent", "g_kv_latent", "w_uq", "w_ukv",
             "g_q_head", "g_k_head", "w_proj_mla", "w_proj_sb", "w_out", "w_ffn_in", "w_ffn_out"]
    return (loss, grad_x[None], *[G[n] for n in order], *[DL[n] for n in order],
            *[NM[n] for n in order], *[NV[n] for n in order])
```

```python
import functools
import math

import jax
import jax.numpy as jnp
from jax import lax
from jax.experimental import pallas as pl
from jax.experimental.pallas import tpu as pltpu

F32 = jnp.float32
BF16 = jnp.bfloat16
MESH = pl.DeviceIdType.MESH

EPS = 1e-6
ROPE_THETA = 10000.0
NOPE = 128
ROPE = 64
QK_DIM = NOPE + ROPE
HEAD_PAD = 256
HEAD = 128
N_DEV = 8
LANE = 128
VMEM_LIMIT = 48 * 1024 * 1024

ADAM_LR = 0.001
ADAM_B1 = 0.9
ADAM_B2 = 0.999
ADAM_EPS = 1e-08
ADAM_WD = 0.01
ADAM_STEP = 10


def _tile(n, target):
    if n <= target:
        return n
    t = (target // LANE) * LANE
    while t >= LANE:
        if n % t == 0:
            return t
        t -= LANE
    return n


def _row_tile(rows, row_bytes, budget=24 * 1024 * 1024):
    cap = max(8, budget // (2 * row_bytes))
    best = None
    for t in range(8, min(rows, cap) + 1, 8):
        if rows % t == 0:
            best = t
    return best if best is not None else rows


def _params(sem):
    return pltpu.CompilerParams(dimension_semantics=sem, vmem_limit_bytes=VMEM_LIMIT)


def _rows(tm, w, col=0):
    return pl.BlockSpec((tm, w), lambda i: (i, col))


def _vec(w, col=0, rows=1):
    return pl.BlockSpec((rows, w), lambda i: (0, col))


MM_VMEM_BUDGET = 36 * 1024 * 1024


def _mm(a, b, *, name, ta=False, tb=False, out_dtype=F32, a_fn=None, bias=None, tm=1024, tn=1024, pieces=None,
        col_perm=None, fused=None):
    M = a.shape[1] if ta else a.shape[0]
    K = a.shape[0] if ta else a.shape[1]
    N = b.shape[0] if tb else b.shape[1]
    assert K == (b.shape[1] if tb else b.shape[0]), (a.shape, b.shape, ta, tb)
    if pieces == "cols":
        tm, tn = _tile(M // 2, tm), _tile(N // 4, tn)
        assert (M // 2) % tm == 0 and (N // 4) % tn == 0
    elif pieces == "rows":
        tm, tn = M // 4, _tile(N, tn)
    else:
        tm, tn = _tile(M, tm), _tile(N, tn)
    sa, sb, so = a.dtype.itemsize, b.dtype.itemsize, jnp.dtype(out_dtype).itemsize

    def fits(tk):
        return 2 * tk * (tm * sa + tn * sb) + tm * tn * (2 * so + 4) <= MM_VMEM_BUDGET

    tk = K
    while not fits(tk):
        smaller = _tile(K, tk - LANE)
        if smaller >= tk:
            break
        tk = smaller
    nk = K // tk
    dn = (((0 if ta else 1,), (1 if tb else 0,)), ((), ()))
    b_outer = nk == 1 and a.size * sa * (N // tn) < b.size * sb * (M // tm)

    n_extra = len(fused[1]) if fused else 0
    n_out = len(fused[2]) if fused else 1

    def body(*refs):
        a_ref, b_ref = refs[:2]
        bias_ref = refs[2] if bias is not None else None
        first = 3 if bias is not None else 2
        extra_refs = refs[first:first + n_extra]
        out_refs = refs[first + n_extra:first + n_extra + n_out]
        o_ref = out_refs[0]
        av = a_ref[...]
        if a_fn is not None:
            av = a_fn(av.astype(F32))
        part = lax.dot_general(av.astype(BF16), b_ref[...].astype(BF16), dn, preferred_element_type=F32)

        def finish(r):
            if bias is not None:
                r = r + bias_ref[...]
            if fused:
                for ref, tile in zip(out_refs, fused[0](r, *[e[...] for e in extra_refs])):
                    ref[...] = tile.astype(ref.dtype)
            elif pieces == "rows":
                o_ref[0] = r[:tm // 2].astype(o_ref.dtype)
                o_ref[1] = r[tm // 2:].astype(o_ref.dtype)
            else:
                o_ref[...] = r.astype(o_ref.dtype)

        if nk == 1:
            finish(part)
        else:
            acc_ref = refs[-1]
            k = pl.program_id(2)

            @pl.when(k == 0)
            def _():
                acc_ref[...] = part

            @pl.when(k > 0)
            def _():
                acc_ref[...] += part

            @pl.when(k == nk - 1)
            def _():
                finish(acc_ref[...])

    def ij(g0, g1):
        return (g1, g0) if b_outer else (g0, g1)

    def amap(g0, g1, k):
        i, _ = ij(g0, g1)
        return (k, i) if ta else (i, k)

    def bmap(g0, g1, k):
        _, j = ij(g0, g1)
        return (j, k) if tb else (k, j)

    in_specs = [pl.BlockSpec((tk, tm) if ta else (tm, tk), amap), pl.BlockSpec((tn, tk) if tb else (tk, tn), bmap)]
    args = [a, b]
    if bias is not None:
        in_specs.append(pl.BlockSpec((1, tn), lambda g0, g1, k: (0, ij(g0, g1)[1])))
        args.append(bias)
    grid = (N // tn, M // tm, nk) if b_outer else (M // tm, N // tn, nk)
    if pieces == "cols":
        ni, nj = M // 2 // tm, N // 4 // tn

        def omap(g0, g1, k):
            i, j = ij(g0, g1)
            j = col_perm(j) if col_perm else j
            return (i // ni, j // nj, i % ni, j % nj)

        out_spec = pl.BlockSpec((None, None, tm, tn), omap)
        out_shape = jax.ShapeDtypeStruct((2, 4, M // 2, N // 4), out_dtype)
    elif pieces == "rows":
        out_spec = pl.BlockSpec((2, None, tm // 2, tn), lambda g0, g1, k: (0, ij(g0, g1)[0], 0, ij(g0, g1)[1]))
        out_shape = jax.ShapeDtypeStruct((2, 4, tm // 2, N), out_dtype)
    else:
        out_spec = pl.BlockSpec((tm, tn), lambda g0, g1, k: ij(g0, g1))
        out_shape = jax.ShapeDtypeStruct((M, N), out_dtype)
    if fused:
        for arr, width in fused[1]:
            in_specs.append(pl.BlockSpec((tm, width), lambda g0, g1, k: ij(g0, g1)))
            args.append(arr)
        out_spec = [pl.BlockSpec((tm, width), lambda g0, g1, k: ij(g0, g1)) for _, width, _ in fused[2]]
        out_shape = [jax.ShapeDtypeStruct((M, cols), dt) for cols, _, dt in fused[2]]
    return pl.pallas_call(
        body, name=name, grid=grid, in_specs=in_specs, out_specs=out_spec, out_shape=out_shape,
        scratch_shapes=[pltpu.VMEM((tm, tn), F32)] if nk > 1 else [],
        compiler_params=_params(("parallel", "parallel", "arbitrary")),
    )(*args)


def _rms_rows(v):
    return lax.rsqrt(jnp.mean(v * v, axis=-1, keepdims=True) + EPS)


def _rmsmod(x, g, ada, sc_col, sh_col, *, name):
    S, D = x.shape
    tm = _tile(S, 256)

    def body(x_ref, g_ref, sc_ref, sh_ref, h_ref):
        xv = x_ref[...]
        h = (xv * _rms_rows(xv) * g_ref[...]) * (1.0 + sc_ref[...]) + sh_ref[...]
        h_ref[...] = h.astype(h_ref.dtype)

    return pl.pallas_call(
        body, name=name, grid=(S // tm,),
        in_specs=[_rows(tm, D), _vec(D), _vec(D, sc_col), _vec(D, sh_col)],
        out_specs=_rows(tm, D), out_shape=jax.ShapeDtypeStruct((S, D), BF16),
        compiler_params=_params(("parallel",)),
    )(x, g, ada, ada)


def _latent_norm(proj, g_q, g_kv, ql):
    S = proj.shape[0]
    tm = _tile(S, 512)

    def body(cq_ref, ckv_ref, gq_ref, gkv_ref, oq_ref, okv_ref):
        cq = cq_ref[...]
        oq_ref[...] = (cq * _rms_rows(cq) * gq_ref[...]).astype(BF16)
        ckv = ckv_ref[...]
        okv_ref[...] = (ckv * _rms_rows(ckv) * gkv_ref[...]).astype(BF16)

    return pl.pallas_call(
        body, name="latent_norm", grid=(S // tm,),
        in_specs=[_rows(tm, ql, 0), _rows(tm, ql, 1), _vec(ql), _vec(ql)],
        out_specs=[_rows(tm, ql), _rows(tm, ql)],
        out_shape=[jax.ShapeDtypeStruct((S, ql), BF16)] * 2,
        compiler_params=_params(("parallel",)),
    )(proj, proj, g_q, g_kv)


def _rope_fwd(y, c, s1, s2):
    return y * c + pltpu.roll(y, ROPE // 2, 1) * s1 + pltpu.roll(y, HEAD_PAD - ROPE // 2, 1) * s2


def _rope_bwd(d, c, s1, s2):
    return d * c + pltpu.roll(d * s1, HEAD_PAD - ROPE // 2, 1) + pltpu.roll(d * s2, ROPE // 2, 1)


def _head_rms(v):
    return lax.rsqrt(jnp.sum(v * v, axis=-1, keepdims=True) * (1.0 / QK_DIM) + EPS)


def _q_prep(q0, g_qh, tabs, nh):
    S = q0.shape[0]
    tm = _tile(S, 256)

    def body(q_ref, g_ref, c_ref, s1_ref, s2_ref, o_ref):
        c, s1, s2, g = c_ref[...], s1_ref[...], s2_ref[...], g_ref[...]
        for h in range(nh):
            sl = slice(h * HEAD_PAD, (h + 1) * HEAD_PAD)
            xs = q_ref[:, sl]
            o_ref[:, sl] = (_rope_fwd(xs * _head_rms(xs) * g, c, s1, s2) * (QK_DIM ** -0.5)).astype(BF16)

    w = nh * HEAD_PAD
    return pl.pallas_call(
        body, name="mla_q_prep", grid=(S // tm,),
        in_specs=[_rows(tm, w), _vec(HEAD_PAD)] + [_rows(tm, HEAD_PAD)] * 3,
        out_specs=_rows(tm, w), out_shape=jax.ShapeDtypeStruct((S, w), BF16),
        compiler_params=_params(("parallel",)),
    )(q0, g_qh, *tabs)


def _k_prep(kv0, proj, kpe_col, g_kh, tabs, nh):
    S = kv0.shape[0]
    tm = _tile(S, 256)

    def body(kv_ref, kpe_ref, g_ref, c_ref, s1_ref, s2_ref, o_ref):
        c, s1, s2, g = c_ref[...], s1_ref[...], s2_ref[...], g_ref[...]
        kpe = kpe_ref[...]
        for h in range(nh):
            k0 = jnp.concatenate([kv_ref[:, h * HEAD:(h + 1) * HEAD], kpe], axis=1)
            o_ref[:, h * HEAD_PAD:(h + 1) * HEAD_PAD] = _rope_fwd(k0 * _head_rms(k0) * g, c, s1, s2).astype(BF16)

    return pl.pallas_call(
        body, name="mla_k_prep", grid=(S // tm,),
        in_specs=[_rows(tm, nh * HEAD, 0), _rows(tm, LANE, kpe_col), _vec(HEAD_PAD)] + [_rows(tm, HEAD_PAD)] * 3,
        out_specs=_rows(tm, nh * HEAD_PAD), out_shape=jax.ShapeDtypeStruct((S, nh * HEAD_PAD), BF16),
        compiler_params=_params(("parallel",)),
    )(kv0, proj, g_kh, *tabs)


def _gate_merge(pa, pb, proj, gla_col, glb_col):
    S, D = pa.shape
    tm = _tile(S, 256)

    def body(pa_ref, pb_ref, ga_ref, gb_ref, o_ref):
        o_ref[...] = (jax.nn.sigmoid(ga_ref[...]) * pa_ref[...] + jax.nn.sigmoid(gb_ref[...]) * pb_ref[...]).astype(BF16)

    return pl.pallas_call(
        body, name="gate_merge", grid=(S // tm,),
        in_specs=[_rows(tm, D), _rows(tm, D), _rows(tm, D, gla_col), _rows(tm, D, glb_col)],
        out_specs=_rows(tm, D), out_shape=jax.ShapeDtypeStruct((S, D), BF16),
        compiler_params=_params(("parallel",)),
    )(pa, pb, proj, proj)


def _resid_rmsmod(x, o, g, ada, gt_col, sc_col, sh_col):
    S, D = x.shape
    tm = _tile(S, 256)

    def body(x_ref, o_ref, g_ref, gt_ref, sc_ref, sh_ref, x2_ref, h_ref):
        x2 = x_ref[...] + gt_ref[...] * o_ref[...]
        x2_ref[...] = x2
        h_ref[...] = ((x2 * _rms_rows(x2) * g_ref[...]) * (1.0 + sc_ref[...]) + sh_ref[...]).astype(BF16)

    return pl.pallas_call(
        body, name="resid_rmsmod2", grid=(S // tm,),
        in_specs=[_rows(tm, D), _rows(tm, D), _vec(D), _vec(D, gt_col), _vec(D, sc_col), _vec(D, sh_col)],
        out_specs=[_rows(tm, D), _rows(tm, D)],
        out_shape=[jax.ShapeDtypeStruct((S, D), F32), jax.ShapeDtypeStruct((S, D), BF16)],
        compiler_params=_params(("parallel",)),
    )(x, o, g, ada, ada, ada)


def _swiglu_tile(ib):
    def fn(r):
        pairs = r.shape[1] // (2 * ib)
        act = [jax.nn.silu(r[:, 2 * p * ib:(2 * p + 1) * ib]) * r[:, (2 * p + 1) * ib:(2 * p + 2) * ib] for p in range(pairs)]
        return r, jnp.concatenate(act, axis=1) if pairs > 1 else act[0]
    return fn


def _swiglu_bwd_tile(ib):
    def fn(d, ff):
        ff = ff.astype(F32)
        out = []
        for p in range(d.shape[1] // ib):
            dp = d[:, p * ib:(p + 1) * ib]
            g = ff[:, 2 * p * ib:(2 * p + 1) * ib]
            u = ff[:, (2 * p + 1) * ib:(2 * p + 2) * ib]
            sg = jax.nn.sigmoid(g)
            out += [dp * u * sg * (1.0 + g * (1.0 - sg)), dp * g * sg]
        return (jnp.concatenate(out, axis=1),)
    return fn


def _loss_head(x2, f, tgt, ada, gt_col):
    S, D = x2.shape
    tm = _tile(S, 256)

    def body(x2_ref, f_ref, t_ref, gt_ref, dy_ref, df_ref, red_ref, loss_ref):
        @pl.when(pl.program_id(0) == 0)
        def _():
            red_ref[...] = jnp.zeros_like(red_ref)
            loss_ref[...] = jnp.zeros_like(loss_ref)

        fv = f_ref[...]
        gt = gt_ref[...]
        err = x2_ref[...] + gt * fv - t_ref[...]
        dy = err * (1.0 / D)
        dy_ref[...] = dy
        df_ref[...] = (dy * gt).astype(BF16)
        red_ref[0:1, :] += jnp.sum(dy * fv, axis=0, keepdims=True)
        loss_ref[...] += (0.5 / D) * jnp.sum(err * err)

    return pl.pallas_call(
        body, name="loss_head", grid=(S // tm,),
        in_specs=[_rows(tm, D), _rows(tm, D), _rows(tm, D), _vec(D, gt_col)],
        out_specs=[_rows(tm, D), _rows(tm, D), _vec(D, rows=8), _vec(LANE, rows=8)],
        out_shape=[jax.ShapeDtypeStruct((S, D), F32), jax.ShapeDtypeStruct((S, D), BF16),
                   jax.ShapeDtypeStruct((8, D), F32), jax.ShapeDtypeStruct((8, LANE), F32)],
        compiler_params=_params(("arbitrary",)),
    )(x2, f, tgt, ada)


def _rmsmod2_bwd(dh2, x2, dy, o, g, ada, sc_col, gt_col):
    S, D = x2.shape
    tm = _tile(S, 256)

    def body(dh_ref, x2_ref, dy_ref, o_ref, g_ref, sc_ref, gt_ref, dx_ref, do_ref, red_ref):
        @pl.when(pl.program_id(0) == 0)
        def _():
            red_ref[...] = jnp.zeros_like(red_ref)

        dh = dh_ref[...]
        x2 = x2_ref[...]
        gv = g_ref[...]
        mod = 1.0 + sc_ref[...]
        r = _rms_rows(x2)
        xn = x2 * r
        t = dh * xn
        red_ref[0:1, :] += jnp.sum(dh, axis=0, keepdims=True)
        red_ref[1:2, :] += jnp.sum(t * gv, axis=0, keepdims=True)
        red_ref[2:3, :] += jnp.sum(t * mod, axis=0, keepdims=True)
        dxn = dh * gv * mod
        dx = dy_ref[...] + r * (dxn - xn * jnp.mean(dxn * xn, axis=-1, keepdims=True))
        dx_ref[...] = dx
        red_ref[3:4, :] += jnp.sum(dx * o_ref[...], axis=0, keepdims=True)
        do_ref[...] = (dx * gt_ref[...]).astype(BF16)

    return pl.pallas_call(
        body, name="rmsmod2_bwd", grid=(S // tm,),
        in_specs=[_rows(tm, D)] * 4 + [_vec(D), _vec(D, sc_col), _vec(D, gt_col)],
        out_specs=[_rows(tm, D), _rows(tm, D), _vec(D, rows=8)],
        out_shape=[jax.ShapeDtypeStruct((S, D), F32), jax.ShapeDtypeStruct((S, D), BF16),
                   jax.ShapeDtypeStruct((8, D), F32)],
        compiler_params=_params(("arbitrary",)),
    )(dh2, x2, dy, o, g, ada, ada)


def _rmsmod1_bwd(dh, x, dx2, g, ada, sc_col):
    S, D = x.shape
    tm = _tile(S, 256)

    def body(dh_ref, x_ref, dx2_ref, g_ref, sc_ref, gx_ref, red_ref):
        @pl.when(pl.program_id(0) == 0)
        def _():
            red_ref[...] = jnp.zeros_like(red_ref)

        dh = dh_ref[...]
        xv = x_ref[...]
        gv = g_ref[...]
        mod = 1.0 + sc_ref[...]
        r = _rms_rows(xv)
        xn = xv * r
        t = dh * xn
        red_ref[0:1, :] += jnp.sum(dh, axis=0, keepdims=True)
        red_ref[1:2, :] += jnp.sum(t * gv, axis=0, keepdims=True)
        red_ref[2:3, :] += jnp.sum(t * mod, axis=0, keepdims=True)
        dxn = dh * gv * mod
        gx_ref[...] = dx2_ref[...] + r * (dxn - xn * jnp.mean(dxn * xn, axis=-1, keepdims=True))

    return pl.pallas_call(
        body, name="rmsmod1_bwd", grid=(S // tm,),
        in_specs=[_rows(tm, D)] * 3 + [_vec(D), _vec(D, sc_col)],
        out_specs=[_rows(tm, D), _vec(D, rows=8)],
        out_shape=[jax.ShapeDtypeStruct((S, D), F32), jax.ShapeDtypeStruct((8, D), F32)],
        compiler_params=_params(("arbitrary",)),
    )(dh, x, dx2, g, ada)


def _gate_bwd(dm, pa, pb, proj, gla_col, glb_col):
    S, D = pa.shape
    tm = _tile(S, 256)

    def body(dm_ref, pa_ref, pb_ref, la_ref, lb_ref, dpa_ref, dpb_ref, dla_ref, dlb_ref):
        dm_ = dm_ref[...]
        ga = jax.nn.sigmoid(la_ref[...])
        gb = jax.nn.sigmoid(lb_ref[...])
        dpa_ref[...] = (dm_ * ga).astype(BF16)
        dpb_ref[...] = (dm_ * gb).astype(BF16)
        dla_ref[...] = (dm_ * pa_ref[...] * ga * (1.0 - ga)).astype(BF16)
        dlb_ref[...] = (dm_ * pb_ref[...] * gb * (1.0 - gb)).astype(BF16)

    return pl.pallas_call(
        body, name="gate_bwd", grid=(S // tm,),
        in_specs=[_rows(tm, D)] * 3 + [_rows(tm, D, gla_col), _rows(tm, D, glb_col)],
        out_specs=[_rows(tm, D)] * 4, out_shape=[jax.ShapeDtypeStruct((S, D), BF16)] * 4,
        compiler_params=_params(("parallel",)),
    )(dm, pa, pb, proj, proj)


def _q_prep_bwd(dq, q0, g_qh, tabs, nh):
    S = q0.shape[0]
    tm = _tile(S, 256)

    def body(dq_ref, q_ref, g_ref, c_ref, s1_ref, s2_ref, o_ref, red_ref):
        @pl.when(pl.program_id(0) == 0)
        def _():
            red_ref[...] = jnp.zeros_like(red_ref)

        c, s1, s2, g = c_ref[...], s1_ref[...], s2_ref[...], g_ref[...]
        dg = jnp.zeros((1, HEAD_PAD), F32)
        for h in range(nh):
            sl = slice(h * HEAD_PAD, (h + 1) * HEAD_PAD)
            d1 = _rope_bwd(dq_ref[:, sl], c, s1, s2)
            xs = q_ref[:, sl]
            r = _head_rms(xs)
            qn = xs * r
            dg = dg + jnp.sum(d1 * qn, axis=0, keepdims=True)
            dn = d1 * g
            o_ref[:, sl] = (r * (dn - qn * (jnp.sum(dn * qn, axis=-1, keepdims=True) * (1.0 / QK_DIM)))).astype(BF16)
        red_ref[0:1, :] += dg

    w = nh * HEAD_PAD
    return pl.pallas_call(
        body, name="mla_q_prep_bwd", grid=(S // tm,),
        in_specs=[_rows(tm, w), _rows(tm, w), _vec(HEAD_PAD)] + [_rows(tm, HEAD_PAD)] * 3,
        out_specs=[_rows(tm, w), _vec(HEAD_PAD, rows=8)],
        out_shape=[jax.ShapeDtypeStruct((S, w), BF16), jax.ShapeDtypeStruct((8, HEAD_PAD), F32)],
        compiler_params=_params(("arbitrary",)),
    )(dq, q0, g_qh, *tabs)


def _k_prep_bwd(dk, dv, kv0, proj, kpe_col, g_kh, tabs, nh):
    S = kv0.shape[0]
    tm = _tile(S, 256)
    wv = nh * HEAD

    def body(dk_ref, dv_ref, kv_ref, kpe_ref, g_ref, c_ref, s1_ref, s2_ref, o_ref, dpe_ref, red_ref):
        @pl.when(pl.program_id(0) == 0)
        def _():
            red_ref[...] = jnp.zeros_like(red_ref)

        c, s1, s2, g = c_ref[...], s1_ref[...], s2_ref[...], g_ref[...]
        kpe = kpe_ref[...]
        dg = jnp.zeros((1, HEAD_PAD), F32)
        dpe = jnp.zeros((tm, LANE), F32)
        for h in range(nh):
            d1 = _rope_bwd(dk_ref[:, h * HEAD_PAD:(h + 1) * HEAD_PAD], c, s1, s2)
            k0 = jnp.concatenate([kv_ref[:, h * HEAD:(h + 1) * HEAD], kpe], axis=1)
            r = _head_rms(k0)
            kn = k0 * r
            dg = dg + jnp.sum(d1 * kn, axis=0, keepdims=True)
            dn = d1 * g
            dk0 = r * (dn - kn * (jnp.sum(dn * kn, axis=-1, keepdims=True) * (1.0 / QK_DIM)))
            o_ref[:, h * HEAD:(h + 1) * HEAD] = dk0[:, :HEAD].astype(BF16)
            dpe = dpe + dk0[:, HEAD:]
        o_ref[:, wv:] = dv_ref[...].astype(BF16)
        dpe_ref[...] = dpe.astype(BF16)
        red_ref[0:1, :] += dg

    return pl.pallas_call(
        body, name="mla_k_prep_bwd", grid=(S // tm,),
        in_specs=[_rows(tm, nh * HEAD_PAD), _rows(tm, wv), _rows(tm, wv, 0), _rows(tm, LANE, kpe_col),
                  _vec(HEAD_PAD)] + [_rows(tm, HEAD_PAD)] * 3,
        out_specs=[_rows(tm, 2 * wv), _rows(tm, LANE), _vec(HEAD_PAD, rows=8)],
        out_shape=[jax.ShapeDtypeStruct((S, 2 * wv), BF16), jax.ShapeDtypeStruct((S, LANE), BF16),
                   jax.ShapeDtypeStruct((8, HEAD_PAD), F32)],
        compiler_params=_params(("arbitrary",)),
    )(dk, dv, kv0, proj, g_kh, *tabs)


def _latent_norm_bwd(dcqn, dckvn, proj, g_q, g_kv, ql):
    S = proj.shape[0]
    tm = _tile(S, 512)

    def body(dq_ref, dkv_ref, cq_ref, ckv_ref, gq_ref, gkv_ref, oq_ref, okv_ref, red_ref):
        @pl.when(pl.program_id(0) == 0)
        def _():
            red_ref[...] = jnp.zeros_like(red_ref)

        for row, (d_ref, c_ref, g_ref, o_ref) in enumerate(((dq_ref, cq_ref, gq_ref, oq_ref),
                                                            (dkv_ref, ckv_ref, gkv_ref, okv_ref))):
            d = d_ref[...]
            cv = c_ref[...]
            r = _rms_rows(cv)
            ch = cv * r
            red_ref[row:row + 1, :] += jnp.sum(d * ch, axis=0, keepdims=True)
            dn = d * g_ref[...]
            o_ref[...] = (r * (dn - ch * jnp.mean(dn * ch, axis=-1, keepdims=True))).astype(BF16)

    return pl.pallas_call(
        body, name="latent_norm_bwd", grid=(S // tm,),
        in_specs=[_rows(tm, ql), _rows(tm, ql), _rows(tm, ql, 0), _rows(tm, ql, 1), _vec(ql), _vec(ql)],
        out_specs=[_rows(tm, ql), _rows(tm, ql), _vec(ql, rows=8)],
        out_shape=[jax.ShapeDtypeStruct((S, ql), BF16)] * 2 + [jax.ShapeDtypeStruct((8, ql), F32)],
        compiler_params=_params(("arbitrary",)),
    )(dcqn, dckvn, proj, proj, g_q, g_kv)


NEG = -1e30
ATT_TILE = 512
SB_TILE = 512
SB_SUB = 128
_NT = (((1,), (1,)), ((), ()))
_TN = (((0,), (0,)), ((), ()))


def _dot(a, b, dn=(((1,), (0,)), ((), ()))):
    return lax.dot_general(a, b, dn, preferred_element_type=F32)


def _key_rows(kb, t):
    return pl.ds(pl.multiple_of(kb * t, t), t)


def _diag_mask(t, strict):
    r = lax.broadcasted_iota(jnp.int32, (t, t), 0)
    c = lax.broadcasted_iota(jnp.int32, (t, t), 1)
    return c < r if strict else c <= r


def _mla_fwd(q, k, kv0, nh):
    S = q.shape[0]
    t = _tile(S, ATT_TILE)

    def body(q_ref, k_ref, v_ref, o_ref, lse_ref):
        i = pl.program_id(1)
        qv = q_ref[...]

        def block(kb, carry, masked):
            m, l, acc = carry
            rows = _key_rows(kb, t)
            s = _dot(qv, k_ref[rows, :], _NT)
            if masked:
                s = jnp.where(_diag_mask(t, False), s, NEG)
            m_new = jnp.maximum(m, jnp.max(s, axis=-1, keepdims=True))
            alpha = jnp.exp(m - m_new)
            p = jnp.exp(s - m_new)
            l = alpha * l + jnp.sum(p, axis=-1, keepdims=True)
            acc = alpha * acc + _dot(p.astype(BF16), v_ref[rows, :].astype(BF16))
            return m_new, l, acc

        init = (jnp.full((t, 1), NEG, F32), jnp.zeros((t, 1), F32), jnp.zeros((t, HEAD), F32))
        carry = lax.fori_loop(0, i, lambda kb, c: block(kb, c, False), init)
        m, l, acc = block(i, carry, True)
        o_ref[...] = acc / l
        lse_ref[...] = m + jnp.log(l)

    return pl.pallas_call(
        body, name="mla_attn_fwd", grid=(nh, S // t),
        in_specs=[pl.BlockSpec((t, HEAD_PAD), lambda h, i: (i, h)),
                  pl.BlockSpec((S, HEAD_PAD), lambda h, i: (0, h)),
                  pl.BlockSpec((S, HEAD), lambda h, i: (0, nh + h))],
        out_specs=[pl.BlockSpec((t, HEAD), lambda h, i: (i, h)),
                   pl.BlockSpec((None, t, 1), lambda h, i: (h, i, 0))],
        out_shape=[jax.ShapeDtypeStruct((S, nh * HEAD), F32), jax.ShapeDtypeStruct((nh, S, 1), F32)],
        compiler_params=_params(("parallel", "arbitrary")),
    )(q, k, kv0)


def _mla_bwd(q, k, kv0, o, do, lse, nh):
    S = q.shape[0]
    t = _tile(S, ATT_TILE)
    scale = QK_DIM ** -0.5

    def body(q_ref, k_ref, v_ref, o_ref, do_ref, lse_ref, dq_ref, dk_ref, dv_ref):
        i = pl.program_id(1)

        @pl.when(i == 0)
        def _():
            dk_ref[...] = jnp.zeros_like(dk_ref)
            dv_ref[...] = jnp.zeros_like(dv_ref)

        qv = q_ref[...]
        dov = do_ref[...]
        delta = jnp.sum(dov * o_ref[...], axis=-1, keepdims=True)
        dob = dov.astype(BF16)
        lse = lse_ref[...]

        def block(kb, dq, masked):
            rows = _key_rows(kb, t)
            ks = k_ref[rows, :]
            vs = v_ref[rows, :].astype(BF16)
            p = jnp.exp(_dot(qv, ks, _NT) - lse)
            if masked:
                p = jnp.where(_diag_mask(t, False), p, 0.0)
            ds = (p * (_dot(dob, vs, _NT) - delta)).astype(BF16)
            dk_ref[rows, :] += _dot(ds, qv, _TN)
            dv_ref[rows, :] += _dot(p.astype(BF16), dob, _TN)
            return dq + _dot(ds, ks)

        dq = lax.fori_loop(0, i, lambda kb, c: block(kb, c, False), jnp.zeros((t, HEAD_PAD), F32))
        dq_ref[...] = block(i, dq, True) * scale

    return pl.pallas_call(
        body, name="mla_attn_bwd", grid=(nh, S // t),
        in_specs=[pl.BlockSpec((t, HEAD_PAD), lambda h, i: (i, h)),
                  pl.BlockSpec((S, HEAD_PAD), lambda h, i: (0, h)),
                  pl.BlockSpec((S, HEAD), lambda h, i: (0, nh + h)),
                  pl.BlockSpec((t, HEAD), lambda h, i: (i, h)),
                  pl.BlockSpec((t, HEAD), lambda h, i: (i, h)),
                  pl.BlockSpec((None, t, 1), lambda h, i: (h, i, 0))],
        out_specs=[pl.BlockSpec((t, HEAD_PAD), lambda h, i: (i, h)),
                   pl.BlockSpec((S, HEAD_PAD), lambda h, i: (0, h)),
                   pl.BlockSpec((S, HEAD), lambda h, i: (0, h))],
        out_shape=[jax.ShapeDtypeStruct((S, nh * HEAD_PAD), F32), jax.ShapeDtypeStruct((S, nh * HEAD_PAD), F32),
                   jax.ShapeDtypeStruct((S, nh * HEAD), F32)],
        compiler_params=_params(("parallel", "arbitrary")),
    )(q, k, kv0, o, do, lse)


def _tri(n, cmp):
    r = lax.broadcasted_iota(jnp.int32, (n, n), 0)
    c = lax.broadcasted_iota(jnp.int32, (n, n), 1)
    return jnp.where(cmp(r, c), 1.0, 0.0).astype(BF16)


def _sb_block(qv, ks, run, upper, t, masked):
    z = _dot(qv, ks, _NT)
    lb = jnp.minimum(z, 0.0) - jnp.log(1.0 + jnp.exp(-jnp.abs(z)))
    lom = lb - z
    mask = _diag_mask(t, True) if masked else None
    if masked:
        lom = jnp.where(mask, lom, 0.0)
    tails = []
    for sblk in reversed(range(t // SB_SUB)):
        part = lom[:, sblk * SB_SUB:(sblk + 1) * SB_SUB]
        tails.append(_dot(part.astype(BF16), upper) + run)
        run = run + jnp.sum(part, axis=-1, keepdims=True)
    a = jnp.exp(lb + jnp.concatenate(tails[::-1], axis=1))
    if masked:
        a = jnp.where(mask, a, 0.0)
    return a, lb, mask, run


def _sb_fwd(proj, q_col, k_col, v_col, nh):
    S = proj.shape[0]
    t = _tile(S, SB_TILE)
    assert S // t <= LANE
    scale = HEAD ** -0.5

    def body(q_ref, k_ref, v_ref, o_ref, runs_ref):
        i = pl.program_id(1)
        qv = (q_ref[...] * scale).astype(BF16)
        upper = _tri(SB_SUB, lambda j, s: j > s)
        lane = lax.broadcasted_iota(jnp.int32, (t, LANE), 1)

        def block(kb, carry, masked):
            run, acc, runs = carry
            runs = jnp.where(lane == kb, run, runs)
            rows = _key_rows(kb, t)
            a, _, _, run = _sb_block(qv, k_ref[rows, :].astype(BF16), run, upper, t, masked)
            return run, acc + _dot(a.astype(BF16), v_ref[rows, :].astype(BF16)), runs

        carry = block(i, (jnp.zeros((t, 1), F32), jnp.zeros((t, HEAD), F32), jnp.zeros((t, LANE), F32)), True)
        _, o_ref[...], runs_ref[...] = lax.fori_loop(0, i, lambda j, c: block(i - 1 - j, c, False), carry)

    return pl.pallas_call(
        body, name="sb_attn_fwd", grid=(nh, S // t),
        in_specs=[pl.BlockSpec((t, HEAD), lambda h, i: (i, q_col + h)),
                  pl.BlockSpec((S, HEAD), lambda h, i: (0, k_col + h)),
                  pl.BlockSpec((S, HEAD), lambda h, i: (0, v_col + h))],
        out_specs=[pl.BlockSpec((t, HEAD), lambda h, i: (i, h)), pl.BlockSpec((None, t, LANE), lambda h, i: (h, i, 0))],
        out_shape=[jax.ShapeDtypeStruct((S, nh * HEAD), F32), jax.ShapeDtypeStruct((nh, S, LANE), F32)],
        compiler_params=_params(("parallel", "arbitrary")),
    )(proj, proj, proj)


def _sb_bwd(proj, q_col, k_col, v_col, dy, runs, nh):
    S = proj.shape[0]
    t = _tile(S, SB_TILE)
    scale = HEAD ** -0.5

    def body(q_ref, k_ref, v_ref, dy_ref, runs_ref, dq_ref, dk_ref, dv_ref):
        i = pl.program_id(1)

        @pl.when(i == 0)
        def _():
            dk_ref[...] = jnp.zeros_like(dk_ref)
            dv_ref[...] = jnp.zeros_like(dv_ref)

        qv = (q_ref[...] * scale).astype(BF16)
        dyb = dy_ref[...].astype(BF16)
        runs_v = runs_ref[...]
        lane = lax.broadcasted_iota(jnp.int32, (t, LANE), 1)
        upper = _tri(SB_SUB, lambda j, s: j > s)
        before = _tri(SB_SUB, lambda s, j: s < j)

        def block(kb, carry, masked):
            prefix, dq = carry
            rows = _key_rows(kb, t)
            ks = k_ref[rows, :].astype(BF16)
            vs = v_ref[rows, :].astype(BF16)
            run = jnp.sum(jnp.where(lane == kb, runs_v, 0.0), axis=-1, keepdims=True)
            a, lb, mask, _ = _sb_block(qv, ks, run, upper, t, masked)
            dl = a * _dot(dyb, vs, _NT)
            lefts = []
            for sblk in range(t // SB_SUB):
                part = dl[:, sblk * SB_SUB:(sblk + 1) * SB_SUB]
                lefts.append(_dot(part.astype(BF16), before) + prefix)
                prefix = prefix + jnp.sum(part, axis=-1, keepdims=True)
            beta = jnp.exp(lb)
            dz = dl * (1.0 - beta) - beta * jnp.concatenate(lefts, axis=1)
            if masked:
                dz = jnp.where(mask, dz, 0.0)
            dz = dz.astype(BF16)
            dk_ref[rows, :] += _dot(dz, qv, _TN)
            dv_ref[rows, :] += _dot(a.astype(BF16), dyb, _TN)
            return prefix, dq + _dot(dz, ks)

        carry = lax.fori_loop(0, i, lambda kb, c: block(kb, c, False),
                              (jnp.zeros((t, 1), F32), jnp.zeros((t, HEAD), F32)))
        dq_ref[...] = block(i, carry, True)[1] * scale

    full = pl.BlockSpec((S, HEAD), lambda h, i: (0, h))
    tile = pl.BlockSpec((t, HEAD), lambda h, i: (i, h))
    return pl.pallas_call(
        body, name="sb_attn_bwd", grid=(nh, S // t),
        in_specs=[pl.BlockSpec((t, HEAD), lambda h, i: (i, q_col + h)),
                  pl.BlockSpec((S, HEAD), lambda h, i: (0, k_col + h)),
                  pl.BlockSpec((S, HEAD), lambda h, i: (0, v_col + h)), tile,
                  pl.BlockSpec((None, t, LANE), lambda h, i: (h, i, 0))],
        out_specs=[tile, full, full],
        out_shape=[jax.ShapeDtypeStruct((S, nh * HEAD), F32)] * 3,
        compiler_params=_params(("parallel", "arbitrary")),
    )(proj, proj, proj, dy, runs)


def _place():
    return lax.axis_index("x"), lax.axis_index("y"), lax.axis_index("c")


def _other_chips(x, y):
    return [(1 - x, y), (x, 1 - y), (1 - x, 1 - y)]


def _dev_index(p):
    return 4 * p[0] + 2 * p[1] + p[2]


def _gather_blocks(blocks, *, name, in_vmem):
    n = len(blocks)
    per = 7

    def body(*refs):
        ins, outs = refs[:n], refs[n:2 * n]
        send_sems, recv_sems, local_sems = refs[2 * n:]
        x, y, c = _place()
        me, sibling = (x, y, c), (x, y, 1 - c)
        chips = _other_chips(x, y)

        def slot(a, p):
            return outs[a].at[_dev_index(p)]

        def copy(a, k, block, to, src=None):
            return pltpu.make_async_remote_copy(
                src_ref=slot(a, block) if src is None else src, dst_ref=slot(a, block),
                send_sem=send_sems.at[a * per + k], recv_sem=recv_sems.at[a * per + k],
                device_id=to, device_id_type=MESH)

        mine = [pltpu.make_async_copy(ins[a], slot(a, me), local_sems.at[a]) for a in range(n)] if in_vmem else []
        for cp in mine:
            cp.start()
        first = []
        for a in range(n):
            first.append(copy(a, 0, me, sibling, src=ins[a]))
            first += [copy(a, 1 + j, me, (*chip, c), src=ins[a]) for j, chip in enumerate(chips)]
        for cp in first:
            cp.start()
        passed = []
        for a in range(n):
            for j, chip in enumerate(chips):
                copy(a, 1 + j, (*chip, c), me).wait_recv()
                cp = copy(a, 4 + j, (*chip, c), sibling)
                cp.start()
                passed.append(cp)
        for a in range(n):
            copy(a, 0, sibling, me).wait_recv()
            for j, chip in enumerate(chips):
                copy(a, 4 + j, (*chip, 1 - c), me).wait_recv()
        for cp in first + passed:
            cp.wait_send()
        for cp in mine:
            cp.wait()

    space = pltpu.VMEM if in_vmem else pl.ANY
    spec = pl.BlockSpec(memory_space=space)
    outs = pl.pallas_call(
        body, name=name, in_specs=[spec] * n, out_specs=[spec] * n,
        out_shape=[jax.ShapeDtypeStruct((N_DEV,) + b.shape, b.dtype) for b in blocks],
        scratch_shapes=[pltpu.SemaphoreType.DMA((n * per,)), pltpu.SemaphoreType.DMA((n * per,)),
                        pltpu.SemaphoreType.DMA((n,))],
        compiler_params=pltpu.CompilerParams(vmem_limit_bytes=VMEM_LIMIT),
    )(*blocks)
    return list(outs)


def _sibling_swap(arrs, *, name, whole=False):
    n = len(arrs)

    def body(*refs):
        ins, outs = refs[:n], refs[n:2 * n]
        send_sems, recv_sems = refs[2 * n:]
        x, y, c = _place()
        copies = [pltpu.make_async_remote_copy(
            src_ref=ins[a] if whole else ins[a].at[1 - c], dst_ref=outs[a],
            send_sem=send_sems.at[a], recv_sem=recv_sems.at[a],
            device_id=(x, y, 1 - c), device_id_type=MESH) for a in range(n)]
        for cp in copies:
            cp.start()
        for cp in copies:
            cp.wait()

    spec = pl.BlockSpec(memory_space=pl.ANY)
    return list(pl.pallas_call(
        body, name=name, in_specs=[spec] * n, out_specs=[spec] * n,
        out_shape=[jax.ShapeDtypeStruct(a.shape if whole else a.shape[1:], a.dtype) for a in arrs],
        scratch_shapes=[pltpu.SemaphoreType.DMA((n,)), pltpu.SemaphoreType.DMA((n,))],
    )(*arrs))


_HBM = pl.BlockSpec(memory_space=pltpu.HBM)
_SEM = pl.BlockSpec(memory_space=pltpu.SEMAPHORE)
_EFFECT = pltpu.SideEffectType.DATAFLOW_SIDE_EFFECTING


def _in_hbm(a):
    return pltpu.with_memory_space_constraint(a, pltpu.HBM)


def _split_copies(srcs, lands, send_sems, recv_sems, plan):
    x, y, c = _place()
    copies = []
    for a, (src, land) in enumerate(zip(srcs, lands)):
        steps = plan(x, y, c)
        for k, (pick, slot, to) in enumerate(steps):
            copies.append(pltpu.make_async_remote_copy(
                src_ref=pick(src), dst_ref=slot(land), send_sem=send_sems.at[a * len(steps) + k],
                recv_sem=recv_sems.at[a * len(steps) + k], device_id=to, device_id_type=MESH))
    return copies


def _split_start(srcs, land_shapes, plan, per, *, name):
    n = len(srcs)

    def body(*refs):
        send_sems, recv_sems = refs[2 * n], refs[2 * n + 1]
        for cp in _split_copies(refs[:n], refs[n:2 * n], send_sems, recv_sems, plan):
            cp.start()
        token = refs[-1]
        token[...] = jnp.zeros_like(token)

    lands = [_in_hbm(lax.empty(s.shape, s.dtype)) for s in land_shapes]
    outs = pl.pallas_call(
        body, name=name,
        out_shape=(pltpu.SemaphoreType.DMA((n * per,)), pltpu.SemaphoreType.DMA((n * per,)),
                   *[pltpu.HBM(s.shape, s.dtype) for s in srcs], *[pltpu.HBM(s.shape, s.dtype) for s in land_shapes],
                   jax.ShapeDtypeStruct((8, LANE), F32)),
        in_specs=[_HBM] * (2 * n),
        out_specs=(_SEM, _SEM, *[_HBM] * (2 * n), pl.BlockSpec(memory_space=pltpu.VMEM)),
        input_output_aliases={i: 2 + i for i in range(2 * n)},
        compiler_params=pltpu.CompilerParams(has_side_effects=_EFFECT),
    )(*[_in_hbm(s) for s in srcs], *lands)
    return outs[0], outs[1], list(outs[2:2 + n]), list(outs[2 + n:2 + 2 * n]), outs[-1]


def _split_wait(send_sems, recv_sems, srcs, lands, after, plan, *, name):
    n = len(srcs)

    def body(*refs):
        for cp in _split_copies(refs[:n], refs[n:2 * n], refs[2 * n], refs[2 * n + 1], plan):
            cp.wait_send()
            cp.wait_recv()

    outs = pl.pallas_call(
        body, name=name,
        out_shape=(*[pltpu.HBM(s.shape, s.dtype) for s in srcs], *[pltpu.HBM(s.shape, s.dtype) for s in lands]),
        in_specs=[_HBM] * (2 * n) + [_SEM, _SEM, pl.BlockSpec(memory_space=pl.ANY)],
        out_specs=tuple([_HBM] * (2 * n)),
        input_output_aliases={i: i for i in range(2 * n)},
        compiler_params=pltpu.CompilerParams(has_side_effects=_EFFECT),
    )(*srcs, *lands, send_sems, recv_sems, after)
    return list(outs[:n]), list(outs[n:])


def _gather_plan(x, y, c):
    slot = lambda land: land.at[_dev_index((x, y, c))]
    whole = lambda src: src
    return [(whole, slot, (x, y, 1 - c))] + [(whole, slot, (px, py, c)) for px, py in _other_chips(x, y)]


def _exchange_plan(x, y, c):
    return [(lambda src, k=2 * px + py: src.at[k], lambda land, j=j: land.at[j], (px, py, c))
            for j, (px, py) in enumerate(_other_chips(x, y))]


def _gather_forward(lands, *, name):
    n = len(lands)

    def body(*refs):
        lands_in, outs = refs[:n], refs[n:2 * n]
        send_sems, recv_sems = refs[2 * n:]
        x, y, c = _place()
        copies = []
        for a in range(n):
            for j, (px, py) in enumerate(_other_chips(x, y)):
                copies.append((pltpu.make_async_remote_copy(
                    src_ref=lands_in[a].at[_dev_index((px, py, c))], dst_ref=outs[a].at[_dev_index((px, py, c))],
                    send_sem=send_sems.at[3 * a + j], recv_sem=recv_sems.at[3 * a + j],
                    device_id=(x, y, 1 - c), device_id_type=MESH), a, j, (px, py)))
        for cp, _, _, _ in copies:
            cp.start()
        for cp, a, j, (px, py) in copies:
            cp.wait_send()
            pltpu.make_async_remote_copy(
                src_ref=lands_in[a].at[_dev_index((px, py, 1 - c))], dst_ref=outs[a].at[_dev_index((px, py, 1 - c))],
                send_sem=send_sems.at[3 * a + j], recv_sem=recv_sems.at[3 * a + j],
                device_id=(x, y, 1 - c), device_id_type=MESH).wait_recv()

    spec = pl.BlockSpec(memory_space=pl.ANY)
    return list(pl.pallas_call(
        body, name=name, in_specs=[spec] * n, out_specs=[spec] * n,
        out_shape=[jax.ShapeDtypeStruct(a.shape, a.dtype) for a in lands],
        input_output_aliases={a: a for a in range(n)},
        scratch_shapes=[pltpu.SemaphoreType.DMA((3 * n,)), pltpu.SemaphoreType.DMA((3 * n,))],
    )(*lands))


def _flat2(a, lead):
    return a.reshape(a.shape[:lead] + (-1, a.shape[-1]))


def _pair_sum(g, recv, c_idx, *, name):
    _, nchip, r, w = g.shape
    tm = _tile(r, 256) if r % 8 == 0 else r

    def body(c_ref, g_ref, r_ref, o_ref):
        o_ref[...] = (g_ref[...].astype(F32) + r_ref[...].astype(F32)).astype(o_ref.dtype)

    return pl.pallas_call(
        body, name=name,
        grid_spec=pltpu.PrefetchScalarGridSpec(
            num_scalar_prefetch=1, grid=(nchip, r // tm),
            in_specs=[pl.BlockSpec((None, None, tm, w), lambda k, i, c_ref: (c_ref[0], k, i, 0)),
                      pl.BlockSpec((None, tm, w), lambda k, i, c_ref: (k, i, 0))],
            out_specs=pl.BlockSpec((None, tm, w), lambda k, i, c_ref: (k, i, 0))),
        out_shape=jax.ShapeDtypeStruct((nchip, r, w), BF16),
        compiler_params=_params(("parallel", "parallel")),
    )(c_idx, g, recv)


def _chip_sum(s1, recv, chip_idx, *, name):
    _, r, w = s1.shape
    tm = _tile(r, 256) if r % 8 == 0 else r

    def body(k_ref, s_ref, r_ref, o_ref):
        acc = s_ref[...].astype(F32)
        for j in range(3):
            acc = acc + r_ref[j].astype(F32)
        o_ref[...] = acc

    return pl.pallas_call(
        body, name=name,
        grid_spec=pltpu.PrefetchScalarGridSpec(
            num_scalar_prefetch=1, grid=(r // tm,),
            in_specs=[pl.BlockSpec((None, tm, w), lambda i, k_ref: (k_ref[0], i, 0)),
                      pl.BlockSpec((3, tm, w), lambda i, k_ref: (0, i, 0))],
            out_specs=pl.BlockSpec((tm, w), lambda i, k_ref: (i, 0))),
        out_shape=jax.ShapeDtypeStruct((r, w), F32),
        compiler_params=_params(("parallel",)),
    )(chip_idx, s1, recv)


def _adam_math(w, g, m, v):
    m = ADAM_B1 * m + (1.0 - ADAM_B1) * g
    v = ADAM_B2 * v + (1.0 - ADAM_B2) * (g * g)
    m_hat = m / (1.0 - ADAM_B1 ** ADAM_STEP)
    v_hat = v / (1.0 - ADAM_B2 ** ADAM_STEP)
    delta = -ADAM_LR * (m_hat / (jnp.sqrt(v_hat) + ADAM_EPS) + ADAM_WD * w)
    return delta, m, v


def _adamw(w, mine, other, c_idx, m, v, *, name):
    r, cw = w.shape
    hr = r // 2
    tm = _row_tile(hr, 9 * cw * 4)

    def body(c_ref, w_ref, a_ref, b_ref, m_ref, v_ref, g_ref, d_ref, nm_ref, nv_ref):
        g = jnp.where(pl.program_id(0) == c_ref[0], a_ref[...], b_ref[...])
        g_ref[...] = g
        d_ref[...], nm_ref[...], nv_ref[...] = _adam_math(w_ref[...], g, m_ref[...], v_ref[...])

    full = pl.BlockSpec((None, tm, cw), lambda h, i, c_ref: (h, i, 0))
    half = pl.BlockSpec((tm, cw), lambda h, i, c_ref: (i, 0))
    outs = pl.pallas_call(
        body, name=name,
        grid_spec=pltpu.PrefetchScalarGridSpec(
            num_scalar_prefetch=1, grid=(2, hr // tm),
            in_specs=[full, half, half, full, full], out_specs=[full] * 4),
        out_shape=[jax.ShapeDtypeStruct((2, hr, cw), F32)] * 4,
        compiler_params=_params(("parallel", "parallel")),
    )(c_idx, w.reshape(2, hr, cw), mine, other, m.reshape(2, hr, cw), v.reshape(2, hr, cw))
    return [o.reshape(r, cw) for o in outs]


def _adamw_ada(cact_t, dada, w, m, v):
    r, cw = w.shape
    nb = cact_t.shape[1]
    tm = _tile(r, 256)
    tn = _tile(cw, 1024)

    def body(a_ref, d_ref, w_ref, m_ref, v_ref, g_ref, dl_ref, nm_ref, nv_ref):
        a = a_ref[...]
        d = d_ref[...]
        g = a[:, 0:1] * d[0:1, :]
        for b in range(1, nb):
            g = g + a[:, b:b + 1] * d[b:b + 1, :]
        g_ref[...] = g
        dl_ref[...], nm_ref[...], nv_ref[...] = _adam_math(w_ref[...], g, m_ref[...], v_ref[...])

    blk = pl.BlockSpec((tm, tn), lambda i, j: (i, j))
    return pl.pallas_call(
        body, name="adamw_ada", grid=(r // tm, cw // tn),
        in_specs=[pl.BlockSpec((tm, nb), lambda i, j: (i, 0)), pl.BlockSpec((nb, tn), lambda i, j: (0, j)), blk, blk, blk],
        out_specs=[blk] * 4, out_shape=[jax.ShapeDtypeStruct((r, cw), F32)] * 4,
        compiler_params=_params(("parallel", "parallel")),
    )(cact_t, dada, w, m, v)


def _adamw_vec(parts, w, m, v):
    n = w.shape[1]

    def body(p_ref, w_ref, m_ref, v_ref, g_ref, d_ref, nm_ref, nv_ref):
        p = p_ref[...]
        g = p[0:1, :]
        for b in range(1, N_DEV):
            g = g + p[b:b + 1, :]
        g_ref[...] = g
        d_ref[...], nm_ref[...], nv_ref[...] = _adam_math(w_ref[...], g, m_ref[...], v_ref[...])

    return pl.pallas_call(
        body, name="adamw_vec", out_shape=[jax.ShapeDtypeStruct((1, n), F32)] * 4,
        compiler_params=pltpu.CompilerParams(vmem_limit_bytes=VMEM_LIMIT),
    )(parts, w, m, v)


def _w_in_segments(kpe0, d_in, cs):
    segs = []
    for k in range(4):
        lo, hi = k * cs, (k + 1) * cs
        for a, b, shift in ((0, kpe0, 0), (kpe0, kpe0 + ROPE, d_in - ROPE - kpe0), (kpe0 + ROPE, d_in, -ROPE)):
            a, b = max(lo, a), min(hi, b)
            if a < b:
                segs.append((k, a - lo, a + shift, b - a))
    return segs


def _w_in_layout(g8, kpe0):
    _, hr, cs = g8.shape
    rows, d_in = 2 * hr, 4 * cs
    segs = _w_in_segments(kpe0, d_in, cs)
    tm = _tile(rows, 256)

    def body(g_ref, o_ref):
        for k, src, dst, w in segs:
            o_ref[:, dst:dst + w] = g_ref[k, :, src:src + w]
        o_ref[:, d_in:] = jnp.zeros((tm, ROPE), o_ref.dtype)

    return pl.pallas_call(
        body, name="w_in_layout", grid=(rows // tm,),
        in_specs=[pl.BlockSpec((4, tm, cs), lambda i: (0, i, 0))], out_specs=_rows(tm, d_in + ROPE),
        out_shape=jax.ShapeDtypeStruct((rows, d_in + ROPE), g8.dtype), compiler_params=_params(("parallel",)),
    )(g8.reshape(4, rows, cs))


def _w_in_grad_pieces(g, kpe0):
    rows, d_in_p = g.shape
    d_in = d_in_p - ROPE
    cs = d_in // 4
    segs = _w_in_segments(kpe0, d_in, cs)
    hr = rows // 2
    tm = _tile(hr, 256)
    per_half = hr // tm

    def body(g_ref, o_ref):
        for k, src, dst, w in segs:
            o_ref[k, :, src:src + w] = g_ref[:, dst:dst + w]

    return pl.pallas_call(
        body, name="w_in_grad_pieces", grid=(rows // tm,),
        in_specs=[_rows(tm, d_in_p)],
        out_specs=pl.BlockSpec((None, 4, tm, cs), lambda i: (i // per_half, 0, i % per_half, 0)),
        out_shape=jax.ShapeDtypeStruct((2, 4, hr, cs), g.dtype), compiler_params=_params(("parallel",)),
    )(g)


def _interleave_layout(g8, ib):
    _, hr, cs = g8.shape
    rows, per_chip, per_half = 2 * hr, cs // ib, 2 * cs // ib
    tm = _tile(rows, 1024)

    def src(jj):
        return jj // 2 + per_half * (jj % 2)

    def body(g_ref, o_ref):
        o_ref[...] = g_ref[...]

    return pl.pallas_call(
        body, name="interleave_layout", grid=(rows // tm, 4 * per_chip),
        in_specs=[pl.BlockSpec((None, tm, ib), lambda i, jj: (src(jj) // per_chip, i, src(jj) % per_chip))],
        out_specs=pl.BlockSpec((tm, ib), lambda i, jj: (i, jj)),
        out_shape=jax.ShapeDtypeStruct((rows, 4 * cs), g8.dtype), compiler_params=_params(("parallel", "parallel")),
    )(g8.reshape(4, rows, cs))


def _cols_from_chips(g8, rows):
    cs = g8.shape[-1]
    return g8.reshape(4, rows, cs).transpose(1, 0, 2).reshape(rows, 4 * cs)


def _cols_to_pieces(g):
    rows, c4 = g.shape
    return g.reshape(2, rows // 2, 4, c4 // 4).transpose(0, 2, 1, 3)


def _rows_to_pieces(g):
    r4, cols = g.shape
    return g.reshape(4, 2, r4 // 8, cols).transpose(1, 0, 2, 3)


def _pad_cols(a, w):
    return jnp.pad(a, ((0, 0), (0, w - a.shape[1])))


def kernel(x, c, positions, w_ada, b_ada, g_norm1, g_norm2, w_in, g_q_latent, g_kv_latent, w_uq, w_ukv, g_q_head, g_k_head, w_proj_mla, w_proj_sb, w_out, w_ffn_in, w_ffn_out, loss_target, m_w_ada, m_b_ada, m_g_norm1, m_g_norm2, m_w_in, m_g_q_latent, m_g_kv_latent, m_w_uq, m_w_ukv, m_g_q_head, m_g_k_head, m_w_proj_mla, m_w_proj_sb, m_w_out, m_w_ffn_in, m_w_ffn_out, v_w_ada, v_b_ada, v_g_norm1, v_g_norm2, v_w_in, v_g_q_latent, v_g_kv_latent, v_w_uq, v_w_ukv, v_g_q_head, v_g_k_head, v_w_proj_mla, v_w_proj_sb, v_w_out, v_w_ffn_in, v_w_ffn_out):
    xi, yi, ci = _place()
    chip = 2 * xi + yi
    dev = 2 * chip + ci
    c_idx = jnp.reshape(ci, (1,)).astype(jnp.int32)
    chip_idx = jnp.reshape(chip, (1,)).astype(jnp.int32)

    x = x[0]
    tgt = loss_target[0]
    S, D = x.shape
    ql = g_q_latent.shape[1]
    assert g_kv_latent.shape[1] == ql
    mlaw = w_proj_mla.shape[1]
    nh = mlaw // HEAD
    sbw = w_proj_sb.shape[1]
    assert sbw == mlaw
    dff = w_ffn_out.shape[1] * 4
    d_in = 2 * ql + ROPE + 3 * sbw + 2 * D
    d_in_p = d_in + ROPE
    q_col = (2 * ql) // HEAD
    k_col = q_col + nh
    v_col = k_col + nh
    gla_col = (2 * ql + 3 * sbw) // D
    glb_col = gla_col + 1
    kpe_col = (d_in - ROPE) // LANE
    assert (2 * ql + 3 * sbw) % D == 0 and (d_in - ROPE) % LANE == 0

    mats = {"w_in": w_in[0], "w_uq": w_uq[0], "w_ukv": w_ukv[0], "w_proj_mla": w_proj_mla[0],
            "w_proj_sb": w_proj_sb[0], "w_out": w_out[0], "w_ffn_in": w_ffn_in[0], "w_ffn_out": w_ffn_out[0]}
    names = list(mats)
    row_sharded = {"w_out", "w_ffn_out"}

    c_all = _gather_blocks([jnp.broadcast_to(c, (8, D))], name="gather_cond", in_vmem=True)[0][:, 0, :]
    n_ada = w_ada.shape[2]
    b_shard = lax.dynamic_slice_in_dim(b_ada, chip * n_ada, n_ada, axis=1)
    ada_shard = _mm(c_all, w_ada[0], name="ada_proj", a_fn=jax.nn.silu, bias=b_shard)
    ada_all = _gather_blocks([ada_shard], name="gather_ada", in_vmem=True)[0]
    ada_rows = lax.dynamic_index_in_dim(ada_all, dev, axis=1, keepdims=False)
    ada = ada_rows[0::2].reshape(1, 4 * n_ada)
    SH1, SC1, GT1, SH2, SC2, GT2 = range(6)

    def after(dep, a):
        return a + (dep.reshape(-1)[0:1].reshape((1,) * a.ndim) * 0).astype(a.dtype)

    def fill_own(g8, own):
        return lax.dynamic_update_index_in_dim(g8, own, dev, 0)

    halves = []
    for nm in names:
        w = mats[nm]
        hr = w.shape[0] // 2
        halves.append(lax.dynamic_slice_in_dim(w, ci * hr, hr, axis=0).astype(BF16))
    half_of = dict(zip(names, halves))
    early = ["w_in", "w_uq", "w_ukv"]
    late = ["w_proj_mla", "w_proj_sb", "w_out", "w_ffn_in", "w_ffn_out"]
    early_halves = [half_of[nm] for nm in early]
    early_halves[0] = after(ada, early_halves[0])
    early_got = _gather_blocks(early_halves, name="gather_weights", in_vmem=False)
    gathered = {nm: fill_own(g8, own) for nm, g8, own in zip(early, early_got, early_halves)}
    late_halves = [half_of[nm] for nm in late]
    late_halves[0] = after(gathered[early[1]], late_halves[0])
    late_send, late_recv, late_srcs, late_lands, late_token = _split_start(
        late_halves, [jax.ShapeDtypeStruct((N_DEV,) + h.shape, h.dtype) for h in late_halves], _gather_plan, 4,
        name="gather_late_start")
    ada = ada + late_token[0:1, 0:1]

    def full_cols(nm):
        return _cols_from_chips(gathered[nm], mats[nm].shape[0])

    kpe0 = 2 * ql
    w_in_p = _w_in_layout(gathered["w_in"], kpe0)
    w_uq_p = jnp.pad(full_cols("w_uq").reshape(ql, nh, QK_DIM), ((0, 0), (0, 0), (0, HEAD_PAD - QK_DIM))
                     ).reshape(ql, nh * HEAD_PAD)
    w_ukv4 = full_cols("w_ukv").reshape(ql, nh, 2 * HEAD)
    w_ukv_p = jnp.concatenate([w_ukv4[:, :, :HEAD].reshape(ql, mlaw), w_ukv4[:, :, HEAD:].reshape(ql, mlaw)], axis=1)

    half = ROPE // 2
    freqs = ROPE_THETA ** (-jnp.arange(half, dtype=F32) / half)
    ang = positions[0].astype(F32)[:, None] * freqs
    cos, sin = jnp.cos(ang), jnp.sin(ang)
    one = jnp.ones((S, NOPE), F32)
    zero = jnp.zeros((S, NOPE), F32)
    zh = jnp.zeros((S, half), F32)
    tabs = (jnp.concatenate([one, cos, cos, one[:, :HEAD_PAD - QK_DIM]], axis=1),
            jnp.concatenate([zero, zh, sin, zero[:, :HEAD_PAD - QK_DIM]], axis=1),
            jnp.concatenate([zero, -sin, zh, zero[:, :HEAD_PAD - QK_DIM]], axis=1))
    g_qh_p = _pad_cols(g_q_head, HEAD_PAD)
    g_kh_p = _pad_cols(g_k_head, HEAD_PAD)

    h1 = _rmsmod(x, g_norm1, ada, SC1, SH1, name="rmsmod1")
    proj = _mm(h1, w_in_p, name="mm_proj", tn=640)
    cqn, ckvn = _latent_norm(proj, g_q_latent, g_kv_latent, ql)
    q0 = _mm(cqn, w_uq_p, name="mm_q_up")
    kv0 = _mm(ckvn, w_ukv_p, name="mm_kv_up")
    q = _q_prep(q0, g_qh_p, tabs, nh)
    k = _k_prep(kv0, proj, kpe_col, g_kh_p, tabs, nh)
    y_a, lse = _mla_fwd(q, k, kv0, nh)
    y_b, sb_runs = _sb_fwd(proj, q_col, k_col, v_col, nh)
    late_srcs, late_lands = _split_wait(late_send, late_recv, late_srcs, late_lands, y_b, _gather_plan,
                                        name="gather_late_wait")
    late_got = _gather_forward(late_lands, name="gather_late_forward")
    gathered.update({nm: fill_own(g8, own) for nm, g8, own in zip(late, late_got, late_srcs)})
    w_pm = full_cols("w_proj_mla")
    w_ps = full_cols("w_proj_sb")
    w_o = gathered["w_out"].reshape(D, D)
    ib = 256 if (dff // 2) % 256 == 0 else LANE
    nb = dff // ib
    w_fi = _interleave_layout(gathered["w_ffn_in"], ib)
    w_fo = gathered["w_ffn_out"].reshape(dff, D)
    pa = _mm(y_a, w_pm, name="mm_proj_mla", out_dtype=BF16)
    pb = _mm(y_b, w_ps, name="mm_proj_sb", out_dtype=BF16)
    merged = _gate_merge(pa, pb, proj, gla_col, glb_col)
    o = _mm(merged, w_o, name="mm_out")
    x2, h2 = _resid_rmsmod(x, o, g_norm2, ada, GT1, SC2, SH2)
    ff, act = _mm(h2, w_fi, name="mm_ffn_in", tn=4 * ib,
                  fused=(_swiglu_tile(ib), [], [(2 * dff, 4 * ib, BF16), (dff, 2 * ib, BF16)]))
    f = _mm(act, w_fo, name="mm_ffn_out")
    dy, df, red_l, loss_p = _loss_head(x2, f, tgt, ada, GT2)

    dff_, = _mm(df, w_fo, name="mm_d_act", tb=True, tn=2 * ib,
                fused=(_swiglu_bwd_tile(ib), [(ff, 4 * ib)], [(2 * dff, 4 * ib, BF16)]))
    def pc(kind):
        if kind == "cols":
            return kind
        return kind if (D // 4) % LANE == 0 and (dff // 4) % LANE == 0 else None

    gw_fo = _mm(act, df, name="mm_gw_ffn_out", ta=True, out_dtype=BF16, pieces=pc("rows"))
    dh2 = _mm(dff_, w_fi, name="mm_d_h2", tb=True)
    gw_fi = _mm(h2, dff_, name="mm_gw_ffn_in", ta=True, out_dtype=BF16, pieces="cols", tn=ib,
                col_perm=lambda jj: jj // 2 + nb * (jj % 2))

    def pair_sums(nms, grads, tag):
        pcs = [g if g.ndim == 4 else (_rows_to_pieces if nm in row_sharded else _cols_to_pieces)(g)
               for nm, g in zip(nms, grads)]
        got = _sibling_swap(pcs, name="rs_sibling_swap_" + tag)
        return [_pair_sum(p, r, c_idx, name="rs_pair_sum_" + nm) for p, r, nm in zip(pcs, got, nms)]

    ffn = ["w_ffn_in", "w_ffn_out"]
    ffn_pair = pair_sums(ffn, [gw_fi, gw_fo], "ffn")
    ffn_send, ffn_recv, ffn_pair, ffn_lands, ffn_token = _split_start(
        ffn_pair, [jax.ShapeDtypeStruct((3,) + p.shape[1:], p.dtype) for p in ffn_pair], _exchange_plan, 3,
        name="rs_exchange_ffn_start")
    ada = ada + ffn_token[0:1, 0:1]
    dx2, do, red_2 = _rmsmod2_bwd(dh2, x2, dy, o, g_norm2, ada, SC2, GT1)
    dmerged = _mm(do, w_o, name="mm_d_merged", tb=True, out_dtype=BF16)
    gw_o = _mm(merged, do, name="mm_gw_out", ta=True, out_dtype=BF16, pieces=pc("rows"))
    dpa, dpb, dgla, dglb = _gate_bwd(dmerged, pa, pb, proj, gla_col, glb_col)
    dya = _mm(dpa, w_pm, name="mm_d_ya", tb=True, out_dtype=BF16)
    gw_pm = _mm(y_a, dpa, name="mm_gw_proj_mla", ta=True, out_dtype=BF16, pieces=pc("cols"))
    dyb = _mm(dpb, w_ps, name="mm_d_yb", tb=True, out_dtype=BF16)
    gw_ps = _mm(y_b, dpb, name="mm_gw_proj_sb", ta=True, out_dtype=BF16, pieces=pc("cols"))
    mid = ["w_proj_mla", "w_proj_sb", "w_out"]
    mid_pair = pair_sums(mid, [gw_pm, gw_ps, gw_o], "mid")
    mid_send, mid_recv, mid_pair, mid_lands, mid_token = _split_start(
        mid_pair, [jax.ShapeDtypeStruct((3,) + p.shape[1:], p.dtype) for p in mid_pair], _exchange_plan, 3,
        name="rs_exchange_mid_start")
    lse = lse + mid_token[0:1, 0:1]
    dq, dk, dv = _mla_bwd(q, k, kv0, y_a, dya, lse, nh)
    dq_sb, dk_sb, dv_sb = _sb_bwd(proj, q_col, k_col, v_col, dyb, sb_runs, nh)
    dq0, red_qh = _q_prep_bwd(dq, q0, g_qh_p, tabs, nh)
    dkv0, dkpe, red_kh = _k_prep_bwd(dk, dv, kv0, proj, kpe_col, g_kh_p, tabs, nh)
    dcqn = _mm(dq0, w_uq_p, name="mm_d_cqn", tb=True, out_dtype=BF16)
    gw_uq_p = _mm(cqn, dq0, name="mm_gw_uq", ta=True, out_dtype=BF16)
    dckvn = _mm(dkv0, w_ukv_p, name="mm_d_ckvn", tb=True, out_dtype=BF16)
    gw_ukv_p = _mm(ckvn, dkv0, name="mm_gw_ukv", ta=True, out_dtype=BF16)
    dcq, dckv, red_lat = _latent_norm_bwd(dcqn, dckvn, proj, g_q_latent, g_kv_latent, ql)
    dproj = jnp.concatenate([dcq, dckv, dq_sb.astype(BF16), dk_sb.astype(BF16), dv_sb.astype(BF16),
                             dgla, dglb, dkpe], axis=1)
    gw_in_p = _mm(h1, dproj, name="mm_gw_in", ta=True, out_dtype=BF16, tn=640)

    gw_in = _w_in_grad_pieces(gw_in_p, kpe0)
    gw_uq = gw_uq_p.reshape(ql, nh, HEAD_PAD)[:, :, :QK_DIM].reshape(ql, nh * QK_DIM)
    gw_ukv = jnp.concatenate([gw_ukv_p[:, :mlaw].reshape(ql, nh, HEAD), gw_ukv_p[:, mlaw:].reshape(ql, nh, HEAD)],
                             axis=2).reshape(ql, 2 * mlaw)
    last = ["w_in", "w_uq", "w_ukv"]
    assert last + mid + ffn == names

    last_pair = pair_sums(last, [gw_in, gw_uq, gw_ukv], "last")
    last_send, last_recv, last_pair, last_lands, last_token = _split_start(
        last_pair, [jax.ShapeDtypeStruct((3,) + p.shape[1:], p.dtype) for p in last_pair], _exchange_plan, 3,
        name="rs_exchange_last_start")
    ada = ada + last_token[0:1, 0:1]
    dh1 = _mm(dproj, w_in_p, name="mm_d_h1", tb=True, bias=jnp.zeros((1, D), F32) + last_token[0:1, 0:1])
    grad_x, red_1 = _rmsmod1_bwd(dh1, x, dx2, g_norm1, ada, SC1)
    last_pair, last_chips = _split_wait(last_send, last_recv, last_pair, last_lands, grad_x, _exchange_plan,
                                        name="rs_exchange_last_wait")
    mid_pair, mid_chips = _split_wait(mid_send, mid_recv, mid_pair, mid_lands, grad_x, _exchange_plan,
                                      name="rs_exchange_mid_wait")
    ffn_pair, ffn_chips = _split_wait(ffn_send, ffn_recv, ffn_pair, ffn_lands, grad_x, _exchange_plan,
                                      name="rs_exchange_ffn_wait")
    reduced = [_chip_sum(s, r, chip_idx, name="rs_chip_sum_" + nm)
               for s, r, nm in zip(last_pair + mid_pair + ffn_pair, last_chips + mid_chips + ffn_chips, names)]
    from_sibling2 = _sibling_swap(reduced, name="rs_sibling_send", whole=True)

    vec_names = ["b_ada", "g_norm1", "g_norm2", "g_q_latent", "g_kv_latent", "g_q_head", "g_k_head"]
    vec_w = dict(b_ada=b_ada, g_norm1=g_norm1, g_norm2=g_norm2, g_q_latent=g_q_latent, g_kv_latent=g_kv_latent,
                 g_q_head=g_q_head, g_k_head=g_k_head)
    vec_m = dict(b_ada=m_b_ada, g_norm1=m_g_norm1, g_norm2=m_g_norm2, g_q_latent=m_g_q_latent,
                 g_kv_latent=m_g_kv_latent, g_q_head=m_g_q_head, g_k_head=m_g_k_head)
    vec_v = dict(b_ada=v_b_ada, g_norm1=v_g_norm1, g_norm2=v_g_norm2, g_q_latent=v_g_q_latent,
                 g_kv_latent=v_g_kv_latent, g_q_head=v_g_q_head, g_k_head=v_g_k_head)
    d_ada = jnp.concatenate([red_1[0:1], red_1[1:2], red_2[3:4], red_2[0:1], red_2[1:2], red_l[0:1]], axis=1)
    vec_parts = dict(b_ada=d_ada, g_norm1=red_1[2:3], g_norm2=red_2[2:3], g_q_latent=red_lat[0:1],
                     g_kv_latent=red_lat[1:2], g_q_head=red_qh[0:1], g_k_head=red_kh[0:1])
    widths = [-(-vec_w[nm].shape[1] // LANE) * LANE for nm in vec_names]
    offs = [sum(widths[:i]) for i in range(len(widths))]
    pack = lambda d: jnp.concatenate([_pad_cols(d[nm][:, :vec_w[nm].shape[1]], wd) for nm, wd in zip(vec_names, widths)], axis=1)
    nvec = sum(widths) + LANE
    no_loss = jnp.zeros((1, LANE), F32)
    parts = jnp.concatenate([pack(vec_parts), loss_p[0:1, :]], axis=1)
    parts_all = _gather_blocks([jnp.broadcast_to(parts, (8, nvec))], name="gather_vec_grads",
                               in_vmem=True)[0][:, 0, :]
    gvec, dvec, nmvec, nvvec = _adamw_vec(parts_all, *[jnp.concatenate([pack(d), no_loss], axis=1)
                                                       for d in (vec_w, vec_m, vec_v)])
    loss = gvec[0, nvec - LANE]
    unpack = lambda a: {nm: a[:, o_:o_ + vec_w[nm].shape[1]] for nm, o_ in zip(vec_names, offs)}
    gvec, dvec, nmvec, nvvec = unpack(gvec), unpack(dvec), unpack(nmvec), unpack(nvvec)

    dada_all = lax.dynamic_slice_in_dim(parts_all[:, :6 * D], chip * n_ada, n_ada, axis=1)
    cact_t = jax.nn.silu(c_all).T
    g_ada, d_ada_w, nm_ada, nv_ada = _adamw_ada(cact_t, dada_all, w_ada[0], m_w_ada[0], v_w_ada[0])

    ms = dict(w_in=m_w_in, w_uq=m_w_uq, w_ukv=m_w_ukv, w_proj_mla=m_w_proj_mla, w_proj_sb=m_w_proj_sb,
              w_out=m_w_out, w_ffn_in=m_w_ffn_in, w_ffn_out=m_w_ffn_out)
    vs = dict(w_in=v_w_in, w_uq=v_w_uq, w_ukv=v_w_ukv, w_proj_mla=v_w_proj_mla, w_proj_sb=v_w_proj_sb,
              w_out=v_w_out, w_ffn_in=v_w_ffn_in, w_ffn_out=v_w_ffn_out)
    G, DL, NM, NV = {}, {}, {}, {}
    for nm, mine, other in zip(names, reduced, from_sibling2):
        g_, d_, m_, v_ = _adamw(mats[nm], mine, other, c_idx, ms[nm][0], vs[nm][0], name="adamw_" + nm)
        G[nm], DL[nm], NM[nm], NV[nm] = g_[None], d_[None], m_[None], v_[None]
    G["w_ada"], DL["w_ada"], NM["w_ada"], NV["w_ada"] = g_ada[None], d_ada_w[None], nm_ada[None], nv_ada[None]
    for nm in vec_names:
        G[nm], DL[nm], NM[nm], NV[nm] = gvec[nm], dvec[nm], nmvec[nm], nvvec[nm]

    order = ["w_ada", "b_ada", "g_norm1", "g_norm2", "w_in", "g_q_latent", "g_kv_latent", "w_uq", "w_ukv",
             "g_q_head", "g_k_head", "w_proj_mla", "w_proj_sb", "w_out", "w_ffn_in", "w_ffn_out"]
    return (loss, grad_x[None], *[G[n] for n in order], *[DL[n] for n in order],
            *[NM[n] for n in order], *[NV[n] for n in order])
```

```python
import functools
import math

import jax
import jax.numpy as jnp
from jax import lax
from jax.experimental import pallas as pl
from jax.experimental.pallas import tpu as pltpu

F32 = jnp.float32
BF16 = jnp.bfloat16
MESH = pl.DeviceIdType.MESH

EPS = 1e-6
ROPE_THETA = 10000.0
NOPE = 128
ROPE = 64
QK_DIM = NOPE + ROPE
HEAD_PAD = 256
HEAD = 128
N_DEV = 8
LANE = 128
VMEM_LIMIT = 48 * 1024 * 1024

ADAM_LR = 0.001
ADAM_B1 = 0.9
ADAM_B2 = 0.999
ADAM_EPS = 1e-08
ADAM_WD = 0.01
ADAM_STEP = 10


def _tile(n, target):
    if n <= target:
        return n
    t = (target // LANE) * LANE
    while t >= LANE:
        if n % t == 0:
            return t
        t -= LANE
    return n


def _row_tile(rows, row_bytes, budget=24 * 1024 * 1024):
    cap = max(8, budget // (2 * row_bytes))
    best = None
    for t in range(8, min(rows, cap) + 1, 8):
        if rows % t == 0:
            best = t
    return best if best is not None else rows


def _params(sem):
    return pltpu.CompilerParams(dimension_semantics=sem, vmem_limit_bytes=VMEM_LIMIT)


def _rows(tm, w, col=0):
    return pl.BlockSpec((tm, w), lambda i: (i, col))


def _vec(w, col=0, rows=1):
    return pl.BlockSpec((rows, w), lambda i: (0, col))


MM_VMEM_BUDGET = 36 * 1024 * 1024


def _mm(a, b, *, name, ta=False, tb=False, out_dtype=F32, a_fn=None, bias=None, tm=1024, tn=1024, pieces=None,
        col_perm=None, fused=None):
    M = a.shape[1] if ta else a.shape[0]
    K = a.shape[0] if ta else a.shape[1]
    N = b.shape[0] if tb else b.shape[1]
    assert K == (b.shape[1] if tb else b.shape[0]), (a.shape, b.shape, ta, tb)
    if pieces == "cols":
        tm, tn = _tile(M // 2, tm), _tile(N // 4, tn)
        assert (M // 2) % tm == 0 and (N // 4) % tn == 0
    elif pieces == "rows":
        tm, tn = M // 4, _tile(N, tn)
    else:
        tm, tn = _tile(M, tm), _tile(N, tn)
    sa, sb, so = a.dtype.itemsize, b.dtype.itemsize, jnp.dtype(out_dtype).itemsize

    def fits(tk):
        return 2 * tk * (tm * sa + tn * sb) + tm * tn * (2 * so + 4) <= MM_VMEM_BUDGET

    tk = K
    while not fits(tk):
        smaller = _tile(K, tk - LANE)
        if smaller >= tk:
            break
        tk = smaller
    nk = K // tk
    dn = (((0 if ta else 1,), (1 if tb else 0,)), ((), ()))
    b_outer = nk == 1 and a.size * sa * (N // tn) < b.size * sb * (M // tm)

    n_extra = len(fused[1]) if fused else 0
    n_out = len(fused[2]) if fused else 1

    def body(*refs):
        a_ref, b_ref = refs[:2]
        bias_ref = refs[2] if bias is not None else None
        first = 3 if bias is not None else 2
        extra_refs = refs[first:first + n_extra]
        out_refs = refs[first + n_extra:first + n_extra + n_out]
        o_ref = out_refs[0]
        av = a_ref[...]
        if a_fn is not None:
            av = a_fn(av.astype(F32))
        part = lax.dot_general(av.astype(BF16), b_ref[...].astype(BF16), dn, preferred_element_type=F32)

        def finish(r):
            if bias is not None:
                r = r + bias_ref[...]
            if fused:
                for ref, tile in zip(out_refs, fused[0](r, *[e[...] for e in extra_refs])):
                    ref[...] = tile.astype(ref.dtype)
            elif pieces == "rows":
                o_ref[0] = r[:tm // 2].astype(o_ref.dtype)
                o_ref[1] = r[tm // 2:].astype(o_ref.dtype)
            else:
                o_ref[...] = r.astype(o_ref.dtype)

        if nk == 1:
            finish(part)
        else:
            acc_ref = refs[-1]
            k = pl.program_id(2)

            @pl.when(k == 0)
            def _():
                acc_ref[...] = part

            @pl.when(k > 0)
            def _():
                acc_ref[...] += part

            @pl.when(k == nk - 1)
            def _():
                finish(acc_ref[...])

    def ij(g0, g1):
        return (g1, g0) if b_outer else (g0, g1)

    def amap(g0, g1, k):
        i, _ = ij(g0, g1)
        return (k, i) if ta else (i, k)

    def bmap(g0, g1, k):
        _, j = ij(g0, g1)
        return (j, k) if tb else (k, j)

    in_specs = [pl.BlockSpec((tk, tm) if ta else (tm, tk), amap), pl.BlockSpec((tn, tk) if tb else (tk, tn), bmap)]
    args = [a, b]
    if bias is not None:
        in_specs.append(pl.BlockSpec((1, tn), lambda g0, g1, k: (0, ij(g0, g1)[1])))
        args.append(bias)
    grid = (N // tn, M // tm, nk) if b_outer else (M // tm, N // tn, nk)
    if pieces == "cols":
        ni, nj = M // 2 // tm, N // 4 // tn

        def omap(g0, g1, k):
            i, j = ij(g0, g1)
            j = col_perm(j) if col_perm else j
            return (i // ni, j // nj, i % ni, j % nj)

        out_spec = pl.BlockSpec((None, None, tm, tn), omap)
        out_shape = jax.ShapeDtypeStruct((2, 4, M // 2, N // 4), out_dtype)
    elif pieces == "rows":
        out_spec = pl.BlockSpec((2, None, tm // 2, tn), lambda g0, g1, k: (0, ij(g0, g1)[0], 0, ij(g0, g1)[1]))
        out_shape = jax.ShapeDtypeStruct((2, 4, tm // 2, N), out_dtype)
    else:
        out_spec = pl.BlockSpec((tm, tn), lambda g0, g1, k: ij(g0, g1))
        out_shape = jax.ShapeDtypeStruct((M, N), out_dtype)
    if fused:
        for arr, width in fused[1]:
            in_specs.append(pl.BlockSpec((tm, width), lambda g0, g1, k: ij(g0, g1)))
            args.append(arr)
        out_spec = [pl.BlockSpec((tm, width), lambda g0, g1, k: ij(g0, g1)) for _, width, _ in fused[2]]
        out_shape = [jax.ShapeDtypeStruct((M, cols), dt) for cols, _, dt in fused[2]]
    return pl.pallas_call(
        body, name=name, grid=grid, in_specs=in_specs, out_specs=out_spec, out_shape=out_shape,
        scratch_shapes=[pltpu.VMEM((tm, tn), F32)] if nk > 1 else [],
        compiler_params=_params(("parallel", "parallel", "arbitrary")),
    )(*args)


def _rms_rows(v):
    return lax.rsqrt(jnp.mean(v * v, axis=-1, keepdims=True) + EPS)


def _rmsmod(x, g, ada, sc_col, sh_col, *, name):
    S, D = x.shape
    tm = _tile(S, 256)

    def body(x_ref, g_ref, sc_ref, sh_ref, h_ref):
        xv = x_ref[...]
        h = (xv * _rms_rows(xv) * g_ref[...]) * (1.0 + sc_ref[...]) + sh_ref[...]
        h_ref[...] = h.astype(h_ref.dtype)

    return pl.pallas_call(
        body, name=name, grid=(S // tm,),
        in_specs=[_rows(tm, D), _vec(D), _vec(D, sc_col), _vec(D, sh_col)],
        out_specs=_rows(tm, D), out_shape=jax.ShapeDtypeStruct((S, D), BF16),
        compiler_params=_params(("parallel",)),
    )(x, g, ada, ada)


def _latent_norm(proj, g_q, g_kv, ql):
    S = proj.shape[0]
    tm = _tile(S, 512)

    def body(cq_ref, ckv_ref, gq_ref, gkv_ref, oq_ref, okv_ref):
        cq = cq_ref[...].astype(F32)
        oq_ref[...] = (cq * _rms_rows(cq) * gq_ref[...]).astype(BF16)
        ckv = ckv_ref[...].astype(F32)
        okv_ref[...] = (ckv * _rms_rows(ckv) * gkv_ref[...]).astype(BF16)

    return pl.pallas_call(
        body, name="latent_norm", grid=(S // tm,),
        in_specs=[_rows(tm, ql, 0), _rows(tm, ql, 1), _vec(ql), _vec(ql)],
        out_specs=[_rows(tm, ql), _rows(tm, ql)],
        out_shape=[jax.ShapeDtypeStruct((S, ql), BF16)] * 2,
        compiler_params=_params(("parallel",)),
    )(proj, proj, g_q, g_kv)


def _rope_fwd(y, c, s1, s2):
    return y * c + pltpu.roll(y, ROPE // 2, 1) * s1 + pltpu.roll(y, HEAD_PAD - ROPE // 2, 1) * s2


def _rope_bwd(d, c, s1, s2):
    return d * c + pltpu.roll(d * s1, HEAD_PAD - ROPE // 2, 1) + pltpu.roll(d * s2, ROPE // 2, 1)


def _head_rms(v):
    return lax.rsqrt(jnp.sum(v * v, axis=-1, keepdims=True) * (1.0 / QK_DIM) + EPS)


def _q_prep(q0, g_qh, tabs, nh):
    S = q0.shape[0]
    tm = _tile(S, 256)

    def body(q_ref, g_ref, c_ref, s1_ref, s2_ref, o_ref):
        c, s1, s2, g = c_ref[...], s1_ref[...], s2_ref[...], g_ref[...]
        for h in range(nh):
            sl = slice(h * HEAD_PAD, (h + 1) * HEAD_PAD)
            xs = q_ref[:, sl]
            o_ref[:, sl] = (_rope_fwd(xs * _head_rms(xs) * g, c, s1, s2) * (QK_DIM ** -0.5)).astype(BF16)

    w = nh * HEAD_PAD
    return pl.pallas_call(
        body, name="mla_q_prep", grid=(S // tm,),
        in_specs=[_rows(tm, w), _vec(HEAD_PAD)] + [_rows(tm, HEAD_PAD)] * 3,
        out_specs=_rows(tm, w), out_shape=jax.ShapeDtypeStruct((S, w), BF16),
        compiler_params=_params(("parallel",)),
    )(q0, g_qh, *tabs)


def _k_prep(kv0, proj, kpe_col, g_kh, tabs, nh):
    S = kv0.shape[0]
    tm = _tile(S, 256)

    def body(kv_ref, kpe_ref, g_ref, c_ref, s1_ref, s2_ref, o_ref):
        c, s1, s2, g = c_ref[...], s1_ref[...], s2_ref[...], g_ref[...]
        kpe = kpe_ref[...].astype(F32)
        for h in range(nh):
            k0 = jnp.concatenate([kv_ref[:, h * HEAD:(h + 1) * HEAD], kpe], axis=1)
            o_ref[:, h * HEAD_PAD:(h + 1) * HEAD_PAD] = _rope_fwd(k0 * _head_rms(k0) * g, c, s1, s2).astype(BF16)

    return pl.pallas_call(
        body, name="mla_k_prep", grid=(S // tm,),
        in_specs=[_rows(tm, nh * HEAD, 0), _rows(tm, LANE, kpe_col), _vec(HEAD_PAD)] + [_rows(tm, HEAD_PAD)] * 3,
        out_specs=_rows(tm, nh * HEAD_PAD), out_shape=jax.ShapeDtypeStruct((S, nh * HEAD_PAD), BF16),
        compiler_params=_params(("parallel",)),
    )(kv0, proj, g_kh, *tabs)


def _gate_merge(pa, pb, proj, gla_col, glb_col):
    S, D = pa.shape
    tm = _tile(S, 256)

    def body(pa_ref, pb_ref, ga_ref, gb_ref, o_ref):
        o_ref[...] = (jax.nn.sigmoid(ga_ref[...].astype(F32)) * pa_ref[...] + jax.nn.sigmoid(gb_ref[...].astype(F32)) * pb_ref[...]).astype(BF16)

    return pl.pallas_call(
        body, name="gate_merge", grid=(S // tm,),
        in_specs=[_rows(tm, D), _rows(tm, D), _rows(tm, D, gla_col), _rows(tm, D, glb_col)],
        out_specs=_rows(tm, D), out_shape=jax.ShapeDtypeStruct((S, D), BF16),
        compiler_params=_params(("parallel",)),
    )(pa, pb, proj, proj)


def _resid_rmsmod(x, o, g, ada, gt_col, sc_col, sh_col):
    S, D = x.shape
    tm = _tile(S, 256)

    def body(x_ref, o_ref, g_ref, gt_ref, sc_ref, sh_ref, x2_ref, h_ref):
        x2 = x_ref[...] + gt_ref[...] * o_ref[...]
        x2_ref[...] = x2
        h_ref[...] = ((x2 * _rms_rows(x2) * g_ref[...]) * (1.0 + sc_ref[...]) + sh_ref[...]).astype(BF16)

    return pl.pallas_call(
        body, name="resid_rmsmod2", grid=(S // tm,),
        in_specs=[_rows(tm, D), _rows(tm, D), _vec(D), _vec(D, gt_col), _vec(D, sc_col), _vec(D, sh_col)],
        out_specs=[_rows(tm, D), _rows(tm, D)],
        out_shape=[jax.ShapeDtypeStruct((S, D), F32), jax.ShapeDtypeStruct((S, D), BF16)],
        compiler_params=_params(("parallel",)),
    )(x, o, g, ada, ada, ada)


def _swiglu_tile(ib):
    def fn(r):
        pairs = r.shape[1] // (2 * ib)
        act = [jax.nn.silu(r[:, 2 * p * ib:(2 * p + 1) * ib]) * r[:, (2 * p + 1) * ib:(2 * p + 2) * ib] for p in range(pairs)]
        return r, jnp.concatenate(act, axis=1) if pairs > 1 else act[0]
    return fn


def _swiglu_bwd_tile(ib):
    def fn(d, ff):
        ff = ff.astype(F32)
        out = []
        for p in range(d.shape[1] // ib):
            dp = d[:, p * ib:(p + 1) * ib]
            g = ff[:, 2 * p * ib:(2 * p + 1) * ib]
            u = ff[:, (2 * p + 1) * ib:(2 * p + 2) * ib]
            sg = jax.nn.sigmoid(g)
            out += [dp * u * sg * (1.0 + g * (1.0 - sg)), dp * g * sg]
        return (jnp.concatenate(out, axis=1),)
    return fn


def _loss_head(x2, f, tgt, ada, gt_col):
    S, D = x2.shape
    tm = _tile(S, 256)

    def body(x2_ref, f_ref, t_ref, gt_ref, dy_ref, df_ref, red_ref, loss_ref):
        @pl.when(pl.program_id(0) == 0)
        def _():
            red_ref[...] = jnp.zeros_like(red_ref)
            loss_ref[...] = jnp.zeros_like(loss_ref)

        fv = f_ref[...]
        gt = gt_ref[...]
        err = x2_ref[...] + gt * fv - t_ref[...]
        dy = err * (1.0 / D)
        dy_ref[...] = dy
        df_ref[...] = (dy * gt).astype(BF16)
        red_ref[0:1, :] += jnp.sum(dy * fv, axis=0, keepdims=True)
        loss_ref[...] += (0.5 / D) * jnp.sum(err * err)

    return pl.pallas_call(
        body, name="loss_head", grid=(S // tm,),
        in_specs=[_rows(tm, D), _rows(tm, D), _rows(tm, D), _vec(D, gt_col)],
        out_specs=[_rows(tm, D), _rows(tm, D), _vec(D, rows=8), _vec(LANE, rows=8)],
        out_shape=[jax.ShapeDtypeStruct((S, D), F32), jax.ShapeDtypeStruct((S, D), BF16),
                   jax.ShapeDtypeStruct((8, D), F32), jax.ShapeDtypeStruct((8, LANE), F32)],
        compiler_params=_params(("arbitrary",)),
    )(x2, f, tgt, ada)


def _rmsmod2_bwd(dh2, x2, dy, o, g, ada, sc_col, gt_col):
    S, D = x2.shape
    tm = _tile(S, 256)

    def body(dh_ref, x2_ref, dy_ref, o_ref, g_ref, sc_ref, gt_ref, dx_ref, do_ref, red_ref):
        @pl.when(pl.program_id(0) == 0)
        def _():
            red_ref[...] = jnp.zeros_like(red_ref)

        dh = dh_ref[...]
        x2 = x2_ref[...]
        gv = g_ref[...]
        mod = 1.0 + sc_ref[...]
        r = _rms_rows(x2)
        xn = x2 * r
        t = dh * xn
        red_ref[0:1, :] += jnp.sum(dh, axis=0, keepdims=True)
        red_ref[1:2, :] += jnp.sum(t * gv, axis=0, keepdims=True)
        red_ref[2:3, :] += jnp.sum(t * mod, axis=0, keepdims=True)
        dxn = dh * gv * mod
        dx = dy_ref[...] + r * (dxn - xn * jnp.mean(dxn * xn, axis=-1, keepdims=True))
        dx_ref[...] = dx
        red_ref[3:4, :] += jnp.sum(dx * o_ref[...], axis=0, keepdims=True)
        do_ref[...] = (dx * gt_ref[...]).astype(BF16)

    return pl.pallas_call(
        body, name="rmsmod2_bwd", grid=(S // tm,),
        in_specs=[_rows(tm, D)] * 4 + [_vec(D), _vec(D, sc_col), _vec(D, gt_col)],
        out_specs=[_rows(tm, D), _rows(tm, D), _vec(D, rows=8)],
        out_shape=[jax.ShapeDtypeStruct((S, D), F32), jax.ShapeDtypeStruct((S, D), BF16),
                   jax.ShapeDtypeStruct((8, D), F32)],
        compiler_params=_params(("arbitrary",)),
    )(dh2, x2, dy, o, g, ada, ada)


def _rmsmod1_bwd(dh, x, dx2, g, ada, sc_col):
    S, D = x.shape
    tm = _tile(S, 256)

    def body(dh_ref, x_ref, dx2_ref, g_ref, sc_ref, gx_ref, red_ref):
        @pl.when(pl.program_id(0) == 0)
        def _():
            red_ref[...] = jnp.zeros_like(red_ref)

        dh = dh_ref[...]
        xv = x_ref[...]
        gv = g_ref[...]
        mod = 1.0 + sc_ref[...]
        r = _rms_rows(xv)
        xn = xv * r
        t = dh * xn
        red_ref[0:1, :] += jnp.sum(dh, axis=0, keepdims=True)
        red_ref[1:2, :] += jnp.sum(t * gv, axis=0, keepdims=True)
        red_ref[2:3, :] += jnp.sum(t * mod, axis=0, keepdims=True)
        dxn = dh * gv * mod
        gx_ref[...] = dx2_ref[...] + r * (dxn - xn * jnp.mean(dxn * xn, axis=-1, keepdims=True))

    return pl.pallas_call(
        body, name="rmsmod1_bwd", grid=(S // tm,),
        in_specs=[_rows(tm, D)] * 3 + [_vec(D), _vec(D, sc_col)],
        out_specs=[_rows(tm, D), _vec(D, rows=8)],
        out_shape=[jax.ShapeDtypeStruct((S, D), F32), jax.ShapeDtypeStruct((8, D), F32)],
        compiler_params=_params(("arbitrary",)),
    )(dh, x, dx2, g, ada)


def _gate_bwd(dm, pa, pb, proj, gla_col, glb_col):
    S, D = pa.shape
    tm = _tile(S, 256)

    def body(dm_ref, pa_ref, pb_ref, la_ref, lb_ref, dpa_ref, dpb_ref, dla_ref, dlb_ref):
        dm_ = dm_ref[...]
        ga = jax.nn.sigmoid(la_ref[...].astype(F32))
        gb = jax.nn.sigmoid(lb_ref[...].astype(F32))
        dpa_ref[...] = (dm_ * ga).astype(BF16)
        dpb_ref[...] = (dm_ * gb).astype(BF16)
        dla_ref[...] = (dm_ * pa_ref[...] * ga * (1.0 - ga)).astype(BF16)
        dlb_ref[...] = (dm_ * pb_ref[...] * gb * (1.0 - gb)).astype(BF16)

    return pl.pallas_call(
        body, name="gate_bwd", grid=(S // tm,),
        in_specs=[_rows(tm, D)] * 3 + [_rows(tm, D, gla_col), _rows(tm, D, glb_col)],
        out_specs=[_rows(tm, D)] * 4, out_shape=[jax.ShapeDtypeStruct((S, D), BF16)] * 4,
        compiler_params=_params(("parallel",)),
    )(dm, pa, pb, proj, proj)


def _q_prep_bwd(dq, q0, g_qh, tabs, nh):
    S = q0.shape[0]
    tm = _tile(S, 256)

    def body(dq_ref, q_ref, g_ref, c_ref, s1_ref, s2_ref, o_ref, red_ref):
        @pl.when(pl.program_id(0) == 0)
        def _():
            red_ref[...] = jnp.zeros_like(red_ref)

        c, s1, s2, g = c_ref[...], s1_ref[...], s2_ref[...], g_ref[...]
        dg = jnp.zeros((1, HEAD_PAD), F32)
        for h in range(nh):
            sl = slice(h * HEAD_PAD, (h + 1) * HEAD_PAD)
            d1 = _rope_bwd(dq_ref[:, sl], c, s1, s2)
            xs = q_ref[:, sl]
            r = _head_rms(xs)
            qn = xs * r
            dg = dg + jnp.sum(d1 * qn, axis=0, keepdims=True)
            dn = d1 * g
            o_ref[:, sl] = (r * (dn - qn * (jnp.sum(dn * qn, axis=-1, keepdims=True) * (1.0 / QK_DIM)))).astype(BF16)
        red_ref[0:1, :] += dg

    w = nh * HEAD_PAD
    return pl.pallas_call(
        body, name="mla_q_prep_bwd", grid=(S // tm,),
        in_specs=[_rows(tm, w), _rows(tm, w), _vec(HEAD_PAD)] + [_rows(tm, HEAD_PAD)] * 3,
        out_specs=[_rows(tm, w), _vec(HEAD_PAD, rows=8)],
        out_shape=[jax.ShapeDtypeStruct((S, w), BF16), jax.ShapeDtypeStruct((8, HEAD_PAD), F32)],
        compiler_params=_params(("arbitrary",)),
    )(dq, q0, g_qh, *tabs)


def _k_prep_bwd(dk, dv, kv0, proj, kpe_col, g_kh, tabs, nh):
    S = kv0.shape[0]
    tm = _tile(S, 256)
    wv = nh * HEAD

    def body(dk_ref, dv_ref, kv_ref, kpe_ref, g_ref, c_ref, s1_ref, s2_ref, o_ref, dpe_ref, red_ref):
        @pl.when(pl.program_id(0) == 0)
        def _():
            red_ref[...] = jnp.zeros_like(red_ref)

        c, s1, s2, g = c_ref[...], s1_ref[...], s2_ref[...], g_ref[...]
        kpe = kpe_ref[...].astype(F32)
        dg = jnp.zeros((1, HEAD_PAD), F32)
        dpe = jnp.zeros((tm, LANE), F32)
        for h in range(nh):
            d1 = _rope_bwd(dk_ref[:, h * HEAD_PAD:(h + 1) * HEAD_PAD], c, s1, s2)
            k0 = jnp.concatenate([kv_ref[:, h * HEAD:(h + 1) * HEAD], kpe], axis=1)
            r = _head_rms(k0)
            kn = k0 * r
            dg = dg + jnp.sum(d1 * kn, axis=0, keepdims=True)
            dn = d1 * g
            dk0 = r * (dn - kn * (jnp.sum(dn * kn, axis=-1, keepdims=True) * (1.0 / QK_DIM)))
            o_ref[:, h * HEAD:(h + 1) * HEAD] = dk0[:, :HEAD].astype(BF16)
            dpe = dpe + dk0[:, HEAD:]
        o_ref[:, wv:] = dv_ref[...].astype(BF16)
        dpe_ref[...] = dpe.astype(BF16)
        red_ref[0:1, :] += dg

    return pl.pallas_call(
        body, name="mla_k_prep_bwd", grid=(S // tm,),
        in_specs=[_rows(tm, nh * HEAD_PAD), _rows(tm, wv), _rows(tm, wv, 0), _rows(tm, LANE, kpe_col),
                  _vec(HEAD_PAD)] + [_rows(tm, HEAD_PAD)] * 3,
        out_specs=[_rows(tm, 2 * wv), _rows(tm, LANE), _vec(HEAD_PAD, rows=8)],
        out_shape=[jax.ShapeDtypeStruct((S, 2 * wv), BF16), jax.ShapeDtypeStruct((S, LANE), BF16),
                   jax.ShapeDtypeStruct((8, HEAD_PAD), F32)],
        compiler_params=_params(("arbitrary",)),
    )(dk, dv, kv0, proj, g_kh, *tabs)


def _latent_norm_bwd(dcqn, dckvn, proj, g_q, g_kv, ql):
    S = proj.shape[0]
    tm = _tile(S, 512)

    def body(dq_ref, dkv_ref, cq_ref, ckv_ref, gq_ref, gkv_ref, oq_ref, okv_ref, red_ref):
        @pl.when(pl.program_id(0) == 0)
        def _():
            red_ref[...] = jnp.zeros_like(red_ref)

        for row, (d_ref, c_ref, g_ref, o_ref) in enumerate(((dq_ref, cq_ref, gq_ref, oq_ref),
                                                            (dkv_ref, ckv_ref, gkv_ref, okv_ref))):
            d = d_ref[...]
            cv = c_ref[...].astype(F32)
            r = _rms_rows(cv)
            ch = cv * r
            red_ref[row:row + 1, :] += jnp.sum(d * ch, axis=0, keepdims=True)
            dn = d * g_ref[...]
            o_ref[...] = (r * (dn - ch * jnp.mean(dn * ch, axis=-1, keepdims=True))).astype(BF16)

    return pl.pallas_call(
        body, name="latent_norm_bwd", grid=(S // tm,),
        in_specs=[_rows(tm, ql), _rows(tm, ql), _rows(tm, ql, 0), _rows(tm, ql, 1), _vec(ql), _vec(ql)],
        out_specs=[_rows(tm, ql), _rows(tm, ql), _vec(ql, rows=8)],
        out_shape=[jax.ShapeDtypeStruct((S, ql), BF16)] * 2 + [jax.ShapeDtypeStruct((8, ql), F32)],
        compiler_params=_params(("arbitrary",)),
    )(dcqn, dckvn, proj, proj, g_q, g_kv)


NEG = -1e30
ATT_TILE = 512
SB_TILE = 512
SB_SUB = 128
_NT = (((1,), (1,)), ((), ()))
_TN = (((0,), (0,)), ((), ()))


def _dot(a, b, dn=(((1,), (0,)), ((), ()))):
    return lax.dot_general(a, b, dn, preferred_element_type=F32)


def _key_rows(kb, t):
    return pl.ds(pl.multiple_of(kb * t, t), t)


def _diag_mask(t, strict):
    r = lax.broadcasted_iota(jnp.int32, (t, t), 0)
    c = lax.broadcasted_iota(jnp.int32, (t, t), 1)
    return c < r if strict else c <= r


def _mla_fwd(q, k, kv0, nh):
    S = q.shape[0]
    t = _tile(S, ATT_TILE)

    def body(q_ref, k_ref, v_ref, o_ref, lse_ref):
        i = pl.program_id(1)
        qv = q_ref[...]

        def block(kb, carry, masked):
            m, l, acc = carry
            rows = _key_rows(kb, t)
            s = _dot(qv, k_ref[rows, :], _NT)
            if masked:
                s = jnp.where(_diag_mask(t, False), s, NEG)
            m_new = jnp.maximum(m, jnp.max(s, axis=-1, keepdims=True))
            alpha = jnp.exp(m - m_new)
            p = jnp.exp(s - m_new)
            l = alpha * l + jnp.sum(p, axis=-1, keepdims=True)
            acc = alpha * acc + _dot(p.astype(BF16), v_ref[rows, :].astype(BF16))
            return m_new, l, acc

        init = (jnp.full((t, 1), NEG, F32), jnp.zeros((t, 1), F32), jnp.zeros((t, HEAD), F32))
        carry = lax.fori_loop(0, i, lambda kb, c: block(kb, c, False), init)
        m, l, acc = block(i, carry, True)
        o_ref[...] = acc / l
        lse_ref[...] = m + jnp.log(l)

    return pl.pallas_call(
        body, name="mla_attn_fwd", grid=(nh, S // t),
        in_specs=[pl.BlockSpec((t, HEAD_PAD), lambda h, i: (i, h)),
                  pl.BlockSpec((S, HEAD_PAD), lambda h, i: (0, h)),
                  pl.BlockSpec((S, HEAD), lambda h, i: (0, nh + h))],
        out_specs=[pl.BlockSpec((t, HEAD), lambda h, i: (i, h)),
                   pl.BlockSpec((None, t, 1), lambda h, i: (h, i, 0))],
        out_shape=[jax.ShapeDtypeStruct((S, nh * HEAD), F32), jax.ShapeDtypeStruct((nh, S, 1), F32)],
        compiler_params=_params(("parallel", "arbitrary")),
    )(q, k, kv0)


def _mla_bwd(q, k, kv0, o, do, lse, nh):
    S = q.shape[0]
    t = _tile(S, ATT_TILE)
    scale = QK_DIM ** -0.5

    def body(q_ref, k_ref, v_ref, o_ref, do_ref, lse_ref, dq_ref, dk_ref, dv_ref):
        i = pl.program_id(1)

        @pl.when(i == 0)
        def _():
            dk_ref[...] = jnp.zeros_like(dk_ref)
            dv_ref[...] = jnp.zeros_like(dv_ref)

        qv = q_ref[...]
        dov = do_ref[...]
        delta = jnp.sum(dov * o_ref[...], axis=-1, keepdims=True)
        dob = dov.astype(BF16)
        lse = lse_ref[...]

        def block(kb, dq, masked):
            rows = _key_rows(kb, t)
            ks = k_ref[rows, :]
            vs = v_ref[rows, :].astype(BF16)
            p = jnp.exp(_dot(qv, ks, _NT) - lse)
            if masked:
                p = jnp.where(_diag_mask(t, False), p, 0.0)
            ds = (p * (_dot(dob, vs, _NT) - delta)).astype(BF16)
            dk_ref[rows, :] += _dot(ds, qv, _TN)
            dv_ref[rows, :] += _dot(p.astype(BF16), dob, _TN)
            return dq + _dot(ds, ks)

        dq = lax.fori_loop(0, i, lambda kb, c: block(kb, c, False), jnp.zeros((t, HEAD_PAD), F32))
        dq_ref[...] = block(i, dq, True) * scale

    return pl.pallas_call(
        body, name="mla_attn_bwd", grid=(nh, S // t),
        in_specs=[pl.BlockSpec((t, HEAD_PAD), lambda h, i: (i, h)),
                  pl.BlockSpec((S, HEAD_PAD), lambda h, i: (0, h)),
                  pl.BlockSpec((S, HEAD), lambda h, i: (0, nh + h)),
                  pl.BlockSpec((t, HEAD), lambda h, i: (i, h)),
                  pl.BlockSpec((t, HEAD), lambda h, i: (i, h)),
                  pl.BlockSpec((None, t, 1), lambda h, i: (h, i, 0))],
        out_specs=[pl.BlockSpec((t, HEAD_PAD), lambda h, i: (i, h)),
                   pl.BlockSpec((S, HEAD_PAD), lambda h, i: (0, h)),
                   pl.BlockSpec((S, HEAD), lambda h, i: (0, h))],
        out_shape=[jax.ShapeDtypeStruct((S, nh * HEAD_PAD), F32), jax.ShapeDtypeStruct((S, nh * HEAD_PAD), F32),
                   jax.ShapeDtypeStruct((S, nh * HEAD), F32)],
        compiler_params=_params(("parallel", "arbitrary")),
    )(q, k, kv0, o, do, lse)


def _tri(n, cmp):
    r = lax.broadcasted_iota(jnp.int32, (n, n), 0)
    c = lax.broadcasted_iota(jnp.int32, (n, n), 1)
    return jnp.where(cmp(r, c), 1.0, 0.0).astype(BF16)


def _sb_block(qv, ks, run, upper, t, masked):
    z = _dot(qv, ks, _NT)
    lb = jnp.minimum(z, 0.0) - jnp.log(1.0 + jnp.exp(-jnp.abs(z)))
    lom = lb - z
    mask = _diag_mask(t, True) if masked else None
    if masked:
        lom = jnp.where(mask, lom, 0.0)
    tails = []
    for sblk in reversed(range(t // SB_SUB)):
        part = lom[:, sblk * SB_SUB:(sblk + 1) * SB_SUB]
        tails.append(_dot(part.astype(BF16), upper) + run)
        run = run + jnp.sum(part, axis=-1, keepdims=True)
    a = jnp.exp(lb + jnp.concatenate(tails[::-1], axis=1))
    if masked:
        a = jnp.where(mask, a, 0.0)
    return a, lb, mask, run


def _sb_fwd(proj, q_col, k_col, v_col, nh):
    S = proj.shape[0]
    t = _tile(S, SB_TILE)
    assert S // t <= LANE
    scale = HEAD ** -0.5

    def body(q_ref, k_ref, v_ref, o_ref, runs_ref):
        i = pl.program_id(1)
        qv = (q_ref[...].astype(F32) * scale).astype(BF16)
        upper = _tri(SB_SUB, lambda j, s: j > s)
        lane = lax.broadcasted_iota(jnp.int32, (t, LANE), 1)

        def block(kb, carry, masked):
            run, acc, runs = carry
            runs = jnp.where(lane == kb, run, runs)
            rows = _key_rows(kb, t)
            a, _, _, run = _sb_block(qv, k_ref[rows, :].astype(BF16), run, upper, t, masked)
            return run, acc + _dot(a.astype(BF16), v_ref[rows, :].astype(BF16)), runs

        carry = block(i, (jnp.zeros((t, 1), F32), jnp.zeros((t, HEAD), F32), jnp.zeros((t, LANE), F32)), True)
        _, o_ref[...], runs_ref[...] = lax.fori_loop(0, i, lambda j, c: block(i - 1 - j, c, False), carry)

    return pl.pallas_call(
        body, name="sb_attn_fwd", grid=(nh, S // t),
        in_specs=[pl.BlockSpec((t, HEAD), lambda h, i: (i, q_col + h)),
                  pl.BlockSpec((S, HEAD), lambda h, i: (0, k_col + h)),
                  pl.BlockSpec((S, HEAD), lambda h, i: (0, v_col + h))],
        out_specs=[pl.BlockSpec((t, HEAD), lambda h, i: (i, h)), pl.BlockSpec((None, t, LANE), lambda h, i: (h, i, 0))],
        out_shape=[jax.ShapeDtypeStruct((S, nh * HEAD), F32), jax.ShapeDtypeStruct((nh, S, LANE), F32)],
        compiler_params=_params(("parallel", "arbitrary")),
    )(proj, proj, proj)


def _sb_bwd(proj, q_col, k_col, v_col, dy, runs, nh):
    S = proj.shape[0]
    t = _tile(S, SB_TILE)
    scale = HEAD ** -0.5

    def body(q_ref, k_ref, v_ref, dy_ref, runs_ref, dq_ref, dk_ref, dv_ref):
        i = pl.program_id(1)

        @pl.when(i == 0)
        def _():
            dk_ref[...] = jnp.zeros_like(dk_ref)
            dv_ref[...] = jnp.zeros_like(dv_ref)

        qv = (q_ref[...].astype(F32) * scale).astype(BF16)
        dyb = dy_ref[...].astype(BF16)
        runs_v = runs_ref[...]
        lane = lax.broadcasted_iota(jnp.int32, (t, LANE), 1)
        upper = _tri(SB_SUB, lambda j, s: j > s)
        before = _tri(SB_SUB, lambda s, j: s < j)

        def block(kb, carry, masked):
            prefix, dq = carry
            rows = _key_rows(kb, t)
            ks = k_ref[rows, :].astype(BF16)
            vs = v_ref[rows, :].astype(BF16)
            run = jnp.sum(jnp.where(lane == kb, runs_v, 0.0), axis=-1, keepdims=True)
            a, lb, mask, _ = _sb_block(qv, ks, run, upper, t, masked)
            dl = a * _dot(dyb, vs, _NT)
            lefts = []
            for sblk in range(t // SB_SUB):
                part = dl[:, sblk * SB_SUB:(sblk + 1) * SB_SUB]
                lefts.append(_dot(part.astype(BF16), before) + prefix)
                prefix = prefix + jnp.sum(part, axis=-1, keepdims=True)
            beta = jnp.exp(lb)
            dz = dl * (1.0 - beta) - beta * jnp.concatenate(lefts, axis=1)
            if masked:
                dz = jnp.where(mask, dz, 0.0)
            dz = dz.astype(BF16)
            dk_ref[rows, :] += _dot(dz, qv, _TN)
            dv_ref[rows, :] += _dot(a.astype(BF16), dyb, _TN)
            return prefix, dq + _dot(dz, ks)

        carry = lax.fori_loop(0, i, lambda kb, c: block(kb, c, False),
                              (jnp.zeros((t, 1), F32), jnp.zeros((t, HEAD), F32)))
        dq_ref[...] = block(i, carry, True)[1] * scale

    full = pl.BlockSpec((S, HEAD), lambda h, i: (0, h))
    tile = pl.BlockSpec((t, HEAD), lambda h, i: (i, h))
    return pl.pallas_call(
        body, name="sb_attn_bwd", grid=(nh, S // t),
        in_specs=[pl.BlockSpec((t, HEAD), lambda h, i: (i, q_col + h)),
                  pl.BlockSpec((S, HEAD), lambda h, i: (0, k_col + h)),
                  pl.BlockSpec((S, HEAD), lambda h, i: (0, v_col + h)), tile,
                  pl.BlockSpec((None, t, LANE), lambda h, i: (h, i, 0))],
        out_specs=[tile, full, full],
        out_shape=[jax.ShapeDtypeStruct((S, nh * HEAD), F32)] * 3,
        compiler_params=_params(("parallel", "arbitrary")),
    )(proj, proj, proj, dy, runs)


def _place():
    return lax.axis_index("x"), lax.axis_index("y"), lax.axis_index("c")


def _other_chips(x, y):
    return [(1 - x, y), (x, 1 - y), (1 - x, 1 - y)]


def _dev_index(p):
    return 4 * p[0] + 2 * p[1] + p[2]


def _gather_blocks(blocks, *, name, in_vmem):
    n = len(blocks)
    per = 7

    def body(*refs):
        ins, outs = refs[:n], refs[n:2 * n]
        send_sems, recv_sems, local_sems = refs[2 * n:]
        x, y, c = _place()
        me, sibling = (x, y, c), (x, y, 1 - c)
        chips = _other_chips(x, y)

        def slot(a, p):
            return outs[a].at[_dev_index(p)]

        def copy(a, k, block, to, src=None):
            return pltpu.make_async_remote_copy(
                src_ref=slot(a, block) if src is None else src, dst_ref=slot(a, block),
                send_sem=send_sems.at[a * per + k], recv_sem=recv_sems.at[a * per + k],
                device_id=to, device_id_type=MESH)

        mine = [pltpu.make_async_copy(ins[a], slot(a, me), local_sems.at[a]) for a in range(n)] if in_vmem else []
        for cp in mine:
            cp.start()
        first = []
        for a in range(n):
            first.append(copy(a, 0, me, sibling, src=ins[a]))
            first += [copy(a, 1 + j, me, (*chip, c), src=ins[a]) for j, chip in enumerate(chips)]
        for cp in first:
            cp.start()
        passed = []
        for a in range(n):
            for j, chip in enumerate(chips):
                copy(a, 1 + j, (*chip, c), me).wait_recv()
                cp = copy(a, 4 + j, (*chip, c), sibling)
                cp.start()
                passed.append(cp)
        for a in range(n):
            copy(a, 0, sibling, me).wait_recv()
            for j, chip in enumerate(chips):
                copy(a, 4 + j, (*chip, 1 - c), me).wait_recv()
        for cp in first + passed:
            cp.wait_send()
        for cp in mine:
            cp.wait()

    space = pltpu.VMEM if in_vmem else pl.ANY
    spec = pl.BlockSpec(memory_space=space)
    outs = pl.pallas_call(
        body, name=name, in_specs=[spec] * n, out_specs=[spec] * n,
        out_shape=[jax.ShapeDtypeStruct((N_DEV,) + b.shape, b.dtype) for b in blocks],
        scratch_shapes=[pltpu.SemaphoreType.DMA((n * per,)), pltpu.SemaphoreType.DMA((n * per,)),
                        pltpu.SemaphoreType.DMA((n,))],
        compiler_params=pltpu.CompilerParams(vmem_limit_bytes=VMEM_LIMIT),
    )(*blocks)
    return list(outs)


def _sibling_swap(arrs, *, name, whole=False):
    n = len(arrs)

    def body(*refs):
        ins, outs = refs[:n], refs[n:2 * n]
        send_sems, recv_sems = refs[2 * n:]
        x, y, c = _place()
        copies = [pltpu.make_async_remote_copy(
            src_ref=ins[a] if whole else ins[a].at[1 - c], dst_ref=outs[a],
            send_sem=send_sems.at[a], recv_sem=recv_sems.at[a],
            device_id=(x, y, 1 - c), device_id_type=MESH) for a in range(n)]
        for cp in copies:
            cp.start()
        for cp in copies:
            cp.wait()

    spec = pl.BlockSpec(memory_space=pl.ANY)
    return list(pl.pallas_call(
        body, name=name, in_specs=[spec] * n, out_specs=[spec] * n,
        out_shape=[jax.ShapeDtypeStruct(a.shape if whole else a.shape[1:], a.dtype) for a in arrs],
        scratch_shapes=[pltpu.SemaphoreType.DMA((n,)), pltpu.SemaphoreType.DMA((n,))],
    )(*arrs))


_HBM = pl.BlockSpec(memory_space=pltpu.HBM)
_SEM = pl.BlockSpec(memory_space=pltpu.SEMAPHORE)
_EFFECT = pltpu.SideEffectType.DATAFLOW_SIDE_EFFECTING


def _in_hbm(a):
    return pltpu.with_memory_space_constraint(a, pltpu.HBM)


def _split_copies(srcs, lands, send_sems, recv_sems, plan):
    x, y, c = _place()
    copies = []
    for a, (src, land) in enumerate(zip(srcs, lands)):
        steps = plan(x, y, c)
        for k, (pick, slot, to) in enumerate(steps):
            copies.append(pltpu.make_async_remote_copy(
                src_ref=pick(src), dst_ref=slot(land), send_sem=send_sems.at[a * len(steps) + k],
                recv_sem=recv_sems.at[a * len(steps) + k], device_id=to, device_id_type=MESH))
    return copies


def _split_start(srcs, land_shapes, plan, per, *, name):
    n = len(srcs)

    def body(*refs):
        send_sems, recv_sems = refs[2 * n], refs[2 * n + 1]
        for cp in _split_copies(refs[:n], refs[n:2 * n], send_sems, recv_sems, plan):
            cp.start()
        token = refs[-1]
        token[...] = jnp.zeros_like(token)

    lands = [_in_hbm(lax.empty(s.shape, s.dtype)) for s in land_shapes]
    outs = pl.pallas_call(
        body, name=name,
        out_shape=(pltpu.SemaphoreType.DMA((n * per,)), pltpu.SemaphoreType.DMA((n * per,)),
                   *[pltpu.HBM(s.shape, s.dtype) for s in srcs], *[pltpu.HBM(s.shape, s.dtype) for s in land_shapes],
                   jax.ShapeDtypeStruct((8, LANE), F32)),
        in_specs=[_HBM] * (2 * n),
        out_specs=(_SEM, _SEM, *[_HBM] * (2 * n), pl.BlockSpec(memory_space=pltpu.VMEM)),
        input_output_aliases={i: 2 + i for i in range(2 * n)},
        compiler_params=pltpu.CompilerParams(has_side_effects=_EFFECT),
    )(*[_in_hbm(s) for s in srcs], *lands)
    return outs[0], outs[1], list(outs[2:2 + n]), list(outs[2 + n:2 + 2 * n]), outs[-1]


def _split_wait(send_sems, recv_sems, srcs, lands, after, plan, *, name):
    n = len(srcs)

    def body(*refs):
        for cp in _split_copies(refs[:n], refs[n:2 * n], refs[2 * n], refs[2 * n + 1], plan):
            cp.wait_send()
            cp.wait_recv()

    outs = pl.pallas_call(
        body, name=name,
        out_shape=(*[pltpu.HBM(s.shape, s.dtype) for s in srcs], *[pltpu.HBM(s.shape, s.dtype) for s in lands]),
        in_specs=[_HBM] * (2 * n) + [_SEM, _SEM, pl.BlockSpec(memory_space=pl.ANY)],
        out_specs=tuple([_HBM] * (2 * n)),
        input_output_aliases={i: i for i in range(2 * n)},
        compiler_params=pltpu.CompilerParams(has_side_effects=_EFFECT),
    )(*srcs, *lands, send_sems, recv_sems, after)
    return list(outs[:n]), list(outs[n:])


def _gather_plan(x, y, c):
    slot = lambda land: land.at[_dev_index((x, y, c))]
    whole = lambda src: src
    return [(whole, slot, (x, y, 1 - c))] + [(whole, slot, (px, py, c)) for px, py in _other_chips(x, y)]


def _swap_plan(x, y, c):
    return [(lambda src: src.at[1 - c], lambda land: land, (x, y, 1 - c))]


def _exchange_plan(x, y, c):
    return [(lambda src, k=2 * px + py: src.at[k], lambda land, j=j: land.at[j], (px, py, c))
            for j, (px, py) in enumerate(_other_chips(x, y))]


def _gather_forward(lands, *, name):
    n = len(lands)

    def body(*refs):
        lands_in, outs = refs[:n], refs[n:2 * n]
        send_sems, recv_sems = refs[2 * n:]
        x, y, c = _place()
        copies = []
        for a in range(n):
            for j, (px, py) in enumerate(_other_chips(x, y)):
                copies.append((pltpu.make_async_remote_copy(
                    src_ref=lands_in[a].at[_dev_index((px, py, c))], dst_ref=outs[a].at[_dev_index((px, py, c))],
                    send_sem=send_sems.at[3 * a + j], recv_sem=recv_sems.at[3 * a + j],
                    device_id=(x, y, 1 - c), device_id_type=MESH), a, j, (px, py)))
        for cp, _, _, _ in copies:
            cp.start()
        for cp, a, j, (px, py) in copies:
            cp.wait_send()
            pltpu.make_async_remote_copy(
                src_ref=lands_in[a].at[_dev_index((px, py, 1 - c))], dst_ref=outs[a].at[_dev_index((px, py, 1 - c))],
                send_sem=send_sems.at[3 * a + j], recv_sem=recv_sems.at[3 * a + j],
                device_id=(x, y, 1 - c), device_id_type=MESH).wait_recv()

    spec = pl.BlockSpec(memory_space=pl.ANY)
    return list(pl.pallas_call(
        body, name=name, in_specs=[spec] * n, out_specs=[spec] * n,
        out_shape=[jax.ShapeDtypeStruct(a.shape, a.dtype) for a in lands],
        input_output_aliases={a: a for a in range(n)},
        scratch_shapes=[pltpu.SemaphoreType.DMA((3 * n,)), pltpu.SemaphoreType.DMA((3 * n,))],
    )(*lands))


def _flat2(a, lead):
    return a.reshape(a.shape[:lead] + (-1, a.shape[-1]))


def _pair_sum(g, recv, c_idx, *, name):
    _, nchip, r, w = g.shape
    tm = _tile(r, 256) if r % 8 == 0 else r

    def body(c_ref, g_ref, r_ref, o_ref):
        o_ref[...] = (g_ref[...].astype(F32) + r_ref[...].astype(F32)).astype(o_ref.dtype)

    return pl.pallas_call(
        body, name=name,
        grid_spec=pltpu.PrefetchScalarGridSpec(
            num_scalar_prefetch=1, grid=(nchip, r // tm),
            in_specs=[pl.BlockSpec((None, None, tm, w), lambda k, i, c_ref: (c_ref[0], k, i, 0)),
                      pl.BlockSpec((None, tm, w), lambda k, i, c_ref: (k, i, 0))],
            out_specs=pl.BlockSpec((None, tm, w), lambda k, i, c_ref: (k, i, 0))),
        out_shape=jax.ShapeDtypeStruct((nchip, r, w), BF16),
        compiler_params=_params(("parallel", "parallel")),
    )(c_idx, g, recv)


def _chip_sum(s1, recv, chip_idx, *, name):
    _, r, w = s1.shape
    tm = _tile(r, 256) if r % 8 == 0 else r

    def body(k_ref, s_ref, r_ref, o_ref):
        acc = s_ref[...].astype(F32)
        for j in range(3):
            acc = acc + r_ref[j].astype(F32)
        o_ref[...] = acc

    return pl.pallas_call(
        body, name=name,
        grid_spec=pltpu.PrefetchScalarGridSpec(
            num_scalar_prefetch=1, grid=(r // tm,),
            in_specs=[pl.BlockSpec((None, tm, w), lambda i, k_ref: (k_ref[0], i, 0)),
                      pl.BlockSpec((3, tm, w), lambda i, k_ref: (0, i, 0))],
            out_specs=pl.BlockSpec((tm, w), lambda i, k_ref: (i, 0))),
        out_shape=jax.ShapeDtypeStruct((r, w), F32),
        compiler_params=_params(("parallel",)),
    )(chip_idx, s1, recv)


def _adam_math(w, g, m, v):
    m = ADAM_B1 * m + (1.0 - ADAM_B1) * g
    v = ADAM_B2 * v + (1.0 - ADAM_B2) * (g * g)
    m_hat = m / (1.0 - ADAM_B1 ** ADAM_STEP)
    v_hat = v / (1.0 - ADAM_B2 ** ADAM_STEP)
    delta = -ADAM_LR * (m_hat / (jnp.sqrt(v_hat) + ADAM_EPS) + ADAM_WD * w)
    return delta, m, v


def _adamw(w, mine, other, c_idx, m, v, *, name):
    r, cw = w.shape
    hr = r // 2
    tm = _row_tile(hr, 9 * cw * 4)

    def body(c_ref, w_ref, a_ref, b_ref, m_ref, v_ref, g_ref, d_ref, nm_ref, nv_ref):
        g = jnp.where(pl.program_id(0) == c_ref[0], a_ref[...], b_ref[...])
        g_ref[...] = g
        d_ref[...], nm_ref[...], nv_ref[...] = _adam_math(w_ref[...], g, m_ref[...], v_ref[...])

    full = pl.BlockSpec((None, tm, cw), lambda h, i, c_ref: (h, i, 0))
    half = pl.BlockSpec((tm, cw), lambda h, i, c_ref: (i, 0))
    outs = pl.pallas_call(
        body, name=name,
        grid_spec=pltpu.PrefetchScalarGridSpec(
            num_scalar_prefetch=1, grid=(2, hr // tm),
            in_specs=[full, half, half, full, full], out_specs=[full] * 4),
        out_shape=[jax.ShapeDtypeStruct((2, hr, cw), F32)] * 4,
        compiler_params=_params(("parallel", "parallel")),
    )(c_idx, w.reshape(2, hr, cw), mine, other, m.reshape(2, hr, cw), v.reshape(2, hr, cw))
    return [o.reshape(r, cw) for o in outs]


def _adamw_ada(cact_t, dada, w, m, v):
    r, cw = w.shape
    nb = cact_t.shape[1]
    tm = _tile(r, 256)
    tn = _tile(cw, 1024)

    def body(a_ref, d_ref, w_ref, m_ref, v_ref, g_ref, dl_ref, nm_ref, nv_ref):
        a = a_ref[...]
        d = d_ref[...]
        g = a[:, 0:1] * d[0:1, :]
        for b in range(1, nb):
            g = g + a[:, b:b + 1] * d[b:b + 1, :]
        g_ref[...] = g
        dl_ref[...], nm_ref[...], nv_ref[...] = _adam_math(w_ref[...], g, m_ref[...], v_ref[...])

    blk = pl.BlockSpec((tm, tn), lambda i, j: (i, j))
    return pl.pallas_call(
        body, name="adamw_ada", grid=(r // tm, cw // tn),
        in_specs=[pl.BlockSpec((tm, nb), lambda i, j: (i, 0)), pl.BlockSpec((nb, tn), lambda i, j: (0, j)), blk, blk, blk],
        out_specs=[blk] * 4, out_shape=[jax.ShapeDtypeStruct((r, cw), F32)] * 4,
        compiler_params=_params(("parallel", "parallel")),
    )(cact_t, dada, w, m, v)


def _adamw_vec(parts, w, m, v):
    n = w.shape[1]

    def body(p_ref, w_ref, m_ref, v_ref, g_ref, d_ref, nm_ref, nv_ref):
        p = p_ref[...]
        g = p[0:1, :]
        for b in range(1, N_DEV):
            g = g + p[b:b + 1, :]
        g_ref[...] = g
        d_ref[...], nm_ref[...], nv_ref[...] = _adam_math(w_ref[...], g, m_ref[...], v_ref[...])

    return pl.pallas_call(
        body, name="adamw_vec", out_shape=[jax.ShapeDtypeStruct((1, n), F32)] * 4,
        compiler_params=pltpu.CompilerParams(vmem_limit_bytes=VMEM_LIMIT),
    )(parts, w, m, v)


def _w_in_segments(kpe0, d_in, cs):
    segs = []
    for k in range(4):
        lo, hi = k * cs, (k + 1) * cs
        for a, b, shift in ((0, kpe0, 0), (kpe0, kpe0 + ROPE, d_in - ROPE - kpe0), (kpe0 + ROPE, d_in, -ROPE)):
            a, b = max(lo, a), min(hi, b)
            if a < b:
                segs.append((k, a - lo, a + shift, b - a))
    return segs


def _w_in_layout(g8, kpe0):
    _, hr, cs = g8.shape
    rows, d_in = 2 * hr, 4 * cs
    segs = _w_in_segments(kpe0, d_in, cs)
    tm = _tile(rows, 256)

    def body(g_ref, o_ref):
        for k, src, dst, w in segs:
            o_ref[:, dst:dst + w] = g_ref[k, :, src:src + w]
        o_ref[:, d_in:] = jnp.zeros((tm, ROPE), o_ref.dtype)

    return pl.pallas_call(
        body, name="w_in_layout", grid=(rows // tm,),
        in_specs=[pl.BlockSpec((4, tm, cs), lambda i: (0, i, 0))], out_specs=_rows(tm, d_in + ROPE),
        out_shape=jax.ShapeDtypeStruct((rows, d_in + ROPE), g8.dtype), compiler_params=_params(("parallel",)),
    )(g8.reshape(4, rows, cs))


def _w_in_grad_pieces(g, kpe0):
    rows, d_in_p = g.shape
    d_in = d_in_p - ROPE
    cs = d_in // 4
    segs = _w_in_segments(kpe0, d_in, cs)
    hr = rows // 2
    tm = _tile(hr, 256)
    per_half = hr // tm

    def body(g_ref, o_ref):
        for k, src, dst, w in segs:
            o_ref[k, :, src:src + w] = g_ref[:, dst:dst + w]

    return pl.pallas_call(
        body, name="w_in_grad_pieces", grid=(rows // tm,),
        in_specs=[_rows(tm, d_in_p)],
        out_specs=pl.BlockSpec((None, 4, tm, cs), lambda i: (i // per_half, 0, i % per_half, 0)),
        out_shape=jax.ShapeDtypeStruct((2, 4, hr, cs), g.dtype), compiler_params=_params(("parallel",)),
    )(g)


def _interleave_layout(g8, ib):
    _, hr, cs = g8.shape
    rows, per_chip, per_half = 2 * hr, cs // ib, 2 * cs // ib
    tm = _tile(rows, 2048)

    def src(jj):
        return jj // 2 + per_half * (jj % 2)

    def body(g_ref, o_ref):
        o_ref[...] = g_ref[...]

    return pl.pallas_call(
        body, name="interleave_layout", grid=(rows // tm, 4 * per_chip),
        in_specs=[pl.BlockSpec((None, tm, ib), lambda i, jj: (src(jj) // per_chip, i, src(jj) % per_chip))],
        out_specs=pl.BlockSpec((tm, ib), lambda i, jj: (i, jj)),
        out_shape=jax.ShapeDtypeStruct((rows, 4 * cs), g8.dtype), compiler_params=_params(("parallel", "parallel")),
    )(g8.reshape(4, rows, cs))


def _cols_from_chips(g8, rows):
    cs = g8.shape[-1]
    return g8.reshape(4, rows, cs).transpose(1, 0, 2).reshape(rows, 4 * cs)


def _cols_to_pieces(g):
    rows, c4 = g.shape
    return g.reshape(2, rows // 2, 4, c4 // 4).transpose(0, 2, 1, 3)


def _rows_to_pieces(g):
    r4, cols = g.shape
    return g.reshape(4, 2, r4 // 8, cols).transpose(1, 0, 2, 3)


def _pad_cols(a, w):
    return jnp.pad(a, ((0, 0), (0, w - a.shape[1])))


def kernel(x, c, positions, w_ada, b_ada, g_norm1, g_norm2, w_in, g_q_latent, g_kv_latent, w_uq, w_ukv, g_q_head, g_k_head, w_proj_mla, w_proj_sb, w_out, w_ffn_in, w_ffn_out, loss_target, m_w_ada, m_b_ada, m_g_norm1, m_g_norm2, m_w_in, m_g_q_latent, m_g_kv_latent, m_w_uq, m_w_ukv, m_g_q_head, m_g_k_head, m_w_proj_mla, m_w_proj_sb, m_w_out, m_w_ffn_in, m_w_ffn_out, v_w_ada, v_b_ada, v_g_norm1, v_g_norm2, v_w_in, v_g_q_latent, v_g_kv_latent, v_w_uq, v_w_ukv, v_g_q_head, v_g_k_head, v_w_proj_mla, v_w_proj_sb, v_w_out, v_w_ffn_in, v_w_ffn_out):
    xi, yi, ci = _place()
    chip = 2 * xi + yi
    dev = 2 * chip + ci
    c_idx = jnp.reshape(ci, (1,)).astype(jnp.int32)
    chip_idx = jnp.reshape(chip, (1,)).astype(jnp.int32)

    x = x[0]
    tgt = loss_target[0]
    S, D = x.shape
    ql = g_q_latent.shape[1]
    assert g_kv_latent.shape[1] == ql
    mlaw = w_proj_mla.shape[1]
    nh = mlaw // HEAD
    sbw = w_proj_sb.shape[1]
    assert sbw == mlaw
    dff = w_ffn_out.shape[1] * 4
    d_in = 2 * ql + ROPE + 3 * sbw + 2 * D
    d_in_p = d_in + ROPE
    q_col = (2 * ql) // HEAD
    k_col = q_col + nh
    v_col = k_col + nh
    gla_col = (2 * ql + 3 * sbw) // D
    glb_col = gla_col + 1
    kpe_col = (d_in - ROPE) // LANE
    assert (2 * ql + 3 * sbw) % D == 0 and (d_in - ROPE) % LANE == 0

    mats = {"w_in": w_in[0], "w_uq": w_uq[0], "w_ukv": w_ukv[0], "w_proj_mla": w_proj_mla[0],
            "w_proj_sb": w_proj_sb[0], "w_out": w_out[0], "w_ffn_in": w_ffn_in[0], "w_ffn_out": w_ffn_out[0]}
    names = list(mats)
    row_sharded = {"w_out", "w_ffn_out"}

    c_all = _gather_blocks([jnp.broadcast_to(c, (8, D))], name="gather_cond", in_vmem=True)[0][:, 0, :]
    n_ada = w_ada.shape[2]
    b_shard = lax.dynamic_slice_in_dim(b_ada, chip * n_ada, n_ada, axis=1)
    ada_shard = _mm(c_all, w_ada[0], name="ada_proj", a_fn=jax.nn.silu, bias=b_shard)
    ada_all = _gather_blocks([ada_shard], name="gather_ada", in_vmem=True)[0]
    ada_rows = lax.dynamic_index_in_dim(ada_all, dev, axis=1, keepdims=False)
    ada = ada_rows[0::2].reshape(1, 4 * n_ada)
    SH1, SC1, GT1, SH2, SC2, GT2 = range(6)

    def after(dep, a):
        return a + (dep.reshape(-1)[0:1].reshape((1,) * a.ndim) * 0).astype(a.dtype)

    def fill_own(g8, own):
        return lax.dynamic_update_index_in_dim(g8, own, dev, 0)

    halves = []
    for nm in names:
        w = mats[nm]
        hr = w.shape[0] // 2
        halves.append(lax.dynamic_slice_in_dim(w, ci * hr, hr, axis=0).astype(BF16))
    half_of = dict(zip(names, halves))
    early = ["w_in", "w_uq", "w_ukv"]
    late = ["w_proj_mla", "w_proj_sb", "w_out", "w_ffn_in", "w_ffn_out"]
    early_halves = [half_of[nm] for nm in early]
    early_halves[0] = after(ada, early_halves[0])
    early_got = _gather_blocks(early_halves, name="gather_weights", in_vmem=False)
    gathered = {nm: fill_own(g8, own) for nm, g8, own in zip(early, early_got, early_halves)}
    late_halves = [half_of[nm] for nm in late]
    late_halves[0] = after(gathered[early[1]], late_halves[0])
    late_send, late_recv, late_srcs, late_lands, late_token = _split_start(
        late_halves, [jax.ShapeDtypeStruct((N_DEV,) + h.shape, h.dtype) for h in late_halves], _gather_plan, 4,
        name="gather_late_start")
    ada = ada + late_token[0:1, 0:1]

    def full_cols(nm):
        return _cols_from_chips(gathered[nm], mats[nm].shape[0])

    kpe0 = 2 * ql
    w_in_p = _w_in_layout(gathered["w_in"], kpe0)
    w_uq_p = jnp.pad(full_cols("w_uq").reshape(ql, nh, QK_DIM), ((0, 0), (0, 0), (0, HEAD_PAD - QK_DIM))
                     ).reshape(ql, nh * HEAD_PAD)
    w_ukv4 = full_cols("w_ukv").reshape(ql, nh, 2 * HEAD)
    w_ukv_p = jnp.concatenate([w_ukv4[:, :, :HEAD].reshape(ql, mlaw), w_ukv4[:, :, HEAD:].reshape(ql, mlaw)], axis=1)

    half = ROPE // 2
    freqs = ROPE_THETA ** (-jnp.arange(half, dtype=F32) / half)
    ang = positions[0].astype(F32)[:, None] * freqs
    cos, sin = jnp.cos(ang), jnp.sin(ang)
    one = jnp.ones((S, NOPE), F32)
    zero = jnp.zeros((S, NOPE), F32)
    zh = jnp.zeros((S, half), F32)
    tabs = (jnp.concatenate([one, cos, cos, one[:, :HEAD_PAD - QK_DIM]], axis=1),
            jnp.concatenate([zero, zh, sin, zero[:, :HEAD_PAD - QK_DIM]], axis=1),
            jnp.concatenate([zero, -sin, zh, zero[:, :HEAD_PAD - QK_DIM]], axis=1))
    g_qh_p = _pad_cols(g_q_head, HEAD_PAD)
    g_kh_p = _pad_cols(g_k_head, HEAD_PAD)

    h1 = _rmsmod(x, g_norm1, ada, SC1, SH1, name="rmsmod1")
    proj = _mm(h1, w_in_p, name="mm_proj", tn=640, out_dtype=BF16)
    cqn, ckvn = _latent_norm(proj, g_q_latent, g_kv_latent, ql)
    q0 = _mm(cqn, w_uq_p, name="mm_q_up")
    kv0 = _mm(ckvn, w_ukv_p, name="mm_kv_up")
    q = _q_prep(q0, g_qh_p, tabs, nh)
    k = _k_prep(kv0, proj, kpe_col, g_kh_p, tabs, nh)
    y_a, lse = _mla_fwd(q, k, kv0, nh)
    y_b, sb_runs = _sb_fwd(proj, q_col, k_col, v_col, nh)
    late_srcs, late_lands = _split_wait(late_send, late_recv, late_srcs, late_lands, y_b, _gather_plan,
                                        name="gather_late_wait")
    late_got = _gather_forward(late_lands, name="gather_late_forward")
    gathered.update({nm: fill_own(g8, own) for nm, g8, own in zip(late, late_got, late_srcs)})
    w_pm = full_cols("w_proj_mla")
    w_ps = full_cols("w_proj_sb")
    w_o = gathered["w_out"].reshape(D, D)
    ib = 256 if (dff // 2) % 256 == 0 else LANE
    nb = dff // ib
    w_fi = _interleave_layout(gathered["w_ffn_in"], ib)
    w_fo = gathered["w_ffn_out"].reshape(dff, D)
    pa = _mm(y_a, w_pm, name="mm_proj_mla", out_dtype=BF16)
    pb = _mm(y_b, w_ps, name="mm_proj_sb", out_dtype=BF16)
    merged = _gate_merge(pa, pb, proj, gla_col, glb_col)
    o = _mm(merged, w_o, name="mm_out")
    x2, h2 = _resid_rmsmod(x, o, g_norm2, ada, GT1, SC2, SH2)
    ff, act = _mm(h2, w_fi, name="mm_ffn_in", tn=4 * ib,
                  fused=(_swiglu_tile(ib), [], [(2 * dff, 4 * ib, BF16), (dff, 2 * ib, BF16)]))
    f = _mm(act, w_fo, name="mm_ffn_out")
    dy, df, red_l, loss_p = _loss_head(x2, f, tgt, ada, GT2)

    dff_, = _mm(df, w_fo, name="mm_d_act", tb=True, tn=2 * ib,
                fused=(_swiglu_bwd_tile(ib), [(ff, 4 * ib)], [(2 * dff, 4 * ib, BF16)]))
    def pc(kind):
        if kind == "cols":
            return kind
        return kind if (D // 4) % LANE == 0 and (dff // 4) % LANE == 0 else None

    gw_fo = _mm(act, df, name="mm_gw_ffn_out", ta=True, out_dtype=BF16, pieces=pc("rows"))
    dh2 = _mm(dff_, w_fi, name="mm_d_h2", tb=True)
    gw_fi = _mm(h2, dff_, name="mm_gw_ffn_in", ta=True, out_dtype=BF16, pieces="cols", tn=ib,
                col_perm=lambda jj: jj // 2 + nb * (jj % 2))

    def to_pieces(nms, grads):
        return [g if g.ndim == 4 else (_rows_to_pieces if nm in row_sharded else _cols_to_pieces)(g)
                for nm, g in zip(nms, grads)]

    def pair_sums(nms, pcs, got):
        return [_pair_sum(p, r, c_idx, name="rs_pair_sum_" + nm) for p, r, nm in zip(pcs, got, nms)]

    def swap_start(pcs, tag):
        return _split_start(pcs, [jax.ShapeDtypeStruct(p.shape[1:], p.dtype) for p in pcs], _swap_plan, 1,
                            name="rs_swap_%s_start" % tag)

    def exchange_start(pair, tag):
        return _split_start(pair, [jax.ShapeDtypeStruct((3,) + p.shape[1:], p.dtype) for p in pair], _exchange_plan, 3,
                            name="rs_exchange_%s_start" % tag)

    ffn = ["w_ffn_in", "w_ffn_out"]
    sw = swap_start(to_pieces(ffn, [gw_fi, gw_fo]), "ffn")
    ada = ada + sw[4][0:1, 0:1]
    dx2, do, red_2 = _rmsmod2_bwd(dh2, x2, dy, o, g_norm2, ada, SC2, GT1)
    ffn_pcs, ffn_got = _split_wait(sw[0], sw[1], sw[2], sw[3], dx2, _swap_plan, name="rs_swap_ffn_wait")
    ffn_send, ffn_recv, ffn_pair, ffn_lands, ffn_token = exchange_start(pair_sums(ffn, ffn_pcs, ffn_got), "ffn")
    dmerged = _mm(do, w_o, name="mm_d_merged", tb=True, out_dtype=BF16,
                  bias=jnp.zeros((1, D), F32) + ffn_token[0:1, 0:1])
    gw_o = _mm(merged, do, name="mm_gw_out", ta=True, out_dtype=BF16, pieces=pc("rows"))
    dpa, dpb, dgla, dglb = _gate_bwd(dmerged, pa, pb, proj, gla_col, glb_col)
    dya = _mm(dpa, w_pm, name="mm_d_ya", tb=True, out_dtype=BF16)
    gw_pm = _mm(y_a, dpa, name="mm_gw_proj_mla", ta=True, out_dtype=BF16, pieces=pc("cols"))
    dyb = _mm(dpb, w_ps, name="mm_d_yb", tb=True, out_dtype=BF16)
    gw_ps = _mm(y_b, dpb, name="mm_gw_proj_sb", ta=True, out_dtype=BF16, pieces=pc("cols"))
    mid = ["w_proj_mla", "w_proj_sb", "w_out"]
    mid_pcs = to_pieces(mid, [gw_pm, gw_ps, gw_o])
    mid_pair = pair_sums(mid, mid_pcs, _sibling_swap(mid_pcs, name="rs_sibling_swap_mid"))
    mid_send, mid_recv, mid_pair, mid_lands, mid_token = exchange_start(mid_pair, "mid")
    lse = lse + mid_token[0:1, 0:1]
    dq, dk, dv = _mla_bwd(q, k, kv0, y_a, dya, lse, nh)
    dq_sb, dk_sb, dv_sb = _sb_bwd(proj, q_col, k_col, v_col, dyb, sb_runs, nh)
    dq0, red_qh = _q_prep_bwd(dq, q0, g_qh_p, tabs, nh)
    dkv0, dkpe, red_kh = _k_prep_bwd(dk, dv, kv0, proj, kpe_col, g_kh_p, tabs, nh)
    dcqn = _mm(dq0, w_uq_p, name="mm_d_cqn", tb=True, out_dtype=BF16)
    gw_uq_p = _mm(cqn, dq0, name="mm_gw_uq", ta=True, out_dtype=BF16)
    dckvn = _mm(dkv0, w_ukv_p, name="mm_d_ckvn", tb=True, out_dtype=BF16)
    gw_ukv_p = _mm(ckvn, dkv0, name="mm_gw_ukv", ta=True, out_dtype=BF16)
    dcq, dckv, red_lat = _latent_norm_bwd(dcqn, dckvn, proj, g_q_latent, g_kv_latent, ql)
    dproj = jnp.concatenate([dcq, dckv, dq_sb.astype(BF16), dk_sb.astype(BF16), dv_sb.astype(BF16),
                             dgla, dglb, dkpe], axis=1)
    gw_in_p = _mm(h1, dproj, name="mm_gw_in", ta=True, out_dtype=BF16, tn=640)

    gw_in = _w_in_grad_pieces(gw_in_p, kpe0)
    gw_uq = gw_uq_p.reshape(ql, nh, HEAD_PAD)[:, :, :QK_DIM].reshape(ql, nh * QK_DIM)
    gw_ukv = jnp.concatenate([gw_ukv_p[:, :mlaw].reshape(ql, nh, HEAD), gw_ukv_p[:, mlaw:].reshape(ql, nh, HEAD)],
                             axis=2).reshape(ql, 2 * mlaw)
    last = ["w_in", "w_uq", "w_ukv"]
    assert last + mid + ffn == names

    last_pcs = to_pieces(last, [gw_in, gw_uq, gw_ukv])
    last_pair = pair_sums(last, last_pcs, _sibling_swap(last_pcs, name="rs_sibling_swap_last"))
    last_send, last_recv, last_pair, last_lands, last_token = exchange_start(last_pair, "last")
    ada = ada + last_token[0:1, 0:1]
    dh1 = _mm(dproj, w_in_p, name="mm_d_h1", tb=True, bias=jnp.zeros((1, D), F32) + last_token[0:1, 0:1])
    grad_x, red_1 = _rmsmod1_bwd(dh1, x, dx2, g_norm1, ada, SC1)
    last_pair, last_chips = _split_wait(last_send, last_recv, last_pair, last_lands, grad_x, _exchange_plan,
                                        name="rs_exchange_last_wait")
    mid_pair, mid_chips = _split_wait(mid_send, mid_recv, mid_pair, mid_lands, grad_x, _exchange_plan,
                                      name="rs_exchange_mid_wait")
    ffn_pair, ffn_chips = _split_wait(ffn_send, ffn_recv, ffn_pair, ffn_lands, grad_x, _exchange_plan,
                                      name="rs_exchange_ffn_wait")
    reduced = [_chip_sum(s, r, chip_idx, name="rs_chip_sum_" + nm)
               for s, r, nm in zip(last_pair + mid_pair + ffn_pair, last_chips + mid_chips + ffn_chips, names)]
    from_sibling2 = _sibling_swap(reduced, name="rs_sibling_send", whole=True)

    vec_names = ["b_ada", "g_norm1", "g_norm2", "g_q_latent", "g_kv_latent", "g_q_head", "g_k_head"]
    vec_w = dict(b_ada=b_ada, g_norm1=g_norm1, g_norm2=g_norm2, g_q_latent=g_q_latent, g_kv_latent=g_kv_latent,
                 g_q_head=g_q_head, g_k_head=g_k_head)
    vec_m = dict(b_ada=m_b_ada, g_norm1=m_g_norm1, g_norm2=m_g_norm2, g_q_latent=m_g_q_latent,
                 g_kv_latent=m_g_kv_latent, g_q_head=m_g_q_head, g_k_head=m_g_k_head)
    vec_v = dict(b_ada=v_b_ada, g_norm1=v_g_norm1, g_norm2=v_g_norm2, g_q_latent=v_g_q_latent,
                 g_kv_latent=v_g_kv_latent, g_q_head=v_g_q_head, g_k_head=v_g_k_head)
    d_ada = jnp.concatenate([red_1[0:1], red_1[1:2], red_2[3:4], red_2[0:1], red_2[1:2], red_l[0:1]], axis=1)
    vec_parts = dict(b_ada=d_ada, g_norm1=red_1[2:3], g_norm2=red_2[2:3], g_q_latent=red_lat[0:1],
                     g_kv_latent=red_lat[1:2], g_q_head=red_qh[0:1], g_k_head=red_kh[0:1])
    widths = [-(-vec_w[nm].shape[1] // LANE) * LANE for nm in vec_names]
    offs = [sum(widths[:i]) for i in range(len(widths))]
    pack = lambda d: jnp.concatenate([_pad_cols(d[nm][:, :vec_w[nm].shape[1]], wd) for nm, wd in zip(vec_names, widths)], axis=1)
    nvec = sum(widths) + LANE
    no_loss = jnp.zeros((1, LANE), F32)
    parts = jnp.concatenate([pack(vec_parts), loss_p[0:1, :]], axis=1)
    parts_all = _gather_blocks([jnp.broadcast_to(parts, (8, nvec))], name="gather_vec_grads",
                               in_vmem=True)[0][:, 0, :]
    gvec, dvec, nmvec, nvvec = _adamw_vec(parts_all, *[jnp.concatenate([pack(d), no_loss], axis=1)
                                                       for d in (vec_w, vec_m, vec_v)])
    loss = gvec[0, nvec - LANE]
    unpack = lambda a: {nm: a[:, o_:o_ + vec_w[nm].shape[1]] for nm, o_ in zip(vec_names, offs)}
    gvec, dvec, nmvec, nvvec = unpack(gvec), unpack(dvec), unpack(nmvec), unpack(nvvec)

    dada_all = lax.dynamic_slice_in_dim(parts_all[:, :6 * D], chip * n_ada, n_ada, axis=1)
    cact_t = jax.nn.silu(c_all).T
    g_ada, d_ada_w, nm_ada, nv_ada = _adamw_ada(cact_t, dada_all, w_ada[0], m_w_ada[0], v_w_ada[0])

    ms = dict(w_in=m_w_in, w_uq=m_w_uq, w_ukv=m_w_ukv, w_proj_mla=m_w_proj_mla, w_proj_sb=m_w_proj_sb,
              w_out=m_w_out, w_ffn_in=m_w_ffn_in, w_ffn_out=m_w_ffn_out)
    vs = dict(w_in=v_w_in, w_uq=v_w_uq, w_ukv=v_w_ukv, w_proj_mla=v_w_proj_mla, w_proj_sb=v_w_proj_sb,
              w_out=v_w_out, w_ffn_in=v_w_ffn_in, w_ffn_out=v_w_ffn_out)
    G, DL, NM, NV = {}, {}, {}, {}
    for nm, mine, other in zip(names, reduced, from_sibling2):
        g_, d_, m_, v_ = _adamw(mats[nm], mine, other, c_idx, ms[nm][0], vs[nm][0], name="adamw_" + nm)
        G[nm], DL[nm], NM[nm], NV[nm] = g_[None], d_[None], m_[None], v_[None]
    G["w_ada"], DL["w_ada"], NM["w_ada"], NV["w_ada"] = g_ada[None], d_ada_w[None], nm_ada[None], nv_ada[None]
    for nm in vec_names:
        G[nm], DL[nm], NM[nm], NV[nm] = gvec[nm], dvec[nm], nmvec[nm], nvvec[nm]

    order = ["w_ada", "b_ada", "g_norm1", "g_norm2", "w_in", "g_q_latent", "g_kv_latent", "w_uq", "w_ukv",
             "g_q_head", "g_k_head", "w_proj_mla", "w_proj_sb", "w_out", "w_ffn_in", "w_ffn_out"]
    return (loss, grad_x[None], *[G[n] for n in order], *[DL[n] for n in order],
            *[NM[n] for n in order], *[NV[n] for n in order])
```

```python
import functools
import math

import jax
import jax.numpy as jnp
from jax import lax
from jax.experimental import pallas as pl
from jax.experimental.pallas import tpu as pltpu

F32 = jnp.float32
BF16 = jnp.bfloat16
MESH = pl.DeviceIdType.MESH

EPS = 1e-6
ROPE_THETA = 10000.0
NOPE = 128
ROPE = 64
QK_DIM = NOPE + ROPE
HEAD_PAD = 256
HEAD = 128
N_DEV = 8
LANE = 128
VMEM_LIMIT = 48 * 1024 * 1024

ADAM_LR = 0.001
ADAM_B1 = 0.9
ADAM_B2 = 0.999
ADAM_EPS = 1e-08
ADAM_WD = 0.01
ADAM_STEP = 10


def _tile(n, target):
    if n <= target:
        return n
    t = (target // LANE) * LANE
    while t >= LANE:
        if n % t == 0:
            return t
        t -= LANE
    return n


def _row_tile(rows, row_bytes, budget=24 * 1024 * 1024):
    cap = max(8, budget // (2 * row_bytes))
    best = None
    for t in range(8, min(rows, cap) + 1, 8):
        if rows % t == 0:
            best = t
    return best if best is not None else rows


def _params(sem):
    return pltpu.CompilerParams(dimension_semantics=sem, vmem_limit_bytes=VMEM_LIMIT)


def _rows(tm, w, col=0):
    return pl.BlockSpec((tm, w), lambda i: (i, col))


def _vec(w, col=0, rows=1):
    return pl.BlockSpec((rows, w), lambda i: (0, col))


MM_VMEM_BUDGET = 36 * 1024 * 1024


def _mm(a, b, *, name, ta=False, tb=False, out_dtype=F32, a_fn=None, bias=None, tm=1024, tn=1024, pieces=None,
        col_perm=None, fused=None):
    M = a.shape[1] if ta else a.shape[0]
    K = a.shape[0] if ta else a.shape[1]
    N = b.shape[0] if tb else b.shape[1]
    assert K == (b.shape[1] if tb else b.shape[0]), (a.shape, b.shape, ta, tb)
    if pieces == "cols":
        tm, tn = _tile(M // 2, tm), _tile(N // 4, tn)
        assert (M // 2) % tm == 0 and (N // 4) % tn == 0
    elif pieces == "rows":
        tm, tn = M // 4, _tile(N, tn)
    else:
        tm, tn = _tile(M, tm), _tile(N, tn)
    sa, sb, so = a.dtype.itemsize, b.dtype.itemsize, jnp.dtype(out_dtype).itemsize

    def fits(tk):
        return 2 * tk * (tm * sa + tn * sb) + tm * tn * (2 * so + 4) <= MM_VMEM_BUDGET

    tk = K
    while not fits(tk):
        smaller = _tile(K, tk - LANE)
        if smaller >= tk:
            break
        tk = smaller
    nk = K // tk
    dn = (((0 if ta else 1,), (1 if tb else 0,)), ((), ()))
    b_outer = nk == 1 and a.size * sa * (N // tn) < b.size * sb * (M // tm)

    n_extra = len(fused[1]) if fused else 0
    n_out = len(fused[2]) if fused else 1

    def body(*refs):
        a_ref, b_ref = refs[:2]
        bias_ref = refs[2] if bias is not None else None
        first = 3 if bias is not None else 2
        extra_refs = refs[first:first + n_extra]
        out_refs = refs[first + n_extra:first + n_extra + n_out]
        o_ref = out_refs[0]
        av = a_ref[...]
        if a_fn is not None:
            av = a_fn(av.astype(F32))
        part = lax.dot_general(av.astype(BF16), b_ref[...].astype(BF16), dn, preferred_element_type=F32)

        def finish(r):
            if bias is not None:
                r = r + bias_ref[...]
            if fused:
                for ref, tile in zip(out_refs, fused[0](r, *[e[...] for e in extra_refs])):
                    ref[...] = tile.astype(ref.dtype)
            elif pieces == "rows":
                o_ref[0] = r[:tm // 2].astype(o_ref.dtype)
                o_ref[1] = r[tm // 2:].astype(o_ref.dtype)
            else:
                o_ref[...] = r.astype(o_ref.dtype)

        if nk == 1:
            finish(part)
        else:
            acc_ref = refs[-1]
            k = pl.program_id(2)

            @pl.when(k == 0)
            def _():
                acc_ref[...] = part

            @pl.when(k > 0)
            def _():
                acc_ref[...] += part

            @pl.when(k == nk - 1)
            def _():
                finish(acc_ref[...])

    def ij(g0, g1):
        return (g1, g0) if b_outer else (g0, g1)

    def amap(g0, g1, k):
        i, _ = ij(g0, g1)
        return (k, i) if ta else (i, k)

    def bmap(g0, g1, k):
        _, j = ij(g0, g1)
        return (j, k) if tb else (k, j)

    in_specs = [pl.BlockSpec((tk, tm) if ta else (tm, tk), amap), pl.BlockSpec((tn, tk) if tb else (tk, tn), bmap)]
    args = [a, b]
    if bias is not None:
        in_specs.append(pl.BlockSpec((1, tn), lambda g0, g1, k: (0, ij(g0, g1)[1])))
        args.append(bias)
    grid = (N // tn, M // tm, nk) if b_outer else (M // tm, N // tn, nk)
    if pieces == "cols":
        ni, nj = M // 2 // tm, N // 4 // tn

        def omap(g0, g1, k):
            i, j = ij(g0, g1)
            j = col_perm(j) if col_perm else j
            return (i // ni, j // nj, i % ni, j % nj)

        out_spec = pl.BlockSpec((None, None, tm, tn), omap)
        out_shape = jax.ShapeDtypeStruct((2, 4, M // 2, N // 4), out_dtype)
    elif pieces == "rows":
        out_spec = pl.BlockSpec((2, None, tm // 2, tn), lambda g0, g1, k: (0, ij(g0, g1)[0], 0, ij(g0, g1)[1]))
        out_shape = jax.ShapeDtypeStruct((2, 4, tm // 2, N), out_dtype)
    else:
        out_spec = pl.BlockSpec((tm, tn), lambda g0, g1, k: ij(g0, g1))
        out_shape = jax.ShapeDtypeStruct((M, N), out_dtype)
    if fused:
        for arr, width in fused[1]:
            in_specs.append(pl.BlockSpec((tm, width), lambda g0, g1, k: ij(g0, g1)))
            args.append(arr)
        out_spec = [pl.BlockSpec((tm, width), lambda g0, g1, k: ij(g0, g1)) for _, width, _ in fused[2]]
        out_shape = [jax.ShapeDtypeStruct((M, cols), dt) for cols, _, dt in fused[2]]
    return pl.pallas_call(
        body, name=name, grid=grid, in_specs=in_specs, out_specs=out_spec, out_shape=out_shape,
        scratch_shapes=[pltpu.VMEM((tm, tn), F32)] if nk > 1 else [],
        compiler_params=_params(("parallel", "parallel", "arbitrary")),
    )(*args)


def _rms_rows(v):
    return lax.rsqrt(jnp.mean(v * v, axis=-1, keepdims=True) + EPS)


def _rmsmod(x, g, ada, sc_col, sh_col, *, name):
    S, D = x.shape
    tm = _tile(S, 256)

    def body(x_ref, g_ref, sc_ref, sh_ref, h_ref):
        xv = x_ref[...]
        h = (xv * _rms_rows(xv) * g_ref[...]) * (1.0 + sc_ref[...]) + sh_ref[...]
        h_ref[...] = h.astype(h_ref.dtype)

    return pl.pallas_call(
        body, name=name, grid=(S // tm,),
        in_specs=[_rows(tm, D), _vec(D), _vec(D, sc_col), _vec(D, sh_col)],
        out_specs=_rows(tm, D), out_shape=jax.ShapeDtypeStruct((S, D), BF16),
        compiler_params=_params(("parallel",)),
    )(x, g, ada, ada)


def _latent_norm(proj, g_q, g_kv, ql):
    S = proj.shape[0]
    tm = _tile(S, 512)

    def body(cq_ref, ckv_ref, gq_ref, gkv_ref, oq_ref, okv_ref):
        cq = cq_ref[...].astype(F32)
        oq_ref[...] = (cq * _rms_rows(cq) * gq_ref[...]).astype(BF16)
        ckv = ckv_ref[...].astype(F32)
        okv_ref[...] = (ckv * _rms_rows(ckv) * gkv_ref[...]).astype(BF16)

    return pl.pallas_call(
        body, name="latent_norm", grid=(S // tm,),
        in_specs=[_rows(tm, ql, 0), _rows(tm, ql, 1), _vec(ql), _vec(ql)],
        out_specs=[_rows(tm, ql), _rows(tm, ql)],
        out_shape=[jax.ShapeDtypeStruct((S, ql), BF16)] * 2,
        compiler_params=_params(("parallel",)),
    )(proj, proj, g_q, g_kv)


def _rope_fwd(y, c, s1, s2):
    return y * c + pltpu.roll(y, ROPE // 2, 1) * s1 + pltpu.roll(y, HEAD_PAD - ROPE // 2, 1) * s2


def _rope_bwd(d, c, s1, s2):
    return d * c + pltpu.roll(d * s1, HEAD_PAD - ROPE // 2, 1) + pltpu.roll(d * s2, ROPE // 2, 1)


def _head_rms(v):
    return lax.rsqrt(jnp.sum(v * v, axis=-1, keepdims=True) * (1.0 / QK_DIM) + EPS)


def _q_prep(q0, g_qh, tabs, nh):
    S = q0.shape[0]
    tm = _tile(S, 256)

    def body(q_ref, g_ref, c_ref, s1_ref, s2_ref, o_ref):
        c, s1, s2, g = c_ref[...], s1_ref[...], s2_ref[...], g_ref[...]
        for h in range(nh):
            sl = slice(h * HEAD_PAD, (h + 1) * HEAD_PAD)
            xs = q_ref[:, sl]
            o_ref[:, sl] = (_rope_fwd(xs * _head_rms(xs) * g, c, s1, s2) * (QK_DIM ** -0.5)).astype(BF16)

    w = nh * HEAD_PAD
    return pl.pallas_call(
        body, name="mla_q_prep", grid=(S // tm,),
        in_specs=[_rows(tm, w), _vec(HEAD_PAD)] + [_rows(tm, HEAD_PAD)] * 3,
        out_specs=_rows(tm, w), out_shape=jax.ShapeDtypeStruct((S, w), BF16),
        compiler_params=_params(("parallel",)),
    )(q0, g_qh, *tabs)


def _k_prep(kv0, proj, kpe_col, g_kh, tabs, nh):
    S = kv0.shape[0]
    tm = _tile(S, 256)

    def body(kv_ref, kpe_ref, g_ref, c_ref, s1_ref, s2_ref, o_ref):
        c, s1, s2, g = c_ref[...], s1_ref[...], s2_ref[...], g_ref[...]
        kpe = kpe_ref[...].astype(F32)
        for h in range(nh):
            k0 = jnp.concatenate([kv_ref[:, h * HEAD:(h + 1) * HEAD], kpe], axis=1)
            o_ref[:, h * HEAD_PAD:(h + 1) * HEAD_PAD] = _rope_fwd(k0 * _head_rms(k0) * g, c, s1, s2).astype(BF16)

    return pl.pallas_call(
        body, name="mla_k_prep", grid=(S // tm,),
        in_specs=[_rows(tm, nh * HEAD, 0), _rows(tm, LANE, kpe_col), _vec(HEAD_PAD)] + [_rows(tm, HEAD_PAD)] * 3,
        out_specs=_rows(tm, nh * HEAD_PAD), out_shape=jax.ShapeDtypeStruct((S, nh * HEAD_PAD), BF16),
        compiler_params=_params(("parallel",)),
    )(kv0, proj, g_kh, *tabs)


def _gate_merge(pa, pb, proj, gla_col, glb_col):
    S, D = pa.shape
    tm = _tile(S, 256)

    def body(pa_ref, pb_ref, ga_ref, gb_ref, o_ref):
        o_ref[...] = (jax.nn.sigmoid(ga_ref[...].astype(F32)) * pa_ref[...] + jax.nn.sigmoid(gb_ref[...].astype(F32)) * pb_ref[...]).astype(BF16)

    return pl.pallas_call(
        body, name="gate_merge", grid=(S // tm,),
        in_specs=[_rows(tm, D), _rows(tm, D), _rows(tm, D, gla_col), _rows(tm, D, glb_col)],
        out_specs=_rows(tm, D), out_shape=jax.ShapeDtypeStruct((S, D), BF16),
        compiler_params=_params(("parallel",)),
    )(pa, pb, proj, proj)


def _resid_rmsmod(x, o, g, ada, gt_col, sc_col, sh_col):
    S, D = x.shape
    tm = _tile(S, 256)

    def body(x_ref, o_ref, g_ref, gt_ref, sc_ref, sh_ref, x2_ref, h_ref):
        x2 = x_ref[...] + gt_ref[...] * o_ref[...]
        x2_ref[...] = x2
        h_ref[...] = ((x2 * _rms_rows(x2) * g_ref[...]) * (1.0 + sc_ref[...]) + sh_ref[...]).astype(BF16)

    return pl.pallas_call(
        body, name="resid_rmsmod2", grid=(S // tm,),
        in_specs=[_rows(tm, D), _rows(tm, D), _vec(D), _vec(D, gt_col), _vec(D, sc_col), _vec(D, sh_col)],
        out_specs=[_rows(tm, D), _rows(tm, D)],
        out_shape=[jax.ShapeDtypeStruct((S, D), F32), jax.ShapeDtypeStruct((S, D), BF16)],
        compiler_params=_params(("parallel",)),
    )(x, o, g, ada, ada, ada)


def _swiglu_tile(ib):
    def fn(r):
        pairs = r.shape[1] // (2 * ib)
        act = [jax.nn.silu(r[:, 2 * p * ib:(2 * p + 1) * ib]) * r[:, (2 * p + 1) * ib:(2 * p + 2) * ib] for p in range(pairs)]
        return r, jnp.concatenate(act, axis=1) if pairs > 1 else act[0]
    return fn


def _swiglu_bwd_tile(ib):
    def fn(d, ff):
        ff = ff.astype(F32)
        out = []
        for p in range(d.shape[1] // ib):
            dp = d[:, p * ib:(p + 1) * ib]
            g = ff[:, 2 * p * ib:(2 * p + 1) * ib]
            u = ff[:, (2 * p + 1) * ib:(2 * p + 2) * ib]
            sg = jax.nn.sigmoid(g)
            out += [dp * u * sg * (1.0 + g * (1.0 - sg)), dp * g * sg]
        return (jnp.concatenate(out, axis=1),)
    return fn


def _loss_head(x2, f, tgt, ada, gt_col):
    S, D = x2.shape
    tm = _tile(S, 256)

    def body(x2_ref, f_ref, t_ref, gt_ref, dy_ref, df_ref, red_ref, loss_ref):
        @pl.when(pl.program_id(0) == 0)
        def _():
            red_ref[...] = jnp.zeros_like(red_ref)
            loss_ref[...] = jnp.zeros_like(loss_ref)

        fv = f_ref[...]
        gt = gt_ref[...]
        err = x2_ref[...] + gt * fv - t_ref[...]
        dy = err * (1.0 / D)
        dy_ref[...] = dy
        df_ref[...] = (dy * gt).astype(BF16)
        red_ref[0:1, :] += jnp.sum(dy * fv, axis=0, keepdims=True)
        loss_ref[...] += (0.5 / D) * jnp.sum(err * err)

    return pl.pallas_call(
        body, name="loss_head", grid=(S // tm,),
        in_specs=[_rows(tm, D), _rows(tm, D), _rows(tm, D), _vec(D, gt_col)],
        out_specs=[_rows(tm, D), _rows(tm, D), _vec(D, rows=8), _vec(LANE, rows=8)],
        out_shape=[jax.ShapeDtypeStruct((S, D), F32), jax.ShapeDtypeStruct((S, D), BF16),
                   jax.ShapeDtypeStruct((8, D), F32), jax.ShapeDtypeStruct((8, LANE), F32)],
        compiler_params=_params(("arbitrary",)),
    )(x2, f, tgt, ada)


def _rmsmod2_bwd(dh2, x2, dy, o, g, ada, sc_col, gt_col):
    S, D = x2.shape
    tm = _tile(S, 256)

    def body(dh_ref, x2_ref, dy_ref, o_ref, g_ref, sc_ref, gt_ref, dx_ref, do_ref, red_ref):
        @pl.when(pl.program_id(0) == 0)
        def _():
            red_ref[...] = jnp.zeros_like(red_ref)

        dh = dh_ref[...]
        x2 = x2_ref[...]
        gv = g_ref[...]
        mod = 1.0 + sc_ref[...]
        r = _rms_rows(x2)
        xn = x2 * r
        t = dh * xn
        red_ref[0:1, :] += jnp.sum(dh, axis=0, keepdims=True)
        red_ref[1:2, :] += jnp.sum(t * gv, axis=0, keepdims=True)
        red_ref[2:3, :] += jnp.sum(t * mod, axis=0, keepdims=True)
        dxn = dh * gv * mod
        dx = dy_ref[...] + r * (dxn - xn * jnp.mean(dxn * xn, axis=-1, keepdims=True))
        dx_ref[...] = dx
        red_ref[3:4, :] += jnp.sum(dx * o_ref[...], axis=0, keepdims=True)
        do_ref[...] = (dx * gt_ref[...]).astype(BF16)

    return pl.pallas_call(
        body, name="rmsmod2_bwd", grid=(S // tm,),
        in_specs=[_rows(tm, D)] * 4 + [_vec(D), _vec(D, sc_col), _vec(D, gt_col)],
        out_specs=[_rows(tm, D), _rows(tm, D), _vec(D, rows=8)],
        out_shape=[jax.ShapeDtypeStruct((S, D), F32), jax.ShapeDtypeStruct((S, D), BF16),
                   jax.ShapeDtypeStruct((8, D), F32)],
        compiler_params=_params(("arbitrary",)),
    )(dh2, x2, dy, o, g, ada, ada)


def _rmsmod1_bwd(dh, x, dx2, g, ada, sc_col):
    S, D = x.shape
    tm = _tile(S, 256)

    def body(dh_ref, x_ref, dx2_ref, g_ref, sc_ref, gx_ref, red_ref):
        @pl.when(pl.program_id(0) == 0)
        def _():
            red_ref[...] = jnp.zeros_like(red_ref)

        dh = dh_ref[...]
        xv = x_ref[...]
        gv = g_ref[...]
        mod = 1.0 + sc_ref[...]
        r = _rms_rows(xv)
        xn = xv * r
        t = dh * xn
        red_ref[0:1, :] += jnp.sum(dh, axis=0, keepdims=True)
        red_ref[1:2, :] += jnp.sum(t * gv, axis=0, keepdims=True)
        red_ref[2:3, :] += jnp.sum(t * mod, axis=0, keepdims=True)
        dxn = dh * gv * mod
        gx_ref[...] = dx2_ref[...] + r * (dxn - xn * jnp.mean(dxn * xn, axis=-1, keepdims=True))

    return pl.pallas_call(
        body, name="rmsmod1_bwd", grid=(S // tm,),
        in_specs=[_rows(tm, D)] * 3 + [_vec(D), _vec(D, sc_col)],
        out_specs=[_rows(tm, D), _vec(D, rows=8)],
        out_shape=[jax.ShapeDtypeStruct((S, D), F32), jax.ShapeDtypeStruct((8, D), F32)],
        compiler_params=_params(("arbitrary",)),
    )(dh, x, dx2, g, ada)


def _gate_bwd(dm, pa, pb, proj, gla_col, glb_col):
    S, D = pa.shape
    tm = _tile(S, 256)

    def body(dm_ref, pa_ref, pb_ref, la_ref, lb_ref, dpa_ref, dpb_ref, dla_ref, dlb_ref):
        dm_ = dm_ref[...]
        ga = jax.nn.sigmoid(la_ref[...].astype(F32))
        gb = jax.nn.sigmoid(lb_ref[...].astype(F32))
        dpa_ref[...] = (dm_ * ga).astype(BF16)
        dpb_ref[...] = (dm_ * gb).astype(BF16)
        dla_ref[...] = (dm_ * pa_ref[...] * ga * (1.0 - ga)).astype(BF16)
        dlb_ref[...] = (dm_ * pb_ref[...] * gb * (1.0 - gb)).astype(BF16)

    return pl.pallas_call(
        body, name="gate_bwd", grid=(S // tm,),
        in_specs=[_rows(tm, D)] * 3 + [_rows(tm, D, gla_col), _rows(tm, D, glb_col)],
        out_specs=[_rows(tm, D)] * 4, out_shape=[jax.ShapeDtypeStruct((S, D), BF16)] * 4,
        compiler_params=_params(("parallel",)),
    )(dm, pa, pb, proj, proj)


def _q_prep_bwd(dq, q0, g_qh, tabs, nh):
    S = q0.shape[0]
    tm = _tile(S, 256)

    def body(dq_ref, q_ref, g_ref, c_ref, s1_ref, s2_ref, o_ref, red_ref):
        @pl.when(pl.program_id(0) == 0)
        def _():
            red_ref[...] = jnp.zeros_like(red_ref)

        c, s1, s2, g = c_ref[...], s1_ref[...], s2_ref[...], g_ref[...]
        dg = jnp.zeros((1, HEAD_PAD), F32)
        for h in range(nh):
            sl = slice(h * HEAD_PAD, (h + 1) * HEAD_PAD)
            d1 = _rope_bwd(dq_ref[:, sl], c, s1, s2)
            xs = q_ref[:, sl]
            r = _head_rms(xs)
            qn = xs * r
            dg = dg + jnp.sum(d1 * qn, axis=0, keepdims=True)
            dn = d1 * g
            o_ref[:, sl] = (r * (dn - qn * (jnp.sum(dn * qn, axis=-1, keepdims=True) * (1.0 / QK_DIM)))).astype(BF16)
        red_ref[0:1, :] += dg

    w = nh * HEAD_PAD
    return pl.pallas_call(
        body, name="mla_q_prep_bwd", grid=(S // tm,),
        in_specs=[_rows(tm, w), _rows(tm, w), _vec(HEAD_PAD)] + [_rows(tm, HEAD_PAD)] * 3,
        out_specs=[_rows(tm, w), _vec(HEAD_PAD, rows=8)],
        out_shape=[jax.ShapeDtypeStruct((S, w), BF16), jax.ShapeDtypeStruct((8, HEAD_PAD), F32)],
        compiler_params=_params(("arbitrary",)),
    )(dq, q0, g_qh, *tabs)


def _k_prep_bwd(dk, dv, kv0, proj, kpe_col, g_kh, tabs, nh):
    S = kv0.shape[0]
    tm = _tile(S, 256)
    wv = nh * HEAD

    def body(dk_ref, dv_ref, kv_ref, kpe_ref, g_ref, c_ref, s1_ref, s2_ref, o_ref, dpe_ref, red_ref):
        @pl.when(pl.program_id(0) == 0)
        def _():
            red_ref[...] = jnp.zeros_like(red_ref)

        c, s1, s2, g = c_ref[...], s1_ref[...], s2_ref[...], g_ref[...]
        kpe = kpe_ref[...].astype(F32)
        dg = jnp.zeros((1, HEAD_PAD), F32)
        dpe = jnp.zeros((tm, LANE), F32)
        for h in range(nh):
            d1 = _rope_bwd(dk_ref[:, h * HEAD_PAD:(h + 1) * HEAD_PAD], c, s1, s2)
            k0 = jnp.concatenate([kv_ref[:, h * HEAD:(h + 1) * HEAD], kpe], axis=1)
            r = _head_rms(k0)
            kn = k0 * r
            dg = dg + jnp.sum(d1 * kn, axis=0, keepdims=True)
            dn = d1 * g
            dk0 = r * (dn - kn * (jnp.sum(dn * kn, axis=-1, keepdims=True) * (1.0 / QK_DIM)))
            o_ref[:, h * HEAD:(h + 1) * HEAD] = dk0[:, :HEAD].astype(BF16)
            dpe = dpe + dk0[:, HEAD:]
        o_ref[:, wv:] = dv_ref[...].astype(BF16)
        dpe_ref[...] = dpe.astype(BF16)
        red_ref[0:1, :] += dg

    return pl.pallas_call(
        body, name="mla_k_prep_bwd", grid=(S // tm,),
        in_specs=[_rows(tm, nh * HEAD_PAD), _rows(tm, wv), _rows(tm, wv, 0), _rows(tm, LANE, kpe_col),
                  _vec(HEAD_PAD)] + [_rows(tm, HEAD_PAD)] * 3,
        out_specs=[_rows(tm, 2 * wv), _rows(tm, LANE), _vec(HEAD_PAD, rows=8)],
        out_shape=[jax.ShapeDtypeStruct((S, 2 * wv), BF16), jax.ShapeDtypeStruct((S, LANE), BF16),
                   jax.ShapeDtypeStruct((8, HEAD_PAD), F32)],
        compiler_params=_params(("arbitrary",)),
    )(dk, dv, kv0, proj, g_kh, *tabs)


def _latent_norm_bwd(dcqn, dckvn, proj, g_q, g_kv, ql):
    S = proj.shape[0]
    tm = _tile(S, 512)

    def body(dq_ref, dkv_ref, cq_ref, ckv_ref, gq_ref, gkv_ref, oq_ref, okv_ref, red_ref):
        @pl.when(pl.program_id(0) == 0)
        def _():
            red_ref[...] = jnp.zeros_like(red_ref)

        for row, (d_ref, c_ref, g_ref, o_ref) in enumerate(((dq_ref, cq_ref, gq_ref, oq_ref),
                                                            (dkv_ref, ckv_ref, gkv_ref, okv_ref))):
            d = d_ref[...]
            cv = c_ref[...].astype(F32)
            r = _rms_rows(cv)
            ch = cv * r
            red_ref[row:row + 1, :] += jnp.sum(d * ch, axis=0, keepdims=True)
            dn = d * g_ref[...]
            o_ref[...] = (r * (dn - ch * jnp.mean(dn * ch, axis=-1, keepdims=True))).astype(BF16)

    return pl.pallas_call(
        body, name="latent_norm_bwd", grid=(S // tm,),
        in_specs=[_rows(tm, ql), _rows(tm, ql), _rows(tm, ql, 0), _rows(tm, ql, 1), _vec(ql), _vec(ql)],
        out_specs=[_rows(tm, ql), _rows(tm, ql), _vec(ql, rows=8)],
        out_shape=[jax.ShapeDtypeStruct((S, ql), BF16)] * 2 + [jax.ShapeDtypeStruct((8, ql), F32)],
        compiler_params=_params(("arbitrary",)),
    )(dcqn, dckvn, proj, proj, g_q, g_kv)


NEG = -1e30
ATT_TILE = 512
SB_TILE = 512
SB_SUB = 128
_NT = (((1,), (1,)), ((), ()))
_TN = (((0,), (0,)), ((), ()))


def _dot(a, b, dn=(((1,), (0,)), ((), ()))):
    return lax.dot_general(a, b, dn, preferred_element_type=F32)


def _key_rows(kb, t):
    return pl.ds(pl.multiple_of(kb * t, t), t)


def _diag_mask(t, strict):
    r = lax.broadcasted_iota(jnp.int32, (t, t), 0)
    c = lax.broadcasted_iota(jnp.int32, (t, t), 1)
    return c < r if strict else c <= r


def _mla_fwd(q, k, kv0, nh):
    S = q.shape[0]
    t = _tile(S, ATT_TILE)

    def body(q_ref, k_ref, v_ref, o_ref, lse_ref):
        i = pl.program_id(1)
        qv = q_ref[...]

        def block(kb, carry, masked):
            m, l, acc = carry
            rows = _key_rows(kb, t)
            s = _dot(qv, k_ref[rows, :], _NT)
            if masked:
                s = jnp.where(_diag_mask(t, False), s, NEG)
            m_new = jnp.maximum(m, jnp.max(s, axis=-1, keepdims=True))
            alpha = jnp.exp(m - m_new)
            p = jnp.exp(s - m_new)
            l = alpha * l + jnp.sum(p, axis=-1, keepdims=True)
            acc = alpha * acc + _dot(p.astype(BF16), v_ref[rows, :].astype(BF16))
            return m_new, l, acc

        init = (jnp.full((t, 1), NEG, F32), jnp.zeros((t, 1), F32), jnp.zeros((t, HEAD), F32))
        carry = lax.fori_loop(0, i, lambda kb, c: block(kb, c, False), init)
        m, l, acc = block(i, carry, True)
        o_ref[...] = acc / l
        lse_ref[...] = m + jnp.log(l)

    return pl.pallas_call(
        body, name="mla_attn_fwd", grid=(nh, S // t),
        in_specs=[pl.BlockSpec((t, HEAD_PAD), lambda h, i: (i, h)),
                  pl.BlockSpec((S, HEAD_PAD), lambda h, i: (0, h)),
                  pl.BlockSpec((S, HEAD), lambda h, i: (0, nh + h))],
        out_specs=[pl.BlockSpec((t, HEAD), lambda h, i: (i, h)),
                   pl.BlockSpec((None, t, 1), lambda h, i: (h, i, 0))],
        out_shape=[jax.ShapeDtypeStruct((S, nh * HEAD), F32), jax.ShapeDtypeStruct((nh, S, 1), F32)],
        compiler_params=_params(("parallel", "arbitrary")),
    )(q, k, kv0)


def _mla_bwd(q, k, kv0, o, do, lse, nh):
    S = q.shape[0]
    t = _tile(S, ATT_TILE)
    scale = QK_DIM ** -0.5

    def body(q_ref, k_ref, v_ref, o_ref, do_ref, lse_ref, dq_ref, dk_ref, dv_ref):
        i = pl.program_id(1)

        @pl.when(i == 0)
        def _():
            dk_ref[...] = jnp.zeros_like(dk_ref)
            dv_ref[...] = jnp.zeros_like(dv_ref)

        qv = q_ref[...]
        dov = do_ref[...]
        delta = jnp.sum(dov * o_ref[...], axis=-1, keepdims=True)
        dob = dov.astype(BF16)
        lse = lse_ref[...]

        def block(kb, dq, masked):
            rows = _key_rows(kb, t)
            ks = k_ref[rows, :]
            vs = v_ref[rows, :].astype(BF16)
            p = jnp.exp(_dot(qv, ks, _NT) - lse)
            if masked:
                p = jnp.where(_diag_mask(t, False), p, 0.0)
            ds = (p * (_dot(dob, vs, _NT) - delta)).astype(BF16)
            dk_ref[rows, :] += _dot(ds, qv, _TN)
            dv_ref[rows, :] += _dot(p.astype(BF16), dob, _TN)
            return dq + _dot(ds, ks)

        dq = lax.fori_loop(0, i, lambda kb, c: block(kb, c, False), jnp.zeros((t, HEAD_PAD), F32))
        dq_ref[...] = block(i, dq, True) * scale

    return pl.pallas_call(
        body, name="mla_attn_bwd", grid=(nh, S // t),
        in_specs=[pl.BlockSpec((t, HEAD_PAD), lambda h, i: (i, h)),
                  pl.BlockSpec((S, HEAD_PAD), lambda h, i: (0, h)),
                  pl.BlockSpec((S, HEAD), lambda h, i: (0, nh + h)),
                  pl.BlockSpec((t, HEAD), lambda h, i: (i, h)),
                  pl.BlockSpec((t, HEAD), lambda h, i: (i, h)),
                  pl.BlockSpec((None, t, 1), lambda h, i: (h, i, 0))],
        out_specs=[pl.BlockSpec((t, HEAD_PAD), lambda h, i: (i, h)),
                   pl.BlockSpec((S, HEAD_PAD), lambda h, i: (0, h)),
                   pl.BlockSpec((S, HEAD), lambda h, i: (0, h))],
        out_shape=[jax.ShapeDtypeStruct((S, nh * HEAD_PAD), F32), jax.ShapeDtypeStruct((S, nh * HEAD_PAD), F32),
                   jax.ShapeDtypeStruct((S, nh * HEAD), F32)],
        compiler_params=_params(("parallel", "arbitrary")),
    )(q, k, kv0, o, do, lse)


def _tri(n, cmp):
    r = lax.broadcasted_iota(jnp.int32, (n, n), 0)
    c = lax.broadcasted_iota(jnp.int32, (n, n), 1)
    return jnp.where(cmp(r, c), 1.0, 0.0).astype(BF16)


def _sb_block(qv, ks, run, upper, t, masked):
    z = _dot(qv, ks, _NT)
    lb = jnp.minimum(z, 0.0) - jnp.log(1.0 + jnp.exp(-jnp.abs(z)))
    lom = lb - z
    mask = _diag_mask(t, True) if masked else None
    if masked:
        lom = jnp.where(mask, lom, 0.0)
    tails = []
    for sblk in reversed(range(t // SB_SUB)):
        part = lom[:, sblk * SB_SUB:(sblk + 1) * SB_SUB]
        tails.append(_dot(part.astype(BF16), upper) + run)
        run = run + jnp.sum(part, axis=-1, keepdims=True)
    a = jnp.exp(lb + jnp.concatenate(tails[::-1], axis=1))
    if masked:
        a = jnp.where(mask, a, 0.0)
    return a, lb, mask, run


def _sb_fwd(proj, q_col, k_col, v_col, nh):
    S = proj.shape[0]
    t = _tile(S, SB_TILE)
    assert S // t <= LANE
    scale = HEAD ** -0.5

    def body(q_ref, k_ref, v_ref, o_ref, runs_ref):
        i = pl.program_id(1)
        qv = (q_ref[...].astype(F32) * scale).astype(BF16)
        upper = _tri(SB_SUB, lambda j, s: j > s)
        lane = lax.broadcasted_iota(jnp.int32, (t, LANE), 1)

        def block(kb, carry, masked):
            run, acc, runs = carry
            runs = jnp.where(lane == kb, run, runs)
            rows = _key_rows(kb, t)
            a, _, _, run = _sb_block(qv, k_ref[rows, :].astype(BF16), run, upper, t, masked)
            return run, acc + _dot(a.astype(BF16), v_ref[rows, :].astype(BF16)), runs

        carry = block(i, (jnp.zeros((t, 1), F32), jnp.zeros((t, HEAD), F32), jnp.zeros((t, LANE), F32)), True)
        _, o_ref[...], runs_ref[...] = lax.fori_loop(0, i, lambda j, c: block(i - 1 - j, c, False), carry)

    return pl.pallas_call(
        body, name="sb_attn_fwd", grid=(nh, S // t),
        in_specs=[pl.BlockSpec((t, HEAD), lambda h, i: (i, q_col + h)),
                  pl.BlockSpec((S, HEAD), lambda h, i: (0, k_col + h)),
                  pl.BlockSpec((S, HEAD), lambda h, i: (0, v_col + h))],
        out_specs=[pl.BlockSpec((t, HEAD), lambda h, i: (i, h)), pl.BlockSpec((None, t, LANE), lambda h, i: (h, i, 0))],
        out_shape=[jax.ShapeDtypeStruct((S, nh * HEAD), F32), jax.ShapeDtypeStruct((nh, S, LANE), F32)],
        compiler_params=_params(("parallel", "arbitrary")),
    )(proj, proj, proj)


def _sb_bwd(proj, q_col, k_col, v_col, dy, runs, nh):
    S = proj.shape[0]
    t = _tile(S, SB_TILE)
    scale = HEAD ** -0.5

    def body(q_ref, k_ref, v_ref, dy_ref, runs_ref, dq_ref, dk_ref, dv_ref):
        i = pl.program_id(1)

        @pl.when(i == 0)
        def _():
            dk_ref[...] = jnp.zeros_like(dk_ref)
            dv_ref[...] = jnp.zeros_like(dv_ref)

        qv = (q_ref[...].astype(F32) * scale).astype(BF16)
        dyb = dy_ref[...].astype(BF16)
        runs_v = runs_ref[...]
        lane = lax.broadcasted_iota(jnp.int32, (t, LANE), 1)
        upper = _tri(SB_SUB, lambda j, s: j > s)
        before = _tri(SB_SUB, lambda s, j: s < j)

        def block(kb, carry, masked):
            prefix, dq = carry
            rows = _key_rows(kb, t)
            ks = k_ref[rows, :].astype(BF16)
            vs = v_ref[rows, :].astype(BF16)
            run = jnp.sum(jnp.where(lane == kb, runs_v, 0.0), axis=-1, keepdims=True)
            a, lb, mask, _ = _sb_block(qv, ks, run, upper, t, masked)
            dl = a * _dot(dyb, vs, _NT)
            lefts = []
            for sblk in range(t // SB_SUB):
                part = dl[:, sblk * SB_SUB:(sblk + 1) * SB_SUB]
                lefts.append(_dot(part.astype(BF16), before) + prefix)
                prefix = prefix + jnp.sum(part, axis=-1, keepdims=True)
            beta = jnp.exp(lb)
            dz = dl * (1.0 - beta) - beta * jnp.concatenate(lefts, axis=1)
            if masked:
                dz = jnp.where(mask, dz, 0.0)
            dz = dz.astype(BF16)
            dk_ref[rows, :] += _dot(dz, qv, _TN)
            dv_ref[rows, :] += _dot(a.astype(BF16), dyb, _TN)
            return prefix, dq + _dot(dz, ks)

        carry = lax.fori_loop(0, i, lambda kb, c: block(kb, c, False),
                              (jnp.zeros((t, 1), F32), jnp.zeros((t, HEAD), F32)))
        dq_ref[...] = block(i, carry, True)[1] * scale

    full = pl.BlockSpec((S, HEAD), lambda h, i: (0, h))
    tile = pl.BlockSpec((t, HEAD), lambda h, i: (i, h))
    return pl.pallas_call(
        body, name="sb_attn_bwd", grid=(nh, S // t),
        in_specs=[pl.BlockSpec((t, HEAD), lambda h, i: (i, q_col + h)),
                  pl.BlockSpec((S, HEAD), lambda h, i: (0, k_col + h)),
                  pl.BlockSpec((S, HEAD), lambda h, i: (0, v_col + h)), tile,
                  pl.BlockSpec((None, t, LANE), lambda h, i: (h, i, 0))],
        out_specs=[tile, full, full],
        out_shape=[jax.ShapeDtypeStruct((S, nh * HEAD), F32)] * 3,
        compiler_params=_params(("parallel", "arbitrary")),
    )(proj, proj, proj, dy, runs)


def _place():
    return lax.axis_index("x"), lax.axis_index("y"), lax.axis_index("c")


def _other_chips(x, y):
    return [(1 - x, y), (x, 1 - y), (1 - x, 1 - y)]


def _dev_index(p):
    return 4 * p[0] + 2 * p[1] + p[2]


def _gather_blocks(blocks, *, name, in_vmem):
    n = len(blocks)
    per = 7

    def body(*refs):
        ins, outs = refs[:n], refs[n:2 * n]
        send_sems, recv_sems, local_sems = refs[2 * n:]
        x, y, c = _place()
        me, sibling = (x, y, c), (x, y, 1 - c)
        chips = _other_chips(x, y)

        def slot(a, p):
            return outs[a].at[_dev_index(p)]

        def copy(a, k, block, to, src=None):
            return pltpu.make_async_remote_copy(
                src_ref=slot(a, block) if src is None else src, dst_ref=slot(a, block),
                send_sem=send_sems.at[a * per + k], recv_sem=recv_sems.at[a * per + k],
                device_id=to, device_id_type=MESH)

        mine = [pltpu.make_async_copy(ins[a], slot(a, me), local_sems.at[a]) for a in range(n)] if in_vmem else []
        for cp in mine:
            cp.start()
        first = []
        for a in range(n):
            first.append(copy(a, 0, me, sibling, src=ins[a]))
            first += [copy(a, 1 + j, me, (*chip, c), src=ins[a]) for j, chip in enumerate(chips)]
        for cp in first:
            cp.start()
        passed = []
        for a in range(n):
            for j, chip in enumerate(chips):
                copy(a, 1 + j, (*chip, c), me).wait_recv()
                cp = copy(a, 4 + j, (*chip, c), sibling)
                cp.start()
                passed.append(cp)
        for a in range(n):
            copy(a, 0, sibling, me).wait_recv()
            for j, chip in enumerate(chips):
                copy(a, 4 + j, (*chip, 1 - c), me).wait_recv()
        for cp in first + passed:
            cp.wait_send()
        for cp in mine:
            cp.wait()

    space = pltpu.VMEM if in_vmem else pl.ANY
    spec = pl.BlockSpec(memory_space=space)
    outs = pl.pallas_call(
        body, name=name, in_specs=[spec] * n, out_specs=[spec] * n,
        out_shape=[jax.ShapeDtypeStruct((N_DEV,) + b.shape, b.dtype) for b in blocks],
        scratch_shapes=[pltpu.SemaphoreType.DMA((n * per,)), pltpu.SemaphoreType.DMA((n * per,)),
                        pltpu.SemaphoreType.DMA((n,))],
        compiler_params=pltpu.CompilerParams(vmem_limit_bytes=VMEM_LIMIT),
    )(*blocks)
    return list(outs)


def _sibling_swap(arrs, *, name, whole=False):
    n = len(arrs)

    def body(*refs):
        ins, outs = refs[:n], refs[n:2 * n]
        send_sems, recv_sems = refs[2 * n:]
        x, y, c = _place()
        copies = [pltpu.make_async_remote_copy(
            src_ref=ins[a] if whole else ins[a].at[1 - c], dst_ref=outs[a],
            send_sem=send_sems.at[a], recv_sem=recv_sems.at[a],
            device_id=(x, y, 1 - c), device_id_type=MESH) for a in range(n)]
        for cp in copies:
            cp.start()
        for cp in copies:
            cp.wait()

    spec = pl.BlockSpec(memory_space=pl.ANY)
    return list(pl.pallas_call(
        body, name=name, in_specs=[spec] * n, out_specs=[spec] * n,
        out_shape=[jax.ShapeDtypeStruct(a.shape if whole else a.shape[1:], a.dtype) for a in arrs],
        scratch_shapes=[pltpu.SemaphoreType.DMA((n,)), pltpu.SemaphoreType.DMA((n,))],
    )(*arrs))


_HBM = pl.BlockSpec(memory_space=pltpu.HBM)
_SEM = pl.BlockSpec(memory_space=pltpu.SEMAPHORE)
_EFFECT = pltpu.SideEffectType.DATAFLOW_SIDE_EFFECTING


def _in_hbm(a):
    return pltpu.with_memory_space_constraint(a, pltpu.HBM)


def _split_copies(srcs, lands, send_sems, recv_sems, plan):
    x, y, c = _place()
    copies = []
    for a, (src, land) in enumerate(zip(srcs, lands)):
        steps = plan(x, y, c)
        for k, (pick, slot, to) in enumerate(steps):
            copies.append(pltpu.make_async_remote_copy(
                src_ref=pick(src), dst_ref=slot(land), send_sem=send_sems.at[a * len(steps) + k],
                recv_sem=recv_sems.at[a * len(steps) + k], device_id=to, device_id_type=MESH))
    return copies


def _split_start(srcs, land_shapes, plan, per, *, name):
    n = len(srcs)

    def body(*refs):
        send_sems, recv_sems = refs[2 * n], refs[2 * n + 1]
        for cp in _split_copies(refs[:n], refs[n:2 * n], send_sems, recv_sems, plan):
            cp.start()
        token = refs[-1]
        token[...] = jnp.zeros_like(token)

    lands = [_in_hbm(lax.empty(s.shape, s.dtype)) for s in land_shapes]
    outs = pl.pallas_call(
        body, name=name,
        out_shape=(pltpu.SemaphoreType.DMA((n * per,)), pltpu.SemaphoreType.DMA((n * per,)),
                   *[pltpu.HBM(s.shape, s.dtype) for s in srcs], *[pltpu.HBM(s.shape, s.dtype) for s in land_shapes],
                   jax.ShapeDtypeStruct((8, LANE), F32)),
        in_specs=[_HBM] * (2 * n),
        out_specs=(_SEM, _SEM, *[_HBM] * (2 * n), pl.BlockSpec(memory_space=pltpu.VMEM)),
        input_output_aliases={i: 2 + i for i in range(2 * n)},
        compiler_params=pltpu.CompilerParams(has_side_effects=_EFFECT),
    )(*[_in_hbm(s) for s in srcs], *lands)
    return outs[0], outs[1], list(outs[2:2 + n]), list(outs[2 + n:2 + 2 * n]), outs[-1]


def _split_wait(send_sems, recv_sems, srcs, lands, after, plan, *, name):
    n = len(srcs)

    def body(*refs):
        for cp in _split_copies(refs[:n], refs[n:2 * n], refs[2 * n], refs[2 * n + 1], plan):
            cp.wait_send()
            cp.wait_recv()

    outs = pl.pallas_call(
        body, name=name,
        out_shape=(*[pltpu.HBM(s.shape, s.dtype) for s in srcs], *[pltpu.HBM(s.shape, s.dtype) for s in lands]),
        in_specs=[_HBM] * (2 * n) + [_SEM, _SEM, pl.BlockSpec(memory_space=pl.ANY)],
        out_specs=tuple([_HBM] * (2 * n)),
        input_output_aliases={i: i for i in range(2 * n)},
        compiler_params=pltpu.CompilerParams(has_side_effects=_EFFECT),
    )(*srcs, *lands, send_sems, recv_sems, after)
    return list(outs[:n]), list(outs[n:])


def _gather_plan(x, y, c):
    slot = lambda land: land.at[_dev_index((x, y, c))]
    whole = lambda src: src
    return [(whole, slot, (x, y, 1 - c))] + [(whole, slot, (px, py, c)) for px, py in _other_chips(x, y)]


def _swap_plan(x, y, c):
    return [(lambda src: src.at[1 - c], lambda land: land, (x, y, 1 - c))]


def _exchange_plan(x, y, c):
    return [(lambda src, k=2 * px + py: src.at[k], lambda land, j=j: land.at[j], (px, py, c))
            for j, (px, py) in enumerate(_other_chips(x, y))]


def _gather_forward(lands, *, name):
    n = len(lands)

    def body(*refs):
        lands_in, outs = refs[:n], refs[n:2 * n]
        send_sems, recv_sems = refs[2 * n:]
        x, y, c = _place()
        copies = []
        for a in range(n):
            for j, (px, py) in enumerate(_other_chips(x, y)):
                copies.append((pltpu.make_async_remote_copy(
                    src_ref=lands_in[a].at[_dev_index((px, py, c))], dst_ref=outs[a].at[_dev_index((px, py, c))],
                    send_sem=send_sems.at[3 * a + j], recv_sem=recv_sems.at[3 * a + j],
                    device_id=(x, y, 1 - c), device_id_type=MESH), a, j, (px, py)))
        for cp, _, _, _ in copies:
            cp.start()
        for cp, a, j, (px, py) in copies:
            cp.wait_send()
            pltpu.make_async_remote_copy(
                src_ref=lands_in[a].at[_dev_index((px, py, 1 - c))], dst_ref=outs[a].at[_dev_index((px, py, 1 - c))],
                send_sem=send_sems.at[3 * a + j], recv_sem=recv_sems.at[3 * a + j],
                device_id=(x, y, 1 - c), device_id_type=MESH).wait_recv()

    spec = pl.BlockSpec(memory_space=pl.ANY)
    return list(pl.pallas_call(
        body, name=name, in_specs=[spec] * n, out_specs=[spec] * n,
        out_shape=[jax.ShapeDtypeStruct(a.shape, a.dtype) for a in lands],
        input_output_aliases={a: a for a in range(n)},
        scratch_shapes=[pltpu.SemaphoreType.DMA((3 * n,)), pltpu.SemaphoreType.DMA((3 * n,))],
    )(*lands))


def _flat2(a, lead):
    return a.reshape(a.shape[:lead] + (-1, a.shape[-1]))


def _pair_sum(g, recv, c_idx, *, name):
    _, nchip, r, w = g.shape
    tm = _tile(r, 256) if r % 8 == 0 else r

    def body(c_ref, g_ref, r_ref, o_ref):
        o_ref[...] = (g_ref[...].astype(F32) + r_ref[...].astype(F32)).astype(o_ref.dtype)

    return pl.pallas_call(
        body, name=name,
        grid_spec=pltpu.PrefetchScalarGridSpec(
            num_scalar_prefetch=1, grid=(nchip, r // tm),
            in_specs=[pl.BlockSpec((None, None, tm, w), lambda k, i, c_ref: (c_ref[0], k, i, 0)),
                      pl.BlockSpec((None, tm, w), lambda k, i, c_ref: (k, i, 0))],
            out_specs=pl.BlockSpec((None, tm, w), lambda k, i, c_ref: (k, i, 0))),
        out_shape=jax.ShapeDtypeStruct((nchip, r, w), BF16),
        compiler_params=_params(("parallel", "parallel")),
    )(c_idx, g, recv)


def _chip_sum(s1, recv, chip_idx, *, name):
    _, r, w = s1.shape
    tm = _tile(r, 256) if r % 8 == 0 else r

    def body(k_ref, s_ref, r_ref, o_ref):
        acc = s_ref[...].astype(F32)
        for j in range(3):
            acc = acc + r_ref[j].astype(F32)
        o_ref[...] = acc

    return pl.pallas_call(
        body, name=name,
        grid_spec=pltpu.PrefetchScalarGridSpec(
            num_scalar_prefetch=1, grid=(r // tm,),
            in_specs=[pl.BlockSpec((None, tm, w), lambda i, k_ref: (k_ref[0], i, 0)),
                      pl.BlockSpec((3, tm, w), lambda i, k_ref: (0, i, 0))],
            out_specs=pl.BlockSpec((tm, w), lambda i, k_ref: (i, 0))),
        out_shape=jax.ShapeDtypeStruct((r, w), F32),
        compiler_params=_params(("parallel",)),
    )(chip_idx, s1, recv)


def _adam_math(w, g, m, v):
    m = ADAM_B1 * m + (1.0 - ADAM_B1) * g
    v = ADAM_B2 * v + (1.0 - ADAM_B2) * (g * g)
    m_hat = m / (1.0 - ADAM_B1 ** ADAM_STEP)
    v_hat = v / (1.0 - ADAM_B2 ** ADAM_STEP)
    delta = -ADAM_LR * (m_hat / (jnp.sqrt(v_hat) + ADAM_EPS) + ADAM_WD * w)
    return delta, m, v


def _adamw(w, mine, other, c_idx, m, v, *, name):
    r, cw = w.shape
    hr = r // 2
    tm = _row_tile(hr, 9 * cw * 4)

    def body(c_ref, w_ref, a_ref, b_ref, m_ref, v_ref, g_ref, d_ref, nm_ref, nv_ref):
        g = jnp.where(pl.program_id(0) == c_ref[0], a_ref[...], b_ref[...])
        g_ref[...] = g
        d_ref[...], nm_ref[...], nv_ref[...] = _adam_math(w_ref[...], g, m_ref[...], v_ref[...])

    per_half = hr // tm
    full = pl.BlockSpec((tm, cw), lambda h, i, c_ref: (h * per_half + i, 0))
    half = pl.BlockSpec((tm, cw), lambda h, i, c_ref: (i, 0))
    return pl.pallas_call(
        body, name=name,
        grid_spec=pltpu.PrefetchScalarGridSpec(
            num_scalar_prefetch=1, grid=(2, per_half),
            in_specs=[full, half, half, full, full], out_specs=[full] * 4),
        out_shape=[jax.ShapeDtypeStruct((r, cw), F32)] * 4,
        compiler_params=_params(("parallel", "parallel")),
    )(c_idx, w, mine, other, m, v)


def _adamw_ada(cact_t, dada, w, m, v):
    r, cw = w.shape
    nb = cact_t.shape[1]
    tm = _tile(r, 256)
    tn = _tile(cw, 1024)

    def body(a_ref, d_ref, w_ref, m_ref, v_ref, g_ref, dl_ref, nm_ref, nv_ref):
        a = a_ref[...]
        d = d_ref[...]
        g = a[:, 0:1] * d[0:1, :]
        for b in range(1, nb):
            g = g + a[:, b:b + 1] * d[b:b + 1, :]
        g_ref[...] = g
        dl_ref[...], nm_ref[...], nv_ref[...] = _adam_math(w_ref[...], g, m_ref[...], v_ref[...])

    blk = pl.BlockSpec((tm, tn), lambda i, j: (i, j))
    return pl.pallas_call(
        body, name="adamw_ada", grid=(r // tm, cw // tn),
        in_specs=[pl.BlockSpec((tm, nb), lambda i, j: (i, 0)), pl.BlockSpec((nb, tn), lambda i, j: (0, j)), blk, blk, blk],
        out_specs=[blk] * 4, out_shape=[jax.ShapeDtypeStruct((r, cw), F32)] * 4,
        compiler_params=_params(("parallel", "parallel")),
    )(cact_t, dada, w, m, v)


def _adamw_vec(parts, w, m, v):
    n = w.shape[1]

    def body(p_ref, w_ref, m_ref, v_ref, g_ref, d_ref, nm_ref, nv_ref):
        p = p_ref[...]
        g = p[0:1, :]
        for b in range(1, N_DEV):
            g = g + p[b:b + 1, :]
        g_ref[...] = g
        d_ref[...], nm_ref[...], nv_ref[...] = _adam_math(w_ref[...], g, m_ref[...], v_ref[...])

    return pl.pallas_call(
        body, name="adamw_vec", out_shape=[jax.ShapeDtypeStruct((1, n), F32)] * 4,
        compiler_params=pltpu.CompilerParams(vmem_limit_bytes=VMEM_LIMIT),
    )(parts, w, m, v)


def _w_in_segments(kpe0, d_in, cs):
    segs = []
    for k in range(4):
        lo, hi = k * cs, (k + 1) * cs
        for a, b, shift in ((0, kpe0, 0), (kpe0, kpe0 + ROPE, d_in - ROPE - kpe0), (kpe0 + ROPE, d_in, -ROPE)):
            a, b = max(lo, a), min(hi, b)
            if a < b:
                segs.append((k, a - lo, a + shift, b - a))
    return segs


def _w_in_layout(g8, kpe0):
    _, hr, cs = g8.shape
    rows, d_in = 2 * hr, 4 * cs
    segs = _w_in_segments(kpe0, d_in, cs)
    tm = _tile(rows, 256)

    def body(g_ref, o_ref):
        for k, src, dst, w in segs:
            o_ref[:, dst:dst + w] = g_ref[k, :, src:src + w]
        o_ref[:, d_in:] = jnp.zeros((tm, ROPE), o_ref.dtype)

    return pl.pallas_call(
        body, name="w_in_layout", grid=(rows // tm,),
        in_specs=[pl.BlockSpec((4, tm, cs), lambda i: (0, i, 0))], out_specs=_rows(tm, d_in + ROPE),
        out_shape=jax.ShapeDtypeStruct((rows, d_in + ROPE), g8.dtype), compiler_params=_params(("parallel",)),
    )(g8.reshape(4, rows, cs))


def _w_in_grad_pieces(g, kpe0):
    rows, d_in_p = g.shape
    d_in = d_in_p - ROPE
    cs = d_in // 4
    segs = _w_in_segments(kpe0, d_in, cs)
    hr = rows // 2
    tm = _tile(hr, 256)
    per_half = hr // tm

    def body(g_ref, o_ref):
        for k, src, dst, w in segs:
            o_ref[k, :, src:src + w] = g_ref[:, dst:dst + w]

    return pl.pallas_call(
        body, name="w_in_grad_pieces", grid=(rows // tm,),
        in_specs=[_rows(tm, d_in_p)],
        out_specs=pl.BlockSpec((None, 4, tm, cs), lambda i: (i // per_half, 0, i % per_half, 0)),
        out_shape=jax.ShapeDtypeStruct((2, 4, hr, cs), g.dtype), compiler_params=_params(("parallel",)),
    )(g)


def _interleave_layout(g8, ib):
    _, hr, cs = g8.shape
    rows, per_chip, per_half = 2 * hr, cs // ib, 2 * cs // ib
    tm = _tile(rows, 2048)

    def src(jj):
        return jj // 2 + per_half * (jj % 2)

    def body(g_ref, o_ref):
        o_ref[...] = g_ref[...]

    return pl.pallas_call(
        body, name="interleave_layout", grid=(rows // tm, 4 * per_chip),
        in_specs=[pl.BlockSpec((None, tm, ib), lambda i, jj: (src(jj) // per_chip, i, src(jj) % per_chip))],
        out_specs=pl.BlockSpec((tm, ib), lambda i, jj: (i, jj)),
        out_shape=jax.ShapeDtypeStruct((rows, 4 * cs), g8.dtype), compiler_params=_params(("parallel", "parallel")),
    )(g8.reshape(4, rows, cs))


def _cols_from_chips(g8, rows):
    cs = g8.shape[-1]
    return g8.reshape(4, rows, cs).transpose(1, 0, 2).reshape(rows, 4 * cs)


def _cols_to_pieces(g):
    rows, c4 = g.shape
    return g.reshape(2, rows // 2, 4, c4 // 4).transpose(0, 2, 1, 3)


def _rows_to_pieces(g):
    r4, cols = g.shape
    return g.reshape(4, 2, r4 // 8, cols).transpose(1, 0, 2, 3)


def _pad_cols(a, w):
    return jnp.pad(a, ((0, 0), (0, w - a.shape[1])))


def kernel(x, c, positions, w_ada, b_ada, g_norm1, g_norm2, w_in, g_q_latent, g_kv_latent, w_uq, w_ukv, g_q_head, g_k_head, w_proj_mla, w_proj_sb, w_out, w_ffn_in, w_ffn_out, loss_target, m_w_ada, m_b_ada, m_g_norm1, m_g_norm2, m_w_in, m_g_q_latent, m_g_kv_latent, m_w_uq, m_w_ukv, m_g_q_head, m_g_k_head, m_w_proj_mla, m_w_proj_sb, m_w_out, m_w_ffn_in, m_w_ffn_out, v_w_ada, v_b_ada, v_g_norm1, v_g_norm2, v_w_in, v_g_q_latent, v_g_kv_latent, v_w_uq, v_w_ukv, v_g_q_head, v_g_k_head, v_w_proj_mla, v_w_proj_sb, v_w_out, v_w_ffn_in, v_w_ffn_out):
    xi, yi, ci = _place()
    chip = 2 * xi + yi
    dev = 2 * chip + ci
    c_idx = jnp.reshape(ci, (1,)).astype(jnp.int32)
    chip_idx = jnp.reshape(chip, (1,)).astype(jnp.int32)

    x = x[0]
    tgt = loss_target[0]
    S, D = x.shape
    ql = g_q_latent.shape[1]
    assert g_kv_latent.shape[1] == ql
    mlaw = w_proj_mla.shape[1]
    nh = mlaw // HEAD
    sbw = w_proj_sb.shape[1]
    assert sbw == mlaw
    dff = w_ffn_out.shape[1] * 4
    d_in = 2 * ql + ROPE + 3 * sbw + 2 * D
    d_in_p = d_in + ROPE
    q_col = (2 * ql) // HEAD
    k_col = q_col + nh
    v_col = k_col + nh
    gla_col = (2 * ql + 3 * sbw) // D
    glb_col = gla_col + 1
    kpe_col = (d_in - ROPE) // LANE
    assert (2 * ql + 3 * sbw) % D == 0 and (d_in - ROPE) % LANE == 0

    mats = {"w_in": w_in[0], "w_uq": w_uq[0], "w_ukv": w_ukv[0], "w_proj_mla": w_proj_mla[0],
            "w_proj_sb": w_proj_sb[0], "w_out": w_out[0], "w_ffn_in": w_ffn_in[0], "w_ffn_out": w_ffn_out[0]}
    names = list(mats)
    row_sharded = {"w_out", "w_ffn_out"}

    c_all = _gather_blocks([jnp.broadcast_to(c, (8, D))], name="gather_cond", in_vmem=True)[0][:, 0, :]
    n_ada = w_ada.shape[2]
    b_shard = lax.dynamic_slice_in_dim(b_ada, chip * n_ada, n_ada, axis=1)
    ada_shard = _mm(c_all, w_ada[0], name="ada_proj", a_fn=jax.nn.silu, bias=b_shard)
    ada_all = _gather_blocks([ada_shard], name="gather_ada", in_vmem=True)[0]
    ada_rows = lax.dynamic_index_in_dim(ada_all, dev, axis=1, keepdims=False)
    ada = ada_rows[0::2].reshape(1, 4 * n_ada)
    SH1, SC1, GT1, SH2, SC2, GT2 = range(6)

    def after(dep, a):
        return a + (dep.reshape(-1)[0:1].reshape((1,) * a.ndim) * 0).astype(a.dtype)

    def fill_own(g8, own):
        return lax.dynamic_update_index_in_dim(g8, own, dev, 0)

    halves = []
    for nm in names:
        w = mats[nm]
        hr = w.shape[0] // 2
        halves.append(lax.dynamic_slice_in_dim(w, ci * hr, hr, axis=0).astype(BF16))
    half_of = dict(zip(names, halves))
    early = ["w_in", "w_uq", "w_ukv"]
    late = ["w_proj_mla", "w_proj_sb", "w_out", "w_ffn_in", "w_ffn_out"]
    early_halves = [half_of[nm] for nm in early]
    early_halves[0] = after(ada, early_halves[0])
    early_got = _gather_blocks(early_halves, name="gather_weights", in_vmem=False)
    gathered = {nm: fill_own(g8, own) for nm, g8, own in zip(early, early_got, early_halves)}
    late_halves = [half_of[nm] for nm in late]
    late_halves[0] = after(gathered[early[1]], late_halves[0])
    late_send, late_recv, late_srcs, late_lands, late_token = _split_start(
        late_halves, [jax.ShapeDtypeStruct((N_DEV,) + h.shape, h.dtype) for h in late_halves], _gather_plan, 4,
        name="gather_late_start")
    ada = ada + late_token[0:1, 0:1]

    def full_cols(nm):
        return _cols_from_chips(gathered[nm], mats[nm].shape[0])

    kpe0 = 2 * ql
    w_in_p = _w_in_layout(gathered["w_in"], kpe0)
    w_uq_p = jnp.pad(full_cols("w_uq").reshape(ql, nh, QK_DIM), ((0, 0), (0, 0), (0, HEAD_PAD - QK_DIM))
                     ).reshape(ql, nh * HEAD_PAD)
    w_ukv4 = full_cols("w_ukv").reshape(ql, nh, 2 * HEAD)
    w_ukv_p = jnp.concatenate([w_ukv4[:, :, :HEAD].reshape(ql, mlaw), w_ukv4[:, :, HEAD:].reshape(ql, mlaw)], axis=1)

    half = ROPE // 2
    freqs = ROPE_THETA ** (-jnp.arange(half, dtype=F32) / half)
    ang = positions[0].astype(F32)[:, None] * freqs
    cos, sin = jnp.cos(ang), jnp.sin(ang)
    one = jnp.ones((S, NOPE), F32)
    zero = jnp.zeros((S, NOPE), F32)
    zh = jnp.zeros((S, half), F32)
    tabs = (jnp.concatenate([one, cos, cos, one[:, :HEAD_PAD - QK_DIM]], axis=1),
            jnp.concatenate([zero, zh, sin, zero[:, :HEAD_PAD - QK_DIM]], axis=1),
            jnp.concatenate([zero, -sin, zh, zero[:, :HEAD_PAD - QK_DIM]], axis=1))
    g_qh_p = _pad_cols(g_q_head, HEAD_PAD)
    g_kh_p = _pad_cols(g_k_head, HEAD_PAD)

    h1 = _rmsmod(x, g_norm1, ada, SC1, SH1, name="rmsmod1")
    proj = _mm(h1, w_in_p, name="mm_proj", tn=640, out_dtype=BF16)
    cqn, ckvn = _latent_norm(proj, g_q_latent, g_kv_latent, ql)
    q0 = _mm(cqn, w_uq_p, name="mm_q_up")
    kv0 = _mm(ckvn, w_ukv_p, name="mm_kv_up")
    q = _q_prep(q0, g_qh_p, tabs, nh)
    k = _k_prep(kv0, proj, kpe_col, g_kh_p, tabs, nh)
    y_a, lse = _mla_fwd(q, k, kv0, nh)
    y_b, sb_runs = _sb_fwd(proj, q_col, k_col, v_col, nh)
    late_srcs, late_lands = _split_wait(late_send, late_recv, late_srcs, late_lands, y_b, _gather_plan,
                                        name="gather_late_wait")
    late_got = _gather_forward(late_lands, name="gather_late_forward")
    gathered.update({nm: fill_own(g8, own) for nm, g8, own in zip(late, late_got, late_srcs)})
    w_pm = full_cols("w_proj_mla")
    w_ps = full_cols("w_proj_sb")
    w_o = gathered["w_out"].reshape(D, D)
    ib = 256 if (dff // 2) % 256 == 0 else LANE
    nb = dff // ib
    w_fi = _interleave_layout(gathered["w_ffn_in"], ib)
    w_fo = gathered["w_ffn_out"].reshape(dff, D)
    pa = _mm(y_a, w_pm, name="mm_proj_mla", out_dtype=BF16)
    pb = _mm(y_b, w_ps, name="mm_proj_sb", out_dtype=BF16)
    merged = _gate_merge(pa, pb, proj, gla_col, glb_col)
    o = _mm(merged, w_o, name="mm_out")
    x2, h2 = _resid_rmsmod(x, o, g_norm2, ada, GT1, SC2, SH2)
    ff, act = _mm(h2, w_fi, name="mm_ffn_in", tn=4 * ib,
                  fused=(_swiglu_tile(ib), [], [(2 * dff, 4 * ib, BF16), (dff, 2 * ib, BF16)]))
    f = _mm(act, w_fo, name="mm_ffn_out")
    dy, df, red_l, loss_p = _loss_head(x2, f, tgt, ada, GT2)

    dff_, = _mm(df, w_fo, name="mm_d_act", tb=True, tn=2 * ib,
                fused=(_swiglu_bwd_tile(ib), [(ff, 4 * ib)], [(2 * dff, 4 * ib, BF16)]))
    def pc(kind):
        if kind == "cols":
            return kind
        return kind if (D // 4) % LANE == 0 and (dff // 4) % LANE == 0 else None

    gw_fo = _mm(act, df, name="mm_gw_ffn_out", ta=True, out_dtype=BF16, pieces=pc("rows"))
    dh2 = _mm(dff_, w_fi, name="mm_d_h2", tb=True)
    gw_fi = _mm(h2, dff_, name="mm_gw_ffn_in", ta=True, out_dtype=BF16, pieces="cols", tn=ib,
                col_perm=lambda jj: jj // 2 + nb * (jj % 2))

    def to_pieces(nms, grads):
        return [g if g.ndim == 4 else (_rows_to_pieces if nm in row_sharded else _cols_to_pieces)(g)
                for nm, g in zip(nms, grads)]

    def pair_sums(nms, pcs, got):
        return [_pair_sum(p, r, c_idx, name="rs_pair_sum_" + nm) for p, r, nm in zip(pcs, got, nms)]

    def swap_start(pcs, tag):
        return _split_start(pcs, [jax.ShapeDtypeStruct(p.shape[1:], p.dtype) for p in pcs], _swap_plan, 1,
                            name="rs_swap_%s_start" % tag)

    def exchange_start(pair, tag):
        return _split_start(pair, [jax.ShapeDtypeStruct((3,) + p.shape[1:], p.dtype) for p in pair], _exchange_plan, 3,
                            name="rs_exchange_%s_start" % tag)

    ffn = ["w_ffn_in", "w_ffn_out"]
    sw = swap_start(to_pieces(ffn, [gw_fi, gw_fo]), "ffn")
    ada = ada + sw[4][0:1, 0:1]
    dx2, do, red_2 = _rmsmod2_bwd(dh2, x2, dy, o, g_norm2, ada, SC2, GT1)
    ffn_pcs, ffn_got = _split_wait(sw[0], sw[1], sw[2], sw[3], dx2, _swap_plan, name="rs_swap_ffn_wait")
    ffn_send, ffn_recv, ffn_pair, ffn_lands, ffn_token = exchange_start(pair_sums(ffn, ffn_pcs, ffn_got), "ffn")
    dmerged = _mm(do, w_o, name="mm_d_merged", tb=True, out_dtype=BF16,
                  bias=jnp.zeros((1, D), F32) + ffn_token[0:1, 0:1])
    gw_o = _mm(merged, do, name="mm_gw_out", ta=True, out_dtype=BF16, pieces=pc("rows"))
    dpa, dpb, dgla, dglb = _gate_bwd(dmerged, pa, pb, proj, gla_col, glb_col)
    gw_pm = _mm(y_a, dpa, name="mm_gw_proj_mla", ta=True, out_dtype=BF16, pieces=pc("cols"))
    gw_ps = _mm(y_b, dpb, name="mm_gw_proj_sb", ta=True, out_dtype=BF16, pieces=pc("cols"))
    mid = ["w_proj_mla", "w_proj_sb", "w_out"]
    mid_pcs = to_pieces(mid, [gw_pm, gw_ps, gw_o])
    mid_pair = pair_sums(mid, mid_pcs, _sibling_swap(mid_pcs, name="rs_sibling_swap_mid"))
    mid_send, mid_recv, mid_pair, mid_lands, mid_token = exchange_start(mid_pair, "mid")
    behind_mid = jnp.zeros((1, mlaw), F32) + mid_token[0:1, 0:1]
    dya = _mm(dpa, w_pm, name="mm_d_ya", tb=True, out_dtype=BF16, bias=behind_mid)
    dyb = _mm(dpb, w_ps, name="mm_d_yb", tb=True, out_dtype=BF16, bias=behind_mid)
    dq, dk, dv = _mla_bwd(q, k, kv0, y_a, dya, lse, nh)
    dq_sb, dk_sb, dv_sb = _sb_bwd(proj, q_col, k_col, v_col, dyb, sb_runs, nh)
    dq0, red_qh = _q_prep_bwd(dq, q0, g_qh_p, tabs, nh)
    dkv0, dkpe, red_kh = _k_prep_bwd(dk, dv, kv0, proj, kpe_col, g_kh_p, tabs, nh)
    dcqn = _mm(dq0, w_uq_p, name="mm_d_cqn", tb=True, out_dtype=BF16)
    gw_uq_p = _mm(cqn, dq0, name="mm_gw_uq", ta=True, out_dtype=BF16)
    dckvn = _mm(dkv0, w_ukv_p, name="mm_d_ckvn", tb=True, out_dtype=BF16)
    gw_ukv_p = _mm(ckvn, dkv0, name="mm_gw_ukv", ta=True, out_dtype=BF16)
    dcq, dckv, red_lat = _latent_norm_bwd(dcqn, dckvn, proj, g_q_latent, g_kv_latent, ql)
    dproj = jnp.concatenate([dcq, dckv, dq_sb.astype(BF16), dk_sb.astype(BF16), dv_sb.astype(BF16),
                             dgla, dglb, dkpe], axis=1)
    gw_in_p = _mm(h1, dproj, name="mm_gw_in", ta=True, out_dtype=BF16, tn=640)

    gw_in = _w_in_grad_pieces(gw_in_p, kpe0)
    gw_uq = gw_uq_p.reshape(ql, nh, HEAD_PAD)[:, :, :QK_DIM].reshape(ql, nh * QK_DIM)
    gw_ukv = jnp.concatenate([gw_ukv_p[:, :mlaw].reshape(ql, nh, HEAD), gw_ukv_p[:, mlaw:].reshape(ql, nh, HEAD)],
                             axis=2).reshape(ql, 2 * mlaw)
    last = ["w_in", "w_uq", "w_ukv"]
    assert last + mid + ffn == names

    last_pcs = to_pieces(last, [gw_in, gw_uq, gw_ukv])
    last_pair = pair_sums(last, last_pcs, _sibling_swap(last_pcs, name="rs_sibling_swap_last"))
    last_send, last_recv, last_pair, last_lands, last_token = exchange_start(last_pair, "last")
    ada = ada + last_token[0:1, 0:1]
    dh1 = _mm(dproj, w_in_p, name="mm_d_h1", tb=True, bias=jnp.zeros((1, D), F32) + last_token[0:1, 0:1])
    grad_x, red_1 = _rmsmod1_bwd(dh1, x, dx2, g_norm1, ada, SC1)
    last_pair, last_chips = _split_wait(last_send, last_recv, last_pair, last_lands, grad_x, _exchange_plan,
                                        name="rs_exchange_last_wait")
    mid_pair, mid_chips = _split_wait(mid_send, mid_recv, mid_pair, mid_lands, grad_x, _exchange_plan,
                                      name="rs_exchange_mid_wait")
    ffn_pair, ffn_chips = _split_wait(ffn_send, ffn_recv, ffn_pair, ffn_lands, grad_x, _exchange_plan,
                                      name="rs_exchange_ffn_wait")
    reduced = [_chip_sum(s, r, chip_idx, name="rs_chip_sum_" + nm)
               for s, r, nm in zip(last_pair + mid_pair + ffn_pair, last_chips + mid_chips + ffn_chips, names)]
    from_sibling2 = _sibling_swap(reduced, name="rs_sibling_send", whole=True)

    vec_names = ["b_ada", "g_norm1", "g_norm2", "g_q_latent", "g_kv_latent", "g_q_head", "g_k_head"]
    vec_w = dict(b_ada=b_ada, g_norm1=g_norm1, g_norm2=g_norm2, g_q_latent=g_q_latent, g_kv_latent=g_kv_latent,
                 g_q_head=g_q_head, g_k_head=g_k_head)
    vec_m = dict(b_ada=m_b_ada, g_norm1=m_g_norm1, g_norm2=m_g_norm2, g_q_latent=m_g_q_latent,
                 g_kv_latent=m_g_kv_latent, g_q_head=m_g_q_head, g_k_head=m_g_k_head)
    vec_v = dict(b_ada=v_b_ada, g_norm1=v_g_norm1, g_norm2=v_g_norm2, g_q_latent=v_g_q_latent,
                 g_kv_latent=v_g_kv_latent, g_q_head=v_g_q_head, g_k_head=v_g_k_head)
    d_ada = jnp.concatenate([red_1[0:1], red_1[1:2], red_2[3:4], red_2[0:1], red_2[1:2], red_l[0:1]], axis=1)
    vec_parts = dict(b_ada=d_ada, g_norm1=red_1[2:3], g_norm2=red_2[2:3], g_q_latent=red_lat[0:1],
                     g_kv_latent=red_lat[1:2], g_q_head=red_qh[0:1], g_k_head=red_kh[0:1])
    widths = [-(-vec_w[nm].shape[1] // LANE) * LANE for nm in vec_names]
    offs = [sum(widths[:i]) for i in range(len(widths))]
    pack = lambda d: jnp.concatenate([_pad_cols(d[nm][:, :vec_w[nm].shape[1]], wd) for nm, wd in zip(vec_names, widths)], axis=1)
    nvec = sum(widths) + LANE
    no_loss = jnp.zeros((1, LANE), F32)
    parts = jnp.concatenate([pack(vec_parts), loss_p[0:1, :]], axis=1)
    parts_all = _gather_blocks([jnp.broadcast_to(parts, (8, nvec))], name="gather_vec_grads",
                               in_vmem=True)[0][:, 0, :]
    gvec, dvec, nmvec, nvvec = _adamw_vec(parts_all, *[jnp.concatenate([pack(d), no_loss], axis=1)
                                                       for d in (vec_w, vec_m, vec_v)])
    loss = gvec[0, nvec - LANE]
    unpack = lambda a: {nm: a[:, o_:o_ + vec_w[nm].shape[1]] for nm, o_ in zip(vec_names, offs)}
    gvec, dvec, nmvec, nvvec = unpack(gvec), unpack(dvec), unpack(nmvec), unpack(nvvec)

    dada_all = lax.dynamic_slice_in_dim(parts_all[:, :6 * D], chip * n_ada, n_ada, axis=1)
    cact_t = jax.nn.silu(c_all).T
    g_ada, d_ada_w, nm_ada, nv_ada = _adamw_ada(cact_t, dada_all, w_ada[0], m_w_ada[0], v_w_ada[0])

    ms = dict(w_in=m_w_in, w_uq=m_w_uq, w_ukv=m_w_ukv, w_proj_mla=m_w_proj_mla, w_proj_sb=m_w_proj_sb,
              w_out=m_w_out, w_ffn_in=m_w_ffn_in, w_ffn_out=m_w_ffn_out)
    vs = dict(w_in=v_w_in, w_uq=v_w_uq, w_ukv=v_w_ukv, w_proj_mla=v_w_proj_mla, w_proj_sb=v_w_proj_sb,
              w_out=v_w_out, w_ffn_in=v_w_ffn_in, w_ffn_out=v_w_ffn_out)
    G, DL, NM, NV = {}, {}, {}, {}
    for nm, mine, other in zip(names, reduced, from_sibling2):
        g_, d_, m_, v_ = _adamw(mats[nm], mine, other, c_idx, ms[nm][0], vs[nm][0], name="adamw_" + nm)
        G[nm], DL[nm], NM[nm], NV[nm] = g_[None], d_[None], m_[None], v_[None]
    G["w_ada"], DL["w_ada"], NM["w_ada"], NV["w_ada"] = g_ada[None], d_ada_w[None], nm_ada[None], nv_ada[None]
    for nm in vec_names:
        G[nm], DL[nm], NM[nm], NV[nm] = gvec[nm], dvec[nm], nmvec[nm], nvvec[nm]

    order = ["w_ada", "b_ada", "g_norm1", "g_norm2", "w_in", "g_q_latent", "g_kv_latent", "w_uq", "w_ukv",
             "g_q_head", "g_k_head", "w_proj_mla", "w_proj_sb", "w_out", "w_ffn_in", "w_ffn_out"]
    return (loss, grad_x[None], *[G[n] for n in order], *[DL[n] for n in order],
            *[NM[n] for n in order], *[NV[n] for n in order])
```

```python
import functools
import math

import jax
import jax.numpy as jnp
from jax import lax
from jax.experimental import pallas as pl
from jax.experimental.pallas import tpu as pltpu

F32 = jnp.float32
BF16 = jnp.bfloat16
MESH = pl.DeviceIdType.MESH

EPS = 1e-6
ROPE_THETA = 10000.0
NOPE = 128
ROPE = 64
QK_DIM = NOPE + ROPE
HEAD_PAD = 256
HEAD = 128
N_DEV = 8
LANE = 128
VMEM_LIMIT = 48 * 1024 * 1024

ADAM_LR = 0.001
ADAM_B1 = 0.9
ADAM_B2 = 0.999
ADAM_EPS = 1e-08
ADAM_WD = 0.01
ADAM_STEP = 10


def _tile(n, target):
    if n <= target:
        return n
    t = (target // LANE) * LANE
    while t >= LANE:
        if n % t == 0:
            return t
        t -= LANE
    return n


def _row_tile(rows, row_bytes, budget=24 * 1024 * 1024):
    cap = max(8, budget // (2 * row_bytes))
    best = None
    for t in range(8, min(rows, cap) + 1, 8):
        if rows % t == 0:
            best = t
    return best if best is not None else rows


def _params(sem):
    return pltpu.CompilerParams(dimension_semantics=sem, vmem_limit_bytes=VMEM_LIMIT)


def _rows(tm, w, col=0):
    return pl.BlockSpec((tm, w), lambda i: (i, col))


def _vec(w, col=0, rows=1):
    return pl.BlockSpec((rows, w), lambda i: (0, col))


MM_VMEM_BUDGET = 36 * 1024 * 1024


def _mm(a, b, *, name, ta=False, tb=False, out_dtype=F32, a_fn=None, bias=None, tm=1024, tn=1024, pieces=None,
        col_perm=None, fused=None):
    M = a.shape[1] if ta else a.shape[0]
    K = a.shape[0] if ta else a.shape[1]
    N = b.shape[0] if tb else b.shape[1]
    assert K == (b.shape[1] if tb else b.shape[0]), (a.shape, b.shape, ta, tb)
    if pieces == "cols":
        tm, tn = _tile(M // 2, tm), _tile(N // 4, tn)
        assert (M // 2) % tm == 0 and (N // 4) % tn == 0
    elif pieces == "rows":
        tm, tn = M // 4, _tile(N, tn)
    else:
        tm, tn = _tile(M, tm), _tile(N, tn)
    sa, sb, so = a.dtype.itemsize, b.dtype.itemsize, jnp.dtype(out_dtype).itemsize

    def fits(tk):
        return 2 * tk * (tm * sa + tn * sb) + tm * tn * (2 * so + 4) <= MM_VMEM_BUDGET

    tk = K
    while not fits(tk):
        smaller = _tile(K, tk - LANE)
        if smaller >= tk:
            break
        tk = smaller
    nk = K // tk
    dn = (((0 if ta else 1,), (1 if tb else 0,)), ((), ()))
    b_outer = nk == 1 and a.size * sa * (N // tn) < b.size * sb * (M // tm)

    n_extra = len(fused[1]) if fused else 0
    n_out = len(fused[2]) if fused else 1

    def body(*refs):
        a_ref, b_ref = refs[:2]
        bias_ref = refs[2] if bias is not None else None
        first = 3 if bias is not None else 2
        extra_refs = refs[first:first + n_extra]
        out_refs = refs[first + n_extra:first + n_extra + n_out]
        o_ref = out_refs[0]
        av = a_ref[...]
        if a_fn is not None:
            av = a_fn(av.astype(F32))
        part = lax.dot_general(av.astype(BF16), b_ref[...].astype(BF16), dn, preferred_element_type=F32)

        def finish(r):
            if bias is not None:
                r = r + bias_ref[...]
            if fused:
                for ref, tile in zip(out_refs, fused[0](r, *[e[...] for e in extra_refs])):
                    ref[...] = tile.astype(ref.dtype)
            elif pieces == "rows":
                o_ref[0] = r[:tm // 2].astype(o_ref.dtype)
                o_ref[1] = r[tm // 2:].astype(o_ref.dtype)
            else:
                o_ref[...] = r.astype(o_ref.dtype)

        if nk == 1:
            finish(part)
        else:
            acc_ref = refs[-1]
            k = pl.program_id(2)

            @pl.when(k == 0)
            def _():
                acc_ref[...] = part

            @pl.when(k > 0)
            def _():
                acc_ref[...] += part

            @pl.when(k == nk - 1)
            def _():
                finish(acc_ref[...])

    def ij(g0, g1):
        return (g1, g0) if b_outer else (g0, g1)

    def amap(g0, g1, k):
        i, _ = ij(g0, g1)
        return (k, i) if ta else (i, k)

    def bmap(g0, g1, k):
        _, j = ij(g0, g1)
        return (j, k) if tb else (k, j)

    in_specs = [pl.BlockSpec((tk, tm) if ta else (tm, tk), amap), pl.BlockSpec((tn, tk) if tb else (tk, tn), bmap)]
    args = [a, b]
    if bias is not None:
        in_specs.append(pl.BlockSpec((1, tn), lambda g0, g1, k: (0, ij(g0, g1)[1])))
        args.append(bias)
    grid = (N // tn, M // tm, nk) if b_outer else (M // tm, N // tn, nk)
    if pieces == "cols":
        ni, nj = M // 2 // tm, N // 4 // tn

        def omap(g0, g1, k):
            i, j = ij(g0, g1)
            j = col_perm(j) if col_perm else j
            return (i // ni, j // nj, i % ni, j % nj)

        out_spec = pl.BlockSpec((None, None, tm, tn), omap)
        out_shape = jax.ShapeDtypeStruct((2, 4, M // 2, N // 4), out_dtype)
    elif pieces == "rows":
        out_spec = pl.BlockSpec((2, None, tm // 2, tn), lambda g0, g1, k: (0, ij(g0, g1)[0], 0, ij(g0, g1)[1]))
        out_shape = jax.ShapeDtypeStruct((2, 4, tm // 2, N), out_dtype)
    else:
        out_spec = pl.BlockSpec((tm, tn), lambda g0, g1, k: ij(g0, g1))
        out_shape = jax.ShapeDtypeStruct((M, N), out_dtype)
    if fused:
        for arr, width in fused[1]:
            in_specs.append(pl.BlockSpec((tm, width), lambda g0, g1, k: ij(g0, g1)))
            args.append(arr)
        out_spec = [pl.BlockSpec((tm, width), lambda g0, g1, k: ij(g0, g1)) for _, width, _ in fused[2]]
        out_shape = [jax.ShapeDtypeStruct((M, cols), dt) for cols, _, dt in fused[2]]
    return pl.pallas_call(
        body, name=name, grid=grid, in_specs=in_specs, out_specs=out_spec, out_shape=out_shape,
        scratch_shapes=[pltpu.VMEM((tm, tn), F32)] if nk > 1 else [],
        compiler_params=_params(("parallel", "parallel", "arbitrary")),
    )(*args)


def _rms_rows(v):
    return lax.rsqrt(jnp.mean(v * v, axis=-1, keepdims=True) + EPS)


def _rmsmod(x, g, ada, sc_col, sh_col, *, name):
    S, D = x.shape
    tm = _tile(S, 256)

    def body(x_ref, g_ref, sc_ref, sh_ref, h_ref):
        xv = x_ref[...]
        h = (xv * _rms_rows(xv) * g_ref[...]) * (1.0 + sc_ref[...]) + sh_ref[...]
        h_ref[...] = h.astype(h_ref.dtype)

    return pl.pallas_call(
        body, name=name, grid=(S // tm,),
        in_specs=[_rows(tm, D), _vec(D), _vec(D, sc_col), _vec(D, sh_col)],
        out_specs=_rows(tm, D), out_shape=jax.ShapeDtypeStruct((S, D), BF16),
        compiler_params=_params(("parallel",)),
    )(x, g, ada, ada)


def _latent_norm(proj, g_q, g_kv, ql):
    S = proj.shape[0]
    tm = _tile(S, 512)

    def body(cq_ref, ckv_ref, gq_ref, gkv_ref, oq_ref, okv_ref):
        cq = cq_ref[...].astype(F32)
        oq_ref[...] = (cq * _rms_rows(cq) * gq_ref[...]).astype(BF16)
        ckv = ckv_ref[...].astype(F32)
        okv_ref[...] = (ckv * _rms_rows(ckv) * gkv_ref[...]).astype(BF16)

    return pl.pallas_call(
        body, name="latent_norm", grid=(S // tm,),
        in_specs=[_rows(tm, ql, 0), _rows(tm, ql, 1), _vec(ql), _vec(ql)],
        out_specs=[_rows(tm, ql), _rows(tm, ql)],
        out_shape=[jax.ShapeDtypeStruct((S, ql), BF16)] * 2,
        compiler_params=_params(("parallel",)),
    )(proj, proj, g_q, g_kv)


def _rope_fwd(y, c, s1, s2):
    return y * c + pltpu.roll(y, ROPE // 2, 1) * s1 + pltpu.roll(y, HEAD_PAD - ROPE // 2, 1) * s2


def _rope_bwd(d, c, s1, s2):
    return d * c + pltpu.roll(d * s1, HEAD_PAD - ROPE // 2, 1) + pltpu.roll(d * s2, ROPE // 2, 1)


def _head_rms(v):
    return lax.rsqrt(jnp.sum(v * v, axis=-1, keepdims=True) * (1.0 / QK_DIM) + EPS)


def _q_prep(q0, g_qh, tabs, nh):
    S = q0.shape[0]
    tm = _tile(S, 256)

    def body(q_ref, g_ref, c_ref, s1_ref, s2_ref, o_ref):
        c, s1, s2, g = c_ref[...], s1_ref[...], s2_ref[...], g_ref[...]
        for h in range(nh):
            sl = slice(h * HEAD_PAD, (h + 1) * HEAD_PAD)
            xs = q_ref[:, sl].astype(F32)
            o_ref[:, sl] = (_rope_fwd(xs * _head_rms(xs) * g, c, s1, s2) * (QK_DIM ** -0.5)).astype(BF16)

    w = nh * HEAD_PAD
    return pl.pallas_call(
        body, name="mla_q_prep", grid=(S // tm,),
        in_specs=[_rows(tm, w), _vec(HEAD_PAD)] + [_rows(tm, HEAD_PAD)] * 3,
        out_specs=_rows(tm, w), out_shape=jax.ShapeDtypeStruct((S, w), BF16),
        compiler_params=_params(("parallel",)),
    )(q0, g_qh, *tabs)


def _k_prep(kv0, proj, kpe_col, g_kh, tabs, nh):
    S = kv0.shape[0]
    tm = _tile(S, 256)

    def body(kv_ref, kpe_ref, g_ref, c_ref, s1_ref, s2_ref, o_ref):
        c, s1, s2, g = c_ref[...], s1_ref[...], s2_ref[...], g_ref[...]
        kpe = kpe_ref[...].astype(F32)
        for h in range(nh):
            k0 = jnp.concatenate([kv_ref[:, h * HEAD:(h + 1) * HEAD].astype(F32), kpe], axis=1)
            o_ref[:, h * HEAD_PAD:(h + 1) * HEAD_PAD] = _rope_fwd(k0 * _head_rms(k0) * g, c, s1, s2).astype(BF16)

    return pl.pallas_call(
        body, name="mla_k_prep", grid=(S // tm,),
        in_specs=[_rows(tm, nh * HEAD, 0), _rows(tm, LANE, kpe_col), _vec(HEAD_PAD)] + [_rows(tm, HEAD_PAD)] * 3,
        out_specs=_rows(tm, nh * HEAD_PAD), out_shape=jax.ShapeDtypeStruct((S, nh * HEAD_PAD), BF16),
        compiler_params=_params(("parallel",)),
    )(kv0, proj, g_kh, *tabs)


def _gate_merge(pa, pb, proj, gla_col, glb_col):
    S, D = pa.shape
    tm = _tile(S, 256)

    def body(pa_ref, pb_ref, ga_ref, gb_ref, o_ref):
        o_ref[...] = (jax.nn.sigmoid(ga_ref[...].astype(F32)) * pa_ref[...] + jax.nn.sigmoid(gb_ref[...].astype(F32)) * pb_ref[...]).astype(BF16)

    return pl.pallas_call(
        body, name="gate_merge", grid=(S // tm,),
        in_specs=[_rows(tm, D), _rows(tm, D), _rows(tm, D, gla_col), _rows(tm, D, glb_col)],
        out_specs=_rows(tm, D), out_shape=jax.ShapeDtypeStruct((S, D), BF16),
        compiler_params=_params(("parallel",)),
    )(pa, pb, proj, proj)


def _resid_rmsmod(x, o, g, ada, gt_col, sc_col, sh_col):
    S, D = x.shape
    tm = _tile(S, 256)

    def body(x_ref, o_ref, g_ref, gt_ref, sc_ref, sh_ref, x2_ref, h_ref):
        x2 = x_ref[...] + gt_ref[...] * o_ref[...]
        x2_ref[...] = x2
        h_ref[...] = ((x2 * _rms_rows(x2) * g_ref[...]) * (1.0 + sc_ref[...]) + sh_ref[...]).astype(BF16)

    return pl.pallas_call(
        body, name="resid_rmsmod2", grid=(S // tm,),
        in_specs=[_rows(tm, D), _rows(tm, D), _vec(D), _vec(D, gt_col), _vec(D, sc_col), _vec(D, sh_col)],
        out_specs=[_rows(tm, D), _rows(tm, D)],
        out_shape=[jax.ShapeDtypeStruct((S, D), F32), jax.ShapeDtypeStruct((S, D), BF16)],
        compiler_params=_params(("parallel",)),
    )(x, o, g, ada, ada, ada)


def _swiglu_tile(ib):
    def fn(r):
        pairs = r.shape[1] // (2 * ib)
        act = [jax.nn.silu(r[:, 2 * p * ib:(2 * p + 1) * ib]) * r[:, (2 * p + 1) * ib:(2 * p + 2) * ib] for p in range(pairs)]
        return r, jnp.concatenate(act, axis=1) if pairs > 1 else act[0]
    return fn


def _swiglu_bwd_tile(ib):
    def fn(d, ff):
        ff = ff.astype(F32)
        out = []
        for p in range(d.shape[1] // ib):
            dp = d[:, p * ib:(p + 1) * ib]
            g = ff[:, 2 * p * ib:(2 * p + 1) * ib]
            u = ff[:, (2 * p + 1) * ib:(2 * p + 2) * ib]
            sg = jax.nn.sigmoid(g)
            out += [dp * u * sg * (1.0 + g * (1.0 - sg)), dp * g * sg]
        return (jnp.concatenate(out, axis=1),)
    return fn


def _loss_head(x2, f, tgt, ada, gt_col):
    S, D = x2.shape
    tm = _tile(S, 256)

    def body(x2_ref, f_ref, t_ref, gt_ref, dy_ref, df_ref, red_ref, loss_ref):
        @pl.when(pl.program_id(0) == 0)
        def _():
            red_ref[...] = jnp.zeros_like(red_ref)
            loss_ref[...] = jnp.zeros_like(loss_ref)

        fv = f_ref[...]
        gt = gt_ref[...]
        err = x2_ref[...] + gt * fv - t_ref[...]
        dy = err * (1.0 / D)
        dy_ref[...] = dy
        df_ref[...] = (dy * gt).astype(BF16)
        red_ref[0:1, :] += jnp.sum(dy * fv, axis=0, keepdims=True)
        loss_ref[...] += (0.5 / D) * jnp.sum(err * err)

    return pl.pallas_call(
        body, name="loss_head", grid=(S // tm,),
        in_specs=[_rows(tm, D), _rows(tm, D), _rows(tm, D), _vec(D, gt_col)],
        out_specs=[_rows(tm, D), _rows(tm, D), _vec(D, rows=8), _vec(LANE, rows=8)],
        out_shape=[jax.ShapeDtypeStruct((S, D), F32), jax.ShapeDtypeStruct((S, D), BF16),
                   jax.ShapeDtypeStruct((8, D), F32), jax.ShapeDtypeStruct((8, LANE), F32)],
        compiler_params=_params(("arbitrary",)),
    )(x2, f, tgt, ada)


def _rmsmod2_bwd(dh2, x2, dy, o, g, ada, sc_col, gt_col):
    S, D = x2.shape
    tm = _tile(S, 256)

    def body(dh_ref, x2_ref, dy_ref, o_ref, g_ref, sc_ref, gt_ref, dx_ref, do_ref, red_ref):
        @pl.when(pl.program_id(0) == 0)
        def _():
            red_ref[...] = jnp.zeros_like(red_ref)

        dh = dh_ref[...]
        x2 = x2_ref[...]
        gv = g_ref[...]
        mod = 1.0 + sc_ref[...]
        r = _rms_rows(x2)
        xn = x2 * r
        t = dh * xn
        red_ref[0:1, :] += jnp.sum(dh, axis=0, keepdims=True)
        red_ref[1:2, :] += jnp.sum(t * gv, axis=0, keepdims=True)
        red_ref[2:3, :] += jnp.sum(t * mod, axis=0, keepdims=True)
        dxn = dh * gv * mod
        dx = dy_ref[...] + r * (dxn - xn * jnp.mean(dxn * xn, axis=-1, keepdims=True))
        dx_ref[...] = dx
        red_ref[3:4, :] += jnp.sum(dx * o_ref[...], axis=0, keepdims=True)
        do_ref[...] = (dx * gt_ref[...]).astype(BF16)

    return pl.pallas_call(
        body, name="rmsmod2_bwd", grid=(S // tm,),
        in_specs=[_rows(tm, D)] * 4 + [_vec(D), _vec(D, sc_col), _vec(D, gt_col)],
        out_specs=[_rows(tm, D), _rows(tm, D), _vec(D, rows=8)],
        out_shape=[jax.ShapeDtypeStruct((S, D), F32), jax.ShapeDtypeStruct((S, D), BF16),
                   jax.ShapeDtypeStruct((8, D), F32)],
        compiler_params=_params(("arbitrary",)),
    )(dh2, x2, dy, o, g, ada, ada)


def _rmsmod1_bwd(dh, x, dx2, g, ada, sc_col):
    S, D = x.shape
    tm = _tile(S, 256)

    def body(dh_ref, x_ref, dx2_ref, g_ref, sc_ref, gx_ref, red_ref):
        @pl.when(pl.program_id(0) == 0)
        def _():
            red_ref[...] = jnp.zeros_like(red_ref)

        dh = dh_ref[...]
        xv = x_ref[...]
        gv = g_ref[...]
        mod = 1.0 + sc_ref[...]
        r = _rms_rows(xv)
        xn = xv * r
        t = dh * xn
        red_ref[0:1, :] += jnp.sum(dh, axis=0, keepdims=True)
        red_ref[1:2, :] += jnp.sum(t * gv, axis=0, keepdims=True)
        red_ref[2:3, :] += jnp.sum(t * mod, axis=0, keepdims=True)
        dxn = dh * gv * mod
        gx_ref[...] = dx2_ref[...] + r * (dxn - xn * jnp.mean(dxn * xn, axis=-1, keepdims=True))

    return pl.pallas_call(
        body, name="rmsmod1_bwd", grid=(S // tm,),
        in_specs=[_rows(tm, D)] * 3 + [_vec(D), _vec(D, sc_col)],
        out_specs=[_rows(tm, D), _vec(D, rows=8)],
        out_shape=[jax.ShapeDtypeStruct((S, D), F32), jax.ShapeDtypeStruct((8, D), F32)],
        compiler_params=_params(("arbitrary",)),
    )(dh, x, dx2, g, ada)


def _gate_bwd(dm, pa, pb, proj, gla_col, glb_col):
    S, D = pa.shape
    tm = _tile(S, 256)

    def body(dm_ref, pa_ref, pb_ref, la_ref, lb_ref, dpa_ref, dpb_ref, dla_ref, dlb_ref):
        dm_ = dm_ref[...]
        ga = jax.nn.sigmoid(la_ref[...].astype(F32))
        gb = jax.nn.sigmoid(lb_ref[...].astype(F32))
        dpa_ref[...] = (dm_ * ga).astype(BF16)
        dpb_ref[...] = (dm_ * gb).astype(BF16)
        dla_ref[...] = (dm_ * pa_ref[...] * ga * (1.0 - ga)).astype(BF16)
        dlb_ref[...] = (dm_ * pb_ref[...] * gb * (1.0 - gb)).astype(BF16)

    return pl.pallas_call(
        body, name="gate_bwd", grid=(S // tm,),
        in_specs=[_rows(tm, D)] * 3 + [_rows(tm, D, gla_col), _rows(tm, D, glb_col)],
        out_specs=[_rows(tm, D)] * 4, out_shape=[jax.ShapeDtypeStruct((S, D), BF16)] * 4,
        compiler_params=_params(("parallel",)),
    )(dm, pa, pb, proj, proj)


def _q_prep_bwd(dq, q0, g_qh, tabs, nh):
    S = q0.shape[0]
    tm = _tile(S, 256)

    def body(dq_ref, q_ref, g_ref, c_ref, s1_ref, s2_ref, o_ref, red_ref):
        @pl.when(pl.program_id(0) == 0)
        def _():
            red_ref[...] = jnp.zeros_like(red_ref)

        c, s1, s2, g = c_ref[...], s1_ref[...], s2_ref[...], g_ref[...]
        dg = jnp.zeros((1, HEAD_PAD), F32)
        for h in range(nh):
            sl = slice(h * HEAD_PAD, (h + 1) * HEAD_PAD)
            d1 = _rope_bwd(dq_ref[:, sl], c, s1, s2)
            xs = q_ref[:, sl].astype(F32)
            r = _head_rms(xs)
            qn = xs * r
            dg = dg + jnp.sum(d1 * qn, axis=0, keepdims=True)
            dn = d1 * g
            o_ref[:, sl] = (r * (dn - qn * (jnp.sum(dn * qn, axis=-1, keepdims=True) * (1.0 / QK_DIM)))).astype(BF16)
        red_ref[0:1, :] += dg

    w = nh * HEAD_PAD
    return pl.pallas_call(
        body, name="mla_q_prep_bwd", grid=(S // tm,),
        in_specs=[_rows(tm, w), _rows(tm, w), _vec(HEAD_PAD)] + [_rows(tm, HEAD_PAD)] * 3,
        out_specs=[_rows(tm, w), _vec(HEAD_PAD, rows=8)],
        out_shape=[jax.ShapeDtypeStruct((S, w), BF16), jax.ShapeDtypeStruct((8, HEAD_PAD), F32)],
        compiler_params=_params(("arbitrary",)),
    )(dq, q0, g_qh, *tabs)


def _k_prep_bwd(dk, dv, kv0, proj, kpe_col, g_kh, tabs, nh):
    S = kv0.shape[0]
    tm = _tile(S, 256)
    wv = nh * HEAD

    def body(dk_ref, dv_ref, kv_ref, kpe_ref, g_ref, c_ref, s1_ref, s2_ref, o_ref, dpe_ref, red_ref):
        @pl.when(pl.program_id(0) == 0)
        def _():
            red_ref[...] = jnp.zeros_like(red_ref)

        c, s1, s2, g = c_ref[...], s1_ref[...], s2_ref[...], g_ref[...]
        kpe = kpe_ref[...].astype(F32)
        dg = jnp.zeros((1, HEAD_PAD), F32)
        dpe = jnp.zeros((tm, LANE), F32)
        for h in range(nh):
            d1 = _rope_bwd(dk_ref[:, h * HEAD_PAD:(h + 1) * HEAD_PAD], c, s1, s2)
            k0 = jnp.concatenate([kv_ref[:, h * HEAD:(h + 1) * HEAD].astype(F32), kpe], axis=1)
            r = _head_rms(k0)
            kn = k0 * r
            dg = dg + jnp.sum(d1 * kn, axis=0, keepdims=True)
            dn = d1 * g
            dk0 = r * (dn - kn * (jnp.sum(dn * kn, axis=-1, keepdims=True) * (1.0 / QK_DIM)))
            o_ref[:, h * HEAD:(h + 1) * HEAD] = dk0[:, :HEAD].astype(BF16)
            dpe = dpe + dk0[:, HEAD:]
        o_ref[:, wv:] = dv_ref[...].astype(BF16)
        dpe_ref[...] = dpe.astype(BF16)
        red_ref[0:1, :] += dg

    return pl.pallas_call(
        body, name="mla_k_prep_bwd", grid=(S // tm,),
        in_specs=[_rows(tm, nh * HEAD_PAD), _rows(tm, wv), _rows(tm, wv, 0), _rows(tm, LANE, kpe_col),
                  _vec(HEAD_PAD)] + [_rows(tm, HEAD_PAD)] * 3,
        out_specs=[_rows(tm, 2 * wv), _rows(tm, LANE), _vec(HEAD_PAD, rows=8)],
        out_shape=[jax.ShapeDtypeStruct((S, 2 * wv), BF16), jax.ShapeDtypeStruct((S, LANE), BF16),
                   jax.ShapeDtypeStruct((8, HEAD_PAD), F32)],
        compiler_params=_params(("arbitrary",)),
    )(dk, dv, kv0, proj, g_kh, *tabs)


def _latent_norm_bwd(dcqn, dckvn, proj, g_q, g_kv, ql):
    S = proj.shape[0]
    tm = _tile(S, 512)

    def body(dq_ref, dkv_ref, cq_ref, ckv_ref, gq_ref, gkv_ref, oq_ref, okv_ref, red_ref):
        @pl.when(pl.program_id(0) == 0)
        def _():
            red_ref[...] = jnp.zeros_like(red_ref)

        for row, (d_ref, c_ref, g_ref, o_ref) in enumerate(((dq_ref, cq_ref, gq_ref, oq_ref),
                                                            (dkv_ref, ckv_ref, gkv_ref, okv_ref))):
            d = d_ref[...]
            cv = c_ref[...].astype(F32)
            r = _rms_rows(cv)
            ch = cv * r
            red_ref[row:row + 1, :] += jnp.sum(d * ch, axis=0, keepdims=True)
            dn = d * g_ref[...]
            o_ref[...] = (r * (dn - ch * jnp.mean(dn * ch, axis=-1, keepdims=True))).astype(BF16)

    return pl.pallas_call(
        body, name="latent_norm_bwd", grid=(S // tm,),
        in_specs=[_rows(tm, ql), _rows(tm, ql), _rows(tm, ql, 0), _rows(tm, ql, 1), _vec(ql), _vec(ql)],
        out_specs=[_rows(tm, ql), _rows(tm, ql), _vec(ql, rows=8)],
        out_shape=[jax.ShapeDtypeStruct((S, ql), BF16)] * 2 + [jax.ShapeDtypeStruct((8, ql), F32)],
        compiler_params=_params(("arbitrary",)),
    )(dcqn, dckvn, proj, proj, g_q, g_kv)


NEG = -1e30
ATT_TILE = 512
SB_TILE = 512
SB_SUB = 128
_NT = (((1,), (1,)), ((), ()))
_TN = (((0,), (0,)), ((), ()))


def _dot(a, b, dn=(((1,), (0,)), ((), ()))):
    return lax.dot_general(a, b, dn, preferred_element_type=F32)


def _key_rows(kb, t):
    return pl.ds(pl.multiple_of(kb * t, t), t)


def _diag_mask(t, strict):
    r = lax.broadcasted_iota(jnp.int32, (t, t), 0)
    c = lax.broadcasted_iota(jnp.int32, (t, t), 1)
    return c < r if strict else c <= r


def _mla_fwd(q, k, kv0, nh):
    S = q.shape[0]
    t = _tile(S, ATT_TILE)

    def body(q_ref, k_ref, v_ref, o_ref, lse_ref):
        i = pl.program_id(1)
        qv = q_ref[...]

        def block(kb, carry, masked):
            m, l, acc = carry
            rows = _key_rows(kb, t)
            s = _dot(qv, k_ref[rows, :], _NT)
            if masked:
                s = jnp.where(_diag_mask(t, False), s, NEG)
            m_new = jnp.maximum(m, jnp.max(s, axis=-1, keepdims=True))
            alpha = jnp.exp(m - m_new)
            p = jnp.exp(s - m_new)
            l = alpha * l + jnp.sum(p, axis=-1, keepdims=True)
            acc = alpha * acc + _dot(p.astype(BF16), v_ref[rows, :].astype(BF16))
            return m_new, l, acc

        init = (jnp.full((t, 1), NEG, F32), jnp.zeros((t, 1), F32), jnp.zeros((t, HEAD), F32))
        carry = lax.fori_loop(0, i, lambda kb, c: block(kb, c, False), init)
        m, l, acc = block(i, carry, True)
        o_ref[...] = acc / l
        lse_ref[...] = m + jnp.log(l)

    return pl.pallas_call(
        body, name="mla_attn_fwd", grid=(nh, S // t),
        in_specs=[pl.BlockSpec((t, HEAD_PAD), lambda h, i: (i, h)),
                  pl.BlockSpec((S, HEAD_PAD), lambda h, i: (0, h)),
                  pl.BlockSpec((S, HEAD), lambda h, i: (0, nh + h))],
        out_specs=[pl.BlockSpec((t, HEAD), lambda h, i: (i, h)),
                   pl.BlockSpec((None, t, 1), lambda h, i: (h, i, 0))],
        out_shape=[jax.ShapeDtypeStruct((S, nh * HEAD), F32), jax.ShapeDtypeStruct((nh, S, 1), F32)],
        compiler_params=_params(("parallel", "arbitrary")),
    )(q, k, kv0)


def _mla_bwd(q, k, kv0, o, do, lse, nh):
    S = q.shape[0]
    t = _tile(S, ATT_TILE)
    scale = QK_DIM ** -0.5

    def body(q_ref, k_ref, v_ref, o_ref, do_ref, lse_ref, dq_ref, dk_ref, dv_ref):
        i = pl.program_id(1)

        @pl.when(i == 0)
        def _():
            dk_ref[...] = jnp.zeros_like(dk_ref)
            dv_ref[...] = jnp.zeros_like(dv_ref)

        qv = q_ref[...]
        dov = do_ref[...]
        delta = jnp.sum(dov * o_ref[...], axis=-1, keepdims=True)
        dob = dov.astype(BF16)
        lse = lse_ref[...]

        def block(kb, dq, masked):
            rows = _key_rows(kb, t)
            ks = k_ref[rows, :]
            vs = v_ref[rows, :].astype(BF16)
            p = jnp.exp(_dot(qv, ks, _NT) - lse)
            if masked:
                p = jnp.where(_diag_mask(t, False), p, 0.0)
            ds = (p * (_dot(dob, vs, _NT) - delta)).astype(BF16)
            dk_ref[rows, :] += _dot(ds, qv, _TN)
            dv_ref[rows, :] += _dot(p.astype(BF16), dob, _TN)
            return dq + _dot(ds, ks)

        dq = lax.fori_loop(0, i, lambda kb, c: block(kb, c, False), jnp.zeros((t, HEAD_PAD), F32))
        dq_ref[...] = block(i, dq, True) * scale

    return pl.pallas_call(
        body, name="mla_attn_bwd", grid=(nh, S // t),
        in_specs=[pl.BlockSpec((t, HEAD_PAD), lambda h, i: (i, h)),
                  pl.BlockSpec((S, HEAD_PAD), lambda h, i: (0, h)),
                  pl.BlockSpec((S, HEAD), lambda h, i: (0, nh + h)),
                  pl.BlockSpec((t, HEAD), lambda h, i: (i, h)),
                  pl.BlockSpec((t, HEAD), lambda h, i: (i, h)),
                  pl.BlockSpec((None, t, 1), lambda h, i: (h, i, 0))],
        out_specs=[pl.BlockSpec((t, HEAD_PAD), lambda h, i: (i, h)),
                   pl.BlockSpec((S, HEAD_PAD), lambda h, i: (0, h)),
                   pl.BlockSpec((S, HEAD), lambda h, i: (0, h))],
        out_shape=[jax.ShapeDtypeStruct((S, nh * HEAD_PAD), F32), jax.ShapeDtypeStruct((S, nh * HEAD_PAD), F32),
                   jax.ShapeDtypeStruct((S, nh * HEAD), F32)],
        compiler_params=_params(("parallel", "arbitrary")),
    )(q, k, kv0, o, do, lse)


def _tri(n, cmp):
    r = lax.broadcasted_iota(jnp.int32, (n, n), 0)
    c = lax.broadcasted_iota(jnp.int32, (n, n), 1)
    return jnp.where(cmp(r, c), 1.0, 0.0).astype(BF16)


def _sb_block(qv, ks, run, upper, t, masked):
    z = _dot(qv, ks, _NT)
    lb = jnp.minimum(z, 0.0) - jnp.log(1.0 + jnp.exp(-jnp.abs(z)))
    lom = lb - z
    mask = _diag_mask(t, True) if masked else None
    if masked:
        lom = jnp.where(mask, lom, 0.0)
    tails = []
    for sblk in reversed(range(t // SB_SUB)):
        part = lom[:, sblk * SB_SUB:(sblk + 1) * SB_SUB]
        tails.append(_dot(part.astype(BF16), upper) + run)
        run = run + jnp.sum(part, axis=-1, keepdims=True)
    a = jnp.exp(lb + jnp.concatenate(tails[::-1], axis=1))
    if masked:
        a = jnp.where(mask, a, 0.0)
    return a, lb, mask, run


def _sb_fwd(proj, q_col, k_col, v_col, nh):
    S = proj.shape[0]
    t = _tile(S, SB_TILE)
    assert S // t <= LANE
    scale = HEAD ** -0.5

    def body(q_ref, k_ref, v_ref, o_ref, runs_ref):
        i = pl.program_id(1)
        qv = (q_ref[...].astype(F32) * scale).astype(BF16)
        upper = _tri(SB_SUB, lambda j, s: j > s)
        lane = lax.broadcasted_iota(jnp.int32, (t, LANE), 1)

        def block(kb, carry, masked):
            run, acc, runs = carry
            runs = jnp.where(lane == kb, run, runs)
            rows = _key_rows(kb, t)
            a, _, _, run = _sb_block(qv, k_ref[rows, :].astype(BF16), run, upper, t, masked)
            return run, acc + _dot(a.astype(BF16), v_ref[rows, :].astype(BF16)), runs

        carry = block(i, (jnp.zeros((t, 1), F32), jnp.zeros((t, HEAD), F32), jnp.zeros((t, LANE), F32)), True)
        _, o_ref[...], runs_ref[...] = lax.fori_loop(0, i, lambda j, c: block(i - 1 - j, c, False), carry)

    return pl.pallas_call(
        body, name="sb_attn_fwd", grid=(nh, S // t),
        in_specs=[pl.BlockSpec((t, HEAD), lambda h, i: (i, q_col + h)),
                  pl.BlockSpec((S, HEAD), lambda h, i: (0, k_col + h)),
                  pl.BlockSpec((S, HEAD), lambda h, i: (0, v_col + h))],
        out_specs=[pl.BlockSpec((t, HEAD), lambda h, i: (i, h)), pl.BlockSpec((None, t, LANE), lambda h, i: (h, i, 0))],
        out_shape=[jax.ShapeDtypeStruct((S, nh * HEAD), F32), jax.ShapeDtypeStruct((nh, S, LANE), F32)],
        compiler_params=_params(("parallel", "arbitrary")),
    )(proj, proj, proj)


def _sb_bwd(proj, q_col, k_col, v_col, dy, runs, nh):
    S = proj.shape[0]
    t = _tile(S, SB_TILE)
    scale = HEAD ** -0.5

    def body(q_ref, k_ref, v_ref, dy_ref, runs_ref, dq_ref, dk_ref, dv_ref):
        i = pl.program_id(1)

        @pl.when(i == 0)
        def _():
            dk_ref[...] = jnp.zeros_like(dk_ref)
            dv_ref[...] = jnp.zeros_like(dv_ref)

        qv = (q_ref[...].astype(F32) * scale).astype(BF16)
        dyb = dy_ref[...].astype(BF16)
        runs_v = runs_ref[...]
        lane = lax.broadcasted_iota(jnp.int32, (t, LANE), 1)
        upper = _tri(SB_SUB, lambda j, s: j > s)
        before = _tri(SB_SUB, lambda s, j: s < j)

        def block(kb, carry, masked):
            prefix, dq = carry
            rows = _key_rows(kb, t)
            ks = k_ref[rows, :].astype(BF16)
            vs = v_ref[rows, :].astype(BF16)
            run = jnp.sum(jnp.where(lane == kb, runs_v, 0.0), axis=-1, keepdims=True)
            a, lb, mask, _ = _sb_block(qv, ks, run, upper, t, masked)
            dl = a * _dot(dyb, vs, _NT)
            lefts = []
            for sblk in range(t // SB_SUB):
                part = dl[:, sblk * SB_SUB:(sblk + 1) * SB_SUB]
                lefts.append(_dot(part.astype(BF16), before) + prefix)
                prefix = prefix + jnp.sum(part, axis=-1, keepdims=True)
            beta = jnp.exp(lb)
            dz = dl * (1.0 - beta) - beta * jnp.concatenate(lefts, axis=1)
            if masked:
                dz = jnp.where(mask, dz, 0.0)
            dz = dz.astype(BF16)
            dk_ref[rows, :] += _dot(dz, qv, _TN)
            dv_ref[rows, :] += _dot(a.astype(BF16), dyb, _TN)
            return prefix, dq + _dot(dz, ks)

        carry = lax.fori_loop(0, i, lambda kb, c: block(kb, c, False),
                              (jnp.zeros((t, 1), F32), jnp.zeros((t, HEAD), F32)))
        dq_ref[...] = block(i, carry, True)[1] * scale

    full = pl.BlockSpec((S, HEAD), lambda h, i: (0, h))
    tile = pl.BlockSpec((t, HEAD), lambda h, i: (i, h))
    return pl.pallas_call(
        body, name="sb_attn_bwd", grid=(nh, S // t),
        in_specs=[pl.BlockSpec((t, HEAD), lambda h, i: (i, q_col + h)),
                  pl.BlockSpec((S, HEAD), lambda h, i: (0, k_col + h)),
                  pl.BlockSpec((S, HEAD), lambda h, i: (0, v_col + h)), tile,
                  pl.BlockSpec((None, t, LANE), lambda h, i: (h, i, 0))],
        out_specs=[tile, full, full],
        out_shape=[jax.ShapeDtypeStruct((S, nh * HEAD), F32)] * 3,
        compiler_params=_params(("parallel", "arbitrary")),
    )(proj, proj, proj, dy, runs)


def _place():
    return lax.axis_index("x"), lax.axis_index("y"), lax.axis_index("c")


def _other_chips(x, y):
    return [(1 - x, y), (x, 1 - y), (1 - x, 1 - y)]


def _dev_index(p):
    return 4 * p[0] + 2 * p[1] + p[2]


def _gather_blocks(blocks, *, name, in_vmem):
    n = len(blocks)
    per = 7

    def body(*refs):
        ins, outs = refs[:n], refs[n:2 * n]
        send_sems, recv_sems, local_sems = refs[2 * n:]
        x, y, c = _place()
        me, sibling = (x, y, c), (x, y, 1 - c)
        chips = _other_chips(x, y)

        def slot(a, p):
            return outs[a].at[_dev_index(p)]

        def copy(a, k, block, to, src=None):
            return pltpu.make_async_remote_copy(
                src_ref=slot(a, block) if src is None else src, dst_ref=slot(a, block),
                send_sem=send_sems.at[a * per + k], recv_sem=recv_sems.at[a * per + k],
                device_id=to, device_id_type=MESH)

        mine = [pltpu.make_async_copy(ins[a], slot(a, me), local_sems.at[a]) for a in range(n)] if in_vmem else []
        for cp in mine:
            cp.start()
        first = []
        for a in range(n):
            first.append(copy(a, 0, me, sibling, src=ins[a]))
            first += [copy(a, 1 + j, me, (*chip, c), src=ins[a]) for j, chip in enumerate(chips)]
        for cp in first:
            cp.start()
        passed = []
        for a in range(n):
            for j, chip in enumerate(chips):
                copy(a, 1 + j, (*chip, c), me).wait_recv()
                cp = copy(a, 4 + j, (*chip, c), sibling)
                cp.start()
                passed.append(cp)
        for a in range(n):
            copy(a, 0, sibling, me).wait_recv()
            for j, chip in enumerate(chips):
                copy(a, 4 + j, (*chip, 1 - c), me).wait_recv()
        for cp in first + passed:
            cp.wait_send()
        for cp in mine:
            cp.wait()

    space = pltpu.VMEM if in_vmem else pl.ANY
    spec = pl.BlockSpec(memory_space=space)
    outs = pl.pallas_call(
        body, name=name, in_specs=[spec] * n, out_specs=[spec] * n,
        out_shape=[jax.ShapeDtypeStruct((N_DEV,) + b.shape, b.dtype) for b in blocks],
        scratch_shapes=[pltpu.SemaphoreType.DMA((n * per,)), pltpu.SemaphoreType.DMA((n * per,)),
                        pltpu.SemaphoreType.DMA((n,))],
        compiler_params=pltpu.CompilerParams(vmem_limit_bytes=VMEM_LIMIT),
    )(*blocks)
    return list(outs)


def _sibling_swap(arrs, *, name, whole=False):
    n = len(arrs)

    def body(*refs):
        ins, outs = refs[:n], refs[n:2 * n]
        send_sems, recv_sems = refs[2 * n:]
        x, y, c = _place()
        copies = [pltpu.make_async_remote_copy(
            src_ref=ins[a] if whole else ins[a].at[1 - c], dst_ref=outs[a],
            send_sem=send_sems.at[a], recv_sem=recv_sems.at[a],
            device_id=(x, y, 1 - c), device_id_type=MESH) for a in range(n)]
        for cp in copies:
            cp.start()
        for cp in copies:
            cp.wait()

    spec = pl.BlockSpec(memory_space=pl.ANY)
    return list(pl.pallas_call(
        body, name=name, in_specs=[spec] * n, out_specs=[spec] * n,
        out_shape=[jax.ShapeDtypeStruct(a.shape if whole else a.shape[1:], a.dtype) for a in arrs],
        scratch_shapes=[pltpu.SemaphoreType.DMA((n,)), pltpu.SemaphoreType.DMA((n,))],
    )(*arrs))


_HBM = pl.BlockSpec(memory_space=pltpu.HBM)
_SEM = pl.BlockSpec(memory_space=pltpu.SEMAPHORE)
_EFFECT = pltpu.SideEffectType.DATAFLOW_SIDE_EFFECTING


def _in_hbm(a):
    return pltpu.with_memory_space_constraint(a, pltpu.HBM)


def _split_copies(srcs, lands, send_sems, recv_sems, plan):
    x, y, c = _place()
    copies = []
    for a, (src, land) in enumerate(zip(srcs, lands)):
        steps = plan(x, y, c)
        for k, (pick, slot, to) in enumerate(steps):
            copies.append(pltpu.make_async_remote_copy(
                src_ref=pick(src), dst_ref=slot(land), send_sem=send_sems.at[a * len(steps) + k],
                recv_sem=recv_sems.at[a * len(steps) + k], device_id=to, device_id_type=MESH))
    return copies


def _split_start(srcs, land_shapes, plan, per, *, name):
    n = len(srcs)

    def body(*refs):
        send_sems, recv_sems = refs[2 * n], refs[2 * n + 1]
        for cp in _split_copies(refs[:n], refs[n:2 * n], send_sems, recv_sems, plan):
            cp.start()
        token = refs[-1]
        token[...] = jnp.zeros_like(token)

    lands = [_in_hbm(lax.empty(s.shape, s.dtype)) for s in land_shapes]
    outs = pl.pallas_call(
        body, name=name,
        out_shape=(pltpu.SemaphoreType.DMA((n * per,)), pltpu.SemaphoreType.DMA((n * per,)),
                   *[pltpu.HBM(s.shape, s.dtype) for s in srcs], *[pltpu.HBM(s.shape, s.dtype) for s in land_shapes],
                   jax.ShapeDtypeStruct((8, LANE), F32)),
        in_specs=[_HBM] * (2 * n),
        out_specs=(_SEM, _SEM, *[_HBM] * (2 * n), pl.BlockSpec(memory_space=pltpu.VMEM)),
        input_output_aliases={i: 2 + i for i in range(2 * n)},
        compiler_params=pltpu.CompilerParams(has_side_effects=_EFFECT),
    )(*[_in_hbm(s) for s in srcs], *lands)
    return outs[0], outs[1], list(outs[2:2 + n]), list(outs[2 + n:2 + 2 * n]), outs[-1]


def _split_wait(send_sems, recv_sems, srcs, lands, after, plan, *, name):
    n = len(srcs)

    def body(*refs):
        for cp in _split_copies(refs[:n], refs[n:2 * n], refs[2 * n], refs[2 * n + 1], plan):
            cp.wait_send()
            cp.wait_recv()

    outs = pl.pallas_call(
        body, name=name,
        out_shape=(*[pltpu.HBM(s.shape, s.dtype) for s in srcs], *[pltpu.HBM(s.shape, s.dtype) for s in lands]),
        in_specs=[_HBM] * (2 * n) + [_SEM, _SEM, pl.BlockSpec(memory_space=pl.ANY)],
        out_specs=tuple([_HBM] * (2 * n)),
        input_output_aliases={i: i for i in range(2 * n)},
        compiler_params=pltpu.CompilerParams(has_side_effects=_EFFECT),
    )(*srcs, *lands, send_sems, recv_sems, after)
    return list(outs[:n]), list(outs[n:])


def _gather_plan(x, y, c):
    slot = lambda land: land.at[_dev_index((x, y, c))]
    whole = lambda src: src
    return [(whole, slot, (x, y, 1 - c))] + [(whole, slot, (px, py, c)) for px, py in _other_chips(x, y)]


def _swap_plan(x, y, c):
    return [(lambda src: src.at[1 - c], lambda land: land, (x, y, 1 - c))]


def _send_plan(x, y, c):
    return [(lambda src: src, lambda land: land, (x, y, 1 - c))]


def _exchange_plan(x, y, c):
    return [(lambda src, k=2 * px + py: src.at[k], lambda land, j=j: land.at[j], (px, py, c))
            for j, (px, py) in enumerate(_other_chips(x, y))]


def _gather_forward(lands, *, name):
    n = len(lands)

    def body(*refs):
        lands_in, outs = refs[:n], refs[n:2 * n]
        send_sems, recv_sems = refs[2 * n:]
        x, y, c = _place()
        copies = []
        for a in range(n):
            for j, (px, py) in enumerate(_other_chips(x, y)):
                copies.append((pltpu.make_async_remote_copy(
                    src_ref=lands_in[a].at[_dev_index((px, py, c))], dst_ref=outs[a].at[_dev_index((px, py, c))],
                    send_sem=send_sems.at[3 * a + j], recv_sem=recv_sems.at[3 * a + j],
                    device_id=(x, y, 1 - c), device_id_type=MESH), a, j, (px, py)))
        for cp, _, _, _ in copies:
            cp.start()
        for cp, a, j, (px, py) in copies:
            cp.wait_send()
            pltpu.make_async_remote_copy(
                src_ref=lands_in[a].at[_dev_index((px, py, 1 - c))], dst_ref=outs[a].at[_dev_index((px, py, 1 - c))],
                send_sem=send_sems.at[3 * a + j], recv_sem=recv_sems.at[3 * a + j],
                device_id=(x, y, 1 - c), device_id_type=MESH).wait_recv()

    spec = pl.BlockSpec(memory_space=pl.ANY)
    return list(pl.pallas_call(
        body, name=name, in_specs=[spec] * n, out_specs=[spec] * n,
        out_shape=[jax.ShapeDtypeStruct(a.shape, a.dtype) for a in lands],
        input_output_aliases={a: a for a in range(n)},
        scratch_shapes=[pltpu.SemaphoreType.DMA((3 * n,)), pltpu.SemaphoreType.DMA((3 * n,))],
    )(*lands))


def _flat2(a, lead):
    return a.reshape(a.shape[:lead] + (-1, a.shape[-1]))


def _pair_sum(g, recv, c_idx, *, name):
    _, nchip, r, w = g.shape
    tm = _tile(r, 256) if r % 8 == 0 else r

    def body(c_ref, g_ref, r_ref, o_ref):
        o_ref[...] = (g_ref[...].astype(F32) + r_ref[...].astype(F32)).astype(o_ref.dtype)

    return pl.pallas_call(
        body, name=name,
        grid_spec=pltpu.PrefetchScalarGridSpec(
            num_scalar_prefetch=1, grid=(nchip, r // tm),
            in_specs=[pl.BlockSpec((None, None, tm, w), lambda k, i, c_ref: (c_ref[0], k, i, 0)),
                      pl.BlockSpec((None, tm, w), lambda k, i, c_ref: (k, i, 0))],
            out_specs=pl.BlockSpec((None, tm, w), lambda k, i, c_ref: (k, i, 0))),
        out_shape=jax.ShapeDtypeStruct((nchip, r, w), BF16),
        compiler_params=_params(("parallel", "parallel")),
    )(c_idx, g, recv)


def _chip_sum(s1, recv, chip_idx, *, name):
    _, r, w = s1.shape
    tm = _tile(r, 256) if r % 8 == 0 else r

    def body(k_ref, s_ref, r_ref, o_ref):
        acc = s_ref[...].astype(F32)
        for j in range(3):
            acc = acc + r_ref[j].astype(F32)
        o_ref[...] = acc

    return pl.pallas_call(
        body, name=name,
        grid_spec=pltpu.PrefetchScalarGridSpec(
            num_scalar_prefetch=1, grid=(r // tm,),
            in_specs=[pl.BlockSpec((None, tm, w), lambda i, k_ref: (k_ref[0], i, 0)),
                      pl.BlockSpec((3, tm, w), lambda i, k_ref: (0, i, 0))],
            out_specs=pl.BlockSpec((tm, w), lambda i, k_ref: (i, 0))),
        out_shape=jax.ShapeDtypeStruct((r, w), F32),
        compiler_params=_params(("parallel",)),
    )(chip_idx, s1, recv)


def _adam_math(w, g, m, v):
    m = ADAM_B1 * m + (1.0 - ADAM_B1) * g
    v = ADAM_B2 * v + (1.0 - ADAM_B2) * (g * g)
    m_hat = m / (1.0 - ADAM_B1 ** ADAM_STEP)
    v_hat = v / (1.0 - ADAM_B2 ** ADAM_STEP)
    delta = -ADAM_LR * (m_hat / (jnp.sqrt(v_hat) + ADAM_EPS) + ADAM_WD * w)
    return delta, m, v


def _adamw(w, mine, other, c_idx, m, v, *, name):
    r, cw = w.shape
    hr = r // 2
    tm = _row_tile(hr, 9 * cw * 4)

    def body(c_ref, w_ref, a_ref, b_ref, m_ref, v_ref, g_ref, d_ref, nm_ref, nv_ref):
        g = jnp.where(pl.program_id(0) == c_ref[0], a_ref[...], b_ref[...])
        g_ref[...] = g
        d_ref[...], nm_ref[...], nv_ref[...] = _adam_math(w_ref[...], g, m_ref[...], v_ref[...])

    per_half = hr // tm
    full = pl.BlockSpec((tm, cw), lambda h, i, c_ref: (h * per_half + i, 0))
    half = pl.BlockSpec((tm, cw), lambda h, i, c_ref: (i, 0))
    return pl.pallas_call(
        body, name=name,
        grid_spec=pltpu.PrefetchScalarGridSpec(
            num_scalar_prefetch=1, grid=(2, per_half),
            in_specs=[full, half, half, full, full], out_specs=[full] * 4),
        out_shape=[jax.ShapeDtypeStruct((r, cw), F32)] * 4,
        compiler_params=_params(("parallel", "parallel")),
    )(c_idx, w, mine, other, m, v)


def _adamw_ada(cact_t, dada, w, m, v):
    r, cw = w.shape
    nb = cact_t.shape[1]
    tm = _tile(r, 256)
    tn = _tile(cw, 1024)

    def body(a_ref, d_ref, w_ref, m_ref, v_ref, g_ref, dl_ref, nm_ref, nv_ref):
        a = a_ref[...]
        d = d_ref[...]
        g = a[:, 0:1] * d[0:1, :]
        for b in range(1, nb):
            g = g + a[:, b:b + 1] * d[b:b + 1, :]
        g_ref[...] = g
        dl_ref[...], nm_ref[...], nv_ref[...] = _adam_math(w_ref[...], g, m_ref[...], v_ref[...])

    blk = pl.BlockSpec((tm, tn), lambda i, j: (i, j))
    return pl.pallas_call(
        body, name="adamw_ada", grid=(r // tm, cw // tn),
        in_specs=[pl.BlockSpec((tm, nb), lambda i, j: (i, 0)), pl.BlockSpec((nb, tn), lambda i, j: (0, j)), blk, blk, blk],
        out_specs=[blk] * 4, out_shape=[jax.ShapeDtypeStruct((r, cw), F32)] * 4,
        compiler_params=_params(("parallel", "parallel")),
    )(cact_t, dada, w, m, v)


def _adamw_vec(parts, w, m, v):
    n = w.shape[1]

    def body(p_ref, w_ref, m_ref, v_ref, g_ref, d_ref, nm_ref, nv_ref):
        p = p_ref[...]
        g = p[0:1, :]
        for b in range(1, N_DEV):
            g = g + p[b:b + 1, :]
        g_ref[...] = g
        d_ref[...], nm_ref[...], nv_ref[...] = _adam_math(w_ref[...], g, m_ref[...], v_ref[...])

    return pl.pallas_call(
        body, name="adamw_vec", out_shape=[jax.ShapeDtypeStruct((1, n), F32)] * 4,
        compiler_params=pltpu.CompilerParams(vmem_limit_bytes=VMEM_LIMIT),
    )(parts, w, m, v)


def _w_in_segments(kpe0, d_in, cs):
    segs = []
    for k in range(4):
        lo, hi = k * cs, (k + 1) * cs
        for a, b, shift in ((0, kpe0, 0), (kpe0, kpe0 + ROPE, d_in - ROPE - kpe0), (kpe0 + ROPE, d_in, -ROPE)):
            a, b = max(lo, a), min(hi, b)
            if a < b:
                segs.append((k, a - lo, a + shift, b - a))
    return segs


def _w_in_layout(g8, kpe0):
    _, hr, cs = g8.shape
    rows, d_in = 2 * hr, 4 * cs
    segs = _w_in_segments(kpe0, d_in, cs)
    tm = _tile(rows, 256)

    def body(g_ref, o_ref):
        for k, src, dst, w in segs:
            o_ref[:, dst:dst + w] = g_ref[k, :, src:src + w]
        o_ref[:, d_in:] = jnp.zeros((tm, ROPE), o_ref.dtype)

    return pl.pallas_call(
        body, name="w_in_layout", grid=(rows // tm,),
        in_specs=[pl.BlockSpec((4, tm, cs), lambda i: (0, i, 0))], out_specs=_rows(tm, d_in + ROPE),
        out_shape=jax.ShapeDtypeStruct((rows, d_in + ROPE), g8.dtype), compiler_params=_params(("parallel",)),
    )(g8.reshape(4, rows, cs))


def _w_in_grad_pieces(g, kpe0):
    rows, d_in_p = g.shape
    d_in = d_in_p - ROPE
    cs = d_in // 4
    segs = _w_in_segments(kpe0, d_in, cs)
    hr = rows // 2
    tm = _tile(hr, 256)
    per_half = hr // tm

    def body(g_ref, o_ref):
        for k, src, dst, w in segs:
            o_ref[k, :, src:src + w] = g_ref[:, dst:dst + w]

    return pl.pallas_call(
        body, name="w_in_grad_pieces", grid=(rows // tm,),
        in_specs=[_rows(tm, d_in_p)],
        out_specs=pl.BlockSpec((None, 4, tm, cs), lambda i: (i // per_half, 0, i % per_half, 0)),
        out_shape=jax.ShapeDtypeStruct((2, 4, hr, cs), g.dtype), compiler_params=_params(("parallel",)),
    )(g)


def _interleave_layout(g8, ib):
    _, hr, cs = g8.shape
    rows, per_chip, per_half = 2 * hr, cs // ib, 2 * cs // ib
    tm = _tile(rows, 2048)

    def src(jj):
        return jj // 2 + per_half * (jj % 2)

    def body(g_ref, o_ref):
        o_ref[...] = g_ref[...]

    return pl.pallas_call(
        body, name="interleave_layout", grid=(rows // tm, 4 * per_chip),
        in_specs=[pl.BlockSpec((None, tm, ib), lambda i, jj: (src(jj) // per_chip, i, src(jj) % per_chip))],
        out_specs=pl.BlockSpec((tm, ib), lambda i, jj: (i, jj)),
        out_shape=jax.ShapeDtypeStruct((rows, 4 * cs), g8.dtype), compiler_params=_params(("parallel", "parallel")),
    )(g8.reshape(4, rows, cs))


def _cols_from_chips(g8, rows):
    cs = g8.shape[-1]
    return g8.reshape(4, rows, cs).transpose(1, 0, 2).reshape(rows, 4 * cs)


def _cols_to_pieces(g):
    rows, c4 = g.shape
    return g.reshape(2, rows // 2, 4, c4 // 4).transpose(0, 2, 1, 3)


def _rows_to_pieces(g):
    r4, cols = g.shape
    return g.reshape(4, 2, r4 // 8, cols).transpose(1, 0, 2, 3)


def _pad_cols(a, w):
    return jnp.pad(a, ((0, 0), (0, w - a.shape[1])))


def kernel(x, c, positions, w_ada, b_ada, g_norm1, g_norm2, w_in, g_q_latent, g_kv_latent, w_uq, w_ukv, g_q_head, g_k_head, w_proj_mla, w_proj_sb, w_out, w_ffn_in, w_ffn_out, loss_target, m_w_ada, m_b_ada, m_g_norm1, m_g_norm2, m_w_in, m_g_q_latent, m_g_kv_latent, m_w_uq, m_w_ukv, m_g_q_head, m_g_k_head, m_w_proj_mla, m_w_proj_sb, m_w_out, m_w_ffn_in, m_w_ffn_out, v_w_ada, v_b_ada, v_g_norm1, v_g_norm2, v_w_in, v_g_q_latent, v_g_kv_latent, v_w_uq, v_w_ukv, v_g_q_head, v_g_k_head, v_w_proj_mla, v_w_proj_sb, v_w_out, v_w_ffn_in, v_w_ffn_out):
    xi, yi, ci = _place()
    chip = 2 * xi + yi
    dev = 2 * chip + ci
    c_idx = jnp.reshape(ci, (1,)).astype(jnp.int32)
    chip_idx = jnp.reshape(chip, (1,)).astype(jnp.int32)

    x = x[0]
    tgt = loss_target[0]
    S, D = x.shape
    ql = g_q_latent.shape[1]
    assert g_kv_latent.shape[1] == ql
    mlaw = w_proj_mla.shape[1]
    nh = mlaw // HEAD
    sbw = w_proj_sb.shape[1]
    assert sbw == mlaw
    dff = w_ffn_out.shape[1] * 4
    d_in = 2 * ql + ROPE + 3 * sbw + 2 * D
    d_in_p = d_in + ROPE
    q_col = (2 * ql) // HEAD
    k_col = q_col + nh
    v_col = k_col + nh
    gla_col = (2 * ql + 3 * sbw) // D
    glb_col = gla_col + 1
    kpe_col = (d_in - ROPE) // LANE
    assert (2 * ql + 3 * sbw) % D == 0 and (d_in - ROPE) % LANE == 0

    mats = {"w_in": w_in[0], "w_uq": w_uq[0], "w_ukv": w_ukv[0], "w_proj_mla": w_proj_mla[0],
            "w_proj_sb": w_proj_sb[0], "w_out": w_out[0], "w_ffn_in": w_ffn_in[0], "w_ffn_out": w_ffn_out[0]}
    names = list(mats)
    row_sharded = {"w_out", "w_ffn_out"}

    c_all = _gather_blocks([jnp.broadcast_to(c, (8, D))], name="gather_cond", in_vmem=True)[0][:, 0, :]
    n_ada = w_ada.shape[2]
    b_shard = lax.dynamic_slice_in_dim(b_ada, chip * n_ada, n_ada, axis=1)
    ada_shard = _mm(c_all, w_ada[0], name="ada_proj", a_fn=jax.nn.silu, bias=b_shard)
    ada_all = _gather_blocks([ada_shard], name="gather_ada", in_vmem=True)[0]
    ada_rows = lax.dynamic_index_in_dim(ada_all, dev, axis=1, keepdims=False)
    ada = ada_rows[0::2].reshape(1, 4 * n_ada)
    SH1, SC1, GT1, SH2, SC2, GT2 = range(6)

    def after(dep, a):
        return a + (dep.reshape(-1)[0:1].reshape((1,) * a.ndim) * 0).astype(a.dtype)

    def fill_own(g8, own):
        return lax.dynamic_update_index_in_dim(g8, own, dev, 0)

    halves = []
    for nm in names:
        w = mats[nm]
        hr = w.shape[0] // 2
        halves.append(lax.dynamic_slice_in_dim(w, ci * hr, hr, axis=0).astype(BF16))
    half_of = dict(zip(names, halves))
    early = ["w_in", "w_uq", "w_ukv"]
    late = ["w_proj_mla", "w_proj_sb", "w_out", "w_ffn_in", "w_ffn_out"]
    early_halves = [half_of[nm] for nm in early]
    early_halves[0] = after(ada, early_halves[0])
    early_got = _gather_blocks(early_halves, name="gather_weights", in_vmem=False)
    gathered = {nm: fill_own(g8, own) for nm, g8, own in zip(early, early_got, early_halves)}
    late_halves = [half_of[nm] for nm in late]
    late_halves[0] = after(gathered[early[1]], late_halves[0])
    late_send, late_recv, late_srcs, late_lands, late_token = _split_start(
        late_halves, [jax.ShapeDtypeStruct((N_DEV,) + h.shape, h.dtype) for h in late_halves], _gather_plan, 4,
        name="gather_late_start")
    ada = ada + late_token[0:1, 0:1]

    def full_cols(nm):
        return _cols_from_chips(gathered[nm], mats[nm].shape[0])

    kpe0 = 2 * ql
    w_in_p = _w_in_layout(gathered["w_in"], kpe0)
    w_uq_p = jnp.pad(full_cols("w_uq").reshape(ql, nh, QK_DIM), ((0, 0), (0, 0), (0, HEAD_PAD - QK_DIM))
                     ).reshape(ql, nh * HEAD_PAD)
    w_ukv4 = full_cols("w_ukv").reshape(ql, nh, 2 * HEAD)
    w_ukv_p = jnp.concatenate([w_ukv4[:, :, :HEAD].reshape(ql, mlaw), w_ukv4[:, :, HEAD:].reshape(ql, mlaw)], axis=1)

    half = ROPE // 2
    freqs = ROPE_THETA ** (-jnp.arange(half, dtype=F32) / half)
    ang = positions[0].astype(F32)[:, None] * freqs
    cos, sin = jnp.cos(ang), jnp.sin(ang)
    one = jnp.ones((S, NOPE), F32)
    zero = jnp.zeros((S, NOPE), F32)
    zh = jnp.zeros((S, half), F32)
    tabs = (jnp.concatenate([one, cos, cos, one[:, :HEAD_PAD - QK_DIM]], axis=1),
            jnp.concatenate([zero, zh, sin, zero[:, :HEAD_PAD - QK_DIM]], axis=1),
            jnp.concatenate([zero, -sin, zh, zero[:, :HEAD_PAD - QK_DIM]], axis=1))
    g_qh_p = _pad_cols(g_q_head, HEAD_PAD)
    g_kh_p = _pad_cols(g_k_head, HEAD_PAD)

    h1 = _rmsmod(x, g_norm1, ada, SC1, SH1, name="rmsmod1")
    proj = _mm(h1, w_in_p, name="mm_proj", tn=640, out_dtype=BF16)
    cqn, ckvn = _latent_norm(proj, g_q_latent, g_kv_latent, ql)
    q0 = _mm(cqn, w_uq_p, name="mm_q_up", out_dtype=BF16)
    kv0 = _mm(ckvn, w_ukv_p, name="mm_kv_up", out_dtype=BF16)
    q = _q_prep(q0, g_qh_p, tabs, nh)
    k = _k_prep(kv0, proj, kpe_col, g_kh_p, tabs, nh)
    y_a, lse = _mla_fwd(q, k, kv0, nh)
    y_b, sb_runs = _sb_fwd(proj, q_col, k_col, v_col, nh)
    late_srcs, late_lands = _split_wait(late_send, late_recv, late_srcs, late_lands, y_b, _gather_plan,
                                        name="gather_late_wait")
    late_got = _gather_forward(late_lands, name="gather_late_forward")
    gathered.update({nm: fill_own(g8, own) for nm, g8, own in zip(late, late_got, late_srcs)})
    w_pm = full_cols("w_proj_mla")
    w_ps = full_cols("w_proj_sb")
    w_o = gathered["w_out"].reshape(D, D)
    ib = 256 if (dff // 2) % 256 == 0 else LANE
    nb = dff // ib
    w_fi = _interleave_layout(gathered["w_ffn_in"], ib)
    w_fo = gathered["w_ffn_out"].reshape(dff, D)
    pa = _mm(y_a, w_pm, name="mm_proj_mla", out_dtype=BF16)
    pb = _mm(y_b, w_ps, name="mm_proj_sb", out_dtype=BF16)
    merged = _gate_merge(pa, pb, proj, gla_col, glb_col)
    o = _mm(merged, w_o, name="mm_out")
    x2, h2 = _resid_rmsmod(x, o, g_norm2, ada, GT1, SC2, SH2)
    ff, act = _mm(h2, w_fi, name="mm_ffn_in", tn=4 * ib,
                  fused=(_swiglu_tile(ib), [], [(2 * dff, 4 * ib, BF16), (dff, 2 * ib, BF16)]))
    f = _mm(act, w_fo, name="mm_ffn_out")
    dy, df, red_l, loss_p = _loss_head(x2, f, tgt, ada, GT2)

    dff_, = _mm(df, w_fo, name="mm_d_act", tb=True, tn=2 * ib,
                fused=(_swiglu_bwd_tile(ib), [(ff, 4 * ib)], [(2 * dff, 4 * ib, BF16)]))
    def pc(kind):
        if kind == "cols":
            return kind
        return kind if (D // 4) % LANE == 0 and (dff // 4) % LANE == 0 else None

    gw_fo = _mm(act, df, name="mm_gw_ffn_out", ta=True, out_dtype=BF16, pieces=pc("rows"))
    dh2 = _mm(dff_, w_fi, name="mm_d_h2", tb=True)
    gw_fi = _mm(h2, dff_, name="mm_gw_ffn_in", ta=True, out_dtype=BF16, pieces="cols", tn=ib,
                col_perm=lambda jj: jj // 2 + nb * (jj % 2))

    def to_pieces(nms, grads):
        return [g if g.ndim == 4 else (_rows_to_pieces if nm in row_sharded else _cols_to_pieces)(g)
                for nm, g in zip(nms, grads)]

    def pair_sums(nms, pcs, got):
        return [_pair_sum(p, r, c_idx, name="rs_pair_sum_" + nm) for p, r, nm in zip(pcs, got, nms)]

    def swap_start(pcs, tag):
        return _split_start(pcs, [jax.ShapeDtypeStruct(p.shape[1:], p.dtype) for p in pcs], _swap_plan, 1,
                            name="rs_swap_%s_start" % tag)

    def exchange_start(pair, tag):
        return _split_start(pair, [jax.ShapeDtypeStruct((3,) + p.shape[1:], p.dtype) for p in pair], _exchange_plan, 3,
                            name="rs_exchange_%s_start" % tag)

    ffn = ["w_ffn_in", "w_ffn_out"]
    sw = swap_start(to_pieces(ffn, [gw_fi, gw_fo]), "ffn")
    ada = ada + sw[4][0:1, 0:1]
    dx2, do, red_2 = _rmsmod2_bwd(dh2, x2, dy, o, g_norm2, ada, SC2, GT1)
    ffn_pcs, ffn_got = _split_wait(sw[0], sw[1], sw[2], sw[3], dx2, _swap_plan, name="rs_swap_ffn_wait")
    ffn_send, ffn_recv, ffn_pair, ffn_lands, ffn_token = exchange_start(pair_sums(ffn, ffn_pcs, ffn_got), "ffn")
    dmerged = _mm(do, w_o, name="mm_d_merged", tb=True, out_dtype=BF16,
                  bias=jnp.zeros((1, D), F32) + ffn_token[0:1, 0:1])
    gw_o = _mm(merged, do, name="mm_gw_out", ta=True, out_dtype=BF16, pieces=pc("rows"))
    dpa, dpb, dgla, dglb = _gate_bwd(dmerged, pa, pb, proj, gla_col, glb_col)
    gw_pm = _mm(y_a, dpa, name="mm_gw_proj_mla", ta=True, out_dtype=BF16, pieces=pc("cols"))
    gw_ps = _mm(y_b, dpb, name="mm_gw_proj_sb", ta=True, out_dtype=BF16, pieces=pc("cols"))
    mid = ["w_proj_mla", "w_proj_sb", "w_out"]
    mid_pcs = to_pieces(mid, [gw_pm, gw_ps, gw_o])
    mid_pair = pair_sums(mid, mid_pcs, _sibling_swap(mid_pcs, name="rs_sibling_swap_mid"))
    mid_send, mid_recv, mid_pair, mid_lands, mid_token = exchange_start(mid_pair, "mid")
    behind_mid = jnp.zeros((1, mlaw), F32) + mid_token[0:1, 0:1]
    dya = _mm(dpa, w_pm, name="mm_d_ya", tb=True, out_dtype=BF16, bias=behind_mid)
    dyb = _mm(dpb, w_ps, name="mm_d_yb", tb=True, out_dtype=BF16, bias=behind_mid)
    dq, dk, dv = _mla_bwd(q, k, kv0, y_a, dya, lse, nh)
    dq_sb, dk_sb, dv_sb = _sb_bwd(proj, q_col, k_col, v_col, dyb, sb_runs, nh)
    dq0, red_qh = _q_prep_bwd(dq, q0, g_qh_p, tabs, nh)
    dkv0, dkpe, red_kh = _k_prep_bwd(dk, dv, kv0, proj, kpe_col, g_kh_p, tabs, nh)
    dcqn = _mm(dq0, w_uq_p, name="mm_d_cqn", tb=True, out_dtype=BF16)
    gw_uq_p = _mm(cqn, dq0, name="mm_gw_uq", ta=True, out_dtype=BF16)
    dckvn = _mm(dkv0, w_ukv_p, name="mm_d_ckvn", tb=True, out_dtype=BF16)
    gw_ukv_p = _mm(ckvn, dkv0, name="mm_gw_ukv", ta=True, out_dtype=BF16)
    dcq, dckv, red_lat = _latent_norm_bwd(dcqn, dckvn, proj, g_q_latent, g_kv_latent, ql)
    dproj = jnp.concatenate([dcq, dckv, dq_sb.astype(BF16), dk_sb.astype(BF16), dv_sb.astype(BF16),
                             dgla, dglb, dkpe], axis=1)
    gw_in_p = _mm(h1, dproj, name="mm_gw_in", ta=True, out_dtype=BF16, tn=640)

    gw_in = _w_in_grad_pieces(gw_in_p, kpe0)
    gw_uq = gw_uq_p.reshape(ql, nh, HEAD_PAD)[:, :, :QK_DIM].reshape(ql, nh * QK_DIM)
    gw_ukv = jnp.concatenate([gw_ukv_p[:, :mlaw].reshape(ql, nh, HEAD), gw_ukv_p[:, mlaw:].reshape(ql, nh, HEAD)],
                             axis=2).reshape(ql, 2 * mlaw)
    last = ["w_in", "w_uq", "w_ukv"]
    assert last + mid + ffn == names

    last_pcs = to_pieces(last, [gw_in, gw_uq, gw_ukv])
    last_pair = pair_sums(last, last_pcs, _sibling_swap(last_pcs, name="rs_sibling_swap_last"))
    last_send, last_recv, last_pair, last_lands, last_token = exchange_start(last_pair, "last")
    ada = ada + last_token[0:1, 0:1]
    dh1 = _mm(dproj, w_in_p, name="mm_d_h1", tb=True, bias=jnp.zeros((1, D), F32) + last_token[0:1, 0:1])
    grad_x, red_1 = _rmsmod1_bwd(dh1, x, dx2, g_norm1, ada, SC1)
    last_pair, last_chips = _split_wait(last_send, last_recv, last_pair, last_lands, grad_x, _exchange_plan,
                                        name="rs_exchange_last_wait")
    mid_pair, mid_chips = _split_wait(mid_send, mid_recv, mid_pair, mid_lands, grad_x, _exchange_plan,
                                      name="rs_exchange_mid_wait")
    ffn_pair, ffn_chips = _split_wait(ffn_send, ffn_recv, ffn_pair, ffn_lands, grad_x, _exchange_plan,
                                      name="rs_exchange_ffn_wait")
    reduced = [_chip_sum(s, r, chip_idx, name="rs_chip_sum_" + nm)
               for s, r, nm in zip(last_pair + mid_pair + ffn_pair, last_chips + mid_chips + ffn_chips, names)]
    send = _split_start(reduced, [jax.ShapeDtypeStruct(r_.shape, r_.dtype) for r_ in reduced], _send_plan, 1,
                        name="rs_send_start")

    vec_names = ["b_ada", "g_norm1", "g_norm2", "g_q_latent", "g_kv_latent", "g_q_head", "g_k_head"]
    vec_w = dict(b_ada=b_ada, g_norm1=g_norm1, g_norm2=g_norm2, g_q_latent=g_q_latent, g_kv_latent=g_kv_latent,
                 g_q_head=g_q_head, g_k_head=g_k_head)
    vec_m = dict(b_ada=m_b_ada, g_norm1=m_g_norm1, g_norm2=m_g_norm2, g_q_latent=m_g_q_latent,
                 g_kv_latent=m_g_kv_latent, g_q_head=m_g_q_head, g_k_head=m_g_k_head)
    vec_v = dict(b_ada=v_b_ada, g_norm1=v_g_norm1, g_norm2=v_g_norm2, g_q_latent=v_g_q_latent,
                 g_kv_latent=v_g_kv_latent, g_q_head=v_g_q_head, g_k_head=v_g_k_head)
    d_ada = jnp.concatenate([red_1[0:1], red_1[1:2], red_2[3:4], red_2[0:1], red_2[1:2], red_l[0:1]], axis=1)
    vec_parts = dict(b_ada=d_ada, g_norm1=red_1[2:3], g_norm2=red_2[2:3], g_q_latent=red_lat[0:1],
                     g_kv_latent=red_lat[1:2], g_q_head=red_qh[0:1], g_k_head=red_kh[0:1])
    widths = [-(-vec_w[nm].shape[1] // LANE) * LANE for nm in vec_names]
    offs = [sum(widths[:i]) for i in range(len(widths))]
    pack = lambda d: jnp.concatenate([_pad_cols(d[nm][:, :vec_w[nm].shape[1]], wd) for nm, wd in zip(vec_names, widths)], axis=1)
    nvec = sum(widths) + LANE
    no_loss = jnp.zeros((1, LANE), F32)
    parts = jnp.concatenate([pack(vec_parts), loss_p[0:1, :]], axis=1) + send[4][0:1, 0:1]
    parts_all = _gather_blocks([jnp.broadcast_to(parts, (8, nvec))], name="gather_vec_grads",
                               in_vmem=True)[0][:, 0, :]
    gvec, dvec, nmvec, nvvec = _adamw_vec(parts_all, *[jnp.concatenate([pack(d), no_loss], axis=1)
                                                       for d in (vec_w, vec_m, vec_v)])
    loss = gvec[0, nvec - LANE]
    unpack = lambda a: {nm: a[:, o_:o_ + vec_w[nm].shape[1]] for nm, o_ in zip(vec_names, offs)}
    gvec, dvec, nmvec, nvvec = unpack(gvec), unpack(dvec), unpack(nmvec), unpack(nvvec)

    dada_all = lax.dynamic_slice_in_dim(parts_all[:, :6 * D], chip * n_ada, n_ada, axis=1)
    cact_t = jax.nn.silu(c_all).T
    g_ada, d_ada_w, nm_ada, nv_ada = _adamw_ada(cact_t, dada_all, w_ada[0], m_w_ada[0], v_w_ada[0])
    reduced, from_sibling2 = _split_wait(send[0], send[1], send[2], send[3], g_ada, _send_plan, name="rs_send_wait")

    ms = dict(w_in=m_w_in, w_uq=m_w_uq, w_ukv=m_w_ukv, w_proj_mla=m_w_proj_mla, w_proj_sb=m_w_proj_sb,
              w_out=m_w_out, w_ffn_in=m_w_ffn_in, w_ffn_out=m_w_ffn_out)
    vs = dict(w_in=v_w_in, w_uq=v_w_uq, w_ukv=v_w_ukv, w_proj_mla=v_w_proj_mla, w_proj_sb=v_w_proj_sb,
              w_out=v_w_out, w_ffn_in=v_w_ffn_in, w_ffn_out=v_w_ffn_out)
    G, DL, NM, NV = {}, {}, {}, {}
    for nm, mine, other in zip(names, reduced, from_sibling2):
        g_, d_, m_, v_ = _adamw(mats[nm], mine, other, c_idx, ms[nm][0], vs[nm][0], name="adamw_" + nm)
        G[nm], DL[nm], NM[nm], NV[nm] = g_[None], d_[None], m_[None], v_[None]
    G["w_ada"], DL["w_ada"], NM["w_ada"], NV["w_ada"] = g_ada[None], d_ada_w[None], nm_ada[None], nv_ada[None]
    for nm in vec_names:
        G[nm], DL[nm], NM[nm], NV[nm] = gvec[nm], dvec[nm], nmvec[nm], nvvec[nm]

    order = ["w_ada", "b_ada", "g_norm1", "g_norm2", "w_in", "g_q_latent", "g_kv_latent", "w_uq", "w_ukv",
             "g_q_head", "g_k_head", "w_proj_mla", "w_proj_sb", "w_out", "w_ffn_in", "w_ffn_out"]
    return (loss, grad_x[None], *[G[n] for n in order], *[DL[n] for n in order],
            *[NM[n] for n in order], *[NV[n] for n in order])
```

```python
import functools
import math

import jax
import jax.numpy as jnp
from jax import lax
from jax.experimental import pallas as pl
from jax.experimental.pallas import tpu as pltpu

F32 = jnp.float32
BF16 = jnp.bfloat16
MESH = pl.DeviceIdType.MESH

EPS = 1e-6
ROPE_THETA = 10000.0
NOPE = 128
ROPE = 64
QK_DIM = NOPE + ROPE
HEAD_PAD = 256
HEAD = 128
N_DEV = 8
LANE = 128
VMEM_LIMIT = 48 * 1024 * 1024

ADAM_LR = 0.001
ADAM_B1 = 0.9
ADAM_B2 = 0.999
ADAM_EPS = 1e-08
ADAM_WD = 0.01
ADAM_STEP = 10


def _tile(n, target):
    if n <= target:
        return n
    t = (target // LANE) * LANE
    while t >= LANE:
        if n % t == 0:
            return t
        t -= LANE
    return n


def _row_tile(rows, row_bytes, budget=24 * 1024 * 1024):
    cap = max(8, budget // (2 * row_bytes))
    best = None
    for t in range(8, min(rows, cap) + 1, 8):
        if rows % t == 0:
            best = t
    return best if best is not None else rows


def _params(sem):
    return pltpu.CompilerParams(dimension_semantics=sem, vmem_limit_bytes=VMEM_LIMIT)


def _rows(tm, w, col=0):
    return pl.BlockSpec((tm, w), lambda i: (i, col))


def _vec(w, col=0, rows=1):
    return pl.BlockSpec((rows, w), lambda i: (0, col))


MM_VMEM_BUDGET = 36 * 1024 * 1024


def _mm(a, b, *, name, ta=False, tb=False, out_dtype=F32, a_fn=None, bias=None, tm=1024, tn=1024, pieces=None,
        col_perm=None, fused=None):
    M = a.shape[1] if ta else a.shape[0]
    K = a.shape[0] if ta else a.shape[1]
    N = b.shape[0] if tb else b.shape[1]
    assert K == (b.shape[1] if tb else b.shape[0]), (a.shape, b.shape, ta, tb)
    if pieces == "cols":
        tm, tn = _tile(M // 2, tm), _tile(N // 4, tn)
        assert (M // 2) % tm == 0 and (N // 4) % tn == 0
    elif pieces == "rows":
        tm, tn = M // 4, _tile(N, tn)
    else:
        tm, tn = _tile(M, tm), _tile(N, tn)
    sa, sb, so = a.dtype.itemsize, b.dtype.itemsize, jnp.dtype(out_dtype).itemsize

    def fits(tk):
        return 2 * tk * (tm * sa + tn * sb) + tm * tn * (2 * so + 4) <= MM_VMEM_BUDGET

    tk = K
    while not fits(tk):
        smaller = _tile(K, tk - LANE)
        if smaller >= tk:
            break
        tk = smaller
    nk = K // tk
    dn = (((0 if ta else 1,), (1 if tb else 0,)), ((), ()))
    b_outer = nk == 1 and a.size * sa * (N // tn) < b.size * sb * (M // tm)

    n_extra = len(fused[1]) if fused else 0
    n_out = len(fused[2]) if fused else 1

    def body(*refs):
        a_ref, b_ref = refs[:2]
        bias_ref = refs[2] if bias is not None else None
        first = 3 if bias is not None else 2
        extra_refs = refs[first:first + n_extra]
        out_refs = refs[first + n_extra:first + n_extra + n_out]
        o_ref = out_refs[0]
        av = a_ref[...]
        if a_fn is not None:
            av = a_fn(av.astype(F32))
        part = lax.dot_general(av.astype(BF16), b_ref[...].astype(BF16), dn, preferred_element_type=F32)

        def finish(r):
            if bias is not None:
                r = r + bias_ref[...]
            if fused:
                for ref, tile in zip(out_refs, fused[0](r, *[e[...] for e in extra_refs])):
                    ref[...] = tile.astype(ref.dtype)
            elif pieces == "rows":
                o_ref[0] = r[:tm // 2].astype(o_ref.dtype)
                o_ref[1] = r[tm // 2:].astype(o_ref.dtype)
            else:
                o_ref[...] = r.astype(o_ref.dtype)

        if nk == 1:
            finish(part)
        else:
            acc_ref = refs[-1]
            k = pl.program_id(2)

            @pl.when(k == 0)
            def _():
                acc_ref[...] = part

            @pl.when(k > 0)
            def _():
                acc_ref[...] += part

            @pl.when(k == nk - 1)
            def _():
                finish(acc_ref[...])

    def ij(g0, g1):
        return (g1, g0) if b_outer else (g0, g1)

    def amap(g0, g1, k):
        i, _ = ij(g0, g1)
        return (k, i) if ta else (i, k)

    def bmap(g0, g1, k):
        _, j = ij(g0, g1)
        return (j, k) if tb else (k, j)

    in_specs = [pl.BlockSpec((tk, tm) if ta else (tm, tk), amap), pl.BlockSpec((tn, tk) if tb else (tk, tn), bmap)]
    args = [a, b]
    if bias is not None:
        in_specs.append(pl.BlockSpec((1, tn), lambda g0, g1, k: (0, ij(g0, g1)[1])))
        args.append(bias)
    grid = (N // tn, M // tm, nk) if b_outer else (M // tm, N // tn, nk)
    if pieces == "cols":
        ni, nj = M // 2 // tm, N // 4 // tn

        def omap(g0, g1, k):
            i, j = ij(g0, g1)
            j = col_perm(j) if col_perm else j
            return (i // ni, j // nj, i % ni, j % nj)

        out_spec = pl.BlockSpec((None, None, tm, tn), omap)
        out_shape = jax.ShapeDtypeStruct((2, 4, M // 2, N // 4), out_dtype)
    elif pieces == "rows":
        out_spec = pl.BlockSpec((2, None, tm // 2, tn), lambda g0, g1, k: (0, ij(g0, g1)[0], 0, ij(g0, g1)[1]))
        out_shape = jax.ShapeDtypeStruct((2, 4, tm // 2, N), out_dtype)
    else:
        out_spec = pl.BlockSpec((tm, tn), lambda g0, g1, k: ij(g0, g1))
        out_shape = jax.ShapeDtypeStruct((M, N), out_dtype)
    if fused:
        for arr, width in fused[1]:
            in_specs.append(pl.BlockSpec((tm, width), lambda g0, g1, k: ij(g0, g1)))
            args.append(arr)
        out_spec = [pl.BlockSpec((tm, width), lambda g0, g1, k: ij(g0, g1)) for _, width, _ in fused[2]]
        out_shape = [jax.ShapeDtypeStruct((M, cols), dt) for cols, _, dt in fused[2]]
    return pl.pallas_call(
        body, name=name, grid=grid, in_specs=in_specs, out_specs=out_spec, out_shape=out_shape,
        scratch_shapes=[pltpu.VMEM((tm, tn), F32)] if nk > 1 else [],
        compiler_params=_params(("parallel", "parallel", "arbitrary")),
    )(*args)


def _rms_rows(v):
    return lax.rsqrt(jnp.mean(v * v, axis=-1, keepdims=True) + EPS)


def _rmsmod(x, g, ada, sc_col, sh_col, *, name):
    S, D = x.shape
    tm = _tile(S, 256)

    def body(x_ref, g_ref, sc_ref, sh_ref, h_ref):
        xv = x_ref[...]
        h = (xv * _rms_rows(xv) * g_ref[...]) * (1.0 + sc_ref[...]) + sh_ref[...]
        h_ref[...] = h.astype(h_ref.dtype)

    return pl.pallas_call(
        body, name=name, grid=(S // tm,),
        in_specs=[_rows(tm, D), _vec(D), _vec(D, sc_col), _vec(D, sh_col)],
        out_specs=_rows(tm, D), out_shape=jax.ShapeDtypeStruct((S, D), BF16),
        compiler_params=_params(("parallel",)),
    )(x, g, ada, ada)


def _latent_norm(proj, g_q, g_kv, ql):
    S = proj.shape[0]
    tm = _tile(S, 512)

    def body(cq_ref, ckv_ref, gq_ref, gkv_ref, oq_ref, okv_ref):
        cq = cq_ref[...].astype(F32)
        oq_ref[...] = (cq * _rms_rows(cq) * gq_ref[...]).astype(BF16)
        ckv = ckv_ref[...].astype(F32)
        okv_ref[...] = (ckv * _rms_rows(ckv) * gkv_ref[...]).astype(BF16)

    return pl.pallas_call(
        body, name="latent_norm", grid=(S // tm,),
        in_specs=[_rows(tm, ql, 0), _rows(tm, ql, 1), _vec(ql), _vec(ql)],
        out_specs=[_rows(tm, ql), _rows(tm, ql)],
        out_shape=[jax.ShapeDtypeStruct((S, ql), BF16)] * 2,
        compiler_params=_params(("parallel",)),
    )(proj, proj, g_q, g_kv)


def _rope_fwd(y, c, s1, s2):
    return y * c + pltpu.roll(y, ROPE // 2, 1) * s1 + pltpu.roll(y, HEAD_PAD - ROPE // 2, 1) * s2


def _rope_bwd(d, c, s1, s2):
    return d * c + pltpu.roll(d * s1, HEAD_PAD - ROPE // 2, 1) + pltpu.roll(d * s2, ROPE // 2, 1)


def _head_rms(v):
    return lax.rsqrt(jnp.sum(v * v, axis=-1, keepdims=True) * (1.0 / QK_DIM) + EPS)


def _q_prep(q0, g_qh, tabs, nh):
    S = q0.shape[0]
    tm = _tile(S, 256)

    def body(q_ref, g_ref, c_ref, s1_ref, s2_ref, o_ref):
        c, s1, s2, g = c_ref[...], s1_ref[...], s2_ref[...], g_ref[...]
        for h in range(nh):
            sl = slice(h * HEAD_PAD, (h + 1) * HEAD_PAD)
            xs = q_ref[:, sl].astype(F32)
            o_ref[:, sl] = (_rope_fwd(xs * _head_rms(xs) * g, c, s1, s2) * (QK_DIM ** -0.5)).astype(BF16)

    w = nh * HEAD_PAD
    return pl.pallas_call(
        body, name="mla_q_prep", grid=(S // tm,),
        in_specs=[_rows(tm, w), _vec(HEAD_PAD)] + [_rows(tm, HEAD_PAD)] * 3,
        out_specs=_rows(tm, w), out_shape=jax.ShapeDtypeStruct((S, w), BF16),
        compiler_params=_params(("parallel",)),
    )(q0, g_qh, *tabs)


def _k_prep(kv0, proj, kpe_col, g_kh, tabs, nh):
    S = kv0.shape[0]
    tm = _tile(S, 256)

    def body(kv_ref, kpe_ref, g_ref, c_ref, s1_ref, s2_ref, o_ref):
        c, s1, s2, g = c_ref[...], s1_ref[...], s2_ref[...], g_ref[...]
        kpe = kpe_ref[...].astype(F32)
        for h in range(nh):
            k0 = jnp.concatenate([kv_ref[:, h * HEAD:(h + 1) * HEAD].astype(F32), kpe], axis=1)
            o_ref[:, h * HEAD_PAD:(h + 1) * HEAD_PAD] = _rope_fwd(k0 * _head_rms(k0) * g, c, s1, s2).astype(BF16)

    return pl.pallas_call(
        body, name="mla_k_prep", grid=(S // tm,),
        in_specs=[_rows(tm, nh * HEAD, 0), _rows(tm, LANE, kpe_col), _vec(HEAD_PAD)] + [_rows(tm, HEAD_PAD)] * 3,
        out_specs=_rows(tm, nh * HEAD_PAD), out_shape=jax.ShapeDtypeStruct((S, nh * HEAD_PAD), BF16),
        compiler_params=_params(("parallel",)),
    )(kv0, proj, g_kh, *tabs)


def _gate_merge(pa, pb, proj, gla_col, glb_col):
    S, D = pa.shape
    tm = _tile(S, 256)

    def body(pa_ref, pb_ref, ga_ref, gb_ref, o_ref):
        o_ref[...] = (jax.nn.sigmoid(ga_ref[...].astype(F32)) * pa_ref[...] + jax.nn.sigmoid(gb_ref[...].astype(F32)) * pb_ref[...]).astype(BF16)

    return pl.pallas_call(
        body, name="gate_merge", grid=(S // tm,),
        in_specs=[_rows(tm, D), _rows(tm, D), _rows(tm, D, gla_col), _rows(tm, D, glb_col)],
        out_specs=_rows(tm, D), out_shape=jax.ShapeDtypeStruct((S, D), BF16),
        compiler_params=_params(("parallel",)),
    )(pa, pb, proj, proj)


def _resid_rmsmod(x, o, g, ada, gt_col, sc_col, sh_col):
    S, D = x.shape
    tm = _tile(S, 256)

    def body(x_ref, o_ref, g_ref, gt_ref, sc_ref, sh_ref, x2_ref, h_ref):
        x2 = x_ref[...] + gt_ref[...] * o_ref[...]
        x2_ref[...] = x2
        h_ref[...] = ((x2 * _rms_rows(x2) * g_ref[...]) * (1.0 + sc_ref[...]) + sh_ref[...]).astype(BF16)

    return pl.pallas_call(
        body, name="resid_rmsmod2", grid=(S // tm,),
        in_specs=[_rows(tm, D), _rows(tm, D), _vec(D), _vec(D, gt_col), _vec(D, sc_col), _vec(D, sh_col)],
        out_specs=[_rows(tm, D), _rows(tm, D)],
        out_shape=[jax.ShapeDtypeStruct((S, D), F32), jax.ShapeDtypeStruct((S, D), BF16)],
        compiler_params=_params(("parallel",)),
    )(x, o, g, ada, ada, ada)


def _swiglu_tile(ib):
    def fn(r):
        pairs = r.shape[1] // (2 * ib)
        act = [jax.nn.silu(r[:, 2 * p * ib:(2 * p + 1) * ib]) * r[:, (2 * p + 1) * ib:(2 * p + 2) * ib] for p in range(pairs)]
        return r, jnp.concatenate(act, axis=1) if pairs > 1 else act[0]
    return fn


def _swiglu_bwd_tile(ib):
    def fn(d, ff):
        ff = ff.astype(F32)
        out = []
        for p in range(d.shape[1] // ib):
            dp = d[:, p * ib:(p + 1) * ib]
            g = ff[:, 2 * p * ib:(2 * p + 1) * ib]
            u = ff[:, (2 * p + 1) * ib:(2 * p + 2) * ib]
            sg = jax.nn.sigmoid(g)
            out += [dp * u * sg * (1.0 + g * (1.0 - sg)), dp * g * sg]
        return (jnp.concatenate(out, axis=1),)
    return fn


def _loss_head(x2, f, tgt, ada, gt_col):
    S, D = x2.shape
    tm = _tile(S, 256)

    def body(x2_ref, f_ref, t_ref, gt_ref, dy_ref, df_ref, red_ref, loss_ref):
        @pl.when(pl.program_id(0) == 0)
        def _():
            red_ref[...] = jnp.zeros_like(red_ref)
            loss_ref[...] = jnp.zeros_like(loss_ref)

        fv = f_ref[...]
        gt = gt_ref[...]
        err = x2_ref[...] + gt * fv - t_ref[...]
        dy = err * (1.0 / D)
        dy_ref[...] = dy
        df_ref[...] = (dy * gt).astype(BF16)
        red_ref[0:1, :] += jnp.sum(dy * fv, axis=0, keepdims=True)
        loss_ref[...] += (0.5 / D) * jnp.sum(err * err)

    return pl.pallas_call(
        body, name="loss_head", grid=(S // tm,),
        in_specs=[_rows(tm, D), _rows(tm, D), _rows(tm, D), _vec(D, gt_col)],
        out_specs=[_rows(tm, D), _rows(tm, D), _vec(D, rows=8), _vec(LANE, rows=8)],
        out_shape=[jax.ShapeDtypeStruct((S, D), F32), jax.ShapeDtypeStruct((S, D), BF16),
                   jax.ShapeDtypeStruct((8, D), F32), jax.ShapeDtypeStruct((8, LANE), F32)],
        compiler_params=_params(("arbitrary",)),
    )(x2, f, tgt, ada)


def _rmsmod2_bwd(dh2, x2, dy, o, g, ada, sc_col, gt_col):
    S, D = x2.shape
    tm = _tile(S, 256)

    def body(dh_ref, x2_ref, dy_ref, o_ref, g_ref, sc_ref, gt_ref, dx_ref, do_ref, red_ref):
        @pl.when(pl.program_id(0) == 0)
        def _():
            red_ref[...] = jnp.zeros_like(red_ref)

        dh = dh_ref[...]
        x2 = x2_ref[...]
        gv = g_ref[...]
        mod = 1.0 + sc_ref[...]
        r = _rms_rows(x2)
        xn = x2 * r
        t = dh * xn
        red_ref[0:1, :] += jnp.sum(dh, axis=0, keepdims=True)
        red_ref[1:2, :] += jnp.sum(t * gv, axis=0, keepdims=True)
        red_ref[2:3, :] += jnp.sum(t * mod, axis=0, keepdims=True)
        dxn = dh * gv * mod
        dx = dy_ref[...] + r * (dxn - xn * jnp.mean(dxn * xn, axis=-1, keepdims=True))
        dx_ref[...] = dx
        red_ref[3:4, :] += jnp.sum(dx * o_ref[...], axis=0, keepdims=True)
        do_ref[...] = (dx * gt_ref[...]).astype(BF16)

    return pl.pallas_call(
        body, name="rmsmod2_bwd", grid=(S // tm,),
        in_specs=[_rows(tm, D)] * 4 + [_vec(D), _vec(D, sc_col), _vec(D, gt_col)],
        out_specs=[_rows(tm, D), _rows(tm, D), _vec(D, rows=8)],
        out_shape=[jax.ShapeDtypeStruct((S, D), F32), jax.ShapeDtypeStruct((S, D), BF16),
                   jax.ShapeDtypeStruct((8, D), F32)],
        compiler_params=_params(("arbitrary",)),
    )(dh2, x2, dy, o, g, ada, ada)


def _rmsmod1_bwd(dh, x, dx2, g, ada, sc_col):
    S, D = x.shape
    tm = _tile(S, 256)

    def body(dh_ref, x_ref, dx2_ref, g_ref, sc_ref, gx_ref, red_ref):
        @pl.when(pl.program_id(0) == 0)
        def _():
            red_ref[...] = jnp.zeros_like(red_ref)

        dh = dh_ref[...]
        xv = x_ref[...]
        gv = g_ref[...]
        mod = 1.0 + sc_ref[...]
        r = _rms_rows(xv)
        xn = xv * r
        t = dh * xn
        red_ref[0:1, :] += jnp.sum(dh, axis=0, keepdims=True)
        red_ref[1:2, :] += jnp.sum(t * gv, axis=0, keepdims=True)
        red_ref[2:3, :] += jnp.sum(t * mod, axis=0, keepdims=True)
        dxn = dh * gv * mod
        gx_ref[...] = dx2_ref[...] + r * (dxn - xn * jnp.mean(dxn * xn, axis=-1, keepdims=True))

    return pl.pallas_call(
        body, name="rmsmod1_bwd", grid=(S // tm,),
        in_specs=[_rows(tm, D)] * 3 + [_vec(D), _vec(D, sc_col)],
        out_specs=[_rows(tm, D), _vec(D, rows=8)],
        out_shape=[jax.ShapeDtypeStruct((S, D), F32), jax.ShapeDtypeStruct((8, D), F32)],
        compiler_params=_params(("arbitrary",)),
    )(dh, x, dx2, g, ada)


def _gate_bwd(dm, pa, pb, proj, gla_col, glb_col):
    S, D = pa.shape
    tm = _tile(S, 256)

    def body(dm_ref, pa_ref, pb_ref, la_ref, lb_ref, dpa_ref, dpb_ref, dla_ref, dlb_ref):
        dm_ = dm_ref[...]
        ga = jax.nn.sigmoid(la_ref[...].astype(F32))
        gb = jax.nn.sigmoid(lb_ref[...].astype(F32))
        dpa_ref[...] = (dm_ * ga).astype(BF16)
        dpb_ref[...] = (dm_ * gb).astype(BF16)
        dla_ref[...] = (dm_ * pa_ref[...] * ga * (1.0 - ga)).astype(BF16)
        dlb_ref[...] = (dm_ * pb_ref[...] * gb * (1.0 - gb)).astype(BF16)

    return pl.pallas_call(
        body, name="gate_bwd", grid=(S // tm,),
        in_specs=[_rows(tm, D)] * 3 + [_rows(tm, D, gla_col), _rows(tm, D, glb_col)],
        out_specs=[_rows(tm, D)] * 4, out_shape=[jax.ShapeDtypeStruct((S, D), BF16)] * 4,
        compiler_params=_params(("parallel",)),
    )(dm, pa, pb, proj, proj)


def _q_prep_bwd(dq, q0, g_qh, tabs, nh):
    S = q0.shape[0]
    tm = _tile(S, 256)

    def body(dq_ref, q_ref, g_ref, c_ref, s1_ref, s2_ref, o_ref, red_ref):
        @pl.when(pl.program_id(0) == 0)
        def _():
            red_ref[...] = jnp.zeros_like(red_ref)

        c, s1, s2, g = c_ref[...], s1_ref[...], s2_ref[...], g_ref[...]
        dg = jnp.zeros((1, HEAD_PAD), F32)
        for h in range(nh):
            sl = slice(h * HEAD_PAD, (h + 1) * HEAD_PAD)
            d1 = _rope_bwd(dq_ref[:, sl], c, s1, s2)
            xs = q_ref[:, sl].astype(F32)
            r = _head_rms(xs)
            qn = xs * r
            dg = dg + jnp.sum(d1 * qn, axis=0, keepdims=True)
            dn = d1 * g
            o_ref[:, sl] = (r * (dn - qn * (jnp.sum(dn * qn, axis=-1, keepdims=True) * (1.0 / QK_DIM)))).astype(BF16)
        red_ref[0:1, :] += dg

    w = nh * HEAD_PAD
    return pl.pallas_call(
        body, name="mla_q_prep_bwd", grid=(S // tm,),
        in_specs=[_rows(tm, w), _rows(tm, w), _vec(HEAD_PAD)] + [_rows(tm, HEAD_PAD)] * 3,
        out_specs=[_rows(tm, w), _vec(HEAD_PAD, rows=8)],
        out_shape=[jax.ShapeDtypeStruct((S, w), BF16), jax.ShapeDtypeStruct((8, HEAD_PAD), F32)],
        compiler_params=_params(("arbitrary",)),
    )(dq, q0, g_qh, *tabs)


def _k_prep_bwd(dk, dv, kv0, proj, kpe_col, g_kh, tabs, nh):
    S = kv0.shape[0]
    tm = _tile(S, 256)
    wv = nh * HEAD

    def body(dk_ref, dv_ref, kv_ref, kpe_ref, g_ref, c_ref, s1_ref, s2_ref, o_ref, dpe_ref, red_ref):
        @pl.when(pl.program_id(0) == 0)
        def _():
            red_ref[...] = jnp.zeros_like(red_ref)

        c, s1, s2, g = c_ref[...], s1_ref[...], s2_ref[...], g_ref[...]
        kpe = kpe_ref[...].astype(F32)
        dg = jnp.zeros((1, HEAD_PAD), F32)
        dpe = jnp.zeros((tm, LANE), F32)
        for h in range(nh):
            d1 = _rope_bwd(dk_ref[:, h * HEAD_PAD:(h + 1) * HEAD_PAD], c, s1, s2)
            k0 = jnp.concatenate([kv_ref[:, h * HEAD:(h + 1) * HEAD].astype(F32), kpe], axis=1)
            r = _head_rms(k0)
            kn = k0 * r
            dg = dg + jnp.sum(d1 * kn, axis=0, keepdims=True)
            dn = d1 * g
            dk0 = r * (dn - kn * (jnp.sum(dn * kn, axis=-1, keepdims=True) * (1.0 / QK_DIM)))
            o_ref[:, h * HEAD:(h + 1) * HEAD] = dk0[:, :HEAD].astype(BF16)
            dpe = dpe + dk0[:, HEAD:]
        o_ref[:, wv:] = dv_ref[...].astype(BF16)
        dpe_ref[...] = dpe.astype(BF16)
        red_ref[0:1, :] += dg

    return pl.pallas_call(
        body, name="mla_k_prep_bwd", grid=(S // tm,),
        in_specs=[_rows(tm, nh * HEAD_PAD), _rows(tm, wv), _rows(tm, wv, 0), _rows(tm, LANE, kpe_col),
                  _vec(HEAD_PAD)] + [_rows(tm, HEAD_PAD)] * 3,
        out_specs=[_rows(tm, 2 * wv), _rows(tm, LANE), _vec(HEAD_PAD, rows=8)],
        out_shape=[jax.ShapeDtypeStruct((S, 2 * wv), BF16), jax.ShapeDtypeStruct((S, LANE), BF16),
                   jax.ShapeDtypeStruct((8, HEAD_PAD), F32)],
        compiler_params=_params(("arbitrary",)),
    )(dk, dv, kv0, proj, g_kh, *tabs)


def _latent_norm_bwd(dcqn, dckvn, proj, g_q, g_kv, ql):
    S = proj.shape[0]
    tm = _tile(S, 512)

    def body(dq_ref, dkv_ref, cq_ref, ckv_ref, gq_ref, gkv_ref, oq_ref, okv_ref, red_ref):
        @pl.when(pl.program_id(0) == 0)
        def _():
            red_ref[...] = jnp.zeros_like(red_ref)

        for row, (d_ref, c_ref, g_ref, o_ref) in enumerate(((dq_ref, cq_ref, gq_ref, oq_ref),
                                                            (dkv_ref, ckv_ref, gkv_ref, okv_ref))):
            d = d_ref[...]
            cv = c_ref[...].astype(F32)
            r = _rms_rows(cv)
            ch = cv * r
            red_ref[row:row + 1, :] += jnp.sum(d * ch, axis=0, keepdims=True)
            dn = d * g_ref[...]
            o_ref[...] = (r * (dn - ch * jnp.mean(dn * ch, axis=-1, keepdims=True))).astype(BF16)

    return pl.pallas_call(
        body, name="latent_norm_bwd", grid=(S // tm,),
        in_specs=[_rows(tm, ql), _rows(tm, ql), _rows(tm, ql, 0), _rows(tm, ql, 1), _vec(ql), _vec(ql)],
        out_specs=[_rows(tm, ql), _rows(tm, ql), _vec(ql, rows=8)],
        out_shape=[jax.ShapeDtypeStruct((S, ql), BF16)] * 2 + [jax.ShapeDtypeStruct((8, ql), F32)],
        compiler_params=_params(("arbitrary",)),
    )(dcqn, dckvn, proj, proj, g_q, g_kv)


NEG = -1e30
ATT_TILE = 512
SB_TILE = 512
SB_SUB = 128
_NT = (((1,), (1,)), ((), ()))
_TN = (((0,), (0,)), ((), ()))


def _dot(a, b, dn=(((1,), (0,)), ((), ()))):
    return lax.dot_general(a, b, dn, preferred_element_type=F32)


def _key_rows(kb, t):
    return pl.ds(pl.multiple_of(kb * t, t), t)


def _diag_mask(t, strict):
    r = lax.broadcasted_iota(jnp.int32, (t, t), 0)
    c = lax.broadcasted_iota(jnp.int32, (t, t), 1)
    return c < r if strict else c <= r


def _mla_fwd(q, k, kv0, nh):
    S = q.shape[0]
    t = _tile(S, ATT_TILE)

    def body(q_ref, k_ref, v_ref, o_ref, lse_ref):
        i = pl.program_id(1)
        qv = q_ref[...]

        def block(kb, carry, masked):
            m, l, acc = carry
            rows = _key_rows(kb, t)
            s = _dot(qv, k_ref[rows, :], _NT)
            if masked:
                s = jnp.where(_diag_mask(t, False), s, NEG)
            m_new = jnp.maximum(m, jnp.max(s, axis=-1, keepdims=True))
            alpha = jnp.exp(m - m_new)
            p = jnp.exp(s - m_new)
            l = alpha * l + jnp.sum(p, axis=-1, keepdims=True)
            acc = alpha * acc + _dot(p.astype(BF16), v_ref[rows, :].astype(BF16))
            return m_new, l, acc

        init = (jnp.full((t, 1), NEG, F32), jnp.zeros((t, 1), F32), jnp.zeros((t, HEAD), F32))
        carry = lax.fori_loop(0, i, lambda kb, c: block(kb, c, False), init)
        m, l, acc = block(i, carry, True)
        o_ref[...] = acc / l
        lse_ref[...] = m + jnp.log(l)

    return pl.pallas_call(
        body, name="mla_attn_fwd", grid=(nh, S // t),
        in_specs=[pl.BlockSpec((t, HEAD_PAD), lambda h, i: (i, h)),
                  pl.BlockSpec((S, HEAD_PAD), lambda h, i: (0, h)),
                  pl.BlockSpec((S, HEAD), lambda h, i: (0, nh + h))],
        out_specs=[pl.BlockSpec((t, HEAD), lambda h, i: (i, h)),
                   pl.BlockSpec((None, t, 1), lambda h, i: (h, i, 0))],
        out_shape=[jax.ShapeDtypeStruct((S, nh * HEAD), F32), jax.ShapeDtypeStruct((nh, S, 1), F32)],
        compiler_params=_params(("parallel", "arbitrary")),
    )(q, k, kv0)


def _mla_bwd(q, k, kv0, o, do, lse, nh):
    S = q.shape[0]
    t = _tile(S, ATT_TILE)
    scale = QK_DIM ** -0.5

    def body(q_ref, k_ref, v_ref, o_ref, do_ref, lse_ref, dq_ref, dk_ref, dv_ref):
        i = pl.program_id(1)

        @pl.when(i == 0)
        def _():
            dk_ref[...] = jnp.zeros_like(dk_ref)
            dv_ref[...] = jnp.zeros_like(dv_ref)

        qv = q_ref[...]
        dov = do_ref[...]
        delta = jnp.sum(dov * o_ref[...], axis=-1, keepdims=True)
        dob = dov.astype(BF16)
        lse = lse_ref[...]

        def block(kb, dq, masked):
            rows = _key_rows(kb, t)
            ks = k_ref[rows, :]
            vs = v_ref[rows, :].astype(BF16)
            p = jnp.exp(_dot(qv, ks, _NT) - lse)
            if masked:
                p = jnp.where(_diag_mask(t, False), p, 0.0)
            ds = (p * (_dot(dob, vs, _NT) - delta)).astype(BF16)
            dk_ref[rows, :] += _dot(ds, qv, _TN)
            dv_ref[rows, :] += _dot(p.astype(BF16), dob, _TN)
            return dq + _dot(ds, ks)

        dq = lax.fori_loop(0, i, lambda kb, c: block(kb, c, False), jnp.zeros((t, HEAD_PAD), F32))
        dq_ref[...] = block(i, dq, True) * scale

    return pl.pallas_call(
        body, name="mla_attn_bwd", grid=(nh, S // t),
        in_specs=[pl.BlockSpec((t, HEAD_PAD), lambda h, i: (i, h)),
                  pl.BlockSpec((S, HEAD_PAD), lambda h, i: (0, h)),
                  pl.BlockSpec((S, HEAD), lambda h, i: (0, nh + h)),
                  pl.BlockSpec((t, HEAD), lambda h, i: (i, h)),
                  pl.BlockSpec((t, HEAD), lambda h, i: (i, h)),
                  pl.BlockSpec((None, t, 1), lambda h, i: (h, i, 0))],
        out_specs=[pl.BlockSpec((t, HEAD_PAD), lambda h, i: (i, h)),
                   pl.BlockSpec((S, HEAD_PAD), lambda h, i: (0, h)),
                   pl.BlockSpec((S, HEAD), lambda h, i: (0, h))],
        out_shape=[jax.ShapeDtypeStruct((S, nh * HEAD_PAD), F32), jax.ShapeDtypeStruct((S, nh * HEAD_PAD), F32),
                   jax.ShapeDtypeStruct((S, nh * HEAD), F32)],
        compiler_params=_params(("parallel", "arbitrary")),
    )(q, k, kv0, o, do, lse)


def _tri(n, cmp):
    r = lax.broadcasted_iota(jnp.int32, (n, n), 0)
    c = lax.broadcasted_iota(jnp.int32, (n, n), 1)
    return jnp.where(cmp(r, c), 1.0, 0.0).astype(BF16)


def _sb_block(qv, ks, run, upper, t, masked):
    z = _dot(qv, ks, _NT)
    lb = jnp.minimum(z, 0.0) - jnp.log(1.0 + jnp.exp(-jnp.abs(z)))
    lom = lb - z
    mask = _diag_mask(t, True) if masked else None
    if masked:
        lom = jnp.where(mask, lom, 0.0)
    tails = []
    for sblk in reversed(range(t // SB_SUB)):
        part = lom[:, sblk * SB_SUB:(sblk + 1) * SB_SUB]
        tails.append(_dot(part.astype(BF16), upper) + run)
        run = run + jnp.sum(part, axis=-1, keepdims=True)
    a = jnp.exp(lb + jnp.concatenate(tails[::-1], axis=1))
    if masked:
        a = jnp.where(mask, a, 0.0)
    return a, lb, mask, run


def _sb_fwd(proj, q_col, k_col, v_col, nh):
    S = proj.shape[0]
    t = _tile(S, SB_TILE)
    assert S // t <= LANE
    scale = HEAD ** -0.5

    def body(q_ref, k_ref, v_ref, o_ref, runs_ref):
        i = pl.program_id(1)
        qv = (q_ref[...].astype(F32) * scale).astype(BF16)
        upper = _tri(SB_SUB, lambda j, s: j > s)
        lane = lax.broadcasted_iota(jnp.int32, (t, LANE), 1)

        def block(kb, carry, masked):
            run, acc, runs = carry
            runs = jnp.where(lane == kb, run, runs)
            rows = _key_rows(kb, t)
            a, _, _, run = _sb_block(qv, k_ref[rows, :].astype(BF16), run, upper, t, masked)
            return run, acc + _dot(a.astype(BF16), v_ref[rows, :].astype(BF16)), runs

        carry = block(i, (jnp.zeros((t, 1), F32), jnp.zeros((t, HEAD), F32), jnp.zeros((t, LANE), F32)), True)
        _, o_ref[...], runs_ref[...] = lax.fori_loop(0, i, lambda j, c: block(i - 1 - j, c, False), carry)

    return pl.pallas_call(
        body, name="sb_attn_fwd", grid=(nh, S // t),
        in_specs=[pl.BlockSpec((t, HEAD), lambda h, i: (i, q_col + h)),
                  pl.BlockSpec((S, HEAD), lambda h, i: (0, k_col + h)),
                  pl.BlockSpec((S, HEAD), lambda h, i: (0, v_col + h))],
        out_specs=[pl.BlockSpec((t, HEAD), lambda h, i: (i, h)), pl.BlockSpec((None, t, LANE), lambda h, i: (h, i, 0))],
        out_shape=[jax.ShapeDtypeStruct((S, nh * HEAD), F32), jax.ShapeDtypeStruct((nh, S, LANE), F32)],
        compiler_params=_params(("parallel", "arbitrary")),
    )(proj, proj, proj)


def _sb_bwd(proj, q_col, k_col, v_col, dy, runs, nh):
    S = proj.shape[0]
    t = _tile(S, SB_TILE)
    scale = HEAD ** -0.5

    def body(q_ref, k_ref, v_ref, dy_ref, runs_ref, dq_ref, dk_ref, dv_ref):
        i = pl.program_id(1)

        @pl.when(i == 0)
        def _():
            dk_ref[...] = jnp.zeros_like(dk_ref)
            dv_ref[...] = jnp.zeros_like(dv_ref)

        qv = (q_ref[...].astype(F32) * scale).astype(BF16)
        dyb = dy_ref[...].astype(BF16)
        runs_v = runs_ref[...]
        lane = lax.broadcasted_iota(jnp.int32, (t, LANE), 1)
        upper = _tri(SB_SUB, lambda j, s: j > s)
        before = _tri(SB_SUB, lambda s, j: s < j)

        def block(kb, carry, masked):
            prefix, dq = carry
            rows = _key_rows(kb, t)
            ks = k_ref[rows, :].astype(BF16)
            vs = v_ref[rows, :].astype(BF16)
            run = jnp.sum(jnp.where(lane == kb, runs_v, 0.0), axis=-1, keepdims=True)
            a, lb, mask, _ = _sb_block(qv, ks, run, upper, t, masked)
            dl = a * _dot(dyb, vs, _NT)
            lefts = []
            for sblk in range(t // SB_SUB):
                part = dl[:, sblk * SB_SUB:(sblk + 1) * SB_SUB]
                lefts.append(_dot(part.astype(BF16), before) + prefix)
                prefix = prefix + jnp.sum(part, axis=-1, keepdims=True)
            beta = jnp.exp(lb)
            dz = dl * (1.0 - beta) - beta * jnp.concatenate(lefts, axis=1)
            if masked:
                dz = jnp.where(mask, dz, 0.0)
            dz = dz.astype(BF16)
            dk_ref[rows, :] += _dot(dz, qv, _TN)
            dv_ref[rows, :] += _dot(a.astype(BF16), dyb, _TN)
            return prefix, dq + _dot(dz, ks)

        carry = lax.fori_loop(0, i, lambda kb, c: block(kb, c, False),
                              (jnp.zeros((t, 1), F32), jnp.zeros((t, HEAD), F32)))
        dq_ref[...] = block(i, carry, True)[1] * scale

    full = pl.BlockSpec((S, HEAD), lambda h, i: (0, h))
    tile = pl.BlockSpec((t, HEAD), lambda h, i: (i, h))
    return pl.pallas_call(
        body, name="sb_attn_bwd", grid=(nh, S // t),
        in_specs=[pl.BlockSpec((t, HEAD), lambda h, i: (i, q_col + h)),
                  pl.BlockSpec((S, HEAD), lambda h, i: (0, k_col + h)),
                  pl.BlockSpec((S, HEAD), lambda h, i: (0, v_col + h)), tile,
                  pl.BlockSpec((None, t, LANE), lambda h, i: (h, i, 0))],
        out_specs=[tile, full, full],
        out_shape=[jax.ShapeDtypeStruct((S, nh * HEAD), F32)] * 3,
        compiler_params=_params(("parallel", "arbitrary")),
    )(proj, proj, proj, dy, runs)


def _place():
    return lax.axis_index("x"), lax.axis_index("y"), lax.axis_index("c")


def _other_chips(x, y):
    return [(1 - x, y), (x, 1 - y), (1 - x, 1 - y)]


def _dev_index(p):
    return 4 * p[0] + 2 * p[1] + p[2]


def _gather_blocks(blocks, *, name, in_vmem):
    n = len(blocks)
    per = 7

    def body(*refs):
        ins, outs = refs[:n], refs[n:2 * n]
        send_sems, recv_sems, local_sems = refs[2 * n:]
        x, y, c = _place()
        me, sibling = (x, y, c), (x, y, 1 - c)
        chips = _other_chips(x, y)

        def slot(a, p):
            return outs[a].at[_dev_index(p)]

        def copy(a, k, block, to, src=None):
            return pltpu.make_async_remote_copy(
                src_ref=slot(a, block) if src is None else src, dst_ref=slot(a, block),
                send_sem=send_sems.at[a * per + k], recv_sem=recv_sems.at[a * per + k],
                device_id=to, device_id_type=MESH)

        mine = [pltpu.make_async_copy(ins[a], slot(a, me), local_sems.at[a]) for a in range(n)] if in_vmem else []
        for cp in mine:
            cp.start()
        first = []
        for a in range(n):
            first.append(copy(a, 0, me, sibling, src=ins[a]))
            first += [copy(a, 1 + j, me, (*chip, c), src=ins[a]) for j, chip in enumerate(chips)]
        for cp in first:
            cp.start()
        passed = []
        for a in range(n):
            for j, chip in enumerate(chips):
                copy(a, 1 + j, (*chip, c), me).wait_recv()
                cp = copy(a, 4 + j, (*chip, c), sibling)
                cp.start()
                passed.append(cp)
        for a in range(n):
            copy(a, 0, sibling, me).wait_recv()
            for j, chip in enumerate(chips):
                copy(a, 4 + j, (*chip, 1 - c), me).wait_recv()
        for cp in first + passed:
            cp.wait_send()
        for cp in mine:
            cp.wait()

    space = pltpu.VMEM if in_vmem else pl.ANY
    spec = pl.BlockSpec(memory_space=space)
    outs = pl.pallas_call(
        body, name=name, in_specs=[spec] * n, out_specs=[spec] * n,
        out_shape=[jax.ShapeDtypeStruct((N_DEV,) + b.shape, b.dtype) for b in blocks],
        scratch_shapes=[pltpu.SemaphoreType.DMA((n * per,)), pltpu.SemaphoreType.DMA((n * per,)),
                        pltpu.SemaphoreType.DMA((n,))],
        compiler_params=pltpu.CompilerParams(vmem_limit_bytes=VMEM_LIMIT),
    )(*blocks)
    return list(outs)


def _sibling_swap(arrs, *, name, whole=False):
    n = len(arrs)

    def body(*refs):
        ins, outs = refs[:n], refs[n:2 * n]
        send_sems, recv_sems = refs[2 * n:]
        x, y, c = _place()
        copies = [pltpu.make_async_remote_copy(
            src_ref=ins[a] if whole else ins[a].at[1 - c], dst_ref=outs[a],
            send_sem=send_sems.at[a], recv_sem=recv_sems.at[a],
            device_id=(x, y, 1 - c), device_id_type=MESH) for a in range(n)]
        for cp in copies:
            cp.start()
        for cp in copies:
            cp.wait()

    spec = pl.BlockSpec(memory_space=pl.ANY)
    return list(pl.pallas_call(
        body, name=name, in_specs=[spec] * n, out_specs=[spec] * n,
        out_shape=[jax.ShapeDtypeStruct(a.shape if whole else a.shape[1:], a.dtype) for a in arrs],
        scratch_shapes=[pltpu.SemaphoreType.DMA((n,)), pltpu.SemaphoreType.DMA((n,))],
    )(*arrs))


_HBM = pl.BlockSpec(memory_space=pltpu.HBM)
_SEM = pl.BlockSpec(memory_space=pltpu.SEMAPHORE)
_EFFECT = pltpu.SideEffectType.DATAFLOW_SIDE_EFFECTING


def _in_hbm(a):
    return pltpu.with_memory_space_constraint(a, pltpu.HBM)


def _split_copies(srcs, lands, send_sems, recv_sems, plan):
    x, y, c = _place()
    copies = []
    for a, (src, land) in enumerate(zip(srcs, lands)):
        steps = plan(x, y, c)
        for k, (pick, slot, to) in enumerate(steps):
            copies.append(pltpu.make_async_remote_copy(
                src_ref=pick(src), dst_ref=slot(land), send_sem=send_sems.at[a * len(steps) + k],
                recv_sem=recv_sems.at[a * len(steps) + k], device_id=to, device_id_type=MESH))
    return copies


def _split_start(srcs, land_shapes, plan, per, *, name):
    n = len(srcs)

    def body(*refs):
        send_sems, recv_sems = refs[2 * n], refs[2 * n + 1]
        for cp in _split_copies(refs[:n], refs[n:2 * n], send_sems, recv_sems, plan):
            cp.start()
        token = refs[-1]
        token[...] = jnp.zeros_like(token)

    lands = [_in_hbm(lax.empty(s.shape, s.dtype)) for s in land_shapes]
    outs = pl.pallas_call(
        body, name=name,
        out_shape=(pltpu.SemaphoreType.DMA((n * per,)), pltpu.SemaphoreType.DMA((n * per,)),
                   *[pltpu.HBM(s.shape, s.dtype) for s in srcs], *[pltpu.HBM(s.shape, s.dtype) for s in land_shapes],
                   jax.ShapeDtypeStruct((8, LANE), F32)),
        in_specs=[_HBM] * (2 * n),
        out_specs=(_SEM, _SEM, *[_HBM] * (2 * n), pl.BlockSpec(memory_space=pltpu.VMEM)),
        input_output_aliases={i: 2 + i for i in range(2 * n)},
        compiler_params=pltpu.CompilerParams(has_side_effects=_EFFECT),
    )(*[_in_hbm(s) for s in srcs], *lands)
    return outs[0], outs[1], list(outs[2:2 + n]), list(outs[2 + n:2 + 2 * n]), outs[-1]


def _split_wait(send_sems, recv_sems, srcs, lands, after, plan, *, name):
    n = len(srcs)

    def body(*refs):
        for cp in _split_copies(refs[:n], refs[n:2 * n], refs[2 * n], refs[2 * n + 1], plan):
            cp.wait_send()
            cp.wait_recv()

    outs = pl.pallas_call(
        body, name=name,
        out_shape=(*[pltpu.HBM(s.shape, s.dtype) for s in srcs], *[pltpu.HBM(s.shape, s.dtype) for s in lands]),
        in_specs=[_HBM] * (2 * n) + [_SEM, _SEM, pl.BlockSpec(memory_space=pl.ANY)],
        out_specs=tuple([_HBM] * (2 * n)),
        input_output_aliases={i: i for i in range(2 * n)},
        compiler_params=pltpu.CompilerParams(has_side_effects=_EFFECT),
    )(*srcs, *lands, send_sems, recv_sems, after)
    return list(outs[:n]), list(outs[n:])


def _gather_plan(x, y, c):
    slot = lambda land: land.at[_dev_index((x, y, c))]
    whole = lambda src: src
    return [(whole, slot, (x, y, 1 - c))] + [(whole, slot, (px, py, c)) for px, py in _other_chips(x, y)]


def _swap_plan(x, y, c):
    return [(lambda src: src.at[1 - c], lambda land: land, (x, y, 1 - c))]


def _exchange_plan(x, y, c):
    return [(lambda src, k=2 * px + py: src.at[k], lambda land, j=j: land.at[j], (px, py, c))
            for j, (px, py) in enumerate(_other_chips(x, y))]


def _gather_forward(lands, *, name):
    n = len(lands)

    def body(*refs):
        lands_in, outs = refs[:n], refs[n:2 * n]
        send_sems, recv_sems = refs[2 * n:]
        x, y, c = _place()
        copies = []
        for a in range(n):
            for j, (px, py) in enumerate(_other_chips(x, y)):
                copies.append((pltpu.make_async_remote_copy(
                    src_ref=lands_in[a].at[_dev_index((px, py, c))], dst_ref=outs[a].at[_dev_index((px, py, c))],
                    send_sem=send_sems.at[3 * a + j], recv_sem=recv_sems.at[3 * a + j],
                    device_id=(x, y, 1 - c), device_id_type=MESH), a, j, (px, py)))
        for cp, _, _, _ in copies:
            cp.start()
        for cp, a, j, (px, py) in copies:
            cp.wait_send()
            pltpu.make_async_remote_copy(
                src_ref=lands_in[a].at[_dev_index((px, py, 1 - c))], dst_ref=outs[a].at[_dev_index((px, py, 1 - c))],
                send_sem=send_sems.at[3 * a + j], recv_sem=recv_sems.at[3 * a + j],
                device_id=(x, y, 1 - c), device_id_type=MESH).wait_recv()

    spec = pl.BlockSpec(memory_space=pl.ANY)
    return list(pl.pallas_call(
        body, name=name, in_specs=[spec] * n, out_specs=[spec] * n,
        out_shape=[jax.ShapeDtypeStruct(a.shape, a.dtype) for a in lands],
        input_output_aliases={a: a for a in range(n)},
        scratch_shapes=[pltpu.SemaphoreType.DMA((3 * n,)), pltpu.SemaphoreType.DMA((3 * n,))],
    )(*lands))


def _flat2(a, lead):
    return a.reshape(a.shape[:lead] + (-1, a.shape[-1]))


def _pair_sum(g, recv, c_idx, *, name):
    _, nchip, r, w = g.shape
    tm = _tile(r, 256) if r % 8 == 0 else r

    def body(c_ref, g_ref, r_ref, o_ref):
        o_ref[...] = (g_ref[...].astype(F32) + r_ref[...].astype(F32)).astype(o_ref.dtype)

    return pl.pallas_call(
        body, name=name,
        grid_spec=pltpu.PrefetchScalarGridSpec(
            num_scalar_prefetch=1, grid=(nchip, r // tm),
            in_specs=[pl.BlockSpec((None, None, tm, w), lambda k, i, c_ref: (c_ref[0], k, i, 0)),
                      pl.BlockSpec((None, tm, w), lambda k, i, c_ref: (k, i, 0))],
            out_specs=pl.BlockSpec((None, tm, w), lambda k, i, c_ref: (k, i, 0))),
        out_shape=jax.ShapeDtypeStruct((nchip, r, w), BF16),
        compiler_params=_params(("parallel", "parallel")),
    )(c_idx, g, recv)


def _chip_sum(s1, recv, chip_idx, *, name):
    _, r, w = s1.shape
    tm = _tile(r, 256) if r % 8 == 0 else r

    def body(k_ref, s_ref, r_ref, o_ref):
        acc = s_ref[...].astype(F32)
        for j in range(3):
            acc = acc + r_ref[j].astype(F32)
        o_ref[...] = acc

    return pl.pallas_call(
        body, name=name,
        grid_spec=pltpu.PrefetchScalarGridSpec(
            num_scalar_prefetch=1, grid=(r // tm,),
            in_specs=[pl.BlockSpec((None, tm, w), lambda i, k_ref: (k_ref[0], i, 0)),
                      pl.BlockSpec((3, tm, w), lambda i, k_ref: (0, i, 0))],
            out_specs=pl.BlockSpec((tm, w), lambda i, k_ref: (i, 0))),
        out_shape=jax.ShapeDtypeStruct((r, w), F32),
        compiler_params=_params(("parallel",)),
    )(chip_idx, s1, recv)


def _adam_math(w, g, m, v):
    m = ADAM_B1 * m + (1.0 - ADAM_B1) * g
    v = ADAM_B2 * v + (1.0 - ADAM_B2) * (g * g)
    m_hat = m / (1.0 - ADAM_B1 ** ADAM_STEP)
    v_hat = v / (1.0 - ADAM_B2 ** ADAM_STEP)
    delta = -ADAM_LR * (m_hat / (jnp.sqrt(v_hat) + ADAM_EPS) + ADAM_WD * w)
    return delta, m, v


def _adamw(w, mine, other, c_idx, m, v, *, name):
    r, cw = w.shape
    hr = r // 2
    tm = _row_tile(hr, 9 * cw * 4)

    def body(c_ref, w_ref, a_ref, b_ref, m_ref, v_ref, g_ref, d_ref, nm_ref, nv_ref):
        g = jnp.where(pl.program_id(0) == c_ref[0], a_ref[...], b_ref[...])
        g_ref[...] = g
        d_ref[...], nm_ref[...], nv_ref[...] = _adam_math(w_ref[...], g, m_ref[...], v_ref[...])

    per_half = hr // tm
    full = pl.BlockSpec((tm, cw), lambda h, i, c_ref: (h * per_half + i, 0))
    mine_spec = pl.BlockSpec((tm, cw), lambda h, i, c_ref: (jnp.where(h == c_ref[0], i, 0), 0))
    other_spec = pl.BlockSpec((tm, cw), lambda h, i, c_ref: (jnp.where(h == c_ref[0], 0, i), 0))
    return pl.pallas_call(
        body, name=name,
        grid_spec=pltpu.PrefetchScalarGridSpec(
            num_scalar_prefetch=1, grid=(2, per_half),
            in_specs=[full, mine_spec, other_spec, full, full], out_specs=[full] * 4),
        out_shape=[jax.ShapeDtypeStruct((r, cw), F32)] * 4,
        compiler_params=_params(("parallel", "parallel")),
    )(c_idx, w, mine, other, m, v)


def _adamw_ada(cact_t, dada, w, m, v):
    r, cw = w.shape
    nb = cact_t.shape[1]
    tm = _tile(r, 256)
    tn = _tile(cw, 1024)

    def body(a_ref, d_ref, w_ref, m_ref, v_ref, g_ref, dl_ref, nm_ref, nv_ref):
        a = a_ref[...]
        d = d_ref[...]
        g = a[:, 0:1] * d[0:1, :]
        for b in range(1, nb):
            g = g + a[:, b:b + 1] * d[b:b + 1, :]
        g_ref[...] = g
        dl_ref[...], nm_ref[...], nv_ref[...] = _adam_math(w_ref[...], g, m_ref[...], v_ref[...])

    blk = pl.BlockSpec((tm, tn), lambda i, j: (i, j))
    return pl.pallas_call(
        body, name="adamw_ada", grid=(r // tm, cw // tn),
        in_specs=[pl.BlockSpec((tm, nb), lambda i, j: (i, 0)), pl.BlockSpec((nb, tn), lambda i, j: (0, j)), blk, blk, blk],
        out_specs=[blk] * 4, out_shape=[jax.ShapeDtypeStruct((r, cw), F32)] * 4,
        compiler_params=_params(("parallel", "parallel")),
    )(cact_t, dada, w, m, v)


def _adamw_vec(parts, w, m, v):
    n = w.shape[1]

    def body(p_ref, w_ref, m_ref, v_ref, g_ref, d_ref, nm_ref, nv_ref):
        p = p_ref[...]
        g = p[0:1, :]
        for b in range(1, N_DEV):
            g = g + p[b:b + 1, :]
        g_ref[...] = g
        d_ref[...], nm_ref[...], nv_ref[...] = _adam_math(w_ref[...], g, m_ref[...], v_ref[...])

    return pl.pallas_call(
        body, name="adamw_vec", out_shape=[jax.ShapeDtypeStruct((1, n), F32)] * 4,
        compiler_params=pltpu.CompilerParams(vmem_limit_bytes=VMEM_LIMIT),
    )(parts, w, m, v)


def _w_in_segments(kpe0, d_in, cs):
    segs = []
    for k in range(4):
        lo, hi = k * cs, (k + 1) * cs
        for a, b, shift in ((0, kpe0, 0), (kpe0, kpe0 + ROPE, d_in - ROPE - kpe0), (kpe0 + ROPE, d_in, -ROPE)):
            a, b = max(lo, a), min(hi, b)
            if a < b:
                segs.append((k, a - lo, a + shift, b - a))
    return segs


def _w_in_layout(g8, kpe0):
    _, hr, cs = g8.shape
    rows, d_in = 2 * hr, 4 * cs
    segs = _w_in_segments(kpe0, d_in, cs)
    tm = _tile(rows, 256)

    def body(g_ref, o_ref):
        for k, src, dst, w in segs:
            o_ref[:, dst:dst + w] = g_ref[k, :, src:src + w]
        o_ref[:, d_in:] = jnp.zeros((tm, ROPE), o_ref.dtype)

    return pl.pallas_call(
        body, name="w_in_layout", grid=(rows // tm,),
        in_specs=[pl.BlockSpec((4, tm, cs), lambda i: (0, i, 0))], out_specs=_rows(tm, d_in + ROPE),
        out_shape=jax.ShapeDtypeStruct((rows, d_in + ROPE), g8.dtype), compiler_params=_params(("parallel",)),
    )(g8.reshape(4, rows, cs))


def _w_in_grad_pieces(g, kpe0):
    rows, d_in_p = g.shape
    d_in = d_in_p - ROPE
    cs = d_in // 4
    segs = _w_in_segments(kpe0, d_in, cs)
    hr = rows // 2
    tm = _tile(hr, 256)
    per_half = hr // tm

    def body(g_ref, o_ref):
        for k, src, dst, w in segs:
            o_ref[k, :, src:src + w] = g_ref[:, dst:dst + w]

    return pl.pallas_call(
        body, name="w_in_grad_pieces", grid=(rows // tm,),
        in_specs=[_rows(tm, d_in_p)],
        out_specs=pl.BlockSpec((None, 4, tm, cs), lambda i: (i // per_half, 0, i % per_half, 0)),
        out_shape=jax.ShapeDtypeStruct((2, 4, hr, cs), g.dtype), compiler_params=_params(("parallel",)),
    )(g)


def _interleave_layout(g8, ib):
    _, hr, cs = g8.shape
    rows, per_chip, per_half = 2 * hr, cs // ib, 2 * cs // ib
    tm = _tile(rows, 2048)

    def src(jj):
        return jj // 2 + per_half * (jj % 2)

    def body(g_ref, o_ref):
        o_ref[...] = g_ref[...]

    return pl.pallas_call(
        body, name="interleave_layout", grid=(rows // tm, 4 * per_chip),
        in_specs=[pl.BlockSpec((None, tm, ib), lambda i, jj: (src(jj) // per_chip, i, src(jj) % per_chip))],
        out_specs=pl.BlockSpec((tm, ib), lambda i, jj: (i, jj)),
        out_shape=jax.ShapeDtypeStruct((rows, 4 * cs), g8.dtype), compiler_params=_params(("parallel", "parallel")),
    )(g8.reshape(4, rows, cs))


def _cols_from_chips(g8, rows):
    cs = g8.shape[-1]
    return g8.reshape(4, rows, cs).transpose(1, 0, 2).reshape(rows, 4 * cs)


def _cols_to_pieces(g):
    rows, c4 = g.shape
    return g.reshape(2, rows // 2, 4, c4 // 4).transpose(0, 2, 1, 3)


def _rows_to_pieces(g):
    r4, cols = g.shape
    return g.reshape(4, 2, r4 // 8, cols).transpose(1, 0, 2, 3)


def _pad_cols(a, w):
    return jnp.pad(a, ((0, 0), (0, w - a.shape[1])))


def kernel(x, c, positions, w_ada, b_ada, g_norm1, g_norm2, w_in, g_q_latent, g_kv_latent, w_uq, w_ukv, g_q_head, g_k_head, w_proj_mla, w_proj_sb, w_out, w_ffn_in, w_ffn_out, loss_target, m_w_ada, m_b_ada, m_g_norm1, m_g_norm2, m_w_in, m_g_q_latent, m_g_kv_latent, m_w_uq, m_w_ukv, m_g_q_head, m_g_k_head, m_w_proj_mla, m_w_proj_sb, m_w_out, m_w_ffn_in, m_w_ffn_out, v_w_ada, v_b_ada, v_g_norm1, v_g_norm2, v_w_in, v_g_q_latent, v_g_kv_latent, v_w_uq, v_w_ukv, v_g_q_head, v_g_k_head, v_w_proj_mla, v_w_proj_sb, v_w_out, v_w_ffn_in, v_w_ffn_out):
    xi, yi, ci = _place()
    chip = 2 * xi + yi
    dev = 2 * chip + ci
    c_idx = jnp.reshape(ci, (1,)).astype(jnp.int32)
    chip_idx = jnp.reshape(chip, (1,)).astype(jnp.int32)

    x = x[0]
    tgt = loss_target[0]
    S, D = x.shape
    ql = g_q_latent.shape[1]
    assert g_kv_latent.shape[1] == ql
    mlaw = w_proj_mla.shape[1]
    nh = mlaw // HEAD
    sbw = w_proj_sb.shape[1]
    assert sbw == mlaw
    dff = w_ffn_out.shape[1] * 4
    d_in = 2 * ql + ROPE + 3 * sbw + 2 * D
    d_in_p = d_in + ROPE
    q_col = (2 * ql) // HEAD
    k_col = q_col + nh
    v_col = k_col + nh
    gla_col = (2 * ql + 3 * sbw) // D
    glb_col = gla_col + 1
    kpe_col = (d_in - ROPE) // LANE
    assert (2 * ql + 3 * sbw) % D == 0 and (d_in - ROPE) % LANE == 0

    mats = {"w_in": w_in[0], "w_uq": w_uq[0], "w_ukv": w_ukv[0], "w_proj_mla": w_proj_mla[0],
            "w_proj_sb": w_proj_sb[0], "w_out": w_out[0], "w_ffn_in": w_ffn_in[0], "w_ffn_out": w_ffn_out[0]}
    names = list(mats)
    row_sharded = {"w_out", "w_ffn_out"}

    c_all = _gather_blocks([jnp.broadcast_to(c, (8, D))], name="gather_cond", in_vmem=True)[0][:, 0, :]
    n_ada = w_ada.shape[2]
    b_shard = lax.dynamic_slice_in_dim(b_ada, chip * n_ada, n_ada, axis=1)
    ada_shard = _mm(c_all, w_ada[0], name="ada_proj", a_fn=jax.nn.silu, bias=b_shard)
    ada_all = _gather_blocks([ada_shard], name="gather_ada", in_vmem=True)[0]
    ada_rows = lax.dynamic_index_in_dim(ada_all, dev, axis=1, keepdims=False)
    ada = ada_rows[0::2].reshape(1, 4 * n_ada)
    SH1, SC1, GT1, SH2, SC2, GT2 = range(6)

    def after(dep, a):
        return a + (dep.reshape(-1)[0:1].reshape((1,) * a.ndim) * 0).astype(a.dtype)

    def fill_own(g8, own):
        return lax.dynamic_update_index_in_dim(g8, own, dev, 0)

    halves = []
    for nm in names:
        w = mats[nm]
        hr = w.shape[0] // 2
        halves.append(lax.dynamic_slice_in_dim(w, ci * hr, hr, axis=0).astype(BF16))
    half_of = dict(zip(names, halves))
    early = ["w_in", "w_uq", "w_ukv"]
    late = ["w_proj_mla", "w_proj_sb", "w_out", "w_ffn_in", "w_ffn_out"]
    early_halves = [half_of[nm] for nm in early]
    early_halves[0] = after(ada, early_halves[0])
    early_got = _gather_blocks(early_halves, name="gather_weights", in_vmem=False)
    gathered = {nm: fill_own(g8, own) for nm, g8, own in zip(early, early_got, early_halves)}
    late_halves = [half_of[nm] for nm in late]
    late_halves[0] = after(gathered[early[1]], late_halves[0])
    late_send, late_recv, late_srcs, late_lands, late_token = _split_start(
        late_halves, [jax.ShapeDtypeStruct((N_DEV,) + h.shape, h.dtype) for h in late_halves], _gather_plan, 4,
        name="gather_late_start")
    ada = ada + late_token[0:1, 0:1]

    def full_cols(nm):
        return _cols_from_chips(gathered[nm], mats[nm].shape[0])

    kpe0 = 2 * ql
    w_in_p = _w_in_layout(gathered["w_in"], kpe0)
    w_uq_p = jnp.pad(full_cols("w_uq").reshape(ql, nh, QK_DIM), ((0, 0), (0, 0), (0, HEAD_PAD - QK_DIM))
                     ).reshape(ql, nh * HEAD_PAD)
    w_ukv4 = full_cols("w_ukv").reshape(ql, nh, 2 * HEAD)
    w_ukv_p = jnp.concatenate([w_ukv4[:, :, :HEAD].reshape(ql, mlaw), w_ukv4[:, :, HEAD:].reshape(ql, mlaw)], axis=1)

    half = ROPE // 2
    freqs = ROPE_THETA ** (-jnp.arange(half, dtype=F32) / half)
    ang = positions[0].astype(F32)[:, None] * freqs
    cos, sin = jnp.cos(ang), jnp.sin(ang)
    one = jnp.ones((S, NOPE), F32)
    zero = jnp.zeros((S, NOPE), F32)
    zh = jnp.zeros((S, half), F32)
    tabs = (jnp.concatenate([one, cos, cos, one[:, :HEAD_PAD - QK_DIM]], axis=1),
            jnp.concatenate([zero, zh, sin, zero[:, :HEAD_PAD - QK_DIM]], axis=1),
            jnp.concatenate([zero, -sin, zh, zero[:, :HEAD_PAD - QK_DIM]], axis=1))
    g_qh_p = _pad_cols(g_q_head, HEAD_PAD)
    g_kh_p = _pad_cols(g_k_head, HEAD_PAD)

    h1 = _rmsmod(x, g_norm1, ada, SC1, SH1, name="rmsmod1")
    proj = _mm(h1, w_in_p, name="mm_proj", tn=640, out_dtype=BF16)
    cqn, ckvn = _latent_norm(proj, g_q_latent, g_kv_latent, ql)
    q0 = _mm(cqn, w_uq_p, name="mm_q_up", out_dtype=BF16)
    kv0 = _mm(ckvn, w_ukv_p, name="mm_kv_up", out_dtype=BF16)
    q = _q_prep(q0, g_qh_p, tabs, nh)
    k = _k_prep(kv0, proj, kpe_col, g_kh_p, tabs, nh)
    y_a, lse = _mla_fwd(q, k, kv0, nh)
    y_b, sb_runs = _sb_fwd(proj, q_col, k_col, v_col, nh)
    late_srcs, late_lands = _split_wait(late_send, late_recv, late_srcs, late_lands, y_b, _gather_plan,
                                        name="gather_late_wait")
    late_got = _gather_forward(late_lands, name="gather_late_forward")
    gathered.update({nm: fill_own(g8, own) for nm, g8, own in zip(late, late_got, late_srcs)})
    w_pm = full_cols("w_proj_mla")
    w_ps = full_cols("w_proj_sb")
    w_o = gathered["w_out"].reshape(D, D)
    ib = 256 if (dff // 2) % 256 == 0 else LANE
    nb = dff // ib
    w_fi = _interleave_layout(gathered["w_ffn_in"], ib)
    w_fo = gathered["w_ffn_out"].reshape(dff, D)
    pa = _mm(y_a, w_pm, name="mm_proj_mla", out_dtype=BF16)
    pb = _mm(y_b, w_ps, name="mm_proj_sb", out_dtype=BF16)
    merged = _gate_merge(pa, pb, proj, gla_col, glb_col)
    o = _mm(merged, w_o, name="mm_out")
    x2, h2 = _resid_rmsmod(x, o, g_norm2, ada, GT1, SC2, SH2)
    ff, act = _mm(h2, w_fi, name="mm_ffn_in", tn=4 * ib,
                  fused=(_swiglu_tile(ib), [], [(2 * dff, 4 * ib, BF16), (dff, 2 * ib, BF16)]))
    f = _mm(act, w_fo, name="mm_ffn_out")
    dy, df, red_l, loss_p = _loss_head(x2, f, tgt, ada, GT2)

    dff_, = _mm(df, w_fo, name="mm_d_act", tb=True, tn=2 * ib,
                fused=(_swiglu_bwd_tile(ib), [(ff, 4 * ib)], [(2 * dff, 4 * ib, BF16)]))
    def pc(kind):
        if kind == "cols":
            return kind
        return kind if (D // 4) % LANE == 0 and (dff // 4) % LANE == 0 else None

    gw_fo = _mm(act, df, name="mm_gw_ffn_out", ta=True, out_dtype=BF16, pieces=pc("rows"))
    dh2 = _mm(dff_, w_fi, name="mm_d_h2", tb=True)
    gw_fi = _mm(h2, dff_, name="mm_gw_ffn_in", ta=True, out_dtype=BF16, pieces="cols", tn=ib,
                col_perm=lambda jj: jj // 2 + nb * (jj % 2))

    def to_pieces(nms, grads):
        return [g if g.ndim == 4 else (_rows_to_pieces if nm in row_sharded else _cols_to_pieces)(g)
                for nm, g in zip(nms, grads)]

    def pair_sums(nms, pcs, got):
        return [_pair_sum(p, r, c_idx, name="rs_pair_sum_" + nm) for p, r, nm in zip(pcs, got, nms)]

    def swap_start(pcs, tag):
        return _split_start(pcs, [jax.ShapeDtypeStruct(p.shape[1:], p.dtype) for p in pcs], _swap_plan, 1,
                            name="rs_swap_%s_start" % tag)

    def exchange_start(pair, tag):
        return _split_start(pair, [jax.ShapeDtypeStruct((3,) + p.shape[1:], p.dtype) for p in pair], _exchange_plan, 3,
                            name="rs_exchange_%s_start" % tag)

    ffn = ["w_ffn_in", "w_ffn_out"]
    sw = swap_start(to_pieces(ffn, [gw_fi, gw_fo]), "ffn")
    ada = ada + sw[4][0:1, 0:1]
    dx2, do, red_2 = _rmsmod2_bwd(dh2, x2, dy, o, g_norm2, ada, SC2, GT1)
    ffn_pcs, ffn_got = _split_wait(sw[0], sw[1], sw[2], sw[3], dx2, _swap_plan, name="rs_swap_ffn_wait")
    ffn_send, ffn_recv, ffn_pair, ffn_lands, ffn_token = exchange_start(pair_sums(ffn, ffn_pcs, ffn_got), "ffn")
    dmerged = _mm(do, w_o, name="mm_d_merged", tb=True, out_dtype=BF16,
                  bias=jnp.zeros((1, D), F32) + ffn_token[0:1, 0:1])
    gw_o = _mm(merged, do, name="mm_gw_out", ta=True, out_dtype=BF16, pieces=pc("rows"))
    dpa, dpb, dgla, dglb = _gate_bwd(dmerged, pa, pb, proj, gla_col, glb_col)
    gw_pm = _mm(y_a, dpa, name="mm_gw_proj_mla", ta=True, out_dtype=BF16, pieces=pc("cols"))
    gw_ps = _mm(y_b, dpb, name="mm_gw_proj_sb", ta=True, out_dtype=BF16, pieces=pc("cols"))
    mid = ["w_proj_mla", "w_proj_sb", "w_out"]
    mid_pcs = to_pieces(mid, [gw_pm, gw_ps, gw_o])
    mid_pair = pair_sums(mid, mid_pcs, _sibling_swap(mid_pcs, name="rs_sibling_swap_mid"))
    mid_send, mid_recv, mid_pair, mid_lands, mid_token = exchange_start(mid_pair, "mid")
    behind_mid = jnp.zeros((1, mlaw), F32) + mid_token[0:1, 0:1]
    dya = _mm(dpa, w_pm, name="mm_d_ya", tb=True, out_dtype=BF16, bias=behind_mid)
    dyb = _mm(dpb, w_ps, name="mm_d_yb", tb=True, out_dtype=BF16, bias=behind_mid)
    dq, dk, dv = _mla_bwd(q, k, kv0, y_a, dya, lse, nh)
    dq_sb, dk_sb, dv_sb = _sb_bwd(proj, q_col, k_col, v_col, dyb, sb_runs, nh)
    dq0, red_qh = _q_prep_bwd(dq, q0, g_qh_p, tabs, nh)
    dkv0, dkpe, red_kh = _k_prep_bwd(dk, dv, kv0, proj, kpe_col, g_kh_p, tabs, nh)
    dcqn = _mm(dq0, w_uq_p, name="mm_d_cqn", tb=True, out_dtype=BF16)
    gw_uq_p = _mm(cqn, dq0, name="mm_gw_uq", ta=True, out_dtype=BF16)
    dckvn = _mm(dkv0, w_ukv_p, name="mm_d_ckvn", tb=True, out_dtype=BF16)
    gw_ukv_p = _mm(ckvn, dkv0, name="mm_gw_ukv", ta=True, out_dtype=BF16)
    dcq, dckv, red_lat = _latent_norm_bwd(dcqn, dckvn, proj, g_q_latent, g_kv_latent, ql)
    dproj = jnp.concatenate([dcq, dckv, dq_sb.astype(BF16), dk_sb.astype(BF16), dv_sb.astype(BF16),
                             dgla, dglb, dkpe], axis=1)
    gw_in_p = _mm(h1, dproj, name="mm_gw_in", ta=True, out_dtype=BF16, tn=640)

    gw_in = _w_in_grad_pieces(gw_in_p, kpe0)
    gw_uq = gw_uq_p.reshape(ql, nh, HEAD_PAD)[:, :, :QK_DIM].reshape(ql, nh * QK_DIM)
    gw_ukv = jnp.concatenate([gw_ukv_p[:, :mlaw].reshape(ql, nh, HEAD), gw_ukv_p[:, mlaw:].reshape(ql, nh, HEAD)],
                             axis=2).reshape(ql, 2 * mlaw)
    last = ["w_in", "w_uq", "w_ukv"]
    assert last + mid + ffn == names

    last_pcs = to_pieces(last, [gw_in, gw_uq, gw_ukv])
    last_pair = pair_sums(last, last_pcs, _sibling_swap(last_pcs, name="rs_sibling_swap_last"))
    last_send, last_recv, last_pair, last_lands, last_token = exchange_start(last_pair, "last")
    ada = ada + last_token[0:1, 0:1]
    dh1 = _mm(dproj, w_in_p, name="mm_d_h1", tb=True, bias=jnp.zeros((1, D), F32) + last_token[0:1, 0:1])
    grad_x, red_1 = _rmsmod1_bwd(dh1, x, dx2, g_norm1, ada, SC1)
    last_pair, last_chips = _split_wait(last_send, last_recv, last_pair, last_lands, grad_x, _exchange_plan,
                                        name="rs_exchange_last_wait")
    mid_pair, mid_chips = _split_wait(mid_send, mid_recv, mid_pair, mid_lands, grad_x, _exchange_plan,
                                      name="rs_exchange_mid_wait")
    ffn_pair, ffn_chips = _split_wait(ffn_send, ffn_recv, ffn_pair, ffn_lands, grad_x, _exchange_plan,
                                      name="rs_exchange_ffn_wait")
    reduced = [_chip_sum(s, r, chip_idx, name="rs_chip_sum_" + nm)
               for s, r, nm in zip(last_pair + mid_pair + ffn_pair, last_chips + mid_chips + ffn_chips, names)]
    from_sibling2 = _sibling_swap(reduced, name="rs_sibling_send", whole=True)

    vec_names = ["b_ada", "g_norm1", "g_norm2", "g_q_latent", "g_kv_latent", "g_q_head", "g_k_head"]
    vec_w = dict(b_ada=b_ada, g_norm1=g_norm1, g_norm2=g_norm2, g_q_latent=g_q_latent, g_kv_latent=g_kv_latent,
                 g_q_head=g_q_head, g_k_head=g_k_head)
    vec_m = dict(b_ada=m_b_ada, g_norm1=m_g_norm1, g_norm2=m_g_norm2, g_q_latent=m_g_q_latent,
                 g_kv_latent=m_g_kv_latent, g_q_head=m_g_q_head, g_k_head=m_g_k_head)
    vec_v = dict(b_ada=v_b_ada, g_norm1=v_g_norm1, g_norm2=v_g_norm2, g_q_latent=v_g_q_latent,
                 g_kv_latent=v_g_kv_latent, g_q_head=v_g_q_head, g_k_head=v_g_k_head)
    d_ada = jnp.concatenate([red_1[0:1], red_1[1:2], red_2[3:4], red_2[0:1], red_2[1:2], red_l[0:1]], axis=1)
    vec_parts = dict(b_ada=d_ada, g_norm1=red_1[2:3], g_norm2=red_2[2:3], g_q_latent=red_lat[0:1],
                     g_kv_latent=red_lat[1:2], g_q_head=red_qh[0:1], g_k_head=red_kh[0:1])
    widths = [-(-vec_w[nm].shape[1] // LANE) * LANE for nm in vec_names]
    offs = [sum(widths[:i]) for i in range(len(widths))]
    pack = lambda d: jnp.concatenate([_pad_cols(d[nm][:, :vec_w[nm].shape[1]], wd) for nm, wd in zip(vec_names, widths)], axis=1)
    nvec = sum(widths) + LANE
    no_loss = jnp.zeros((1, LANE), F32)
    parts = jnp.concatenate([pack(vec_parts), loss_p[0:1, :]], axis=1)
    parts_all = _gather_blocks([jnp.broadcast_to(parts, (8, nvec))], name="gather_vec_grads",
                               in_vmem=True)[0][:, 0, :]
    gvec, dvec, nmvec, nvvec = _adamw_vec(parts_all, *[jnp.concatenate([pack(d), no_loss], axis=1)
                                                       for d in (vec_w, vec_m, vec_v)])
    loss = gvec[0, nvec - LANE]
    unpack = lambda a: {nm: a[:, o_:o_ + vec_w[nm].shape[1]] for nm, o_ in zip(vec_names, offs)}
    gvec, dvec, nmvec, nvvec = unpack(gvec), unpack(dvec), unpack(nmvec), unpack(nvvec)

    dada_all = lax.dynamic_slice_in_dim(parts_all[:, :6 * D], chip * n_ada, n_ada, axis=1)
    cact_t = jax.nn.silu(c_all).T
    g_ada, d_ada_w, nm_ada, nv_ada = _adamw_ada(cact_t, dada_all, w_ada[0], m_w_ada[0], v_w_ada[0])

    ms = dict(w_in=m_w_in, w_uq=m_w_uq, w_ukv=m_w_ukv, w_proj_mla=m_w_proj_mla, w_proj_sb=m_w_proj_sb,
              w_out=m_w_out, w_ffn_in=m_w_ffn_in, w_ffn_out=m_w_ffn_out)
    vs = dict(w_in=v_w_in, w_uq=v_w_uq, w_ukv=v_w_ukv, w_proj_mla=v_w_proj_mla, w_proj_sb=v_w_proj_sb,
              w_out=v_w_out, w_ffn_in=v_w_ffn_in, w_ffn_out=v_w_ffn_out)
    G, DL, NM, NV = {}, {}, {}, {}
    for nm, mine, other in zip(names, reduced, from_sibling2):
        g_, d_, m_, v_ = _adamw(mats[nm], mine, other, c_idx, ms[nm][0], vs[nm][0], name="adamw_" + nm)
        G[nm], DL[nm], NM[nm], NV[nm] = g_[None], d_[None], m_[None], v_[None]
    G["w_ada"], DL["w_ada"], NM["w_ada"], NV["w_ada"] = g_ada[None], d_ada_w[None], nm_ada[None], nv_ada[None]
    for nm in vec_names:
        G[nm], DL[nm], NM[nm], NV[nm] = gvec[nm], dvec[nm], nmvec[nm], nvvec[nm]

    order = ["w_ada", "b_ada", "g_norm1", "g_norm2", "w_in", "g_q_latent", "g_kv_latent", "w_uq", "w_ukv",
             "g_q_head", "g_k_head", "w_proj_mla", "w_proj_sb", "w_out", "w_ffn_in", "w_ffn_out"]
    return (loss, grad_x[None], *[G[n] for n in order], *[DL[n] for n in order],
            *[NM[n] for n in order], *[NV[n] for n in order])
```

```python
import functools
import math

import jax
import jax.numpy as jnp
from jax import lax
from jax.experimental import pallas as pl
from jax.experimental.pallas import tpu as pltpu

F32 = jnp.float32
BF16 = jnp.bfloat16
MESH = pl.DeviceIdType.MESH

EPS = 1e-6
ROPE_THETA = 10000.0
NOPE = 128
ROPE = 64
QK_DIM = NOPE + ROPE
HEAD_PAD = 256
HEAD = 128
N_DEV = 8
LANE = 128
VMEM_LIMIT = 48 * 1024 * 1024

ADAM_LR = 0.001
ADAM_B1 = 0.9
ADAM_B2 = 0.999
ADAM_EPS = 1e-08
ADAM_WD = 0.01
ADAM_STEP = 10


def _tile(n, target):
    if n <= target:
        return n
    t = (target // LANE) * LANE
    while t >= LANE:
        if n % t == 0:
            return t
        t -= LANE
    return n


def _row_tile(rows, row_bytes, budget=24 * 1024 * 1024):
    cap = max(8, budget // (2 * row_bytes))
    best = None
    for t in range(8, min(rows, cap) + 1, 8):
        if rows % t == 0:
            best = t
    return best if best is not None else rows


def _params(sem):
    return pltpu.CompilerParams(dimension_semantics=sem, vmem_limit_bytes=VMEM_LIMIT)


def _rows(tm, w, col=0):
    return pl.BlockSpec((tm, w), lambda i: (i, col))


def _vec(w, col=0, rows=1):
    return pl.BlockSpec((rows, w), lambda i: (0, col))


MM_VMEM_BUDGET = 36 * 1024 * 1024


def _mm(a, b, *, name, ta=False, tb=False, out_dtype=F32, a_fn=None, bias=None, tm=1024, tn=1024, pieces=None,
        col_perm=None, fused=None):
    M = a.shape[1] if ta else a.shape[0]
    K = a.shape[0] if ta else a.shape[1]
    N = b.shape[0] if tb else b.shape[1]
    assert K == (b.shape[1] if tb else b.shape[0]), (a.shape, b.shape, ta, tb)
    if pieces == "cols":
        tm, tn = _tile(M // 2, tm), _tile(N // 4, tn)
        assert (M // 2) % tm == 0 and (N // 4) % tn == 0
    elif pieces == "rows":
        tm, tn = M // 4, _tile(N, tn)
    else:
        tm, tn = _tile(M, tm), _tile(N, tn)
    sa, sb, so = a.dtype.itemsize, b.dtype.itemsize, jnp.dtype(out_dtype).itemsize

    def fits(tk):
        return 2 * tk * (tm * sa + tn * sb) + tm * tn * (2 * so + 4) <= MM_VMEM_BUDGET

    tk = K
    while not fits(tk):
        smaller = _tile(K, tk - LANE)
        if smaller >= tk:
            break
        tk = smaller
    nk = K // tk
    dn = (((0 if ta else 1,), (1 if tb else 0,)), ((), ()))
    b_outer = nk == 1 and a.size * sa * (N // tn) < b.size * sb * (M // tm)

    n_extra = len(fused[1]) if fused else 0
    n_out = len(fused[2]) if fused else 1

    def body(*refs):
        a_ref, b_ref = refs[:2]
        bias_ref = refs[2] if bias is not None else None
        first = 3 if bias is not None else 2
        extra_refs = refs[first:first + n_extra]
        out_refs = refs[first + n_extra:first + n_extra + n_out]
        o_ref = out_refs[0]
        av = a_ref[...]
        if a_fn is not None:
            av = a_fn(av.astype(F32))
        part = lax.dot_general(av.astype(BF16), b_ref[...].astype(BF16), dn, preferred_element_type=F32)

        def finish(r):
            if bias is not None:
                r = r + bias_ref[...]
            if fused:
                for ref, tile in zip(out_refs, fused[0](r, *[e[...] for e in extra_refs])):
                    ref[...] = tile.astype(ref.dtype)
            elif pieces == "rows":
                o_ref[0] = r[:tm // 2].astype(o_ref.dtype)
                o_ref[1] = r[tm // 2:].astype(o_ref.dtype)
            else:
                o_ref[...] = r.astype(o_ref.dtype)

        if nk == 1:
            finish(part)
        else:
            acc_ref = refs[-1]
            k = pl.program_id(2)

            @pl.when(k == 0)
            def _():
                acc_ref[...] = part

            @pl.when(k > 0)
            def _():
                acc_ref[...] += part

            @pl.when(k == nk - 1)
            def _():
                finish(acc_ref[...])

    def ij(g0, g1):
        return (g1, g0) if b_outer else (g0, g1)

    def amap(g0, g1, k):
        i, _ = ij(g0, g1)
        return (k, i) if ta else (i, k)

    def bmap(g0, g1, k):
        _, j = ij(g0, g1)
        return (j, k) if tb else (k, j)

    in_specs = [pl.BlockSpec((tk, tm) if ta else (tm, tk), amap), pl.BlockSpec((tn, tk) if tb else (tk, tn), bmap)]
    args = [a, b]
    if bias is not None:
        in_specs.append(pl.BlockSpec((1, tn), lambda g0, g1, k: (0, ij(g0, g1)[1])))
        args.append(bias)
    grid = (N // tn, M // tm, nk) if b_outer else (M // tm, N // tn, nk)
    if pieces == "cols":
        ni, nj = M // 2 // tm, N // 4 // tn

        def omap(g0, g1, k):
            i, j = ij(g0, g1)
            j = col_perm(j) if col_perm else j
            return (i // ni, j // nj, i % ni, j % nj)

        out_spec = pl.BlockSpec((None, None, tm, tn), omap)
        out_shape = jax.ShapeDtypeStruct((2, 4, M // 2, N // 4), out_dtype)
    elif pieces == "rows":
        out_spec = pl.BlockSpec((2, None, tm // 2, tn), lambda g0, g1, k: (0, ij(g0, g1)[0], 0, ij(g0, g1)[1]))
        out_shape = jax.ShapeDtypeStruct((2, 4, tm // 2, N), out_dtype)
    else:
        out_spec = pl.BlockSpec((tm, tn), lambda g0, g1, k: ij(g0, g1))
        out_shape = jax.ShapeDtypeStruct((M, N), out_dtype)
    if fused:
        for arr, width in fused[1]:
            in_specs.append(pl.BlockSpec((tm, width), lambda g0, g1, k: ij(g0, g1)))
            args.append(arr)
        out_spec = [pl.BlockSpec((tm, width), lambda g0, g1, k: ij(g0, g1)) for _, width, _ in fused[2]]
        out_shape = [jax.ShapeDtypeStruct((M, cols), dt) for cols, _, dt in fused[2]]
    return pl.pallas_call(
        body, name=name, grid=grid, in_specs=in_specs, out_specs=out_spec, out_shape=out_shape,
        scratch_shapes=[pltpu.VMEM((tm, tn), F32)] if nk > 1 else [],
        compiler_params=_params(("parallel", "parallel", "arbitrary")),
    )(*args)


def _rms_rows(v):
    return lax.rsqrt(jnp.mean(v * v, axis=-1, keepdims=True) + EPS)


def _rmsmod(x, g, ada, sc_col, sh_col, *, name):
    S, D = x.shape
    tm = _tile(S, 256)

    def body(x_ref, g_ref, sc_ref, sh_ref, h_ref):
        xv = x_ref[...]
        h = (xv * _rms_rows(xv) * g_ref[...]) * (1.0 + sc_ref[...]) + sh_ref[...]
        h_ref[...] = h.astype(h_ref.dtype)

    return pl.pallas_call(
        body, name=name, grid=(S // tm,),
        in_specs=[_rows(tm, D), _vec(D), _vec(D, sc_col), _vec(D, sh_col)],
        out_specs=_rows(tm, D), out_shape=jax.ShapeDtypeStruct((S, D), BF16),
        compiler_params=_params(("parallel",)),
    )(x, g, ada, ada)


def _latent_norm(proj, g_q, g_kv, ql):
    S = proj.shape[0]
    tm = _tile(S, 512)

    def body(cq_ref, ckv_ref, gq_ref, gkv_ref, oq_ref, okv_ref):
        cq = cq_ref[...].astype(F32)
        oq_ref[...] = (cq * _rms_rows(cq) * gq_ref[...]).astype(BF16)
        ckv = ckv_ref[...].astype(F32)
        okv_ref[...] = (ckv * _rms_rows(ckv) * gkv_ref[...]).astype(BF16)

    return pl.pallas_call(
        body, name="latent_norm", grid=(S // tm,),
        in_specs=[_rows(tm, ql, 0), _rows(tm, ql, 1), _vec(ql), _vec(ql)],
        out_specs=[_rows(tm, ql), _rows(tm, ql)],
        out_shape=[jax.ShapeDtypeStruct((S, ql), BF16)] * 2,
        compiler_params=_params(("parallel",)),
    )(proj, proj, g_q, g_kv)


def _rope_fwd(y, c, s1, s2):
    return y * c + pltpu.roll(y, ROPE // 2, 1) * s1 + pltpu.roll(y, HEAD_PAD - ROPE // 2, 1) * s2


def _rope_bwd(d, c, s1, s2):
    return d * c + pltpu.roll(d * s1, HEAD_PAD - ROPE // 2, 1) + pltpu.roll(d * s2, ROPE // 2, 1)


def _head_rms(v):
    return lax.rsqrt(jnp.sum(v * v, axis=-1, keepdims=True) * (1.0 / QK_DIM) + EPS)


def _q_prep(q0, g_qh, tabs, nh):
    S = q0.shape[0]
    tm = _tile(S, 256)

    def body(q_ref, g_ref, c_ref, s1_ref, s2_ref, o_ref):
        c, s1, s2, g = c_ref[...], s1_ref[...], s2_ref[...], g_ref[...]
        for h in range(nh):
            sl = slice(h * HEAD_PAD, (h + 1) * HEAD_PAD)
            xs = q_ref[:, sl].astype(F32)
            o_ref[:, sl] = (_rope_fwd(xs * _head_rms(xs) * g, c, s1, s2) * (QK_DIM ** -0.5)).astype(BF16)

    w = nh * HEAD_PAD
    return pl.pallas_call(
        body, name="mla_q_prep", grid=(S // tm,),
        in_specs=[_rows(tm, w), _vec(HEAD_PAD)] + [_rows(tm, HEAD_PAD)] * 3,
        out_specs=_rows(tm, w), out_shape=jax.ShapeDtypeStruct((S, w), BF16),
        compiler_params=_params(("parallel",)),
    )(q0, g_qh, *tabs)


def _k_prep(kv0, proj, kpe_col, g_kh, tabs, nh):
    S = kv0.shape[0]
    tm = _tile(S, 256)

    def body(kv_ref, kpe_ref, g_ref, c_ref, s1_ref, s2_ref, o_ref):
        c, s1, s2, g = c_ref[...], s1_ref[...], s2_ref[...], g_ref[...]
        kpe = kpe_ref[...].astype(F32)
        for h in range(nh):
            k0 = jnp.concatenate([kv_ref[:, h * HEAD:(h + 1) * HEAD].astype(F32), kpe], axis=1)
            o_ref[:, h * HEAD_PAD:(h + 1) * HEAD_PAD] = _rope_fwd(k0 * _head_rms(k0) * g, c, s1, s2).astype(BF16)

    return pl.pallas_call(
        body, name="mla_k_prep", grid=(S // tm,),
        in_specs=[_rows(tm, nh * HEAD, 0), _rows(tm, LANE, kpe_col), _vec(HEAD_PAD)] + [_rows(tm, HEAD_PAD)] * 3,
        out_specs=_rows(tm, nh * HEAD_PAD), out_shape=jax.ShapeDtypeStruct((S, nh * HEAD_PAD), BF16),
        compiler_params=_params(("parallel",)),
    )(kv0, proj, g_kh, *tabs)


def _gate_merge(pa, pb, proj, gla_col, glb_col):
    S, D = pa.shape
    tm = _tile(S, 256)

    def body(pa_ref, pb_ref, ga_ref, gb_ref, o_ref):
        o_ref[...] = (jax.nn.sigmoid(ga_ref[...].astype(F32)) * pa_ref[...] + jax.nn.sigmoid(gb_ref[...].astype(F32)) * pb_ref[...]).astype(BF16)

    return pl.pallas_call(
        body, name="gate_merge", grid=(S // tm,),
        in_specs=[_rows(tm, D), _rows(tm, D), _rows(tm, D, gla_col), _rows(tm, D, glb_col)],
        out_specs=_rows(tm, D), out_shape=jax.ShapeDtypeStruct((S, D), BF16),
        compiler_params=_params(("parallel",)),
    )(pa, pb, proj, proj)


def _resid_rmsmod(x, o, g, ada, gt_col, sc_col, sh_col):
    S, D = x.shape
    tm = _tile(S, 256)

    def body(x_ref, o_ref, g_ref, gt_ref, sc_ref, sh_ref, x2_ref, h_ref):
        x2 = x_ref[...] + gt_ref[...] * o_ref[...]
        x2_ref[...] = x2
        h_ref[...] = ((x2 * _rms_rows(x2) * g_ref[...]) * (1.0 + sc_ref[...]) + sh_ref[...]).astype(BF16)

    return pl.pallas_call(
        body, name="resid_rmsmod2", grid=(S // tm,),
        in_specs=[_rows(tm, D), _rows(tm, D), _vec(D), _vec(D, gt_col), _vec(D, sc_col), _vec(D, sh_col)],
        out_specs=[_rows(tm, D), _rows(tm, D)],
        out_shape=[jax.ShapeDtypeStruct((S, D), F32), jax.ShapeDtypeStruct((S, D), BF16)],
        compiler_params=_params(("parallel",)),
    )(x, o, g, ada, ada, ada)


def _swiglu_tile(ib):
    def fn(r):
        pairs = r.shape[1] // (2 * ib)
        act = [jax.nn.silu(r[:, 2 * p * ib:(2 * p + 1) * ib]) * r[:, (2 * p + 1) * ib:(2 * p + 2) * ib] for p in range(pairs)]
        return r, jnp.concatenate(act, axis=1) if pairs > 1 else act[0]
    return fn


def _swiglu_bwd_tile(ib):
    def fn(d, ff):
        ff = ff.astype(F32)
        out = []
        for p in range(d.shape[1] // ib):
            dp = d[:, p * ib:(p + 1) * ib]
            g = ff[:, 2 * p * ib:(2 * p + 1) * ib]
            u = ff[:, (2 * p + 1) * ib:(2 * p + 2) * ib]
            sg = jax.nn.sigmoid(g)
            out += [dp * u * sg * (1.0 + g * (1.0 - sg)), dp * g * sg]
        return (jnp.concatenate(out, axis=1),)
    return fn


def _loss_head(x2, f, tgt, ada, gt_col):
    S, D = x2.shape
    tm = _tile(S, 256)

    def body(x2_ref, f_ref, t_ref, gt_ref, dy_ref, df_ref, red_ref, loss_ref):
        @pl.when(pl.program_id(0) == 0)
        def _():
            red_ref[...] = jnp.zeros_like(red_ref)
            loss_ref[...] = jnp.zeros_like(loss_ref)

        fv = f_ref[...]
        gt = gt_ref[...]
        err = x2_ref[...] + gt * fv - t_ref[...]
        dy = err * (1.0 / D)
        dy_ref[...] = dy
        df_ref[...] = (dy * gt).astype(BF16)
        red_ref[0:1, :] += jnp.sum(dy * fv, axis=0, keepdims=True)
        loss_ref[...] += (0.5 / D) * jnp.sum(err * err)

    return pl.pallas_call(
        body, name="loss_head", grid=(S // tm,),
        in_specs=[_rows(tm, D), _rows(tm, D), _rows(tm, D), _vec(D, gt_col)],
        out_specs=[_rows(tm, D), _rows(tm, D), _vec(D, rows=8), _vec(LANE, rows=8)],
        out_shape=[jax.ShapeDtypeStruct((S, D), F32), jax.ShapeDtypeStruct((S, D), BF16),
                   jax.ShapeDtypeStruct((8, D), F32), jax.ShapeDtypeStruct((8, LANE), F32)],
        compiler_params=_params(("arbitrary",)),
    )(x2, f, tgt, ada)


def _rmsmod2_bwd(dh2, x2, dy, o, g, ada, sc_col, gt_col):
    S, D = x2.shape
    tm = _tile(S, 256)

    def body(dh_ref, x2_ref, dy_ref, o_ref, g_ref, sc_ref, gt_ref, dx_ref, do_ref, red_ref):
        @pl.when(pl.program_id(0) == 0)
        def _():
            red_ref[...] = jnp.zeros_like(red_ref)

        dh = dh_ref[...]
        x2 = x2_ref[...]
        gv = g_ref[...]
        mod = 1.0 + sc_ref[...]
        r = _rms_rows(x2)
        xn = x2 * r
        t = dh * xn
        red_ref[0:1, :] += jnp.sum(dh, axis=0, keepdims=True)
        red_ref[1:2, :] += jnp.sum(t * gv, axis=0, keepdims=True)
        red_ref[2:3, :] += jnp.sum(t * mod, axis=0, keepdims=True)
        dxn = dh * gv * mod
        dx = dy_ref[...] + r * (dxn - xn * jnp.mean(dxn * xn, axis=-1, keepdims=True))
        dx_ref[...] = dx
        red_ref[3:4, :] += jnp.sum(dx * o_ref[...], axis=0, keepdims=True)
        do_ref[...] = (dx * gt_ref[...]).astype(BF16)

    return pl.pallas_call(
        body, name="rmsmod2_bwd", grid=(S // tm,),
        in_specs=[_rows(tm, D)] * 4 + [_vec(D), _vec(D, sc_col), _vec(D, gt_col)],
        out_specs=[_rows(tm, D), _rows(tm, D), _vec(D, rows=8)],
        out_shape=[jax.ShapeDtypeStruct((S, D), F32), jax.ShapeDtypeStruct((S, D), BF16),
                   jax.ShapeDtypeStruct((8, D), F32)],
        compiler_params=_params(("arbitrary",)),
    )(dh2, x2, dy, o, g, ada, ada)


def _rmsmod1_bwd(dh, x, dx2, g, ada, sc_col):
    S, D = x.shape
    tm = _tile(S, 256)

    def body(dh_ref, x_ref, dx2_ref, g_ref, sc_ref, gx_ref, red_ref):
        @pl.when(pl.program_id(0) == 0)
        def _():
            red_ref[...] = jnp.zeros_like(red_ref)

        dh = dh_ref[...]
        xv = x_ref[...]
        gv = g_ref[...]
        mod = 1.0 + sc_ref[...]
        r = _rms_rows(xv)
        xn = xv * r
        t = dh * xn
        red_ref[0:1, :] += jnp.sum(dh, axis=0, keepdims=True)
        red_ref[1:2, :] += jnp.sum(t * gv, axis=0, keepdims=True)
        red_ref[2:3, :] += jnp.sum(t * mod, axis=0, keepdims=True)
        dxn = dh * gv * mod
        gx_ref[...] = dx2_ref[...] + r * (dxn - xn * jnp.mean(dxn * xn, axis=-1, keepdims=True))

    return pl.pallas_call(
        body, name="rmsmod1_bwd", grid=(S // tm,),
        in_specs=[_rows(tm, D)] * 3 + [_vec(D), _vec(D, sc_col)],
        out_specs=[_rows(tm, D), _vec(D, rows=8)],
        out_shape=[jax.ShapeDtypeStruct((S, D), F32), jax.ShapeDtypeStruct((8, D), F32)],
        compiler_params=_params(("arbitrary",)),
    )(dh, x, dx2, g, ada)


def _gate_bwd(dm, pa, pb, proj, gla_col, glb_col):
    S, D = pa.shape
    tm = _tile(S, 256)

    def body(dm_ref, pa_ref, pb_ref, la_ref, lb_ref, dpa_ref, dpb_ref, dla_ref, dlb_ref):
        dm_ = dm_ref[...]
        ga = jax.nn.sigmoid(la_ref[...].astype(F32))
        gb = jax.nn.sigmoid(lb_ref[...].astype(F32))
        dpa_ref[...] = (dm_ * ga).astype(BF16)
        dpb_ref[...] = (dm_ * gb).astype(BF16)
        dla_ref[...] = (dm_ * pa_ref[...] * ga * (1.0 - ga)).astype(BF16)
        dlb_ref[...] = (dm_ * pb_ref[...] * gb * (1.0 - gb)).astype(BF16)

    return pl.pallas_call(
        body, name="gate_bwd", grid=(S // tm,),
        in_specs=[_rows(tm, D)] * 3 + [_rows(tm, D, gla_col), _rows(tm, D, glb_col)],
        out_specs=[_rows(tm, D)] * 4, out_shape=[jax.ShapeDtypeStruct((S, D), BF16)] * 4,
        compiler_params=_params(("parallel",)),
    )(dm, pa, pb, proj, proj)


def _q_prep_bwd(dq, q0, g_qh, tabs, nh):
    S = q0.shape[0]
    tm = _tile(S, 256)

    def body(dq_ref, q_ref, g_ref, c_ref, s1_ref, s2_ref, o_ref, red_ref):
        @pl.when(pl.program_id(0) == 0)
        def _():
            red_ref[...] = jnp.zeros_like(red_ref)

        c, s1, s2, g = c_ref[...], s1_ref[...], s2_ref[...], g_ref[...]
        dg = jnp.zeros((1, HEAD_PAD), F32)
        for h in range(nh):
            sl = slice(h * HEAD_PAD, (h + 1) * HEAD_PAD)
            d1 = _rope_bwd(dq_ref[:, sl], c, s1, s2)
            xs = q_ref[:, sl].astype(F32)
            r = _head_rms(xs)
            qn = xs * r
            dg = dg + jnp.sum(d1 * qn, axis=0, keepdims=True)
            dn = d1 * g
            o_ref[:, sl] = (r * (dn - qn * (jnp.sum(dn * qn, axis=-1, keepdims=True) * (1.0 / QK_DIM)))).astype(BF16)
        red_ref[0:1, :] += dg

    w = nh * HEAD_PAD
    return pl.pallas_call(
        body, name="mla_q_prep_bwd", grid=(S // tm,),
        in_specs=[_rows(tm, w), _rows(tm, w), _vec(HEAD_PAD)] + [_rows(tm, HEAD_PAD)] * 3,
        out_specs=[_rows(tm, w), _vec(HEAD_PAD, rows=8)],
        out_shape=[jax.ShapeDtypeStruct((S, w), BF16), jax.ShapeDtypeStruct((8, HEAD_PAD), F32)],
        compiler_params=_params(("arbitrary",)),
    )(dq, q0, g_qh, *tabs)


def _k_prep_bwd(dk, dv, kv0, proj, kpe_col, g_kh, tabs, nh):
    S = kv0.shape[0]
    tm = _tile(S, 256)
    wv = nh * HEAD

    def body(dk_ref, dv_ref, kv_ref, kpe_ref, g_ref, c_ref, s1_ref, s2_ref, o_ref, dpe_ref, red_ref):
        @pl.when(pl.program_id(0) == 0)
        def _():
            red_ref[...] = jnp.zeros_like(red_ref)

        c, s1, s2, g = c_ref[...], s1_ref[...], s2_ref[...], g_ref[...]
        kpe = kpe_ref[...].astype(F32)
        dg = jnp.zeros((1, HEAD_PAD), F32)
        dpe = jnp.zeros((tm, LANE), F32)
        for h in range(nh):
            d1 = _rope_bwd(dk_ref[:, h * HEAD_PAD:(h + 1) * HEAD_PAD], c, s1, s2)
            k0 = jnp.concatenate([kv_ref[:, h * HEAD:(h + 1) * HEAD].astype(F32), kpe], axis=1)
            r = _head_rms(k0)
            kn = k0 * r
            dg = dg + jnp.sum(d1 * kn, axis=0, keepdims=True)
            dn = d1 * g
            dk0 = r * (dn - kn * (jnp.sum(dn * kn, axis=-1, keepdims=True) * (1.0 / QK_DIM)))
            o_ref[:, h * HEAD:(h + 1) * HEAD] = dk0[:, :HEAD].astype(BF16)
            dpe = dpe + dk0[:, HEAD:]
        o_ref[:, wv:] = dv_ref[...].astype(BF16)
        dpe_ref[...] = dpe.astype(BF16)
        red_ref[0:1, :] += dg

    return pl.pallas_call(
        body, name="mla_k_prep_bwd", grid=(S // tm,),
        in_specs=[_rows(tm, nh * HEAD_PAD), _rows(tm, wv), _rows(tm, wv, 0), _rows(tm, LANE, kpe_col),
                  _vec(HEAD_PAD)] + [_rows(tm, HEAD_PAD)] * 3,
        out_specs=[_rows(tm, 2 * wv), _rows(tm, LANE), _vec(HEAD_PAD, rows=8)],
        out_shape=[jax.ShapeDtypeStruct((S, 2 * wv), BF16), jax.ShapeDtypeStruct((S, LANE), BF16),
                   jax.ShapeDtypeStruct((8, HEAD_PAD), F32)],
        compiler_params=_params(("arbitrary",)),
    )(dk, dv, kv0, proj, g_kh, *tabs)


def _latent_norm_bwd(dcqn, dckvn, proj, g_q, g_kv, ql):
    S = proj.shape[0]
    tm = _tile(S, 512)

    def body(dq_ref, dkv_ref, cq_ref, ckv_ref, gq_ref, gkv_ref, oq_ref, okv_ref, red_ref):
        @pl.when(pl.program_id(0) == 0)
        def _():
            red_ref[...] = jnp.zeros_like(red_ref)

        for row, (d_ref, c_ref, g_ref, o_ref) in enumerate(((dq_ref, cq_ref, gq_ref, oq_ref),
                                                            (dkv_ref, ckv_ref, gkv_ref, okv_ref))):
            d = d_ref[...]
            cv = c_ref[...].astype(F32)
            r = _rms_rows(cv)
            ch = cv * r
            red_ref[row:row + 1, :] += jnp.sum(d * ch, axis=0, keepdims=True)
            dn = d * g_ref[...]
            o_ref[...] = (r * (dn - ch * jnp.mean(dn * ch, axis=-1, keepdims=True))).astype(BF16)

    return pl.pallas_call(
        body, name="latent_norm_bwd", grid=(S // tm,),
        in_specs=[_rows(tm, ql), _rows(tm, ql), _rows(tm, ql, 0), _rows(tm, ql, 1), _vec(ql), _vec(ql)],
        out_specs=[_rows(tm, ql), _rows(tm, ql), _vec(ql, rows=8)],
        out_shape=[jax.ShapeDtypeStruct((S, ql), BF16)] * 2 + [jax.ShapeDtypeStruct((8, ql), F32)],
        compiler_params=_params(("arbitrary",)),
    )(dcqn, dckvn, proj, proj, g_q, g_kv)


NEG = -1e30
ATT_TILE = 512
SB_TILE = 512
SB_SUB = 256
_NT = (((1,), (1,)), ((), ()))
_TN = (((0,), (0,)), ((), ()))


def _dot(a, b, dn=(((1,), (0,)), ((), ()))):
    return lax.dot_general(a, b, dn, preferred_element_type=F32)


def _key_rows(kb, t):
    return pl.ds(pl.multiple_of(kb * t, t), t)


def _diag_mask(t, strict):
    r = lax.broadcasted_iota(jnp.int32, (t, t), 0)
    c = lax.broadcasted_iota(jnp.int32, (t, t), 1)
    return c < r if strict else c <= r


def _mla_fwd(q, k, kv0, nh):
    S = q.shape[0]
    t = _tile(S, ATT_TILE)

    def body(q_ref, k_ref, v_ref, o_ref, lse_ref):
        i = pl.program_id(1)
        qv = q_ref[...]

        def block(kb, carry, masked):
            m, l, acc = carry
            rows = _key_rows(kb, t)
            s = _dot(qv, k_ref[rows, :], _NT)
            if masked:
                s = jnp.where(_diag_mask(t, False), s, NEG)
            m_new = jnp.maximum(m, jnp.max(s, axis=-1, keepdims=True))
            alpha = jnp.exp(m - m_new)
            p = jnp.exp(s - m_new)
            l = alpha * l + jnp.sum(p, axis=-1, keepdims=True)
            acc = alpha * acc + _dot(p.astype(BF16), v_ref[rows, :].astype(BF16))
            return m_new, l, acc

        init = (jnp.full((t, 1), NEG, F32), jnp.zeros((t, 1), F32), jnp.zeros((t, HEAD), F32))
        carry = lax.fori_loop(0, i, lambda kb, c: block(kb, c, False), init)
        m, l, acc = block(i, carry, True)
        o_ref[...] = acc / l
        lse_ref[...] = m + jnp.log(l)

    return pl.pallas_call(
        body, name="mla_attn_fwd", grid=(nh, S // t),
        in_specs=[pl.BlockSpec((t, HEAD_PAD), lambda h, i: (i, h)),
                  pl.BlockSpec((S, HEAD_PAD), lambda h, i: (0, h)),
                  pl.BlockSpec((S, HEAD), lambda h, i: (0, nh + h))],
        out_specs=[pl.BlockSpec((t, HEAD), lambda h, i: (i, h)),
                   pl.BlockSpec((None, t, 1), lambda h, i: (h, i, 0))],
        out_shape=[jax.ShapeDtypeStruct((S, nh * HEAD), F32), jax.ShapeDtypeStruct((nh, S, 1), F32)],
        compiler_params=_params(("parallel", "arbitrary")),
    )(q, k, kv0)


def _mla_bwd(q, k, kv0, o, do, lse, nh):
    S = q.shape[0]
    t = _tile(S, ATT_TILE)
    scale = QK_DIM ** -0.5

    def body(q_ref, k_ref, v_ref, o_ref, do_ref, lse_ref, dq_ref, dk_ref, dv_ref):
        i = pl.program_id(1)

        @pl.when(i == 0)
        def _():
            dk_ref[...] = jnp.zeros_like(dk_ref)
            dv_ref[...] = jnp.zeros_like(dv_ref)

        qv = q_ref[...]
        dov = do_ref[...]
        delta = jnp.sum(dov * o_ref[...], axis=-1, keepdims=True)
        dob = dov.astype(BF16)
        lse = lse_ref[...]

        def block(kb, dq, masked):
            rows = _key_rows(kb, t)
            ks = k_ref[rows, :]
            vs = v_ref[rows, :].astype(BF16)
            p = jnp.exp(_dot(qv, ks, _NT) - lse)
            if masked:
                p = jnp.where(_diag_mask(t, False), p, 0.0)
            ds = (p * (_dot(dob, vs, _NT) - delta)).astype(BF16)
            dk_ref[rows, :] += _dot(ds, qv, _TN)
            dv_ref[rows, :] += _dot(p.astype(BF16), dob, _TN)
            return dq + _dot(ds, ks)

        dq = lax.fori_loop(0, i, lambda kb, c: block(kb, c, False), jnp.zeros((t, HEAD_PAD), F32))
        dq_ref[...] = block(i, dq, True) * scale

    return pl.pallas_call(
        body, name="mla_attn_bwd", grid=(nh, S // t),
        in_specs=[pl.BlockSpec((t, HEAD_PAD), lambda h, i: (i, h)),
                  pl.BlockSpec((S, HEAD_PAD), lambda h, i: (0, h)),
                  pl.BlockSpec((S, HEAD), lambda h, i: (0, nh + h)),
                  pl.BlockSpec((t, HEAD), lambda h, i: (i, h)),
                  pl.BlockSpec((t, HEAD), lambda h, i: (i, h)),
                  pl.BlockSpec((None, t, 1), lambda h, i: (h, i, 0))],
        out_specs=[pl.BlockSpec((t, HEAD_PAD), lambda h, i: (i, h)),
                   pl.BlockSpec((S, HEAD_PAD), lambda h, i: (0, h)),
                   pl.BlockSpec((S, HEAD), lambda h, i: (0, h))],
        out_shape=[jax.ShapeDtypeStruct((S, nh * HEAD_PAD), F32), jax.ShapeDtypeStruct((S, nh * HEAD_PAD), F32),
                   jax.ShapeDtypeStruct((S, nh * HEAD), F32)],
        compiler_params=_params(("parallel", "arbitrary")),
    )(q, k, kv0, o, do, lse)


def _tri(n, cmp):
    r = lax.broadcasted_iota(jnp.int32, (n, n), 0)
    c = lax.broadcasted_iota(jnp.int32, (n, n), 1)
    return jnp.where(cmp(r, c), 1.0, 0.0).astype(BF16)


def _sb_block(qv, ks, run, upper, t, masked):
    z = _dot(qv, ks, _NT)
    lb = jnp.minimum(z, 0.0) - jnp.log(1.0 + jnp.exp(-jnp.abs(z)))
    lom = lb - z
    mask = _diag_mask(t, True) if masked else None
    if masked:
        lom = jnp.where(mask, lom, 0.0)
    tails = []
    for sblk in reversed(range(t // SB_SUB)):
        part = lom[:, sblk * SB_SUB:(sblk + 1) * SB_SUB]
        tails.append(_dot(part.astype(BF16), upper) + run)
        run = run + jnp.sum(part, axis=-1, keepdims=True)
    a = jnp.exp(lb + jnp.concatenate(tails[::-1], axis=1))
    if masked:
        a = jnp.where(mask, a, 0.0)
    return a, lb, mask, run


def _sb_fwd(proj, q_col, k_col, v_col, nh):
    S = proj.shape[0]
    t = _tile(S, SB_TILE)
    assert S // t <= LANE
    scale = HEAD ** -0.5

    def body(q_ref, k_ref, v_ref, o_ref, runs_ref):
        i = pl.program_id(1)
        qv = (q_ref[...].astype(F32) * scale).astype(BF16)
        upper = _tri(SB_SUB, lambda j, s: j > s)
        lane = lax.broadcasted_iota(jnp.int32, (t, LANE), 1)

        def block(kb, carry, masked):
            run, acc, runs = carry
            runs = jnp.where(lane == kb, run, runs)
            rows = _key_rows(kb, t)
            a, _, _, run = _sb_block(qv, k_ref[rows, :].astype(BF16), run, upper, t, masked)
            return run, acc + _dot(a.astype(BF16), v_ref[rows, :].astype(BF16)), runs

        carry = block(i, (jnp.zeros((t, 1), F32), jnp.zeros((t, HEAD), F32), jnp.zeros((t, LANE), F32)), True)
        _, o_ref[...], runs_ref[...] = lax.fori_loop(0, i, lambda j, c: block(i - 1 - j, c, False), carry)

    return pl.pallas_call(
        body, name="sb_attn_fwd", grid=(nh, S // t),
        in_specs=[pl.BlockSpec((t, HEAD), lambda h, i: (i, q_col + h)),
                  pl.BlockSpec((S, HEAD), lambda h, i: (0, k_col + h)),
                  pl.BlockSpec((S, HEAD), lambda h, i: (0, v_col + h))],
        out_specs=[pl.BlockSpec((t, HEAD), lambda h, i: (i, h)), pl.BlockSpec((None, t, LANE), lambda h, i: (h, i, 0))],
        out_shape=[jax.ShapeDtypeStruct((S, nh * HEAD), F32), jax.ShapeDtypeStruct((nh, S, LANE), F32)],
        compiler_params=_params(("parallel", "arbitrary")),
    )(proj, proj, proj)


def _sb_bwd(proj, q_col, k_col, v_col, dy, runs, nh):
    S = proj.shape[0]
    t = _tile(S, SB_TILE)
    scale = HEAD ** -0.5

    def body(q_ref, k_ref, v_ref, dy_ref, runs_ref, dq_ref, dk_ref, dv_ref):
        i = pl.program_id(1)

        @pl.when(i == 0)
        def _():
            dk_ref[...] = jnp.zeros_like(dk_ref)
            dv_ref[...] = jnp.zeros_like(dv_ref)

        qv = (q_ref[...].astype(F32) * scale).astype(BF16)
        dyb = dy_ref[...].astype(BF16)
        runs_v = runs_ref[...]
        lane = lax.broadcasted_iota(jnp.int32, (t, LANE), 1)
        upper = _tri(SB_SUB, lambda j, s: j > s)
        before = _tri(SB_SUB, lambda s, j: s < j)

        def block(kb, carry, masked):
            prefix, dq = carry
            rows = _key_rows(kb, t)
            ks = k_ref[rows, :].astype(BF16)
            vs = v_ref[rows, :].astype(BF16)
            run = jnp.sum(jnp.where(lane == kb, runs_v, 0.0), axis=-1, keepdims=True)
            a, lb, mask, _ = _sb_block(qv, ks, run, upper, t, masked)
            dl = a * _dot(dyb, vs, _NT)
            lefts = []
            for sblk in range(t // SB_SUB):
                part = dl[:, sblk * SB_SUB:(sblk + 1) * SB_SUB]
                lefts.append(_dot(part.astype(BF16), before) + prefix)
                prefix = prefix + jnp.sum(part, axis=-1, keepdims=True)
            beta = jnp.exp(lb)
            dz = dl * (1.0 - beta) - beta * jnp.concatenate(lefts, axis=1)
            if masked:
                dz = jnp.where(mask, dz, 0.0)
            dz = dz.astype(BF16)
            dk_ref[rows, :] += _dot(dz, qv, _TN)
            dv_ref[rows, :] += _dot(a.astype(BF16), dyb, _TN)
            return prefix, dq + _dot(dz, ks)

        carry = lax.fori_loop(0, i, lambda kb, c: block(kb, c, False),
                              (jnp.zeros((t, 1), F32), jnp.zeros((t, HEAD), F32)))
        dq_ref[...] = block(i, carry, True)[1] * scale

    full = pl.BlockSpec((S, HEAD), lambda h, i: (0, h))
    tile = pl.BlockSpec((t, HEAD), lambda h, i: (i, h))
    return pl.pallas_call(
        body, name="sb_attn_bwd", grid=(nh, S // t),
        in_specs=[pl.BlockSpec((t, HEAD), lambda h, i: (i, q_col + h)),
                  pl.BlockSpec((S, HEAD), lambda h, i: (0, k_col + h)),
                  pl.BlockSpec((S, HEAD), lambda h, i: (0, v_col + h)), tile,
                  pl.BlockSpec((None, t, LANE), lambda h, i: (h, i, 0))],
        out_specs=[tile, full, full],
        out_shape=[jax.ShapeDtypeStruct((S, nh * HEAD), F32)] * 3,
        compiler_params=_params(("parallel", "arbitrary")),
    )(proj, proj, proj, dy, runs)


def _place():
    return lax.axis_index("x"), lax.axis_index("y"), lax.axis_index("c")


def _other_chips(x, y):
    return [(1 - x, y), (x, 1 - y), (1 - x, 1 - y)]


def _dev_index(p):
    return 4 * p[0] + 2 * p[1] + p[2]


def _gather_blocks(blocks, *, name, in_vmem):
    n = len(blocks)
    per = 7

    def body(*refs):
        ins, outs = refs[:n], refs[n:2 * n]
        send_sems, recv_sems, local_sems = refs[2 * n:]
        x, y, c = _place()
        me, sibling = (x, y, c), (x, y, 1 - c)
        chips = _other_chips(x, y)

        def slot(a, p):
            return outs[a].at[_dev_index(p)]

        def copy(a, k, block, to, src=None):
            return pltpu.make_async_remote_copy(
                src_ref=slot(a, block) if src is None else src, dst_ref=slot(a, block),
                send_sem=send_sems.at[a * per + k], recv_sem=recv_sems.at[a * per + k],
                device_id=to, device_id_type=MESH)

        mine = [pltpu.make_async_copy(ins[a], slot(a, me), local_sems.at[a]) for a in range(n)] if in_vmem else []
        for cp in mine:
            cp.start()
        first = []
        for a in range(n):
            first.append(copy(a, 0, me, sibling, src=ins[a]))
            first += [copy(a, 1 + j, me, (*chip, c), src=ins[a]) for j, chip in enumerate(chips)]
        for cp in first:
            cp.start()
        passed = []
        for a in range(n):
            for j, chip in enumerate(chips):
                copy(a, 1 + j, (*chip, c), me).wait_recv()
                cp = copy(a, 4 + j, (*chip, c), sibling)
                cp.start()
                passed.append(cp)
        for a in range(n):
            copy(a, 0, sibling, me).wait_recv()
            for j, chip in enumerate(chips):
                copy(a, 4 + j, (*chip, 1 - c), me).wait_recv()
        for cp in first + passed:
            cp.wait_send()
        for cp in mine:
            cp.wait()

    space = pltpu.VMEM if in_vmem else pl.ANY
    spec = pl.BlockSpec(memory_space=space)
    outs = pl.pallas_call(
        body, name=name, in_specs=[spec] * n, out_specs=[spec] * n,
        out_shape=[jax.ShapeDtypeStruct((N_DEV,) + b.shape, b.dtype) for b in blocks],
        scratch_shapes=[pltpu.SemaphoreType.DMA((n * per,)), pltpu.SemaphoreType.DMA((n * per,)),
                        pltpu.SemaphoreType.DMA((n,))],
        compiler_params=pltpu.CompilerParams(vmem_limit_bytes=VMEM_LIMIT),
    )(*blocks)
    return list(outs)


def _sibling_swap(arrs, *, name, whole=False):
    n = len(arrs)

    def body(*refs):
        ins, outs = refs[:n], refs[n:2 * n]
        send_sems, recv_sems = refs[2 * n:]
        x, y, c = _place()
        copies = [pltpu.make_async_remote_copy(
            src_ref=ins[a] if whole else ins[a].at[1 - c], dst_ref=outs[a],
            send_sem=send_sems.at[a], recv_sem=recv_sems.at[a],
            device_id=(x, y, 1 - c), device_id_type=MESH) for a in range(n)]
        for cp in copies:
            cp.start()
        for cp in copies:
            cp.wait()

    spec = pl.BlockSpec(memory_space=pl.ANY)
    return list(pl.pallas_call(
        body, name=name, in_specs=[spec] * n, out_specs=[spec] * n,
        out_shape=[jax.ShapeDtypeStruct(a.shape if whole else a.shape[1:], a.dtype) for a in arrs],
        scratch_shapes=[pltpu.SemaphoreType.DMA((n,)), pltpu.SemaphoreType.DMA((n,))],
    )(*arrs))


_HBM = pl.BlockSpec(memory_space=pltpu.HBM)
_SEM = pl.BlockSpec(memory_space=pltpu.SEMAPHORE)
_EFFECT = pltpu.SideEffectType.DATAFLOW_SIDE_EFFECTING


def _in_hbm(a):
    return pltpu.with_memory_space_constraint(a, pltpu.HBM)


def _split_copies(srcs, lands, send_sems, recv_sems, plan):
    x, y, c = _place()
    copies = []
    for a, (src, land) in enumerate(zip(srcs, lands)):
        steps = plan(x, y, c)
        for k, (pick, slot, to) in enumerate(steps):
            copies.append(pltpu.make_async_remote_copy(
                src_ref=pick(src), dst_ref=slot(land), send_sem=send_sems.at[a * len(steps) + k],
                recv_sem=recv_sems.at[a * len(steps) + k], device_id=to, device_id_type=MESH))
    return copies


def _split_start(srcs, land_shapes, plan, per, *, name):
    n = len(srcs)

    def body(*refs):
        send_sems, recv_sems = refs[2 * n], refs[2 * n + 1]
        for cp in _split_copies(refs[:n], refs[n:2 * n], send_sems, recv_sems, plan):
            cp.start()
        token = refs[-1]
        token[...] = jnp.zeros_like(token)

    lands = [_in_hbm(lax.empty(s.shape, s.dtype)) for s in land_shapes]
    outs = pl.pallas_call(
        body, name=name,
        out_shape=(pltpu.SemaphoreType.DMA((n * per,)), pltpu.SemaphoreType.DMA((n * per,)),
                   *[pltpu.HBM(s.shape, s.dtype) for s in srcs], *[pltpu.HBM(s.shape, s.dtype) for s in land_shapes],
                   jax.ShapeDtypeStruct((8, LANE), F32)),
        in_specs=[_HBM] * (2 * n),
        out_specs=(_SEM, _SEM, *[_HBM] * (2 * n), pl.BlockSpec(memory_space=pltpu.VMEM)),
        input_output_aliases={i: 2 + i for i in range(2 * n)},
        compiler_params=pltpu.CompilerParams(has_side_effects=_EFFECT),
    )(*[_in_hbm(s) for s in srcs], *lands)
    return outs[0], outs[1], list(outs[2:2 + n]), list(outs[2 + n:2 + 2 * n]), outs[-1]


def _split_wait(send_sems, recv_sems, srcs, lands, after, plan, *, name):
    n = len(srcs)

    def body(*refs):
        for cp in _split_copies(refs[:n], refs[n:2 * n], refs[2 * n], refs[2 * n + 1], plan):
            cp.wait_send()
            cp.wait_recv()

    outs = pl.pallas_call(
        body, name=name,
        out_shape=(*[pltpu.HBM(s.shape, s.dtype) for s in srcs], *[pltpu.HBM(s.shape, s.dtype) for s in lands]),
        in_specs=[_HBM] * (2 * n) + [_SEM, _SEM, pl.BlockSpec(memory_space=pl.ANY)],
        out_specs=tuple([_HBM] * (2 * n)),
        input_output_aliases={i: i for i in range(2 * n)},
        compiler_params=pltpu.CompilerParams(has_side_effects=_EFFECT),
    )(*srcs, *lands, send_sems, recv_sems, after)
    return list(outs[:n]), list(outs[n:])


def _gather_plan(x, y, c):
    slot = lambda land: land.at[_dev_index((x, y, c))]
    whole = lambda src: src
    return [(whole, slot, (x, y, 1 - c))] + [(whole, slot, (px, py, c)) for px, py in _other_chips(x, y)]


def _swap_plan(x, y, c):
    return [(lambda src: src.at[1 - c], lambda land: land, (x, y, 1 - c))]


def _exchange_plan(x, y, c):
    return [(lambda src, k=2 * px + py: src.at[k], lambda land, j=j: land.at[j], (px, py, c))
            for j, (px, py) in enumerate(_other_chips(x, y))]


def _gather_forward(lands, *, name):
    n = len(lands)

    def body(*refs):
        lands_in, outs = refs[:n], refs[n:2 * n]
        send_sems, recv_sems = refs[2 * n:]
        x, y, c = _place()
        copies = []
        for a in range(n):
            for j, (px, py) in enumerate(_other_chips(x, y)):
                copies.append((pltpu.make_async_remote_copy(
                    src_ref=lands_in[a].at[_dev_index((px, py, c))], dst_ref=outs[a].at[_dev_index((px, py, c))],
                    send_sem=send_sems.at[3 * a + j], recv_sem=recv_sems.at[3 * a + j],
                    device_id=(x, y, 1 - c), device_id_type=MESH), a, j, (px, py)))
        for cp, _, _, _ in copies:
            cp.start()
        for cp, a, j, (px, py) in copies:
            cp.wait_send()
            pltpu.make_async_remote_copy(
                src_ref=lands_in[a].at[_dev_index((px, py, 1 - c))], dst_ref=outs[a].at[_dev_index((px, py, 1 - c))],
                send_sem=send_sems.at[3 * a + j], recv_sem=recv_sems.at[3 * a + j],
                device_id=(x, y, 1 - c), device_id_type=MESH).wait_recv()

    spec = pl.BlockSpec(memory_space=pl.ANY)
    return list(pl.pallas_call(
        body, name=name, in_specs=[spec] * n, out_specs=[spec] * n,
        out_shape=[jax.ShapeDtypeStruct(a.shape, a.dtype) for a in lands],
        input_output_aliases={a: a for a in range(n)},
        scratch_shapes=[pltpu.SemaphoreType.DMA((3 * n,)), pltpu.SemaphoreType.DMA((3 * n,))],
    )(*lands))


def _flat2(a, lead):
    return a.reshape(a.shape[:lead] + (-1, a.shape[-1]))


def _pair_sum(g, recv, c_idx, *, name):
    _, nchip, r, w = g.shape
    tm = _tile(r, 256) if r % 8 == 0 else r

    def body(c_ref, g_ref, r_ref, o_ref):
        o_ref[...] = (g_ref[...].astype(F32) + r_ref[...].astype(F32)).astype(o_ref.dtype)

    return pl.pallas_call(
        body, name=name,
        grid_spec=pltpu.PrefetchScalarGridSpec(
            num_scalar_prefetch=1, grid=(nchip, r // tm),
            in_specs=[pl.BlockSpec((None, None, tm, w), lambda k, i, c_ref: (c_ref[0], k, i, 0)),
                      pl.BlockSpec((None, tm, w), lambda k, i, c_ref: (k, i, 0))],
            out_specs=pl.BlockSpec((None, tm, w), lambda k, i, c_ref: (k, i, 0))),
        out_shape=jax.ShapeDtypeStruct((nchip, r, w), BF16),
        compiler_params=_params(("parallel", "parallel")),
    )(c_idx, g, recv)


def _chip_sum(s1, recv, chip_idx, *, name):
    _, r, w = s1.shape
    tm = _tile(r, 256) if r % 8 == 0 else r

    def body(k_ref, s_ref, r_ref, o_ref):
        acc = s_ref[...].astype(F32)
        for j in range(3):
            acc = acc + r_ref[j].astype(F32)
        o_ref[...] = acc

    return pl.pallas_call(
        body, name=name,
        grid_spec=pltpu.PrefetchScalarGridSpec(
            num_scalar_prefetch=1, grid=(r // tm,),
            in_specs=[pl.BlockSpec((None, tm, w), lambda i, k_ref: (k_ref[0], i, 0)),
                      pl.BlockSpec((3, tm, w), lambda i, k_ref: (0, i, 0))],
            out_specs=pl.BlockSpec((tm, w), lambda i, k_ref: (i, 0))),
        out_shape=jax.ShapeDtypeStruct((r, w), F32),
        compiler_params=_params(("parallel",)),
    )(chip_idx, s1, recv)


def _adam_math(w, g, m, v):
    m = ADAM_B1 * m + (1.0 - ADAM_B1) * g
    v = ADAM_B2 * v + (1.0 - ADAM_B2) * (g * g)
    m_hat = m / (1.0 - ADAM_B1 ** ADAM_STEP)
    v_hat = v / (1.0 - ADAM_B2 ** ADAM_STEP)
    delta = -ADAM_LR * (m_hat / (jnp.sqrt(v_hat) + ADAM_EPS) + ADAM_WD * w)
    return delta, m, v


def _adamw(w, mine, other, c_idx, m, v, *, name):
    r, cw = w.shape
    hr = r // 2
    tm = _row_tile(hr, 9 * cw * 4)

    def body(c_ref, w_ref, a_ref, b_ref, m_ref, v_ref, g_ref, d_ref, nm_ref, nv_ref):
        g = jnp.where(pl.program_id(0) == c_ref[0], a_ref[...], b_ref[...])
        g_ref[...] = g
        d_ref[...], nm_ref[...], nv_ref[...] = _adam_math(w_ref[...], g, m_ref[...], v_ref[...])

    per_half = hr // tm
    full = pl.BlockSpec((tm, cw), lambda h, i, c_ref: (h * per_half + i, 0))
    mine_spec = pl.BlockSpec((tm, cw), lambda h, i, c_ref: (jnp.where(h == c_ref[0], i, 0), 0))
    other_spec = pl.BlockSpec((tm, cw), lambda h, i, c_ref: (jnp.where(h == c_ref[0], 0, i), 0))
    return pl.pallas_call(
        body, name=name,
        grid_spec=pltpu.PrefetchScalarGridSpec(
            num_scalar_prefetch=1, grid=(2, per_half),
            in_specs=[full, mine_spec, other_spec, full, full], out_specs=[full] * 4),
        out_shape=[jax.ShapeDtypeStruct((r, cw), F32)] * 4,
        compiler_params=_params(("parallel", "parallel")),
    )(c_idx, w, mine, other, m, v)


def _adamw_ada(cact_t, dada, w, m, v):
    r, cw = w.shape
    nb = cact_t.shape[1]
    tm = _tile(r, 256)
    tn = _tile(cw, 1024)

    def body(a_ref, d_ref, w_ref, m_ref, v_ref, g_ref, dl_ref, nm_ref, nv_ref):
        a = a_ref[...]
        d = d_ref[...]
        g = a[:, 0:1] * d[0:1, :]
        for b in range(1, nb):
            g = g + a[:, b:b + 1] * d[b:b + 1, :]
        g_ref[...] = g
        dl_ref[...], nm_ref[...], nv_ref[...] = _adam_math(w_ref[...], g, m_ref[...], v_ref[...])

    blk = pl.BlockSpec((tm, tn), lambda i, j: (i, j))
    return pl.pallas_call(
        body, name="adamw_ada", grid=(r // tm, cw // tn),
        in_specs=[pl.BlockSpec((tm, nb), lambda i, j: (i, 0)), pl.BlockSpec((nb, tn), lambda i, j: (0, j)), blk, blk, blk],
        out_specs=[blk] * 4, out_shape=[jax.ShapeDtypeStruct((r, cw), F32)] * 4,
        compiler_params=_params(("parallel", "parallel")),
    )(cact_t, dada, w, m, v)


def _adamw_vec(parts, w, m, v):
    n = w.shape[1]

    def body(p_ref, w_ref, m_ref, v_ref, g_ref, d_ref, nm_ref, nv_ref):
        p = p_ref[...]
        g = p[0:1, :]
        for b in range(1, N_DEV):
            g = g + p[b:b + 1, :]
        g_ref[...] = g
        d_ref[...], nm_ref[...], nv_ref[...] = _adam_math(w_ref[...], g, m_ref[...], v_ref[...])

    return pl.pallas_call(
        body, name="adamw_vec", out_shape=[jax.ShapeDtypeStruct((1, n), F32)] * 4,
        compiler_params=pltpu.CompilerParams(vmem_limit_bytes=VMEM_LIMIT),
    )(parts, w, m, v)


def _w_in_segments(kpe0, d_in, cs):
    segs = []
    for k in range(4):
        lo, hi = k * cs, (k + 1) * cs
        for a, b, shift in ((0, kpe0, 0), (kpe0, kpe0 + ROPE, d_in - ROPE - kpe0), (kpe0 + ROPE, d_in, -ROPE)):
            a, b = max(lo, a), min(hi, b)
            if a < b:
                segs.append((k, a - lo, a + shift, b - a))
    return segs


def _w_in_layout(g8, kpe0):
    _, hr, cs = g8.shape
    rows, d_in = 2 * hr, 4 * cs
    segs = _w_in_segments(kpe0, d_in, cs)
    tm = _tile(rows, 256)

    def body(g_ref, o_ref):
        for k, src, dst, w in segs:
            o_ref[:, dst:dst + w] = g_ref[k, :, src:src + w]
        o_ref[:, d_in:] = jnp.zeros((tm, ROPE), o_ref.dtype)

    return pl.pallas_call(
        body, name="w_in_layout", grid=(rows // tm,),
        in_specs=[pl.BlockSpec((4, tm, cs), lambda i: (0, i, 0))], out_specs=_rows(tm, d_in + ROPE),
        out_shape=jax.ShapeDtypeStruct((rows, d_in + ROPE), g8.dtype), compiler_params=_params(("parallel",)),
    )(g8.reshape(4, rows, cs))


def _w_in_grad_pieces(g, kpe0):
    rows, d_in_p = g.shape
    d_in = d_in_p - ROPE
    cs = d_in // 4
    segs = _w_in_segments(kpe0, d_in, cs)
    hr = rows // 2
    tm = _tile(hr, 256)
    per_half = hr // tm

    def body(g_ref, o_ref):
        for k, src, dst, w in segs:
            o_ref[k, :, src:src + w] = g_ref[:, dst:dst + w]

    return pl.pallas_call(
        body, name="w_in_grad_pieces", grid=(rows // tm,),
        in_specs=[_rows(tm, d_in_p)],
        out_specs=pl.BlockSpec((None, 4, tm, cs), lambda i: (i // per_half, 0, i % per_half, 0)),
        out_shape=jax.ShapeDtypeStruct((2, 4, hr, cs), g.dtype), compiler_params=_params(("parallel",)),
    )(g)


def _interleave_layout(g8, ib):
    _, hr, cs = g8.shape
    rows, per_chip, per_half = 2 * hr, cs // ib, 2 * cs // ib
    tm = _tile(rows, 2048)

    def src(jj):
        return jj // 2 + per_half * (jj % 2)

    def body(g_ref, o_ref):
        o_ref[...] = g_ref[...]

    return pl.pallas_call(
        body, name="interleave_layout", grid=(rows // tm, 4 * per_chip),
        in_specs=[pl.BlockSpec((None, tm, ib), lambda i, jj: (src(jj) // per_chip, i, src(jj) % per_chip))],
        out_specs=pl.BlockSpec((tm, ib), lambda i, jj: (i, jj)),
        out_shape=jax.ShapeDtypeStruct((rows, 4 * cs), g8.dtype), compiler_params=_params(("parallel", "parallel")),
    )(g8.reshape(4, rows, cs))


def _cols_from_chips(g8, rows):
    cs = g8.shape[-1]
    return g8.reshape(4, rows, cs).transpose(1, 0, 2).reshape(rows, 4 * cs)


def _cols_to_pieces(g):
    rows, c4 = g.shape
    return g.reshape(2, rows // 2, 4, c4 // 4).transpose(0, 2, 1, 3)


def _rows_to_pieces(g):
    r4, cols = g.shape
    return g.reshape(4, 2, r4 // 8, cols).transpose(1, 0, 2, 3)


def _pad_cols(a, w):
    return jnp.pad(a, ((0, 0), (0, w - a.shape[1])))


def kernel(x, c, positions, w_ada, b_ada, g_norm1, g_norm2, w_in, g_q_latent, g_kv_latent, w_uq, w_ukv, g_q_head, g_k_head, w_proj_mla, w_proj_sb, w_out, w_ffn_in, w_ffn_out, loss_target, m_w_ada, m_b_ada, m_g_norm1, m_g_norm2, m_w_in, m_g_q_latent, m_g_kv_latent, m_w_uq, m_w_ukv, m_g_q_head, m_g_k_head, m_w_proj_mla, m_w_proj_sb, m_w_out, m_w_ffn_in, m_w_ffn_out, v_w_ada, v_b_ada, v_g_norm1, v_g_norm2, v_w_in, v_g_q_latent, v_g_kv_latent, v_w_uq, v_w_ukv, v_g_q_head, v_g_k_head, v_w_proj_mla, v_w_proj_sb, v_w_out, v_w_ffn_in, v_w_ffn_out):
    xi, yi, ci = _place()
    chip = 2 * xi + yi
    dev = 2 * chip + ci
    c_idx = jnp.reshape(ci, (1,)).astype(jnp.int32)
    chip_idx = jnp.reshape(chip, (1,)).astype(jnp.int32)

    x = x[0]
    tgt = loss_target[0]
    S, D = x.shape
    ql = g_q_latent.shape[1]
    assert g_kv_latent.shape[1] == ql
    mlaw = w_proj_mla.shape[1]
    nh = mlaw // HEAD
    sbw = w_proj_sb.shape[1]
    assert sbw == mlaw
    dff = w_ffn_out.shape[1] * 4
    d_in = 2 * ql + ROPE + 3 * sbw + 2 * D
    d_in_p = d_in + ROPE
    q_col = (2 * ql) // HEAD
    k_col = q_col + nh
    v_col = k_col + nh
    gla_col = (2 * ql + 3 * sbw) // D
    glb_col = gla_col + 1
    kpe_col = (d_in - ROPE) // LANE
    assert (2 * ql + 3 * sbw) % D == 0 and (d_in - ROPE) % LANE == 0

    mats = {"w_in": w_in[0], "w_uq": w_uq[0], "w_ukv": w_ukv[0], "w_proj_mla": w_proj_mla[0],
            "w_proj_sb": w_proj_sb[0], "w_out": w_out[0], "w_ffn_in": w_ffn_in[0], "w_ffn_out": w_ffn_out[0]}
    names = list(mats)
    row_sharded = {"w_out", "w_ffn_out"}

    c_all = _gather_blocks([jnp.broadcast_to(c, (8, D))], name="gather_cond", in_vmem=True)[0][:, 0, :]
    n_ada = w_ada.shape[2]
    b_shard = lax.dynamic_slice_in_dim(b_ada, chip * n_ada, n_ada, axis=1)
    ada_shard = _mm(c_all, w_ada[0], name="ada_proj", a_fn=jax.nn.silu, bias=b_shard)
    ada_all = _gather_blocks([ada_shard], name="gather_ada", in_vmem=True)[0]
    ada_rows = lax.dynamic_index_in_dim(ada_all, dev, axis=1, keepdims=False)
    ada = ada_rows[0::2].reshape(1, 4 * n_ada)
    SH1, SC1, GT1, SH2, SC2, GT2 = range(6)

    def after(dep, a):
        return a + (dep.reshape(-1)[0:1].reshape((1,) * a.ndim) * 0).astype(a.dtype)

    def fill_own(g8, own):
        return lax.dynamic_update_index_in_dim(g8, own, dev, 0)

    halves = []
    for nm in names:
        w = mats[nm]
        hr = w.shape[0] // 2
        halves.append(lax.dynamic_slice_in_dim(w, ci * hr, hr, axis=0).astype(BF16))
    half_of = dict(zip(names, halves))
    early = ["w_in", "w_uq", "w_ukv"]
    late = ["w_proj_mla", "w_proj_sb", "w_out", "w_ffn_in", "w_ffn_out"]
    early_halves = [half_of[nm] for nm in early]
    early_halves[0] = after(ada, early_halves[0])
    early_got = _gather_blocks(early_halves, name="gather_weights", in_vmem=False)
    gathered = {nm: fill_own(g8, own) for nm, g8, own in zip(early, early_got, early_halves)}
    late_halves = [half_of[nm] for nm in late]
    late_halves[0] = after(gathered[early[1]], late_halves[0])
    late_send, late_recv, late_srcs, late_lands, late_token = _split_start(
        late_halves, [jax.ShapeDtypeStruct((N_DEV,) + h.shape, h.dtype) for h in late_halves], _gather_plan, 4,
        name="gather_late_start")
    ada = ada + late_token[0:1, 0:1]

    def full_cols(nm):
        return _cols_from_chips(gathered[nm], mats[nm].shape[0])

    kpe0 = 2 * ql
    w_in_p = _w_in_layout(gathered["w_in"], kpe0)
    w_uq_p = jnp.pad(full_cols("w_uq").reshape(ql, nh, QK_DIM), ((0, 0), (0, 0), (0, HEAD_PAD - QK_DIM))
                     ).reshape(ql, nh * HEAD_PAD)
    w_ukv4 = full_cols("w_ukv").reshape(ql, nh, 2 * HEAD)
    w_ukv_p = jnp.concatenate([w_ukv4[:, :, :HEAD].reshape(ql, mlaw), w_ukv4[:, :, HEAD:].reshape(ql, mlaw)], axis=1)

    half = ROPE // 2
    freqs = ROPE_THETA ** (-jnp.arange(half, dtype=F32) / half)
    ang = positions[0].astype(F32)[:, None] * freqs
    cos, sin = jnp.cos(ang), jnp.sin(ang)
    one = jnp.ones((S, NOPE), F32)
    zero = jnp.zeros((S, NOPE), F32)
    zh = jnp.zeros((S, half), F32)
    tabs = (jnp.concatenate([one, cos, cos, one[:, :HEAD_PAD - QK_DIM]], axis=1),
            jnp.concatenate([zero, zh, sin, zero[:, :HEAD_PAD - QK_DIM]], axis=1),
            jnp.concatenate([zero, -sin, zh, zero[:, :HEAD_PAD - QK_DIM]], axis=1))
    g_qh_p = _pad_cols(g_q_head, HEAD_PAD)
    g_kh_p = _pad_cols(g_k_head, HEAD_PAD)

    h1 = _rmsmod(x, g_norm1, ada, SC1, SH1, name="rmsmod1")
    proj = _mm(h1, w_in_p, name="mm_proj", tn=640, out_dtype=BF16)
    cqn, ckvn = _latent_norm(proj, g_q_latent, g_kv_latent, ql)
    q0 = _mm(cqn, w_uq_p, name="mm_q_up", out_dtype=BF16)
    kv0 = _mm(ckvn, w_ukv_p, name="mm_kv_up", out_dtype=BF16)
    q = _q_prep(q0, g_qh_p, tabs, nh)
    k = _k_prep(kv0, proj, kpe_col, g_kh_p, tabs, nh)
    y_a, lse = _mla_fwd(q, k, kv0, nh)
    y_b, sb_runs = _sb_fwd(proj, q_col, k_col, v_col, nh)
    late_srcs, late_lands = _split_wait(late_send, late_recv, late_srcs, late_lands, y_b, _gather_plan,
                                        name="gather_late_wait")
    late_got = _gather_forward(late_lands, name="gather_late_forward")
    gathered.update({nm: fill_own(g8, own) for nm, g8, own in zip(late, late_got, late_srcs)})
    w_pm = full_cols("w_proj_mla")
    w_ps = full_cols("w_proj_sb")
    w_o = gathered["w_out"].reshape(D, D)
    ib = 256 if (dff // 2) % 256 == 0 else LANE
    nb = dff // ib
    w_fi = _interleave_layout(gathered["w_ffn_in"], ib)
    w_fo = gathered["w_ffn_out"].reshape(dff, D)
    pa = _mm(y_a, w_pm, name="mm_proj_mla", out_dtype=BF16)
    pb = _mm(y_b, w_ps, name="mm_proj_sb", out_dtype=BF16)
    merged = _gate_merge(pa, pb, proj, gla_col, glb_col)
    o = _mm(merged, w_o, name="mm_out")
    x2, h2 = _resid_rmsmod(x, o, g_norm2, ada, GT1, SC2, SH2)
    ff, act = _mm(h2, w_fi, name="mm_ffn_in", tn=4 * ib,
                  fused=(_swiglu_tile(ib), [], [(2 * dff, 4 * ib, BF16), (dff, 2 * ib, BF16)]))
    f = _mm(act, w_fo, name="mm_ffn_out")
    dy, df, red_l, loss_p = _loss_head(x2, f, tgt, ada, GT2)

    dff_, = _mm(df, w_fo, name="mm_d_act", tb=True, tn=2 * ib,
                fused=(_swiglu_bwd_tile(ib), [(ff, 4 * ib)], [(2 * dff, 4 * ib, BF16)]))
    def pc(kind):
        if kind == "cols":
            return kind
        return kind if (D // 4) % LANE == 0 and (dff // 4) % LANE == 0 else None

    gw_fo = _mm(act, df, name="mm_gw_ffn_out", ta=True, out_dtype=BF16, pieces=pc("rows"))
    dh2 = _mm(dff_, w_fi, name="mm_d_h2", tb=True)
    gw_fi = _mm(h2, dff_, name="mm_gw_ffn_in", ta=True, out_dtype=BF16, pieces="cols", tn=ib,
                col_perm=lambda jj: jj // 2 + nb * (jj % 2))

    def to_pieces(nms, grads):
        return [g if g.ndim == 4 else (_rows_to_pieces if nm in row_sharded else _cols_to_pieces)(g)
                for nm, g in zip(nms, grads)]

    def pair_sums(nms, pcs, got):
        return [_pair_sum(p, r, c_idx, name="rs_pair_sum_" + nm) for p, r, nm in zip(pcs, got, nms)]

    def swap_start(pcs, tag):
        return _split_start(pcs, [jax.ShapeDtypeStruct(p.shape[1:], p.dtype) for p in pcs], _swap_plan, 1,
                            name="rs_swap_%s_start" % tag)

    def exchange_start(pair, tag):
        return _split_start(pair, [jax.ShapeDtypeStruct((3,) + p.shape[1:], p.dtype) for p in pair], _exchange_plan, 3,
                            name="rs_exchange_%s_start" % tag)

    ffn = ["w_ffn_in", "w_ffn_out"]
    sw = swap_start(to_pieces(ffn, [gw_fi, gw_fo]), "ffn")
    ada = ada + sw[4][0:1, 0:1]
    dx2, do, red_2 = _rmsmod2_bwd(dh2, x2, dy, o, g_norm2, ada, SC2, GT1)
    ffn_pcs, ffn_got = _split_wait(sw[0], sw[1], sw[2], sw[3], dx2, _swap_plan, name="rs_swap_ffn_wait")
    ffn_send, ffn_recv, ffn_pair, ffn_lands, ffn_token = exchange_start(pair_sums(ffn, ffn_pcs, ffn_got), "ffn")
    dmerged = _mm(do, w_o, name="mm_d_merged", tb=True, out_dtype=BF16,
                  bias=jnp.zeros((1, D), F32) + ffn_token[0:1, 0:1])
    gw_o = _mm(merged, do, name="mm_gw_out", ta=True, out_dtype=BF16, pieces=pc("rows"))
    dpa, dpb, dgla, dglb = _gate_bwd(dmerged, pa, pb, proj, gla_col, glb_col)
    gw_pm = _mm(y_a, dpa, name="mm_gw_proj_mla", ta=True, out_dtype=BF16, pieces=pc("cols"))
    gw_ps = _mm(y_b, dpb, name="mm_gw_proj_sb", ta=True, out_dtype=BF16, pieces=pc("cols"))
    mid = ["w_proj_mla", "w_proj_sb", "w_out"]
    mid_pcs = to_pieces(mid, [gw_pm, gw_ps, gw_o])
    mid_pair = pair_sums(mid, mid_pcs, _sibling_swap(mid_pcs, name="rs_sibling_swap_mid"))
    mid_send, mid_recv, mid_pair, mid_lands, mid_token = exchange_start(mid_pair, "mid")
    behind_mid = jnp.zeros((1, mlaw), F32) + mid_token[0:1, 0:1]
    dya = _mm(dpa, w_pm, name="mm_d_ya", tb=True, out_dtype=BF16, bias=behind_mid)
    dyb = _mm(dpb, w_ps, name="mm_d_yb", tb=True, out_dtype=BF16, bias=behind_mid)
    dq, dk, dv = _mla_bwd(q, k, kv0, y_a, dya, lse, nh)
    dq_sb, dk_sb, dv_sb = _sb_bwd(proj, q_col, k_col, v_col, dyb, sb_runs, nh)
    dq0, red_qh = _q_prep_bwd(dq, q0, g_qh_p, tabs, nh)
    dkv0, dkpe, red_kh = _k_prep_bwd(dk, dv, kv0, proj, kpe_col, g_kh_p, tabs, nh)
    dcqn = _mm(dq0, w_uq_p, name="mm_d_cqn", tb=True, out_dtype=BF16)
    gw_uq_p = _mm(cqn, dq0, name="mm_gw_uq", ta=True, out_dtype=BF16)
    dckvn = _mm(dkv0, w_ukv_p, name="mm_d_ckvn", tb=True, out_dtype=BF16)
    gw_ukv_p = _mm(ckvn, dkv0, name="mm_gw_ukv", ta=True, out_dtype=BF16)
    dcq, dckv, red_lat = _latent_norm_bwd(dcqn, dckvn, proj, g_q_latent, g_kv_latent, ql)
    dproj = jnp.concatenate([dcq, dckv, dq_sb.astype(BF16), dk_sb.astype(BF16), dv_sb.astype(BF16),
                             dgla, dglb, dkpe], axis=1)
    gw_in_p = _mm(h1, dproj, name="mm_gw_in", ta=True, out_dtype=BF16, tn=640)

    gw_in = _w_in_grad_pieces(gw_in_p, kpe0)
    gw_uq = gw_uq_p.reshape(ql, nh, HEAD_PAD)[:, :, :QK_DIM].reshape(ql, nh * QK_DIM)
    gw_ukv = jnp.concatenate([gw_ukv_p[:, :mlaw].reshape(ql, nh, HEAD), gw_ukv_p[:, mlaw:].reshape(ql, nh, HEAD)],
                             axis=2).reshape(ql, 2 * mlaw)
    last = ["w_in", "w_uq", "w_ukv"]
    assert last + mid + ffn == names

    last_pcs = to_pieces(last, [gw_in, gw_uq, gw_ukv])
    last_pair = pair_sums(last, last_pcs, _sibling_swap(last_pcs, name="rs_sibling_swap_last"))
    last_send, last_recv, last_pair, last_lands, last_token = exchange_start(last_pair, "last")
    ada = ada + last_token[0:1, 0:1]
    dh1 = _mm(dproj, w_in_p, name="mm_d_h1", tb=True, bias=jnp.zeros((1, D), F32) + last_token[0:1, 0:1])
    grad_x, red_1 = _rmsmod1_bwd(dh1, x, dx2, g_norm1, ada, SC1)
    last_pair, last_chips = _split_wait(last_send, last_recv, last_pair, last_lands, grad_x, _exchange_plan,
                                        name="rs_exchange_last_wait")
    mid_pair, mid_chips = _split_wait(mid_send, mid_recv, mid_pair, mid_lands, grad_x, _exchange_plan,
                                      name="rs_exchange_mid_wait")
    ffn_pair, ffn_chips = _split_wait(ffn_send, ffn_recv, ffn_pair, ffn_lands, grad_x, _exchange_plan,
                                      name="rs_exchange_ffn_wait")
    reduced = [_chip_sum(s, r, chip_idx, name="rs_chip_sum_" + nm)
               for s, r, nm in zip(last_pair + mid_pair + ffn_pair, last_chips + mid_chips + ffn_chips, names)]
    from_sibling2 = _sibling_swap(reduced, name="rs_sibling_send", whole=True)

    vec_names = ["b_ada", "g_norm1", "g_norm2", "g_q_latent", "g_kv_latent", "g_q_head", "g_k_head"]
    vec_w = dict(b_ada=b_ada, g_norm1=g_norm1, g_norm2=g_norm2, g_q_latent=g_q_latent, g_kv_latent=g_kv_latent,
                 g_q_head=g_q_head, g_k_head=g_k_head)
    vec_m = dict(b_ada=m_b_ada, g_norm1=m_g_norm1, g_norm2=m_g_norm2, g_q_latent=m_g_q_latent,
                 g_kv_latent=m_g_kv_latent, g_q_head=m_g_q_head, g_k_head=m_g_k_head)
    vec_v = dict(b_ada=v_b_ada, g_norm1=v_g_norm1, g_norm2=v_g_norm2, g_q_latent=v_g_q_latent,
                 g_kv_latent=v_g_kv_latent, g_q_head=v_g_q_head, g_k_head=v_g_k_head)
    d_ada = jnp.concatenate([red_1[0:1], red_1[1:2], red_2[3:4], red_2[0:1], red_2[1:2], red_l[0:1]], axis=1)
    vec_parts = dict(b_ada=d_ada, g_norm1=red_1[2:3], g_norm2=red_2[2:3], g_q_latent=red_lat[0:1],
                     g_kv_latent=red_lat[1:2], g_q_head=red_qh[0:1], g_k_head=red_kh[0:1])
    widths = [-(-vec_w[nm].shape[1] // LANE) * LANE for nm in vec_names]
    offs = [sum(widths[:i]) for i in range(len(widths))]
    pack = lambda d: jnp.concatenate([_pad_cols(d[nm][:, :vec_w[nm].shape[1]], wd) for nm, wd in zip(vec_names, widths)], axis=1)
    nvec = sum(widths) + LANE
    no_loss = jnp.zeros((1, LANE), F32)
    parts = jnp.concatenate([pack(vec_parts), loss_p[0:1, :]], axis=1)
    parts_all = _gather_blocks([jnp.broadcast_to(parts, (8, nvec))], name="gather_vec_grads",
                               in_vmem=True)[0][:, 0, :]
    gvec, dvec, nmvec, nvvec = _adamw_vec(parts_all, *[jnp.concatenate([pack(d), no_loss], axis=1)
                                                       for d in (vec_w, vec_m, vec_v)])
    loss = gvec[0, nvec - LANE]
    unpack = lambda a: {nm: a[:, o_:o_ + vec_w[nm].shape[1]] for nm, o_ in zip(vec_names, offs)}
    gvec, dvec, nmvec, nvvec = unpack(gvec), unpack(dvec), unpack(nmvec), unpack(nvvec)

    dada_all = lax.dynamic_slice_in_dim(parts_all[:, :6 * D], chip * n_ada, n_ada, axis=1)
    cact_t = jax.nn.silu(c_all).T
    g_ada, d_ada_w, nm_ada, nv_ada = _adamw_ada(cact_t, dada_all, w_ada[0], m_w_ada[0], v_w_ada[0])

    ms = dict(w_in=m_w_in, w_uq=m_w_uq, w_ukv=m_w_ukv, w_proj_mla=m_w_proj_mla, w_proj_sb=m_w_proj_sb,
              w_out=m_w_out, w_ffn_in=m_w_ffn_in, w_ffn_out=m_w_ffn_out)
    vs = dict(w_in=v_w_in, w_uq=v_w_uq, w_ukv=v_w_ukv, w_proj_mla=v_w_proj_mla, w_proj_sb=v_w_proj_sb,
              w_out=v_w_out, w_ffn_in=v_w_ffn_in, w_ffn_out=v_w_ffn_out)
    G, DL, NM, NV = {}, {}, {}, {}
    for nm, mine, other in zip(names, reduced, from_sibling2):
        g_, d_, m_, v_ = _adamw(mats[nm], mine, other, c_idx, ms[nm][0], vs[nm][0], name="adamw_" + nm)
        G[nm], DL[nm], NM[nm], NV[nm] = g_[None], d_[None], m_[None], v_[None]
    G["w_ada"], DL["w_ada"], NM["w_ada"], NV["w_ada"] = g_ada[None], d_ada_w[None], nm_ada[None], nv_ada[None]
    for nm in vec_names:
        G[nm], DL[nm], NM[nm], NV[nm] = gvec[nm], dvec[nm], nmvec[nm], nvvec[nm]

    order = ["w_ada", "b_ada", "g_norm1", "g_norm2", "w_in", "g_q_latent", "g_kv_latent", "w_uq", "w_ukv",
             "g_q_head", "g_k_head", "w_proj_mla", "w_proj_sb", "w_out", "w_ffn_in", "w_ffn_out"]
    return (loss, grad_x[None], *[G[n] for n in order], *[DL[n] for n in order],
            *[NM[n] for n in order], *[NV[n] for n in order])
```

```python
import functools
import math

import jax
import jax.numpy as jnp
from jax import lax
from jax.experimental import pallas as pl
from jax.experimental.pallas import tpu as pltpu

F32 = jnp.float32
BF16 = jnp.bfloat16
MESH = pl.DeviceIdType.MESH

EPS = 1e-6
ROPE_THETA = 10000.0
NOPE = 128
ROPE = 64
QK_DIM = NOPE + ROPE
HEAD_PAD = 256
HEAD = 128
N_DEV = 8
LANE = 128
VMEM_LIMIT = 48 * 1024 * 1024

ADAM_LR = 0.001
ADAM_B1 = 0.9
ADAM_B2 = 0.999
ADAM_EPS = 1e-08
ADAM_WD = 0.01
ADAM_STEP = 10


def _tile(n, target):
    if n <= target:
        return n
    t = (target // LANE) * LANE
    while t >= LANE:
        if n % t == 0:
            return t
        t -= LANE
    return n


def _row_tile(rows, row_bytes, budget=24 * 1024 * 1024):
    cap = max(8, budget // (2 * row_bytes))
    best = None
    for t in range(8, min(rows, cap) + 1, 8):
        if rows % t == 0:
            best = t
    return best if best is not None else rows


def _params(sem):
    return pltpu.CompilerParams(dimension_semantics=sem, vmem_limit_bytes=VMEM_LIMIT)


def _rows(tm, w, col=0):
    return pl.BlockSpec((tm, w), lambda i: (i, col))


def _vec(w, col=0, rows=1):
    return pl.BlockSpec((rows, w), lambda i: (0, col))


MM_VMEM_BUDGET = 36 * 1024 * 1024


def _mm(a, b, *, name, ta=False, tb=False, out_dtype=F32, a_fn=None, bias=None, tm=1024, tn=1024, pieces=None,
        col_perm=None, fused=None):
    M = a.shape[1] if ta else a.shape[0]
    K = a.shape[0] if ta else a.shape[1]
    N = b.shape[0] if tb else b.shape[1]
    assert K == (b.shape[1] if tb else b.shape[0]), (a.shape, b.shape, ta, tb)
    if pieces == "cols":
        tm, tn = _tile(M // 2, tm), _tile(N // 4, tn)
        assert (M // 2) % tm == 0 and (N // 4) % tn == 0
    elif pieces == "rows":
        tm, tn = M // 4, _tile(N, tn)
    else:
        tm, tn = _tile(M, tm), _tile(N, tn)
    sa, sb, so = a.dtype.itemsize, b.dtype.itemsize, jnp.dtype(out_dtype).itemsize

    def fits(tk):
        return 2 * tk * (tm * sa + tn * sb) + tm * tn * (2 * so + 4) <= MM_VMEM_BUDGET

    tk = K
    while not fits(tk):
        smaller = _tile(K, tk - LANE)
        if smaller >= tk:
            break
        tk = smaller
    nk = K // tk
    dn = (((0 if ta else 1,), (1 if tb else 0,)), ((), ()))
    b_outer = nk == 1 and a.size * sa * (N // tn) < b.size * sb * (M // tm)

    n_extra = len(fused[1]) if fused else 0
    n_out = len(fused[2]) if fused else 1

    def body(*refs):
        a_ref, b_ref = refs[:2]
        bias_ref = refs[2] if bias is not None else None
        first = 3 if bias is not None else 2
        extra_refs = refs[first:first + n_extra]
        out_refs = refs[first + n_extra:first + n_extra + n_out]
        o_ref = out_refs[0]
        av = a_ref[...]
        if a_fn is not None:
            av = a_fn(av.astype(F32))
        part = lax.dot_general(av.astype(BF16), b_ref[...].astype(BF16), dn, preferred_element_type=F32)

        def finish(r):
            if bias is not None:
                r = r + bias_ref[...]
            if fused:
                for ref, tile in zip(out_refs, fused[0](r, *[e[...] for e in extra_refs])):
                    ref[...] = tile.astype(ref.dtype)
            elif pieces == "rows":
                o_ref[0] = r[:tm // 2].astype(o_ref.dtype)
                o_ref[1] = r[tm // 2:].astype(o_ref.dtype)
            else:
                o_ref[...] = r.astype(o_ref.dtype)

        if nk == 1:
            finish(part)
        else:
            acc_ref = refs[-1]
            k = pl.program_id(2)

            @pl.when(k == 0)
            def _():
                acc_ref[...] = part

            @pl.when(k > 0)
            def _():
                acc_ref[...] += part

            @pl.when(k == nk - 1)
            def _():
                finish(acc_ref[...])

    def ij(g0, g1):
        return (g1, g0) if b_outer else (g0, g1)

    def amap(g0, g1, k):
        i, _ = ij(g0, g1)
        return (k, i) if ta else (i, k)

    def bmap(g0, g1, k):
        _, j = ij(g0, g1)
        return (j, k) if tb else (k, j)

    in_specs = [pl.BlockSpec((tk, tm) if ta else (tm, tk), amap), pl.BlockSpec((tn, tk) if tb else (tk, tn), bmap)]
    args = [a, b]
    if bias is not None:
        in_specs.append(pl.BlockSpec((1, tn), lambda g0, g1, k: (0, ij(g0, g1)[1])))
        args.append(bias)
    grid = (N // tn, M // tm, nk) if b_outer else (M // tm, N // tn, nk)
    if pieces == "cols":
        ni, nj = M // 2 // tm, N // 4 // tn

        def omap(g0, g1, k):
            i, j = ij(g0, g1)
            j = col_perm(j) if col_perm else j
            return (i // ni, j // nj, i % ni, j % nj)

        out_spec = pl.BlockSpec((None, None, tm, tn), omap)
        out_shape = jax.ShapeDtypeStruct((2, 4, M // 2, N // 4), out_dtype)
    elif pieces == "rows":
        out_spec = pl.BlockSpec((2, None, tm // 2, tn), lambda g0, g1, k: (0, ij(g0, g1)[0], 0, ij(g0, g1)[1]))
        out_shape = jax.ShapeDtypeStruct((2, 4, tm // 2, N), out_dtype)
    else:
        out_spec = pl.BlockSpec((tm, tn), lambda g0, g1, k: ij(g0, g1))
        out_shape = jax.ShapeDtypeStruct((M, N), out_dtype)
    if fused:
        for arr, width in fused[1]:
            in_specs.append(pl.BlockSpec((tm, width), lambda g0, g1, k: ij(g0, g1)))
            args.append(arr)
        out_spec = [pl.BlockSpec((tm, width), lambda g0, g1, k: ij(g0, g1)) for _, width, _ in fused[2]]
        out_shape = [jax.ShapeDtypeStruct((M, cols), dt) for cols, _, dt in fused[2]]
    return pl.pallas_call(
        body, name=name, grid=grid, in_specs=in_specs, out_specs=out_spec, out_shape=out_shape,
        scratch_shapes=[pltpu.VMEM((tm, tn), F32)] if nk > 1 else [],
        compiler_params=_params(("parallel", "parallel", "arbitrary")),
    )(*args)


def _rms_rows(v):
    return lax.rsqrt(jnp.mean(v * v, axis=-1, keepdims=True) + EPS)


def _rmsmod(x, g, ada, sc_col, sh_col, *, name):
    S, D = x.shape
    tm = _tile(S, 256)

    def body(x_ref, g_ref, sc_ref, sh_ref, h_ref):
        xv = x_ref[...]
        h = (xv * _rms_rows(xv) * g_ref[...]) * (1.0 + sc_ref[...]) + sh_ref[...]
        h_ref[...] = h.astype(h_ref.dtype)

    return pl.pallas_call(
        body, name=name, grid=(S // tm,),
        in_specs=[_rows(tm, D), _vec(D), _vec(D, sc_col), _vec(D, sh_col)],
        out_specs=_rows(tm, D), out_shape=jax.ShapeDtypeStruct((S, D), BF16),
        compiler_params=_params(("parallel",)),
    )(x, g, ada, ada)


def _latent_norm(proj, g_q, g_kv, ql):
    S = proj.shape[0]
    tm = _tile(S, 512)

    def body(cq_ref, ckv_ref, gq_ref, gkv_ref, oq_ref, okv_ref):
        cq = cq_ref[...].astype(F32)
        oq_ref[...] = (cq * _rms_rows(cq) * gq_ref[...]).astype(BF16)
        ckv = ckv_ref[...].astype(F32)
        okv_ref[...] = (ckv * _rms_rows(ckv) * gkv_ref[...]).astype(BF16)

    return pl.pallas_call(
        body, name="latent_norm", grid=(S // tm,),
        in_specs=[_rows(tm, ql, 0), _rows(tm, ql, 1), _vec(ql), _vec(ql)],
        out_specs=[_rows(tm, ql), _rows(tm, ql)],
        out_shape=[jax.ShapeDtypeStruct((S, ql), BF16)] * 2,
        compiler_params=_params(("parallel",)),
    )(proj, proj, g_q, g_kv)


def _rope_fwd(y, c, s1, s2):
    return y * c + pltpu.roll(y, ROPE // 2, 1) * s1 + pltpu.roll(y, HEAD_PAD - ROPE // 2, 1) * s2


def _rope_bwd(d, c, s1, s2):
    return d * c + pltpu.roll(d * s1, HEAD_PAD - ROPE // 2, 1) + pltpu.roll(d * s2, ROPE // 2, 1)


def _head_rms(v):
    return lax.rsqrt(jnp.sum(v * v, axis=-1, keepdims=True) * (1.0 / QK_DIM) + EPS)


def _q_prep(q0, g_qh, tabs, nh):
    S = q0.shape[0]
    tm = _tile(S, 256)

    def body(q_ref, g_ref, c_ref, s1_ref, s2_ref, o_ref):
        c, s1, s2, g = c_ref[...], s1_ref[...], s2_ref[...], g_ref[...]
        for h in range(nh):
            sl = slice(h * HEAD_PAD, (h + 1) * HEAD_PAD)
            xs = q_ref[:, sl].astype(F32)
            o_ref[:, sl] = (_rope_fwd(xs * _head_rms(xs) * g, c, s1, s2) * (QK_DIM ** -0.5)).astype(BF16)

    w = nh * HEAD_PAD
    return pl.pallas_call(
        body, name="mla_q_prep", grid=(S // tm,),
        in_specs=[_rows(tm, w), _vec(HEAD_PAD)] + [_rows(tm, HEAD_PAD)] * 3,
        out_specs=_rows(tm, w), out_shape=jax.ShapeDtypeStruct((S, w), BF16),
        compiler_params=_params(("parallel",)),
    )(q0, g_qh, *tabs)


def _k_prep(kv0, proj, kpe_col, g_kh, tabs, nh):
    S = kv0.shape[0]
    tm = _tile(S, 256)

    def body(kv_ref, kpe_ref, g_ref, c_ref, s1_ref, s2_ref, o_ref):
        c, s1, s2, g = c_ref[...], s1_ref[...], s2_ref[...], g_ref[...]
        kpe = kpe_ref[...].astype(F32)
        for h in range(nh):
            k0 = jnp.concatenate([kv_ref[:, h * HEAD:(h + 1) * HEAD].astype(F32), kpe], axis=1)
            o_ref[:, h * HEAD_PAD:(h + 1) * HEAD_PAD] = _rope_fwd(k0 * _head_rms(k0) * g, c, s1, s2).astype(BF16)

    return pl.pallas_call(
        body, name="mla_k_prep", grid=(S // tm,),
        in_specs=[_rows(tm, nh * HEAD, 0), _rows(tm, LANE, kpe_col), _vec(HEAD_PAD)] + [_rows(tm, HEAD_PAD)] * 3,
        out_specs=_rows(tm, nh * HEAD_PAD), out_shape=jax.ShapeDtypeStruct((S, nh * HEAD_PAD), BF16),
        compiler_params=_params(("parallel",)),
    )(kv0, proj, g_kh, *tabs)


def _gate_merge(pa, pb, proj, gla_col, glb_col):
    S, D = pa.shape
    tm = _tile(S, 256)

    def body(pa_ref, pb_ref, ga_ref, gb_ref, o_ref):
        o_ref[...] = (jax.nn.sigmoid(ga_ref[...].astype(F32)) * pa_ref[...] + jax.nn.sigmoid(gb_ref[...].astype(F32)) * pb_ref[...]).astype(BF16)

    return pl.pallas_call(
        body, name="gate_merge", grid=(S // tm,),
        in_specs=[_rows(tm, D), _rows(tm, D), _rows(tm, D, gla_col), _rows(tm, D, glb_col)],
        out_specs=_rows(tm, D), out_shape=jax.ShapeDtypeStruct((S, D), BF16),
        compiler_params=_params(("parallel",)),
    )(pa, pb, proj, proj)


def _resid_rmsmod(x, o, g, ada, gt_col, sc_col, sh_col):
    S, D = x.shape
    tm = _tile(S, 256)

    def body(x_ref, o_ref, g_ref, gt_ref, sc_ref, sh_ref, x2_ref, h_ref):
        x2 = x_ref[...] + gt_ref[...] * o_ref[...]
        x2_ref[...] = x2
        h_ref[...] = ((x2 * _rms_rows(x2) * g_ref[...]) * (1.0 + sc_ref[...]) + sh_ref[...]).astype(BF16)

    return pl.pallas_call(
        body, name="resid_rmsmod2", grid=(S // tm,),
        in_specs=[_rows(tm, D), _rows(tm, D), _vec(D), _vec(D, gt_col), _vec(D, sc_col), _vec(D, sh_col)],
        out_specs=[_rows(tm, D), _rows(tm, D)],
        out_shape=[jax.ShapeDtypeStruct((S, D), F32), jax.ShapeDtypeStruct((S, D), BF16)],
        compiler_params=_params(("parallel",)),
    )(x, o, g, ada, ada, ada)


def _swiglu_tile(ib):
    def fn(r):
        pairs = r.shape[1] // (2 * ib)
        act = [jax.nn.silu(r[:, 2 * p * ib:(2 * p + 1) * ib]) * r[:, (2 * p + 1) * ib:(2 * p + 2) * ib] for p in range(pairs)]
        return r, jnp.concatenate(act, axis=1) if pairs > 1 else act[0]
    return fn


def _swiglu_bwd_tile(ib):
    def fn(d, ff):
        ff = ff.astype(F32)
        out = []
        for p in range(d.shape[1] // ib):
            dp = d[:, p * ib:(p + 1) * ib]
            g = ff[:, 2 * p * ib:(2 * p + 1) * ib]
            u = ff[:, (2 * p + 1) * ib:(2 * p + 2) * ib]
            sg = jax.nn.sigmoid(g)
            out += [dp * u * sg * (1.0 + g * (1.0 - sg)), dp * g * sg]
        return (jnp.concatenate(out, axis=1),)
    return fn


def _loss_head(x2, f, tgt, ada, gt_col):
    S, D = x2.shape
    tm = _tile(S, 256)

    def body(x2_ref, f_ref, t_ref, gt_ref, dy_ref, df_ref, red_ref, loss_ref):
        @pl.when(pl.program_id(0) == 0)
        def _():
            red_ref[...] = jnp.zeros_like(red_ref)
            loss_ref[...] = jnp.zeros_like(loss_ref)

        fv = f_ref[...]
        gt = gt_ref[...]
        err = x2_ref[...] + gt * fv - t_ref[...]
        dy = err * (1.0 / D)
        dy_ref[...] = dy
        df_ref[...] = (dy * gt).astype(BF16)
        red_ref[0:1, :] += jnp.sum(dy * fv, axis=0, keepdims=True)
        loss_ref[...] += (0.5 / D) * jnp.sum(err * err)

    return pl.pallas_call(
        body, name="loss_head", grid=(S // tm,),
        in_specs=[_rows(tm, D), _rows(tm, D), _rows(tm, D), _vec(D, gt_col)],
        out_specs=[_rows(tm, D), _rows(tm, D), _vec(D, rows=8), _vec(LANE, rows=8)],
        out_shape=[jax.ShapeDtypeStruct((S, D), F32), jax.ShapeDtypeStruct((S, D), BF16),
                   jax.ShapeDtypeStruct((8, D), F32), jax.ShapeDtypeStruct((8, LANE), F32)],
        compiler_params=_params(("arbitrary",)),
    )(x2, f, tgt, ada)


def _rmsmod2_bwd(dh2, x2, dy, o, g, ada, sc_col, gt_col):
    S, D = x2.shape
    tm = _tile(S, 256)

    def body(dh_ref, x2_ref, dy_ref, o_ref, g_ref, sc_ref, gt_ref, dx_ref, do_ref, red_ref):
        @pl.when(pl.program_id(0) == 0)
        def _():
            red_ref[...] = jnp.zeros_like(red_ref)

        dh = dh_ref[...]
        x2 = x2_ref[...]
        gv = g_ref[...]
        mod = 1.0 + sc_ref[...]
        r = _rms_rows(x2)
        xn = x2 * r
        t = dh * xn
        red_ref[0:1, :] += jnp.sum(dh, axis=0, keepdims=True)
        red_ref[1:2, :] += jnp.sum(t * gv, axis=0, keepdims=True)
        red_ref[2:3, :] += jnp.sum(t * mod, axis=0, keepdims=True)
        dxn = dh * gv * mod
        dx = dy_ref[...] + r * (dxn - xn * jnp.mean(dxn * xn, axis=-1, keepdims=True))
        dx_ref[...] = dx
        red_ref[3:4, :] += jnp.sum(dx * o_ref[...], axis=0, keepdims=True)
        do_ref[...] = (dx * gt_ref[...]).astype(BF16)

    return pl.pallas_call(
        body, name="rmsmod2_bwd", grid=(S // tm,),
        in_specs=[_rows(tm, D)] * 4 + [_vec(D), _vec(D, sc_col), _vec(D, gt_col)],
        out_specs=[_rows(tm, D), _rows(tm, D), _vec(D, rows=8)],
        out_shape=[jax.ShapeDtypeStruct((S, D), F32), jax.ShapeDtypeStruct((S, D), BF16),
                   jax.ShapeDtypeStruct((8, D), F32)],
        compiler_params=_params(("arbitrary",)),
    )(dh2, x2, dy, o, g, ada, ada)


def _rmsmod1_bwd(dh, x, dx2, g, ada, sc_col):
    S, D = x.shape
    tm = _tile(S, 256)

    def body(dh_ref, x_ref, dx2_ref, g_ref, sc_ref, gx_ref, red_ref):
        @pl.when(pl.program_id(0) == 0)
        def _():
            red_ref[...] = jnp.zeros_like(red_ref)

        dh = dh_ref[...]
        xv = x_ref[...]
        gv = g_ref[...]
        mod = 1.0 + sc_ref[...]
        r = _rms_rows(xv)
        xn = xv * r
        t = dh * xn
        red_ref[0:1, :] += jnp.sum(dh, axis=0, keepdims=True)
        red_ref[1:2, :] += jnp.sum(t * gv, axis=0, keepdims=True)
        red_ref[2:3, :] += jnp.sum(t * mod, axis=0, keepdims=True)
        dxn = dh * gv * mod
        gx_ref[...] = dx2_ref[...] + r * (dxn - xn * jnp.mean(dxn * xn, axis=-1, keepdims=True))

    return pl.pallas_call(
        body, name="rmsmod1_bwd", grid=(S // tm,),
        in_specs=[_rows(tm, D)] * 3 + [_vec(D), _vec(D, sc_col)],
        out_specs=[_rows(tm, D), _vec(D, rows=8)],
        out_shape=[jax.ShapeDtypeStruct((S, D), F32), jax.ShapeDtypeStruct((8, D), F32)],
        compiler_params=_params(("arbitrary",)),
    )(dh, x, dx2, g, ada)


def _gate_bwd(dm, pa, pb, proj, gla_col, glb_col):
    S, D = pa.shape
    tm = _tile(S, 256)

    def body(dm_ref, pa_ref, pb_ref, la_ref, lb_ref, dpa_ref, dpb_ref, dla_ref, dlb_ref):
        dm_ = dm_ref[...]
        ga = jax.nn.sigmoid(la_ref[...].astype(F32))
        gb = jax.nn.sigmoid(lb_ref[...].astype(F32))
        dpa_ref[...] = (dm_ * ga).astype(BF16)
        dpb_ref[...] = (dm_ * gb).astype(BF16)
        dla_ref[...] = (dm_ * pa_ref[...] * ga * (1.0 - ga)).astype(BF16)
        dlb_ref[...] = (dm_ * pb_ref[...] * gb * (1.0 - gb)).astype(BF16)

    return pl.pallas_call(
        body, name="gate_bwd", grid=(S // tm,),
        in_specs=[_rows(tm, D)] * 3 + [_rows(tm, D, gla_col), _rows(tm, D, glb_col)],
        out_specs=[_rows(tm, D)] * 4, out_shape=[jax.ShapeDtypeStruct((S, D), BF16)] * 4,
        compiler_params=_params(("parallel",)),
    )(dm, pa, pb, proj, proj)


def _q_prep_bwd(dq, q0, g_qh, tabs, nh):
    S = q0.shape[0]
    tm = _tile(S, 256)

    def body(dq_ref, q_ref, g_ref, c_ref, s1_ref, s2_ref, o_ref, red_ref):
        @pl.when(pl.program_id(0) == 0)
        def _():
            red_ref[...] = jnp.zeros_like(red_ref)

        c, s1, s2, g = c_ref[...], s1_ref[...], s2_ref[...], g_ref[...]
        dg = jnp.zeros((1, HEAD_PAD), F32)
        for h in range(nh):
            sl = slice(h * HEAD_PAD, (h + 1) * HEAD_PAD)
            d1 = _rope_bwd(dq_ref[:, sl], c, s1, s2)
            xs = q_ref[:, sl].astype(F32)
            r = _head_rms(xs)
            qn = xs * r
            dg = dg + jnp.sum(d1 * qn, axis=0, keepdims=True)
            dn = d1 * g
            o_ref[:, sl] = (r * (dn - qn * (jnp.sum(dn * qn, axis=-1, keepdims=True) * (1.0 / QK_DIM)))).astype(BF16)
        red_ref[0:1, :] += dg

    w = nh * HEAD_PAD
    return pl.pallas_call(
        body, name="mla_q_prep_bwd", grid=(S // tm,),
        in_specs=[_rows(tm, w), _rows(tm, w), _vec(HEAD_PAD)] + [_rows(tm, HEAD_PAD)] * 3,
        out_specs=[_rows(tm, w), _vec(HEAD_PAD, rows=8)],
        out_shape=[jax.ShapeDtypeStruct((S, w), BF16), jax.ShapeDtypeStruct((8, HEAD_PAD), F32)],
        compiler_params=_params(("arbitrary",)),
    )(dq, q0, g_qh, *tabs)


def _k_prep_bwd(dk, dv, kv0, proj, kpe_col, g_kh, tabs, nh):
    S = kv0.shape[0]
    tm = _tile(S, 256)
    wv = nh * HEAD

    def body(dk_ref, dv_ref, kv_ref, kpe_ref, g_ref, c_ref, s1_ref, s2_ref, o_ref, dpe_ref, red_ref):
        @pl.when(pl.program_id(0) == 0)
        def _():
            red_ref[...] = jnp.zeros_like(red_ref)

        c, s1, s2, g = c_ref[...], s1_ref[...], s2_ref[...], g_ref[...]
        kpe = kpe_ref[...].astype(F32)
        dg = jnp.zeros((1, HEAD_PAD), F32)
        dpe = jnp.zeros((tm, LANE), F32)
        for h in range(nh):
            d1 = _rope_bwd(dk_ref[:, h * HEAD_PAD:(h + 1) * HEAD_PAD], c, s1, s2)
            k0 = jnp.concatenate([kv_ref[:, h * HEAD:(h + 1) * HEAD].astype(F32), kpe], axis=1)
            r = _head_rms(k0)
            kn = k0 * r
            dg = dg + jnp.sum(d1 * kn, axis=0, keepdims=True)
            dn = d1 * g
            dk0 = r * (dn - kn * (jnp.sum(dn * kn, axis=-1, keepdims=True) * (1.0 / QK_DIM)))
            o_ref[:, h * HEAD:(h + 1) * HEAD] = dk0[:, :HEAD].astype(BF16)
            dpe = dpe + dk0[:, HEAD:]
        o_ref[:, wv:] = dv_ref[...].astype(BF16)
        dpe_ref[...] = dpe.astype(BF16)
        red_ref[0:1, :] += dg

    return pl.pallas_call(
        body, name="mla_k_prep_bwd", grid=(S // tm,),
        in_specs=[_rows(tm, nh * HEAD_PAD), _rows(tm, wv), _rows(tm, wv, 0), _rows(tm, LANE, kpe_col),
                  _vec(HEAD_PAD)] + [_rows(tm, HEAD_PAD)] * 3,
        out_specs=[_rows(tm, 2 * wv), _rows(tm, LANE), _vec(HEAD_PAD, rows=8)],
        out_shape=[jax.ShapeDtypeStruct((S, 2 * wv), BF16), jax.ShapeDtypeStruct((S, LANE), BF16),
                   jax.ShapeDtypeStruct((8, HEAD_PAD), F32)],
        compiler_params=_params(("arbitrary",)),
    )(dk, dv, kv0, proj, g_kh, *tabs)


def _latent_norm_bwd(dcqn, dckvn, proj, g_q, g_kv, ql):
    S = proj.shape[0]
    tm = _tile(S, 512)

    def body(dq_ref, dkv_ref, cq_ref, ckv_ref, gq_ref, gkv_ref, oq_ref, okv_ref, red_ref):
        @pl.when(pl.program_id(0) == 0)
        def _():
            red_ref[...] = jnp.zeros_like(red_ref)

        for row, (d_ref, c_ref, g_ref, o_ref) in enumerate(((dq_ref, cq_ref, gq_ref, oq_ref),
                                                            (dkv_ref, ckv_ref, gkv_ref, okv_ref))):
            d = d_ref[...]
            cv = c_ref[...].astype(F32)
            r = _rms_rows(cv)
            ch = cv * r
            red_ref[row:row + 1, :] += jnp.sum(d * ch, axis=0, keepdims=True)
            dn = d * g_ref[...]
            o_ref[...] = (r * (dn - ch * jnp.mean(dn * ch, axis=-1, keepdims=True))).astype(BF16)

    return pl.pallas_call(
        body, name="latent_norm_bwd", grid=(S // tm,),
        in_specs=[_rows(tm, ql), _rows(tm, ql), _rows(tm, ql, 0), _rows(tm, ql, 1), _vec(ql), _vec(ql)],
        out_specs=[_rows(tm, ql), _rows(tm, ql), _vec(ql, rows=8)],
        out_shape=[jax.ShapeDtypeStruct((S, ql), BF16)] * 2 + [jax.ShapeDtypeStruct((8, ql), F32)],
        compiler_params=_params(("arbitrary",)),
    )(dcqn, dckvn, proj, proj, g_q, g_kv)


NEG = -1e30
ATT_TILE = 512
SB_TILE = 512
SB_SUB = 256
_NT = (((1,), (1,)), ((), ()))
_TN = (((0,), (0,)), ((), ()))


def _dot(a, b, dn=(((1,), (0,)), ((), ()))):
    return lax.dot_general(a, b, dn, preferred_element_type=F32)


def _key_rows(kb, t):
    return pl.ds(pl.multiple_of(kb * t, t), t)


def _diag_mask(t, strict):
    r = lax.broadcasted_iota(jnp.int32, (t, t), 0)
    c = lax.broadcasted_iota(jnp.int32, (t, t), 1)
    return c < r if strict else c <= r


def _mla_fwd(q, k, kv0, nh):
    S = q.shape[0]
    t = _tile(S, ATT_TILE)

    def body(q_ref, k_ref, v_ref, o_ref, lse_ref):
        i = pl.program_id(1)
        qv = q_ref[...]

        def block(kb, carry, masked):
            m, l, acc = carry
            rows = _key_rows(kb, t)
            s = _dot(qv, k_ref[rows, :], _NT)
            if masked:
                s = jnp.where(_diag_mask(t, False), s, NEG)
            m_new = jnp.maximum(m, jnp.max(s, axis=-1, keepdims=True))
            alpha = jnp.exp(m - m_new)
            p = jnp.exp(s - m_new)
            l = alpha * l + jnp.sum(p, axis=-1, keepdims=True)
            acc = alpha * acc + _dot(p.astype(BF16), v_ref[rows, :].astype(BF16))
            return m_new, l, acc

        init = (jnp.full((t, 1), NEG, F32), jnp.zeros((t, 1), F32), jnp.zeros((t, HEAD), F32))
        carry = lax.fori_loop(0, i, lambda kb, c: block(kb, c, False), init)
        m, l, acc = block(i, carry, True)
        o_ref[...] = acc / l
        lse_ref[...] = m + jnp.log(l)

    return pl.pallas_call(
        body, name="mla_attn_fwd", grid=(nh, S // t),
        in_specs=[pl.BlockSpec((t, HEAD_PAD), lambda h, i: (i, h)),
                  pl.BlockSpec((S, HEAD_PAD), lambda h, i: (0, h)),
                  pl.BlockSpec((S, HEAD), lambda h, i: (0, nh + h))],
        out_specs=[pl.BlockSpec((t, HEAD), lambda h, i: (i, h)),
                   pl.BlockSpec((None, t, 1), lambda h, i: (h, i, 0))],
        out_shape=[jax.ShapeDtypeStruct((S, nh * HEAD), F32), jax.ShapeDtypeStruct((nh, S, 1), F32)],
        compiler_params=_params(("parallel", "arbitrary")),
    )(q, k, kv0)


def _mla_bwd(q, k, kv0, o, do, lse, nh):
    S = q.shape[0]
    t = _tile(S, ATT_TILE)
    scale = QK_DIM ** -0.5

    def body(q_ref, k_ref, v_ref, o_ref, do_ref, lse_ref, dq_ref, dk_ref, dv_ref):
        i = pl.program_id(1)

        @pl.when(i == 0)
        def _():
            dk_ref[...] = jnp.zeros_like(dk_ref)
            dv_ref[...] = jnp.zeros_like(dv_ref)

        qv = q_ref[...]
        dov = do_ref[...]
        delta = jnp.sum(dov * o_ref[...], axis=-1, keepdims=True)
        dob = dov.astype(BF16)
        lse = lse_ref[...]

        def block(kb, dq, masked):
            rows = _key_rows(kb, t)
            ks = k_ref[rows, :]
            vs = v_ref[rows, :].astype(BF16)
            p = jnp.exp(_dot(qv, ks, _NT) - lse)
            if masked:
                p = jnp.where(_diag_mask(t, False), p, 0.0)
            ds = (p * (_dot(dob, vs, _NT) - delta)).astype(BF16)
            dk_ref[rows, :] += _dot(ds, qv, _TN)
            dv_ref[rows, :] += _dot(p.astype(BF16), dob, _TN)
            return dq + _dot(ds, ks)

        dq = lax.fori_loop(0, i, lambda kb, c: block(kb, c, False), jnp.zeros((t, HEAD_PAD), F32))
        dq_ref[...] = block(i, dq, True) * scale

    return pl.pallas_call(
        body, name="mla_attn_bwd", grid=(nh, S // t),
        in_specs=[pl.BlockSpec((t, HEAD_PAD), lambda h, i: (i, h)),
                  pl.BlockSpec((S, HEAD_PAD), lambda h, i: (0, h)),
                  pl.BlockSpec((S, HEAD), lambda h, i: (0, nh + h)),
                  pl.BlockSpec((t, HEAD), lambda h, i: (i, h)),
                  pl.BlockSpec((t, HEAD), lambda h, i: (i, h)),
                  pl.BlockSpec((None, t, 1), lambda h, i: (h, i, 0))],
        out_specs=[pl.BlockSpec((t, HEAD_PAD), lambda h, i: (i, h)),
                   pl.BlockSpec((S, HEAD_PAD), lambda h, i: (0, h)),
                   pl.BlockSpec((S, HEAD), lambda h, i: (0, h))],
        out_shape=[jax.ShapeDtypeStruct((S, nh * HEAD_PAD), F32), jax.ShapeDtypeStruct((S, nh * HEAD_PAD), F32),
                   jax.ShapeDtypeStruct((S, nh * HEAD), F32)],
        compiler_params=_params(("parallel", "arbitrary")),
    )(q, k, kv0, o, do, lse)


def _tri(n, cmp):
    r = lax.broadcasted_iota(jnp.int32, (n, n), 0)
    c = lax.broadcasted_iota(jnp.int32, (n, n), 1)
    return jnp.where(cmp(r, c), 1.0, 0.0).astype(BF16)


def _sb_block(qv, ks, run, upper, t, masked):
    z = _dot(qv, ks, _NT)
    lb = jnp.minimum(z, 0.0) - jnp.log(1.0 + jnp.exp(-jnp.abs(z)))
    lom = lb - z
    mask = _diag_mask(t, True) if masked else None
    if masked:
        lom = jnp.where(mask, lom, 0.0)
    tails = []
    for sblk in reversed(range(t // SB_SUB)):
        part = lom[:, sblk * SB_SUB:(sblk + 1) * SB_SUB]
        tails.append(_dot(part.astype(BF16), upper) + run)
        run = run + jnp.sum(part, axis=-1, keepdims=True)
    a = jnp.exp(lb + jnp.concatenate(tails[::-1], axis=1))
    if masked:
        a = jnp.where(mask, a, 0.0)
    return a, lb, mask, run


def _sb_fwd(proj, q_col, k_col, v_col, nh):
    S = proj.shape[0]
    t = _tile(S, SB_TILE)
    assert S // t <= LANE
    scale = HEAD ** -0.5

    def body(q_ref, k_ref, v_ref, o_ref, runs_ref):
        i = pl.program_id(1)
        qv = (q_ref[...].astype(F32) * scale).astype(BF16)
        upper = _tri(SB_SUB, lambda j, s: j > s)
        lane = lax.broadcasted_iota(jnp.int32, (t, LANE), 1)

        def block(kb, carry, masked):
            run, acc, runs = carry
            runs = jnp.where(lane == kb, run, runs)
            rows = _key_rows(kb, t)
            a, _, _, run = _sb_block(qv, k_ref[rows, :].astype(BF16), run, upper, t, masked)
            return run, acc + _dot(a.astype(BF16), v_ref[rows, :].astype(BF16)), runs

        carry = block(i, (jnp.zeros((t, 1), F32), jnp.zeros((t, HEAD), F32), jnp.zeros((t, LANE), F32)), True)
        _, o_ref[...], runs_ref[...] = lax.fori_loop(0, i, lambda j, c: block(i - 1 - j, c, False), carry)

    return pl.pallas_call(
        body, name="sb_attn_fwd", grid=(nh, S // t),
        in_specs=[pl.BlockSpec((t, HEAD), lambda h, i: (i, q_col + h)),
                  pl.BlockSpec((S, HEAD), lambda h, i: (0, k_col + h)),
                  pl.BlockSpec((S, HEAD), lambda h, i: (0, v_col + h))],
        out_specs=[pl.BlockSpec((t, HEAD), lambda h, i: (i, h)), pl.BlockSpec((None, t, LANE), lambda h, i: (h, i, 0))],
        out_shape=[jax.ShapeDtypeStruct((S, nh * HEAD), F32), jax.ShapeDtypeStruct((nh, S, LANE), F32)],
        compiler_params=_params(("parallel", "arbitrary")),
    )(proj, proj, proj)


def _sb_bwd(proj, q_col, k_col, v_col, dy, runs, nh):
    S = proj.shape[0]
    t = _tile(S, SB_TILE)
    scale = HEAD ** -0.5

    def body(q_ref, k_ref, v_ref, dy_ref, runs_ref, dq_ref, dk_ref, dv_ref):
        i = pl.program_id(1)

        @pl.when(i == 0)
        def _():
            dk_ref[...] = jnp.zeros_like(dk_ref)
            dv_ref[...] = jnp.zeros_like(dv_ref)

        qv = (q_ref[...].astype(F32) * scale).astype(BF16)
        dyb = dy_ref[...].astype(BF16)
        runs_v = runs_ref[...]
        lane = lax.broadcasted_iota(jnp.int32, (t, LANE), 1)
        upper = _tri(SB_SUB, lambda j, s: j > s)
        before = _tri(SB_SUB, lambda s, j: s < j)

        def block(kb, carry, masked):
            prefix, dq = carry
            rows = _key_rows(kb, t)
            ks = k_ref[rows, :].astype(BF16)
            vs = v_ref[rows, :].astype(BF16)
            run = jnp.sum(jnp.where(lane == kb, runs_v, 0.0), axis=-1, keepdims=True)
            a, lb, mask, _ = _sb_block(qv, ks, run, upper, t, masked)
            dl = a * _dot(dyb, vs, _NT)
            lefts = []
            for sblk in range(t // SB_SUB):
                part = dl[:, sblk * SB_SUB:(sblk + 1) * SB_SUB]
                lefts.append(_dot(part.astype(BF16), before) + prefix)
                prefix = prefix + jnp.sum(part, axis=-1, keepdims=True)
            beta = jnp.exp(lb)
            dz = dl * (1.0 - beta) - beta * jnp.concatenate(lefts, axis=1)
            if masked:
                dz = jnp.where(mask, dz, 0.0)
            dz = dz.astype(BF16)
            dk_ref[rows, :] += _dot(dz, qv, _TN)
            dv_ref[rows, :] += _dot(a.astype(BF16), dyb, _TN)
            return prefix, dq + _dot(dz, ks)

        carry = lax.fori_loop(0, i, lambda kb, c: block(kb, c, False),
                              (jnp.zeros((t, 1), F32), jnp.zeros((t, HEAD), F32)))
        dq_ref[...] = block(i, carry, True)[1] * scale

    full = pl.BlockSpec((S, HEAD), lambda h, i: (0, h))
    tile = pl.BlockSpec((t, HEAD), lambda h, i: (i, h))
    return pl.pallas_call(
        body, name="sb_attn_bwd", grid=(nh, S // t),
        in_specs=[pl.BlockSpec((t, HEAD), lambda h, i: (i, q_col + h)),
                  pl.BlockSpec((S, HEAD), lambda h, i: (0, k_col + h)),
                  pl.BlockSpec((S, HEAD), lambda h, i: (0, v_col + h)), tile,
                  pl.BlockSpec((None, t, LANE), lambda h, i: (h, i, 0))],
        out_specs=[tile, full, full],
        out_shape=[jax.ShapeDtypeStruct((S, nh * HEAD), F32)] * 3,
        compiler_params=_params(("parallel", "arbitrary")),
    )(proj, proj, proj, dy, runs)


def _place():
    return lax.axis_index("x"), lax.axis_index("y"), lax.axis_index("c")


def _other_chips(x, y):
    return [(1 - x, y), (x, 1 - y), (1 - x, 1 - y)]


def _dev_index(p):
    return 4 * p[0] + 2 * p[1] + p[2]


def _gather_blocks(blocks, *, name, in_vmem):
    n = len(blocks)
    per = 7

    def body(*refs):
        ins, outs = refs[:n], refs[n:2 * n]
        send_sems, recv_sems, local_sems = refs[2 * n:]
        x, y, c = _place()
        me, sibling = (x, y, c), (x, y, 1 - c)
        chips = _other_chips(x, y)

        def slot(a, p):
            return outs[a].at[_dev_index(p)]

        def copy(a, k, block, to, src=None):
            return pltpu.make_async_remote_copy(
                src_ref=slot(a, block) if src is None else src, dst_ref=slot(a, block),
                send_sem=send_sems.at[a * per + k], recv_sem=recv_sems.at[a * per + k],
                device_id=to, device_id_type=MESH)

        mine = [pltpu.make_async_copy(ins[a], slot(a, me), local_sems.at[a]) for a in range(n)] if in_vmem else []
        for cp in mine:
            cp.start()
        first = []
        for a in range(n):
            first.append(copy(a, 0, me, sibling, src=ins[a]))
            first += [copy(a, 1 + j, me, (*chip, c), src=ins[a]) for j, chip in enumerate(chips)]
        for cp in first:
            cp.start()
        passed = []
        for a in range(n):
            for j, chip in enumerate(chips):
                copy(a, 1 + j, (*chip, c), me).wait_recv()
                cp = copy(a, 4 + j, (*chip, c), sibling)
                cp.start()
                passed.append(cp)
        for a in range(n):
            copy(a, 0, sibling, me).wait_recv()
            for j, chip in enumerate(chips):
                copy(a, 4 + j, (*chip, 1 - c), me).wait_recv()
        for cp in first + passed:
            cp.wait_send()
        for cp in mine:
            cp.wait()

    space = pltpu.VMEM if in_vmem else pl.ANY
    spec = pl.BlockSpec(memory_space=space)
    outs = pl.pallas_call(
        body, name=name, in_specs=[spec] * n, out_specs=[spec] * n,
        out_shape=[jax.ShapeDtypeStruct((N_DEV,) + b.shape, b.dtype) for b in blocks],
        scratch_shapes=[pltpu.SemaphoreType.DMA((n * per,)), pltpu.SemaphoreType.DMA((n * per,)),
                        pltpu.SemaphoreType.DMA((n,))],
        compiler_params=pltpu.CompilerParams(vmem_limit_bytes=VMEM_LIMIT),
    )(*blocks)
    return list(outs)


def _sibling_swap(arrs, *, name, whole=False):
    n = len(arrs)

    def body(*refs):
        ins, outs = refs[:n], refs[n:2 * n]
        send_sems, recv_sems = refs[2 * n:]
        x, y, c = _place()
        copies = [pltpu.make_async_remote_copy(
            src_ref=ins[a] if whole else ins[a].at[1 - c], dst_ref=outs[a],
            send_sem=send_sems.at[a], recv_sem=recv_sems.at[a],
            device_id=(x, y, 1 - c), device_id_type=MESH) for a in range(n)]
        for cp in copies:
            cp.start()
        for cp in copies:
            cp.wait()

    spec = pl.BlockSpec(memory_space=pl.ANY)
    return list(pl.pallas_call(
        body, name=name, in_specs=[spec] * n, out_specs=[spec] * n,
        out_shape=[jax.ShapeDtypeStruct(a.shape if whole else a.shape[1:], a.dtype) for a in arrs],
        scratch_shapes=[pltpu.SemaphoreType.DMA((n,)), pltpu.SemaphoreType.DMA((n,))],
    )(*arrs))


_HBM = pl.BlockSpec(memory_space=pltpu.HBM)
_SEM = pl.BlockSpec(memory_space=pltpu.SEMAPHORE)
_EFFECT = pltpu.SideEffectType.DATAFLOW_SIDE_EFFECTING


def _in_hbm(a):
    return pltpu.with_memory_space_constraint(a, pltpu.HBM)


def _split_copies(srcs, lands, send_sems, recv_sems, plan):
    x, y, c = _place()
    copies = []
    for a, (src, land) in enumerate(zip(srcs, lands)):
        steps = plan(x, y, c)
        for k, (pick, slot, to) in enumerate(steps):
            copies.append(pltpu.make_async_remote_copy(
                src_ref=pick(src), dst_ref=slot(land), send_sem=send_sems.at[a * len(steps) + k],
                recv_sem=recv_sems.at[a * len(steps) + k], device_id=to, device_id_type=MESH))
    return copies


def _split_start(srcs, land_shapes, plan, per, *, name):
    n = len(srcs)

    def body(*refs):
        send_sems, recv_sems = refs[2 * n], refs[2 * n + 1]
        for cp in _split_copies(refs[:n], refs[n:2 * n], send_sems, recv_sems, plan):
            cp.start()
        token = refs[-1]
        token[...] = jnp.zeros_like(token)

    lands = [_in_hbm(lax.empty(s.shape, s.dtype)) for s in land_shapes]
    outs = pl.pallas_call(
        body, name=name,
        out_shape=(pltpu.SemaphoreType.DMA((n * per,)), pltpu.SemaphoreType.DMA((n * per,)),
                   *[pltpu.HBM(s.shape, s.dtype) for s in srcs], *[pltpu.HBM(s.shape, s.dtype) for s in land_shapes],
                   jax.ShapeDtypeStruct((8, LANE), F32)),
        in_specs=[_HBM] * (2 * n),
        out_specs=(_SEM, _SEM, *[_HBM] * (2 * n), pl.BlockSpec(memory_space=pltpu.VMEM)),
        input_output_aliases={i: 2 + i for i in range(2 * n)},
        compiler_params=pltpu.CompilerParams(has_side_effects=_EFFECT),
    )(*[_in_hbm(s) for s in srcs], *lands)
    return outs[0], outs[1], list(outs[2:2 + n]), list(outs[2 + n:2 + 2 * n]), outs[-1]


def _split_wait(send_sems, recv_sems, srcs, lands, after, plan, *, name):
    n = len(srcs)

    def body(*refs):
        for cp in _split_copies(refs[:n], refs[n:2 * n], refs[2 * n], refs[2 * n + 1], plan):
            cp.wait_send()
            cp.wait_recv()

    outs = pl.pallas_call(
        body, name=name,
        out_shape=(*[pltpu.HBM(s.shape, s.dtype) for s in srcs], *[pltpu.HBM(s.shape, s.dtype) for s in lands]),
        in_specs=[_HBM] * (2 * n) + [_SEM, _SEM, pl.BlockSpec(memory_space=pl.ANY)],
        out_specs=tuple([_HBM] * (2 * n)),
        input_output_aliases={i: i for i in range(2 * n)},
        compiler_params=pltpu.CompilerParams(has_side_effects=_EFFECT),
    )(*srcs, *lands, send_sems, recv_sems, after)
    return list(outs[:n]), list(outs[n:])


def _gather_plan(x, y, c):
    slot = lambda land: land.at[_dev_index((x, y, c))]
    whole = lambda src: src
    return [(whole, slot, (x, y, 1 - c))] + [(whole, slot, (px, py, c)) for px, py in _other_chips(x, y)]


def _swap_plan(x, y, c):
    return [(lambda src: src.at[1 - c], lambda land: land, (x, y, 1 - c))]


def _exchange_plan(x, y, c):
    return [(lambda src, k=2 * px + py: src.at[k], lambda land, j=j: land.at[j], (px, py, c))
            for j, (px, py) in enumerate(_other_chips(x, y))]


def _gather_forward(lands, *, name):
    n = len(lands)

    def body(*refs):
        lands_in, outs = refs[:n], refs[n:2 * n]
        send_sems, recv_sems = refs[2 * n:]
        x, y, c = _place()
        copies = []
        for a in range(n):
            for j, (px, py) in enumerate(_other_chips(x, y)):
                copies.append((pltpu.make_async_remote_copy(
                    src_ref=lands_in[a].at[_dev_index((px, py, c))], dst_ref=outs[a].at[_dev_index((px, py, c))],
                    send_sem=send_sems.at[3 * a + j], recv_sem=recv_sems.at[3 * a + j],
                    device_id=(x, y, 1 - c), device_id_type=MESH), a, j, (px, py)))
        for cp, _, _, _ in copies:
            cp.start()
        for cp, a, j, (px, py) in copies:
            cp.wait_send()
            pltpu.make_async_remote_copy(
                src_ref=lands_in[a].at[_dev_index((px, py, 1 - c))], dst_ref=outs[a].at[_dev_index((px, py, 1 - c))],
                send_sem=send_sems.at[3 * a + j], recv_sem=recv_sems.at[3 * a + j],
                device_id=(x, y, 1 - c), device_id_type=MESH).wait_recv()

    spec = pl.BlockSpec(memory_space=pl.ANY)
    return list(pl.pallas_call(
        body, name=name, in_specs=[spec] * n, out_specs=[spec] * n,
        out_shape=[jax.ShapeDtypeStruct(a.shape, a.dtype) for a in lands],
        input_output_aliases={a: a for a in range(n)},
        scratch_shapes=[pltpu.SemaphoreType.DMA((3 * n,)), pltpu.SemaphoreType.DMA((3 * n,))],
    )(*lands))


def _flat2(a, lead):
    return a.reshape(a.shape[:lead] + (-1, a.shape[-1]))


def _pair_sum(g, recv, c_idx, *, name):
    _, nchip, r, w = g.shape
    tm = _tile(r, 256) if r % 8 == 0 else r

    def body(c_ref, g_ref, r_ref, o_ref):
        o_ref[...] = (g_ref[...].astype(F32) + r_ref[...].astype(F32)).astype(o_ref.dtype)

    return pl.pallas_call(
        body, name=name,
        grid_spec=pltpu.PrefetchScalarGridSpec(
            num_scalar_prefetch=1, grid=(nchip, r // tm),
            in_specs=[pl.BlockSpec((None, None, tm, w), lambda k, i, c_ref: (c_ref[0], k, i, 0)),
                      pl.BlockSpec((None, tm, w), lambda k, i, c_ref: (k, i, 0))],
            out_specs=pl.BlockSpec((None, tm, w), lambda k, i, c_ref: (k, i, 0))),
        out_shape=jax.ShapeDtypeStruct((nchip, r, w), BF16),
        compiler_params=_params(("parallel", "parallel")),
    )(c_idx, g, recv)


def _chip_sum(s1, recv, chip_idx, *, name):
    _, r, w = s1.shape
    tm = _tile(r, 256) if r % 8 == 0 else r

    def body(k_ref, s_ref, r_ref, o_ref):
        acc = s_ref[...].astype(F32)
        for j in range(3):
            acc = acc + r_ref[j].astype(F32)
        o_ref[...] = acc

    return pl.pallas_call(
        body, name=name,
        grid_spec=pltpu.PrefetchScalarGridSpec(
            num_scalar_prefetch=1, grid=(r // tm,),
            in_specs=[pl.BlockSpec((None, tm, w), lambda i, k_ref: (k_ref[0], i, 0)),
                      pl.BlockSpec((3, tm, w), lambda i, k_ref: (0, i, 0))],
            out_specs=pl.BlockSpec((tm, w), lambda i, k_ref: (i, 0))),
        out_shape=jax.ShapeDtypeStruct((r, w), F32),
        compiler_params=_params(("parallel",)),
    )(chip_idx, s1, recv)


def _adam_math(w, g, m, v):
    m = ADAM_B1 * m + (1.0 - ADAM_B1) * g
    v = ADAM_B2 * v + (1.0 - ADAM_B2) * (g * g)
    m_hat = m / (1.0 - ADAM_B1 ** ADAM_STEP)
    v_hat = v / (1.0 - ADAM_B2 ** ADAM_STEP)
    delta = -ADAM_LR * (m_hat / (jnp.sqrt(v_hat) + ADAM_EPS) + ADAM_WD * w)
    return delta, m, v


def _adamw(w, mine, other, c_idx, m, v, *, name):
    r, cw = w.shape
    hr = r // 2
    tm = _row_tile(hr, 9 * cw * 4)

    def body(c_ref, w_ref, a_ref, b_ref, m_ref, v_ref, g_ref, d_ref, nm_ref, nv_ref):
        g = jnp.where(pl.program_id(0) == c_ref[0], a_ref[...], b_ref[...])
        g_ref[...] = g
        d_ref[...], nm_ref[...], nv_ref[...] = _adam_math(w_ref[...], g, m_ref[...], v_ref[...])

    per_half = hr // tm
    full = pl.BlockSpec((tm, cw), lambda h, i, c_ref: (h * per_half + i, 0))
    mine_spec = pl.BlockSpec((tm, cw), lambda h, i, c_ref: (jnp.where(h == c_ref[0], i, 0), 0))
    other_spec = pl.BlockSpec((tm, cw), lambda h, i, c_ref: (jnp.where(h == c_ref[0], 0, i), 0))
    return pl.pallas_call(
        body, name=name,
        grid_spec=pltpu.PrefetchScalarGridSpec(
            num_scalar_prefetch=1, grid=(2, per_half),
            in_specs=[full, mine_spec, other_spec, full, full], out_specs=[full] * 4),
        out_shape=[jax.ShapeDtypeStruct((r, cw), F32)] * 4,
        compiler_params=_params(("parallel", "parallel")),
    )(c_idx, w, mine, other, m, v)


def _adamw_ada(cact_t, dada, w, m, v):
    r, cw = w.shape
    nb = cact_t.shape[1]
    tm = _tile(r, 256)
    tn = _tile(cw, 1024)

    def body(a_ref, d_ref, w_ref, m_ref, v_ref, g_ref, dl_ref, nm_ref, nv_ref):
        a = a_ref[...]
        d = d_ref[...]
        g = a[:, 0:1] * d[0:1, :]
        for b in range(1, nb):
            g = g + a[:, b:b + 1] * d[b:b + 1, :]
        g_ref[...] = g
        dl_ref[...], nm_ref[...], nv_ref[...] = _adam_math(w_ref[...], g, m_ref[...], v_ref[...])

    blk = pl.BlockSpec((tm, tn), lambda i, j: (i, j))
    return pl.pallas_call(
        body, name="adamw_ada", grid=(r // tm, cw // tn),
        in_specs=[pl.BlockSpec((tm, nb), lambda i, j: (i, 0)), pl.BlockSpec((nb, tn), lambda i, j: (0, j)), blk, blk, blk],
        out_specs=[blk] * 4, out_shape=[jax.ShapeDtypeStruct((r, cw), F32)] * 4,
        compiler_params=_params(("parallel", "parallel")),
    )(cact_t, dada, w, m, v)


def _adamw_vec(parts, w, m, v):
    n = w.shape[1]

    def body(p_ref, w_ref, m_ref, v_ref, g_ref, d_ref, nm_ref, nv_ref):
        p = p_ref[...]
        g = p[0:1, :]
        for b in range(1, N_DEV):
            g = g + p[b:b + 1, :]
        g_ref[...] = g
        d_ref[...], nm_ref[...], nv_ref[...] = _adam_math(w_ref[...], g, m_ref[...], v_ref[...])

    return pl.pallas_call(
        body, name="adamw_vec", out_shape=[jax.ShapeDtypeStruct((1, n), F32)] * 4,
        compiler_params=pltpu.CompilerParams(vmem_limit_bytes=VMEM_LIMIT),
    )(parts, w, m, v)


def _w_in_segments(kpe0, d_in, cs):
    segs = []
    for k in range(4):
        lo, hi = k * cs, (k + 1) * cs
        for a, b, shift in ((0, kpe0, 0), (kpe0, kpe0 + ROPE, d_in - ROPE - kpe0), (kpe0 + ROPE, d_in, -ROPE)):
            a, b = max(lo, a), min(hi, b)
            if a < b:
                segs.append((k, a - lo, a + shift, b - a))
    return segs


def _w_in_layout(g8, kpe0):
    _, hr, cs = g8.shape
    rows, d_in = 2 * hr, 4 * cs
    segs = _w_in_segments(kpe0, d_in, cs)
    tm = _tile(rows, 256)

    def body(g_ref, o_ref):
        for k, src, dst, w in segs:
            o_ref[:, dst:dst + w] = g_ref[k, :, src:src + w]
        o_ref[:, d_in:] = jnp.zeros((tm, ROPE), o_ref.dtype)

    return pl.pallas_call(
        body, name="w_in_layout", grid=(rows // tm,),
        in_specs=[pl.BlockSpec((4, tm, cs), lambda i: (0, i, 0))], out_specs=_rows(tm, d_in + ROPE),
        out_shape=jax.ShapeDtypeStruct((rows, d_in + ROPE), g8.dtype), compiler_params=_params(("parallel",)),
    )(g8.reshape(4, rows, cs))


def _w_in_grad_pieces(g, kpe0):
    rows, d_in_p = g.shape
    d_in = d_in_p - ROPE
    cs = d_in // 4
    segs = _w_in_segments(kpe0, d_in, cs)
    hr = rows // 2
    tm = _tile(hr, 256)
    per_half = hr // tm

    def body(g_ref, o_ref):
        for k, src, dst, w in segs:
            o_ref[k, :, src:src + w] = g_ref[:, dst:dst + w]

    return pl.pallas_call(
        body, name="w_in_grad_pieces", grid=(rows // tm,),
        in_specs=[_rows(tm, d_in_p)],
        out_specs=pl.BlockSpec((None, 4, tm, cs), lambda i: (i // per_half, 0, i % per_half, 0)),
        out_shape=jax.ShapeDtypeStruct((2, 4, hr, cs), g.dtype), compiler_params=_params(("parallel",)),
    )(g)


def _interleave_layout(g8, ib):
    _, hr, cs = g8.shape
    rows, per_chip, per_half = 2 * hr, cs // ib, 2 * cs // ib
    tm = _tile(rows, 2048)

    def src(jj):
        return jj // 2 + per_half * (jj % 2)

    def body(g_ref, o_ref):
        o_ref[...] = g_ref[...]

    return pl.pallas_call(
        body, name="interleave_layout", grid=(rows // tm, 4 * per_chip),
        in_specs=[pl.BlockSpec((None, tm, ib), lambda i, jj: (src(jj) // per_chip, i, src(jj) % per_chip))],
        out_specs=pl.BlockSpec((tm, ib), lambda i, jj: (i, jj)),
        out_shape=jax.ShapeDtypeStruct((rows, 4 * cs), g8.dtype), compiler_params=_params(("parallel", "parallel")),
    )(g8.reshape(4, rows, cs))


def _cols_from_chips(g8, rows):
    cs = g8.shape[-1]
    return g8.reshape(4, rows, cs).transpose(1, 0, 2).reshape(rows, 4 * cs)


def _cols_to_pieces(g):
    rows, c4 = g.shape
    return g.reshape(2, rows // 2, 4, c4 // 4).transpose(0, 2, 1, 3)


def _rows_to_pieces(g):
    r4, cols = g.shape
    return g.reshape(4, 2, r4 // 8, cols).transpose(1, 0, 2, 3)


def _pad_cols(a, w):
    return jnp.pad(a, ((0, 0), (0, w - a.shape[1])))


def kernel(x, c, positions, w_ada, b_ada, g_norm1, g_norm2, w_in, g_q_latent, g_kv_latent, w_uq, w_ukv, g_q_head, g_k_head, w_proj_mla, w_proj_sb, w_out, w_ffn_in, w_ffn_out, loss_target, m_w_ada, m_b_ada, m_g_norm1, m_g_norm2, m_w_in, m_g_q_latent, m_g_kv_latent, m_w_uq, m_w_ukv, m_g_q_head, m_g_k_head, m_w_proj_mla, m_w_proj_sb, m_w_out, m_w_ffn_in, m_w_ffn_out, v_w_ada, v_b_ada, v_g_norm1, v_g_norm2, v_w_in, v_g_q_latent, v_g_kv_latent, v_w_uq, v_w_ukv, v_g_q_head, v_g_k_head, v_w_proj_mla, v_w_proj_sb, v_w_out, v_w_ffn_in, v_w_ffn_out):
    xi, yi, ci = _place()
    chip = 2 * xi + yi
    dev = 2 * chip + ci
    c_idx = jnp.reshape(ci, (1,)).astype(jnp.int32)
    chip_idx = jnp.reshape(chip, (1,)).astype(jnp.int32)

    x = x[0]
    tgt = loss_target[0]
    S, D = x.shape
    ql = g_q_latent.shape[1]
    assert g_kv_latent.shape[1] == ql
    mlaw = w_proj_mla.shape[1]
    nh = mlaw // HEAD
    sbw = w_proj_sb.shape[1]
    assert sbw == mlaw
    dff = w_ffn_out.shape[1] * 4
    d_in = 2 * ql + ROPE + 3 * sbw + 2 * D
    d_in_p = d_in + ROPE
    q_col = (2 * ql) // HEAD
    k_col = q_col + nh
    v_col = k_col + nh
    gla_col = (2 * ql + 3 * sbw) // D
    glb_col = gla_col + 1
    kpe_col = (d_in - ROPE) // LANE
    assert (2 * ql + 3 * sbw) % D == 0 and (d_in - ROPE) % LANE == 0

    mats = {"w_in": w_in[0], "w_uq": w_uq[0], "w_ukv": w_ukv[0], "w_proj_mla": w_proj_mla[0],
            "w_proj_sb": w_proj_sb[0], "w_out": w_out[0], "w_ffn_in": w_ffn_in[0], "w_ffn_out": w_ffn_out[0]}
    names = list(mats)
    row_sharded = {"w_out", "w_ffn_out"}

    c_all = _gather_blocks([jnp.broadcast_to(c, (8, D))], name="gather_cond", in_vmem=True)[0][:, 0, :]
    n_ada = w_ada.shape[2]
    b_shard = lax.dynamic_slice_in_dim(b_ada, chip * n_ada, n_ada, axis=1)
    ada_shard = _mm(c_all, w_ada[0], name="ada_proj", a_fn=jax.nn.silu, bias=b_shard)
    ada_all = _gather_blocks([ada_shard], name="gather_ada", in_vmem=True)[0]
    ada_rows = lax.dynamic_index_in_dim(ada_all, dev, axis=1, keepdims=False)
    ada = ada_rows[0::2].reshape(1, 4 * n_ada)
    SH1, SC1, GT1, SH2, SC2, GT2 = range(6)

    def after(dep, a):
        return a + (dep.reshape(-1)[0:1].reshape((1,) * a.ndim) * 0).astype(a.dtype)

    def fill_own(g8, own):
        return lax.dynamic_update_index_in_dim(g8, own, dev, 0)

    halves = []
    for nm in names:
        w = mats[nm]
        hr = w.shape[0] // 2
        halves.append(lax.dynamic_slice_in_dim(w, ci * hr, hr, axis=0).astype(BF16))
    half_of = dict(zip(names, halves))
    early = ["w_in", "w_uq", "w_ukv"]
    late = ["w_proj_mla", "w_proj_sb", "w_out", "w_ffn_in", "w_ffn_out"]
    early_halves = [half_of[nm] for nm in early]
    early_halves[0] = after(ada, early_halves[0])
    early_got = _gather_blocks(early_halves, name="gather_weights", in_vmem=False)
    gathered = {nm: fill_own(g8, own) for nm, g8, own in zip(early, early_got, early_halves)}
    late_halves = [half_of[nm] for nm in late]
    late_halves[0] = after(gathered[early[1]], late_halves[0])
    late_send, late_recv, late_srcs, late_lands, late_token = _split_start(
        late_halves, [jax.ShapeDtypeStruct((N_DEV,) + h.shape, h.dtype) for h in late_halves], _gather_plan, 4,
        name="gather_late_start")
    ada = ada + late_token[0:1, 0:1]

    def full_cols(nm):
        return _cols_from_chips(gathered[nm], mats[nm].shape[0])

    kpe0 = 2 * ql
    w_in_p = _w_in_layout(gathered["w_in"], kpe0)
    w_uq_p = jnp.pad(full_cols("w_uq").reshape(ql, nh, QK_DIM), ((0, 0), (0, 0), (0, HEAD_PAD - QK_DIM))
                     ).reshape(ql, nh * HEAD_PAD)
    w_ukv4 = full_cols("w_ukv").reshape(ql, nh, 2 * HEAD)
    w_ukv_p = jnp.concatenate([w_ukv4[:, :, :HEAD].reshape(ql, mlaw), w_ukv4[:, :, HEAD:].reshape(ql, mlaw)], axis=1)

    half = ROPE // 2
    freqs = ROPE_THETA ** (-jnp.arange(half, dtype=F32) / half)
    ang = positions[0].astype(F32)[:, None] * freqs
    cos, sin = jnp.cos(ang), jnp.sin(ang)
    one = jnp.ones((S, NOPE), F32)
    zero = jnp.zeros((S, NOPE), F32)
    zh = jnp.zeros((S, half), F32)
    tabs = (jnp.concatenate([one, cos, cos, one[:, :HEAD_PAD - QK_DIM]], axis=1),
            jnp.concatenate([zero, zh, sin, zero[:, :HEAD_PAD - QK_DIM]], axis=1),
            jnp.concatenate([zero, -sin, zh, zero[:, :HEAD_PAD - QK_DIM]], axis=1))
    g_qh_p = _pad_cols(g_q_head, HEAD_PAD)
    g_kh_p = _pad_cols(g_k_head, HEAD_PAD)

    h1 = _rmsmod(x, g_norm1, ada, SC1, SH1, name="rmsmod1")
    proj = _mm(h1, w_in_p, name="mm_proj", tn=1664, out_dtype=BF16)
    cqn, ckvn = _latent_norm(proj, g_q_latent, g_kv_latent, ql)
    q0 = _mm(cqn, w_uq_p, name="mm_q_up", out_dtype=BF16)
    kv0 = _mm(ckvn, w_ukv_p, name="mm_kv_up", out_dtype=BF16)
    q = _q_prep(q0, g_qh_p, tabs, nh)
    k = _k_prep(kv0, proj, kpe_col, g_kh_p, tabs, nh)
    y_a, lse = _mla_fwd(q, k, kv0, nh)
    y_b, sb_runs = _sb_fwd(proj, q_col, k_col, v_col, nh)
    late_srcs, late_lands = _split_wait(late_send, late_recv, late_srcs, late_lands, y_b, _gather_plan,
                                        name="gather_late_wait")
    late_got = _gather_forward(late_lands, name="gather_late_forward")
    gathered.update({nm: fill_own(g8, own) for nm, g8, own in zip(late, late_got, late_srcs)})
    w_pm = full_cols("w_proj_mla")
    w_ps = full_cols("w_proj_sb")
    w_o = gathered["w_out"].reshape(D, D)
    ib = 256 if (dff // 2) % 256 == 0 else LANE
    nb = dff // ib
    w_fi = _interleave_layout(gathered["w_ffn_in"], ib)
    w_fo = gathered["w_ffn_out"].reshape(dff, D)
    pa = _mm(y_a, w_pm, name="mm_proj_mla", out_dtype=BF16)
    pb = _mm(y_b, w_ps, name="mm_proj_sb", out_dtype=BF16)
    merged = _gate_merge(pa, pb, proj, gla_col, glb_col)
    o = _mm(merged, w_o, name="mm_out")
    x2, h2 = _resid_rmsmod(x, o, g_norm2, ada, GT1, SC2, SH2)
    ff, act = _mm(h2, w_fi, name="mm_ffn_in", tn=4 * ib,
                  fused=(_swiglu_tile(ib), [], [(2 * dff, 4 * ib, BF16), (dff, 2 * ib, BF16)]))
    f = _mm(act, w_fo, name="mm_ffn_out")
    dy, df, red_l, loss_p = _loss_head(x2, f, tgt, ada, GT2)

    dff_, = _mm(df, w_fo, name="mm_d_act", tb=True, tn=2 * ib,
                fused=(_swiglu_bwd_tile(ib), [(ff, 4 * ib)], [(2 * dff, 4 * ib, BF16)]))
    def pc(kind):
        if kind == "cols":
            return kind
        return kind if (D // 4) % LANE == 0 and (dff // 4) % LANE == 0 else None

    gw_fo = _mm(act, df, name="mm_gw_ffn_out", ta=True, out_dtype=BF16, pieces=pc("rows"))
    dh2 = _mm(dff_, w_fi, name="mm_d_h2", tb=True)
    gw_fi = _mm(h2, dff_, name="mm_gw_ffn_in", ta=True, out_dtype=BF16, pieces="cols", tn=ib,
                col_perm=lambda jj: jj // 2 + nb * (jj % 2))

    def to_pieces(nms, grads):
        return [g if g.ndim == 4 else (_rows_to_pieces if nm in row_sharded else _cols_to_pieces)(g)
                for nm, g in zip(nms, grads)]

    def pair_sums(nms, pcs, got):
        return [_pair_sum(p, r, c_idx, name="rs_pair_sum_" + nm) for p, r, nm in zip(pcs, got, nms)]

    def swap_start(pcs, tag):
        return _split_start(pcs, [jax.ShapeDtypeStruct(p.shape[1:], p.dtype) for p in pcs], _swap_plan, 1,
                            name="rs_swap_%s_start" % tag)

    def exchange_start(pair, tag):
        return _split_start(pair, [jax.ShapeDtypeStruct((3,) + p.shape[1:], p.dtype) for p in pair], _exchange_plan, 3,
                            name="rs_exchange_%s_start" % tag)

    ffn = ["w_ffn_in", "w_ffn_out"]
    sw = swap_start(to_pieces(ffn, [gw_fi, gw_fo]), "ffn")
    ada = ada + sw[4][0:1, 0:1]
    dx2, do, red_2 = _rmsmod2_bwd(dh2, x2, dy, o, g_norm2, ada, SC2, GT1)
    ffn_pcs, ffn_got = _split_wait(sw[0], sw[1], sw[2], sw[3], dx2, _swap_plan, name="rs_swap_ffn_wait")
    ffn_send, ffn_recv, ffn_pair, ffn_lands, ffn_token = exchange_start(pair_sums(ffn, ffn_pcs, ffn_got), "ffn")
    dmerged = _mm(do, w_o, name="mm_d_merged", tb=True, out_dtype=BF16,
                  bias=jnp.zeros((1, D), F32) + ffn_token[0:1, 0:1])
    gw_o = _mm(merged, do, name="mm_gw_out", ta=True, out_dtype=BF16, pieces=pc("rows"))
    dpa, dpb, dgla, dglb = _gate_bwd(dmerged, pa, pb, proj, gla_col, glb_col)
    gw_pm = _mm(y_a, dpa, name="mm_gw_proj_mla", ta=True, out_dtype=BF16, pieces=pc("cols"))
    gw_ps = _mm(y_b, dpb, name="mm_gw_proj_sb", ta=True, out_dtype=BF16, pieces=pc("cols"))
    mid = ["w_proj_mla", "w_proj_sb", "w_out"]
    mid_pcs = to_pieces(mid, [gw_pm, gw_ps, gw_o])
    mid_pair = pair_sums(mid, mid_pcs, _sibling_swap(mid_pcs, name="rs_sibling_swap_mid"))
    mid_send, mid_recv, mid_pair, mid_lands, mid_token = exchange_start(mid_pair, "mid")
    behind_mid = jnp.zeros((1, mlaw), F32) + mid_token[0:1, 0:1]
    dya = _mm(dpa, w_pm, name="mm_d_ya", tb=True, out_dtype=BF16, bias=behind_mid)
    dyb = _mm(dpb, w_ps, name="mm_d_yb", tb=True, out_dtype=BF16, bias=behind_mid)
    dq, dk, dv = _mla_bwd(q, k, kv0, y_a, dya, lse, nh)
    dq_sb, dk_sb, dv_sb = _sb_bwd(proj, q_col, k_col, v_col, dyb, sb_runs, nh)
    dq0, red_qh = _q_prep_bwd(dq, q0, g_qh_p, tabs, nh)
    dkv0, dkpe, red_kh = _k_prep_bwd(dk, dv, kv0, proj, kpe_col, g_kh_p, tabs, nh)
    dcqn = _mm(dq0, w_uq_p, name="mm_d_cqn", tb=True, out_dtype=BF16)
    gw_uq_p = _mm(cqn, dq0, name="mm_gw_uq", ta=True, out_dtype=BF16)
    dckvn = _mm(dkv0, w_ukv_p, name="mm_d_ckvn", tb=True, out_dtype=BF16)
    gw_ukv_p = _mm(ckvn, dkv0, name="mm_gw_ukv", ta=True, out_dtype=BF16)
    dcq, dckv, red_lat = _latent_norm_bwd(dcqn, dckvn, proj, g_q_latent, g_kv_latent, ql)
    dproj = jnp.concatenate([dcq, dckv, dq_sb.astype(BF16), dk_sb.astype(BF16), dv_sb.astype(BF16),
                             dgla, dglb, dkpe], axis=1)
    gw_in_p = _mm(h1, dproj, name="mm_gw_in", ta=True, out_dtype=BF16, tn=1664)

    gw_in = _w_in_grad_pieces(gw_in_p, kpe0)
    gw_uq = gw_uq_p.reshape(ql, nh, HEAD_PAD)[:, :, :QK_DIM].reshape(ql, nh * QK_DIM)
    gw_ukv = jnp.concatenate([gw_ukv_p[:, :mlaw].reshape(ql, nh, HEAD), gw_ukv_p[:, mlaw:].reshape(ql, nh, HEAD)],
                             axis=2).reshape(ql, 2 * mlaw)
    last = ["w_in", "w_uq", "w_ukv"]
    assert last + mid + ffn == names

    last_pcs = to_pieces(last, [gw_in, gw_uq, gw_ukv])
    last_pair = pair_sums(last, last_pcs, _sibling_swap(last_pcs, name="rs_sibling_swap_last"))
    last_send, last_recv, last_pair, last_lands, last_token = exchange_start(last_pair, "last")
    ada = ada + last_token[0:1, 0:1]
    dh1 = _mm(dproj, w_in_p, name="mm_d_h1", tb=True, bias=jnp.zeros((1, D), F32) + last_token[0:1, 0:1])
    grad_x, red_1 = _rmsmod1_bwd(dh1, x, dx2, g_norm1, ada, SC1)
    last_pair, last_chips = _split_wait(last_send, last_recv, last_pair, last_lands, grad_x, _exchange_plan,
                                        name="rs_exchange_last_wait")
    mid_pair, mid_chips = _split_wait(mid_send, mid_recv, mid_pair, mid_lands, grad_x, _exchange_plan,
                                      name="rs_exchange_mid_wait")
    ffn_pair, ffn_chips = _split_wait(ffn_send, ffn_recv, ffn_pair, ffn_lands, grad_x, _exchange_plan,
                                      name="rs_exchange_ffn_wait")
    reduced = [_chip_sum(s, r, chip_idx, name="rs_chip_sum_" + nm)
               for s, r, nm in zip(last_pair + mid_pair + ffn_pair, last_chips + mid_chips + ffn_chips, names)]
    from_sibling2 = _sibling_swap(reduced, name="rs_sibling_send", whole=True)

    vec_names = ["b_ada", "g_norm1", "g_norm2", "g_q_latent", "g_kv_latent", "g_q_head", "g_k_head"]
    vec_w = dict(b_ada=b_ada, g_norm1=g_norm1, g_norm2=g_norm2, g_q_latent=g_q_latent, g_kv_latent=g_kv_latent,
                 g_q_head=g_q_head, g_k_head=g_k_head)
    vec_m = dict(b_ada=m_b_ada, g_norm1=m_g_norm1, g_norm2=m_g_norm2, g_q_latent=m_g_q_latent,
                 g_kv_latent=m_g_kv_latent, g_q_head=m_g_q_head, g_k_head=m_g_k_head)
    vec_v = dict(b_ada=v_b_ada, g_norm1=v_g_norm1, g_norm2=v_g_norm2, g_q_latent=v_g_q_latent,
                 g_kv_latent=v_g_kv_latent, g_q_head=v_g_q_head, g_k_head=v_g_k_head)
    d_ada = jnp.concatenate([red_1[0:1], red_1[1:2], red_2[3:4], red_2[0:1], red_2[1:2], red_l[0:1]], axis=1)
    vec_parts = dict(b_ada=d_ada, g_norm1=red_1[2:3], g_norm2=red_2[2:3], g_q_latent=red_lat[0:1],
                     g_kv_latent=red_lat[1:2], g_q_head=red_qh[0:1], g_k_head=red_kh[0:1])
    widths = [-(-vec_w[nm].shape[1] // LANE) * LANE for nm in vec_names]
    offs = [sum(widths[:i]) for i in range(len(widths))]
    pack = lambda d: jnp.concatenate([_pad_cols(d[nm][:, :vec_w[nm].shape[1]], wd) for nm, wd in zip(vec_names, widths)], axis=1)
    nvec = sum(widths) + LANE
    no_loss = jnp.zeros((1, LANE), F32)
    parts = jnp.concatenate([pack(vec_parts), loss_p[0:1, :]], axis=1)
    parts_all = _gather_blocks([jnp.broadcast_to(parts, (8, nvec))], name="gather_vec_grads",
                               in_vmem=True)[0][:, 0, :]
    gvec, dvec, nmvec, nvvec = _adamw_vec(parts_all, *[jnp.concatenate([pack(d), no_loss], axis=1)
                                                       for d in (vec_w, vec_m, vec_v)])
    loss = gvec[0, nvec - LANE]
    unpack = lambda a: {nm: a[:, o_:o_ + vec_w[nm].shape[1]] for nm, o_ in zip(vec_names, offs)}
    gvec, dvec, nmvec, nvvec = unpack(gvec), unpack(dvec), unpack(nmvec), unpack(nvvec)

    dada_all = lax.dynamic_slice_in_dim(parts_all[:, :6 * D], chip * n_ada, n_ada, axis=1)
    cact_t = jax.nn.silu(c_all).T
    g_ada, d_ada_w, nm_ada, nv_ada = _adamw_ada(cact_t, dada_all, w_ada[0], m_w_ada[0], v_w_ada[0])

    ms = dict(w_in=m_w_in, w_uq=m_w_uq, w_ukv=m_w_ukv, w_proj_mla=m_w_proj_mla, w_proj_sb=m_w_proj_sb,
              w_out=m_w_out, w_ffn_in=m_w_ffn_in, w_ffn_out=m_w_ffn_out)
    vs = dict(w_in=v_w_in, w_uq=v_w_uq, w_ukv=v_w_ukv, w_proj_mla=v_w_proj_mla, w_proj_sb=v_w_proj_sb,
              w_out=v_w_out, w_ffn_in=v_w_ffn_in, w_ffn_out=v_w_ffn_out)
    G, DL, NM, NV = {}, {}, {}, {}
    for nm, mine, other in zip(names, reduced, from_sibling2):
        g_, d_, m_, v_ = _adamw(mats[nm], mine, other, c_idx, ms[nm][0], vs[nm][0], name="adamw_" + nm)
        G[nm], DL[nm], NM[nm], NV[nm] = g_[None], d_[None], m_[None], v_[None]
    G["w_ada"], DL["w_ada"], NM["w_ada"], NV["w_ada"] = g_ada[None], d_ada_w[None], nm_ada[None], nv_ada[None]
    for nm in vec_names:
        G[nm], DL[nm], NM[nm], NV[nm] = gvec[nm], dvec[nm], nmvec[nm], nvvec[nm]

    order = ["w_ada", "b_ada", "g_norm1", "g_norm2", "w_in", "g_q_latent", "g_kv_latent", "w_uq", "w_ukv",
             "g_q_head", "g_k_head", "w_proj_mla", "w_proj_sb", "w_out", "w_ffn_in", "w_ffn_out"]
    return (loss, grad_x[None], *[G[n] for n in order], *[DL[n] for n in order],
            *[NM[n] for n in order], *[NV[n] for n in order])
```
